```python
import math
import jax, jax.numpy as jnp
from jax import lax
import numpy as np

D_MODEL = 1024
BATCH = 8
SEQ = 4096
DEPTH = 1

MEM_LEN = 256
HEAD_DIM = 64
SSD_HEADS = 16
SSD_WIDTH = SSD_HEADS * HEAD_DIM
SSD_GROUPS = 2
SSD_STATE = 128
SSD_CONV = 4
CHUNK = 128
XBC_WIDTH = SSD_WIDTH + 2 * SSD_GROUPS * SSD_STATE
CF_GROUPS = 16
CF_WIDTH = CF_GROUPS * HEAD_DIM
CF_CONV = 31
MIX_WIDTH = SSD_WIDTH + CF_WIDTH
Z_END = SSD_WIDTH
XBC_END = Z_END + XBC_WIDTH
DT_END = XBC_END + SSD_HEADS
IN_WIDTH = DT_END + 2 * CF_WIDTH
X_HEADS = 4
X_HEAD_DIM = D_MODEL // X_HEADS
D_FF = int(math.ceil(8 * D_MODEL / 3 / 256) * 256)
EPS = 1e-6

kernel_name = "hybrid_ssd_conformer_xattn_block"


def rmsnorm(x, g):
    xf = x.astype(jnp.float32)
    y = xf * lax.rsqrt(jnp.mean(xf * xf, axis=-1, keepdims=True) + EPS)
    return (y * g.astype(jnp.float32)).astype(x.dtype)


def layernorm(x, g, b):
    xf = x.astype(jnp.float32)
    mu = jnp.mean(xf, axis=-1, keepdims=True)
    var = jnp.mean(jnp.square(xf - mu), axis=-1, keepdims=True)
    y = (xf - mu) * lax.rsqrt(var + EPS)
    return (y * g.astype(jnp.float32) + b.astype(jnp.float32)).astype(x.dtype)


def causal_depthwise_conv(x, w, b):
    K, C = w.shape
    y = lax.conv_general_dilated(
        x, w[:, None, :].astype(x.dtype), window_strides=(1,), padding=[(K - 1, 0)],
        dimension_numbers=("NWC", "WIO", "NWC"), feature_group_count=C)
    return y + b.astype(x.dtype)


def ssd_chunked(xh, dt, A, Bg, Cg):
    Bsz, S, H, P = xh.shape
    G, N = Bg.shape[-2:]
    R = H // G
    nc = S // CHUNK
    f32 = jnp.float32
    x = xh.astype(f32).reshape(Bsz, nc, CHUNK, G, R, P)
    dtc = dt.astype(f32).reshape(Bsz, nc, CHUNK, G, R)
    Bc = Bg.astype(f32).reshape(Bsz, nc, CHUNK, G, N)
    Cc = Cg.astype(f32).reshape(Bsz, nc, CHUNK, G, N)
    dA = dtc * A.astype(f32).reshape(G, R)
    Acs = jnp.cumsum(dA, axis=2)
    seg = Acs[:, :, :, None] - Acs[:, :, None]
    causal = jnp.tril(jnp.ones((CHUNK, CHUNK), dtype=bool))[:, :, None, None]
    decay = jnp.exp(jnp.where(causal, seg, -jnp.inf))
    CB = jnp.einsum("bclgn,bcsgn->bclsg", Cc, Bc)
    scores = CB[..., None] * decay * dtc[:, :, None]
    y_diag = jnp.einsum("bclsgr,bcsgrp->bclgrp", scores, x)
    decay_to_end = jnp.exp(Acs[:, :, -1:] - Acs)
    states = jnp.einsum("bclgn,bclgr,bclgrp->bcgrpn", Bc, decay_to_end * dtc, x)
    chunk_decay = jnp.exp(Acs[:, :, -1])

    def step(carry, inp):
        st, dec = inp
        new = carry * dec[..., None, None] + st
        return new, carry

    init = jnp.zeros((Bsz, G, R, P, N), f32)
    _, prev = lax.scan(step, init, (jnp.moveaxis(states, 1, 0), jnp.moveaxis(chunk_decay, 1, 0)))
    prev = jnp.moveaxis(prev, 0, 1)
    y_off = jnp.einsum("bclgn,bcgrpn,bclgr->bclgrp", Cc, prev, jnp.exp(Acs))
    return (y_diag + y_off).reshape(Bsz, S, H, P)


def _fwd_setup_inputs(seed: int = 0) -> dict:
    key = jax.random.key(seed)
    ks = jax.random.split(key, 26)
    f32 = jnp.float32

    def nrm(k, shape, scale):
        return jax.random.normal(k, shape, f32) * scale

    def gain(k, shape):
        return 1.0 + 0.05 * jax.random.normal(k, shape, f32)

    dt0 = jnp.exp(jax.random.uniform(ks[5], (DEPTH, SSD_HEADS), f32, math.log(1e-3), math.log(1e-1)))
    return {
        "x": nrm(ks[0], (BATCH, SEQ, D_MODEL), 1.0),
        "mem": nrm(ks[1], (BATCH, MEM_LEN, D_MODEL), 1.0),
        "norm_mix_g": gain(ks[2], (DEPTH, D_MODEL)),
        "w_in": nrm(ks[3], (DEPTH, D_MODEL, IN_WIDTH), D_MODEL ** -0.5),
        "ssd_conv_w": nrm(ks[4], (DEPTH, SSD_CONV, XBC_WIDTH), SSD_CONV ** -0.5),
        "ssd_conv_b": nrm(ks[6], (DEPTH, XBC_WIDTH), 0.02),
        "ssd_dt_bias": dt0 + jnp.log(-jnp.expm1(-dt0)),
        "ssd_A_log": jnp.log(jax.random.uniform(ks[7], (DEPTH, SSD_HEADS), f32, 1.0, 16.0)),
        "ssd_D": gain(ks[8], (DEPTH, SSD_HEADS)),
        "ssd_norm_g": gain(ks[9], (DEPTH, SSD_WIDTH)),
        "cf_conv_w": nrm(ks[10], (DEPTH, CF_CONV, CF_WIDTH), CF_CONV ** -0.5),
        "cf_conv_b": nrm(ks[11], (DEPTH, CF_WIDTH), 0.02),
        "cf_ln_g": gain(ks[12], (DEPTH, CF_WIDTH)),
        "cf_ln_b": nrm(ks[13], (DEPTH, CF_WIDTH), 0.02),
        "w_out": nrm(ks[14], (DEPTH, MIX_WIDTH, D_MODEL), MIX_WIDTH ** -0.5),
        "norm_xattn_g": gain(ks[15], (DEPTH, D_MODEL)),
        "norm_mem_g": gain(ks[16], (DEPTH, D_MODEL)),
        "w_q": nrm(ks[17], (DEPTH, D_MODEL, D_MODEL), D_MODEL ** -0.5),
        "w_kv": nrm(ks[18], (DEPTH, D_MODEL, 2 * D_MODEL), D_MODEL ** -0.5),
        "w_o": nrm(ks[19], (DEPTH, D_MODEL, D_MODEL), D_MODEL ** -0.5),
        "norm_ffn_g": gain(ks[20], (DEPTH, D_MODEL)),
        "w_gate": nrm(ks[21], (DEPTH, D_MODEL, D_FF), D_MODEL ** -0.5),
        "w_up": nrm(ks[22], (DEPTH, D_MODEL, D_FF), D_MODEL ** -0.5),
        "w_down": nrm(ks[23], (DEPTH, D_FF, D_MODEL), D_FF ** -0.5),
        "norm_final_g": gain(ks[24], (D_MODEL,)),
    }


def _fwd_reference(x, mem, norm_mix_g, w_in, ssd_conv_w, ssd_conv_b, ssd_dt_bias, ssd_A_log, ssd_D,
              ssd_norm_g, cf_conv_w, cf_conv_b, cf_ln_g, cf_ln_b, w_out, norm_xattn_g, norm_mem_g,
              w_q, w_kv, w_o, norm_ffn_g, w_gate, w_up, w_down, norm_final_g):
    Bsz, S, _ = x.shape
    M = mem.shape[1]
    for i in range(DEPTH):
        h = rmsnorm(x, norm_mix_g[i])
        proj = h @ w_in[i]
        z, xbc, dt_raw, glu = jnp.split(proj, [Z_END, XBC_END, DT_END], axis=-1)

        xbc = jax.nn.silu(causal_depthwise_conv(xbc, ssd_conv_w[i], ssd_conv_b[i]))
        xs, Bm, Cm = jnp.split(xbc, [SSD_WIDTH, SSD_WIDTH + SSD_GROUPS * SSD_STATE], axis=-1)
        dt = jax.nn.softplus(dt_raw.astype(jnp.float32) + ssd_dt_bias[i].astype(jnp.float32))
        A = -jnp.exp(ssd_A_log[i].astype(jnp.float32))
        xh = xs.reshape(Bsz, S, SSD_HEADS, HEAD_DIM)
        y = ssd_chunked(xh, dt, A,
                        Bm.reshape(Bsz, S, SSD_GROUPS, SSD_STATE),
                        Cm.reshape(Bsz, S, SSD_GROUPS, SSD_STATE))
        y = y + ssd_D[i].astype(jnp.float32)[:, None] * xh.astype(jnp.float32)
        y = y.reshape(Bsz, S, SSD_WIDTH) * jax.nn.silu(z.astype(jnp.float32))
        y = rmsnorm(y.reshape(Bsz, S, SSD_GROUPS, SSD_WIDTH // SSD_GROUPS),
                    ssd_norm_g[i].reshape(SSD_GROUPS, SSD_WIDTH // SSD_GROUPS))
        y = y.reshape(Bsz, S, SSD_WIDTH).astype(x.dtype)

        a, g = jnp.split(glu, 2, axis=-1)
        u = a * jax.nn.sigmoid(g)
        u = causal_depthwise_conv(u, cf_conv_w[i], cf_conv_b[i])
        u = jax.nn.silu(layernorm(u, cf_ln_g[i], cf_ln_b[i]))

        x = x + jnp.concatenate([y, u], axis=-1) @ w_out[i]

        q = (rmsnorm(x, norm_xattn_g[i]) @ w_q[i]).reshape(Bsz, S, X_HEADS, X_HEAD_DIM)
        kv = rmsnorm(mem, norm_mem_g[i]) @ w_kv[i]
        k, v = jnp.split(kv, 2, axis=-1)
        k = k.reshape(Bsz, M, X_HEADS, X_HEAD_DIM)
        v = v.reshape(Bsz, M, X_HEADS, X_HEAD_DIM)
        s = jnp.einsum("bshd,bmhd->bhsm", q.astype(jnp.float32), k.astype(jnp.float32))
        p = jax.nn.softmax(s * (X_HEAD_DIM ** -0.5), axis=-1).astype(v.dtype)
        o = jnp.einsum("bhsm,bmhd->bshd", p, v).reshape(Bsz, S, D_MODEL)
        x = x + o @ w_o[i]

        hf = rmsnorm(x, norm_ffn_g[i])
        x = x + (jax.nn.silu(hf @ w_gate[i]) * (hf @ w_up[i])) @ w_down[i]
    return rmsnorm(x, norm_final_g)


import jax as _jax
import jax.numpy as _jnp

TWIN_FORMAT = 'train_step'
FWD_PARAMS = ['x', 'mem', 'norm_mix_g', 'w_in', 'ssd_conv_w', 'ssd_conv_b', 'ssd_dt_bias', 'ssd_A_log', 'ssd_D', 'ssd_norm_g', 'cf_conv_w', 'cf_conv_b', 'cf_ln_g', 'cf_ln_b', 'w_out', 'norm_xattn_g', 'norm_mem_g', 'w_q', 'w_kv', 'w_o', 'norm_ffn_g', 'w_gate', 'w_up', 'w_down', 'norm_final_g']
TWIN_WEIGHTS = ['norm_mix_g', 'w_in', 'ssd_conv_w', 'ssd_conv_b', 'ssd_dt_bias', 'ssd_A_log', 'ssd_D', 'ssd_norm_g', 'cf_conv_w', 'cf_conv_b', 'cf_ln_g', 'cf_ln_b', 'w_out', 'norm_xattn_g', 'norm_mem_g', 'w_q', 'w_kv', 'w_o', 'norm_ffn_g', 'w_gate', 'w_up', 'w_down', 'norm_final_g']
TWIN_DIFF_INPUT = 'x'
TWIN_INPUTS = ['x', 'mem', 'norm_mix_g', 'w_in', 'ssd_conv_w', 'ssd_conv_b', 'ssd_dt_bias', 'ssd_A_log', 'ssd_D', 'ssd_norm_g', 'cf_conv_w', 'cf_conv_b', 'cf_ln_g', 'cf_ln_b', 'w_out', 'norm_xattn_g', 'norm_mem_g', 'w_q', 'w_kv', 'w_o', 'norm_ffn_g', 'w_gate', 'w_up', 'w_down', 'norm_final_g', 'loss_target', 'm_norm_mix_g', 'm_w_in', 'm_ssd_conv_w', 'm_ssd_conv_b', 'm_ssd_dt_bias', 'm_ssd_A_log', 'm_ssd_D', 'm_ssd_norm_g', 'm_cf_conv_w', 'm_cf_conv_b', 'm_cf_ln_g', 'm_cf_ln_b', 'm_w_out', 'm_norm_xattn_g', 'm_norm_mem_g', 'm_w_q', 'm_w_kv', 'm_w_o', 'm_norm_ffn_g', 'm_w_gate', 'm_w_up', 'm_w_down', 'm_norm_final_g', 'v_norm_mix_g', 'v_w_in', 'v_ssd_conv_w', 'v_ssd_conv_b', 'v_ssd_dt_bias', 'v_ssd_A_log', 'v_ssd_D', 'v_ssd_norm_g', 'v_cf_conv_w', 'v_cf_conv_b', 'v_cf_ln_g', 'v_cf_ln_b', 'v_w_out', 'v_norm_xattn_g', 'v_norm_mem_g', 'v_w_q', 'v_w_kv', 'v_w_o', 'v_norm_ffn_g', 'v_w_gate', 'v_w_up', 'v_w_down', 'v_norm_final_g']
TWIN_OUTPUTS = ['loss', 'grad_x', 'grad_norm_mix_g', 'grad_w_in', 'grad_ssd_conv_w', 'grad_ssd_conv_b', 'grad_ssd_dt_bias', 'grad_ssd_A_log', 'grad_ssd_D', 'grad_ssd_norm_g', 'grad_cf_conv_w', 'grad_cf_conv_b', 'grad_cf_ln_g', 'grad_cf_ln_b', 'grad_w_out', 'grad_norm_xattn_g', 'grad_norm_mem_g', 'grad_w_q', 'grad_w_kv', 'grad_w_o', 'grad_norm_ffn_g', 'grad_w_gate', 'grad_w_up', 'grad_w_down', 'grad_norm_final_g', 'delta_norm_mix_g', 'delta_w_in', 'delta_ssd_conv_w', 'delta_ssd_conv_b', 'delta_ssd_dt_bias', 'delta_ssd_A_log', 'delta_ssd_D', 'delta_ssd_norm_g', 'delta_cf_conv_w', 'delta_cf_conv_b', 'delta_cf_ln_g', 'delta_cf_ln_b', 'delta_w_out', 'delta_norm_xattn_g', 'delta_norm_mem_g', 'delta_w_q', 'delta_w_kv', 'delta_w_o', 'delta_norm_ffn_g', 'delta_w_gate', 'delta_w_up', 'delta_w_down', 'delta_norm_final_g', 'new_m_norm_mix_g', 'new_m_w_in', 'new_m_ssd_conv_w', 'new_m_ssd_conv_b', 'new_m_ssd_dt_bias', 'new_m_ssd_A_log', 'new_m_ssd_D', 'new_m_ssd_norm_g', 'new_m_cf_conv_w', 'new_m_cf_conv_b', 'new_m_cf_ln_g', 'new_m_cf_ln_b', 'new_m_w_out', 'new_m_norm_xattn_g', 'new_m_norm_mem_g', 'new_m_w_q', 'new_m_w_kv', 'new_m_w_o', 'new_m_norm_ffn_g', 'new_m_w_gate', 'new_m_w_up', 'new_m_w_down', 'new_m_norm_final_g', 'new_v_norm_mix_g', 'new_v_w_in', 'new_v_ssd_conv_w', 'new_v_ssd_conv_b', 'new_v_ssd_dt_bias', 'new_v_ssd_A_log', 'new_v_ssd_D', 'new_v_ssd_norm_g', 'new_v_cf_conv_w', 'new_v_cf_conv_b', 'new_v_cf_ln_g', 'new_v_cf_ln_b', 'new_v_w_out', 'new_v_norm_xattn_g', 'new_v_norm_mem_g', 'new_v_w_q', 'new_v_w_kv', 'new_v_w_o', 'new_v_norm_ffn_g', 'new_v_w_gate', 'new_v_w_up', 'new_v_w_down', 'new_v_norm_final_g']
TWIN_LEAF_KINDS = {'loss': 'loss', 'grad_x': 'grad_x', 'grad_norm_mix_g': 'grad_w', 'grad_w_in': 'grad_w', 'grad_ssd_conv_w': 'grad_w', 'grad_ssd_conv_b': 'grad_w', 'grad_ssd_dt_bias': 'grad_w', 'grad_ssd_A_log': 'grad_w', 'grad_ssd_D': 'grad_w', 'grad_ssd_norm_g': 'grad_w', 'grad_cf_conv_w': 'grad_w', 'grad_cf_conv_b': 'grad_w', 'grad_cf_ln_g': 'grad_w', 'grad_cf_ln_b': 'grad_w', 'grad_w_out': 'grad_w', 'grad_norm_xattn_g': 'grad_w', 'grad_norm_mem_g': 'grad_w', 'grad_w_q': 'grad_w', 'grad_w_kv': 'grad_w', 'grad_w_o': 'grad_w', 'grad_norm_ffn_g': 'grad_w', 'grad_w_gate': 'grad_w', 'grad_w_up': 'grad_w', 'grad_w_down': 'grad_w', 'grad_norm_final_g': 'grad_w', 'delta_norm_mix_g': 'delta_w', 'delta_w_in': 'delta_w', 'delta_ssd_conv_w': 'delta_w', 'delta_ssd_conv_b': 'delta_w', 'delta_ssd_dt_bias': 'delta_w', 'delta_ssd_A_log': 'delta_w', 'delta_ssd_D': 'delta_w', 'delta_ssd_norm_g': 'delta_w', 'delta_cf_conv_w': 'delta_w', 'delta_cf_conv_b': 'delta_w', 'delta_cf_ln_g': 'delta_w', 'delta_cf_ln_b': 'delta_w', 'delta_w_out': 'delta_w', 'delta_norm_xattn_g': 'delta_w', 'delta_norm_mem_g': 'delta_w', 'delta_w_q': 'delta_w', 'delta_w_kv': 'delta_w', 'delta_w_o': 'delta_w', 'delta_norm_ffn_g': 'delta_w', 'delta_w_gate': 'delta_w', 'delta_w_up': 'delta_w', 'delta_w_down': 'delta_w', 'delta_norm_final_g': 'delta_w', 'new_m_norm_mix_g': 'new_m', 'new_m_w_in': 'new_m', 'new_m_ssd_conv_w': 'new_m', 'new_m_ssd_conv_b': 'new_m', 'new_m_ssd_dt_bias': 'new_m', 'new_m_ssd_A_log': 'new_m', 'new_m_ssd_D': 'new_m', 'new_m_ssd_norm_g': 'new_m', 'new_m_cf_conv_w': 'new_m', 'new_m_cf_conv_b': 'new_m', 'new_m_cf_ln_g': 'new_m', 'new_m_cf_ln_b': 'new_m', 'new_m_w_out': 'new_m', 'new_m_norm_xattn_g': 'new_m', 'new_m_norm_mem_g': 'new_m', 'new_m_w_q': 'new_m', 'new_m_w_kv': 'new_m', 'new_m_w_o': 'new_m', 'new_m_norm_ffn_g': 'new_m', 'new_m_w_gate': 'new_m', 'new_m_w_up': 'new_m', 'new_m_w_down': 'new_m', 'new_m_norm_final_g': 'new_m', 'new_v_norm_mix_g': 'new_v', 'new_v_w_in': 'new_v', 'new_v_ssd_conv_w': 'new_v', 'new_v_ssd_conv_b': 'new_v', 'new_v_ssd_dt_bias': 'new_v', 'new_v_ssd_A_log': 'new_v', 'new_v_ssd_D': 'new_v', 'new_v_ssd_norm_g': 'new_v', 'new_v_cf_conv_w': 'new_v', 'new_v_cf_conv_b': 'new_v', 'new_v_cf_ln_g': 'new_v', 'new_v_cf_ln_b': 'new_v', 'new_v_w_out': 'new_v', 'new_v_norm_xattn_g': 'new_v', 'new_v_norm_mem_g': 'new_v', 'new_v_w_q': 'new_v', 'new_v_w_kv': 'new_v', 'new_v_w_o': 'new_v', 'new_v_norm_ffn_g': 'new_v', 'new_v_w_gate': 'new_v', 'new_v_w_up': 'new_v', 'new_v_w_down': 'new_v', 'new_v_norm_final_g': 'new_v'}


def _forward(args):
    return _fwd_reference(*[args[k] for k in FWD_PARAMS])


def _output_shape():
    def fwd():
        inp = _fwd_setup_inputs(0)
        return _fwd_reference(*[inp[k] for k in FWD_PARAMS])
    out = _jax.eval_shape(fwd)
    return out.shape, out.dtype

N_MICROBATCH = 1
ADAM_LR = 0.001
ADAM_B1 = 0.9
ADAM_B2 = 0.999
ADAM_EPS = 1e-08
ADAM_WD = 0.01
ADAM_STEP = 10
PER_EXAMPLE_BATCH_AXIS = {'x': 0, 'mem': 0, 'loss_target': 0}
SHARED_INPUTS = []
_WEIGHT_DTYPES = {'norm_mix_g': _jnp.float32, 'w_in': _jnp.float32, 'ssd_conv_w': _jnp.float32, 'ssd_conv_b': _jnp.float32, 'ssd_dt_bias': _jnp.float32, 'ssd_A_log': _jnp.float32, 'ssd_D': _jnp.float32, 'ssd_norm_g': _jnp.float32, 'cf_conv_w': _jnp.float32, 'cf_conv_b': _jnp.float32, 'cf_ln_g': _jnp.float32, 'cf_ln_b': _jnp.float32, 'w_out': _jnp.float32, 'norm_xattn_g': _jnp.float32, 'norm_mem_g': _jnp.float32, 'w_q': _jnp.float32, 'w_kv': _jnp.float32, 'w_o': _jnp.float32, 'norm_ffn_g': _jnp.float32, 'w_gate': _jnp.float32, 'w_up': _jnp.float32, 'w_down': _jnp.float32, 'norm_final_g': _jnp.float32}
MOMENT_SCALE = {'norm_mix_g': 1.779580e-01, 'w_in': 8.313208e-02, 'ssd_conv_w': 9.607603e-02, 'ssd_conv_b': 1.384912e-01, 'ssd_dt_bias': 3.286704e-01, 'ssd_A_log': 4.241261e-01, 'ssd_D': 6.103214e-01, 'ssd_norm_g': 1.154638e-01, 'cf_conv_w': 6.665390e-02, 'cf_conv_b': 2.056171e-01, 'cf_ln_g': 1.066698e-01, 'cf_ln_b': 1.166025e-01, 'w_out': 1.352088e-01, 'norm_xattn_g': 1.602013e-02, 'norm_mem_g': 2.410397e-02, 'w_q': 1.589443e-02, 'w_kv': 1.635196e-02, 'w_o': 1.695956e-02, 'norm_ffn_g': 1.093348e-01, 'w_gate': 4.677319e-02, 'w_up': 4.582935e-02, 'w_down': 7.641748e-02, 'norm_final_g': 3.208578e+01}


def _to_microbatches(a, axis):
    t = _jnp.moveaxis(a, axis, 0)
    t = t.reshape((N_MICROBATCH, t.shape[0] // N_MICROBATCH) + t.shape[1:])
    return _jnp.moveaxis(t, 1, axis + 1)


def setup_inputs(seed: int = 0) -> dict:
    inp = _fwd_setup_inputs(seed)
    key = _jax.random.fold_in(_jax.random.key(seed), 7919)
    shape, _ = _output_shape()
    out = dict(inp)
    out["loss_target"] = _jax.random.normal(_jax.random.fold_in(key, 0), shape, _jnp.float32)
    for i, name in enumerate(TWIN_WEIGHTS):
        w = inp[name].astype(_jnp.float32)
        if MOMENT_SCALE is None:
            s = _jnp.sqrt(_jnp.mean(_jnp.square(w)) + 1e-30)
        else:
            s = MOMENT_SCALE[name]
        km, kv = _jax.random.split(_jax.random.fold_in(key, i + 1))
        out[name] = w
        out["m_" + name] = s * _jax.random.normal(km, w.shape, _jnp.float32)
        out["v_" + name] = (s * s) * _jax.random.uniform(kv, w.shape, _jnp.float32, 0.5, 1.5)
    if N_MICROBATCH > 1:
        for name, axis in PER_EXAMPLE_BATCH_AXIS.items():
            out[name] = _to_microbatches(out[name], axis)
    return {'x': out['x'], 'mem': out['mem'], 'norm_mix_g': out['norm_mix_g'], 'w_in': out['w_in'], 'ssd_conv_w': out['ssd_conv_w'], 'ssd_conv_b': out['ssd_conv_b'], 'ssd_dt_bias': out['ssd_dt_bias'], 'ssd_A_log': out['ssd_A_log'], 'ssd_D': out['ssd_D'], 'ssd_norm_g': out['ssd_norm_g'], 'cf_conv_w': out['cf_conv_w'], 'cf_conv_b': out['cf_conv_b'], 'cf_ln_g': out['cf_ln_g'], 'cf_ln_b': out['cf_ln_b'], 'w_out': out['w_out'], 'norm_xattn_g': out['norm_xattn_g'], 'norm_mem_g': out['norm_mem_g'], 'w_q': out['w_q'], 'w_kv': out['w_kv'], 'w_o': out['w_o'], 'norm_ffn_g': out['norm_ffn_g'], 'w_gate': out['w_gate'], 'w_up': out['w_up'], 'w_down': out['w_down'], 'norm_final_g': out['norm_final_g'], 'loss_target': out['loss_target'], 'm_norm_mix_g': out['m_norm_mix_g'], 'm_w_in': out['m_w_in'], 'm_ssd_conv_w': out['m_ssd_conv_w'], 'm_ssd_conv_b': out['m_ssd_conv_b'], 'm_ssd_dt_bias': out['m_ssd_dt_bias'], 'm_ssd_A_log': out['m_ssd_A_log'], 'm_ssd_D': out['m_ssd_D'], 'm_ssd_norm_g': out['m_ssd_norm_g'], 'm_cf_conv_w': out['m_cf_conv_w'], 'm_cf_conv_b': out['m_cf_conv_b'], 'm_cf_ln_g': out['m_cf_ln_g'], 'm_cf_ln_b': out['m_cf_ln_b'], 'm_w_out': out['m_w_out'], 'm_norm_xattn_g': out['m_norm_xattn_g'], 'm_norm_mem_g': out['m_norm_mem_g'], 'm_w_q': out['m_w_q'], 'm_w_kv': out['m_w_kv'], 'm_w_o': out['m_w_o'], 'm_norm_ffn_g': out['m_norm_ffn_g'], 'm_w_gate': out['m_w_gate'], 'm_w_up': out['m_w_up'], 'm_w_down': out['m_w_down'], 'm_norm_final_g': out['m_norm_final_g'], 'v_norm_mix_g': out['v_norm_mix_g'], 'v_w_in': out['v_w_in'], 'v_ssd_conv_w': out['v_ssd_conv_w'], 'v_ssd_conv_b': out['v_ssd_conv_b'], 'v_ssd_dt_bias': out['v_ssd_dt_bias'], 'v_ssd_A_log': out['v_ssd_A_log'], 'v_ssd_D': out['v_ssd_D'], 'v_ssd_norm_g': out['v_ssd_norm_g'], 'v_cf_conv_w': out['v_cf_conv_w'], 'v_cf_conv_b': out['v_cf_conv_b'], 'v_cf_ln_g': out['v_cf_ln_g'], 'v_cf_ln_b': out['v_cf_ln_b'], 'v_w_out': out['v_w_out'], 'v_norm_xattn_g': out['v_norm_xattn_g'], 'v_norm_mem_g': out['v_norm_mem_g'], 'v_w_q': out['v_w_q'], 'v_w_kv': out['v_w_kv'], 'v_w_o': out['v_w_o'], 'v_norm_ffn_g': out['v_norm_ffn_g'], 'v_w_gate': out['v_w_gate'], 'v_w_up': out['v_w_up'], 'v_w_down': out['v_w_down'], 'v_norm_final_g': out['v_norm_final_g']}


def _loss(weights, diff, rest, loss_target):
    with _jax.named_scope("forward"):
        args = {**rest, TWIN_DIFF_INPUT: diff, **{k: w.astype(_WEIGHT_DTYPES[k]) for k, w in weights.items()}}
        y = _forward(args)
    with _jax.named_scope("loss_head"):
        err = _jnp.square(y.astype(_jnp.float32) - loss_target)
        return 0.5 * _jnp.sum(_jnp.mean(err, axis=-1)) if err.ndim else 0.5 * err


def _adamw(w, g, m, v):
    m = ADAM_B1 * m + (1.0 - ADAM_B1) * g
    v = ADAM_B2 * v + (1.0 - ADAM_B2) * _jnp.square(g)
    m_hat = m / (1.0 - ADAM_B1 ** ADAM_STEP)
    v_hat = v / (1.0 - ADAM_B2 ** ADAM_STEP)
    delta = -ADAM_LR * (m_hat / (_jnp.sqrt(v_hat) + ADAM_EPS) + ADAM_WD * w)
    return delta, m, v


def reference(x, mem, norm_mix_g, w_in, ssd_conv_w, ssd_conv_b, ssd_dt_bias, ssd_A_log, ssd_D, ssd_norm_g, cf_conv_w, cf_conv_b, cf_ln_g, cf_ln_b, w_out, norm_xattn_g, norm_mem_g, w_q, w_kv, w_o, norm_ffn_g, w_gate, w_up, w_down, norm_final_g, loss_target, m_norm_mix_g, m_w_in, m_ssd_conv_w, m_ssd_conv_b, m_ssd_dt_bias, m_ssd_A_log, m_ssd_D, m_ssd_norm_g, m_cf_conv_w, m_cf_conv_b, m_cf_ln_g, m_cf_ln_b, m_w_out, m_norm_xattn_g, m_norm_mem_g, m_w_q, m_w_kv, m_w_o, m_norm_ffn_g, m_w_gate, m_w_up, m_w_down, m_norm_final_g, v_norm_mix_g, v_w_in, v_ssd_conv_w, v_ssd_conv_b, v_ssd_dt_bias, v_ssd_A_log, v_ssd_D, v_ssd_norm_g, v_cf_conv_w, v_cf_conv_b, v_cf_ln_g, v_cf_ln_b, v_w_out, v_norm_xattn_g, v_norm_mem_g, v_w_q, v_w_kv, v_w_o, v_norm_ffn_g, v_w_gate, v_w_up, v_w_down, v_norm_final_g):
    given = dict(x=x, mem=mem, norm_mix_g=norm_mix_g, w_in=w_in, ssd_conv_w=ssd_conv_w, ssd_conv_b=ssd_conv_b, ssd_dt_bias=ssd_dt_bias, ssd_A_log=ssd_A_log, ssd_D=ssd_D, ssd_norm_g=ssd_norm_g, cf_conv_w=cf_conv_w, cf_conv_b=cf_conv_b, cf_ln_g=cf_ln_g, cf_ln_b=cf_ln_b, w_out=w_out, norm_xattn_g=norm_xattn_g, norm_mem_g=norm_mem_g, w_q=w_q, w_kv=w_kv, w_o=w_o, norm_ffn_g=norm_ffn_g, w_gate=w_gate, w_up=w_up, w_down=w_down, norm_final_g=norm_final_g, loss_target=loss_target, m_norm_mix_g=m_norm_mix_g, m_w_in=m_w_in, m_ssd_conv_w=m_ssd_conv_w, m_ssd_conv_b=m_ssd_conv_b, m_ssd_dt_bias=m_ssd_dt_bias, m_ssd_A_log=m_ssd_A_log, m_ssd_D=m_ssd_D, m_ssd_norm_g=m_ssd_norm_g, m_cf_conv_w=m_cf_conv_w, m_cf_conv_b=m_cf_conv_b, m_cf_ln_g=m_cf_ln_g, m_cf_ln_b=m_cf_ln_b, m_w_out=m_w_out, m_norm_xattn_g=m_norm_xattn_g, m_norm_mem_g=m_norm_mem_g, m_w_q=m_w_q, m_w_kv=m_w_kv, m_w_o=m_w_o, m_norm_ffn_g=m_norm_ffn_g, m_w_gate=m_w_gate, m_w_up=m_w_up, m_w_down=m_w_down, m_norm_final_g=m_norm_final_g, v_norm_mix_g=v_norm_mix_g, v_w_in=v_w_in, v_ssd_conv_w=v_ssd_conv_w, v_ssd_conv_b=v_ssd_conv_b, v_ssd_dt_bias=v_ssd_dt_bias, v_ssd_A_log=v_ssd_A_log, v_ssd_D=v_ssd_D, v_ssd_norm_g=v_ssd_norm_g, v_cf_conv_w=v_cf_conv_w, v_cf_conv_b=v_cf_conv_b, v_cf_ln_g=v_cf_ln_g, v_cf_ln_b=v_cf_ln_b, v_w_out=v_w_out, v_norm_xattn_g=v_norm_xattn_g, v_norm_mem_g=v_norm_mem_g, v_w_q=v_w_q, v_w_kv=v_w_kv, v_w_o=v_w_o, v_norm_ffn_g=v_norm_ffn_g, v_w_gate=v_w_gate, v_w_up=v_w_up, v_w_down=v_w_down, v_norm_final_g=v_norm_final_g)
    weights = {n: given[n] for n in TWIN_WEIGHTS}
    shared = {n: given[n] for n in SHARED_INPUTS}
    per_example = {n: given[n] for n in ['x', 'mem']}
    grad_fn = _jax.value_and_grad(_loss, argnums=(0, 1))

    def one_microbatch(ex, loss_target):
        ex = dict(ex)
        diff = ex.pop(TWIN_DIFF_INPUT)
        return grad_fn(weights, diff, {**shared, **ex}, loss_target)

    if N_MICROBATCH == 1:
        loss, (grad_w, grad_x) = one_microbatch(per_example, given["loss_target"])
    else:
        def body(carry, xs):
            loss_sum, grad_sum = carry
            l_k, (gw_k, gx_k) = one_microbatch(xs[0], xs[1])
            with _jax.named_scope("update"):
                return (loss_sum + l_k, _jax.tree.map(_jnp.add, grad_sum, gw_k)), gx_k

        init = (_jnp.zeros((), _jnp.float32), _jax.tree.map(_jnp.zeros_like, weights))
        (loss, grad_w), grad_x = _jax.lax.scan(body, init, (per_example, given["loss_target"]))
    with _jax.named_scope("update"):
        delta_w, new_m, new_v = {}, {}, {}
        for n in TWIN_WEIGHTS:
            delta_w[n], new_m[n], new_v[n] = _adamw(weights[n], grad_w[n], given["m_" + n], given["v_" + n])
    return (loss, grad_x, *[grad_w[n] for n in TWIN_WEIGHTS], *[delta_w[n] for n in TWIN_WEIGHTS],
            *[new_m[n] for n in TWIN_WEIGHTS], *[new_v[n] for n in TWIN_WEIGHTS])
```

```python
import functools
import math

import jax
import jax.numpy as jnp
from jax import lax
from jax.experimental import pallas as pl
from jax.experimental.pallas import tpu as pltpu

_F32 = jnp.float32
_MXU = jnp.bfloat16
_PREC = None
_VMEM_LIMIT = 56 * 1024 * 1024

D_MODEL = 1024
HEAD_DIM = 64
SSD_HEADS = 16
SSD_WIDTH = 1024
SSD_STATE = 128
SSD_CONV = 4
CHUNK = 128
XBC_WIDTH = 1536
CF_WIDTH = 1024
CF_CONV = 31
X_HEADS = 4
X_HEAD_DIM = 256
D_FF = 2816
EPS = 1e-6
N_DEV = 8
LANES = 128
SUBLANES = 8

ADAM_LR = 0.001
ADAM_B1 = 0.9
ADAM_B2 = 0.999
ADAM_EPS = 1e-08
ADAM_WD = 0.01
ADAM_STEP = 10

MESH = pl.DeviceIdType.MESH
WEIGHT_NAMES = ['norm_mix_g', 'w_in', 'ssd_conv_w', 'ssd_conv_b', 'ssd_dt_bias', 'ssd_A_log', 'ssd_D', 'ssd_norm_g',
                'cf_conv_w', 'cf_conv_b', 'cf_ln_g', 'cf_ln_b', 'w_out', 'norm_xattn_g', 'norm_mem_g', 'w_q', 'w_kv',
                'w_o', 'norm_ffn_g', 'w_gate', 'w_up', 'w_down', 'norm_final_g']
BIG = ['w_in', 'w_out', 'w_q', 'w_kv', 'w_o', 'w_gate', 'w_up', 'w_down']
COL_SHARDED = ('w_in', 'w_kv', 'w_gate', 'w_up')


def _params(sem=None):
    return pltpu.CompilerParams(dimension_semantics=sem, vmem_limit_bytes=_VMEM_LIMIT)


def _pick(n, cands):
    for c in cands:
        if n % c == 0:
            return c
    return n


def _mm(a, b, *, ta=False, tb=False, add=None, out_dtype=_F32, name):
    (kdim, m) = a.shape if ta else a.shape[::-1]
    (n, k2) = b.shape if tb else b.shape[::-1]
    assert kdim == k2, (a.shape, b.shape, ta, tb)
    tm = _pick(m, (512, 1408, 256, 128))
    tn = _pick(n, (512, 1408, 256, 128))
    tk = _pick(kdim, (1408, 1024, 512, 256, 128))
    nk = kdim // tk
    dn = (((0 if ta else 1,), (1 if tb else 0,)), ((), ()))

    def body(*refs):
        a_ref, b_ref = refs[0], refs[1]
        add_ref = refs[2] if add is not None else None
        o_ref, acc_ref = refs[-2], refs[-1]
        k = pl.program_id(2)

        @pl.when(k == 0)
        def _():
            acc_ref[...] = jnp.zeros_like(acc_ref)

        acc_ref[...] += lax.dot_general(a_ref[...].astype(_MXU), b_ref[...].astype(_MXU), dn,
                                        preferred_element_type=_F32, precision=_PREC)

        @pl.when(k == nk - 1)
        def _():
            r = acc_ref[...]
            if add_ref is not None:
                r = r + add_ref[...].astype(_F32)
            o_ref[...] = r.astype(o_ref.dtype)

    a_spec = pl.BlockSpec((tk, tm), lambda i, j, k: (k, i)) if ta else pl.BlockSpec((tm, tk), lambda i, j, k: (i, k))
    b_spec = pl.BlockSpec((tn, tk), lambda i, j, k: (j, k)) if tb else pl.BlockSpec((tk, tn), lambda i, j, k: (k, j))
    o_spec = pl.BlockSpec((tm, tn), lambda i, j, k: (i, j))
    ins, specs = [a, b], [a_spec, b_spec]
    if add is not None:
        ins.append(add)
        specs.append(o_spec)
    return pl.pallas_call(
        body, name=name, grid=(m // tm, n // tn, nk), in_specs=specs, out_specs=o_spec,
        out_shape=jax.ShapeDtypeStruct((m, n), out_dtype),
        scratch_shapes=[pltpu.VMEM((tm, tn), _F32)],
        compiler_params=_params(("parallel", "parallel", "arbitrary")),
    )(*ins)


def _row_spec(r, ts):
    if isinstance(r, tuple):
        arr, width, cblk = r
        return arr, pl.BlockSpec((ts, width), lambda i, cblk=cblk: (i, cblk))
    return r, pl.BlockSpec((ts, r.shape[1]), lambda i: (i, 0))


def _tup(v):
    return tuple(v) if isinstance(v, (tuple, list)) else (v,)


def _row_fwd(f, rows, params, outs, *, name, ts=256):
    arrs, specs = zip(*[_row_spec(r, ts) for r in rows])
    s = arrs[0].shape[0]
    ts = min(ts, s)
    n_r, n_p = len(rows), len(params)

    def body(*refs):
        rv = [r[...].astype(_F32) for r in refs[:n_r]]
        pv = [p[...] for p in refs[n_r:n_r + n_p]]
        res = _tup(f(*rv, *pv))
        for o_ref, v in zip(refs[n_r + n_p:], res):
            o_ref[...] = v.astype(o_ref.dtype)

    res = pl.pallas_call(
        body, name=name, grid=(s // ts,),
        in_specs=list(specs) + [pl.BlockSpec(p.shape, lambda i: (0, 0)) for p in params],
        out_specs=[pl.BlockSpec((ts, w), lambda i: (i, 0)) for w, _ in outs],
        out_shape=[jax.ShapeDtypeStruct((s, w), dt) for w, dt in outs],
        compiler_params=_params(("parallel",)),
    )(*arrs, *params)
    return res[0] if len(outs) == 1 else res


def _row_bwd(f, rows, params, cts, *, need=None, adds=None, row_dtypes=None, name, ts=256):
    arrs, specs = zip(*[_row_spec(r, ts) for r in rows])
    s = arrs[0].shape[0]
    ts = min(ts, s)
    n_r, n_p, n_c = len(rows), len(params), len(cts)
    need = [True] * n_r if need is None else need
    adds = {} if adds is None else adds
    add_keys = sorted(adds)
    row_dtypes = [_F32] * n_r if row_dtypes is None else row_dtypes
    needed = [j for j in range(n_r) if need[j]]
    widths = [specs[j].block_shape[1] for j in range(n_r)]

    def body(*refs):
        pos = 0
        r_refs = refs[pos:pos + n_r]; pos += n_r
        p_refs = refs[pos:pos + n_p]; pos += n_p
        c_refs = refs[pos:pos + n_c]; pos += n_c
        a_refs = refs[pos:pos + len(add_keys)]; pos += len(add_keys)
        dr_refs = refs[pos:pos + len(needed)]; pos += len(needed)
        dp_refs = refs[pos:pos + n_p]
        rv = [r[...].astype(_F32) for r in r_refs]
        pv = [p[...] for p in p_refs]
        _, vjp = jax.vjp(lambda *a: _tup(f(*a)), *rv, *pv)
        g = vjp(tuple(c[...].astype(_F32) for c in c_refs))
        for o_ref, j in zip(dr_refs, needed):
            v = g[j]
            if j in adds:
                v = v + a_refs[add_keys.index(j)][...].astype(_F32)
            o_ref[...] = v.astype(o_ref.dtype)
        if n_p:
            @pl.when(pl.program_id(0) == 0)
            def _():
                for dp in dp_refs:
                    dp[...] = jnp.zeros_like(dp)
            for dp, v in zip(dp_refs, g[n_r:]):
                dp[...] += v

    ct_specs = [pl.BlockSpec((ts, c.shape[1]), lambda i: (i, 0)) for c in cts]
    add_specs = [pl.BlockSpec((ts, adds[j].shape[1]), lambda i: (i, 0)) for j in add_keys]
    res = pl.pallas_call(
        body, name=name, grid=(s // ts,),
        in_specs=list(specs) + [pl.BlockSpec(p.shape, lambda i: (0, 0)) for p in params] + ct_specs + add_specs,
        out_specs=[pl.BlockSpec((ts, widths[j]), lambda i: (i, 0)) for j in needed]
        + [pl.BlockSpec(p.shape, lambda i: (0, 0)) for p in params],
        out_shape=[jax.ShapeDtypeStruct((s, widths[j]), row_dtypes[j]) for j in needed]
        + [jax.ShapeDtypeStruct(p.shape, _F32) for p in params],
        compiler_params=_params(("arbitrary",)),
    )(*arrs, *params, *cts, *[adds[j] for j in add_keys])
    return list(res[:len(needed)]), list(res[len(needed):])


_DN = {"nn": (((1,), (0,)), ((), ())), "nt": (((1,), (1,)), ((), ())), "tn": (((0,), (0,)), ((), ()))}


def _make_dot(passes):
    def raw(a, b, kind):
        dn = _DN[kind]
        if passes == 1 or _MXU == _F32:
            return lax.dot_general(a.astype(_MXU), b.astype(_MXU), dn, preferred_element_type=_F32, precision=_PREC)
        a_hi, b_hi = a.astype(_MXU), b.astype(_MXU)
        a_lo = (a - a_hi.astype(_F32)).astype(_MXU)
        b_lo = (b - b_hi.astype(_F32)).astype(_MXU)
        out = lax.dot_general(a_hi, b_hi, dn, preferred_element_type=_F32)
        out = out + lax.dot_general(a_lo, b_hi, dn, preferred_element_type=_F32)
        return out + lax.dot_general(a_hi, b_lo, dn, preferred_element_type=_F32)

    @functools.partial(jax.custom_vjp, nondiff_argnums=(2,))
    def dot(a, b, kind):
        return raw(a, b, kind)

    def fwd(a, b, kind):
        return raw(a, b, kind), (a, b)

    def bwd(kind, res, ct):
        a, b = res
        if kind == "nn":
            return raw(ct, b, "nt"), raw(a, ct, "tn")
        if kind == "nt":
            return raw(ct, b, "nn"), raw(ct, a, "tn")
        return raw(b, ct, "nt"), raw(a, ct, "nn")

    dot.defvjp(fwd, bwd)
    return dot


_dot1 = _make_dot(1)
_dot3 = _make_dot(3)


def _sig(v):
    return 1.0 / (1.0 + jnp.exp(-v))


def _silu(v):
    return v * _sig(v)


def _f_rms(x, g):
    return x * lax.rsqrt(jnp.mean(x * x, axis=-1, keepdims=True) + EPS) * g


def _f_gate(y, xs, z, dexp, g):
    v = (y + dexp * xs) * _silu(z)
    half = SSD_WIDTH // 2
    parts = []
    for grp in range(2):
        vg = v[:, grp * half:(grp + 1) * half]
        parts.append(vg * lax.rsqrt(jnp.mean(vg * vg, axis=-1, keepdims=True) + EPS) * g[:, grp * half:(grp + 1) * half])
    return jnp.concatenate(parts, axis=1)


def _f_ln(u, g, b):
    mu = jnp.mean(u, axis=-1, keepdims=True)
    var = jnp.mean(jnp.square(u - mu), axis=-1, keepdims=True)
    return _silu((u - mu) * lax.rsqrt(var + EPS) * g + b)


def _f_glu(a, g):
    return a * _sig(g)


def _f_swiglu(gate, up):
    return _silu(gate) * up


def _f_att(q, k, v):
    outs = []
    for h in range(X_HEADS):
        sl = slice(h * X_HEAD_DIM, (h + 1) * X_HEAD_DIM)
        s = _dot1(q[:, sl], k[:, sl], "nt") * (X_HEAD_DIM ** -0.5)
        s = s - lax.stop_gradient(jnp.max(s, axis=-1, keepdims=True))
        p = jnp.exp(s)
        p = p / jnp.sum(p, axis=-1, keepdims=True)
        outs.append(_dot1(p, v[:, sl], "nn"))
    return jnp.concatenate(outs, axis=1)


def _loss_bwd(x3, target, g, *, name, ts=256):
    s, d = x3.shape

    def f(x, t, gv):
        return 0.5 * jnp.sum(jnp.mean(jnp.square(_f_rms(x, gv) - t), axis=-1))

    def body(x_ref, t_ref, g_ref, dx_ref, dg_ref, l_ref):
        @pl.when(pl.program_id(0) == 0)
        def _():
            dg_ref[...] = jnp.zeros_like(dg_ref)
            l_ref[...] = jnp.zeros_like(l_ref)

        lv, (dx, dg) = jax.value_and_grad(f, argnums=(0, 2))(x_ref[...], t_ref[...], g_ref[...])
        dx_ref[...] = dx
        dg_ref[...] += dg
        l_ref[...] += lv

    row = pl.BlockSpec((ts, d), lambda i: (i, 0))
    return pl.pallas_call(
        body, name=name, grid=(s // ts,),
        in_specs=[row, row, pl.BlockSpec((1, d), lambda i: (0, 0))],
        out_specs=[row, pl.BlockSpec((1, d), lambda i: (0, 0)), pl.BlockSpec((SUBLANES, LANES), lambda i: (0, 0))],
        out_shape=[jax.ShapeDtypeStruct((s, d), _F32), jax.ShapeDtypeStruct((1, d), _F32),
                   jax.ShapeDtypeStruct((SUBLANES, LANES), _F32)],
        compiler_params=_params(("arbitrary",)),
    )(x3, target, g)


_CONV_PAD = 32
_CONV_ROWS = 128
_CONV_CB = 128


def _conv_taps(k_taps):
    groups = {}
    for k in range(k_taps):
        j = k_taps - 1 - k
        groups.setdefault(j % SUBLANES, []).append((k, j))
    return groups


def _conv_fwd(x, w, b, k_taps, *, name):
    s, c = x.shape
    kp = w.shape[0]
    pad, rows, cb = _CONV_PAD, _CONV_ROWS, _CONV_CB
    groups = _conv_taps(k_taps)

    def body(x_ref, w_ref, b_ref, o_ref, xp_ref):
        xp_ref[0:pad, :] = jnp.zeros((pad, cb), _F32)
        xp_ref[pad:pad + s, :] = x_ref[...]
        wv = w_ref[...]
        bias = jnp.broadcast_to(b_ref[...], (rows, cb))

        def chunk(r, carry):
            base = pl.multiple_of(r * rows, rows)
            win = xp_ref[pl.ds(base, rows + pad), :]
            acc = bias
            for rot, taps in groups.items():
                rolled = win if rot == 0 else pltpu.roll(win, rot, 0)
                for k, j in taps:
                    off = pad - (j - rot)
                    acc = acc + rolled[off:off + rows, :] * wv[k:k + 1, :]
            o_ref[pl.ds(base, rows), :] = acc
            return carry

        lax.fori_loop(0, s // rows, chunk, 0)

    col = pl.BlockSpec((s, cb), lambda i: (0, i))
    return pl.pallas_call(
        body, name=name, grid=(c // cb,),
        in_specs=[col, pl.BlockSpec((kp, cb), lambda i: (0, i)), pl.BlockSpec((1, cb), lambda i: (0, i))],
        out_specs=col, out_shape=jax.ShapeDtypeStruct((s, c), _F32),
        scratch_shapes=[pltpu.VMEM((s + pad, cb), _F32)],
        compiler_params=_params(("parallel",)),
    )(x, w, b)


def _conv_bwd(x, w, dy, k_taps, *, name):
    s, c = x.shape
    kp = w.shape[0]
    pad, rows, cb = _CONV_PAD, _CONV_ROWS, _CONV_CB
    groups = _conv_taps(k_taps)
    win_rows = rows + pad

    def fold(v):
        acc = v[0:SUBLANES, :]
        for i in range(1, rows // SUBLANES):
            acc = acc + v[i * SUBLANES:(i + 1) * SUBLANES, :]
        return acc

    def body(x_ref, w_ref, dy_ref, dx_ref, dw_ref, db_ref, xp_ref, dyp_ref, acc_ref, dbacc_ref):
        xp_ref[0:pad, :] = jnp.zeros((pad, cb), _F32)
        xp_ref[pad:pad + s, :] = x_ref[...]
        dyp_ref[0:s, :] = dy_ref[...]
        dyp_ref[s:s + pad, :] = jnp.zeros((pad, cb), _F32)
        acc_ref[...] = jnp.zeros_like(acc_ref)
        dbacc_ref[...] = jnp.zeros_like(dbacc_ref)
        wv = w_ref[...]

        def chunk(r, carry):
            base = pl.multiple_of(r * rows, rows)
            xwin = xp_ref[pl.ds(base, win_rows), :]
            dwin = dyp_ref[pl.ds(base, win_rows), :]
            dyc = dwin[0:rows, :]
            dxacc = jnp.zeros((rows, cb), _F32)
            for rot, taps in groups.items():
                xr = xwin if rot == 0 else pltpu.roll(xwin, rot, 0)
                dr = dwin if rot == 0 else pltpu.roll(dwin, win_rows - rot, 0)
                for k, j in taps:
                    a8 = j - rot
                    dxacc = dxacc + dr[a8:a8 + rows, :] * wv[k:k + 1, :]
                    prod = dyc * xr[pad - a8:pad - a8 + rows, :]
                    acc_ref[k * SUBLANES:(k + 1) * SUBLANES, :] += fold(prod)
            dbacc_ref[...] += fold(dyc)
            dx_ref[pl.ds(base, rows), :] = dxacc
            return carry

        lax.fori_loop(0, s // rows, chunk, 0)
        dw_ref[...] = jnp.zeros_like(dw_ref)
        for k in range(k_taps):
            dw_ref[k:k + 1, :] = jnp.sum(acc_ref[k * SUBLANES:(k + 1) * SUBLANES, :], axis=0, keepdims=True)
        db_ref[...] = jnp.sum(dbacc_ref[...], axis=0, keepdims=True)

    col = pl.BlockSpec((s, cb), lambda i: (0, i))
    wspec = pl.BlockSpec((kp, cb), lambda i: (0, i))
    bspec = pl.BlockSpec((1, cb), lambda i: (0, i))
    return pl.pallas_call(
        body, name=name, grid=(c // cb,),
        in_specs=[col, wspec, col], out_specs=[col, wspec, bspec],
        out_shape=[jax.ShapeDtypeStruct((s, c), _F32), jax.ShapeDtypeStruct((kp, c), _F32),
                   jax.ShapeDtypeStruct((1, c), _F32)],
        scratch_shapes=[pltpu.VMEM((s + pad, cb), _F32), pltpu.VMEM((s + pad, cb), _F32),
                        pltpu.VMEM((kp * SUBLANES, cb), _F32), pltpu.VMEM((SUBLANES, cb), _F32)],
        compiler_params=_params(("parallel",)),
    )(x, w, dy)


def _tri_sum(v, lower):
    l = v.shape[0]
    r, c = lax.broadcasted_iota(jnp.int32, (l, l), 0), lax.broadcasted_iota(jnp.int32, (l, l), 1)
    tri = ((r >= c) if lower else (r <= c)).astype(jnp.bfloat16)
    hi = v.astype(jnp.bfloat16)
    r1 = v - hi.astype(_F32)
    mid = r1.astype(jnp.bfloat16)
    lo = (r1 - mid.astype(_F32)).astype(jnp.bfloat16)
    out = jnp.zeros_like(v)
    for part in (hi, mid, lo):
        out = out + lax.dot_general(tri, part, _DN["nn"], preferred_element_type=_F32)
    return out


@jax.custom_vjp
def _cumsum_rows(v):
    return _tri_sum(v, True)


_cumsum_rows.defvjp(lambda v: (_tri_sum(v, True), None), lambda _, ct: (_tri_sum(ct, False),))


def _ssd_chunk(xbc, dtraw, prev, bias, alog):
    l = xbc.shape[0]
    xs = xbc[:, :SSD_WIDTH]
    bm = xbc[:, SSD_WIDTH:SSD_WIDTH + 2 * SSD_STATE]
    cm = xbc[:, SSD_WIDTH + 2 * SSD_STATE:]
    v = dtraw + bias
    dt = jnp.maximum(v, 0.0) + jnp.log1p(jnp.exp(-jnp.abs(v)))
    a_neg = -jnp.exp(alog)
    acs = _cumsum_rows(dt * a_neg)
    acs_t = acs.T
    dt_t = dt.T
    total = acs[l - 1:l, :]
    row = lax.broadcasted_iota(jnp.int32, (l, l), 0)
    colv = lax.broadcasted_iota(jnp.int32, (l, l), 1)
    causal = row >= colv
    lane_lo = lax.broadcasted_iota(jnp.int32, (l, LANES), 1) < HEAD_DIM
    row_lo = lax.broadcasted_iota(jnp.int32, (LANES, SSD_STATE), 0) < HEAD_DIM

    def pair_lanes(m, h0):
        return jnp.where(lane_lo, m[:, h0:h0 + 1], m[:, h0 + 1:h0 + 2])

    ys, news = [], []
    cb = {}
    for j in range(SSD_HEADS // 2):
        h0 = 2 * j
        grp = h0 // (SSD_HEADS // 2)
        bg = bm[:, grp * SSD_STATE:(grp + 1) * SSD_STATE]
        cg = cm[:, grp * SSD_STATE:(grp + 1) * SSD_STATE]
        if grp not in cb:
            cb[grp] = _dot3(cg, bg, "nt")
        xp = xs[:, j * LANES:(j + 1) * LANES]
        y = jnp.zeros((l, LANES), _F32)
        for hh, mask in ((h0, lane_lo), (h0 + 1, jnp.logical_not(lane_lo))):
            seg = acs[:, hh:hh + 1] - acs_t[hh:hh + 1, :]
            dec = jnp.exp(jnp.where(causal, seg, -jnp.inf))
            sc = cb[grp] * dec * dt_t[hh:hh + 1, :]
            y = y + _dot3(sc, jnp.where(mask, xp, 0.0), "nn")
        acs_p = pair_lanes(acs, h0)
        prev_p = prev[j * LANES:(j + 1) * LANES, :]
        y = y + _dot3(cg, prev_p, "nt") * jnp.exp(acs_p)
        wgt = jnp.exp(pair_lanes(jnp.broadcast_to(total, (l, LANES)), h0) - acs_p) * pair_lanes(dt, h0)
        st = _dot3(xp * wgt, bg, "tn")
        cdec = jnp.exp(jnp.where(row_lo, total[:, h0:h0 + 1], total[:, h0 + 1:h0 + 2]))
        news.append(prev_p * cdec + st)
        ys.append(y)
    return jnp.concatenate(ys, axis=1), jnp.concatenate(news, axis=0)


def _ssd_fwd(xbc, dtraw, bias, alog, *, name):
    s = xbc.shape[0]
    nc = s // CHUNK
    nstate = SSD_HEADS * HEAD_DIM

    def body(x_ref, dt_ref, b_ref, a_ref, y_ref, st_ref, state_ref):
        @pl.when(pl.program_id(0) == 0)
        def _():
            state_ref[...] = jnp.zeros_like(state_ref)

        prev = state_ref[...]
        st_ref[...] = prev
        y, new = _ssd_chunk(x_ref[...], dt_ref[...], prev, b_ref[...], a_ref[...])
        y_ref[...] = y
        state_ref[...] = new

    small = pl.BlockSpec((1, LANES), lambda i: (0, 0))
    return pl.pallas_call(
        body, name=name, grid=(nc,),
        in_specs=[pl.BlockSpec((CHUNK, XBC_WIDTH), lambda i: (i, 0)), pl.BlockSpec((CHUNK, LANES), lambda i: (i, 0)),
                  small, small],
        out_specs=[pl.BlockSpec((CHUNK, SSD_WIDTH), lambda i: (i, 0)),
                   pl.BlockSpec((None, nstate, SSD_STATE), lambda i: (i, 0, 0))],
        out_shape=[jax.ShapeDtypeStruct((s, SSD_WIDTH), _F32), jax.ShapeDtypeStruct((nc, nstate, SSD_STATE), _F32)],
        scratch_shapes=[pltpu.VMEM((nstate, SSD_STATE), _F32)],
        compiler_params=_params(("arbitrary",)),
    )(xbc, dtraw, bias, alog)


def _ssd_bwd(xbc, dtraw, states, bias, alog, dy, dxs_extra, *, name):
    s = xbc.shape[0]
    nc = s // CHUNK
    nstate = SSD_HEADS * HEAD_DIM

    def body(x_ref, dt_ref, st_ref, b_ref, a_ref, dy_ref, ex_ref, dx_ref, ddt_ref, db_ref, da_ref, dstate_ref):
        @pl.when(pl.program_id(0) == 0)
        def _():
            dstate_ref[...] = jnp.zeros_like(dstate_ref)
            db_ref[...] = jnp.zeros_like(db_ref)
            da_ref[...] = jnp.zeros_like(da_ref)

        _, vjp = jax.vjp(_ssd_chunk, x_ref[...], dt_ref[...], st_ref[...], b_ref[...], a_ref[...])
        dx, ddt, dprev, db, da = vjp((dy_ref[...], dstate_ref[...]))
        dx_ref[:, :SSD_WIDTH] = dx[:, :SSD_WIDTH] + ex_ref[...]
        dx_ref[:, SSD_WIDTH:] = dx[:, SSD_WIDTH:]
        ddt_ref[...] = ddt
        db_ref[...] += db
        da_ref[...] += da
        dstate_ref[...] = dprev

    rev = lambda i: (nc - 1 - i, 0)
    small = pl.BlockSpec((1, LANES), lambda i: (0, 0))
    return pl.pallas_call(
        body, name=name, grid=(nc,),
        in_specs=[pl.BlockSpec((CHUNK, XBC_WIDTH), rev), pl.BlockSpec((CHUNK, LANES), rev),
                  pl.BlockSpec((None, nstate, SSD_STATE), lambda i: (nc - 1 - i, 0, 0)), small, small,
                  pl.BlockSpec((CHUNK, SSD_WIDTH), rev), pl.BlockSpec((CHUNK, SSD_WIDTH), rev)],
        out_specs=[pl.BlockSpec((CHUNK, XBC_WIDTH), rev), pl.BlockSpec((CHUNK, LANES), rev), small, small],
        out_shape=[jax.ShapeDtypeStruct((s, XBC_WIDTH), _F32), jax.ShapeDtypeStruct((s, LANES), _F32),
                   jax.ShapeDtypeStruct((1, LANES), _F32), jax.ShapeDtypeStruct((1, LANES), _F32)],
        scratch_shapes=[pltpu.VMEM((nstate, SSD_STATE), _F32)],
        compiler_params=_params(("arbitrary",)),
    )(xbc, dtraw, states, bias, alog, dy, dxs_extra)


def _pad_cols(a, width):
    return jnp.pad(a, ((0, 0), (0, width - a.shape[1])))


def _pad_rows(a, rows):
    return jnp.pad(a, ((0, rows - a.shape[0]), (0, 0)))


def _local_step(x, mem, target, w):
    bf = _MXU
    w_in = w['w_in']
    z_end, xbc_end, dt_end = SSD_WIDTH, SSD_WIDTH + XBC_WIDTH, SSD_WIDTH + XBC_WIDTH + SSD_HEADS
    w_z, w_xbc = w_in[:, :z_end], w_in[:, z_end:xbc_end]
    w_dt = _pad_cols(w_in[:, xbc_end:dt_end], LANES)
    w_a, w_g = w_in[:, dt_end:dt_end + CF_WIDTH], w_in[:, dt_end + CF_WIDTH:]
    w_out_y, w_out_u = w['w_out'][:SSD_WIDTH], w['w_out'][SSD_WIDTH:]
    ssd_w = _pad_rows(w['ssd_conv_w'], SUBLANES)
    cf_w = _pad_rows(w['cf_conv_w'], 32)
    dt_bias = _pad_cols(w['ssd_dt_bias'], LANES)
    a_log = _pad_cols(w['ssd_A_log'], LANES)
    d_exp = jnp.repeat(w['ssd_D'], HEAD_DIM, axis=1)
    g_final = w['norm_final_g'].reshape(1, D_MODEL)
    d = D_MODEL

    h = _row_fwd(_f_rms, [x], [w['norm_mix_g']], [(d, bf)], name="f_norm_mix")
    z = _mm(h, w_z, name="f_in_z")
    xbc = _mm(h, w_xbc, name="f_in_xbc")
    dtr = _mm(h, w_dt, name="f_in_dt")
    ga = _mm(h, w_a, name="f_in_a")
    gg = _mm(h, w_g, name="f_in_g")
    xbc_c = _conv_fwd(xbc, ssd_w, w['ssd_conv_b'], SSD_CONV, name="f_ssd_conv")
    xbc_a = _row_fwd(_silu, [xbc_c], [], [(XBC_WIDTH, _F32)], name="f_ssd_silu")
    y_ssd, states = _ssd_fwd(xbc_a, dtr, dt_bias, a_log, name="f_ssd")
    xs_win = (xbc_a, SSD_WIDTH, 0)
    y_n = _row_fwd(_f_gate, [y_ssd, xs_win, z], [d_exp, w['ssd_norm_g']], [(d, bf)], name="f_ssd_gate")
    u_pre = _row_fwd(_f_glu, [ga, gg], [], [(CF_WIDTH, _F32)], name="f_glu")
    u_c = _conv_fwd(u_pre, cf_w, w['cf_conv_b'], CF_CONV, name="f_cf_conv")
    u = _row_fwd(_f_ln, [u_c], [w['cf_ln_g'], w['cf_ln_b']], [(d, bf)], name="f_cf_ln")
    x1 = _mm(y_n, w_out_y, add=x, name="f_out_y")
    x1 = _mm(u, w_out_u, add=x1, name="f_out_u")
    hq = _row_fwd(_f_rms, [x1], [w['norm_xattn_g']], [(d, bf)], name="f_norm_xattn")
    q = _mm(hq, w['w_q'], name="f_q")
    memn = _row_fwd(_f_rms, [mem], [w['norm_mem_g']], [(d, bf)], name="f_norm_mem")
    kv = _mm(memn, w['w_kv'], name="f_kv")
    k_mat, v_mat = kv[:, :d], kv[:, d:]
    o = _row_fwd(_f_att, [q], [k_mat, v_mat], [(d, bf)], name="f_att")
    x2 = _mm(o, w['w_o'], add=x1, name="f_o")
    hf = _row_fwd(_f_rms, [x2], [w['norm_ffn_g']], [(d, bf)], name="f_norm_ffn")
    gate = _mm(hf, w['w_gate'], name="f_gate")
    up = _mm(hf, w['w_up'], name="f_up")
    act = _row_fwd(_f_swiglu, [gate, up], [], [(D_FF, bf)], name="f_swiglu", ts=128)
    x3 = _mm(act, w['w_down'], add=x2, name="f_down")

    dx3, dg_final, loss = _loss_bwd(x3, target, g_final, name="b_loss")
    g = {'norm_final_g': dg_final.reshape(d)}

    dact = _mm(dx3, w['w_down'], tb=True, name="b_down_x")
    g['w_down'] = _mm(act, dx3, ta=True, name="b_down_w")
    (dgate, dup), _ = _row_bwd(_f_swiglu, [gate, up], [], [dact], row_dtypes=[bf, bf], name="b_swiglu", ts=128)
    g['w_gate'] = _mm(hf, dgate, ta=True, name="b_gate_w")
    g['w_up'] = _mm(hf, dup, ta=True, name="b_up_w")
    dhf = _mm(dgate, w['w_gate'], tb=True, name="b_gate_x")
    dhf = _mm(dup, w['w_up'], tb=True, add=dhf, name="b_up_x")
    (dx2,), (g['norm_ffn_g'],) = _row_bwd(_f_rms, [x2], [w['norm_ffn_g']], [dhf], adds={0: dx3}, name="b_norm_ffn")

    do = _mm(dx2, w['w_o'], tb=True, name="b_o_x")
    g['w_o'] = _mm(o, dx2, ta=True, name="b_o_w")
    (dq,), (dk, dv) = _row_bwd(_f_att, [q], [k_mat, v_mat], [do], row_dtypes=[bf], name="b_att")
    g['w_q'] = _mm(hq, dq, ta=True, name="b_q_w")
    dhq = _mm(dq, w['w_q'], tb=True, name="b_q_x")
    (dx1,), (g['norm_xattn_g'],) = _row_bwd(_f_rms, [x1], [w['norm_xattn_g']], [dhq], adds={0: dx2}, name="b_norm_xattn")
    dkv = jnp.concatenate([dk, dv], axis=1)
    g['w_kv'] = _mm(memn, dkv, ta=True, name="b_kv_w")
    dmemn = _mm(dkv, w['w_kv'], tb=True, name="b_kv_x")
    _, (g['norm_mem_g'],) = _row_bwd(_f_rms, [mem], [w['norm_mem_g']], [dmemn], need=[False], name="b_norm_mem")

    dyn = _mm(dx1, w_out_y, tb=True, name="b_out_y_x")
    du = _mm(dx1, w_out_u, tb=True, name="b_out_u_x")
    g['w_out'] = jnp.concatenate([_mm(y_n, dx1, ta=True, name="b_out_y_w"), _mm(u, dx1, ta=True, name="b_out_u_w")], axis=0)
    (du_c,), (g['cf_ln_g'], g['cf_ln_b']) = _row_bwd(_f_ln, [u_c], [w['cf_ln_g'], w['cf_ln_b']], [du], name="b_cf_ln")
    du_pre, dcf_w, g['cf_conv_b'] = _conv_bwd(u_pre, cf_w, du_c, CF_CONV, name="b_cf_conv")
    g['cf_conv_w'] = dcf_w[:CF_CONV]
    (dga, dgg), _ = _row_bwd(_f_glu, [ga, gg], [], [du_pre], row_dtypes=[bf, bf], name="b_glu")
    (dy_ssd, dxs, dz), (dd_exp, g['ssd_norm_g']) = _row_bwd(
        _f_gate, [y_ssd, xs_win, z], [d_exp, w['ssd_norm_g']], [dyn], row_dtypes=[_F32, _F32, bf], name="b_ssd_gate")
    g['ssd_D'] = jnp.sum(dd_exp.reshape(SSD_HEADS, HEAD_DIM), axis=1).reshape(1, SSD_HEADS)
    dxbc_a, ddtr, ddt_bias, da_log = _ssd_bwd(xbc_a, dtr, states, dt_bias, a_log, dy_ssd, dxs, name="b_ssd")
    g['ssd_dt_bias'] = ddt_bias[:, :SSD_HEADS]
    g['ssd_A_log'] = da_log[:, :SSD_HEADS]
    (dxbc_c,), _ = _row_bwd(_silu, [xbc_c], [], [dxbc_a], name="b_ssd_silu")
    dxbc, dssd_w, g['ssd_conv_b'] = _conv_bwd(xbc, ssd_w, dxbc_c, SSD_CONV, name="b_ssd_conv")
    g['ssd_conv_w'] = dssd_w[:SSD_CONV]

    dh = _mm(dz, w_z, tb=True, name="b_in_z_x")
    dh = _mm(dxbc, w_xbc, tb=True, add=dh, name="b_in_xbc_x")
    dh = _mm(ddtr, w_dt, tb=True, add=dh, name="b_in_dt_x")
    dh = _mm(dga, w_a, tb=True, add=dh, name="b_in_a_x")
    dh = _mm(dgg, w_g, tb=True, add=dh, name="b_in_g_x")
    g['w_in'] = jnp.concatenate([
        _mm(h, dz, ta=True, name="b_in_z_w"), _mm(h, dxbc, ta=True, name="b_in_xbc_w"),
        _mm(h, ddtr, ta=True, name="b_in_dt_w")[:, :SSD_HEADS],
        _mm(h, dga, ta=True, name="b_in_a_w"), _mm(h, dgg, ta=True, name="b_in_g_w")], axis=1)
    (dx,), (g['norm_mix_g'],) = _row_bwd(_f_rms, [x], [w['norm_mix_g']], [dh], adds={0: dx1}, name="b_norm_mix")
    return loss, dx, g


_ANY = pl.BlockSpec(memory_space=pl.ANY)


def _place():
    x, y, c = lax.axis_index("x"), lax.axis_index("y"), lax.axis_index("c")
    return x, y, c


def _all_gather(arrs, *, name):
    n = len(arrs)

    def body(*refs):
        ins, outs = refs[:n], refs[n:2 * n]
        send_sems, recv_sems, local_sems = refs[2 * n:]
        x, y, c = _place()
        me, sibling = (x, y, c), (x, y, 1 - c)
        chips = [(1 - x, y), (x, 1 - y), (1 - x, 1 - y)]

        def slot(a, dev):
            return outs[a].at[4 * dev[0] + 2 * dev[1] + dev[2]]

        def copy(a, k, block, to, src=None):
            return pltpu.make_async_remote_copy(
                src_ref=slot(a, block) if src is None else src, dst_ref=slot(a, block),
                send_sem=send_sems.at[a, k], recv_sem=recv_sems.at[a, k], device_id=to, device_id_type=MESH)

        mine = [pltpu.make_async_copy(ins[a], slot(a, me), local_sems.at[a]) for a in range(n)]
        for cp in mine:
            cp.start()
        first = []
        for a in range(n):
            first.append(copy(a, 0, me, sibling, src=ins[a]))
            first += [copy(a, 1 + j, me, (*chip, c), src=ins[a]) for j, chip in enumerate(chips)]
        for cp in first:
            cp.start()
        passed = []
        for a in range(n):
            for j, chip in enumerate(chips):
                copy(a, 1 + j, (*chip, c), me).wait_recv()
                fwd = copy(a, 4 + j, (*chip, c), sibling)
                fwd.start()
                passed.append(fwd)
        for a in range(n):
            copy(a, 0, sibling, me).wait_recv()
            for j, chip in enumerate(chips):
                copy(a, 4 + j, (*chip, 1 - c), me).wait_recv()
        for cp in first + passed:
            cp.wait_send()
        for cp in mine:
            cp.wait()

    return pl.pallas_call(
        body, name=name, in_specs=[_ANY] * n, out_specs=[_ANY] * n,
        out_shape=[jax.ShapeDtypeStruct((N_DEV,) + a.shape, a.dtype) for a in arrs],
        scratch_shapes=[pltpu.SemaphoreType.DMA((n, 7)), pltpu.SemaphoreType.DMA((n, 7)), pltpu.SemaphoreType.DMA((n,))],
    )(*arrs)


def _exchange(arrs, *, name):
    n = len(arrs)
    flips = [(dx, dy, dc) for dx in (0, 1) for dy in (0, 1) for dc in (0, 1)][1:]

    def body(*refs):
        ins, outs = refs[:n], refs[n:2 * n]
        send_sems, recv_sems, local_sems = refs[2 * n:]
        x, y, c = _place()
        me = 4 * x + 2 * y + c

        def peer(f):
            return (1 - x if f[0] else x, 1 - y if f[1] else y, 1 - c if f[2] else c)

        def copy(a, k):
            p = peer(flips[k])
            pid = 4 * p[0] + 2 * p[1] + p[2]
            return pltpu.make_async_remote_copy(
                src_ref=ins[a].at[pid], dst_ref=outs[a].at[me], send_sem=send_sems.at[a, k],
                recv_sem=recv_sems.at[a, k], device_id=p, device_id_type=MESH)

        def arrival(a, k):
            p = peer(flips[k])
            pid = 4 * p[0] + 2 * p[1] + p[2]
            return pltpu.make_async_remote_copy(
                src_ref=ins[a].at[pid], dst_ref=outs[a].at[pid], send_sem=send_sems.at[a, k],
                recv_sem=recv_sems.at[a, k], device_id=p, device_id_type=MESH)

        mine = [pltpu.make_async_copy(ins[a].at[me], outs[a].at[me], local_sems.at[a]) for a in range(n)]
        for cp in mine:
            cp.start()
        sends = [copy(a, k) for a in range(n) for k in range(7)]
        for cp in sends:
            cp.start()
        for a in range(n):
            for k in range(7):
                arrival(a, k).wait_recv()
        for cp in sends:
            cp.wait_send()
        for cp in mine:
            cp.wait()

    return pl.pallas_call(
        body, name=name, in_specs=[_ANY] * n, out_specs=[_ANY] * n,
        out_shape=[jax.ShapeDtypeStruct(a.shape, a.dtype) for a in arrs],
        scratch_shapes=[pltpu.SemaphoreType.DMA((n, 7)), pltpu.SemaphoreType.DMA((n, 7)), pltpu.SemaphoreType.DMA((n,))],
    )(*arrs)


def _adamw(parts, w, m, v, *, name):
    p, r, c = parts.shape
    tr = _pick(r, (256, 176, 128, 64, 32, 16, 8))

    def body(p_ref, w_ref, m_ref, v_ref, g_ref, d_ref, nm_ref, nv_ref):
        g = p_ref[0].astype(_F32)
        for i in range(1, p):
            g = g + p_ref[i].astype(_F32)
        wv = w_ref[...]
        mn = ADAM_B1 * m_ref[...] + (1.0 - ADAM_B1) * g
        vn = ADAM_B2 * v_ref[...] + (1.0 - ADAM_B2) * jnp.square(g)
        m_hat = mn / (1.0 - ADAM_B1 ** ADAM_STEP)
        v_hat = vn / (1.0 - ADAM_B2 ** ADAM_STEP)
        g_ref[...] = g
        d_ref[...] = -ADAM_LR * (m_hat / (jnp.sqrt(v_hat) + ADAM_EPS) + ADAM_WD * wv)
        nm_ref[...] = mn
        nv_ref[...] = vn

    blk = pl.BlockSpec((tr, c), lambda i: (i, 0))
    return pl.pallas_call(
        body, name=name, grid=(r // tr,),
        in_specs=[pl.BlockSpec((p, tr, c), lambda i: (0, i, 0)), blk, blk, blk], out_specs=[blk] * 4,
        out_shape=[jax.ShapeDtypeStruct((r, c), _F32)] * 4,
        compiler_params=_params(("parallel",)),
    )(parts, w, m, v)


def _sum_parts(parts, *, name):
    p, r, c = parts.shape

    def body(p_ref, o_ref):
        g = p_ref[0].astype(_F32)
        for i in range(1, p):
            g = g + p_ref[i].astype(_F32)
        o_ref[...] = g

    return pl.pallas_call(body, name=name, out_shape=jax.ShapeDtypeStruct((r, c), _F32))(parts)


def _pack(vals, rows):
    flat = jnp.concatenate([v.reshape(-1) for v in vals])
    return jnp.pad(flat, (0, rows * LANES - flat.shape[0])).reshape(rows, LANES)


def _unpack(packed, shapes):
    flat = packed.reshape(-1)
    out, pos = [], 0
    for shp in shapes:
        size = math.prod(shp)
        out.append(flat[pos:pos + size].reshape(shp))
        pos += size
    return out


def _pack_rows(shapes):
    total = sum(math.prod(s) for s in shapes)
    return -(-total // (LANES * SUBLANES)) * SUBLANES


def kernel(x, mem, norm_mix_g, w_in, ssd_conv_w, ssd_conv_b, ssd_dt_bias, ssd_A_log, ssd_D, ssd_norm_g, cf_conv_w, cf_conv_b, cf_ln_g, cf_ln_b, w_out, norm_xattn_g, norm_mem_g, w_q, w_kv, w_o, norm_ffn_g, w_gate, w_up, w_down, norm_final_g, loss_target, m_norm_mix_g, m_w_in, m_ssd_conv_w, m_ssd_conv_b, m_ssd_dt_bias, m_ssd_A_log, m_ssd_D, m_ssd_norm_g, m_cf_conv_w, m_cf_conv_b, m_cf_ln_g, m_cf_ln_b, m_w_out, m_norm_xattn_g, m_norm_mem_g, m_w_q, m_w_kv, m_w_o, m_norm_ffn_g, m_w_gate, m_w_up, m_w_down, m_norm_final_g, v_norm_mix_g, v_w_in, v_ssd_conv_w, v_ssd_conv_b, v_ssd_dt_bias, v_ssd_A_log, v_ssd_D, v_ssd_norm_g, v_cf_conv_w, v_cf_conv_b, v_cf_ln_g, v_cf_ln_b, v_w_out, v_norm_xattn_g, v_norm_mem_g, v_w_q, v_w_kv, v_w_o, v_norm_ffn_g, v_w_gate, v_w_up, v_w_down, v_norm_final_g):
    args = dict(locals())
    wts = {n: args[n] for n in WEIGHT_NAMES}
    mom = {n: args["m_" + n] for n in WEIGHT_NAMES}
    var = {n: args["v_" + n] for n in WEIGHT_NAMES}
    me = 4 * lax.axis_index("x") + 2 * lax.axis_index("y") + lax.axis_index("c")

    shards = [wts[n][0].astype(_MXU) for n in BIG] + [ssd_conv_w[0], cf_conv_w[0]]
    gathered = _all_gather(shards, name="gather_weights")
    full = {}
    for n, gth in zip(BIG, gathered):
        if n in COL_SHARDED:
            full[n] = jnp.transpose(gth, (1, 0, 2)).reshape(gth.shape[1], N_DEV * gth.shape[2])
        else:
            full[n] = gth.reshape(N_DEV * gth.shape[1], gth.shape[2])
    full['ssd_conv_w'] = jnp.transpose(gathered[-2], (1, 0, 2)).reshape(SSD_CONV, XBC_WIDTH)
    full['cf_conv_w'] = jnp.transpose(gathered[-1], (1, 0, 2)).reshape(CF_CONV, CF_WIDTH)
    for n in WEIGHT_NAMES:
        if n not in full:
            full[n] = wts[n]

    loss_blk, grad_x, g = _local_step(x[0], mem[0], loss_target[0], full)
    loss = lax.psum(loss_blk[0, 0], ("x", "y", "c"))

    blocks = []
    for n in BIG:
        gw = g[n]
        if n in COL_SHARDED:
            gw = jnp.transpose(gw.reshape(gw.shape[0], N_DEV, gw.shape[1] // N_DEV), (1, 0, 2))
        else:
            gw = gw.reshape(N_DEV, gw.shape[0] // N_DEV, gw.shape[1])
        blocks.append(gw.astype(jnp.bfloat16))
    received = _exchange(blocks, name="exchange_grads")
    out_g, out_d, out_m, out_v = {}, {}, {}, {}
    for n, parts in zip(BIG, received):
        res = _adamw(parts, wts[n][0], mom[n][0], var[n][0], name="adamw_" + n)
        out_g[n], out_d[n], out_m[n], out_v[n] = [r[None] for r in res]

    small = [n for n in WEIGHT_NAMES if n not in BIG]
    small_shapes = [full[n].shape for n in small]
    rows = _pack_rows(small_shapes)
    (small_parts,) = _all_gather([_pack([g[n] for n in small], rows)], name="gather_small_grads")
    small_sum = dict(zip(small, _unpack(_sum_parts(small_parts, name="sum_small_grads"), small_shapes)))
    small_sum['ssd_conv_w'] = lax.dynamic_slice_in_dim(small_sum['ssd_conv_w'], me * (XBC_WIDTH // N_DEV), XBC_WIDTH // N_DEV, axis=1)[None]
    small_sum['cf_conv_w'] = lax.dynamic_slice_in_dim(small_sum['cf_conv_w'], me * (CF_WIDTH // N_DEV), CF_WIDTH // N_DEV, axis=1)[None]
    shard_shapes = [wts[n].shape for n in small]
    rows2 = _pack_rows(shard_shapes)
    res = _adamw(_pack([small_sum[n] for n in small], rows2)[None], _pack([wts[n] for n in small], rows2),
                 _pack([mom[n] for n in small], rows2), _pack([var[n] for n in small], rows2), name="adamw_small")
    for dst, packed in zip((out_g, out_d, out_m, out_v), res):
        dst.update(zip(small, _unpack(packed, shard_shapes)))

    return (loss, grad_x[None], *[out_g[n] for n in WEIGHT_NAMES], *[out_d[n] for n in WEIGHT_NAMES],
            *[out_m[n] for n in WEIGHT_NAMES], *[out_v[n] for n in WEIGHT_NAMES])
```

```python
import functools
import math

import jax
import jax.numpy as jnp
from jax import lax
from jax.experimental import pallas as pl
from jax.experimental.pallas import tpu as pltpu

_F32 = jnp.float32
_MXU = jnp.bfloat16
_PREC = None
_VMEM_LIMIT = 56 * 1024 * 1024

D_MODEL = 1024
HEAD_DIM = 64
SSD_HEADS = 16
SSD_WIDTH = 1024
SSD_STATE = 128
SSD_CONV = 4
CHUNK = 128
XBC_WIDTH = 1536
CF_WIDTH = 1024
CF_CONV = 31
X_HEADS = 4
X_HEAD_DIM = 256
D_FF = 2816
EPS = 1e-6
N_DEV = 8
LANES = 128
SUBLANES = 8

ADAM_LR = 0.001
ADAM_B1 = 0.9
ADAM_B2 = 0.999
ADAM_EPS = 1e-08
ADAM_WD = 0.01
ADAM_STEP = 10

MESH = pl.DeviceIdType.MESH
WEIGHT_NAMES = ['norm_mix_g', 'w_in', 'ssd_conv_w', 'ssd_conv_b', 'ssd_dt_bias', 'ssd_A_log', 'ssd_D', 'ssd_norm_g',
                'cf_conv_w', 'cf_conv_b', 'cf_ln_g', 'cf_ln_b', 'w_out', 'norm_xattn_g', 'norm_mem_g', 'w_q', 'w_kv',
                'w_o', 'norm_ffn_g', 'w_gate', 'w_up', 'w_down', 'norm_final_g']
BIG = ['w_in', 'w_out', 'w_q', 'w_kv', 'w_o', 'w_gate', 'w_up', 'w_down']
COL_SHARDED = ('w_in', 'w_kv', 'w_gate', 'w_up')


def _params(sem=None):
    return pltpu.CompilerParams(dimension_semantics=sem, vmem_limit_bytes=_VMEM_LIMIT)


def _pick(n, cands):
    for c in cands:
        if n % c == 0:
            return c
    return n


def _mm(a, b, *, ta=False, tb=False, add=None, out_dtype=_F32, name):
    (kdim, m) = a.shape if ta else a.shape[::-1]
    (n, k2) = b.shape if tb else b.shape[::-1]
    assert kdim == k2, (a.shape, b.shape, ta, tb)
    tm = _pick(m, (512, 1408, 256, 128))
    tn = n if n <= 1536 else _pick(n, (1408, 1024, 512, 256, 128))
    tk = kdim if kdim <= 1536 else _pick(kdim, (1408, 1024, 512, 256, 128))
    nk = kdim // tk
    dn = (((0 if ta else 1,), (1 if tb else 0,)), ((), ()))

    def body(*refs):
        a_ref, b_ref = refs[0], refs[1]
        add_ref = refs[2] if add is not None else None
        o_ref = refs[3 if add is not None else 2]
        acc_ref = refs[-1]
        k = pl.program_id(2)
        prod = lax.dot_general(a_ref[...].astype(_MXU), b_ref[...].astype(_MXU), dn,
                               preferred_element_type=_F32, precision=_PREC)

        def finish(r):
            if add_ref is not None:
                r = r + add_ref[...].astype(_F32)
            o_ref[...] = r.astype(o_ref.dtype)

        if nk == 1:
            finish(prod)
            return

        @pl.when(k == 0)
        def _():
            acc_ref[...] = prod

        @pl.when(jnp.logical_and(k > 0, k < nk - 1))
        def _():
            acc_ref[...] += prod

        @pl.when(k == nk - 1)
        def _():
            finish(acc_ref[...] + prod)

    a_spec = pl.BlockSpec((tk, tm), lambda i, j, k: (k, i)) if ta else pl.BlockSpec((tm, tk), lambda i, j, k: (i, k))
    b_spec = pl.BlockSpec((tn, tk), lambda i, j, k: (j, k)) if tb else pl.BlockSpec((tk, tn), lambda i, j, k: (k, j))
    o_spec = pl.BlockSpec((tm, tn), lambda i, j, k: (i, j))
    ins, specs = [a, b], [a_spec, b_spec]
    if add is not None:
        ins.append(add)
        specs.append(o_spec)
    return pl.pallas_call(
        body, name=name, grid=(m // tm, n // tn, nk), in_specs=specs, out_specs=o_spec,
        out_shape=jax.ShapeDtypeStruct((m, n), out_dtype),
        scratch_shapes=[pltpu.VMEM((tm, tn), _F32)] if nk > 1 else [],
        compiler_params=_params(("parallel", "parallel", "arbitrary")),
    )(*ins)


def _row_spec(r, ts):
    if isinstance(r, tuple):
        arr, width, cblk = r
        return arr, pl.BlockSpec((ts, width), lambda i, cblk=cblk: (i, cblk))
    return r, pl.BlockSpec((ts, r.shape[1]), lambda i: (i, 0))


def _tup(v):
    return tuple(v) if isinstance(v, (tuple, list)) else (v,)


def _row_fwd(f, rows, params, outs, *, name, ts=256):
    arrs, specs = zip(*[_row_spec(r, ts) for r in rows])
    s = arrs[0].shape[0]
    ts = min(ts, s)
    n_r, n_p = len(rows), len(params)

    def body(*refs):
        rv = [r[...].astype(_F32) for r in refs[:n_r]]
        pv = [p[...] for p in refs[n_r:n_r + n_p]]
        res = _tup(f(*rv, *pv))
        for o_ref, v in zip(refs[n_r + n_p:], res):
            o_ref[...] = v.astype(o_ref.dtype)

    res = pl.pallas_call(
        body, name=name, grid=(s // ts,),
        in_specs=list(specs) + [pl.BlockSpec(p.shape, lambda i: (0, 0)) for p in params],
        out_specs=[pl.BlockSpec((ts, w), lambda i: (i, 0)) for w, _ in outs],
        out_shape=[jax.ShapeDtypeStruct((s, w), dt) for w, dt in outs],
        compiler_params=_params(("parallel",)),
    )(*arrs, *params)
    return res[0] if len(outs) == 1 else res


def _row_bwd(f, rows, params, cts, *, need=None, adds=None, row_dtypes=None, name, ts=256):
    arrs, specs = zip(*[_row_spec(r, ts) for r in rows])
    s = arrs[0].shape[0]
    ts = min(ts, s)
    n_r, n_p, n_c = len(rows), len(params), len(cts)
    need = [True] * n_r if need is None else need
    adds = {} if adds is None else adds
    add_keys = sorted(adds)
    row_dtypes = [_F32] * n_r if row_dtypes is None else row_dtypes
    needed = [j for j in range(n_r) if need[j]]
    widths = [specs[j].block_shape[1] for j in range(n_r)]

    def body(*refs):
        pos = 0
        r_refs = refs[pos:pos + n_r]; pos += n_r
        p_refs = refs[pos:pos + n_p]; pos += n_p
        c_refs = refs[pos:pos + n_c]; pos += n_c
        a_refs = refs[pos:pos + len(add_keys)]; pos += len(add_keys)
        dr_refs = refs[pos:pos + len(needed)]; pos += len(needed)
        dp_refs = refs[pos:pos + n_p]
        rv = [r[...].astype(_F32) for r in r_refs]
        pv = [p[...] for p in p_refs]
        _, vjp = jax.vjp(lambda *a: _tup(f(*a)), *rv, *pv)
        g = vjp(tuple(c[...].astype(_F32) for c in c_refs))
        for o_ref, j in zip(dr_refs, needed):
            v = g[j]
            if j in adds:
                v = v + a_refs[add_keys.index(j)][...].astype(_F32)
            o_ref[...] = v.astype(o_ref.dtype)
        if n_p:
            @pl.when(pl.program_id(0) == 0)
            def _():
                for dp in dp_refs:
                    dp[...] = jnp.zeros_like(dp)
            for dp, v in zip(dp_refs, g[n_r:]):
                dp[...] += v

    ct_specs = [pl.BlockSpec((ts, c.shape[1]), lambda i: (i, 0)) for c in cts]
    add_specs = [pl.BlockSpec((ts, adds[j].shape[1]), lambda i: (i, 0)) for j in add_keys]
    res = pl.pallas_call(
        body, name=name, grid=(s // ts,),
        in_specs=list(specs) + [pl.BlockSpec(p.shape, lambda i: (0, 0)) for p in params] + ct_specs + add_specs,
        out_specs=[pl.BlockSpec((ts, widths[j]), lambda i: (i, 0)) for j in needed]
        + [pl.BlockSpec(p.shape, lambda i: (0, 0)) for p in params],
        out_shape=[jax.ShapeDtypeStruct((s, widths[j]), row_dtypes[j]) for j in needed]
        + [jax.ShapeDtypeStruct(p.shape, _F32) for p in params],
        compiler_params=_params(("arbitrary",)),
    )(*arrs, *params, *cts, *[adds[j] for j in add_keys])
    return list(res[:len(needed)]), list(res[len(needed):])


_DN = {"nn": (((1,), (0,)), ((), ())), "nt": (((1,), (1,)), ((), ())), "tn": (((0,), (0,)), ((), ()))}


def _make_dot(passes):
    def raw(a, b, kind):
        dn = _DN[kind]
        if passes == 1 or _MXU == _F32:
            return lax.dot_general(a.astype(_MXU), b.astype(_MXU), dn, preferred_element_type=_F32, precision=_PREC)
        a_hi, b_hi = a.astype(_MXU), b.astype(_MXU)
        a_lo = (a - a_hi.astype(_F32)).astype(_MXU)
        b_lo = (b - b_hi.astype(_F32)).astype(_MXU)
        out = lax.dot_general(a_hi, b_hi, dn, preferred_element_type=_F32)
        out = out + lax.dot_general(a_lo, b_hi, dn, preferred_element_type=_F32)
        return out + lax.dot_general(a_hi, b_lo, dn, preferred_element_type=_F32)

    @functools.partial(jax.custom_vjp, nondiff_argnums=(2,))
    def dot(a, b, kind):
        return raw(a, b, kind)

    def fwd(a, b, kind):
        return raw(a, b, kind), (a, b)

    def bwd(kind, res, ct):
        a, b = res
        if kind == "nn":
            return raw(ct, b, "nt"), raw(a, ct, "tn")
        if kind == "nt":
            return raw(ct, b, "nn"), raw(ct, a, "tn")
        return raw(b, ct, "nt"), raw(a, ct, "nn")

    dot.defvjp(fwd, bwd)
    return dot


_dot1 = _make_dot(1)
_dot3 = _make_dot(3)


def _sig(v):
    return 1.0 / (1.0 + jnp.exp(-v))


def _silu(v):
    return v * _sig(v)


def _f_rms(x, g):
    return x * lax.rsqrt(jnp.mean(x * x, axis=-1, keepdims=True) + EPS) * g


def _f_gate(y, xs, z, dexp, g):
    v = (y + dexp * xs) * _silu(z)
    half = SSD_WIDTH // 2
    parts = []
    for grp in range(2):
        vg = v[:, grp * half:(grp + 1) * half]
        parts.append(vg * lax.rsqrt(jnp.mean(vg * vg, axis=-1, keepdims=True) + EPS) * g[:, grp * half:(grp + 1) * half])
    return jnp.concatenate(parts, axis=1)


def _f_ln(u, g, b):
    mu = jnp.mean(u, axis=-1, keepdims=True)
    var = jnp.mean(jnp.square(u - mu), axis=-1, keepdims=True)
    return _silu((u - mu) * lax.rsqrt(var + EPS) * g + b)


def _f_glu(a, g):
    return a * _sig(g)


def _f_swiglu(gate, up):
    return _silu(gate) * up


def _f_att(q, k, v):
    outs = []
    for h in range(X_HEADS):
        sl = slice(h * X_HEAD_DIM, (h + 1) * X_HEAD_DIM)
        s = _dot1(q[:, sl], k[:, sl], "nt") * (X_HEAD_DIM ** -0.5)
        s = s - lax.stop_gradient(jnp.max(s, axis=-1, keepdims=True))
        p = jnp.exp(s)
        p = p / jnp.sum(p, axis=-1, keepdims=True)
        outs.append(_dot1(p, v[:, sl], "nn"))
    return jnp.concatenate(outs, axis=1)


def _loss_bwd(x3, target, g, *, name, ts=256):
    s, d = x3.shape

    def f(x, t, gv):
        return 0.5 * jnp.sum(jnp.mean(jnp.square(_f_rms(x, gv) - t), axis=-1))

    def body(x_ref, t_ref, g_ref, dx_ref, dg_ref, l_ref):
        @pl.when(pl.program_id(0) == 0)
        def _():
            dg_ref[...] = jnp.zeros_like(dg_ref)
            l_ref[...] = jnp.zeros_like(l_ref)

        lv, (dx, dg) = jax.value_and_grad(f, argnums=(0, 2))(x_ref[...], t_ref[...], g_ref[...])
        dx_ref[...] = dx
        dg_ref[...] += dg
        l_ref[...] += lv

    row = pl.BlockSpec((ts, d), lambda i: (i, 0))
    return pl.pallas_call(
        body, name=name, grid=(s // ts,),
        in_specs=[row, row, pl.BlockSpec((1, d), lambda i: (0, 0))],
        out_specs=[row, pl.BlockSpec((1, d), lambda i: (0, 0)), pl.BlockSpec((SUBLANES, LANES), lambda i: (0, 0))],
        out_shape=[jax.ShapeDtypeStruct((s, d), _F32), jax.ShapeDtypeStruct((1, d), _F32),
                   jax.ShapeDtypeStruct((SUBLANES, LANES), _F32)],
        compiler_params=_params(("arbitrary",)),
    )(x3, target, g)


_CONV_PAD = 32
_CONV_ROWS = 128
_CONV_CB = 128


def _conv_taps(k_taps):
    groups = {}
    for k in range(k_taps):
        j = k_taps - 1 - k
        groups.setdefault(j % SUBLANES, []).append((k, j))
    return groups


def _conv_fwd(x, w, b, k_taps, *, name):
    s, c = x.shape
    kp = w.shape[0]
    pad, rows, cb = _CONV_PAD, _CONV_ROWS, _CONV_CB
    groups = _conv_taps(k_taps)

    def body(x_ref, w_ref, b_ref, o_ref, xp_ref):
        xp_ref[0:pad, :] = jnp.zeros((pad, cb), _F32)
        xp_ref[pad:pad + s, :] = x_ref[...]
        wv = w_ref[...]
        bias = jnp.broadcast_to(b_ref[...], (rows, cb))

        def chunk(r, carry):
            base = pl.multiple_of(r * rows, rows)
            win = xp_ref[pl.ds(base, rows + pad), :]
            acc = bias
            for rot, taps in groups.items():
                rolled = win if rot == 0 else pltpu.roll(win, rot, 0)
                for k, j in taps:
                    off = pad - (j - rot)
                    acc = acc + rolled[off:off + rows, :] * wv[k:k + 1, :]
            o_ref[pl.ds(base, rows), :] = acc
            return carry

        lax.fori_loop(0, s // rows, chunk, 0)

    col = pl.BlockSpec((s, cb), lambda i: (0, i))
    return pl.pallas_call(
        body, name=name, grid=(c // cb,),
        in_specs=[col, pl.BlockSpec((kp, cb), lambda i: (0, i)), pl.BlockSpec((1, cb), lambda i: (0, i))],
        out_specs=col, out_shape=jax.ShapeDtypeStruct((s, c), _F32),
        scratch_shapes=[pltpu.VMEM((s + pad, cb), _F32)],
        compiler_params=_params(("parallel",)),
    )(x, w, b)


def _conv_bwd(x, w, dy, k_taps, *, name):
    s, c = x.shape
    kp = w.shape[0]
    pad, rows, cb = _CONV_PAD, _CONV_ROWS, _CONV_CB
    groups = _conv_taps(k_taps)
    win_rows = rows + pad

    def fold(v):
        acc = v[0:SUBLANES, :]
        for i in range(1, rows // SUBLANES):
            acc = acc + v[i * SUBLANES:(i + 1) * SUBLANES, :]
        return acc

    def body(x_ref, w_ref, dy_ref, dx_ref, dw_ref, db_ref, xp_ref, dyp_ref, acc_ref, dbacc_ref):
        xp_ref[0:pad, :] = jnp.zeros((pad, cb), _F32)
        xp_ref[pad:pad + s, :] = x_ref[...]
        dyp_ref[0:s, :] = dy_ref[...]
        dyp_ref[s:s + pad, :] = jnp.zeros((pad, cb), _F32)
        acc_ref[...] = jnp.zeros_like(acc_ref)
        dbacc_ref[...] = jnp.zeros_like(dbacc_ref)
        wv = w_ref[...]

        def chunk(r, carry):
            base = pl.multiple_of(r * rows, rows)
            xwin = xp_ref[pl.ds(base, win_rows), :]
            dwin = dyp_ref[pl.ds(base, win_rows), :]
            dyc = dwin[0:rows, :]
            dxacc = jnp.zeros((rows, cb), _F32)
            for rot, taps in groups.items():
                xr = xwin if rot == 0 else pltpu.roll(xwin, rot, 0)
                dr = dwin if rot == 0 else pltpu.roll(dwin, win_rows - rot, 0)
                for k, j in taps:
                    a8 = j - rot
                    dxacc = dxacc + dr[a8:a8 + rows, :] * wv[k:k + 1, :]
                    prod = dyc * xr[pad - a8:pad - a8 + rows, :]
                    acc_ref[k * SUBLANES:(k + 1) * SUBLANES, :] += fold(prod)
            dbacc_ref[...] += fold(dyc)
            dx_ref[pl.ds(base, rows), :] = dxacc
            return carry

        lax.fori_loop(0, s // rows, chunk, 0)
        dw_ref[...] = jnp.zeros_like(dw_ref)
        for k in range(k_taps):
            dw_ref[k:k + 1, :] = jnp.sum(acc_ref[k * SUBLANES:(k + 1) * SUBLANES, :], axis=0, keepdims=True)
        db_ref[...] = jnp.sum(dbacc_ref[...], axis=0, keepdims=True)

    col = pl.BlockSpec((s, cb), lambda i: (0, i))
    wspec = pl.BlockSpec((kp, cb), lambda i: (0, i))
    bspec = pl.BlockSpec((1, cb), lambda i: (0, i))
    return pl.pallas_call(
        body, name=name, grid=(c // cb,),
        in_specs=[col, wspec, col], out_specs=[col, wspec, bspec],
        out_shape=[jax.ShapeDtypeStruct((s, c), _F32), jax.ShapeDtypeStruct((kp, c), _F32),
                   jax.ShapeDtypeStruct((1, c), _F32)],
        scratch_shapes=[pltpu.VMEM((s + pad, cb), _F32), pltpu.VMEM((s + pad, cb), _F32),
                        pltpu.VMEM((kp * SUBLANES, cb), _F32), pltpu.VMEM((SUBLANES, cb), _F32)],
        compiler_params=_params(("parallel",)),
    )(x, w, dy)


def _tri_sum(v, lower):
    l = v.shape[0]
    r, c = lax.broadcasted_iota(jnp.int32, (l, l), 0), lax.broadcasted_iota(jnp.int32, (l, l), 1)
    tri = ((r >= c) if lower else (r <= c)).astype(jnp.bfloat16)
    hi = v.astype(jnp.bfloat16)
    r1 = v - hi.astype(_F32)
    mid = r1.astype(jnp.bfloat16)
    lo = (r1 - mid.astype(_F32)).astype(jnp.bfloat16)
    out = jnp.zeros_like(v)
    for part in (hi, mid, lo):
        out = out + lax.dot_general(tri, part, _DN["nn"], preferred_element_type=_F32)
    return out


@jax.custom_vjp
def _cumsum_rows(v):
    return _tri_sum(v, True)


_cumsum_rows.defvjp(lambda v: (_tri_sum(v, True), None), lambda _, ct: (_tri_sum(ct, False),))


def _ssd_chunk(xbc, dtraw, prev, bias, alog):
    l = xbc.shape[0]
    xs = xbc[:, :SSD_WIDTH]
    bm = xbc[:, SSD_WIDTH:SSD_WIDTH + 2 * SSD_STATE]
    cm = xbc[:, SSD_WIDTH + 2 * SSD_STATE:]
    v = dtraw + bias
    dt = jnp.maximum(v, 0.0) + jnp.log1p(jnp.exp(-jnp.abs(v)))
    a_neg = -jnp.exp(alog)
    acs = _cumsum_rows(dt * a_neg)
    acs_t = acs.T
    dt_t = dt.T
    total = acs[l - 1:l, :]
    row = lax.broadcasted_iota(jnp.int32, (l, l), 0)
    colv = lax.broadcasted_iota(jnp.int32, (l, l), 1)
    causal = row >= colv
    lane_lo = lax.broadcasted_iota(jnp.int32, (l, LANES), 1) < HEAD_DIM
    row_lo = lax.broadcasted_iota(jnp.int32, (LANES, SSD_STATE), 0) < HEAD_DIM

    def pair_lanes(m, h0):
        return jnp.where(lane_lo, m[:, h0:h0 + 1], m[:, h0 + 1:h0 + 2])

    ys, news = [], []
    cb = {}
    for j in range(SSD_HEADS // 2):
        h0 = 2 * j
        grp = h0 // (SSD_HEADS // 2)
        bg = bm[:, grp * SSD_STATE:(grp + 1) * SSD_STATE]
        cg = cm[:, grp * SSD_STATE:(grp + 1) * SSD_STATE]
        if grp not in cb:
            cb[grp] = _dot3(cg, bg, "nt")
        xp = xs[:, j * LANES:(j + 1) * LANES]
        y = jnp.zeros((l, LANES), _F32)
        for hh, mask in ((h0, lane_lo), (h0 + 1, jnp.logical_not(lane_lo))):
            seg = acs[:, hh:hh + 1] - acs_t[hh:hh + 1, :]
            dec = jnp.exp(jnp.where(causal, seg, -jnp.inf))
            sc = cb[grp] * dec * dt_t[hh:hh + 1, :]
            y = y + _dot3(sc, jnp.where(mask, xp, 0.0), "nn")
        acs_p = pair_lanes(acs, h0)
        prev_p = prev[j * LANES:(j + 1) * LANES, :]
        y = y + _dot3(cg, prev_p, "nt") * jnp.exp(acs_p)
        wgt = jnp.exp(pair_lanes(jnp.broadcast_to(total, (l, LANES)), h0) - acs_p) * pair_lanes(dt, h0)
        st = _dot3(xp * wgt, bg, "tn")
        cdec = jnp.exp(jnp.where(row_lo, total[:, h0:h0 + 1], total[:, h0 + 1:h0 + 2]))
        news.append(prev_p * cdec + st)
        ys.append(y)
    return jnp.concatenate(ys, axis=1), jnp.concatenate(news, axis=0)


def _ssd_fwd(xbc, dtraw, bias, alog, *, name):
    s = xbc.shape[0]
    nc = s // CHUNK
    nstate = SSD_HEADS * HEAD_DIM

    def body(x_ref, dt_ref, b_ref, a_ref, y_ref, st_ref, state_ref):
        @pl.when(pl.program_id(0) == 0)
        def _():
            state_ref[...] = jnp.zeros_like(state_ref)

        prev = state_ref[...]
        st_ref[...] = prev
        y, new = _ssd_chunk(x_ref[...], dt_ref[...], prev, b_ref[...], a_ref[...])
        y_ref[...] = y
        state_ref[...] = new

    small = pl.BlockSpec((1, LANES), lambda i: (0, 0))
    return pl.pallas_call(
        body, name=name, grid=(nc,),
        in_specs=[pl.BlockSpec((CHUNK, XBC_WIDTH), lambda i: (i, 0)), pl.BlockSpec((CHUNK, LANES), lambda i: (i, 0)),
                  small, small],
        out_specs=[pl.BlockSpec((CHUNK, SSD_WIDTH), lambda i: (i, 0)),
                   pl.BlockSpec((None, nstate, SSD_STATE), lambda i: (i, 0, 0))],
        out_shape=[jax.ShapeDtypeStruct((s, SSD_WIDTH), _F32), jax.ShapeDtypeStruct((nc, nstate, SSD_STATE), _F32)],
        scratch_shapes=[pltpu.VMEM((nstate, SSD_STATE), _F32)],
        compiler_params=_params(("arbitrary",)),
    )(xbc, dtraw, bias, alog)


def _ssd_bwd(xbc, dtraw, states, bias, alog, dy, dxs_extra, *, name):
    s = xbc.shape[0]
    nc = s // CHUNK
    nstate = SSD_HEADS * HEAD_DIM

    def body(x_ref, dt_ref, st_ref, b_ref, a_ref, dy_ref, ex_ref, dx_ref, ddt_ref, db_ref, da_ref, dstate_ref):
        @pl.when(pl.program_id(0) == 0)
        def _():
            dstate_ref[...] = jnp.zeros_like(dstate_ref)
            db_ref[...] = jnp.zeros_like(db_ref)
            da_ref[...] = jnp.zeros_like(da_ref)

        _, vjp = jax.vjp(_ssd_chunk, x_ref[...], dt_ref[...], st_ref[...], b_ref[...], a_ref[...])
        dx, ddt, dprev, db, da = vjp((dy_ref[...], dstate_ref[...]))
        dx_ref[:, :SSD_WIDTH] = dx[:, :SSD_WIDTH] + ex_ref[...]
        dx_ref[:, SSD_WIDTH:] = dx[:, SSD_WIDTH:]
        ddt_ref[...] = ddt
        db_ref[...] += db
        da_ref[...] += da
        dstate_ref[...] = dprev

    rev = lambda i: (nc - 1 - i, 0)
    small = pl.BlockSpec((1, LANES), lambda i: (0, 0))
    return pl.pallas_call(
        body, name=name, grid=(nc,),
        in_specs=[pl.BlockSpec((CHUNK, XBC_WIDTH), rev), pl.BlockSpec((CHUNK, LANES), rev),
                  pl.BlockSpec((None, nstate, SSD_STATE), lambda i: (nc - 1 - i, 0, 0)), small, small,
                  pl.BlockSpec((CHUNK, SSD_WIDTH), rev), pl.BlockSpec((CHUNK, SSD_WIDTH), rev)],
        out_specs=[pl.BlockSpec((CHUNK, XBC_WIDTH), rev), pl.BlockSpec((CHUNK, LANES), rev), small, small],
        out_shape=[jax.ShapeDtypeStruct((s, XBC_WIDTH), _F32), jax.ShapeDtypeStruct((s, LANES), _F32),
                   jax.ShapeDtypeStruct((1, LANES), _F32), jax.ShapeDtypeStruct((1, LANES), _F32)],
        scratch_shapes=[pltpu.VMEM((nstate, SSD_STATE), _F32)],
        compiler_params=_params(("arbitrary",)),
    )(xbc, dtraw, states, bias, alog, dy, dxs_extra)


def _pad_cols(a, width):
    return jnp.pad(a, ((0, 0), (0, width - a.shape[1])))


def _pad_rows(a, rows):
    return jnp.pad(a, ((0, rows - a.shape[0]), (0, 0)))


def _local_step(x, mem, target, w, fetch, emit):
    bf = _MXU
    w_in = fetch('in', None)['w_in']
    z_end, xbc_end, dt_end = SSD_WIDTH, SSD_WIDTH + XBC_WIDTH, SSD_WIDTH + XBC_WIDTH + SSD_HEADS
    w_z, w_xbc = w_in[:, :z_end], w_in[:, z_end:xbc_end]
    w_dt = _pad_cols(w_in[:, xbc_end:dt_end], LANES)
    w_a, w_g = w_in[:, dt_end:dt_end + CF_WIDTH], w_in[:, dt_end + CF_WIDTH:]
    ssd_w = _pad_rows(w['ssd_conv_w'], SUBLANES)
    cf_w = _pad_rows(w['cf_conv_w'], 32)
    dt_bias = _pad_cols(w['ssd_dt_bias'], LANES)
    a_log = _pad_cols(w['ssd_A_log'], LANES)
    d_exp = jnp.repeat(w['ssd_D'], HEAD_DIM, axis=1)
    g_final = w['norm_final_g'].reshape(1, D_MODEL)
    d = D_MODEL

    h = _row_fwd(_f_rms, [x], [w['norm_mix_g']], [(d, bf)], name="f_norm_mix")
    z = _mm(h, w_z, name="f_in_z")
    xbc = _mm(h, w_xbc, name="f_in_xbc")
    dtr = _mm(h, w_dt, name="f_in_dt")
    ga = _mm(h, w_a, name="f_in_a")
    gg = _mm(h, w_g, name="f_in_g")
    xbc_c = _conv_fwd(xbc, ssd_w, w['ssd_conv_b'], SSD_CONV, name="f_ssd_conv")
    xbc_a = _row_fwd(_silu, [xbc_c], [], [(XBC_WIDTH, _F32)], name="f_ssd_silu")
    y_ssd, states = _ssd_fwd(xbc_a, dtr, dt_bias, a_log, name="f_ssd")
    xs_win = (xbc_a, SSD_WIDTH, 0)
    y_n = _row_fwd(_f_gate, [y_ssd, xs_win, z], [d_exp, w['ssd_norm_g']], [(d, bf)], name="f_ssd_gate")
    u_pre = _row_fwd(_f_glu, [ga, gg], [], [(CF_WIDTH, _F32)], name="f_glu")
    u_c = _conv_fwd(u_pre, cf_w, w['cf_conv_b'], CF_CONV, name="f_cf_conv")
    u = _row_fwd(_f_ln, [u_c], [w['cf_ln_g'], w['cf_ln_b']], [(d, bf)], name="f_cf_ln")
    wm = fetch('mid', y_n)
    w_out_y, w_out_u = wm['w_out'][:SSD_WIDTH], wm['w_out'][SSD_WIDTH:]
    x1 = _mm(y_n, w_out_y, add=x, name="f_out_y")
    x1 = _mm(u, w_out_u, add=x1, name="f_out_u")
    hq = _row_fwd(_f_rms, [x1], [w['norm_xattn_g']], [(d, bf)], name="f_norm_xattn")
    q = _mm(hq, wm['w_q'], name="f_q")
    memn = _row_fwd(_f_rms, [mem], [w['norm_mem_g']], [(d, bf)], name="f_norm_mem")
    kv = _mm(memn, wm['w_kv'], name="f_kv")
    k_mat, v_mat = kv[:, :d], kv[:, d:]
    o = _row_fwd(_f_att, [q], [k_mat, v_mat], [(d, bf)], name="f_att")
    x2 = _mm(o, wm['w_o'], add=x1, name="f_o")
    hf = _row_fwd(_f_rms, [x2], [w['norm_ffn_g']], [(d, bf)], name="f_norm_ffn")
    wf = fetch('ffn', hf)
    gate = _mm(hf, wf['w_gate'], name="f_gate")
    up = _mm(hf, wf['w_up'], name="f_up")
    act = _row_fwd(_f_swiglu, [gate, up], [], [(D_FF, bf)], name="f_swiglu", ts=128)
    x3 = _mm(act, wf['w_down'], add=x2, name="f_down")

    dx3, dg_final, loss = _loss_bwd(x3, target, g_final, name="b_loss")
    g = {'norm_final_g': dg_final.reshape(d)}

    dact = _mm(dx3, wf['w_down'], tb=True, name="b_down_x")
    dw_down = _mm(act, dx3, ta=True, name="b_down_w")
    (dgate, dup), _ = _row_bwd(_f_swiglu, [gate, up], [], [dact], row_dtypes=[bf, bf], name="b_swiglu", ts=128)
    emit({'w_down': dw_down, 'w_gate': _mm(hf, dgate, ta=True, name="b_gate_w"),
          'w_up': _mm(hf, dup, ta=True, name="b_up_w")})
    dhf = _mm(dgate, wf['w_gate'], tb=True, name="b_gate_x")
    dhf = _mm(dup, wf['w_up'], tb=True, add=dhf, name="b_up_x")
    (dx2,), (g['norm_ffn_g'],) = _row_bwd(_f_rms, [x2], [w['norm_ffn_g']], [dhf], adds={0: dx3}, name="b_norm_ffn")

    do = _mm(dx2, wm['w_o'], tb=True, name="b_o_x")
    dw_o = _mm(o, dx2, ta=True, name="b_o_w")
    (dq,), (dk, dv) = _row_bwd(_f_att, [q], [k_mat, v_mat], [do], row_dtypes=[bf], name="b_att")
    dw_q = _mm(hq, dq, ta=True, name="b_q_w")
    dhq = _mm(dq, wm['w_q'], tb=True, name="b_q_x")
    (dx1,), (g['norm_xattn_g'],) = _row_bwd(_f_rms, [x1], [w['norm_xattn_g']], [dhq], adds={0: dx2}, name="b_norm_xattn")
    dkv = jnp.concatenate([dk, dv], axis=1)
    emit({'w_o': dw_o, 'w_q': dw_q, 'w_kv': _mm(memn, dkv, ta=True, name="b_kv_w")})
    dmemn = _mm(dkv, wm['w_kv'], tb=True, name="b_kv_x")
    _, (g['norm_mem_g'],) = _row_bwd(_f_rms, [mem], [w['norm_mem_g']], [dmemn], need=[False], name="b_norm_mem")

    dyn = _mm(dx1, w_out_y, tb=True, name="b_out_y_x")
    du = _mm(dx1, w_out_u, tb=True, name="b_out_u_x")
    emit({'w_out': jnp.concatenate([_mm(y_n, dx1, ta=True, name="b_out_y_w"), _mm(u, dx1, ta=True, name="b_out_u_w")], axis=0)})
    (du_c,), (g['cf_ln_g'], g['cf_ln_b']) = _row_bwd(_f_ln, [u_c], [w['cf_ln_g'], w['cf_ln_b']], [du], name="b_cf_ln")
    du_pre, dcf_w, g['cf_conv_b'] = _conv_bwd(u_pre, cf_w, du_c, CF_CONV, name="b_cf_conv")
    g['cf_conv_w'] = dcf_w[:CF_CONV]
    (dga, dgg), _ = _row_bwd(_f_glu, [ga, gg], [], [du_pre], row_dtypes=[bf, bf], name="b_glu")
    (dy_ssd, dxs, dz), (dd_exp, g['ssd_norm_g']) = _row_bwd(
        _f_gate, [y_ssd, xs_win, z], [d_exp, w['ssd_norm_g']], [dyn], row_dtypes=[_F32, _F32, bf], name="b_ssd_gate")
    g['ssd_D'] = jnp.sum(dd_exp.reshape(SSD_HEADS, HEAD_DIM), axis=1).reshape(1, SSD_HEADS)
    dxbc_a, ddtr, ddt_bias, da_log = _ssd_bwd(xbc_a, dtr, states, dt_bias, a_log, dy_ssd, dxs, name="b_ssd")
    g['ssd_dt_bias'] = ddt_bias[:, :SSD_HEADS]
    g['ssd_A_log'] = da_log[:, :SSD_HEADS]
    (dxbc_c,), _ = _row_bwd(_silu, [xbc_c], [], [dxbc_a], name="b_ssd_silu")
    dxbc, dssd_w, g['ssd_conv_b'] = _conv_bwd(xbc, ssd_w, dxbc_c, SSD_CONV, name="b_ssd_conv")
    g['ssd_conv_w'] = dssd_w[:SSD_CONV]

    emit({'w_in': jnp.concatenate([
        _mm(h, dz, ta=True, name="b_in_z_w"), _mm(h, dxbc, ta=True, name="b_in_xbc_w"),
        _mm(h, ddtr, ta=True, name="b_in_dt_w")[:, :SSD_HEADS],
        _mm(h, dga, ta=True, name="b_in_a_w"), _mm(h, dgg, ta=True, name="b_in_g_w")], axis=1)})
    dh = _mm(dz, w_z, tb=True, name="b_in_z_x")
    dh = _mm(dxbc, w_xbc, tb=True, add=dh, name="b_in_xbc_x")
    dh = _mm(ddtr, w_dt, tb=True, add=dh, name="b_in_dt_x")
    dh = _mm(dga, w_a, tb=True, add=dh, name="b_in_a_x")
    dh = _mm(dgg, w_g, tb=True, add=dh, name="b_in_g_x")
    (dx,), (g['norm_mix_g'],) = _row_bwd(_f_rms, [x], [w['norm_mix_g']], [dh], adds={0: dx1}, name="b_norm_mix")
    return loss, dx, g


_ANY = pl.BlockSpec(memory_space=pl.ANY)


def _place():
    x, y, c = lax.axis_index("x"), lax.axis_index("y"), lax.axis_index("c")
    return x, y, c


def _all_gather(arrs, *, name):
    n = len(arrs)

    def body(*refs):
        ins, outs = refs[:n], refs[n:2 * n]
        send_sems, recv_sems, local_sems = refs[2 * n:]
        x, y, c = _place()
        me, sibling = (x, y, c), (x, y, 1 - c)
        chips = [(1 - x, y), (x, 1 - y), (1 - x, 1 - y)]

        def slot(a, dev):
            return outs[a].at[4 * dev[0] + 2 * dev[1] + dev[2]]

        def copy(a, k, block, to, src=None):
            return pltpu.make_async_remote_copy(
                src_ref=slot(a, block) if src is None else src, dst_ref=slot(a, block),
                send_sem=send_sems.at[a, k], recv_sem=recv_sems.at[a, k], device_id=to, device_id_type=MESH)

        mine = [pltpu.make_async_copy(ins[a], slot(a, me), local_sems.at[a]) for a in range(n)]
        for cp in mine:
            cp.start()
        first = []
        for a in range(n):
            first.append(copy(a, 0, me, sibling, src=ins[a]))
            first += [copy(a, 1 + j, me, (*chip, c), src=ins[a]) for j, chip in enumerate(chips)]
        for cp in first:
            cp.start()
        passed = []
        for a in range(n):
            for j, chip in enumerate(chips):
                copy(a, 1 + j, (*chip, c), me).wait_recv()
                fwd = copy(a, 4 + j, (*chip, c), sibling)
                fwd.start()
                passed.append(fwd)
        for a in range(n):
            copy(a, 0, sibling, me).wait_recv()
            for j, chip in enumerate(chips):
                copy(a, 4 + j, (*chip, 1 - c), me).wait_recv()
        for cp in first + passed:
            cp.wait_send()
        for cp in mine:
            cp.wait()

    return pl.pallas_call(
        body, name=name, in_specs=[_ANY] * n, out_specs=[_ANY] * n,
        out_shape=[jax.ShapeDtypeStruct((N_DEV,) + a.shape, a.dtype) for a in arrs],
        scratch_shapes=[pltpu.SemaphoreType.DMA((n, 7)), pltpu.SemaphoreType.DMA((n, 7)), pltpu.SemaphoreType.DMA((n,))],
    )(*arrs)


_HBM = pl.BlockSpec(memory_space=pltpu.HBM)
_SEM = pl.BlockSpec(memory_space=pltpu.SEMAPHORE)
_EFFECT = pltpu.SideEffectType.DATAFLOW_SIDE_EFFECTING
_FLIPS = [(dx, dy, dc) for dx in (0, 1) for dy in (0, 1) for dc in (0, 1)][1:]


def _peer(flip, x, y, c):
    return (1 - x if flip[0] else x, 1 - y if flip[1] else y, 1 - c if flip[2] else c)


def _send_start(srcs, blocked, *, name):
    n = len(srcs)
    lands = [jax.ShapeDtypeStruct(s.shape if blocked else (N_DEV,) + s.shape, s.dtype) for s in srcs]

    def body(*refs):
        src_refs, land_refs = refs[:n], refs[n:2 * n]
        send_sems, recv_sems = refs[2 * n], refs[2 * n + 1]
        token = refs[-1]
        x, y, c = _place()
        me = 4 * x + 2 * y + c
        for a in range(n):
            for k, flip in enumerate(_FLIPS):
                p = _peer(flip, x, y, c)
                src = src_refs[a].at[4 * p[0] + 2 * p[1] + p[2]] if blocked else src_refs[a]
                pltpu.make_async_remote_copy(
                    src_ref=src, dst_ref=land_refs[a].at[me], send_sem=send_sems.at[7 * a + k], recv_sem=recv_sems.at[7 * a + k],
                    device_id=p, device_id_type=MESH).start()
        token[...] = jnp.zeros_like(token)

    res = pl.pallas_call(
        body, name=name,
        out_shape=(pltpu.SemaphoreType.DMA((7 * n,)), pltpu.SemaphoreType.DMA((7 * n,)),
                   *[pltpu.HBM(s.shape, s.dtype) for s in srcs], *[pltpu.HBM(l.shape, l.dtype) for l in lands],
                   jax.ShapeDtypeStruct((SUBLANES, LANES), _F32)),
        in_specs=[_HBM] * (2 * n),
        out_specs=(_SEM, _SEM, *[_HBM] * (2 * n), pl.BlockSpec(memory_space=pltpu.VMEM)),
        input_output_aliases={i: 2 + i for i in range(2 * n)},
        compiler_params=pltpu.CompilerParams(has_side_effects=_EFFECT),
    )(*[pltpu.with_memory_space_constraint(s, pltpu.HBM) for s in srcs],
      *[pltpu.with_memory_space_constraint(lax.empty(l.shape, l.dtype), pltpu.HBM) for l in lands])
    return res[0], res[1], list(res[2:2 + n]), list(res[2 + n:2 + 2 * n]), res[-1]


def _send_wait(handles, after, blocked, *, name):
    send_sems, recv_sems, srcs, lands, _ = handles
    n = len(srcs)

    def body(*refs):
        src_refs, land_refs = refs[:n], refs[n:2 * n]
        send_sems, recv_sems = refs[2 * n], refs[2 * n + 1]
        x, y, c = _place()
        for a in range(n):
            for k, flip in enumerate(_FLIPS):
                p = _peer(flip, x, y, c)
                pid = 4 * p[0] + 2 * p[1] + p[2]
                cp = pltpu.make_async_remote_copy(
                    src_ref=src_refs[a].at[pid] if blocked else src_refs[a], dst_ref=land_refs[a].at[pid],
                    send_sem=send_sems.at[7 * a + k], recv_sem=recv_sems.at[7 * a + k], device_id=p, device_id_type=MESH)
                cp.wait_send()
                cp.wait_recv()

    res = pl.pallas_call(
        body, name=name,
        out_shape=tuple(pltpu.HBM(s.shape, s.dtype) for s in srcs + lands),
        in_specs=[_HBM] * (2 * n) + [_SEM, _SEM, _ANY], out_specs=tuple([_HBM] * (2 * n)),
        input_output_aliases={i: i for i in range(2 * n)},
        compiler_params=pltpu.CompilerParams(has_side_effects=_EFFECT),
    )(*srcs, *lands, send_sems, recv_sems, after)
    return list(res[:n]), list(res[n:])


def _adamw(parts, w, m, v, *, own=None, me=None, name):
    p, r, c = parts.shape
    tr = _pick(r, (256, 176, 128, 64, 32, 16, 8))
    if own is not None:
        return _adamw_own(parts, own, me, w, m, v, tr, name=name)

    def body(p_ref, w_ref, m_ref, v_ref, g_ref, d_ref, nm_ref, nv_ref):
        g = p_ref[0].astype(_F32)
        for i in range(1, p):
            g = g + p_ref[i].astype(_F32)
        _adamw_math(g, w_ref, m_ref, v_ref, g_ref, d_ref, nm_ref, nv_ref)

    blk = pl.BlockSpec((tr, c), lambda i: (i, 0))
    return pl.pallas_call(
        body, name=name, grid=(r // tr,),
        in_specs=[pl.BlockSpec((p, tr, c), lambda i: (0, i, 0)), blk, blk, blk], out_specs=[blk] * 4,
        out_shape=[jax.ShapeDtypeStruct((r, c), _F32)] * 4,
        compiler_params=_params(("parallel",)),
    )(parts, w, m, v)


def _adamw_math(g, w_ref, m_ref, v_ref, g_ref, d_ref, nm_ref, nv_ref):
    wv = w_ref[...]
    mn = ADAM_B1 * m_ref[...] + (1.0 - ADAM_B1) * g
    vn = ADAM_B2 * v_ref[...] + (1.0 - ADAM_B2) * jnp.square(g)
    m_hat = mn / (1.0 - ADAM_B1 ** ADAM_STEP)
    v_hat = vn / (1.0 - ADAM_B2 ** ADAM_STEP)
    g_ref[...] = g
    d_ref[...] = -ADAM_LR * (m_hat / (jnp.sqrt(v_hat) + ADAM_EPS) + ADAM_WD * wv)
    nm_ref[...] = mn
    nv_ref[...] = vn


def _adamw_own(parts, own, me, w, m, v, tr, *, name):
    p, r, c = parts.shape

    def body(me_ref, p_ref, own_ref, w_ref, m_ref, v_ref, g_ref, d_ref, nm_ref, nv_ref):
        mine = own_ref[...].astype(_F32)
        g = jnp.where(me_ref[0] == 0, mine, p_ref[0].astype(_F32))
        for i in range(1, p):
            g = g + jnp.where(me_ref[0] == i, mine, p_ref[i].astype(_F32))
        _adamw_math(g, w_ref, m_ref, v_ref, g_ref, d_ref, nm_ref, nv_ref)

    blk = pl.BlockSpec((tr, c), lambda i, me_ref: (i, 0))
    grid_spec = pltpu.PrefetchScalarGridSpec(
        num_scalar_prefetch=1, grid=(r // tr,),
        in_specs=[pl.BlockSpec((p, tr, c), lambda i, me_ref: (0, i, 0)),
                  pl.BlockSpec((None, tr, c), lambda i, me_ref: (me_ref[0], i, 0)), blk, blk, blk],
        out_specs=[blk] * 4)
    return pl.pallas_call(
        body, name=name, grid_spec=grid_spec, out_shape=[jax.ShapeDtypeStruct((r, c), _F32)] * 4,
        compiler_params=_params(("parallel",)),
    )(me.reshape(1).astype(jnp.int32), parts, own, w, m, v)


def _sum_parts(parts, *, name):
    p, r, c = parts.shape

    def body(p_ref, o_ref):
        g = p_ref[0].astype(_F32)
        for i in range(1, p):
            g = g + p_ref[i].astype(_F32)
        o_ref[...] = g

    return pl.pallas_call(body, name=name, out_shape=jax.ShapeDtypeStruct((r, c), _F32))(parts)


def _pack(vals, rows):
    flat = jnp.concatenate([v.reshape(-1) for v in vals])
    return jnp.pad(flat, (0, rows * LANES - flat.shape[0])).reshape(rows, LANES)


def _unpack(packed, shapes):
    flat = packed.reshape(-1)
    out, pos = [], 0
    for shp in shapes:
        size = math.prod(shp)
        out.append(flat[pos:pos + size].reshape(shp))
        pos += size
    return out


def _pack_rows(shapes):
    total = sum(math.prod(s) for s in shapes)
    return -(-total // (LANES * SUBLANES)) * SUBLANES


def kernel(x, mem, norm_mix_g, w_in, ssd_conv_w, ssd_conv_b, ssd_dt_bias, ssd_A_log, ssd_D, ssd_norm_g, cf_conv_w, cf_conv_b, cf_ln_g, cf_ln_b, w_out, norm_xattn_g, norm_mem_g, w_q, w_kv, w_o, norm_ffn_g, w_gate, w_up, w_down, norm_final_g, loss_target, m_norm_mix_g, m_w_in, m_ssd_conv_w, m_ssd_conv_b, m_ssd_dt_bias, m_ssd_A_log, m_ssd_D, m_ssd_norm_g, m_cf_conv_w, m_cf_conv_b, m_cf_ln_g, m_cf_ln_b, m_w_out, m_norm_xattn_g, m_norm_mem_g, m_w_q, m_w_kv, m_w_o, m_norm_ffn_g, m_w_gate, m_w_up, m_w_down, m_norm_final_g, v_norm_mix_g, v_w_in, v_ssd_conv_w, v_ssd_conv_b, v_ssd_dt_bias, v_ssd_A_log, v_ssd_D, v_ssd_norm_g, v_cf_conv_w, v_cf_conv_b, v_cf_ln_g, v_cf_ln_b, v_w_out, v_norm_xattn_g, v_norm_mem_g, v_w_q, v_w_kv, v_w_o, v_norm_ffn_g, v_w_gate, v_w_up, v_w_down, v_norm_final_g):
    args = dict(locals())
    wts = {n: args[n] for n in WEIGHT_NAMES}
    mom = {n: args["m_" + n] for n in WEIGHT_NAMES}
    var = {n: args["v_" + n] for n in WEIGHT_NAMES}
    me = 4 * lax.axis_index("x") + 2 * lax.axis_index("y") + lax.axis_index("c")

    groups = {'in': ['w_in'], 'mid': ['w_out', 'w_q', 'w_kv', 'w_o'], 'ffn': ['w_gate', 'w_up', 'w_down']}
    gathers = {grp: _send_start([wts[n][0].astype(_MXU) for n in names], False, name="gather_%s_start" % grp)
               for grp, names in groups.items()}
    started = gathers['ffn'][4]

    def fetch(grp, after):
        srcs, lands = _send_wait(gathers[grp], started if after is None else after, False, name="gather_%s_wait" % grp)
        out = {}
        for n, own, gth in zip(groups[grp], srcs, lands):
            gth = lax.dynamic_update_slice_in_dim(gth, own[None], me, axis=0)
            if n in COL_SHARDED:
                out[n] = jnp.transpose(gth, (1, 0, 2)).reshape(gth.shape[1], N_DEV * gth.shape[2])
            else:
                out[n] = gth.reshape(N_DEV * gth.shape[1], gth.shape[2])
        return out

    exchanges = []

    def emit(grads):
        blocks = []
        for n, gw in grads.items():
            if n in COL_SHARDED:
                gw = jnp.transpose(gw.reshape(gw.shape[0], N_DEV, gw.shape[1] // N_DEV), (1, 0, 2))
            else:
                gw = gw.reshape(N_DEV, gw.shape[0] // N_DEV, gw.shape[1])
            blocks.append(gw.astype(jnp.bfloat16))
        first = next(iter(grads))
        exchanges.append((list(grads), _send_start(blocks, True, name="exchange_%s_start" % first), first))

    conv_w = _all_gather([ssd_conv_w[0], cf_conv_w[0]], name="gather_conv_weights")
    full = {n: wts[n] for n in WEIGHT_NAMES if n not in BIG}
    full['ssd_conv_w'] = jnp.transpose(conv_w[0], (1, 0, 2)).reshape(SSD_CONV, XBC_WIDTH)
    full['cf_conv_w'] = jnp.transpose(conv_w[1], (1, 0, 2)).reshape(CF_CONV, CF_WIDTH)
    full['norm_mix_g'] = norm_mix_g + started[0:1, 0:1]

    loss_blk, grad_x, g = _local_step(x[0], mem[0], loss_target[0], full, fetch, emit)
    loss = lax.psum(loss_blk[0, 0], ("x", "y", "c"))

    out_g, out_d, out_m, out_v = {}, {}, {}, {}
    for names, handles, first in exchanges:
        srcs, lands = _send_wait(handles, grad_x, True, name="exchange_%s_wait" % first)
        for n, own, parts in zip(names, srcs, lands):
            res = _adamw(parts, wts[n][0], mom[n][0], var[n][0], own=own, me=me, name="adamw_" + n)
            out_g[n], out_d[n], out_m[n], out_v[n] = [r[None] for r in res]

    small = [n for n in WEIGHT_NAMES if n not in BIG]
    small_shapes = [full[n].shape for n in small]
    rows = _pack_rows(small_shapes)
    (small_parts,) = _all_gather([_pack([g[n] for n in small], rows)], name="gather_small_grads")
    small_sum = dict(zip(small, _unpack(_sum_parts(small_parts, name="sum_small_grads"), small_shapes)))
    small_sum['ssd_conv_w'] = lax.dynamic_slice_in_dim(small_sum['ssd_conv_w'], me * (XBC_WIDTH // N_DEV), XBC_WIDTH // N_DEV, axis=1)[None]
    small_sum['cf_conv_w'] = lax.dynamic_slice_in_dim(small_sum['cf_conv_w'], me * (CF_WIDTH // N_DEV), CF_WIDTH // N_DEV, axis=1)[None]
    shard_shapes = [wts[n].shape for n in small]
    rows2 = _pack_rows(shard_shapes)
    res = _adamw(_pack([small_sum[n] for n in small], rows2)[None], _pack([wts[n] for n in small], rows2),
                 _pack([mom[n] for n in small], rows2), _pack([var[n] for n in small], rows2), name="adamw_small")
    for dst, packed in zip((out_g, out_d, out_m, out_v), res):
        dst.update(zip(small, _unpack(packed, shard_shapes)))

    return (loss, grad_x[None], *[out_g[n] for n in WEIGHT_NAMES], *[out_d[n] for n in WEIGHT_NAMES],
            *[out_m[n] for n in WEIGHT_NAMES], *[out_v[n] for n in WEIGHT_NAMES])
```

```python
import functools
import math

import jax
import jax.numpy as jnp
from jax import lax
from jax.experimental import pallas as pl
from jax.experimental.pallas import tpu as pltpu

_F32 = jnp.float32
_MXU = jnp.bfloat16
_PREC = None
_VMEM_LIMIT = 56 * 1024 * 1024

D_MODEL = 1024
HEAD_DIM = 64
SSD_HEADS = 16
SSD_WIDTH = 1024
SSD_STATE = 128
SSD_CONV = 4
CHUNK = 128
XBC_WIDTH = 1536
CF_WIDTH = 1024
CF_CONV = 31
X_HEADS = 4
X_HEAD_DIM = 256
D_FF = 2816
EPS = 1e-6
N_DEV = 8
LANES = 128
SUBLANES = 8

ADAM_LR = 0.001
ADAM_B1 = 0.9
ADAM_B2 = 0.999
ADAM_EPS = 1e-08
ADAM_WD = 0.01
ADAM_STEP = 10

MESH = pl.DeviceIdType.MESH
WEIGHT_NAMES = ['norm_mix_g', 'w_in', 'ssd_conv_w', 'ssd_conv_b', 'ssd_dt_bias', 'ssd_A_log', 'ssd_D', 'ssd_norm_g',
                'cf_conv_w', 'cf_conv_b', 'cf_ln_g', 'cf_ln_b', 'w_out', 'norm_xattn_g', 'norm_mem_g', 'w_q', 'w_kv',
                'w_o', 'norm_ffn_g', 'w_gate', 'w_up', 'w_down', 'norm_final_g']
BIG = ['w_in', 'w_out', 'w_q', 'w_kv', 'w_o', 'w_gate', 'w_up', 'w_down']
COL_SHARDED = ('w_in', 'w_kv', 'w_gate', 'w_up')


def _params(sem=None):
    return pltpu.CompilerParams(dimension_semantics=sem, vmem_limit_bytes=_VMEM_LIMIT)


def _pick(n, cands):
    for c in cands:
        if n % c == 0:
            return c
    return n


def _mm(a, b, *, ta=False, tb=False, add=None, out_dtype=_F32, name):
    (kdim, m) = a.shape if ta else a.shape[::-1]
    (n, k2) = b.shape if tb else b.shape[::-1]
    assert kdim == k2, (a.shape, b.shape, ta, tb)
    tm = _pick(m, (512, 1408, 256, 128))
    tn = n if n <= 1536 else _pick(n, (1408, 1024, 512, 256, 128))
    tk = kdim if kdim <= 1536 else _pick(kdim, (1408, 1024, 512, 256, 128))
    nk = kdim // tk
    dn = (((0 if ta else 1,), (1 if tb else 0,)), ((), ()))

    def body(*refs):
        a_ref, b_ref = refs[0], refs[1]
        add_ref = refs[2] if add is not None else None
        o_ref = refs[3 if add is not None else 2]
        acc_ref = refs[-1]
        k = pl.program_id(2)
        prod = lax.dot_general(a_ref[...].astype(_MXU), b_ref[...].astype(_MXU), dn,
                               preferred_element_type=_F32, precision=_PREC)

        def finish(r):
            if add_ref is not None:
                r = r + add_ref[...].astype(_F32)
            o_ref[...] = r.astype(o_ref.dtype)

        if nk == 1:
            finish(prod)
            return

        @pl.when(k == 0)
        def _():
            acc_ref[...] = prod

        @pl.when(jnp.logical_and(k > 0, k < nk - 1))
        def _():
            acc_ref[...] += prod

        @pl.when(k == nk - 1)
        def _():
            finish(acc_ref[...] + prod)

    a_spec = pl.BlockSpec((tk, tm), lambda i, j, k: (k, i)) if ta else pl.BlockSpec((tm, tk), lambda i, j, k: (i, k))
    b_spec = pl.BlockSpec((tn, tk), lambda i, j, k: (j, k)) if tb else pl.BlockSpec((tk, tn), lambda i, j, k: (k, j))
    o_spec = pl.BlockSpec((tm, tn), lambda i, j, k: (i, j))
    ins, specs = [a, b], [a_spec, b_spec]
    if add is not None:
        ins.append(add)
        specs.append(o_spec)
    return pl.pallas_call(
        body, name=name, grid=(m // tm, n // tn, nk), in_specs=specs, out_specs=o_spec,
        out_shape=jax.ShapeDtypeStruct((m, n), out_dtype),
        scratch_shapes=[pltpu.VMEM((tm, tn), _F32)] if nk > 1 else [],
        compiler_params=_params(("parallel", "parallel", "arbitrary")),
    )(*ins)


def _row_spec(r, ts):
    if isinstance(r, tuple):
        arr, width, cblk = r
        return arr, pl.BlockSpec((ts, width), lambda i, cblk=cblk: (i, cblk))
    return r, pl.BlockSpec((ts, r.shape[1]), lambda i: (i, 0))


def _tup(v):
    return tuple(v) if isinstance(v, (tuple, list)) else (v,)


def _row_fwd(f, rows, params, outs, *, name, ts=256):
    arrs, specs = zip(*[_row_spec(r, ts) for r in rows])
    s = arrs[0].shape[0]
    ts = min(ts, s)
    n_r, n_p = len(rows), len(params)

    def body(*refs):
        rv = [r[...].astype(_F32) for r in refs[:n_r]]
        pv = [p[...] for p in refs[n_r:n_r + n_p]]
        res = _tup(f(*rv, *pv))
        for o_ref, v in zip(refs[n_r + n_p:], res):
            o_ref[...] = v.astype(o_ref.dtype)

    res = pl.pallas_call(
        body, name=name, grid=(s // ts,),
        in_specs=list(specs) + [pl.BlockSpec(p.shape, lambda i: (0, 0)) for p in params],
        out_specs=[pl.BlockSpec((ts, w), lambda i: (i, 0)) for w, _ in outs],
        out_shape=[jax.ShapeDtypeStruct((s, w), dt) for w, dt in outs],
        compiler_params=_params(("parallel",)),
    )(*arrs, *params)
    return res[0] if len(outs) == 1 else res


def _row_bwd(f, rows, params, cts, *, need=None, adds=None, row_dtypes=None, name, ts=256):
    arrs, specs = zip(*[_row_spec(r, ts) for r in rows])
    s = arrs[0].shape[0]
    ts = min(ts, s)
    n_r, n_p, n_c = len(rows), len(params), len(cts)
    need = [True] * n_r if need is None else need
    adds = {} if adds is None else adds
    add_keys = sorted(adds)
    row_dtypes = [_F32] * n_r if row_dtypes is None else row_dtypes
    needed = [j for j in range(n_r) if need[j]]
    widths = [specs[j].block_shape[1] for j in range(n_r)]

    def body(*refs):
        pos = 0
        r_refs = refs[pos:pos + n_r]; pos += n_r
        p_refs = refs[pos:pos + n_p]; pos += n_p
        c_refs = refs[pos:pos + n_c]; pos += n_c
        a_refs = refs[pos:pos + len(add_keys)]; pos += len(add_keys)
        dr_refs = refs[pos:pos + len(needed)]; pos += len(needed)
        dp_refs = refs[pos:pos + n_p]
        rv = [r[...].astype(_F32) for r in r_refs]
        pv = [p[...] for p in p_refs]
        _, vjp = jax.vjp(lambda *a: _tup(f(*a)), *rv, *pv)
        g = vjp(tuple(c[...].astype(_F32) for c in c_refs))
        for o_ref, j in zip(dr_refs, needed):
            v = g[j]
            if j in adds:
                v = v + a_refs[add_keys.index(j)][...].astype(_F32)
            o_ref[...] = v.astype(o_ref.dtype)
        if n_p:
            @pl.when(pl.program_id(0) == 0)
            def _():
                for dp in dp_refs:
                    dp[...] = jnp.zeros_like(dp)
            for dp, v in zip(dp_refs, g[n_r:]):
                dp[...] += v

    ct_specs = [pl.BlockSpec((ts, c.shape[1]), lambda i: (i, 0)) for c in cts]
    add_specs = [pl.BlockSpec((ts, adds[j].shape[1]), lambda i: (i, 0)) for j in add_keys]
    res = pl.pallas_call(
        body, name=name, grid=(s // ts,),
        in_specs=list(specs) + [pl.BlockSpec(p.shape, lambda i: (0, 0)) for p in params] + ct_specs + add_specs,
        out_specs=[pl.BlockSpec((ts, widths[j]), lambda i: (i, 0)) for j in needed]
        + [pl.BlockSpec(p.shape, lambda i: (0, 0)) for p in params],
        out_shape=[jax.ShapeDtypeStruct((s, widths[j]), row_dtypes[j]) for j in needed]
        + [jax.ShapeDtypeStruct(p.shape, _F32) for p in params],
        compiler_params=_params(("arbitrary",)),
    )(*arrs, *params, *cts, *[adds[j] for j in add_keys])
    return list(res[:len(needed)]), list(res[len(needed):])


_DN = {"nn": (((1,), (0,)), ((), ())), "nt": (((1,), (1,)), ((), ())), "tn": (((0,), (0,)), ((), ()))}


def _make_dot(passes):
    def raw(a, b, kind):
        dn = _DN[kind]
        if passes == 1 or _MXU == _F32:
            return lax.dot_general(a.astype(_MXU), b.astype(_MXU), dn, preferred_element_type=_F32, precision=_PREC)
        a_hi, b_hi = a.astype(_MXU), b.astype(_MXU)
        a_lo = (a - a_hi.astype(_F32)).astype(_MXU)
        b_lo = (b - b_hi.astype(_F32)).astype(_MXU)
        out = lax.dot_general(a_hi, b_hi, dn, preferred_element_type=_F32)
        out = out + lax.dot_general(a_lo, b_hi, dn, preferred_element_type=_F32)
        return out + lax.dot_general(a_hi, b_lo, dn, preferred_element_type=_F32)

    @functools.partial(jax.custom_vjp, nondiff_argnums=(2,))
    def dot(a, b, kind):
        return raw(a, b, kind)

    def fwd(a, b, kind):
        return raw(a, b, kind), (a, b)

    def bwd(kind, res, ct):
        a, b = res
        if kind == "nn":
            return raw(ct, b, "nt"), raw(a, ct, "tn")
        if kind == "nt":
            return raw(ct, b, "nn"), raw(ct, a, "tn")
        return raw(b, ct, "nt"), raw(a, ct, "nn")

    dot.defvjp(fwd, bwd)
    return dot


_dot1 = _make_dot(1)
_dot3 = _make_dot(3)


def _sig(v):
    return 1.0 / (1.0 + jnp.exp(-v))


def _silu(v):
    return v * _sig(v)


def _f_rms(x, g):
    return x * lax.rsqrt(jnp.mean(x * x, axis=-1, keepdims=True) + EPS) * g


def _f_gate(y, xs, z, dexp, g):
    v = (y + dexp * xs) * _silu(z)
    half = SSD_WIDTH // 2
    parts = []
    for grp in range(2):
        vg = v[:, grp * half:(grp + 1) * half]
        parts.append(vg * lax.rsqrt(jnp.mean(vg * vg, axis=-1, keepdims=True) + EPS) * g[:, grp * half:(grp + 1) * half])
    return jnp.concatenate(parts, axis=1)


def _f_ln(u, g, b):
    mu = jnp.mean(u, axis=-1, keepdims=True)
    var = jnp.mean(jnp.square(u - mu), axis=-1, keepdims=True)
    return _silu((u - mu) * lax.rsqrt(var + EPS) * g + b)


def _f_glu(a, g):
    return a * _sig(g)


def _f_swiglu(gate, up):
    return _silu(gate) * up


def _f_att(q, k, v):
    outs = []
    for h in range(X_HEADS):
        sl = slice(h * X_HEAD_DIM, (h + 1) * X_HEAD_DIM)
        s = _dot1(q[:, sl], k[:, sl], "nt") * (X_HEAD_DIM ** -0.5)
        s = s - lax.stop_gradient(jnp.max(s, axis=-1, keepdims=True))
        p = jnp.exp(s)
        p = p / jnp.sum(p, axis=-1, keepdims=True)
        outs.append(_dot1(p, v[:, sl], "nn"))
    return jnp.concatenate(outs, axis=1)


def _loss_bwd(x3, target, g, *, name, ts=256):
    s, d = x3.shape

    def f(x, t, gv):
        return 0.5 * jnp.sum(jnp.mean(jnp.square(_f_rms(x, gv) - t), axis=-1))

    def body(x_ref, t_ref, g_ref, dx_ref, dg_ref, l_ref):
        @pl.when(pl.program_id(0) == 0)
        def _():
            dg_ref[...] = jnp.zeros_like(dg_ref)
            l_ref[...] = jnp.zeros_like(l_ref)

        lv, (dx, dg) = jax.value_and_grad(f, argnums=(0, 2))(x_ref[...], t_ref[...], g_ref[...])
        dx_ref[...] = dx
        dg_ref[...] += dg
        l_ref[...] += lv

    row = pl.BlockSpec((ts, d), lambda i: (i, 0))
    return pl.pallas_call(
        body, name=name, grid=(s // ts,),
        in_specs=[row, row, pl.BlockSpec((1, d), lambda i: (0, 0))],
        out_specs=[row, pl.BlockSpec((1, d), lambda i: (0, 0)), pl.BlockSpec((SUBLANES, LANES), lambda i: (0, 0))],
        out_shape=[jax.ShapeDtypeStruct((s, d), _F32), jax.ShapeDtypeStruct((1, d), _F32),
                   jax.ShapeDtypeStruct((SUBLANES, LANES), _F32)],
        compiler_params=_params(("arbitrary",)),
    )(x3, target, g)


_CONV_PAD = 32
_CONV_ROWS = 128
_CONV_CB = 128


def _conv_taps(k_taps):
    groups = {}
    for k in range(k_taps):
        j = k_taps - 1 - k
        groups.setdefault(j % SUBLANES, []).append((k, j))
    return groups


def _conv_fwd(x, w, b, k_taps, *, name):
    s, c = x.shape
    kp = w.shape[0]
    pad, rows, cb = _CONV_PAD, _CONV_ROWS, _CONV_CB
    groups = _conv_taps(k_taps)

    def body(x_ref, w_ref, b_ref, o_ref, xp_ref):
        xp_ref[0:pad, :] = jnp.zeros((pad, cb), _F32)
        xp_ref[pad:pad + s, :] = x_ref[...]
        wv = w_ref[...]
        bias = jnp.broadcast_to(b_ref[...], (rows, cb))

        def chunk(r, carry):
            base = pl.multiple_of(r * rows, rows)
            win = xp_ref[pl.ds(base, rows + pad), :]
            acc = bias
            for rot, taps in groups.items():
                rolled = win if rot == 0 else pltpu.roll(win, rot, 0)
                for k, j in taps:
                    off = pad - (j - rot)
                    acc = acc + rolled[off:off + rows, :] * wv[k:k + 1, :]
            o_ref[pl.ds(base, rows), :] = acc
            return carry

        lax.fori_loop(0, s // rows, chunk, 0)

    col = pl.BlockSpec((s, cb), lambda i: (0, i))
    return pl.pallas_call(
        body, name=name, grid=(c // cb,),
        in_specs=[col, pl.BlockSpec((kp, cb), lambda i: (0, i)), pl.BlockSpec((1, cb), lambda i: (0, i))],
        out_specs=col, out_shape=jax.ShapeDtypeStruct((s, c), _F32),
        scratch_shapes=[pltpu.VMEM((s + pad, cb), _F32)],
        compiler_params=_params(("parallel",)),
    )(x, w, b)


def _conv_bwd(x, w, dy, k_taps, *, name):
    s, c = x.shape
    kp = w.shape[0]
    pad, rows, cb = _CONV_PAD, _CONV_ROWS, _CONV_CB
    groups = _conv_taps(k_taps)
    win_rows = rows + pad

    def fold(v):
        acc = v[0:SUBLANES, :]
        for i in range(1, rows // SUBLANES):
            acc = acc + v[i * SUBLANES:(i + 1) * SUBLANES, :]
        return acc

    def body(x_ref, w_ref, dy_ref, dx_ref, dw_ref, db_ref, xp_ref, dyp_ref, acc_ref, dbacc_ref):
        xp_ref[0:pad, :] = jnp.zeros((pad, cb), _F32)
        xp_ref[pad:pad + s, :] = x_ref[...]
        dyp_ref[0:s, :] = dy_ref[...]
        dyp_ref[s:s + pad, :] = jnp.zeros((pad, cb), _F32)
        acc_ref[...] = jnp.zeros_like(acc_ref)
        dbacc_ref[...] = jnp.zeros_like(dbacc_ref)
        wv = w_ref[...]

        def chunk(r, carry):
            base = pl.multiple_of(r * rows, rows)
            xwin = xp_ref[pl.ds(base, win_rows), :]
            dwin = dyp_ref[pl.ds(base, win_rows), :]
            dyc = dwin[0:rows, :]
            dxacc = jnp.zeros((rows, cb), _F32)
            for rot, taps in groups.items():
                xr = xwin if rot == 0 else pltpu.roll(xwin, rot, 0)
                dr = dwin if rot == 0 else pltpu.roll(dwin, win_rows - rot, 0)
                for k, j in taps:
                    a8 = j - rot
                    dxacc = dxacc + dr[a8:a8 + rows, :] * wv[k:k + 1, :]
                    prod = dyc * xr[pad - a8:pad - a8 + rows, :]
                    acc_ref[k * SUBLANES:(k + 1) * SUBLANES, :] += fold(prod)
            dbacc_ref[...] += fold(dyc)
            dx_ref[pl.ds(base, rows), :] = dxacc
            return carry

        lax.fori_loop(0, s // rows, chunk, 0)
        dw_ref[...] = jnp.zeros_like(dw_ref)
        for k in range(k_taps):
            dw_ref[k:k + 1, :] = jnp.sum(acc_ref[k * SUBLANES:(k + 1) * SUBLANES, :], axis=0, keepdims=True)
        db_ref[...] = jnp.sum(dbacc_ref[...], axis=0, keepdims=True)

    col = pl.BlockSpec((s, cb), lambda i: (0, i))
    wspec = pl.BlockSpec((kp, cb), lambda i: (0, i))
    bspec = pl.BlockSpec((1, cb), lambda i: (0, i))
    return pl.pallas_call(
        body, name=name, grid=(c // cb,),
        in_specs=[col, wspec, col], out_specs=[col, wspec, bspec],
        out_shape=[jax.ShapeDtypeStruct((s, c), _F32), jax.ShapeDtypeStruct((kp, c), _F32),
                   jax.ShapeDtypeStruct((1, c), _F32)],
        scratch_shapes=[pltpu.VMEM((s + pad, cb), _F32), pltpu.VMEM((s + pad, cb), _F32),
                        pltpu.VMEM((kp * SUBLANES, cb), _F32), pltpu.VMEM((SUBLANES, cb), _F32)],
        compiler_params=_params(("parallel",)),
    )(x, w, dy)


def _tri_sum(v, lower):
    l = v.shape[0]
    r, c = lax.broadcasted_iota(jnp.int32, (l, l), 0), lax.broadcasted_iota(jnp.int32, (l, l), 1)
    tri = ((r >= c) if lower else (r <= c)).astype(jnp.bfloat16)
    hi = v.astype(jnp.bfloat16)
    r1 = v - hi.astype(_F32)
    mid = r1.astype(jnp.bfloat16)
    lo = (r1 - mid.astype(_F32)).astype(jnp.bfloat16)
    out = jnp.zeros_like(v)
    for part in (hi, mid, lo):
        out = out + lax.dot_general(tri, part, _DN["nn"], preferred_element_type=_F32)
    return out


@jax.custom_vjp
def _cumsum_rows(v):
    return _tri_sum(v, True)


_cumsum_rows.defvjp(lambda v: (_tri_sum(v, True), None), lambda _, ct: (_tri_sum(ct, False),))


def _ssd_chunk(xbc, dtraw, prev, bias, alog):
    l = xbc.shape[0]
    xs = xbc[:, :SSD_WIDTH]
    bm = xbc[:, SSD_WIDTH:SSD_WIDTH + 2 * SSD_STATE]
    cm = xbc[:, SSD_WIDTH + 2 * SSD_STATE:]
    v = dtraw + bias
    dt = jnp.maximum(v, 0.0) + jnp.log1p(jnp.exp(-jnp.abs(v)))
    a_neg = -jnp.exp(alog)
    acs = _cumsum_rows(dt * a_neg)
    acs_t = acs.T
    dt_t = dt.T
    total = acs[l - 1:l, :]
    row = lax.broadcasted_iota(jnp.int32, (l, l), 0)
    colv = lax.broadcasted_iota(jnp.int32, (l, l), 1)
    causal = row >= colv
    lane_lo = lax.broadcasted_iota(jnp.int32, (l, LANES), 1) < HEAD_DIM
    row_lo = lax.broadcasted_iota(jnp.int32, (LANES, SSD_STATE), 0) < HEAD_DIM

    def pair_lanes(m, h0):
        return jnp.where(lane_lo, m[:, h0:h0 + 1], m[:, h0 + 1:h0 + 2])

    ys, news = [], []
    cb = {}
    for j in range(SSD_HEADS // 2):
        h0 = 2 * j
        grp = h0 // (SSD_HEADS // 2)
        bg = bm[:, grp * SSD_STATE:(grp + 1) * SSD_STATE]
        cg = cm[:, grp * SSD_STATE:(grp + 1) * SSD_STATE]
        if grp not in cb:
            cb[grp] = _dot1(cg, bg, "nt")
        xp = xs[:, j * LANES:(j + 1) * LANES]
        y = jnp.zeros((l, LANES), _F32)
        for hh, mask in ((h0, lane_lo), (h0 + 1, jnp.logical_not(lane_lo))):
            seg = acs[:, hh:hh + 1] - acs_t[hh:hh + 1, :]
            dec = jnp.exp(jnp.where(causal, seg, -jnp.inf))
            sc = cb[grp] * dec * dt_t[hh:hh + 1, :]
            y = y + _dot1(sc, jnp.where(mask, xp, 0.0), "nn")
        acs_p = pair_lanes(acs, h0)
        prev_p = prev[j * LANES:(j + 1) * LANES, :]
        y = y + _dot3(cg, prev_p, "nt") * jnp.exp(acs_p)
        wgt = jnp.exp(pair_lanes(jnp.broadcast_to(total, (l, LANES)), h0) - acs_p) * pair_lanes(dt, h0)
        st = _dot3(xp * wgt, bg, "tn")
        cdec = jnp.exp(jnp.where(row_lo, total[:, h0:h0 + 1], total[:, h0 + 1:h0 + 2]))
        news.append(prev_p * cdec + st)
        ys.append(y)
    return jnp.concatenate(ys, axis=1), jnp.concatenate(news, axis=0)


def _ssd_fwd(xbc, dtraw, bias, alog, *, name):
    s = xbc.shape[0]
    nc = s // CHUNK
    nstate = SSD_HEADS * HEAD_DIM

    def body(x_ref, dt_ref, b_ref, a_ref, y_ref, st_ref, state_ref):
        @pl.when(pl.program_id(0) == 0)
        def _():
            state_ref[...] = jnp.zeros_like(state_ref)

        prev = state_ref[...]
        st_ref[...] = prev
        y, new = _ssd_chunk(x_ref[...], dt_ref[...], prev, b_ref[...], a_ref[...])
        y_ref[...] = y
        state_ref[...] = new

    small = pl.BlockSpec((1, LANES), lambda i: (0, 0))
    return pl.pallas_call(
        body, name=name, grid=(nc,),
        in_specs=[pl.BlockSpec((CHUNK, XBC_WIDTH), lambda i: (i, 0)), pl.BlockSpec((CHUNK, LANES), lambda i: (i, 0)),
                  small, small],
        out_specs=[pl.BlockSpec((CHUNK, SSD_WIDTH), lambda i: (i, 0)),
                   pl.BlockSpec((None, nstate, SSD_STATE), lambda i: (i, 0, 0))],
        out_shape=[jax.ShapeDtypeStruct((s, SSD_WIDTH), _F32), jax.ShapeDtypeStruct((nc, nstate, SSD_STATE), _F32)],
        scratch_shapes=[pltpu.VMEM((nstate, SSD_STATE), _F32)],
        compiler_params=_params(("arbitrary",)),
    )(xbc, dtraw, bias, alog)


def _ssd_bwd(xbc, dtraw, states, bias, alog, dy, dxs_extra, *, name):
    s = xbc.shape[0]
    nc = s // CHUNK
    nstate = SSD_HEADS * HEAD_DIM

    def body(x_ref, dt_ref, st_ref, b_ref, a_ref, dy_ref, ex_ref, dx_ref, ddt_ref, db_ref, da_ref, dstate_ref):
        @pl.when(pl.program_id(0) == 0)
        def _():
            dstate_ref[...] = jnp.zeros_like(dstate_ref)
            db_ref[...] = jnp.zeros_like(db_ref)
            da_ref[...] = jnp.zeros_like(da_ref)

        _, vjp = jax.vjp(_ssd_chunk, x_ref[...], dt_ref[...], st_ref[...], b_ref[...], a_ref[...])
        dx, ddt, dprev, db, da = vjp((dy_ref[...], dstate_ref[...]))
        dx_ref[:, :SSD_WIDTH] = dx[:, :SSD_WIDTH] + ex_ref[...]
        dx_ref[:, SSD_WIDTH:] = dx[:, SSD_WIDTH:]
        ddt_ref[...] = ddt
        db_ref[...] += db
        da_ref[...] += da
        dstate_ref[...] = dprev

    rev = lambda i: (nc - 1 - i, 0)
    small = pl.BlockSpec((1, LANES), lambda i: (0, 0))
    return pl.pallas_call(
        body, name=name, grid=(nc,),
        in_specs=[pl.BlockSpec((CHUNK, XBC_WIDTH), rev), pl.BlockSpec((CHUNK, LANES), rev),
                  pl.BlockSpec((None, nstate, SSD_STATE), lambda i: (nc - 1 - i, 0, 0)), small, small,
                  pl.BlockSpec((CHUNK, SSD_WIDTH), rev), pl.BlockSpec((CHUNK, SSD_WIDTH), rev)],
        out_specs=[pl.BlockSpec((CHUNK, XBC_WIDTH), rev), pl.BlockSpec((CHUNK, LANES), rev), small, small],
        out_shape=[jax.ShapeDtypeStruct((s, XBC_WIDTH), _F32), jax.ShapeDtypeStruct((s, LANES), _F32),
                   jax.ShapeDtypeStruct((1, LANES), _F32), jax.ShapeDtypeStruct((1, LANES), _F32)],
        scratch_shapes=[pltpu.VMEM((nstate, SSD_STATE), _F32)],
        compiler_params=_params(("arbitrary",)),
    )(xbc, dtraw, states, bias, alog, dy, dxs_extra)


def _pad_cols(a, width):
    return jnp.pad(a, ((0, 0), (0, width - a.shape[1])))


def _pad_rows(a, rows):
    return jnp.pad(a, ((0, rows - a.shape[0]), (0, 0)))


def _tie(a, token):
    return a + token[0:1, 0:1].astype(a.dtype)


def _local_step(x, mem, target, w, fetch, emit):
    bf = _MXU
    w_in = fetch('in', None)['w_in']
    z_end, xbc_end, dt_end = SSD_WIDTH, SSD_WIDTH + XBC_WIDTH, SSD_WIDTH + XBC_WIDTH + SSD_HEADS
    w_z, w_xbc = w_in[:, :z_end], w_in[:, z_end:xbc_end]
    w_dt = _pad_cols(w_in[:, xbc_end:dt_end], LANES)
    w_a, w_g = w_in[:, dt_end:dt_end + CF_WIDTH], w_in[:, dt_end + CF_WIDTH:]
    dt_bias = _pad_cols(w['ssd_dt_bias'], LANES)
    a_log = _pad_cols(w['ssd_A_log'], LANES)
    d_exp = jnp.repeat(w['ssd_D'], HEAD_DIM, axis=1)
    g_final = w['norm_final_g'].reshape(1, D_MODEL)
    d = D_MODEL

    h = _row_fwd(_f_rms, [x], [w['norm_mix_g']], [(d, bf)], name="f_norm_mix")
    z = _mm(h, w_z, name="f_in_z")
    xbc = _mm(h, w_xbc, name="f_in_xbc")
    dtr = _mm(h, w_dt, name="f_in_dt")
    ga = _mm(h, w_a, name="f_in_a")
    gg = _mm(h, w_g, name="f_in_g")
    wc = fetch('conv', xbc)
    ssd_w = _pad_rows(wc['ssd_conv_w'], SUBLANES)
    cf_w = _pad_rows(wc['cf_conv_w'], 32)
    xbc_c = _conv_fwd(xbc, ssd_w, w['ssd_conv_b'], SSD_CONV, name="f_ssd_conv")
    xbc_a = _row_fwd(_silu, [xbc_c], [], [(XBC_WIDTH, _F32)], name="f_ssd_silu")
    y_ssd, states = _ssd_fwd(xbc_a, dtr, dt_bias, a_log, name="f_ssd")
    xs_win = (xbc_a, SSD_WIDTH, 0)
    y_n = _row_fwd(_f_gate, [y_ssd, xs_win, z], [d_exp, w['ssd_norm_g']], [(d, bf)], name="f_ssd_gate")
    u_pre = _row_fwd(_f_glu, [ga, gg], [], [(CF_WIDTH, _F32)], name="f_glu")
    u_c = _conv_fwd(u_pre, cf_w, w['cf_conv_b'], CF_CONV, name="f_cf_conv")
    u = _row_fwd(_f_ln, [u_c], [w['cf_ln_g'], w['cf_ln_b']], [(d, bf)], name="f_cf_ln")
    wm = fetch('mid', y_n)
    w_out_y, w_out_u = wm['w_out'][:SSD_WIDTH], wm['w_out'][SSD_WIDTH:]
    x1 = _mm(y_n, w_out_y, add=x, name="f_out_y")
    x1 = _mm(u, w_out_u, add=x1, name="f_out_u")
    hq = _row_fwd(_f_rms, [x1], [w['norm_xattn_g']], [(d, bf)], name="f_norm_xattn")
    q = _mm(hq, wm['w_q'], name="f_q")
    memn = _row_fwd(_f_rms, [mem], [w['norm_mem_g']], [(d, bf)], name="f_norm_mem")
    kv = _mm(memn, wm['w_kv'], name="f_kv")
    k_mat, v_mat = kv[:, :d], kv[:, d:]
    o = _row_fwd(_f_att, [q], [k_mat, v_mat], [(d, bf)], name="f_att")
    x2 = _mm(o, wm['w_o'], add=x1, name="f_o")
    hf = _row_fwd(_f_rms, [x2], [w['norm_ffn_g']], [(d, bf)], name="f_norm_ffn")
    wf = fetch('ffn', hf)
    gate = _mm(hf, wf['w_gate'], out_dtype=bf, name="f_gate")
    up = _mm(hf, wf['w_up'], out_dtype=bf, name="f_up")
    act = _row_fwd(_f_swiglu, [gate, up], [], [(D_FF, bf)], name="f_swiglu", ts=128)
    x3 = _mm(act, wf['w_down'], add=x2, name="f_down")

    dx3, dg_final, loss = _loss_bwd(x3, target, g_final, name="b_loss")
    g = {'norm_final_g': dg_final.reshape(d)}

    dact = _mm(dx3, wf['w_down'], tb=True, out_dtype=bf, name="b_down_x")
    dw_down = _mm(act, dx3, ta=True, name="b_down_w")
    (dgate, dup), _ = _row_bwd(_f_swiglu, [gate, up], [], [dact], row_dtypes=[bf, bf], name="b_swiglu", ts=128)
    sent = emit({'w_down': dw_down, 'w_gate': _mm(hf, dgate, ta=True, name="b_gate_w"),
                 'w_up': _mm(hf, dup, ta=True, name="b_up_w")})
    dhf = _mm(dgate, wf['w_gate'], tb=True, name="b_gate_x")
    dhf = _mm(dup, wf['w_up'], tb=True, add=dhf, name="b_up_x")
    (dx2,), (g['norm_ffn_g'],) = _row_bwd(_f_rms, [x2], [_tie(w['norm_ffn_g'], sent)], [dhf], adds={0: dx3}, name="b_norm_ffn")

    do = _mm(dx2, wm['w_o'], tb=True, name="b_o_x")
    dw_o = _mm(o, dx2, ta=True, name="b_o_w")
    (dq,), (dk, dv) = _row_bwd(_f_att, [q], [k_mat, v_mat], [do], row_dtypes=[bf], name="b_att")
    dw_q = _mm(hq, dq, ta=True, name="b_q_w")
    dhq = _mm(dq, wm['w_q'], tb=True, name="b_q_x")
    (dx1,), (g['norm_xattn_g'],) = _row_bwd(_f_rms, [x1], [w['norm_xattn_g']], [dhq], adds={0: dx2}, name="b_norm_xattn")
    dkv = jnp.concatenate([dk, dv], axis=1)
    sent = emit({'w_o': dw_o, 'w_q': dw_q, 'w_kv': _mm(memn, dkv, ta=True, name="b_kv_w")})
    dmemn = _mm(dkv, wm['w_kv'], tb=True, name="b_kv_x")
    _, (g['norm_mem_g'],) = _row_bwd(_f_rms, [mem], [w['norm_mem_g']], [dmemn], need=[False], name="b_norm_mem")

    dyn = _mm(dx1, w_out_y, tb=True, name="b_out_y_x")
    du = _mm(dx1, w_out_u, tb=True, name="b_out_u_x")
    (du_c,), (g['cf_ln_g'], g['cf_ln_b']) = _row_bwd(_f_ln, [u_c], [_tie(w['cf_ln_g'], sent), w['cf_ln_b']], [du], name="b_cf_ln")
    sent = emit({'w_out': jnp.concatenate([_mm(y_n, dx1, ta=True, name="b_out_y_w"), _mm(u, dx1, ta=True, name="b_out_u_w")], axis=0)})
    du_pre, dcf_w, g['cf_conv_b'] = _conv_bwd(u_pre, cf_w, du_c, CF_CONV, name="b_cf_conv")
    g['cf_conv_w'] = dcf_w[:CF_CONV]
    (dga, dgg), _ = _row_bwd(_f_glu, [ga, gg], [], [du_pre], row_dtypes=[bf, bf], name="b_glu")
    (dy_ssd, dxs, dz), (dd_exp, g['ssd_norm_g']) = _row_bwd(
        _f_gate, [y_ssd, xs_win, z], [d_exp, _tie(w['ssd_norm_g'], sent)], [dyn], row_dtypes=[_F32, _F32, bf], name="b_ssd_gate")
    g['ssd_D'] = jnp.sum(dd_exp.reshape(SSD_HEADS, HEAD_DIM), axis=1).reshape(1, SSD_HEADS)
    dxbc_a, ddtr, ddt_bias, da_log = _ssd_bwd(xbc_a, dtr, states, dt_bias, a_log, dy_ssd, dxs, name="b_ssd")
    g['ssd_dt_bias'] = ddt_bias[:, :SSD_HEADS]
    g['ssd_A_log'] = da_log[:, :SSD_HEADS]
    (dxbc_c,), _ = _row_bwd(_silu, [xbc_c], [], [dxbc_a], name="b_ssd_silu")
    dxbc, dssd_w, g['ssd_conv_b'] = _conv_bwd(xbc, ssd_w, dxbc_c, SSD_CONV, name="b_ssd_conv")
    g['ssd_conv_w'] = dssd_w[:SSD_CONV]

    sent = emit({'w_in': jnp.concatenate([
        _mm(h, dz, ta=True, name="b_in_z_w"), _mm(h, dxbc, ta=True, name="b_in_xbc_w"),
        _mm(h, ddtr, ta=True, name="b_in_dt_w")[:, :SSD_HEADS],
        _mm(h, dga, ta=True, name="b_in_a_w"), _mm(h, dgg, ta=True, name="b_in_g_w")], axis=1)})
    dh = _mm(dz, _tie(w_z, sent), tb=True, name="b_in_z_x")
    dh = _mm(dxbc, w_xbc, tb=True, add=dh, name="b_in_xbc_x")
    dh = _mm(ddtr, w_dt, tb=True, add=dh, name="b_in_dt_x")
    dh = _mm(dga, w_a, tb=True, add=dh, name="b_in_a_x")
    dh = _mm(dgg, w_g, tb=True, add=dh, name="b_in_g_x")
    (dx,), (g['norm_mix_g'],) = _row_bwd(_f_rms, [x], [w['norm_mix_g']], [dh], adds={0: dx1}, name="b_norm_mix")
    return loss, dx, g


_ANY = pl.BlockSpec(memory_space=pl.ANY)


def _place():
    x, y, c = lax.axis_index("x"), lax.axis_index("y"), lax.axis_index("c")
    return x, y, c


def _all_gather(arrs, *, name):
    n = len(arrs)

    def body(*refs):
        ins, outs = refs[:n], refs[n:2 * n]
        send_sems, recv_sems, local_sems = refs[2 * n:]
        x, y, c = _place()
        me, sibling = (x, y, c), (x, y, 1 - c)
        chips = [(1 - x, y), (x, 1 - y), (1 - x, 1 - y)]

        def slot(a, dev):
            return outs[a].at[4 * dev[0] + 2 * dev[1] + dev[2]]

        def copy(a, k, block, to, src=None):
            return pltpu.make_async_remote_copy(
                src_ref=slot(a, block) if src is None else src, dst_ref=slot(a, block),
                send_sem=send_sems.at[a, k], recv_sem=recv_sems.at[a, k], device_id=to, device_id_type=MESH)

        mine = [pltpu.make_async_copy(ins[a], slot(a, me), local_sems.at[a]) for a in range(n)]
        for cp in mine:
            cp.start()
        first = []
        for a in range(n):
            first.append(copy(a, 0, me, sibling, src=ins[a]))
            first += [copy(a, 1 + j, me, (*chip, c), src=ins[a]) for j, chip in enumerate(chips)]
        for cp in first:
            cp.start()
        passed = []
        for a in range(n):
            for j, chip in enumerate(chips):
                copy(a, 1 + j, (*chip, c), me).wait_recv()
                fwd = copy(a, 4 + j, (*chip, c), sibling)
                fwd.start()
                passed.append(fwd)
        for a in range(n):
            copy(a, 0, sibling, me).wait_recv()
            for j, chip in enumerate(chips):
                copy(a, 4 + j, (*chip, 1 - c), me).wait_recv()
        for cp in first + passed:
            cp.wait_send()
        for cp in mine:
            cp.wait()

    return pl.pallas_call(
        body, name=name, in_specs=[_ANY] * n, out_specs=[_ANY] * n,
        out_shape=[jax.ShapeDtypeStruct((N_DEV,) + a.shape, a.dtype) for a in arrs],
        scratch_shapes=[pltpu.SemaphoreType.DMA((n, 7)), pltpu.SemaphoreType.DMA((n, 7)), pltpu.SemaphoreType.DMA((n,))],
    )(*arrs)


_HBM = pl.BlockSpec(memory_space=pltpu.HBM)
_SEM = pl.BlockSpec(memory_space=pltpu.SEMAPHORE)
_EFFECT = pltpu.SideEffectType.DATAFLOW_SIDE_EFFECTING
_FLIPS = [(dx, dy, dc) for dx in (0, 1) for dy in (0, 1) for dc in (0, 1)][1:]


def _peer(flip, x, y, c):
    return (1 - x if flip[0] else x, 1 - y if flip[1] else y, 1 - c if flip[2] else c)


def _send_start(srcs, blocked, *, after=None, name):
    n = len(srcs)
    lands = [jax.ShapeDtypeStruct(s.shape if blocked else (N_DEV,) + s.shape, s.dtype) for s in srcs]
    n_in = 2 * n + (after is not None)

    def body(*refs):
        src_refs, land_refs = refs[:n], refs[n:2 * n]
        send_sems, recv_sems = refs[n_in], refs[n_in + 1]
        token = refs[-1]
        x, y, c = _place()
        me = 4 * x + 2 * y + c
        for a in range(n):
            for k, flip in enumerate(_FLIPS):
                p = _peer(flip, x, y, c)
                src = src_refs[a].at[4 * p[0] + 2 * p[1] + p[2]] if blocked else src_refs[a]
                pltpu.make_async_remote_copy(
                    src_ref=src, dst_ref=land_refs[a].at[me], send_sem=send_sems.at[7 * a + k], recv_sem=recv_sems.at[7 * a + k],
                    device_id=p, device_id_type=MESH).start()
        token[...] = jnp.zeros_like(token)

    res = pl.pallas_call(
        body, name=name,
        out_shape=(pltpu.SemaphoreType.DMA((7 * n,)), pltpu.SemaphoreType.DMA((7 * n,)),
                   *[pltpu.HBM(s.shape, s.dtype) for s in srcs], *[pltpu.HBM(l.shape, l.dtype) for l in lands],
                   jax.ShapeDtypeStruct((SUBLANES, LANES), _F32)),
        in_specs=[_HBM] * (2 * n) + [_ANY] * (after is not None),
        out_specs=(_SEM, _SEM, *[_HBM] * (2 * n), pl.BlockSpec(memory_space=pltpu.VMEM)),
        input_output_aliases={i: 2 + i for i in range(2 * n)},
        compiler_params=pltpu.CompilerParams(has_side_effects=_EFFECT),
    )(*[pltpu.with_memory_space_constraint(s, pltpu.HBM) for s in srcs],
      *[pltpu.with_memory_space_constraint(lax.empty(l.shape, l.dtype), pltpu.HBM) for l in lands],
      *([after] if after is not None else []))
    return res[0], res[1], list(res[2:2 + n]), list(res[2 + n:2 + 2 * n]), res[-1]


def _send_wait(handles, after, blocked, *, name):
    send_sems, recv_sems, srcs, lands, _ = handles
    n = len(srcs)

    def body(*refs):
        src_refs, land_refs = refs[:n], refs[n:2 * n]
        send_sems, recv_sems = refs[2 * n], refs[2 * n + 1]
        x, y, c = _place()
        for a in range(n):
            for k, flip in enumerate(_FLIPS):
                p = _peer(flip, x, y, c)
                pid = 4 * p[0] + 2 * p[1] + p[2]
                cp = pltpu.make_async_remote_copy(
                    src_ref=src_refs[a].at[pid] if blocked else src_refs[a], dst_ref=land_refs[a].at[pid],
                    send_sem=send_sems.at[7 * a + k], recv_sem=recv_sems.at[7 * a + k], device_id=p, device_id_type=MESH)
                cp.wait_send()
                cp.wait_recv()

    res = pl.pallas_call(
        body, name=name,
        out_shape=tuple(pltpu.HBM(s.shape, s.dtype) for s in srcs + lands),
        in_specs=[_HBM] * (2 * n) + [_SEM, _SEM, _ANY], out_specs=tuple([_HBM] * (2 * n)),
        input_output_aliases={i: i for i in range(2 * n)},
        compiler_params=pltpu.CompilerParams(has_side_effects=_EFFECT),
    )(*srcs, *lands, send_sems, recv_sems, after)
    return list(res[:n]), list(res[n:])


def _adamw(parts, w, m, v, *, own=None, me=None, name):
    p, r, c = parts.shape
    tr = _pick(r, (256, 176, 128, 64, 32, 16, 8))
    if own is not None:
        return _adamw_own(parts, own, me, w, m, v, tr, name=name)

    def body(p_ref, w_ref, m_ref, v_ref, g_ref, d_ref, nm_ref, nv_ref):
        g = p_ref[0].astype(_F32)
        for i in range(1, p):
            g = g + p_ref[i].astype(_F32)
        _adamw_math(g, w_ref, m_ref, v_ref, g_ref, d_ref, nm_ref, nv_ref)

    blk = pl.BlockSpec((tr, c), lambda i: (i, 0))
    return pl.pallas_call(
        body, name=name, grid=(r // tr,),
        in_specs=[pl.BlockSpec((p, tr, c), lambda i: (0, i, 0)), blk, blk, blk], out_specs=[blk] * 4,
        out_shape=[jax.ShapeDtypeStruct((r, c), _F32)] * 4,
        compiler_params=_params(("parallel",)),
    )(parts, w, m, v)


def _adamw_math(g, w_ref, m_ref, v_ref, g_ref, d_ref, nm_ref, nv_ref):
    wv = w_ref[...]
    mn = ADAM_B1 * m_ref[...] + (1.0 - ADAM_B1) * g
    vn = ADAM_B2 * v_ref[...] + (1.0 - ADAM_B2) * jnp.square(g)
    m_hat = mn / (1.0 - ADAM_B1 ** ADAM_STEP)
    v_hat = vn / (1.0 - ADAM_B2 ** ADAM_STEP)
    g_ref[...] = g
    d_ref[...] = -ADAM_LR * (m_hat / (jnp.sqrt(v_hat) + ADAM_EPS) + ADAM_WD * wv)
    nm_ref[...] = mn
    nv_ref[...] = vn


def _adamw_own(parts, own, me, w, m, v, tr, *, name):
    p, r, c = parts.shape

    def body(me_ref, p_ref, own_ref, w_ref, m_ref, v_ref, g_ref, d_ref, nm_ref, nv_ref):
        mine = own_ref[...].astype(_F32)
        g = jnp.where(me_ref[0] == 0, mine, p_ref[0].astype(_F32))
        for i in range(1, p):
            g = g + jnp.where(me_ref[0] == i, mine, p_ref[i].astype(_F32))
        _adamw_math(g, w_ref, m_ref, v_ref, g_ref, d_ref, nm_ref, nv_ref)

    blk = pl.BlockSpec((tr, c), lambda i, me_ref: (i, 0))
    grid_spec = pltpu.PrefetchScalarGridSpec(
        num_scalar_prefetch=1, grid=(r // tr,),
        in_specs=[pl.BlockSpec((p, tr, c), lambda i, me_ref: (0, i, 0)),
                  pl.BlockSpec((None, tr, c), lambda i, me_ref: (me_ref[0], i, 0)), blk, blk, blk],
        out_specs=[blk] * 4)
    return pl.pallas_call(
        body, name=name, grid_spec=grid_spec, out_shape=[jax.ShapeDtypeStruct((r, c), _F32)] * 4,
        compiler_params=_params(("parallel",)),
    )(me.reshape(1).astype(jnp.int32), parts, own, w, m, v)


def _sum_parts(parts, *, name):
    p, r, c = parts.shape

    def body(p_ref, o_ref):
        g = p_ref[0].astype(_F32)
        for i in range(1, p):
            g = g + p_ref[i].astype(_F32)
        o_ref[...] = g

    return pl.pallas_call(body, name=name, out_shape=jax.ShapeDtypeStruct((r, c), _F32))(parts)


def _pack(vals, rows):
    flat = jnp.concatenate([v.reshape(-1) for v in vals])
    return jnp.pad(flat, (0, rows * LANES - flat.shape[0])).reshape(rows, LANES)


def _unpack(packed, shapes):
    flat = packed.reshape(-1)
    out, pos = [], 0
    for shp in shapes:
        size = math.prod(shp)
        out.append(flat[pos:pos + size].reshape(shp))
        pos += size
    return out


def _pack_rows(shapes):
    total = sum(math.prod(s) for s in shapes)
    return -(-total // (LANES * SUBLANES)) * SUBLANES


def kernel(x, mem, norm_mix_g, w_in, ssd_conv_w, ssd_conv_b, ssd_dt_bias, ssd_A_log, ssd_D, ssd_norm_g, cf_conv_w, cf_conv_b, cf_ln_g, cf_ln_b, w_out, norm_xattn_g, norm_mem_g, w_q, w_kv, w_o, norm_ffn_g, w_gate, w_up, w_down, norm_final_g, loss_target, m_norm_mix_g, m_w_in, m_ssd_conv_w, m_ssd_conv_b, m_ssd_dt_bias, m_ssd_A_log, m_ssd_D, m_ssd_norm_g, m_cf_conv_w, m_cf_conv_b, m_cf_ln_g, m_cf_ln_b, m_w_out, m_norm_xattn_g, m_norm_mem_g, m_w_q, m_w_kv, m_w_o, m_norm_ffn_g, m_w_gate, m_w_up, m_w_down, m_norm_final_g, v_norm_mix_g, v_w_in, v_ssd_conv_w, v_ssd_conv_b, v_ssd_dt_bias, v_ssd_A_log, v_ssd_D, v_ssd_norm_g, v_cf_conv_w, v_cf_conv_b, v_cf_ln_g, v_cf_ln_b, v_w_out, v_norm_xattn_g, v_norm_mem_g, v_w_q, v_w_kv, v_w_o, v_norm_ffn_g, v_w_gate, v_w_up, v_w_down, v_norm_final_g):
    args = dict(locals())
    wts = {n: args[n] for n in WEIGHT_NAMES}
    mom = {n: args["m_" + n] for n in WEIGHT_NAMES}
    var = {n: args["v_" + n] for n in WEIGHT_NAMES}
    me = 4 * lax.axis_index("x") + 2 * lax.axis_index("y") + lax.axis_index("c")

    groups = {'in': ['w_in'], 'conv': ['ssd_conv_w', 'cf_conv_w'], 'mid': ['w_out', 'w_q', 'w_kv', 'w_o'],
              'ffn': ['w_gate', 'w_up', 'w_down']}
    gathers, started = {}, None
    for grp, names in groups.items():
        shards = [wts[n][0] if grp == 'conv' else wts[n][0].astype(_MXU) for n in names]
        gathers[grp] = _send_start(shards, False, after=started, name="gather_%s_start" % grp)
        started = gathers[grp][4]

    def fetch(grp, after):
        srcs, lands = _send_wait(gathers[grp], started if after is None else after, False, name="gather_%s_wait" % grp)
        out = {}
        for n, own, gth in zip(groups[grp], srcs, lands):
            gth = lax.dynamic_update_slice_in_dim(gth, own[None], me, axis=0)
            if n in COL_SHARDED or grp == 'conv':
                out[n] = jnp.transpose(gth, (1, 0, 2)).reshape(gth.shape[1], N_DEV * gth.shape[2])
            else:
                out[n] = gth.reshape(N_DEV * gth.shape[1], gth.shape[2])
        return out

    exchanges = []

    def emit(grads):
        blocks = []
        for n, gw in grads.items():
            if n in COL_SHARDED:
                gw = jnp.transpose(gw.reshape(gw.shape[0], N_DEV, gw.shape[1] // N_DEV), (1, 0, 2))
            else:
                gw = gw.reshape(N_DEV, gw.shape[0] // N_DEV, gw.shape[1])
            blocks.append(gw.astype(jnp.bfloat16))
        first = next(iter(grads))
        exchanges.append((list(grads), _send_start(blocks, True, name="exchange_%s_start" % first), first))
        return exchanges[-1][1][4]

    full = {n: wts[n] for n in WEIGHT_NAMES if n not in BIG and n not in groups['conv']}
    full['norm_mix_g'] = _tie(norm_mix_g, started)

    loss_blk, grad_x, g = _local_step(x[0], mem[0], loss_target[0], full, fetch, emit)
    loss = lax.psum(loss_blk[0, 0], ("x", "y", "c"))

    out_g, out_d, out_m, out_v = {}, {}, {}, {}
    for names, handles, first in exchanges:
        srcs, lands = _send_wait(handles, grad_x, True, name="exchange_%s_wait" % first)
        for n, own, parts in zip(names, srcs, lands):
            res = _adamw(parts, wts[n][0], mom[n][0], var[n][0], own=own, me=me, name="adamw_" + n)
            out_g[n], out_d[n], out_m[n], out_v[n] = [r[None] for r in res]

    small = [n for n in WEIGHT_NAMES if n not in BIG]
    small_shapes = [g[n].shape for n in small]
    rows = _pack_rows(small_shapes)
    (small_parts,) = _all_gather([_pack([g[n] for n in small], rows)], name="gather_small_grads")
    small_sum = dict(zip(small, _unpack(_sum_parts(small_parts, name="sum_small_grads"), small_shapes)))
    small_sum['ssd_conv_w'] = lax.dynamic_slice_in_dim(small_sum['ssd_conv_w'], me * (XBC_WIDTH // N_DEV), XBC_WIDTH // N_DEV, axis=1)[None]
    small_sum['cf_conv_w'] = lax.dynamic_slice_in_dim(small_sum['cf_conv_w'], me * (CF_WIDTH // N_DEV), CF_WIDTH // N_DEV, axis=1)[None]
    shard_shapes = [wts[n].shape for n in small]
    rows2 = _pack_rows(shard_shapes)
    res = _adamw(_pack([small_sum[n] for n in small], rows2)[None], _pack([wts[n] for n in small], rows2),
                 _pack([mom[n] for n in small], rows2), _pack([var[n] for n in small], rows2), name="adamw_small")
    for dst, packed in zip((out_g, out_d, out_m, out_v), res):
        dst.update(zip(small, _unpack(packed, shard_shapes)))

    return (loss, grad_x[None], *[out_g[n] for n in WEIGHT_NAMES], *[out_d[n] for n in WEIGHT_NAMES],
            *[out_m[n] for n in WEIGHT_NAMES], *[out_v[n] for n in WEIGHT_NAMES])
```

```python
import functools
import math

import jax
import jax.numpy as jnp
from jax import lax
from jax.experimental import pallas as pl
from jax.experimental.pallas import tpu as pltpu

_F32 = jnp.float32
_MXU = jnp.bfloat16
_PREC = None
_VMEM_LIMIT = 56 * 1024 * 1024

D_MODEL = 1024
HEAD_DIM = 64
SSD_HEADS = 16
SSD_WIDTH = 1024
SSD_STATE = 128
SSD_CONV = 4
CHUNK = 128
XBC_WIDTH = 1536
CF_WIDTH = 1024
CF_CONV = 31
X_HEADS = 4
X_HEAD_DIM = 256
D_FF = 2816
EPS = 1e-6
N_DEV = 8
LANES = 128
SUBLANES = 8

ADAM_LR = 0.001
ADAM_B1 = 0.9
ADAM_B2 = 0.999
ADAM_EPS = 1e-08
ADAM_WD = 0.01
ADAM_STEP = 10

MESH = pl.DeviceIdType.MESH
WEIGHT_NAMES = ['norm_mix_g', 'w_in', 'ssd_conv_w', 'ssd_conv_b', 'ssd_dt_bias', 'ssd_A_log', 'ssd_D', 'ssd_norm_g',
                'cf_conv_w', 'cf_conv_b', 'cf_ln_g', 'cf_ln_b', 'w_out', 'norm_xattn_g', 'norm_mem_g', 'w_q', 'w_kv',
                'w_o', 'norm_ffn_g', 'w_gate', 'w_up', 'w_down', 'norm_final_g']
BIG = ['w_in', 'w_out', 'w_q', 'w_kv', 'w_o', 'w_gate', 'w_up', 'w_down']
COL_SHARDED = ('w_in', 'w_kv', 'w_gate', 'w_up')


def _params(sem=None):
    return pltpu.CompilerParams(dimension_semantics=sem, vmem_limit_bytes=_VMEM_LIMIT)


def _pick(n, cands):
    for c in cands:
        if n % c == 0:
            return c
    return n


def _mm(a, b, *, ta=False, tb=False, add=None, out_dtype=_F32, name):
    (kdim, m) = a.shape if ta else a.shape[::-1]
    (n, k2) = b.shape if tb else b.shape[::-1]
    assert kdim == k2, (a.shape, b.shape, ta, tb)
    if ta:
        tm = m if m <= 1024 else _pick(m, (1408, 1024, 512, 256, 128))
        tn = n if n <= 1536 else _pick(n, (1408, 1024, 512, 256, 128))
        tk = _pick(kdim, (1024, 512, 256, 128))
    else:
        tm = _pick(m, (512, 256, 128))
        tn = n if n <= 2816 else _pick(n, (1408, 1024, 512, 256, 128))
        tk = kdim if kdim <= 2816 else _pick(kdim, (1408, 1024, 512, 256, 128))
    nk = kdim // tk
    dn = (((0 if ta else 1,), (1 if tb else 0,)), ((), ()))

    def body(*refs):
        a_ref, b_ref = refs[0], refs[1]
        add_ref = refs[2] if add is not None else None
        o_ref = refs[3 if add is not None else 2]
        acc_ref = refs[-1]
        k = pl.program_id(2)
        prod = lax.dot_general(a_ref[...].astype(_MXU), b_ref[...].astype(_MXU), dn,
                               preferred_element_type=_F32, precision=_PREC)

        def finish(r):
            if add_ref is not None:
                r = r + add_ref[...].astype(_F32)
            o_ref[...] = r.astype(o_ref.dtype)

        if nk == 1:
            finish(prod)
            return

        @pl.when(k == 0)
        def _():
            acc_ref[...] = prod

        @pl.when(jnp.logical_and(k > 0, k < nk - 1))
        def _():
            acc_ref[...] += prod

        @pl.when(k == nk - 1)
        def _():
            finish(acc_ref[...] + prod)

    a_spec = pl.BlockSpec((tk, tm), lambda i, j, k: (k, i)) if ta else pl.BlockSpec((tm, tk), lambda i, j, k: (i, k))
    b_spec = pl.BlockSpec((tn, tk), lambda i, j, k: (j, k)) if tb else pl.BlockSpec((tk, tn), lambda i, j, k: (k, j))
    o_spec = pl.BlockSpec((tm, tn), lambda i, j, k: (i, j))
    ins, specs = [a, b], [a_spec, b_spec]
    if add is not None:
        ins.append(add)
        specs.append(o_spec)
    return pl.pallas_call(
        body, name=name, grid=(m // tm, n // tn, nk), in_specs=specs, out_specs=o_spec,
        out_shape=jax.ShapeDtypeStruct((m, n), out_dtype),
        scratch_shapes=[pltpu.VMEM((tm, tn), _F32)] if nk > 1 else [],
        compiler_params=_params(("parallel", "parallel", "arbitrary")),
    )(*ins)


def _row_spec(r, ts):
    if isinstance(r, tuple):
        arr, width, cblk = r
        return arr, pl.BlockSpec((ts, width), lambda i, cblk=cblk: (i, cblk))
    return r, pl.BlockSpec((ts, r.shape[1]), lambda i: (i, 0))


def _tup(v):
    return tuple(v) if isinstance(v, (tuple, list)) else (v,)


def _row_fwd(f, rows, params, outs, *, name, ts=256):
    arrs, specs = zip(*[_row_spec(r, ts) for r in rows])
    s = arrs[0].shape[0]
    ts = min(ts, s)
    n_r, n_p = len(rows), len(params)

    def body(*refs):
        rv = [r[...].astype(_F32) for r in refs[:n_r]]
        pv = [p[...] for p in refs[n_r:n_r + n_p]]
        res = _tup(f(*rv, *pv))
        for o_ref, v in zip(refs[n_r + n_p:], res):
            o_ref[...] = v.astype(o_ref.dtype)

    res = pl.pallas_call(
        body, name=name, grid=(s // ts,),
        in_specs=list(specs) + [pl.BlockSpec(p.shape, lambda i: (0, 0)) for p in params],
        out_specs=[pl.BlockSpec((ts, w), lambda i: (i, 0)) for w, _ in outs],
        out_shape=[jax.ShapeDtypeStruct((s, w), dt) for w, dt in outs],
        compiler_params=_params(("parallel",)),
    )(*arrs, *params)
    return res[0] if len(outs) == 1 else res


def _row_bwd(f, rows, params, cts, *, need=None, adds=None, row_dtypes=None, name, ts=256):
    arrs, specs = zip(*[_row_spec(r, ts) for r in rows])
    s = arrs[0].shape[0]
    ts = min(ts, s)
    n_r, n_p, n_c = len(rows), len(params), len(cts)
    need = [True] * n_r if need is None else need
    adds = {} if adds is None else adds
    add_keys = sorted(adds)
    row_dtypes = [_F32] * n_r if row_dtypes is None else row_dtypes
    needed = [j for j in range(n_r) if need[j]]
    widths = [specs[j].block_shape[1] for j in range(n_r)]

    def body(*refs):
        pos = 0
        r_refs = refs[pos:pos + n_r]; pos += n_r
        p_refs = refs[pos:pos + n_p]; pos += n_p
        c_refs = refs[pos:pos + n_c]; pos += n_c
        a_refs = refs[pos:pos + len(add_keys)]; pos += len(add_keys)
        dr_refs = refs[pos:pos + len(needed)]; pos += len(needed)
        dp_refs = refs[pos:pos + n_p]
        rv = [r[...].astype(_F32) for r in r_refs]
        pv = [p[...] for p in p_refs]
        _, vjp = jax.vjp(lambda *a: _tup(f(*a)), *rv, *pv)
        g = vjp(tuple(c[...].astype(_F32) for c in c_refs))
        for o_ref, j in zip(dr_refs, needed):
            v = g[j]
            if j in adds:
                v = v + a_refs[add_keys.index(j)][...].astype(_F32)
            o_ref[...] = v.astype(o_ref.dtype)
        if n_p:
            @pl.when(pl.program_id(0) == 0)
            def _():
                for dp in dp_refs:
                    dp[...] = jnp.zeros_like(dp)
            for dp, v in zip(dp_refs, g[n_r:]):
                dp[...] += v

    ct_specs = [pl.BlockSpec((ts, c.shape[1]), lambda i: (i, 0)) for c in cts]
    add_specs = [pl.BlockSpec((ts, adds[j].shape[1]), lambda i: (i, 0)) for j in add_keys]
    res = pl.pallas_call(
        body, name=name, grid=(s // ts,),
        in_specs=list(specs) + [pl.BlockSpec(p.shape, lambda i: (0, 0)) for p in params] + ct_specs + add_specs,
        out_specs=[pl.BlockSpec((ts, widths[j]), lambda i: (i, 0)) for j in needed]
        + [pl.BlockSpec(p.shape, lambda i: (0, 0)) for p in params],
        out_shape=[jax.ShapeDtypeStruct((s, widths[j]), row_dtypes[j]) for j in needed]
        + [jax.ShapeDtypeStruct(p.shape, _F32) for p in params],
        compiler_params=_params(("arbitrary",)),
    )(*arrs, *params, *cts, *[adds[j] for j in add_keys])
    return list(res[:len(needed)]), list(res[len(needed):])


_DN = {"nn": (((1,), (0,)), ((), ())), "nt": (((1,), (1,)), ((), ())), "tn": (((0,), (0,)), ((), ()))}


def _make_dot(passes):
    def raw(a, b, kind):
        dn = _DN[kind]
        if passes == 1 or _MXU == _F32:
            return lax.dot_general(a.astype(_MXU), b.astype(_MXU), dn, preferred_element_type=_F32, precision=_PREC)
        a_hi, b_hi = a.astype(_MXU), b.astype(_MXU)
        a_lo = (a - a_hi.astype(_F32)).astype(_MXU)
        b_lo = (b - b_hi.astype(_F32)).astype(_MXU)
        out = lax.dot_general(a_hi, b_hi, dn, preferred_element_type=_F32)
        out = out + lax.dot_general(a_lo, b_hi, dn, preferred_element_type=_F32)
        return out + lax.dot_general(a_hi, b_lo, dn, preferred_element_type=_F32)

    @functools.partial(jax.custom_vjp, nondiff_argnums=(2,))
    def dot(a, b, kind):
        return raw(a, b, kind)

    def fwd(a, b, kind):
        return raw(a, b, kind), (a, b)

    def bwd(kind, res, ct):
        a, b = res
        if kind == "nn":
            return raw(ct, b, "nt"), raw(a, ct, "tn")
        if kind == "nt":
            return raw(ct, b, "nn"), raw(ct, a, "tn")
        return raw(b, ct, "nt"), raw(a, ct, "nn")

    dot.defvjp(fwd, bwd)
    return dot


_dot1 = _make_dot(1)
_dot3 = _make_dot(3)


def _sig(v):
    return 1.0 / (1.0 + jnp.exp(-v))


def _silu(v):
    return v * _sig(v)


def _f_rms(x, g):
    return x * lax.rsqrt(jnp.mean(x * x, axis=-1, keepdims=True) + EPS) * g


def _f_gate(y, xs, z, dexp, g):
    v = (y + dexp * xs) * _silu(z)
    half = SSD_WIDTH // 2
    parts = []
    for grp in range(2):
        vg = v[:, grp * half:(grp + 1) * half]
        parts.append(vg * lax.rsqrt(jnp.mean(vg * vg, axis=-1, keepdims=True) + EPS) * g[:, grp * half:(grp + 1) * half])
    return jnp.concatenate(parts, axis=1)


def _f_ln(u, g, b):
    mu = jnp.mean(u, axis=-1, keepdims=True)
    var = jnp.mean(jnp.square(u - mu), axis=-1, keepdims=True)
    return _silu((u - mu) * lax.rsqrt(var + EPS) * g + b)


def _f_glu(a, g):
    return a * _sig(g)


def _f_swiglu(gate, up):
    return _silu(gate) * up


def _f_att(q, k, v):
    outs = []
    for h in range(X_HEADS):
        sl = slice(h * X_HEAD_DIM, (h + 1) * X_HEAD_DIM)
        s = _dot1(q[:, sl], k[:, sl], "nt") * (X_HEAD_DIM ** -0.5)
        s = s - lax.stop_gradient(jnp.max(s, axis=-1, keepdims=True))
        p = jnp.exp(s)
        p = p / jnp.sum(p, axis=-1, keepdims=True)
        outs.append(_dot1(p, v[:, sl], "nn"))
    return jnp.concatenate(outs, axis=1)


def _loss_bwd(x3, target, g, *, name, ts=256):
    s, d = x3.shape

    def f(x, t, gv):
        return 0.5 * jnp.sum(jnp.mean(jnp.square(_f_rms(x, gv) - t), axis=-1))

    def body(x_ref, t_ref, g_ref, dx_ref, dg_ref, l_ref):
        @pl.when(pl.program_id(0) == 0)
        def _():
            dg_ref[...] = jnp.zeros_like(dg_ref)
            l_ref[...] = jnp.zeros_like(l_ref)

        lv, (dx, dg) = jax.value_and_grad(f, argnums=(0, 2))(x_ref[...], t_ref[...], g_ref[...])
        dx_ref[...] = dx
        dg_ref[...] += dg
        l_ref[...] += lv

    row = pl.BlockSpec((ts, d), lambda i: (i, 0))
    return pl.pallas_call(
        body, name=name, grid=(s // ts,),
        in_specs=[row, row, pl.BlockSpec((1, d), lambda i: (0, 0))],
        out_specs=[row, pl.BlockSpec((1, d), lambda i: (0, 0)), pl.BlockSpec((SUBLANES, LANES), lambda i: (0, 0))],
        out_shape=[jax.ShapeDtypeStruct((s, d), _F32), jax.ShapeDtypeStruct((1, d), _F32),
                   jax.ShapeDtypeStruct((SUBLANES, LANES), _F32)],
        compiler_params=_params(("arbitrary",)),
    )(x3, target, g)


_CONV_PAD = 32
_CONV_ROWS = 128
_CONV_CB = 128


def _conv_taps(k_taps):
    groups = {}
    for k in range(k_taps):
        j = k_taps - 1 - k
        groups.setdefault(j % SUBLANES, []).append((k, j))
    return groups


def _conv_fwd(x, w, b, k_taps, *, name):
    s, c = x.shape
    kp = w.shape[0]
    pad, rows, cb = _CONV_PAD, _CONV_ROWS, _CONV_CB
    groups = _conv_taps(k_taps)

    def body(x_ref, w_ref, b_ref, o_ref, xp_ref):
        xp_ref[0:pad, :] = jnp.zeros((pad, cb), _F32)
        xp_ref[pad:pad + s, :] = x_ref[...]
        wv = w_ref[...]
        bias = jnp.broadcast_to(b_ref[...], (rows, cb))

        def chunk(r, carry):
            base = pl.multiple_of(r * rows, rows)
            win = xp_ref[pl.ds(base, rows + pad), :]
            acc = bias
            for rot, taps in groups.items():
                rolled = win if rot == 0 else pltpu.roll(win, rot, 0)
                for k, j in taps:
                    off = pad - (j - rot)
                    acc = acc + rolled[off:off + rows, :] * wv[k:k + 1, :]
            o_ref[pl.ds(base, rows), :] = acc
            return carry

        lax.fori_loop(0, s // rows, chunk, 0)

    col = pl.BlockSpec((s, cb), lambda i: (0, i))
    return pl.pallas_call(
        body, name=name, grid=(c // cb,),
        in_specs=[col, pl.BlockSpec((kp, cb), lambda i: (0, i)), pl.BlockSpec((1, cb), lambda i: (0, i))],
        out_specs=col, out_shape=jax.ShapeDtypeStruct((s, c), _F32),
        scratch_shapes=[pltpu.VMEM((s + pad, cb), _F32)],
        compiler_params=_params(("parallel",)),
    )(x, w, b)


def _conv_bwd(x, w, dy, k_taps, *, name):
    s, c = x.shape
    kp = w.shape[0]
    pad, rows, cb = _CONV_PAD, _CONV_ROWS, _CONV_CB
    groups = _conv_taps(k_taps)
    win_rows = rows + pad

    def fold(v):
        acc = v[0:SUBLANES, :]
        for i in range(1, rows // SUBLANES):
            acc = acc + v[i * SUBLANES:(i + 1) * SUBLANES, :]
        return acc

    def body(x_ref, w_ref, dy_ref, dx_ref, dw_ref, db_ref, xp_ref, dyp_ref, acc_ref, dbacc_ref):
        xp_ref[0:pad, :] = jnp.zeros((pad, cb), _F32)
        xp_ref[pad:pad + s, :] = x_ref[...]
        dyp_ref[0:s, :] = dy_ref[...]
        dyp_ref[s:s + pad, :] = jnp.zeros((pad, cb), _F32)
        acc_ref[...] = jnp.zeros_like(acc_ref)
        dbacc_ref[...] = jnp.zeros_like(dbacc_ref)
        wv = w_ref[...]

        def chunk(r, carry):
            base = pl.multiple_of(r * rows, rows)
            xwin = xp_ref[pl.ds(base, win_rows), :]
            dwin = dyp_ref[pl.ds(base, win_rows), :]
            dyc = dwin[0:rows, :]
            dxacc = jnp.zeros((rows, cb), _F32)
            for rot, taps in groups.items():
                xr = xwin if rot == 0 else pltpu.roll(xwin, rot, 0)
                dr = dwin if rot == 0 else pltpu.roll(dwin, win_rows - rot, 0)
                for k, j in taps:
                    a8 = j - rot
                    dxacc = dxacc + dr[a8:a8 + rows, :] * wv[k:k + 1, :]
                    prod = dyc * xr[pad - a8:pad - a8 + rows, :]
                    acc_ref[k * SUBLANES:(k + 1) * SUBLANES, :] += fold(prod)
            dbacc_ref[...] += fold(dyc)
            dx_ref[pl.ds(base, rows), :] = dxacc
            return carry

        lax.fori_loop(0, s // rows, chunk, 0)
        dw_ref[...] = jnp.zeros_like(dw_ref)
        for k in range(k_taps):
            dw_ref[k:k + 1, :] = jnp.sum(acc_ref[k * SUBLANES:(k + 1) * SUBLANES, :], axis=0, keepdims=True)
        db_ref[...] = jnp.sum(dbacc_ref[...], axis=0, keepdims=True)

    col = pl.BlockSpec((s, cb), lambda i: (0, i))
    wspec = pl.BlockSpec((kp, cb), lambda i: (0, i))
    bspec = pl.BlockSpec((1, cb), lambda i: (0, i))
    return pl.pallas_call(
        body, name=name, grid=(c // cb,),
        in_specs=[col, wspec, col], out_specs=[col, wspec, bspec],
        out_shape=[jax.ShapeDtypeStruct((s, c), _F32), jax.ShapeDtypeStruct((kp, c), _F32),
                   jax.ShapeDtypeStruct((1, c), _F32)],
        scratch_shapes=[pltpu.VMEM((s + pad, cb), _F32), pltpu.VMEM((s + pad, cb), _F32),
                        pltpu.VMEM((kp * SUBLANES, cb), _F32), pltpu.VMEM((SUBLANES, cb), _F32)],
        compiler_params=_params(("parallel",)),
    )(x, w, dy)


def _tri_sum(v, lower):
    l = v.shape[0]
    r, c = lax.broadcasted_iota(jnp.int32, (l, l), 0), lax.broadcasted_iota(jnp.int32, (l, l), 1)
    tri = ((r >= c) if lower else (r <= c)).astype(jnp.bfloat16)
    hi = v.astype(jnp.bfloat16)
    r1 = v - hi.astype(_F32)
    mid = r1.astype(jnp.bfloat16)
    lo = (r1 - mid.astype(_F32)).astype(jnp.bfloat16)
    out = jnp.zeros_like(v)
    for part in (hi, mid, lo):
        out = out + lax.dot_general(tri, part, _DN["nn"], preferred_element_type=_F32)
    return out


@jax.custom_vjp
def _cumsum_rows(v):
    return _tri_sum(v, True)


_cumsum_rows.defvjp(lambda v: (_tri_sum(v, True), None), lambda _, ct: (_tri_sum(ct, False),))


def _ssd_chunk(xbc, dtraw, prev, bias, alog):
    l = xbc.shape[0]
    xs = xbc[:, :SSD_WIDTH]
    bm = xbc[:, SSD_WIDTH:SSD_WIDTH + 2 * SSD_STATE]
    cm = xbc[:, SSD_WIDTH + 2 * SSD_STATE:]
    v = dtraw + bias
    dt = jnp.maximum(v, 0.0) + jnp.log1p(jnp.exp(-jnp.abs(v)))
    a_neg = -jnp.exp(alog)
    acs = _cumsum_rows(dt * a_neg)
    acs_t = acs.T
    dt_t = dt.T
    total = acs[l - 1:l, :]
    row = lax.broadcasted_iota(jnp.int32, (l, l), 0)
    colv = lax.broadcasted_iota(jnp.int32, (l, l), 1)
    causal = row >= colv
    lane_lo = lax.broadcasted_iota(jnp.int32, (l, LANES), 1) < HEAD_DIM
    row_lo = lax.broadcasted_iota(jnp.int32, (LANES, SSD_STATE), 0) < HEAD_DIM

    def pair_lanes(m, h0):
        return jnp.where(lane_lo, m[:, h0:h0 + 1], m[:, h0 + 1:h0 + 2])

    ys, news = [], []
    cb = {}
    for j in range(SSD_HEADS // 2):
        h0 = 2 * j
        grp = h0 // (SSD_HEADS // 2)
        bg = bm[:, grp * SSD_STATE:(grp + 1) * SSD_STATE]
        cg = cm[:, grp * SSD_STATE:(grp + 1) * SSD_STATE]
        if grp not in cb:
            cb[grp] = _dot1(cg, bg, "nt")
        xp = xs[:, j * LANES:(j + 1) * LANES]
        y = jnp.zeros((l, LANES), _F32)
        for hh, mask in ((h0, lane_lo), (h0 + 1, jnp.logical_not(lane_lo))):
            seg = acs[:, hh:hh + 1] - acs_t[hh:hh + 1, :]
            dec = jnp.exp(jnp.where(causal, seg, -jnp.inf))
            sc = cb[grp] * dec * dt_t[hh:hh + 1, :]
            y = y + _dot1(sc, jnp.where(mask, xp, 0.0), "nn")
        acs_p = pair_lanes(acs, h0)
        prev_p = prev[j * LANES:(j + 1) * LANES, :]
        y = y + _dot3(cg, prev_p, "nt") * jnp.exp(acs_p)
        wgt = jnp.exp(pair_lanes(jnp.broadcast_to(total, (l, LANES)), h0) - acs_p) * pair_lanes(dt, h0)
        st = _dot3(xp * wgt, bg, "tn")
        cdec = jnp.exp(jnp.where(row_lo, total[:, h0:h0 + 1], total[:, h0 + 1:h0 + 2]))
        news.append(prev_p * cdec + st)
        ys.append(y)
    return jnp.concatenate(ys, axis=1), jnp.concatenate(news, axis=0)


def _ssd_fwd(xbc, dtraw, bias, alog, *, name):
    s = xbc.shape[0]
    nc = s // CHUNK
    nstate = SSD_HEADS * HEAD_DIM

    def body(x_ref, dt_ref, b_ref, a_ref, y_ref, st_ref, state_ref):
        @pl.when(pl.program_id(0) == 0)
        def _():
            state_ref[...] = jnp.zeros_like(state_ref)

        prev = state_ref[...]
        st_ref[...] = prev
        y, new = _ssd_chunk(x_ref[...], dt_ref[...], prev, b_ref[...], a_ref[...])
        y_ref[...] = y
        state_ref[...] = new

    small = pl.BlockSpec((1, LANES), lambda i: (0, 0))
    return pl.pallas_call(
        body, name=name, grid=(nc,),
        in_specs=[pl.BlockSpec((CHUNK, XBC_WIDTH), lambda i: (i, 0)), pl.BlockSpec((CHUNK, LANES), lambda i: (i, 0)),
                  small, small],
        out_specs=[pl.BlockSpec((CHUNK, SSD_WIDTH), lambda i: (i, 0)),
                   pl.BlockSpec((None, nstate, SSD_STATE), lambda i: (i, 0, 0))],
        out_shape=[jax.ShapeDtypeStruct((s, SSD_WIDTH), _F32), jax.ShapeDtypeStruct((nc, nstate, SSD_STATE), _F32)],
        scratch_shapes=[pltpu.VMEM((nstate, SSD_STATE), _F32)],
        compiler_params=_params(("arbitrary",)),
    )(xbc, dtraw, bias, alog)


def _ssd_bwd(xbc, dtraw, states, bias, alog, dy, dxs_extra, *, name):
    s = xbc.shape[0]
    nc = s // CHUNK
    nstate = SSD_HEADS * HEAD_DIM

    def body(x_ref, dt_ref, st_ref, b_ref, a_ref, dy_ref, ex_ref, dx_ref, ddt_ref, db_ref, da_ref, dstate_ref):
        @pl.when(pl.program_id(0) == 0)
        def _():
            dstate_ref[...] = jnp.zeros_like(dstate_ref)
            db_ref[...] = jnp.zeros_like(db_ref)
            da_ref[...] = jnp.zeros_like(da_ref)

        _, vjp = jax.vjp(_ssd_chunk, x_ref[...], dt_ref[...], st_ref[...], b_ref[...], a_ref[...])
        dx, ddt, dprev, db, da = vjp((dy_ref[...], dstate_ref[...]))
        dx_ref[:, :SSD_WIDTH] = dx[:, :SSD_WIDTH] + ex_ref[...]
        dx_ref[:, SSD_WIDTH:] = dx[:, SSD_WIDTH:]
        ddt_ref[...] = ddt
        db_ref[...] += db
        da_ref[...] += da
        dstate_ref[...] = dprev

    rev = lambda i: (nc - 1 - i, 0)
    small = pl.BlockSpec((1, LANES), lambda i: (0, 0))
    return pl.pallas_call(
        body, name=name, grid=(nc,),
        in_specs=[pl.BlockSpec((CHUNK, XBC_WIDTH), rev), pl.BlockSpec((CHUNK, LANES), rev),
                  pl.BlockSpec((None, nstate, SSD_STATE), lambda i: (nc - 1 - i, 0, 0)), small, small,
                  pl.BlockSpec((CHUNK, SSD_WIDTH), rev), pl.BlockSpec((CHUNK, SSD_WIDTH), rev)],
        out_specs=[pl.BlockSpec((CHUNK, XBC_WIDTH), rev), pl.BlockSpec((CHUNK, LANES), rev), small, small],
        out_shape=[jax.ShapeDtypeStruct((s, XBC_WIDTH), _F32), jax.ShapeDtypeStruct((s, LANES), _F32),
                   jax.ShapeDtypeStruct((1, LANES), _F32), jax.ShapeDtypeStruct((1, LANES), _F32)],
        scratch_shapes=[pltpu.VMEM((nstate, SSD_STATE), _F32)],
        compiler_params=_params(("arbitrary",)),
    )(xbc, dtraw, states, bias, alog, dy, dxs_extra)


def _pad_cols(a, width):
    return jnp.pad(a, ((0, 0), (0, width - a.shape[1])))


def _pad_rows(a, rows):
    return jnp.pad(a, ((0, rows - a.shape[0]), (0, 0)))


def _tie(a, token):
    return a + token[0:1, 0:1].astype(a.dtype)


def _local_step(x, mem, target, w, fetch, emit):
    bf = _MXU
    d = D_MODEL
    h = _row_fwd(_f_rms, [x], [w['norm_mix_g']], [(d, bf)], name="f_norm_mix")
    w_in = fetch('in', h)['w_in']
    z_end, xbc_end, dt_end = SSD_WIDTH, SSD_WIDTH + XBC_WIDTH, SSD_WIDTH + XBC_WIDTH + SSD_HEADS
    w_z, w_xbc = w_in[:, :z_end], w_in[:, z_end:xbc_end]
    w_dt = _pad_cols(w_in[:, xbc_end:dt_end], LANES)
    w_a, w_g = w_in[:, dt_end:dt_end + CF_WIDTH], w_in[:, dt_end + CF_WIDTH:]
    dt_bias = _pad_cols(w['ssd_dt_bias'], LANES)
    a_log = _pad_cols(w['ssd_A_log'], LANES)
    d_exp = jnp.repeat(w['ssd_D'], HEAD_DIM, axis=1)
    g_final = w['norm_final_g'].reshape(1, D_MODEL)

    z = _mm(h, w_z, out_dtype=bf, name="f_in_z")
    xbc = _mm(h, w_xbc, name="f_in_xbc")
    dtr = _mm(h, w_dt, name="f_in_dt")
    ga = _mm(h, w_a, out_dtype=bf, name="f_in_a")
    gg = _mm(h, w_g, out_dtype=bf, name="f_in_g")
    wc = fetch('conv', xbc)
    ssd_w = _pad_rows(wc['ssd_conv_w'], SUBLANES)
    cf_w = _pad_rows(wc['cf_conv_w'], 32)
    xbc_c = _conv_fwd(xbc, ssd_w, w['ssd_conv_b'], SSD_CONV, name="f_ssd_conv")
    xbc_a = _row_fwd(_silu, [xbc_c], [], [(XBC_WIDTH, _F32)], name="f_ssd_silu")
    y_ssd, states = _ssd_fwd(xbc_a, dtr, dt_bias, a_log, name="f_ssd")
    xs_win = (xbc_a, SSD_WIDTH, 0)
    y_n = _row_fwd(_f_gate, [y_ssd, xs_win, z], [d_exp, w['ssd_norm_g']], [(d, bf)], name="f_ssd_gate")
    u_pre = _row_fwd(_f_glu, [ga, gg], [], [(CF_WIDTH, _F32)], name="f_glu")
    u_c = _conv_fwd(u_pre, cf_w, w['cf_conv_b'], CF_CONV, name="f_cf_conv")
    u = _row_fwd(_f_ln, [u_c], [w['cf_ln_g'], w['cf_ln_b']], [(d, bf)], name="f_cf_ln")
    wm = fetch('mid', y_n)
    w_out_y, w_out_u = wm['w_out'][:SSD_WIDTH], wm['w_out'][SSD_WIDTH:]
    x1 = _mm(y_n, w_out_y, add=x, name="f_out_y")
    x1 = _mm(u, w_out_u, add=x1, name="f_out_u")
    hq = _row_fwd(_f_rms, [x1], [w['norm_xattn_g']], [(d, bf)], name="f_norm_xattn")
    q = _mm(hq, wm['w_q'], out_dtype=bf, name="f_q")
    memn = _row_fwd(_f_rms, [mem], [w['norm_mem_g']], [(d, bf)], name="f_norm_mem")
    kv = _mm(memn, wm['w_kv'], name="f_kv")
    k_mat, v_mat = kv[:, :d], kv[:, d:]
    o = _row_fwd(_f_att, [q], [k_mat, v_mat], [(d, bf)], name="f_att")
    x2 = _mm(o, wm['w_o'], add=x1, name="f_o")
    hf = _row_fwd(_f_rms, [x2], [w['norm_ffn_g']], [(d, bf)], name="f_norm_ffn")
    wf = fetch('ffn', hf)
    gate = _mm(hf, wf['w_gate'], out_dtype=bf, name="f_gate")
    up = _mm(hf, wf['w_up'], out_dtype=bf, name="f_up")
    act = _row_fwd(_f_swiglu, [gate, up], [], [(D_FF, bf)], name="f_swiglu")
    x3 = _mm(act, wf['w_down'], add=x2, name="f_down")

    dx3, dg_final, loss = _loss_bwd(x3, target, g_final, name="b_loss")
    g = {'norm_final_g': dg_final.reshape(d)}

    dact = _mm(dx3, wf['w_down'], tb=True, out_dtype=bf, name="b_down_x")
    dw_down = _mm(act, dx3, ta=True, name="b_down_w")
    (dgate, dup), _ = _row_bwd(_f_swiglu, [gate, up], [], [dact], row_dtypes=[bf, bf], name="b_swiglu")
    sent = emit({'w_down': dw_down, 'w_gate': _mm(hf, dgate, ta=True, name="b_gate_w"),
                 'w_up': _mm(hf, dup, ta=True, name="b_up_w")})
    dhf = _mm(dgate, wf['w_gate'], tb=True, name="b_gate_x")
    dhf = _mm(dup, wf['w_up'], tb=True, add=dhf, out_dtype=bf, name="b_up_x")
    (dx2,), (g['norm_ffn_g'],) = _row_bwd(_f_rms, [x2], [_tie(w['norm_ffn_g'], sent)], [dhf], adds={0: dx3}, name="b_norm_ffn")

    do = _mm(dx2, wm['w_o'], tb=True, out_dtype=bf, name="b_o_x")
    dw_o = _mm(o, dx2, ta=True, name="b_o_w")
    (dq,), (dk, dv) = _row_bwd(_f_att, [q], [k_mat, v_mat], [do], row_dtypes=[bf], name="b_att")
    dw_q = _mm(hq, dq, ta=True, name="b_q_w")
    dhq = _mm(dq, wm['w_q'], tb=True, out_dtype=bf, name="b_q_x")
    (dx1,), (g['norm_xattn_g'],) = _row_bwd(_f_rms, [x1], [w['norm_xattn_g']], [dhq], adds={0: dx2}, name="b_norm_xattn")
    dkv = jnp.concatenate([dk, dv], axis=1)
    sent = emit({'w_o': dw_o, 'w_q': dw_q, 'w_kv': _mm(memn, dkv, ta=True, name="b_kv_w")})
    dmemn = _mm(dkv, wm['w_kv'], tb=True, name="b_kv_x")
    _, (g['norm_mem_g'],) = _row_bwd(_f_rms, [mem], [w['norm_mem_g']], [dmemn], need=[False], name="b_norm_mem")

    dyn = _mm(dx1, w_out_y, tb=True, out_dtype=bf, name="b_out_y_x")
    du = _mm(dx1, w_out_u, tb=True, out_dtype=bf, name="b_out_u_x")
    (du_c,), (g['cf_ln_g'], g['cf_ln_b']) = _row_bwd(_f_ln, [u_c], [_tie(w['cf_ln_g'], sent), w['cf_ln_b']], [du], name="b_cf_ln")
    sent = emit({'w_out': jnp.concatenate([_mm(y_n, dx1, ta=True, name="b_out_y_w"), _mm(u, dx1, ta=True, name="b_out_u_w")], axis=0)})
    du_pre, dcf_w, g['cf_conv_b'] = _conv_bwd(u_pre, cf_w, du_c, CF_CONV, name="b_cf_conv")
    g['cf_conv_w'] = dcf_w[:CF_CONV]
    (dga, dgg), _ = _row_bwd(_f_glu, [ga, gg], [], [du_pre], row_dtypes=[bf, bf], name="b_glu")
    (dy_ssd, dxs, dz), (dd_exp, g['ssd_norm_g']) = _row_bwd(
        _f_gate, [y_ssd, xs_win, z], [d_exp, _tie(w['ssd_norm_g'], sent)], [dyn], row_dtypes=[_F32, _F32, bf], name="b_ssd_gate")
    g['ssd_D'] = jnp.sum(dd_exp.reshape(SSD_HEADS, HEAD_DIM), axis=1).reshape(1, SSD_HEADS)
    dxbc_a, ddtr, ddt_bias, da_log = _ssd_bwd(xbc_a, dtr, states, dt_bias, a_log, dy_ssd, dxs, name="b_ssd")
    g['ssd_dt_bias'] = ddt_bias[:, :SSD_HEADS]
    g['ssd_A_log'] = da_log[:, :SSD_HEADS]
    (dxbc_c,), _ = _row_bwd(_silu, [xbc_c], [], [dxbc_a], name="b_ssd_silu")
    dxbc, dssd_w, g['ssd_conv_b'] = _conv_bwd(xbc, ssd_w, dxbc_c, SSD_CONV, name="b_ssd_conv")
    g['ssd_conv_w'] = dssd_w[:SSD_CONV]

    sent = emit({'w_in': jnp.concatenate([
        _mm(h, dz, ta=True, name="b_in_z_w"), _mm(h, dxbc, ta=True, name="b_in_xbc_w"),
        _mm(h, ddtr, ta=True, name="b_in_dt_w")[:, :SSD_HEADS],
        _mm(h, dga, ta=True, name="b_in_a_w"), _mm(h, dgg, ta=True, name="b_in_g_w")], axis=1)})
    dh = _mm(dz, _tie(w_z, sent), tb=True, name="b_in_z_x")
    dh = _mm(dxbc, w_xbc, tb=True, add=dh, name="b_in_xbc_x")
    dh = _mm(ddtr, w_dt, tb=True, add=dh, name="b_in_dt_x")
    dh = _mm(dga, w_a, tb=True, add=dh, name="b_in_a_x")
    dh = _mm(dgg, w_g, tb=True, add=dh, out_dtype=bf, name="b_in_g_x")
    (dx,), (g['norm_mix_g'],) = _row_bwd(_f_rms, [x], [w['norm_mix_g']], [dh], adds={0: dx1}, name="b_norm_mix")
    return loss, dx, g


_ANY = pl.BlockSpec(memory_space=pl.ANY)


def _place():
    x, y, c = lax.axis_index("x"), lax.axis_index("y"), lax.axis_index("c")
    return x, y, c


def _all_gather(arrs, *, name):
    n = len(arrs)

    def body(*refs):
        ins, outs = refs[:n], refs[n:2 * n]
        send_sems, recv_sems, local_sems = refs[2 * n:]
        x, y, c = _place()
        me, sibling = (x, y, c), (x, y, 1 - c)
        chips = [(1 - x, y), (x, 1 - y), (1 - x, 1 - y)]

        def slot(a, dev):
            return outs[a].at[4 * dev[0] + 2 * dev[1] + dev[2]]

        def copy(a, k, block, to, src=None):
            return pltpu.make_async_remote_copy(
                src_ref=slot(a, block) if src is None else src, dst_ref=slot(a, block),
                send_sem=send_sems.at[a, k], recv_sem=recv_sems.at[a, k], device_id=to, device_id_type=MESH)

        mine = [pltpu.make_async_copy(ins[a], slot(a, me), local_sems.at[a]) for a in range(n)]
        for cp in mine:
            cp.start()
        first = []
        for a in range(n):
            first.append(copy(a, 0, me, sibling, src=ins[a]))
            first += [copy(a, 1 + j, me, (*chip, c), src=ins[a]) for j, chip in enumerate(chips)]
        for cp in first:
            cp.start()
        passed = []
        for a in range(n):
            for j, chip in enumerate(chips):
                copy(a, 1 + j, (*chip, c), me).wait_recv()
                fwd = copy(a, 4 + j, (*chip, c), sibling)
                fwd.start()
                passed.append(fwd)
        for a in range(n):
            copy(a, 0, sibling, me).wait_recv()
            for j, chip in enumerate(chips):
                copy(a, 4 + j, (*chip, 1 - c), me).wait_recv()
        for cp in first + passed:
            cp.wait_send()
        for cp in mine:
            cp.wait()

    return pl.pallas_call(
        body, name=name, in_specs=[_ANY] * n, out_specs=[_ANY] * n,
        out_shape=[jax.ShapeDtypeStruct((N_DEV,) + a.shape, a.dtype) for a in arrs],
        scratch_shapes=[pltpu.SemaphoreType.DMA((n, 7)), pltpu.SemaphoreType.DMA((n, 7)), pltpu.SemaphoreType.DMA((n,))],
    )(*arrs)


_HBM = pl.BlockSpec(memory_space=pltpu.HBM)
_SEM = pl.BlockSpec(memory_space=pltpu.SEMAPHORE)
_EFFECT = pltpu.SideEffectType.DATAFLOW_SIDE_EFFECTING
_FLIPS = [(dx, dy, dc) for dx in (0, 1) for dy in (0, 1) for dc in (0, 1)][1:]


def _peer(flip, x, y, c):
    return (1 - x if flip[0] else x, 1 - y if flip[1] else y, 1 - c if flip[2] else c)


def _send_start(srcs, blocked, *, after=None, name):
    n = len(srcs)
    lands = [jax.ShapeDtypeStruct(s.shape if blocked else (N_DEV,) + s.shape, s.dtype) for s in srcs]
    n_in = 2 * n + (after is not None)

    def body(*refs):
        src_refs, land_refs = refs[:n], refs[n:2 * n]
        send_sems, recv_sems = refs[n_in], refs[n_in + 1]
        token = refs[-1]
        x, y, c = _place()
        me = 4 * x + 2 * y + c
        for a in range(n):
            for k, flip in enumerate(_FLIPS):
                p = _peer(flip, x, y, c)
                src = src_refs[a].at[4 * p[0] + 2 * p[1] + p[2]] if blocked else src_refs[a]
                pltpu.make_async_remote_copy(
                    src_ref=src, dst_ref=land_refs[a].at[me], send_sem=send_sems.at[7 * a + k], recv_sem=recv_sems.at[7 * a + k],
                    device_id=p, device_id_type=MESH).start()
        token[...] = jnp.zeros_like(token)

    res = pl.pallas_call(
        body, name=name,
        out_shape=(pltpu.SemaphoreType.DMA((7 * n,)), pltpu.SemaphoreType.DMA((7 * n,)),
                   *[pltpu.HBM(s.shape, s.dtype) for s in srcs], *[pltpu.HBM(l.shape, l.dtype) for l in lands],
                   jax.ShapeDtypeStruct((SUBLANES, LANES), _F32)),
        in_specs=[_HBM] * (2 * n) + [_ANY] * (after is not None),
        out_specs=(_SEM, _SEM, *[_HBM] * (2 * n), pl.BlockSpec(memory_space=pltpu.VMEM)),
        input_output_aliases={i: 2 + i for i in range(2 * n)},
        compiler_params=pltpu.CompilerParams(has_side_effects=_EFFECT),
    )(*[pltpu.with_memory_space_constraint(s, pltpu.HBM) for s in srcs],
      *[pltpu.with_memory_space_constraint(lax.empty(l.shape, l.dtype), pltpu.HBM) for l in lands],
      *([after] if after is not None else []))
    return res[0], res[1], list(res[2:2 + n]), list(res[2 + n:2 + 2 * n]), res[-1]


def _send_wait(handles, after, blocked, *, name):
    send_sems, recv_sems, srcs, lands, _ = handles
    n = len(srcs)

    def body(*refs):
        src_refs, land_refs = refs[:n], refs[n:2 * n]
        send_sems, recv_sems = refs[2 * n], refs[2 * n + 1]
        x, y, c = _place()
        for a in range(n):
            for k, flip in enumerate(_FLIPS):
                p = _peer(flip, x, y, c)
                pid = 4 * p[0] + 2 * p[1] + p[2]
                cp = pltpu.make_async_remote_copy(
                    src_ref=src_refs[a].at[pid] if blocked else src_refs[a], dst_ref=land_refs[a].at[pid],
                    send_sem=send_sems.at[7 * a + k], recv_sem=recv_sems.at[7 * a + k], device_id=p, device_id_type=MESH)
                cp.wait_send()
                cp.wait_recv()

    res = pl.pallas_call(
        body, name=name,
        out_shape=tuple(pltpu.HBM(s.shape, s.dtype) for s in srcs + lands),
        in_specs=[_HBM] * (2 * n) + [_SEM, _SEM, _ANY], out_specs=tuple([_HBM] * (2 * n)),
        input_output_aliases={i: i for i in range(2 * n)},
        compiler_params=pltpu.CompilerParams(has_side_effects=_EFFECT),
    )(*srcs, *lands, send_sems, recv_sems, after)
    return list(res[:n]), list(res[n:])


def _adamw(parts, w, m, v, *, own=None, me=None, name):
    p, r, c = parts.shape
    tr = _pick(r, (256, 176, 128, 64, 32, 16, 8))
    if own is not None:
        return _adamw_own(parts, own, me, w, m, v, tr, name=name)

    def body(p_ref, w_ref, m_ref, v_ref, g_ref, d_ref, nm_ref, nv_ref):
        g = p_ref[0].astype(_F32)
        for i in range(1, p):
            g = g + p_ref[i].astype(_F32)
        _adamw_math(g, w_ref, m_ref, v_ref, g_ref, d_ref, nm_ref, nv_ref)

    blk = pl.BlockSpec((tr, c), lambda i: (i, 0))
    return pl.pallas_call(
        body, name=name, grid=(r // tr,),
        in_specs=[pl.BlockSpec((p, tr, c), lambda i: (0, i, 0)), blk, blk, blk], out_specs=[blk] * 4,
        out_shape=[jax.ShapeDtypeStruct((r, c), _F32)] * 4,
        compiler_params=_params(("parallel",)),
    )(parts, w, m, v)


def _adamw_math(g, w_ref, m_ref, v_ref, g_ref, d_ref, nm_ref, nv_ref):
    wv = w_ref[...]
    mn = ADAM_B1 * m_ref[...] + (1.0 - ADAM_B1) * g
    vn = ADAM_B2 * v_ref[...] + (1.0 - ADAM_B2) * jnp.square(g)
    m_hat = mn / (1.0 - ADAM_B1 ** ADAM_STEP)
    v_hat = vn / (1.0 - ADAM_B2 ** ADAM_STEP)
    g_ref[...] = g
    d_ref[...] = -ADAM_LR * (m_hat / (jnp.sqrt(v_hat) + ADAM_EPS) + ADAM_WD * wv)
    nm_ref[...] = mn
    nv_ref[...] = vn


def _adamw_own(parts, own, me, w, m, v, tr, *, name):
    p, r, c = parts.shape

    def body(me_ref, p_ref, own_ref, w_ref, m_ref, v_ref, g_ref, d_ref, nm_ref, nv_ref):
        mine = own_ref[...].astype(_F32)
        g = jnp.where(me_ref[0] == 0, mine, p_ref[0].astype(_F32))
        for i in range(1, p):
            g = g + jnp.where(me_ref[0] == i, mine, p_ref[i].astype(_F32))
        _adamw_math(g, w_ref, m_ref, v_ref, g_ref, d_ref, nm_ref, nv_ref)

    blk = pl.BlockSpec((tr, c), lambda i, me_ref: (i, 0))
    grid_spec = pltpu.PrefetchScalarGridSpec(
        num_scalar_prefetch=1, grid=(r // tr,),
        in_specs=[pl.BlockSpec((p, tr, c), lambda i, me_ref: (0, i, 0)),
                  pl.BlockSpec((None, tr, c), lambda i, me_ref: (me_ref[0], i, 0)), blk, blk, blk],
        out_specs=[blk] * 4)
    return pl.pallas_call(
        body, name=name, grid_spec=grid_spec, out_shape=[jax.ShapeDtypeStruct((r, c), _F32)] * 4,
        compiler_params=_params(("parallel",)),
    )(me.reshape(1).astype(jnp.int32), parts, own, w, m, v)


def _sum_parts(parts, *, name):
    p, r, c = parts.shape

    def body(p_ref, o_ref):
        g = p_ref[0].astype(_F32)
        for i in range(1, p):
            g = g + p_ref[i].astype(_F32)
        o_ref[...] = g

    return pl.pallas_call(body, name=name, out_shape=jax.ShapeDtypeStruct((r, c), _F32))(parts)


def _pack(vals, rows):
    flat = jnp.concatenate([v.reshape(-1) for v in vals])
    return jnp.pad(flat, (0, rows * LANES - flat.shape[0])).reshape(rows, LANES)


def _unpack(packed, shapes):
    flat = packed.reshape(-1)
    out, pos = [], 0
    for shp in shapes:
        size = math.prod(shp)
        out.append(flat[pos:pos + size].reshape(shp))
        pos += size
    return out


def _pack_rows(shapes):
    total = sum(math.prod(s) for s in shapes)
    return -(-total // (LANES * SUBLANES)) * SUBLANES


def kernel(x, mem, norm_mix_g, w_in, ssd_conv_w, ssd_conv_b, ssd_dt_bias, ssd_A_log, ssd_D, ssd_norm_g, cf_conv_w, cf_conv_b, cf_ln_g, cf_ln_b, w_out, norm_xattn_g, norm_mem_g, w_q, w_kv, w_o, norm_ffn_g, w_gate, w_up, w_down, norm_final_g, loss_target, m_norm_mix_g, m_w_in, m_ssd_conv_w, m_ssd_conv_b, m_ssd_dt_bias, m_ssd_A_log, m_ssd_D, m_ssd_norm_g, m_cf_conv_w, m_cf_conv_b, m_cf_ln_g, m_cf_ln_b, m_w_out, m_norm_xattn_g, m_norm_mem_g, m_w_q, m_w_kv, m_w_o, m_norm_ffn_g, m_w_gate, m_w_up, m_w_down, m_norm_final_g, v_norm_mix_g, v_w_in, v_ssd_conv_w, v_ssd_conv_b, v_ssd_dt_bias, v_ssd_A_log, v_ssd_D, v_ssd_norm_g, v_cf_conv_w, v_cf_conv_b, v_cf_ln_g, v_cf_ln_b, v_w_out, v_norm_xattn_g, v_norm_mem_g, v_w_q, v_w_kv, v_w_o, v_norm_ffn_g, v_w_gate, v_w_up, v_w_down, v_norm_final_g):
    args = dict(locals())
    wts = {n: args[n] for n in WEIGHT_NAMES}
    mom = {n: args["m_" + n] for n in WEIGHT_NAMES}
    var = {n: args["v_" + n] for n in WEIGHT_NAMES}
    me = 4 * lax.axis_index("x") + 2 * lax.axis_index("y") + lax.axis_index("c")

    groups = {'in': ['w_in'], 'conv': ['ssd_conv_w', 'cf_conv_w'], 'mid': ['w_out', 'w_q', 'w_kv', 'w_o'],
              'ffn': ['w_gate', 'w_up', 'w_down']}
    gathers, started = {}, None
    for grp, names in groups.items():
        shards = [wts[n][0] if grp == 'conv' else wts[n][0].astype(_MXU) for n in names]
        gathers[grp] = _send_start(shards, False, after=started, name="gather_%s_start" % grp)
        started = gathers[grp][4]

    def fetch(grp, after):
        srcs, lands = _send_wait(gathers[grp], started if after is None else after, False, name="gather_%s_wait" % grp)
        out = {}
        for n, own, gth in zip(groups[grp], srcs, lands):
            gth = lax.dynamic_update_slice_in_dim(gth, own[None], me, axis=0)
            if n in COL_SHARDED or grp == 'conv':
                out[n] = jnp.transpose(gth, (1, 0, 2)).reshape(gth.shape[1], N_DEV * gth.shape[2])
            else:
                out[n] = gth.reshape(N_DEV * gth.shape[1], gth.shape[2])
        return out

    exchanges = []

    def emit(grads):
        blocks = []
        for n, gw in grads.items():
            if n in COL_SHARDED:
                gw = jnp.transpose(gw.reshape(gw.shape[0], N_DEV, gw.shape[1] // N_DEV), (1, 0, 2))
            else:
                gw = gw.reshape(N_DEV, gw.shape[0] // N_DEV, gw.shape[1])
            blocks.append(gw.astype(jnp.bfloat16))
        first = next(iter(grads))
        exchanges.append((list(grads), _send_start(blocks, True, name="exchange_%s_start" % first), first))
        return exchanges[-1][1][4]

    full = {n: wts[n] for n in WEIGHT_NAMES if n not in BIG and n not in groups['conv']}
    full['norm_mix_g'] = _tie(norm_mix_g, started)

    loss_blk, grad_x, g = _local_step(x[0], mem[0], loss_target[0], full, fetch, emit)
    loss = lax.psum(loss_blk[0, 0], ("x", "y", "c"))

    out_g, out_d, out_m, out_v = {}, {}, {}, {}
    for names, handles, first in exchanges:
        srcs, lands = _send_wait(handles, grad_x, True, name="exchange_%s_wait" % first)
        for n, own, parts in zip(names, srcs, lands):
            res = _adamw(parts, wts[n][0], mom[n][0], var[n][0], own=own, me=me, name="adamw_" + n)
            out_g[n], out_d[n], out_m[n], out_v[n] = [r[None] for r in res]

    small = [n for n in WEIGHT_NAMES if n not in BIG]
    small_shapes = [g[n].shape for n in small]
    rows = _pack_rows(small_shapes)
    (small_parts,) = _all_gather([_pack([g[n] for n in small], rows)], name="gather_small_grads")
    small_sum = dict(zip(small, _unpack(_sum_parts(small_parts, name="sum_small_grads"), small_shapes)))
    small_sum['ssd_conv_w'] = lax.dynamic_slice_in_dim(small_sum['ssd_conv_w'], me * (XBC_WIDTH // N_DEV), XBC_WIDTH // N_DEV, axis=1)[None]
    small_sum['cf_conv_w'] = lax.dynamic_slice_in_dim(small_sum['cf_conv_w'], me * (CF_WIDTH // N_DEV), CF_WIDTH // N_DEV, axis=1)[None]
    shard_shapes = [wts[n].shape for n in small]
    rows2 = _pack_rows(shard_shapes)
    res = _adamw(_pack([small_sum[n] for n in small], rows2)[None], _pack([wts[n] for n in small], rows2),
                 _pack([mom[n] for n in small], rows2), _pack([var[n] for n in small], rows2), name="adamw_small")
    for dst, packed in zip((out_g, out_d, out_m, out_v), res):
        dst.update(zip(small, _unpack(packed, shard_shapes)))

    return (loss, grad_x[None], *[out_g[n] for n in WEIGHT_NAMES], *[out_d[n] for n in WEIGHT_NAMES],
            *[out_m[n] for n in WEIGHT_NAMES], *[out_v[n] for n in WEIGHT_NAMES])
```

```python
import functools
import math

import jax
import jax.numpy as jnp
from jax import lax
from jax.experimental import pallas as pl
from jax.experimental.pallas import tpu as pltpu

_F32 = jnp.float32
_MXU = jnp.bfloat16
_PREC = None
_VMEM_LIMIT = 56 * 1024 * 1024

D_MODEL = 1024
HEAD_DIM = 64
SSD_HEADS = 16
SSD_WIDTH = 1024
SSD_STATE = 128
SSD_CONV = 4
CHUNK = 128
XBC_WIDTH = 1536
CF_WIDTH = 1024
CF_CONV = 31
X_HEADS = 4
X_HEAD_DIM = 256
D_FF = 2816
EPS = 1e-6
N_DEV = 8
LANES = 128
SUBLANES = 8

ADAM_LR = 0.001
ADAM_B1 = 0.9
ADAM_B2 = 0.999
ADAM_EPS = 1e-08
ADAM_WD = 0.01
ADAM_STEP = 10

MESH = pl.DeviceIdType.MESH
WEIGHT_NAMES = ['norm_mix_g', 'w_in', 'ssd_conv_w', 'ssd_conv_b', 'ssd_dt_bias', 'ssd_A_log', 'ssd_D', 'ssd_norm_g',
                'cf_conv_w', 'cf_conv_b', 'cf_ln_g', 'cf_ln_b', 'w_out', 'norm_xattn_g', 'norm_mem_g', 'w_q', 'w_kv',
                'w_o', 'norm_ffn_g', 'w_gate', 'w_up', 'w_down', 'norm_final_g']
BIG = ['w_in', 'w_out', 'w_q', 'w_kv', 'w_o', 'w_gate', 'w_up', 'w_down']
TRANSPOSED = ('w_in', 'w_gate', 'w_up')


def _params(sem=None):
    return pltpu.CompilerParams(dimension_semantics=sem, vmem_limit_bytes=_VMEM_LIMIT)


def _pick(n, cands):
    for c in cands:
        if n % c == 0:
            return c
    return n


def _mm(a, b, *, ta=False, tb=False, add=None, out_dtype=_F32, name):
    (kdim, m) = a.shape if ta else a.shape[::-1]
    (n, k2) = b.shape if tb else b.shape[::-1]
    assert kdim == k2, (a.shape, b.shape, ta, tb)
    if ta:
        tm = m if m <= 1024 else _pick(m, (1408, 1024, 512, 256, 128))
        tn = n if n <= 1536 else _pick(n, (1408, 1024, 512, 256, 128))
        tk = _pick(kdim, (1024, 512, 256, 128))
    else:
        tm = _pick(m, (512, 256, 128))
        tn = n if n <= 2816 else _pick(n, (1408, 1024, 512, 256, 128))
        tk = kdim if kdim <= 2816 else _pick(kdim, (1408, 1024, 512, 256, 128))
    nk = kdim // tk
    dn = (((0 if ta else 1,), (1 if tb else 0,)), ((), ()))

    def body(*refs):
        a_ref, b_ref = refs[0], refs[1]
        add_ref = refs[2] if add is not None else None
        o_ref = refs[3 if add is not None else 2]
        acc_ref = refs[-1]
        k = pl.program_id(2)
        prod = lax.dot_general(a_ref[...].astype(_MXU), b_ref[...].astype(_MXU), dn,
                               preferred_element_type=_F32, precision=_PREC)

        def finish(r):
            if add_ref is not None:
                r = r + add_ref[...].astype(_F32)
            o_ref[...] = r.astype(o_ref.dtype)

        if nk == 1:
            finish(prod)
            return

        @pl.when(k == 0)
        def _():
            acc_ref[...] = prod

        @pl.when(jnp.logical_and(k > 0, k < nk - 1))
        def _():
            acc_ref[...] += prod

        @pl.when(k == nk - 1)
        def _():
            finish(acc_ref[...] + prod)

    a_spec = pl.BlockSpec((tk, tm), lambda i, j, k: (k, i)) if ta else pl.BlockSpec((tm, tk), lambda i, j, k: (i, k))
    b_spec = pl.BlockSpec((tn, tk), lambda i, j, k: (j, k)) if tb else pl.BlockSpec((tk, tn), lambda i, j, k: (k, j))
    o_spec = pl.BlockSpec((tm, tn), lambda i, j, k: (i, j))
    ins, specs = [a, b], [a_spec, b_spec]
    if add is not None:
        ins.append(add)
        specs.append(o_spec)
    return pl.pallas_call(
        body, name=name, grid=(m // tm, n // tn, nk), in_specs=specs, out_specs=o_spec,
        out_shape=jax.ShapeDtypeStruct((m, n), out_dtype),
        scratch_shapes=[pltpu.VMEM((tm, tn), _F32)] if nk > 1 else [],
        compiler_params=_params(("parallel", "parallel", "arbitrary")),
    )(*ins)


def _row_spec(r, ts):
    if isinstance(r, tuple):
        arr, width, cblk = r
        return arr, pl.BlockSpec((ts, width), lambda i, cblk=cblk: (i, cblk))
    return r, pl.BlockSpec((ts, r.shape[1]), lambda i: (i, 0))


def _tup(v):
    return tuple(v) if isinstance(v, (tuple, list)) else (v,)


def _row_fwd(f, rows, params, outs, *, name, ts=256):
    arrs, specs = zip(*[_row_spec(r, ts) for r in rows])
    s = arrs[0].shape[0]
    ts = min(ts, s)
    n_r, n_p = len(rows), len(params)

    def body(*refs):
        rv = [r[...].astype(_F32) for r in refs[:n_r]]
        pv = [p[...] for p in refs[n_r:n_r + n_p]]
        res = _tup(f(*rv, *pv))
        for o_ref, v in zip(refs[n_r + n_p:], res):
            o_ref[...] = v.astype(o_ref.dtype)

    res = pl.pallas_call(
        body, name=name, grid=(s // ts,),
        in_specs=list(specs) + [pl.BlockSpec(p.shape, lambda i: (0, 0)) for p in params],
        out_specs=[pl.BlockSpec((ts, w), lambda i: (i, 0)) for w, _ in outs],
        out_shape=[jax.ShapeDtypeStruct((s, w), dt) for w, dt in outs],
        compiler_params=_params(("parallel",)),
    )(*arrs, *params)
    return res[0] if len(outs) == 1 else res


def _row_bwd(f, rows, params, cts, *, need=None, adds=None, row_dtypes=None, name, ts=256):
    arrs, specs = zip(*[_row_spec(r, ts) for r in rows])
    s = arrs[0].shape[0]
    ts = min(ts, s)
    n_r, n_p, n_c = len(rows), len(params), len(cts)
    need = [True] * n_r if need is None else need
    adds = {} if adds is None else adds
    add_keys = sorted(adds)
    row_dtypes = [_F32] * n_r if row_dtypes is None else row_dtypes
    needed = [j for j in range(n_r) if need[j]]
    widths = [specs[j].block_shape[1] for j in range(n_r)]

    def body(*refs):
        pos = 0
        r_refs = refs[pos:pos + n_r]; pos += n_r
        p_refs = refs[pos:pos + n_p]; pos += n_p
        c_refs = refs[pos:pos + n_c]; pos += n_c
        a_refs = refs[pos:pos + len(add_keys)]; pos += len(add_keys)
        dr_refs = refs[pos:pos + len(needed)]; pos += len(needed)
        dp_refs = refs[pos:pos + n_p]
        rv = [r[...].astype(_F32) for r in r_refs]
        pv = [p[...] for p in p_refs]
        _, vjp = jax.vjp(lambda *a: _tup(f(*a)), *rv, *pv)
        g = vjp(tuple(c[...].astype(_F32) for c in c_refs))
        for o_ref, j in zip(dr_refs, needed):
            v = g[j]
            if j in adds:
                v = v + a_refs[add_keys.index(j)][...].astype(_F32)
            o_ref[...] = v.astype(o_ref.dtype)
        if n_p:
            @pl.when(pl.program_id(0) == 0)
            def _():
                for dp in dp_refs:
                    dp[...] = jnp.zeros_like(dp)
            for dp, v in zip(dp_refs, g[n_r:]):
                dp[...] += v

    ct_specs = [pl.BlockSpec((ts, c.shape[1]), lambda i: (i, 0)) for c in cts]
    add_specs = [pl.BlockSpec((ts, adds[j].shape[1]), lambda i: (i, 0)) for j in add_keys]
    res = pl.pallas_call(
        body, name=name, grid=(s // ts,),
        in_specs=list(specs) + [pl.BlockSpec(p.shape, lambda i: (0, 0)) for p in params] + ct_specs + add_specs,
        out_specs=[pl.BlockSpec((ts, widths[j]), lambda i: (i, 0)) for j in needed]
        + [pl.BlockSpec(p.shape, lambda i: (0, 0)) for p in params],
        out_shape=[jax.ShapeDtypeStruct((s, widths[j]), row_dtypes[j]) for j in needed]
        + [jax.ShapeDtypeStruct(p.shape, _F32) for p in params],
        compiler_params=_params(("arbitrary",)),
    )(*arrs, *params, *cts, *[adds[j] for j in add_keys])
    return list(res[:len(needed)]), list(res[len(needed):])


_DN = {"nn": (((1,), (0,)), ((), ())), "nt": (((1,), (1,)), ((), ())), "tn": (((0,), (0,)), ((), ()))}


def _make_dot(passes):
    def raw(a, b, kind):
        dn = _DN[kind]
        if passes == 1 or _MXU == _F32:
            return lax.dot_general(a.astype(_MXU), b.astype(_MXU), dn, preferred_element_type=_F32, precision=_PREC)
        a_hi, b_hi = a.astype(_MXU), b.astype(_MXU)
        a_lo = (a - a_hi.astype(_F32)).astype(_MXU)
        b_lo = (b - b_hi.astype(_F32)).astype(_MXU)
        out = lax.dot_general(a_hi, b_hi, dn, preferred_element_type=_F32)
        out = out + lax.dot_general(a_lo, b_hi, dn, preferred_element_type=_F32)
        return out + lax.dot_general(a_hi, b_lo, dn, preferred_element_type=_F32)

    @functools.partial(jax.custom_vjp, nondiff_argnums=(2,))
    def dot(a, b, kind):
        return raw(a, b, kind)

    def fwd(a, b, kind):
        return raw(a, b, kind), (a, b)

    def bwd(kind, res, ct):
        a, b = res
        if kind == "nn":
            return raw(ct, b, "nt"), raw(a, ct, "tn")
        if kind == "nt":
            return raw(ct, b, "nn"), raw(ct, a, "tn")
        return raw(b, ct, "nt"), raw(a, ct, "nn")

    dot.defvjp(fwd, bwd)
    return dot


_dot1 = _make_dot(1)
_dot3 = _make_dot(3)


def _sig(v):
    return 1.0 / (1.0 + jnp.exp(-v))


def _silu(v):
    return v * _sig(v)


def _f_rms(x, g):
    return x * lax.rsqrt(jnp.mean(x * x, axis=-1, keepdims=True) + EPS) * g


def _f_gate(y, xs, z, dexp, g):
    v = (y + dexp * xs) * _silu(z)
    half = SSD_WIDTH // 2
    parts = []
    for grp in range(2):
        vg = v[:, grp * half:(grp + 1) * half]
        parts.append(vg * lax.rsqrt(jnp.mean(vg * vg, axis=-1, keepdims=True) + EPS) * g[:, grp * half:(grp + 1) * half])
    return jnp.concatenate(parts, axis=1)


def _f_ln(u, g, b):
    mu = jnp.mean(u, axis=-1, keepdims=True)
    var = jnp.mean(jnp.square(u - mu), axis=-1, keepdims=True)
    return _silu((u - mu) * lax.rsqrt(var + EPS) * g + b)


def _f_glu(a, g):
    return a * _sig(g)


def _f_swiglu(gate, up):
    return _silu(gate) * up


def _f_att(q, k, v):
    outs = []
    for h in range(X_HEADS):
        sl = slice(h * X_HEAD_DIM, (h + 1) * X_HEAD_DIM)
        s = _dot1(q[:, sl], k[:, sl], "nt") * (X_HEAD_DIM ** -0.5)
        s = s - lax.stop_gradient(jnp.max(s, axis=-1, keepdims=True))
        p = jnp.exp(s)
        p = p / jnp.sum(p, axis=-1, keepdims=True)
        outs.append(_dot1(p, v[:, sl], "nn"))
    return jnp.concatenate(outs, axis=1)


def _loss_bwd(x3, target, g, *, name, ts=256):
    s, d = x3.shape

    def f(x, t, gv):
        return 0.5 * jnp.sum(jnp.mean(jnp.square(_f_rms(x, gv) - t), axis=-1))

    def body(x_ref, t_ref, g_ref, dx_ref, dg_ref, l_ref):
        @pl.when(pl.program_id(0) == 0)
        def _():
            dg_ref[...] = jnp.zeros_like(dg_ref)
            l_ref[...] = jnp.zeros_like(l_ref)

        lv, (dx, dg) = jax.value_and_grad(f, argnums=(0, 2))(x_ref[...], t_ref[...], g_ref[...])
        dx_ref[...] = dx
        dg_ref[...] += dg
        l_ref[...] += lv

    row = pl.BlockSpec((ts, d), lambda i: (i, 0))
    return pl.pallas_call(
        body, name=name, grid=(s // ts,),
        in_specs=[row, row, pl.BlockSpec((1, d), lambda i: (0, 0))],
        out_specs=[row, pl.BlockSpec((1, d), lambda i: (0, 0)), pl.BlockSpec((SUBLANES, LANES), lambda i: (0, 0))],
        out_shape=[jax.ShapeDtypeStruct((s, d), _F32), jax.ShapeDtypeStruct((1, d), _F32),
                   jax.ShapeDtypeStruct((SUBLANES, LANES), _F32)],
        compiler_params=_params(("arbitrary",)),
    )(x3, target, g)


_CONV_PAD = 32
_CONV_ROWS = 128
_CONV_CB = 128


def _conv_taps(k_taps):
    groups = {}
    for k in range(k_taps):
        j = k_taps - 1 - k
        groups.setdefault(j % SUBLANES, []).append((k, j))
    return groups


def _conv_fwd(x, w, b, k_taps, *, name):
    s, c = x.shape
    kp = w.shape[0]
    pad, rows, cb = _CONV_PAD, _CONV_ROWS, _CONV_CB
    groups = _conv_taps(k_taps)

    def body(x_ref, w_ref, b_ref, o_ref, xp_ref):
        xp_ref[0:pad, :] = jnp.zeros((pad, cb), _F32)
        xp_ref[pad:pad + s, :] = x_ref[...]
        wv = w_ref[...]
        bias = jnp.broadcast_to(b_ref[...], (rows, cb))

        def chunk(r, carry):
            base = pl.multiple_of(r * rows, rows)
            win = xp_ref[pl.ds(base, rows + pad), :]
            acc = bias
            for rot, taps in groups.items():
                rolled = win if rot == 0 else pltpu.roll(win, rot, 0)
                for k, j in taps:
                    off = pad - (j - rot)
                    acc = acc + rolled[off:off + rows, :] * wv[k:k + 1, :]
            o_ref[pl.ds(base, rows), :] = acc
            return carry

        lax.fori_loop(0, s // rows, chunk, 0)

    col = pl.BlockSpec((s, cb), lambda i: (0, i))
    return pl.pallas_call(
        body, name=name, grid=(c // cb,),
        in_specs=[col, pl.BlockSpec((kp, cb), lambda i: (0, i)), pl.BlockSpec((1, cb), lambda i: (0, i))],
        out_specs=col, out_shape=jax.ShapeDtypeStruct((s, c), _F32),
        scratch_shapes=[pltpu.VMEM((s + pad, cb), _F32)],
        compiler_params=_params(("parallel",)),
    )(x, w, b)


def _conv_bwd(x, w, dy, k_taps, *, name):
    s, c = x.shape
    kp = w.shape[0]
    pad, rows, cb = _CONV_PAD, _CONV_ROWS, _CONV_CB
    groups = _conv_taps(k_taps)
    win_rows = rows + pad

    def fold(v):
        acc = v[0:SUBLANES, :]
        for i in range(1, rows // SUBLANES):
            acc = acc + v[i * SUBLANES:(i + 1) * SUBLANES, :]
        return acc

    def body(x_ref, w_ref, dy_ref, dx_ref, dw_ref, db_ref, xp_ref, dyp_ref, acc_ref, dbacc_ref):
        xp_ref[0:pad, :] = jnp.zeros((pad, cb), _F32)
        xp_ref[pad:pad + s, :] = x_ref[...]
        dyp_ref[0:s, :] = dy_ref[...]
        dyp_ref[s:s + pad, :] = jnp.zeros((pad, cb), _F32)
        acc_ref[...] = jnp.zeros_like(acc_ref)
        dbacc_ref[...] = jnp.zeros_like(dbacc_ref)
        wv = w_ref[...]

        def chunk(r, carry):
            base = pl.multiple_of(r * rows, rows)
            xwin = xp_ref[pl.ds(base, win_rows), :]
            dwin = dyp_ref[pl.ds(base, win_rows), :]
            dyc = dwin[0:rows, :]
            dxacc = jnp.zeros((rows, cb), _F32)
            for rot, taps in groups.items():
                xr = xwin if rot == 0 else pltpu.roll(xwin, rot, 0)
                dr = dwin if rot == 0 else pltpu.roll(dwin, win_rows - rot, 0)
                for k, j in taps:
                    a8 = j - rot
                    dxacc = dxacc + dr[a8:a8 + rows, :] * wv[k:k + 1, :]
                    prod = dyc * xr[pad - a8:pad - a8 + rows, :]
                    acc_ref[k * SUBLANES:(k + 1) * SUBLANES, :] += fold(prod)
            dbacc_ref[...] += fold(dyc)
            dx_ref[pl.ds(base, rows), :] = dxacc
            return carry

        lax.fori_loop(0, s // rows, chunk, 0)
        dw_ref[...] = jnp.zeros_like(dw_ref)
        for k in range(k_taps):
            dw_ref[k:k + 1, :] = jnp.sum(acc_ref[k * SUBLANES:(k + 1) * SUBLANES, :], axis=0, keepdims=True)
        db_ref[...] = jnp.sum(dbacc_ref[...], axis=0, keepdims=True)

    col = pl.BlockSpec((s, cb), lambda i: (0, i))
    wspec = pl.BlockSpec((kp, cb), lambda i: (0, i))
    bspec = pl.BlockSpec((1, cb), lambda i: (0, i))
    return pl.pallas_call(
        body, name=name, grid=(c // cb,),
        in_specs=[col, wspec, col], out_specs=[col, wspec, bspec],
        out_shape=[jax.ShapeDtypeStruct((s, c), _F32), jax.ShapeDtypeStruct((kp, c), _F32),
                   jax.ShapeDtypeStruct((1, c), _F32)],
        scratch_shapes=[pltpu.VMEM((s + pad, cb), _F32), pltpu.VMEM((s + pad, cb), _F32),
                        pltpu.VMEM((kp * SUBLANES, cb), _F32), pltpu.VMEM((SUBLANES, cb), _F32)],
        compiler_params=_params(("parallel",)),
    )(x, w, dy)


def _tri_sum(v, lower):
    l = v.shape[0]
    r, c = lax.broadcasted_iota(jnp.int32, (l, l), 0), lax.broadcasted_iota(jnp.int32, (l, l), 1)
    tri = ((r >= c) if lower else (r <= c)).astype(jnp.bfloat16)
    hi = v.astype(jnp.bfloat16)
    r1 = v - hi.astype(_F32)
    mid = r1.astype(jnp.bfloat16)
    lo = (r1 - mid.astype(_F32)).astype(jnp.bfloat16)
    out = jnp.zeros_like(v)
    for part in (hi, mid, lo):
        out = out + lax.dot_general(tri, part, _DN["nn"], preferred_element_type=_F32)
    return out


@jax.custom_vjp
def _cumsum_rows(v):
    return _tri_sum(v, True)


_cumsum_rows.defvjp(lambda v: (_tri_sum(v, True), None), lambda _, ct: (_tri_sum(ct, False),))


def _ssd_chunk(xbc, dtraw, prev, bias, alog):
    l = xbc.shape[0]
    xs = xbc[:, :SSD_WIDTH]
    bm = xbc[:, SSD_WIDTH:SSD_WIDTH + 2 * SSD_STATE]
    cm = xbc[:, SSD_WIDTH + 2 * SSD_STATE:]
    v = dtraw + bias
    dt = jnp.maximum(v, 0.0) + jnp.log1p(jnp.exp(-jnp.abs(v)))
    a_neg = -jnp.exp(alog)
    acs = _cumsum_rows(dt * a_neg)
    acs_t = acs.T
    dt_t = dt.T
    total = acs[l - 1:l, :]
    row = lax.broadcasted_iota(jnp.int32, (l, l), 0)
    colv = lax.broadcasted_iota(jnp.int32, (l, l), 1)
    causal = row >= colv
    lane_lo = lax.broadcasted_iota(jnp.int32, (l, LANES), 1) < HEAD_DIM
    row_lo = lax.broadcasted_iota(jnp.int32, (LANES, SSD_STATE), 0) < HEAD_DIM

    def pair_lanes(m, h0):
        return jnp.where(lane_lo, m[:, h0:h0 + 1], m[:, h0 + 1:h0 + 2])

    ys, news = [], []
    cb = {}
    for j in range(SSD_HEADS // 2):
        h0 = 2 * j
        grp = h0 // (SSD_HEADS // 2)
        bg = bm[:, grp * SSD_STATE:(grp + 1) * SSD_STATE]
        cg = cm[:, grp * SSD_STATE:(grp + 1) * SSD_STATE]
        if grp not in cb:
            cb[grp] = _dot1(cg, bg, "nt")
        xp = xs[:, j * LANES:(j + 1) * LANES]
        y = jnp.zeros((l, LANES), _F32)
        for hh, mask in ((h0, lane_lo), (h0 + 1, jnp.logical_not(lane_lo))):
            seg = acs[:, hh:hh + 1] - acs_t[hh:hh + 1, :]
            dec = jnp.exp(jnp.where(causal, seg, -jnp.inf))
            sc = cb[grp] * dec * dt_t[hh:hh + 1, :]
            y = y + _dot1(sc, jnp.where(mask, xp, 0.0), "nn")
        acs_p = pair_lanes(acs, h0)
        prev_p = prev[j * LANES:(j + 1) * LANES, :]
        y = y + _dot1(cg, prev_p, "nt") * jnp.exp(acs_p)
        wgt = jnp.exp(pair_lanes(jnp.broadcast_to(total, (l, LANES)), h0) - acs_p) * pair_lanes(dt, h0)
        st = _dot1(xp * wgt, bg, "tn")
        cdec = jnp.exp(jnp.where(row_lo, total[:, h0:h0 + 1], total[:, h0 + 1:h0 + 2]))
        news.append(prev_p * cdec + st)
        ys.append(y)
    return jnp.concatenate(ys, axis=1), jnp.concatenate(news, axis=0)


def _ssd_fwd(xbc, dtraw, bias, alog, *, name):
    s = xbc.shape[0]
    nc = s // CHUNK
    nstate = SSD_HEADS * HEAD_DIM

    def body(x_ref, dt_ref, b_ref, a_ref, y_ref, st_ref, state_ref):
        @pl.when(pl.program_id(0) == 0)
        def _():
            state_ref[...] = jnp.zeros_like(state_ref)

        prev = state_ref[...]
        st_ref[...] = prev
        y, new = _ssd_chunk(x_ref[...], dt_ref[...], prev, b_ref[...], a_ref[...])
        y_ref[...] = y
        state_ref[...] = new

    small = pl.BlockSpec((1, LANES), lambda i: (0, 0))
    return pl.pallas_call(
        body, name=name, grid=(nc,),
        in_specs=[pl.BlockSpec((CHUNK, XBC_WIDTH), lambda i: (i, 0)), pl.BlockSpec((CHUNK, LANES), lambda i: (i, 0)),
                  small, small],
        out_specs=[pl.BlockSpec((CHUNK, SSD_WIDTH), lambda i: (i, 0)),
                   pl.BlockSpec((None, nstate, SSD_STATE), lambda i: (i, 0, 0))],
        out_shape=[jax.ShapeDtypeStruct((s, SSD_WIDTH), _F32), jax.ShapeDtypeStruct((nc, nstate, SSD_STATE), _F32)],
        scratch_shapes=[pltpu.VMEM((nstate, SSD_STATE), _F32)],
        compiler_params=_params(("arbitrary",)),
    )(xbc, dtraw, bias, alog)


def _ssd_bwd(xbc, dtraw, states, bias, alog, dy, dxs_extra, *, name):
    s = xbc.shape[0]
    nc = s // CHUNK
    nstate = SSD_HEADS * HEAD_DIM

    def body(x_ref, dt_ref, st_ref, b_ref, a_ref, dy_ref, ex_ref, dx_ref, ddt_ref, db_ref, da_ref, dstate_ref):
        @pl.when(pl.program_id(0) == 0)
        def _():
            dstate_ref[...] = jnp.zeros_like(dstate_ref)
            db_ref[...] = jnp.zeros_like(db_ref)
            da_ref[...] = jnp.zeros_like(da_ref)

        _, vjp = jax.vjp(_ssd_chunk, x_ref[...], dt_ref[...], st_ref[...], b_ref[...], a_ref[...])
        dx, ddt, dprev, db, da = vjp((dy_ref[...], dstate_ref[...]))
        dx_ref[:, :SSD_WIDTH] = dx[:, :SSD_WIDTH] + ex_ref[...]
        dx_ref[:, SSD_WIDTH:] = dx[:, SSD_WIDTH:]
        ddt_ref[...] = ddt
        db_ref[...] += db
        da_ref[...] += da
        dstate_ref[...] = dprev

    rev = lambda i: (nc - 1 - i, 0)
    small = pl.BlockSpec((1, LANES), lambda i: (0, 0))
    return pl.pallas_call(
        body, name=name, grid=(nc,),
        in_specs=[pl.BlockSpec((CHUNK, XBC_WIDTH), rev), pl.BlockSpec((CHUNK, LANES), rev),
                  pl.BlockSpec((None, nstate, SSD_STATE), lambda i: (nc - 1 - i, 0, 0)), small, small,
                  pl.BlockSpec((CHUNK, SSD_WIDTH), rev), pl.BlockSpec((CHUNK, SSD_WIDTH), rev)],
        out_specs=[pl.BlockSpec((CHUNK, XBC_WIDTH), rev), pl.BlockSpec((CHUNK, LANES), rev), small, small],
        out_shape=[jax.ShapeDtypeStruct((s, XBC_WIDTH), _F32), jax.ShapeDtypeStruct((s, LANES), _F32),
                   jax.ShapeDtypeStruct((1, LANES), _F32), jax.ShapeDtypeStruct((1, LANES), _F32)],
        scratch_shapes=[pltpu.VMEM((nstate, SSD_STATE), _F32)],
        compiler_params=_params(("arbitrary",)),
    )(xbc, dtraw, states, bias, alog, dy, dxs_extra)


def _pad_cols(a, width):
    return jnp.pad(a, ((0, 0), (0, width - a.shape[1])))


def _pad_rows(a, rows):
    return jnp.pad(a, ((0, rows - a.shape[0]), (0, 0)))


def _tie(a, token):
    return a + token[0:1, 0:1].astype(a.dtype)


def _local_step(x, mem, target, w, fetch, emit):
    bf = _MXU
    d = D_MODEL
    h = _row_fwd(_f_rms, [x], [w['norm_mix_g']], [(d, bf)], name="f_norm_mix")
    w_in = fetch('in', h)['w_in']
    z_end, xbc_end, dt_end = SSD_WIDTH, SSD_WIDTH + XBC_WIDTH, SSD_WIDTH + XBC_WIDTH + SSD_HEADS
    w_z, w_xbc = w_in[:z_end], w_in[z_end:xbc_end]
    w_dt = _pad_rows(w_in[xbc_end:dt_end], LANES)
    w_a, w_g = w_in[dt_end:dt_end + CF_WIDTH], w_in[dt_end + CF_WIDTH:]
    dt_bias = _pad_cols(w['ssd_dt_bias'], LANES)
    a_log = _pad_cols(w['ssd_A_log'], LANES)
    d_exp = jnp.repeat(w['ssd_D'], HEAD_DIM, axis=1)
    g_final = w['norm_final_g'].reshape(1, D_MODEL)

    z = _mm(h, w_z, tb=True, out_dtype=bf, name="f_in_z")
    xbc = _mm(h, w_xbc, tb=True, name="f_in_xbc")
    dtr = _mm(h, w_dt, tb=True, name="f_in_dt")
    ga = _mm(h, w_a, tb=True, out_dtype=bf, name="f_in_a")
    gg = _mm(h, w_g, tb=True, out_dtype=bf, name="f_in_g")
    wc = fetch('conv', xbc)
    ssd_w = _pad_rows(wc['ssd_conv_w'], SUBLANES)
    cf_w = _pad_rows(wc['cf_conv_w'], 32)
    xbc_c = _conv_fwd(xbc, ssd_w, w['ssd_conv_b'], SSD_CONV, name="f_ssd_conv")
    xbc_a = _row_fwd(_silu, [xbc_c], [], [(XBC_WIDTH, _F32)], name="f_ssd_silu")
    y_ssd, states = _ssd_fwd(xbc_a, dtr, dt_bias, a_log, name="f_ssd")
    xs_win = (xbc_a, SSD_WIDTH, 0)
    y_n = _row_fwd(_f_gate, [y_ssd, xs_win, z], [d_exp, w['ssd_norm_g']], [(d, bf)], name="f_ssd_gate")
    u_pre = _row_fwd(_f_glu, [ga, gg], [], [(CF_WIDTH, _F32)], name="f_glu")
    u_c = _conv_fwd(u_pre, cf_w, w['cf_conv_b'], CF_CONV, name="f_cf_conv")
    u = _row_fwd(_f_ln, [u_c], [w['cf_ln_g'], w['cf_ln_b']], [(d, bf)], name="f_cf_ln")
    wm = fetch('mid', y_n)
    w_out_y, w_out_u = wm['w_out'][:SSD_WIDTH], wm['w_out'][SSD_WIDTH:]
    x1 = _mm(y_n, w_out_y, add=x, name="f_out_y")
    x1 = _mm(u, w_out_u, add=x1, name="f_out_u")
    hq = _row_fwd(_f_rms, [x1], [w['norm_xattn_g']], [(d, bf)], name="f_norm_xattn")
    q = _mm(hq, wm['w_q'], out_dtype=bf, name="f_q")
    memn = _row_fwd(_f_rms, [mem], [w['norm_mem_g']], [(d, bf)], name="f_norm_mem")
    kv = _mm(memn, wm['w_kv'], name="f_kv")
    k_mat, v_mat = kv[:, :d], kv[:, d:]
    o = _row_fwd(_f_att, [q], [k_mat, v_mat], [(d, bf)], name="f_att")
    x2 = _mm(o, wm['w_o'], add=x1, name="f_o")
    hf = _row_fwd(_f_rms, [x2], [w['norm_ffn_g']], [(d, bf)], name="f_norm_ffn")
    wf = fetch('ffn', hf)
    gate = _mm(hf, wf['w_gate'], tb=True, out_dtype=bf, name="f_gate")
    up = _mm(hf, wf['w_up'], tb=True, out_dtype=bf, name="f_up")
    act = _row_fwd(_f_swiglu, [gate, up], [], [(D_FF, bf)], name="f_swiglu")
    x3 = _mm(act, wf['w_down'], add=x2, name="f_down")

    dx3, dg_final, loss = _loss_bwd(x3, target, g_final, name="b_loss")
    g = {'norm_final_g': dg_final.reshape(d)}

    dact = _mm(dx3, wf['w_down'], tb=True, out_dtype=bf, name="b_down_x")
    dw_down = _mm(act, dx3, ta=True, name="b_down_w")
    (dgate, dup), _ = _row_bwd(_f_swiglu, [gate, up], [], [dact], row_dtypes=[bf, bf], name="b_swiglu")
    sent = emit({'w_down': dw_down, 'w_gate': _mm(dgate, hf, ta=True, name="b_gate_w"),
                 'w_up': _mm(dup, hf, ta=True, name="b_up_w")})
    dhf = _mm(dgate, wf['w_gate'], name="b_gate_x")
    dhf = _mm(dup, wf['w_up'], add=dhf, out_dtype=bf, name="b_up_x")
    (dx2,), (g['norm_ffn_g'],) = _row_bwd(_f_rms, [x2], [_tie(w['norm_ffn_g'], sent)], [dhf], adds={0: dx3}, name="b_norm_ffn")

    do = _mm(dx2, wm['w_o'], tb=True, out_dtype=bf, name="b_o_x")
    dw_o = _mm(o, dx2, ta=True, name="b_o_w")
    (dq,), (dk, dv) = _row_bwd(_f_att, [q], [k_mat, v_mat], [do], row_dtypes=[bf], name="b_att")
    dw_q = _mm(hq, dq, ta=True, name="b_q_w")
    dhq = _mm(dq, wm['w_q'], tb=True, out_dtype=bf, name="b_q_x")
    (dx1,), (g['norm_xattn_g'],) = _row_bwd(_f_rms, [x1], [w['norm_xattn_g']], [dhq], adds={0: dx2}, name="b_norm_xattn")
    dkv = jnp.concatenate([dk, dv], axis=1)
    sent = emit({'w_o': dw_o, 'w_q': dw_q, 'w_kv': _mm(memn, dkv, ta=True, name="b_kv_w")})
    dmemn = _mm(dkv, wm['w_kv'], tb=True, name="b_kv_x")
    _, (g['norm_mem_g'],) = _row_bwd(_f_rms, [mem], [w['norm_mem_g']], [dmemn], need=[False], name="b_norm_mem")

    dyn = _mm(dx1, w_out_y, tb=True, out_dtype=bf, name="b_out_y_x")
    du = _mm(dx1, w_out_u, tb=True, out_dtype=bf, name="b_out_u_x")
    (du_c,), (g['cf_ln_g'], g['cf_ln_b']) = _row_bwd(_f_ln, [u_c], [_tie(w['cf_ln_g'], sent), w['cf_ln_b']], [du], name="b_cf_ln")
    sent = emit({'w_out': jnp.concatenate([_mm(y_n, dx1, ta=True, name="b_out_y_w"), _mm(u, dx1, ta=True, name="b_out_u_w")], axis=0)})
    du_pre, dcf_w, g['cf_conv_b'] = _conv_bwd(u_pre, cf_w, du_c, CF_CONV, name="b_cf_conv")
    g['cf_conv_w'] = dcf_w[:CF_CONV]
    (dga, dgg), _ = _row_bwd(_f_glu, [ga, gg], [], [du_pre], row_dtypes=[bf, bf], name="b_glu")
    (dy_ssd, dxs, dz), (dd_exp, g['ssd_norm_g']) = _row_bwd(
        _f_gate, [y_ssd, xs_win, z], [d_exp, _tie(w['ssd_norm_g'], sent)], [dyn], row_dtypes=[_F32, _F32, bf], name="b_ssd_gate")
    g['ssd_D'] = jnp.sum(dd_exp.reshape(SSD_HEADS, HEAD_DIM), axis=1).reshape(1, SSD_HEADS)
    dxbc_a, ddtr, ddt_bias, da_log = _ssd_bwd(xbc_a, dtr, states, dt_bias, a_log, dy_ssd, dxs, name="b_ssd")
    g['ssd_dt_bias'] = ddt_bias[:, :SSD_HEADS]
    g['ssd_A_log'] = da_log[:, :SSD_HEADS]
    (dxbc_c,), _ = _row_bwd(_silu, [xbc_c], [], [dxbc_a], name="b_ssd_silu")
    dxbc, dssd_w, g['ssd_conv_b'] = _conv_bwd(xbc, ssd_w, dxbc_c, SSD_CONV, name="b_ssd_conv")
    g['ssd_conv_w'] = dssd_w[:SSD_CONV]

    sent = emit({'w_in': jnp.concatenate([
        _mm(dz, h, ta=True, name="b_in_z_w"), _mm(dxbc, h, ta=True, name="b_in_xbc_w"),
        _mm(ddtr, h, ta=True, name="b_in_dt_w")[:SSD_HEADS],
        _mm(dga, h, ta=True, name="b_in_a_w"), _mm(dgg, h, ta=True, name="b_in_g_w")], axis=0)})
    dh = _mm(dz, _tie(w_z, sent), name="b_in_z_x")
    dh = _mm(dxbc, w_xbc, add=dh, name="b_in_xbc_x")
    dh = _mm(ddtr, w_dt, add=dh, name="b_in_dt_x")
    dh = _mm(dga, w_a, add=dh, name="b_in_a_x")
    dh = _mm(dgg, w_g, add=dh, out_dtype=bf, name="b_in_g_x")
    (dx,), (g['norm_mix_g'],) = _row_bwd(_f_rms, [x], [w['norm_mix_g']], [dh], adds={0: dx1}, name="b_norm_mix")
    return loss, dx, g


_ANY = pl.BlockSpec(memory_space=pl.ANY)


def _place():
    x, y, c = lax.axis_index("x"), lax.axis_index("y"), lax.axis_index("c")
    return x, y, c


def _all_gather(arrs, *, name):
    n = len(arrs)

    def body(*refs):
        ins, outs = refs[:n], refs[n:2 * n]
        send_sems, recv_sems, local_sems = refs[2 * n:]
        x, y, c = _place()
        me, sibling = (x, y, c), (x, y, 1 - c)
        chips = [(1 - x, y), (x, 1 - y), (1 - x, 1 - y)]

        def slot(a, dev):
            return outs[a].at[4 * dev[0] + 2 * dev[1] + dev[2]]

        def copy(a, k, block, to, src=None):
            return pltpu.make_async_remote_copy(
                src_ref=slot(a, block) if src is None else src, dst_ref=slot(a, block),
                send_sem=send_sems.at[a, k], recv_sem=recv_sems.at[a, k], device_id=to, device_id_type=MESH)

        mine = [pltpu.make_async_copy(ins[a], slot(a, me), local_sems.at[a]) for a in range(n)]
        for cp in mine:
            cp.start()
        first = []
        for a in range(n):
            first.append(copy(a, 0, me, sibling, src=ins[a]))
            first += [copy(a, 1 + j, me, (*chip, c), src=ins[a]) for j, chip in enumerate(chips)]
        for cp in first:
            cp.start()
        passed = []
        for a in range(n):
            for j, chip in enumerate(chips):
                copy(a, 1 + j, (*chip, c), me).wait_recv()
                fwd = copy(a, 4 + j, (*chip, c), sibling)
                fwd.start()
                passed.append(fwd)
        for a in range(n):
            copy(a, 0, sibling, me).wait_recv()
            for j, chip in enumerate(chips):
                copy(a, 4 + j, (*chip, 1 - c), me).wait_recv()
        for cp in first + passed:
            cp.wait_send()
        for cp in mine:
            cp.wait()

    return pl.pallas_call(
        body, name=name, in_specs=[_ANY] * n, out_specs=[_ANY] * n,
        out_shape=[jax.ShapeDtypeStruct((N_DEV,) + a.shape, a.dtype) for a in arrs],
        scratch_shapes=[pltpu.SemaphoreType.DMA((n, 7)), pltpu.SemaphoreType.DMA((n, 7)), pltpu.SemaphoreType.DMA((n,))],
    )(*arrs)


_HBM = pl.BlockSpec(memory_space=pltpu.HBM)
_SEM = pl.BlockSpec(memory_space=pltpu.SEMAPHORE)
_EFFECT = pltpu.SideEffectType.DATAFLOW_SIDE_EFFECTING
_FLIPS = [(dx, dy, dc) for dx in (0, 1) for dy in (0, 1) for dc in (0, 1)][1:]


def _peer(flip, x, y, c):
    return (1 - x if flip[0] else x, 1 - y if flip[1] else y, 1 - c if flip[2] else c)


def _send_start(srcs, blocked, *, after=None, name):
    n = len(srcs)
    lands = [jax.ShapeDtypeStruct(s.shape if blocked else (N_DEV,) + s.shape, s.dtype) for s in srcs]
    n_in = 2 * n + (after is not None)

    def body(*refs):
        src_refs, land_refs = refs[:n], refs[n:2 * n]
        send_sems, recv_sems = refs[n_in], refs[n_in + 1]
        token = refs[-1]
        x, y, c = _place()
        me = 4 * x + 2 * y + c
        for a in range(n):
            for k, flip in enumerate(_FLIPS):
                p = _peer(flip, x, y, c)
                src = src_refs[a].at[4 * p[0] + 2 * p[1] + p[2]] if blocked else src_refs[a]
                pltpu.make_async_remote_copy(
                    src_ref=src, dst_ref=land_refs[a].at[me], send_sem=send_sems.at[7 * a + k], recv_sem=recv_sems.at[7 * a + k],
                    device_id=p, device_id_type=MESH).start()
        token[...] = jnp.zeros_like(token)

    res = pl.pallas_call(
        body, name=name,
        out_shape=(pltpu.SemaphoreType.DMA((7 * n,)), pltpu.SemaphoreType.DMA((7 * n,)),
                   *[pltpu.HBM(s.shape, s.dtype) for s in srcs], *[pltpu.HBM(l.shape, l.dtype) for l in lands],
                   jax.ShapeDtypeStruct((SUBLANES, LANES), _F32)),
        in_specs=[_HBM] * (2 * n) + [_ANY] * (after is not None),
        out_specs=(_SEM, _SEM, *[_HBM] * (2 * n), pl.BlockSpec(memory_space=pltpu.VMEM)),
        input_output_aliases={i: 2 + i for i in range(2 * n)},
        compiler_params=pltpu.CompilerParams(has_side_effects=_EFFECT),
    )(*[pltpu.with_memory_space_constraint(s, pltpu.HBM) for s in srcs],
      *[pltpu.with_memory_space_constraint(lax.empty(l.shape, l.dtype), pltpu.HBM) for l in lands],
      *([after] if after is not None else []))
    return res[0], res[1], list(res[2:2 + n]), list(res[2 + n:2 + 2 * n]), res[-1]


def _send_wait(handles, after, blocked, *, name):
    send_sems, recv_sems, srcs, lands, _ = handles
    n = len(srcs)

    def body(*refs):
        src_refs, land_refs = refs[:n], refs[n:2 * n]
        send_sems, recv_sems = refs[2 * n], refs[2 * n + 1]
        x, y, c = _place()
        for a in range(n):
            for k, flip in enumerate(_FLIPS):
                p = _peer(flip, x, y, c)
                pid = 4 * p[0] + 2 * p[1] + p[2]
                cp = pltpu.make_async_remote_copy(
                    src_ref=src_refs[a].at[pid] if blocked else src_refs[a], dst_ref=land_refs[a].at[pid],
                    send_sem=send_sems.at[7 * a + k], recv_sem=recv_sems.at[7 * a + k], device_id=p, device_id_type=MESH)
                cp.wait_send()
                cp.wait_recv()

    res = pl.pallas_call(
        body, name=name,
        out_shape=tuple(pltpu.HBM(s.shape, s.dtype) for s in srcs + lands),
        in_specs=[_HBM] * (2 * n) + [_SEM, _SEM, _ANY], out_specs=tuple([_HBM] * (2 * n)),
        input_output_aliases={i: i for i in range(2 * n)},
        compiler_params=pltpu.CompilerParams(has_side_effects=_EFFECT),
    )(*srcs, *lands, send_sems, recv_sems, after)
    return list(res[:n]), list(res[n:])


def _adamw(parts, w, m, v, *, own=None, me=None, name):
    p, r, c = parts.shape
    tr = _pick(r, (256, 176, 128, 64, 32, 16, 8))
    if own is not None:
        tc = c if tr < r else _pick(c, (256, 128))
        return _adamw_own(parts, own, me, w, m, v, tr, tc, name=name)

    def body(p_ref, w_ref, m_ref, v_ref, g_ref, d_ref, nm_ref, nv_ref):
        g = p_ref[0].astype(_F32)
        for i in range(1, p):
            g = g + p_ref[i].astype(_F32)
        _adamw_math(g, w_ref, m_ref, v_ref, g_ref, d_ref, nm_ref, nv_ref)

    blk = pl.BlockSpec((tr, c), lambda i: (i, 0))
    return pl.pallas_call(
        body, name=name, grid=(r // tr,),
        in_specs=[pl.BlockSpec((p, tr, c), lambda i: (0, i, 0)), blk, blk, blk], out_specs=[blk] * 4,
        out_shape=[jax.ShapeDtypeStruct((r, c), _F32)] * 4,
        compiler_params=_params(("parallel",)),
    )(parts, w, m, v)


def _adamw_math(g, w_ref, m_ref, v_ref, g_ref, d_ref, nm_ref, nv_ref):
    wv = w_ref[...]
    mn = ADAM_B1 * m_ref[...] + (1.0 - ADAM_B1) * g
    vn = ADAM_B2 * v_ref[...] + (1.0 - ADAM_B2) * jnp.square(g)
    m_hat = mn / (1.0 - ADAM_B1 ** ADAM_STEP)
    v_hat = vn / (1.0 - ADAM_B2 ** ADAM_STEP)
    g_ref[...] = g
    d_ref[...] = -ADAM_LR * (m_hat / (jnp.sqrt(v_hat) + ADAM_EPS) + ADAM_WD * wv)
    nm_ref[...] = mn
    nv_ref[...] = vn


def _adamw_own(parts, own, me, w, m, v, tr, tc, *, name):
    p, r, c = parts.shape

    def body(me_ref, p_ref, own_ref, w_ref, m_ref, v_ref, g_ref, d_ref, nm_ref, nv_ref):
        mine = own_ref[...].astype(_F32)
        g = jnp.where(me_ref[0] == 0, mine, p_ref[0].astype(_F32))
        for i in range(1, p):
            g = g + jnp.where(me_ref[0] == i, mine, p_ref[i].astype(_F32))
        _adamw_math(g, w_ref, m_ref, v_ref, g_ref, d_ref, nm_ref, nv_ref)

    blk = pl.BlockSpec((tr, tc), lambda i, j, me_ref: (i, j))
    grid_spec = pltpu.PrefetchScalarGridSpec(
        num_scalar_prefetch=1, grid=(r // tr, c // tc),
        in_specs=[pl.BlockSpec((p, tr, tc), lambda i, j, me_ref: (0, i, j)),
                  pl.BlockSpec((None, tr, tc), lambda i, j, me_ref: (me_ref[0], i, j)), blk, blk, blk],
        out_specs=[blk] * 4)
    return pl.pallas_call(
        body, name=name, grid_spec=grid_spec, out_shape=[jax.ShapeDtypeStruct((r, c), _F32)] * 4,
        compiler_params=_params(("parallel", "parallel")),
    )(me.reshape(1).astype(jnp.int32), parts, own, w, m, v)


def _sum_parts(parts, *, name):
    p, r, c = parts.shape

    def body(p_ref, o_ref):
        g = p_ref[0].astype(_F32)
        for i in range(1, p):
            g = g + p_ref[i].astype(_F32)
        o_ref[...] = g

    return pl.pallas_call(body, name=name, out_shape=jax.ShapeDtypeStruct((r, c), _F32))(parts)


def _pack(vals, rows):
    flat = jnp.concatenate([v.reshape(-1) for v in vals])
    return jnp.pad(flat, (0, rows * LANES - flat.shape[0])).reshape(rows, LANES)


def _unpack(packed, shapes):
    flat = packed.reshape(-1)
    out, pos = [], 0
    for shp in shapes:
        size = math.prod(shp)
        out.append(flat[pos:pos + size].reshape(shp))
        pos += size
    return out


def _pack_rows(shapes):
    total = sum(math.prod(s) for s in shapes)
    return -(-total // (LANES * SUBLANES)) * SUBLANES


def kernel(x, mem, norm_mix_g, w_in, ssd_conv_w, ssd_conv_b, ssd_dt_bias, ssd_A_log, ssd_D, ssd_norm_g, cf_conv_w, cf_conv_b, cf_ln_g, cf_ln_b, w_out, norm_xattn_g, norm_mem_g, w_q, w_kv, w_o, norm_ffn_g, w_gate, w_up, w_down, norm_final_g, loss_target, m_norm_mix_g, m_w_in, m_ssd_conv_w, m_ssd_conv_b, m_ssd_dt_bias, m_ssd_A_log, m_ssd_D, m_ssd_norm_g, m_cf_conv_w, m_cf_conv_b, m_cf_ln_g, m_cf_ln_b, m_w_out, m_norm_xattn_g, m_norm_mem_g, m_w_q, m_w_kv, m_w_o, m_norm_ffn_g, m_w_gate, m_w_up, m_w_down, m_norm_final_g, v_norm_mix_g, v_w_in, v_ssd_conv_w, v_ssd_conv_b, v_ssd_dt_bias, v_ssd_A_log, v_ssd_D, v_ssd_norm_g, v_cf_conv_w, v_cf_conv_b, v_cf_ln_g, v_cf_ln_b, v_w_out, v_norm_xattn_g, v_norm_mem_g, v_w_q, v_w_kv, v_w_o, v_norm_ffn_g, v_w_gate, v_w_up, v_w_down, v_norm_final_g):
    args = dict(locals())
    wts = {n: args[n] for n in WEIGHT_NAMES}
    mom = {n: args["m_" + n] for n in WEIGHT_NAMES}
    var = {n: args["v_" + n] for n in WEIGHT_NAMES}
    me = 4 * lax.axis_index("x") + 2 * lax.axis_index("y") + lax.axis_index("c")

    groups = {'in': ['w_in'], 'conv': ['ssd_conv_w', 'cf_conv_w'], 'mid': ['w_out', 'w_q', 'w_kv', 'w_o'],
              'ffn': ['w_gate', 'w_up', 'w_down']}
    def shard(n, a):
        return jnp.transpose(a[0], (1, 0)) if n in TRANSPOSED else a[0]

    gathers, started = {}, None
    for grp, names in groups.items():
        shards = [wts[n][0] if grp == 'conv' else shard(n, wts[n]).astype(_MXU) for n in names]
        gathers[grp] = _send_start(shards, False, after=started, name="gather_%s_start" % grp)
        started = gathers[grp][4]

    def fetch(grp, after):
        srcs, lands = _send_wait(gathers[grp], started if after is None else after, False, name="gather_%s_wait" % grp)
        out = {}
        for n, own, gth in zip(groups[grp], srcs, lands):
            gth = lax.dynamic_update_slice_in_dim(gth, own[None], me, axis=0)
            if n == 'w_kv' or grp == 'conv':
                out[n] = jnp.transpose(gth, (1, 0, 2)).reshape(gth.shape[1], N_DEV * gth.shape[2])
            else:
                out[n] = gth.reshape(N_DEV * gth.shape[1], gth.shape[2])
        return out

    exchanges = []

    def emit(grads):
        blocks = []
        for n, gw in grads.items():
            if n == 'w_kv':
                gw = jnp.transpose(gw.reshape(gw.shape[0], N_DEV, gw.shape[1] // N_DEV), (1, 0, 2))
            else:
                gw = gw.reshape(N_DEV, gw.shape[0] // N_DEV, gw.shape[1])
            blocks.append(gw.astype(jnp.bfloat16))
        first = next(iter(grads))
        exchanges.append((list(grads), _send_start(blocks, True, name="exchange_%s_start" % first), first))
        return exchanges[-1][1][4]

    full = {n: wts[n] for n in WEIGHT_NAMES if n not in BIG and n not in groups['conv']}
    full['norm_mix_g'] = _tie(norm_mix_g, started)

    loss_blk, grad_x, g = _local_step(x[0], mem[0], loss_target[0], full, fetch, emit)
    loss = lax.psum(loss_blk[0, 0], ("x", "y", "c"))

    out_g, out_d, out_m, out_v = {}, {}, {}, {}
    for names, handles, first in exchanges:
        srcs, lands = _send_wait(handles, grad_x, True, name="exchange_%s_wait" % first)
        for n, own, parts in zip(names, srcs, lands):
            res = _adamw(parts, shard(n, wts[n]), shard(n, mom[n]), shard(n, var[n]), own=own, me=me, name="adamw_" + n)
            out_g[n], out_d[n], out_m[n], out_v[n] = [(jnp.transpose(r, (1, 0)) if n in TRANSPOSED else r)[None] for r in res]

    small = [n for n in WEIGHT_NAMES if n not in BIG]
    small_shapes = [g[n].shape for n in small]
    rows = _pack_rows(small_shapes)
    (small_parts,) = _all_gather([_pack([g[n] for n in small], rows)], name="gather_small_grads")
    small_sum = dict(zip(small, _unpack(_sum_parts(small_parts, name="sum_small_grads"), small_shapes)))
    small_sum['ssd_conv_w'] = lax.dynamic_slice_in_dim(small_sum['ssd_conv_w'], me * (XBC_WIDTH // N_DEV), XBC_WIDTH // N_DEV, axis=1)[None]
    small_sum['cf_conv_w'] = lax.dynamic_slice_in_dim(small_sum['cf_conv_w'], me * (CF_WIDTH // N_DEV), CF_WIDTH // N_DEV, axis=1)[None]
    shard_shapes = [wts[n].shape for n in small]
    rows2 = _pack_rows(shard_shapes)
    res = _adamw(_pack([small_sum[n] for n in small], rows2)[None], _pack([wts[n] for n in small], rows2),
                 _pack([mom[n] for n in small], rows2), _pack([var[n] for n in small], rows2), name="adamw_small")
    for dst, packed in zip((out_g, out_d, out_m, out_v), res):
        dst.update(zip(small, _unpack(packed, shard_shapes)))

    return (loss, grad_x[None], *[out_g[n] for n in WEIGHT_NAMES], *[out_d[n] for n in WEIGHT_NAMES],
            *[out_m[n] for n in WEIGHT_NAMES], *[out_v[n] for n in WEIGHT_NAMES])
```

```python
import functools
import math

import jax
import jax.numpy as jnp
from jax import lax
from jax.experimental import pallas as pl
from jax.experimental.pallas import tpu as pltpu

_F32 = jnp.float32
_MXU = jnp.bfloat16
_PREC = None
_VMEM_LIMIT = 56 * 1024 * 1024

D_MODEL = 1024
HEAD_DIM = 64
SSD_HEADS = 16
SSD_WIDTH = 1024
SSD_STATE = 128
SSD_CONV = 4
CHUNK = 128
XBC_WIDTH = 1536
CF_WIDTH = 1024
CF_CONV = 31
X_HEADS = 4
X_HEAD_DIM = 256
D_FF = 2816
EPS = 1e-6
N_DEV = 8
LANES = 128
SUBLANES = 8

ADAM_LR = 0.001
ADAM_B1 = 0.9
ADAM_B2 = 0.999
ADAM_EPS = 1e-08
ADAM_WD = 0.01
ADAM_STEP = 10

MESH = pl.DeviceIdType.MESH
WEIGHT_NAMES = ['norm_mix_g', 'w_in', 'ssd_conv_w', 'ssd_conv_b', 'ssd_dt_bias', 'ssd_A_log', 'ssd_D', 'ssd_norm_g',
                'cf_conv_w', 'cf_conv_b', 'cf_ln_g', 'cf_ln_b', 'w_out', 'norm_xattn_g', 'norm_mem_g', 'w_q', 'w_kv',
                'w_o', 'norm_ffn_g', 'w_gate', 'w_up', 'w_down', 'norm_final_g']
BIG = ['w_in', 'w_out', 'w_q', 'w_kv', 'w_o', 'w_gate', 'w_up', 'w_down']
TRANSPOSED = ('w_in', 'w_gate', 'w_up')


def _params(sem=None):
    return pltpu.CompilerParams(dimension_semantics=sem, vmem_limit_bytes=_VMEM_LIMIT)


def _pick(n, cands):
    for c in cands:
        if n % c == 0:
            return c
    return n


def _mm(a, b, *, ta=False, tb=False, add=None, out_dtype=_F32, name):
    (kdim, m) = a.shape if ta else a.shape[::-1]
    (n, k2) = b.shape if tb else b.shape[::-1]
    assert kdim == k2, (a.shape, b.shape, ta, tb)
    if ta:
        tm = m if m <= 1024 else _pick(m, (1408, 1024, 512, 256, 128))
        tn = n if n <= 1536 else _pick(n, (1408, 1024, 512, 256, 128))
        tk = _pick(kdim, (1024, 512, 256, 128))
    else:
        tm = _pick(m, (512, 256, 128))
        tn = n if n <= 2816 else _pick(n, (1408, 1024, 512, 256, 128))
        tk = kdim if kdim <= 2816 else _pick(kdim, (1408, 1024, 512, 256, 128))
    nk = kdim // tk
    dn = (((0 if ta else 1,), (1 if tb else 0,)), ((), ()))

    def body(*refs):
        a_ref, b_ref = refs[0], refs[1]
        add_ref = refs[2] if add is not None else None
        o_ref = refs[3 if add is not None else 2]
        acc_ref = refs[-1]
        k = pl.program_id(2)
        prod = lax.dot_general(a_ref[...].astype(_MXU), b_ref[...].astype(_MXU), dn,
                               preferred_element_type=_F32, precision=_PREC)

        def finish(r):
            if add_ref is not None:
                r = r + add_ref[...].astype(_F32)
            o_ref[...] = r.astype(o_ref.dtype)

        if nk == 1:
            finish(prod)
            return

        @pl.when(k == 0)
        def _():
            acc_ref[...] = prod

        @pl.when(jnp.logical_and(k > 0, k < nk - 1))
        def _():
            acc_ref[...] += prod

        @pl.when(k == nk - 1)
        def _():
            finish(acc_ref[...] + prod)

    a_spec = pl.BlockSpec((tk, tm), lambda i, j, k: (k, i)) if ta else pl.BlockSpec((tm, tk), lambda i, j, k: (i, k))
    b_spec = pl.BlockSpec((tn, tk), lambda i, j, k: (j, k)) if tb else pl.BlockSpec((tk, tn), lambda i, j, k: (k, j))
    o_spec = pl.BlockSpec((tm, tn), lambda i, j, k: (i, j))
    ins, specs = [a, b], [a_spec, b_spec]
    if add is not None:
        ins.append(add)
        specs.append(o_spec)
    return pl.pallas_call(
        body, name=name, grid=(m // tm, n // tn, nk), in_specs=specs, out_specs=o_spec,
        out_shape=jax.ShapeDtypeStruct((m, n), out_dtype),
        scratch_shapes=[pltpu.VMEM((tm, tn), _F32)] if nk > 1 else [],
        compiler_params=_params(("parallel", "parallel", "arbitrary")),
    )(*ins)


def _row_spec(r, ts):
    if isinstance(r, tuple):
        arr, width, cblk = r
        return arr, pl.BlockSpec((ts, width), lambda i, cblk=cblk: (i, cblk))
    return r, pl.BlockSpec((ts, r.shape[1]), lambda i: (i, 0))


def _tup(v):
    return tuple(v) if isinstance(v, (tuple, list)) else (v,)


def _row_fwd(f, rows, params, outs, *, name, ts=256):
    arrs, specs = zip(*[_row_spec(r, ts) for r in rows])
    s = arrs[0].shape[0]
    ts = min(ts, s)
    n_r, n_p = len(rows), len(params)

    def body(*refs):
        rv = [r[...].astype(_F32) for r in refs[:n_r]]
        pv = [p[...] for p in refs[n_r:n_r + n_p]]
        res = _tup(f(*rv, *pv))
        for o_ref, v in zip(refs[n_r + n_p:], res):
            o_ref[...] = v.astype(o_ref.dtype)

    res = pl.pallas_call(
        body, name=name, grid=(s // ts,),
        in_specs=list(specs) + [pl.BlockSpec(p.shape, lambda i: (0, 0)) for p in params],
        out_specs=[pl.BlockSpec((ts, w), lambda i: (i, 0)) for w, _ in outs],
        out_shape=[jax.ShapeDtypeStruct((s, w), dt) for w, dt in outs],
        compiler_params=_params(("parallel",)),
    )(*arrs, *params)
    return res[0] if len(outs) == 1 else res


def _row_bwd(f, rows, params, cts, *, need=None, adds=None, row_dtypes=None, name, ts=256):
    arrs, specs = zip(*[_row_spec(r, ts) for r in rows])
    s = arrs[0].shape[0]
    ts = min(ts, s)
    n_r, n_p, n_c = len(rows), len(params), len(cts)
    need = [True] * n_r if need is None else need
    adds = {} if adds is None else adds
    add_keys = sorted(adds)
    row_dtypes = [_F32] * n_r if row_dtypes is None else row_dtypes
    needed = [j for j in range(n_r) if need[j]]
    widths = [specs[j].block_shape[1] for j in range(n_r)]

    def body(*refs):
        pos = 0
        r_refs = refs[pos:pos + n_r]; pos += n_r
        p_refs = refs[pos:pos + n_p]; pos += n_p
        c_refs = refs[pos:pos + n_c]; pos += n_c
        a_refs = refs[pos:pos + len(add_keys)]; pos += len(add_keys)
        dr_refs = refs[pos:pos + len(needed)]; pos += len(needed)
        dp_refs = refs[pos:pos + n_p]
        rv = [r[...].astype(_F32) for r in r_refs]
        pv = [p[...] for p in p_refs]
        _, vjp = jax.vjp(lambda *a: _tup(f(*a)), *rv, *pv)
        g = vjp(tuple(c[...].astype(_F32) for c in c_refs))
        for o_ref, j in zip(dr_refs, needed):
            v = g[j]
            if j in adds:
                v = v + a_refs[add_keys.index(j)][...].astype(_F32)
            o_ref[...] = v.astype(o_ref.dtype)
        if n_p:
            @pl.when(pl.program_id(0) == 0)
            def _():
                for dp in dp_refs:
                    dp[...] = jnp.zeros_like(dp)
            for dp, v in zip(dp_refs, g[n_r:]):
                dp[...] += v

    ct_specs = [pl.BlockSpec((ts, c.shape[1]), lambda i: (i, 0)) for c in cts]
    add_specs = [pl.BlockSpec((ts, adds[j].shape[1]), lambda i: (i, 0)) for j in add_keys]
    res = pl.pallas_call(
        body, name=name, grid=(s // ts,),
        in_specs=list(specs) + [pl.BlockSpec(p.shape, lambda i: (0, 0)) for p in params] + ct_specs + add_specs,
        out_specs=[pl.BlockSpec((ts, widths[j]), lambda i: (i, 0)) for j in needed]
        + [pl.BlockSpec(p.shape, lambda i: (0, 0)) for p in params],
        out_shape=[jax.ShapeDtypeStruct((s, widths[j]), row_dtypes[j]) for j in needed]
        + [jax.ShapeDtypeStruct(p.shape, _F32) for p in params],
        compiler_params=_params(("arbitrary",)),
    )(*arrs, *params, *cts, *[adds[j] for j in add_keys])
    return list(res[:len(needed)]), list(res[len(needed):])


_DN = {"nn": (((1,), (0,)), ((), ())), "nt": (((1,), (1,)), ((), ())), "tn": (((0,), (0,)), ((), ()))}


def _make_dot(passes):
    def raw(a, b, kind):
        dn = _DN[kind]
        if passes == 1 or _MXU == _F32:
            return lax.dot_general(a.astype(_MXU), b.astype(_MXU), dn, preferred_element_type=_F32, precision=_PREC)
        a_hi, b_hi = a.astype(_MXU), b.astype(_MXU)
        a_lo = (a - a_hi.astype(_F32)).astype(_MXU)
        b_lo = (b - b_hi.astype(_F32)).astype(_MXU)
        out = lax.dot_general(a_hi, b_hi, dn, preferred_element_type=_F32)
        out = out + lax.dot_general(a_lo, b_hi, dn, preferred_element_type=_F32)
        return out + lax.dot_general(a_hi, b_lo, dn, preferred_element_type=_F32)

    @functools.partial(jax.custom_vjp, nondiff_argnums=(2,))
    def dot(a, b, kind):
        return raw(a, b, kind)

    def fwd(a, b, kind):
        return raw(a, b, kind), (a, b)

    def bwd(kind, res, ct):
        a, b = res
        if kind == "nn":
            return raw(ct, b, "nt"), raw(a, ct, "tn")
        if kind == "nt":
            return raw(ct, b, "nn"), raw(ct, a, "tn")
        return raw(b, ct, "nt"), raw(a, ct, "nn")

    dot.defvjp(fwd, bwd)
    return dot


_dot1 = _make_dot(1)
_dot3 = _make_dot(3)


def _sig(v):
    return 1.0 / (1.0 + jnp.exp(-v))


def _silu(v):
    return v * _sig(v)


def _f_rms(x, g):
    return x * lax.rsqrt(jnp.mean(x * x, axis=-1, keepdims=True) + EPS) * g


def _f_gate(y, xs, z, dexp, g):
    v = (y + dexp * xs) * _silu(z)
    half = SSD_WIDTH // 2
    parts = []
    for grp in range(2):
        vg = v[:, grp * half:(grp + 1) * half]
        parts.append(vg * lax.rsqrt(jnp.mean(vg * vg, axis=-1, keepdims=True) + EPS) * g[:, grp * half:(grp + 1) * half])
    return jnp.concatenate(parts, axis=1)


def _f_ln(u, g, b):
    mu = jnp.mean(u, axis=-1, keepdims=True)
    var = jnp.mean(jnp.square(u - mu), axis=-1, keepdims=True)
    return _silu((u - mu) * lax.rsqrt(var + EPS) * g + b)


def _f_glu(a, g):
    return a * _sig(g)


def _f_swiglu(gate, up):
    return _silu(gate) * up


def _f_att(q, k, v):
    outs = []
    for h in range(X_HEADS):
        sl = slice(h * X_HEAD_DIM, (h + 1) * X_HEAD_DIM)
        s = _dot1(q[:, sl], k[:, sl], "nt") * (X_HEAD_DIM ** -0.5)
        s = s - lax.stop_gradient(jnp.max(s, axis=-1, keepdims=True))
        p = jnp.exp(s)
        p = p / jnp.sum(p, axis=-1, keepdims=True)
        outs.append(_dot1(p, v[:, sl], "nn"))
    return jnp.concatenate(outs, axis=1)


def _loss_bwd(x3, target, g, *, name, ts=256):
    s, d = x3.shape

    def f(x, t, gv):
        return 0.5 * jnp.sum(jnp.mean(jnp.square(_f_rms(x, gv) - t), axis=-1))

    def body(x_ref, t_ref, g_ref, dx_ref, dg_ref, l_ref):
        @pl.when(pl.program_id(0) == 0)
        def _():
            dg_ref[...] = jnp.zeros_like(dg_ref)
            l_ref[...] = jnp.zeros_like(l_ref)

        lv, (dx, dg) = jax.value_and_grad(f, argnums=(0, 2))(x_ref[...], t_ref[...], g_ref[...])
        dx_ref[...] = dx
        dg_ref[...] += dg
        l_ref[...] += lv

    row = pl.BlockSpec((ts, d), lambda i: (i, 0))
    return pl.pallas_call(
        body, name=name, grid=(s // ts,),
        in_specs=[row, row, pl.BlockSpec((1, d), lambda i: (0, 0))],
        out_specs=[row, pl.BlockSpec((1, d), lambda i: (0, 0)), pl.BlockSpec((SUBLANES, LANES), lambda i: (0, 0))],
        out_shape=[jax.ShapeDtypeStruct((s, d), _F32), jax.ShapeDtypeStruct((1, d), _F32),
                   jax.ShapeDtypeStruct((SUBLANES, LANES), _F32)],
        compiler_params=_params(("arbitrary",)),
    )(x3, target, g)


_CONV_PAD = 32
_CONV_ROWS = 128
_CONV_CB = 128


def _conv_taps(k_taps):
    groups = {}
    for k in range(k_taps):
        j = k_taps - 1 - k
        groups.setdefault(j % SUBLANES, []).append((k, j))
    return groups


def _conv_window(win, wv, groups, init):
    pad, rows = _CONV_PAD, _CONV_ROWS
    acc = init
    for rot, taps in groups.items():
        rolled = win if rot == 0 else pltpu.roll(win, rot, 0)
        for k, j in taps:
            off = pad - (j - rot)
            acc = acc + rolled[off:off + rows, :] * wv[k:k + 1, :]
    return acc


def _conv_fill(x_refs, xp_ref, s, glu):
    pad, cb = _CONV_PAD, _CONV_CB
    step = _pick(s, (512, 256, _CONV_ROWS))
    xp_ref[0:pad, :] = jnp.zeros((pad, cb), _F32)

    def fill(r, carry):
        base = pl.multiple_of(r * step, step)
        v = x_refs[0][pl.ds(base, step), :].astype(_F32)
        if glu:
            v = v * _sig(x_refs[1][pl.ds(base, step), :].astype(_F32))
        xp_ref[pl.ds(pad + base, step), :] = v
        return carry

    lax.fori_loop(0, s // step, fill, 0)


def _conv_fwd(xs, w, b, k_taps, *, glu=False, act=False, name):
    s, c = xs[0].shape
    kp = w.shape[0]
    pad, rows, cb = _CONV_PAD, _CONV_ROWS, _CONV_CB
    groups = _conv_taps(k_taps)
    n_in = len(xs)

    def body(*refs):
        x_refs = refs[:n_in]
        w_ref, b_ref, o_ref, xp_ref = refs[n_in:]
        _conv_fill(x_refs, xp_ref, s, glu)
        wv = w_ref[...]
        bias = jnp.broadcast_to(b_ref[...], (rows, cb))

        def chunk(r, carry):
            base = pl.multiple_of(r * rows, rows)
            acc = _conv_window(xp_ref[pl.ds(base, rows + pad), :], wv, groups, bias)
            o_ref[pl.ds(base, rows), :] = _silu(acc) if act else acc
            return carry

        lax.fori_loop(0, s // rows, chunk, 0)

    col = pl.BlockSpec((s, cb), lambda i: (0, i))
    return pl.pallas_call(
        body, name=name, grid=(c // cb,),
        in_specs=[col] * n_in + [pl.BlockSpec((kp, cb), lambda i: (0, i)), pl.BlockSpec((1, cb), lambda i: (0, i))],
        out_specs=col, out_shape=jax.ShapeDtypeStruct((s, c), _F32),
        scratch_shapes=[pltpu.VMEM((s + pad, cb), _F32)],
        compiler_params=_params(("parallel",)),
    )(*xs, w, b)


def _conv_bwd(xs, w, b, dy, k_taps, *, glu=False, act=False, name):
    s, c = xs[0].shape
    kp = w.shape[0]
    pad, rows, cb = _CONV_PAD, _CONV_ROWS, _CONV_CB
    groups = _conv_taps(k_taps)
    win_rows = rows + pad
    n_in = len(xs)

    def fold(v):
        acc = v[0:SUBLANES, :]
        for i in range(1, rows // SUBLANES):
            acc = acc + v[i * SUBLANES:(i + 1) * SUBLANES, :]
        return acc

    def body(*refs):
        x_refs = refs[:n_in]
        w_ref, b_ref, dy_ref = refs[n_in:n_in + 3]
        dx_refs = refs[n_in + 3:2 * n_in + 3]
        dw_ref, db_ref, xp_ref, dyp_ref, acc_ref, dbacc_ref = refs[2 * n_in + 3:]
        _conv_fill(x_refs, xp_ref, s, glu)
        dyp_ref[s:s + pad, :] = jnp.zeros((pad, cb), _F32)
        acc_ref[...] = jnp.zeros_like(acc_ref)
        dbacc_ref[...] = jnp.zeros_like(dbacc_ref)
        wv = w_ref[...]
        bias = jnp.broadcast_to(b_ref[...], (rows, cb))

        def through_act(r, carry):
            base = pl.multiple_of(r * rows, rows)
            d = dy_ref[pl.ds(base, rows), :]
            if act:
                pre = _conv_window(xp_ref[pl.ds(base, win_rows), :], wv, groups, bias)
                sg = _sig(pre)
                d = d * (sg * (1.0 + pre * (1.0 - sg)))
            dyp_ref[pl.ds(base, rows), :] = d
            return carry

        lax.fori_loop(0, s // rows, through_act, 0)

        def chunk(r, carry):
            base = pl.multiple_of(r * rows, rows)
            xwin = xp_ref[pl.ds(base, win_rows), :]
            dwin = dyp_ref[pl.ds(base, win_rows), :]
            dyc = dwin[0:rows, :]
            dxacc = jnp.zeros((rows, cb), _F32)
            for rot, taps in groups.items():
                xr = xwin if rot == 0 else pltpu.roll(xwin, rot, 0)
                dr = dwin if rot == 0 else pltpu.roll(dwin, win_rows - rot, 0)
                for k, j in taps:
                    a8 = j - rot
                    dxacc = dxacc + dr[a8:a8 + rows, :] * wv[k:k + 1, :]
                    prod = dyc * xr[pad - a8:pad - a8 + rows, :]
                    acc_ref[k * SUBLANES:(k + 1) * SUBLANES, :] += fold(prod)
            dbacc_ref[...] += fold(dyc)
            if glu:
                av = x_refs[0][pl.ds(base, rows), :].astype(_F32)
                sg = _sig(x_refs[1][pl.ds(base, rows), :].astype(_F32))
                dx_refs[0][pl.ds(base, rows), :] = (dxacc * sg).astype(dx_refs[0].dtype)
                dx_refs[1][pl.ds(base, rows), :] = (dxacc * av * sg * (1.0 - sg)).astype(dx_refs[1].dtype)
            else:
                dx_refs[0][pl.ds(base, rows), :] = dxacc
            return carry

        lax.fori_loop(0, s // rows, chunk, 0)
        dw_ref[...] = jnp.zeros_like(dw_ref)
        for k in range(k_taps):
            dw_ref[k:k + 1, :] = jnp.sum(acc_ref[k * SUBLANES:(k + 1) * SUBLANES, :], axis=0, keepdims=True)
        db_ref[...] = jnp.sum(dbacc_ref[...], axis=0, keepdims=True)

    col = pl.BlockSpec((s, cb), lambda i: (0, i))
    wspec = pl.BlockSpec((kp, cb), lambda i: (0, i))
    bspec = pl.BlockSpec((1, cb), lambda i: (0, i))
    dx_dtype = xs[0].dtype if glu else _F32
    res = pl.pallas_call(
        body, name=name, grid=(c // cb,),
        in_specs=[col] * n_in + [wspec, bspec, col], out_specs=[col] * n_in + [wspec, bspec],
        out_shape=[jax.ShapeDtypeStruct((s, c), dx_dtype)] * n_in
        + [jax.ShapeDtypeStruct((kp, c), _F32), jax.ShapeDtypeStruct((1, c), _F32)],
        scratch_shapes=[pltpu.VMEM((s + pad, cb), _F32), pltpu.VMEM((s + pad, cb), _F32),
                        pltpu.VMEM((kp * SUBLANES, cb), _F32), pltpu.VMEM((SUBLANES, cb), _F32)],
        compiler_params=_params(("parallel",)),
    )(*xs, w, b, dy)
    return list(res[:n_in]), res[n_in], res[n_in + 1]


def _tri_sum(v, lower):
    l = v.shape[0]
    r, c = lax.broadcasted_iota(jnp.int32, (l, l), 0), lax.broadcasted_iota(jnp.int32, (l, l), 1)
    tri = ((r >= c) if lower else (r <= c)).astype(jnp.bfloat16)
    hi = v.astype(jnp.bfloat16)
    r1 = v - hi.astype(_F32)
    mid = r1.astype(jnp.bfloat16)
    lo = (r1 - mid.astype(_F32)).astype(jnp.bfloat16)
    out = jnp.zeros_like(v)
    for part in (hi, mid, lo):
        out = out + lax.dot_general(tri, part, _DN["nn"], preferred_element_type=_F32)
    return out


@jax.custom_vjp
def _cumsum_rows(v):
    return _tri_sum(v, True)


_cumsum_rows.defvjp(lambda v: (_tri_sum(v, True), None), lambda _, ct: (_tri_sum(ct, False),))


def _ssd_chunk(xbc, dtraw, prev, bias, alog):
    l = xbc.shape[0]
    xs = xbc[:, :SSD_WIDTH]
    bm = xbc[:, SSD_WIDTH:SSD_WIDTH + 2 * SSD_STATE]
    cm = xbc[:, SSD_WIDTH + 2 * SSD_STATE:]
    v = dtraw + bias
    dt = jnp.maximum(v, 0.0) + jnp.log1p(jnp.exp(-jnp.abs(v)))
    a_neg = -jnp.exp(alog)
    acs = _cumsum_rows(dt * a_neg)
    acs_t = acs.T
    dt_t = dt.T
    total = acs[l - 1:l, :]
    row = lax.broadcasted_iota(jnp.int32, (l, l), 0)
    colv = lax.broadcasted_iota(jnp.int32, (l, l), 1)
    causal = row >= colv
    lane_lo = lax.broadcasted_iota(jnp.int32, (l, LANES), 1) < HEAD_DIM
    row_lo = lax.broadcasted_iota(jnp.int32, (LANES, SSD_STATE), 0) < HEAD_DIM

    def pair_lanes(m, h0):
        return jnp.where(lane_lo, m[:, h0:h0 + 1], m[:, h0 + 1:h0 + 2])

    ys, news = [], []
    cb = {}
    for j in range(SSD_HEADS // 2):
        h0 = 2 * j
        grp = h0 // (SSD_HEADS // 2)
        bg = bm[:, grp * SSD_STATE:(grp + 1) * SSD_STATE]
        cg = cm[:, grp * SSD_STATE:(grp + 1) * SSD_STATE]
        if grp not in cb:
            cb[grp] = _dot1(cg, bg, "nt")
        xp = xs[:, j * LANES:(j + 1) * LANES]
        y = jnp.zeros((l, LANES), _F32)
        for hh, mask in ((h0, lane_lo), (h0 + 1, jnp.logical_not(lane_lo))):
            seg = acs[:, hh:hh + 1] - acs_t[hh:hh + 1, :]
            dec = jnp.exp(jnp.where(causal, seg, -jnp.inf))
            sc = cb[grp] * dec * dt_t[hh:hh + 1, :]
            y = y + _dot1(sc, jnp.where(mask, xp, 0.0), "nn")
        acs_p = pair_lanes(acs, h0)
        prev_p = prev[j * LANES:(j + 1) * LANES, :]
        y = y + _dot1(cg, prev_p, "nt") * jnp.exp(acs_p)
        wgt = jnp.exp(pair_lanes(jnp.broadcast_to(total, (l, LANES)), h0) - acs_p) * pair_lanes(dt, h0)
        st = _dot1(xp * wgt, bg, "tn")
        cdec = jnp.exp(jnp.where(row_lo, total[:, h0:h0 + 1], total[:, h0 + 1:h0 + 2]))
        news.append(prev_p * cdec + st)
        ys.append(y)
    return jnp.concatenate(ys, axis=1), jnp.concatenate(news, axis=0)


def _ssd_fwd(xbc, dtraw, bias, alog, *, name):
    s = xbc.shape[0]
    nc = s // CHUNK
    nstate = SSD_HEADS * HEAD_DIM

    def body(x_ref, dt_ref, b_ref, a_ref, y_ref, st_ref, state_ref):
        @pl.when(pl.program_id(0) == 0)
        def _():
            state_ref[...] = jnp.zeros_like(state_ref)

        prev = state_ref[...]
        st_ref[...] = prev
        y, new = _ssd_chunk(x_ref[...], dt_ref[...], prev, b_ref[...], a_ref[...])
        y_ref[...] = y
        state_ref[...] = new

    small = pl.BlockSpec((1, LANES), lambda i: (0, 0))
    return pl.pallas_call(
        body, name=name, grid=(nc,),
        in_specs=[pl.BlockSpec((CHUNK, XBC_WIDTH), lambda i: (i, 0)), pl.BlockSpec((CHUNK, LANES), lambda i: (i, 0)),
                  small, small],
        out_specs=[pl.BlockSpec((CHUNK, SSD_WIDTH), lambda i: (i, 0)),
                   pl.BlockSpec((None, nstate, SSD_STATE), lambda i: (i, 0, 0))],
        out_shape=[jax.ShapeDtypeStruct((s, SSD_WIDTH), _F32), jax.ShapeDtypeStruct((nc, nstate, SSD_STATE), _F32)],
        scratch_shapes=[pltpu.VMEM((nstate, SSD_STATE), _F32)],
        compiler_params=_params(("arbitrary",)),
    )(xbc, dtraw, bias, alog)


def _ssd_bwd(xbc, dtraw, states, bias, alog, dy, dxs_extra, *, name):
    s = xbc.shape[0]
    nc = s // CHUNK
    nstate = SSD_HEADS * HEAD_DIM

    def body(x_ref, dt_ref, st_ref, b_ref, a_ref, dy_ref, ex_ref, dx_ref, ddt_ref, db_ref, da_ref, dstate_ref):
        @pl.when(pl.program_id(0) == 0)
        def _():
            dstate_ref[...] = jnp.zeros_like(dstate_ref)
            db_ref[...] = jnp.zeros_like(db_ref)
            da_ref[...] = jnp.zeros_like(da_ref)

        _, vjp = jax.vjp(_ssd_chunk, x_ref[...], dt_ref[...], st_ref[...], b_ref[...], a_ref[...])
        dx, ddt, dprev, db, da = vjp((dy_ref[...], dstate_ref[...]))
        dx_ref[:, :SSD_WIDTH] = dx[:, :SSD_WIDTH] + ex_ref[...]
        dx_ref[:, SSD_WIDTH:] = dx[:, SSD_WIDTH:]
        ddt_ref[...] = ddt
        db_ref[...] += db
        da_ref[...] += da
        dstate_ref[...] = dprev

    rev = lambda i: (nc - 1 - i, 0)
    small = pl.BlockSpec((1, LANES), lambda i: (0, 0))
    return pl.pallas_call(
        body, name=name, grid=(nc,),
        in_specs=[pl.BlockSpec((CHUNK, XBC_WIDTH), rev), pl.BlockSpec((CHUNK, LANES), rev),
                  pl.BlockSpec((None, nstate, SSD_STATE), lambda i: (nc - 1 - i, 0, 0)), small, small,
                  pl.BlockSpec((CHUNK, SSD_WIDTH), rev), pl.BlockSpec((CHUNK, SSD_WIDTH), rev)],
        out_specs=[pl.BlockSpec((CHUNK, XBC_WIDTH), rev), pl.BlockSpec((CHUNK, LANES), rev), small, small],
        out_shape=[jax.ShapeDtypeStruct((s, XBC_WIDTH), _F32), jax.ShapeDtypeStruct((s, LANES), _F32),
                   jax.ShapeDtypeStruct((1, LANES), _F32), jax.ShapeDtypeStruct((1, LANES), _F32)],
        scratch_shapes=[pltpu.VMEM((nstate, SSD_STATE), _F32)],
        compiler_params=_params(("arbitrary",)),
    )(xbc, dtraw, states, bias, alog, dy, dxs_extra)


def _pad_cols(a, width):
    return jnp.pad(a, ((0, 0), (0, width - a.shape[1])))


def _pad_rows(a, rows):
    return jnp.pad(a, ((0, rows - a.shape[0]), (0, 0)))


def _tie(a, token):
    return a + token[0:1, 0:1].astype(a.dtype)


def _local_step(x, mem, target, w, fetch, emit):
    bf = _MXU
    d = D_MODEL
    h = _row_fwd(_f_rms, [x], [w['norm_mix_g']], [(d, bf)], name="f_norm_mix")
    w_in = fetch('in', h)['w_in']
    z_end, xbc_end, dt_end = SSD_WIDTH, SSD_WIDTH + XBC_WIDTH, SSD_WIDTH + XBC_WIDTH + SSD_HEADS
    w_z, w_xbc = w_in[:z_end], w_in[z_end:xbc_end]
    w_dt = _pad_rows(w_in[xbc_end:dt_end], LANES)
    w_a, w_g = w_in[dt_end:dt_end + CF_WIDTH], w_in[dt_end + CF_WIDTH:]
    dt_bias = _pad_cols(w['ssd_dt_bias'], LANES)
    a_log = _pad_cols(w['ssd_A_log'], LANES)
    d_exp = jnp.repeat(w['ssd_D'], HEAD_DIM, axis=1)
    g_final = w['norm_final_g'].reshape(1, D_MODEL)

    z = _mm(h, w_z, tb=True, out_dtype=bf, name="f_in_z")
    xbc = _mm(h, w_xbc, tb=True, name="f_in_xbc")
    dtr = _mm(h, w_dt, tb=True, name="f_in_dt")
    ga = _mm(h, w_a, tb=True, out_dtype=bf, name="f_in_a")
    gg = _mm(h, w_g, tb=True, out_dtype=bf, name="f_in_g")
    wc = fetch('conv', xbc)
    ssd_w = _pad_rows(wc['ssd_conv_w'], SUBLANES)
    cf_w = _pad_rows(wc['cf_conv_w'], 32)
    xbc_a = _conv_fwd([xbc], ssd_w, w['ssd_conv_b'], SSD_CONV, act=True, name="f_ssd_conv")
    y_ssd, states = _ssd_fwd(xbc_a, dtr, dt_bias, a_log, name="f_ssd")
    xs_win = (xbc_a, SSD_WIDTH, 0)
    y_n = _row_fwd(_f_gate, [y_ssd, xs_win, z], [d_exp, w['ssd_norm_g']], [(d, bf)], name="f_ssd_gate")
    u_c = _conv_fwd([ga, gg], cf_w, w['cf_conv_b'], CF_CONV, glu=True, name="f_cf_conv")
    u = _row_fwd(_f_ln, [u_c], [w['cf_ln_g'], w['cf_ln_b']], [(d, bf)], name="f_cf_ln")
    wm = fetch('mid', y_n)
    w_out_y, w_out_u = wm['w_out'][:SSD_WIDTH], wm['w_out'][SSD_WIDTH:]
    x1 = _mm(y_n, w_out_y, add=x, name="f_out_y")
    x1 = _mm(u, w_out_u, add=x1, name="f_out_u")
    hq = _row_fwd(_f_rms, [x1], [w['norm_xattn_g']], [(d, bf)], name="f_norm_xattn")
    q = _mm(hq, wm['w_q'], out_dtype=bf, name="f_q")
    memn = _row_fwd(_f_rms, [mem], [w['norm_mem_g']], [(d, bf)], name="f_norm_mem")
    kv = _mm(memn, wm['w_kv'], name="f_kv")
    k_mat, v_mat = kv[:, :d], kv[:, d:]
    o = _row_fwd(_f_att, [q], [k_mat, v_mat], [(d, bf)], name="f_att")
    x2 = _mm(o, wm['w_o'], add=x1, name="f_o")
    hf = _row_fwd(_f_rms, [x2], [w['norm_ffn_g']], [(d, bf)], name="f_norm_ffn")
    wf = fetch('ffn', hf)
    gate = _mm(hf, wf['w_gate'], tb=True, out_dtype=bf, name="f_gate")
    up = _mm(hf, wf['w_up'], tb=True, out_dtype=bf, name="f_up")
    act = _row_fwd(_f_swiglu, [gate, up], [], [(D_FF, bf)], name="f_swiglu")
    x3 = _mm(act, wf['w_down'], add=x2, name="f_down")

    dx3, dg_final, loss = _loss_bwd(x3, target, g_final, name="b_loss")
    g = {'norm_final_g': dg_final.reshape(d)}

    dact = _mm(dx3, wf['w_down'], tb=True, out_dtype=bf, name="b_down_x")
    dw_down = _mm(act, dx3, ta=True, name="b_down_w")
    (dgate, dup), _ = _row_bwd(_f_swiglu, [gate, up], [], [dact], row_dtypes=[bf, bf], name="b_swiglu")
    sent = emit({'w_down': dw_down, 'w_gate': _mm(dgate, hf, ta=True, name="b_gate_w"),
                 'w_up': _mm(dup, hf, ta=True, name="b_up_w")})
    dhf = _mm(dgate, wf['w_gate'], name="b_gate_x")
    dhf = _mm(dup, wf['w_up'], add=dhf, out_dtype=bf, name="b_up_x")
    (dx2,), (g['norm_ffn_g'],) = _row_bwd(_f_rms, [x2], [_tie(w['norm_ffn_g'], sent)], [dhf], adds={0: dx3}, name="b_norm_ffn")

    do = _mm(dx2, wm['w_o'], tb=True, out_dtype=bf, name="b_o_x")
    dw_o = _mm(o, dx2, ta=True, name="b_o_w")
    (dq,), (dk, dv) = _row_bwd(_f_att, [q], [k_mat, v_mat], [do], row_dtypes=[bf], name="b_att")
    dw_q = _mm(hq, dq, ta=True, name="b_q_w")
    dhq = _mm(dq, wm['w_q'], tb=True, out_dtype=bf, name="b_q_x")
    (dx1,), (g['norm_xattn_g'],) = _row_bwd(_f_rms, [x1], [w['norm_xattn_g']], [dhq], adds={0: dx2}, name="b_norm_xattn")
    dkv = jnp.concatenate([dk, dv], axis=1)
    sent = emit({'w_o': dw_o, 'w_q': dw_q, 'w_kv': _mm(memn, dkv, ta=True, name="b_kv_w")})
    dmemn = _mm(dkv, wm['w_kv'], tb=True, name="b_kv_x")
    _, (g['norm_mem_g'],) = _row_bwd(_f_rms, [mem], [w['norm_mem_g']], [dmemn], need=[False], name="b_norm_mem")

    dyn = _mm(dx1, w_out_y, tb=True, out_dtype=bf, name="b_out_y_x")
    du = _mm(dx1, w_out_u, tb=True, out_dtype=bf, name="b_out_u_x")
    (du_c,), (g['cf_ln_g'], g['cf_ln_b']) = _row_bwd(_f_ln, [u_c], [_tie(w['cf_ln_g'], sent), w['cf_ln_b']], [du], name="b_cf_ln")
    sent = emit({'w_out': jnp.concatenate([_mm(y_n, dx1, ta=True, name="b_out_y_w"), _mm(u, dx1, ta=True, name="b_out_u_w")], axis=0)})
    (dga, dgg), dcf_w, g['cf_conv_b'] = _conv_bwd([ga, gg], cf_w, w['cf_conv_b'], du_c, CF_CONV, glu=True, name="b_cf_conv")
    g['cf_conv_w'] = dcf_w[:CF_CONV]
    (dy_ssd, dxs, dz), (dd_exp, g['ssd_norm_g']) = _row_bwd(
        _f_gate, [y_ssd, xs_win, z], [d_exp, _tie(w['ssd_norm_g'], sent)], [dyn], row_dtypes=[_F32, _F32, bf], name="b_ssd_gate")
    g['ssd_D'] = jnp.sum(dd_exp.reshape(SSD_HEADS, HEAD_DIM), axis=1).reshape(1, SSD_HEADS)
    dxbc_a, ddtr, ddt_bias, da_log = _ssd_bwd(xbc_a, dtr, states, dt_bias, a_log, dy_ssd, dxs, name="b_ssd")
    g['ssd_dt_bias'] = ddt_bias[:, :SSD_HEADS]
    g['ssd_A_log'] = da_log[:, :SSD_HEADS]
    (dxbc,), dssd_w, g['ssd_conv_b'] = _conv_bwd([xbc], ssd_w, w['ssd_conv_b'], dxbc_a, SSD_CONV, act=True, name="b_ssd_conv")
    g['ssd_conv_w'] = dssd_w[:SSD_CONV]

    sent = emit({'w_in': jnp.concatenate([
        _mm(dz, h, ta=True, name="b_in_z_w"), _mm(dxbc, h, ta=True, name="b_in_xbc_w"),
        _mm(ddtr, h, ta=True, name="b_in_dt_w")[:SSD_HEADS],
        _mm(dga, h, ta=True, name="b_in_a_w"), _mm(dgg, h, ta=True, name="b_in_g_w")], axis=0)})
    dh = _mm(dz, _tie(w_z, sent), name="b_in_z_x")
    dh = _mm(dxbc, w_xbc, add=dh, name="b_in_xbc_x")
    dh = _mm(ddtr, w_dt, add=dh, name="b_in_dt_x")
    dh = _mm(dga, w_a, add=dh, name="b_in_a_x")
    dh = _mm(dgg, w_g, add=dh, out_dtype=bf, name="b_in_g_x")
    (dx,), (g['norm_mix_g'],) = _row_bwd(_f_rms, [x], [w['norm_mix_g']], [dh], adds={0: dx1}, name="b_norm_mix")
    return loss, dx, g


_ANY = pl.BlockSpec(memory_space=pl.ANY)


def _place():
    x, y, c = lax.axis_index("x"), lax.axis_index("y"), lax.axis_index("c")
    return x, y, c


def _all_gather(arrs, *, name):
    n = len(arrs)

    def body(*refs):
        ins, outs = refs[:n], refs[n:2 * n]
        send_sems, recv_sems, local_sems = refs[2 * n:]
        x, y, c = _place()
        me, sibling = (x, y, c), (x, y, 1 - c)
        chips = [(1 - x, y), (x, 1 - y), (1 - x, 1 - y)]

        def slot(a, dev):
            return outs[a].at[4 * dev[0] + 2 * dev[1] + dev[2]]

        def copy(a, k, block, to, src=None):
            return pltpu.make_async_remote_copy(
                src_ref=slot(a, block) if src is None else src, dst_ref=slot(a, block),
                send_sem=send_sems.at[a, k], recv_sem=recv_sems.at[a, k], device_id=to, device_id_type=MESH)

        mine = [pltpu.make_async_copy(ins[a], slot(a, me), local_sems.at[a]) for a in range(n)]
        for cp in mine:
            cp.start()
        first = []
        for a in range(n):
            first.append(copy(a, 0, me, sibling, src=ins[a]))
            first += [copy(a, 1 + j, me, (*chip, c), src=ins[a]) for j, chip in enumerate(chips)]
        for cp in first:
            cp.start()
        passed = []
        for a in range(n):
            for j, chip in enumerate(chips):
                copy(a, 1 + j, (*chip, c), me).wait_recv()
                fwd = copy(a, 4 + j, (*chip, c), sibling)
                fwd.start()
                passed.append(fwd)
        for a in range(n):
            copy(a, 0, sibling, me).wait_recv()
            for j, chip in enumerate(chips):
                copy(a, 4 + j, (*chip, 1 - c), me).wait_recv()
        for cp in first + passed:
            cp.wait_send()
        for cp in mine:
            cp.wait()

    return pl.pallas_call(
        body, name=name, in_specs=[_ANY] * n, out_specs=[_ANY] * n,
        out_shape=[jax.ShapeDtypeStruct((N_DEV,) + a.shape, a.dtype) for a in arrs],
        scratch_shapes=[pltpu.SemaphoreType.DMA((n, 7)), pltpu.SemaphoreType.DMA((n, 7)), pltpu.SemaphoreType.DMA((n,))],
    )(*arrs)


_HBM = pl.BlockSpec(memory_space=pltpu.HBM)
_SEM = pl.BlockSpec(memory_space=pltpu.SEMAPHORE)
_EFFECT = pltpu.SideEffectType.DATAFLOW_SIDE_EFFECTING
_FLIPS = [(dx, dy, dc) for dx in (0, 1) for dy in (0, 1) for dc in (0, 1)][1:]


def _peer(flip, x, y, c):
    return (1 - x if flip[0] else x, 1 - y if flip[1] else y, 1 - c if flip[2] else c)


def _send_start(srcs, blocked, *, after=None, name):
    n = len(srcs)
    lands = [jax.ShapeDtypeStruct(s.shape if blocked else (N_DEV,) + s.shape, s.dtype) for s in srcs]
    n_in = 2 * n + (after is not None)

    def body(*refs):
        src_refs, land_refs = refs[:n], refs[n:2 * n]
        send_sems, recv_sems = refs[n_in], refs[n_in + 1]
        token = refs[-1]
        x, y, c = _place()
        me = 4 * x + 2 * y + c
        for a in range(n):
            for k, flip in enumerate(_FLIPS):
                p = _peer(flip, x, y, c)
                src = src_refs[a].at[4 * p[0] + 2 * p[1] + p[2]] if blocked else src_refs[a]
                pltpu.make_async_remote_copy(
                    src_ref=src, dst_ref=land_refs[a].at[me], send_sem=send_sems.at[7 * a + k], recv_sem=recv_sems.at[7 * a + k],
                    device_id=p, device_id_type=MESH).start()
        token[...] = jnp.zeros_like(token)

    res = pl.pallas_call(
        body, name=name,
        out_shape=(pltpu.SemaphoreType.DMA((7 * n,)), pltpu.SemaphoreType.DMA((7 * n,)),
                   *[pltpu.HBM(s.shape, s.dtype) for s in srcs], *[pltpu.HBM(l.shape, l.dtype) for l in lands],
                   jax.ShapeDtypeStruct((SUBLANES, LANES), _F32)),
        in_specs=[_HBM] * (2 * n) + [_ANY] * (after is not None),
        out_specs=(_SEM, _SEM, *[_HBM] * (2 * n), pl.BlockSpec(memory_space=pltpu.VMEM)),
        input_output_aliases={i: 2 + i for i in range(2 * n)},
        compiler_params=pltpu.CompilerParams(has_side_effects=_EFFECT),
    )(*[pltpu.with_memory_space_constraint(s, pltpu.HBM) for s in srcs],
      *[pltpu.with_memory_space_constraint(lax.empty(l.shape, l.dtype), pltpu.HBM) for l in lands],
      *([after] if after is not None else []))
    return res[0], res[1], list(res[2:2 + n]), list(res[2 + n:2 + 2 * n]), res[-1]


def _send_wait(handles, after, blocked, *, name):
    send_sems, recv_sems, srcs, lands, _ = handles
    n = len(srcs)

    def body(*refs):
        src_refs, land_refs = refs[:n], refs[n:2 * n]
        send_sems, recv_sems = refs[2 * n], refs[2 * n + 1]
        x, y, c = _place()
        for a in range(n):
            for k, flip in enumerate(_FLIPS):
                p = _peer(flip, x, y, c)
                pid = 4 * p[0] + 2 * p[1] + p[2]
                cp = pltpu.make_async_remote_copy(
                    src_ref=src_refs[a].at[pid] if blocked else src_refs[a], dst_ref=land_refs[a].at[pid],
                    send_sem=send_sems.at[7 * a + k], recv_sem=recv_sems.at[7 * a + k], device_id=p, device_id_type=MESH)
                cp.wait_send()
                cp.wait_recv()

    res = pl.pallas_call(
        body, name=name,
        out_shape=tuple(pltpu.HBM(s.shape, s.dtype) for s in srcs + lands),
        in_specs=[_HBM] * (2 * n) + [_SEM, _SEM, _ANY], out_specs=tuple([_HBM] * (2 * n)),
        input_output_aliases={i: i for i in range(2 * n)},
        compiler_params=pltpu.CompilerParams(has_side_effects=_EFFECT),
    )(*srcs, *lands, send_sems, recv_sems, after)
    return list(res[:n]), list(res[n:])


def _adamw(parts, w, m, v, *, own=None, me=None, name):
    p, r, c = parts.shape
    tr = _pick(r, (256, 176, 128, 64, 32, 16, 8))
    if own is not None:
        tc = c if tr < r else _pick(c, (256, 128))
        return _adamw_own(parts, own, me, w, m, v, tr, tc, name=name)

    def body(p_ref, w_ref, m_ref, v_ref, g_ref, d_ref, nm_ref, nv_ref):
        g = p_ref[0].astype(_F32)
        for i in range(1, p):
            g = g + p_ref[i].astype(_F32)
        _adamw_math(g, w_ref, m_ref, v_ref, g_ref, d_ref, nm_ref, nv_ref)

    blk = pl.BlockSpec((tr, c), lambda i: (i, 0))
    return pl.pallas_call(
        body, name=name, grid=(r // tr,),
        in_specs=[pl.BlockSpec((p, tr, c), lambda i: (0, i, 0)), blk, blk, blk], out_specs=[blk] * 4,
        out_shape=[jax.ShapeDtypeStruct((r, c), _F32)] * 4,
        compiler_params=_params(("parallel",)),
    )(parts, w, m, v)


def _adamw_math(g, w_ref, m_ref, v_ref, g_ref, d_ref, nm_ref, nv_ref):
    wv = w_ref[...]
    mn = ADAM_B1 * m_ref[...] + (1.0 - ADAM_B1) * g
    vn = ADAM_B2 * v_ref[...] + (1.0 - ADAM_B2) * jnp.square(g)
    m_hat = mn / (1.0 - ADAM_B1 ** ADAM_STEP)
    v_hat = vn / (1.0 - ADAM_B2 ** ADAM_STEP)
    g_ref[...] = g
    d_ref[...] = -ADAM_LR * (m_hat / (jnp.sqrt(v_hat) + ADAM_EPS) + ADAM_WD * wv)
    nm_ref[...] = mn
    nv_ref[...] = vn


def _adamw_own(parts, own, me, w, m, v, tr, tc, *, name):
    p, r, c = parts.shape

    def body(me_ref, p_ref, own_ref, w_ref, m_ref, v_ref, g_ref, d_ref, nm_ref, nv_ref):
        mine = own_ref[...].astype(_F32)
        g = jnp.where(me_ref[0] == 0, mine, p_ref[0].astype(_F32))
        for i in range(1, p):
            g = g + jnp.where(me_ref[0] == i, mine, p_ref[i].astype(_F32))
        _adamw_math(g, w_ref, m_ref, v_ref, g_ref, d_ref, nm_ref, nv_ref)

    blk = pl.BlockSpec((tr, tc), lambda i, j, me_ref: (i, j))
    grid_spec = pltpu.PrefetchScalarGridSpec(
        num_scalar_prefetch=1, grid=(r // tr, c // tc),
        in_specs=[pl.BlockSpec((p, tr, tc), lambda i, j, me_ref: (0, i, j)),
                  pl.BlockSpec((None, tr, tc), lambda i, j, me_ref: (me_ref[0], i, j)), blk, blk, blk],
        out_specs=[blk] * 4)
    return pl.pallas_call(
        body, name=name, grid_spec=grid_spec, out_shape=[jax.ShapeDtypeStruct((r, c), _F32)] * 4,
        compiler_params=_params(("parallel", "parallel")),
    )(me.reshape(1).astype(jnp.int32), parts, own, w, m, v)


def _sum_parts(parts, *, name):
    p, r, c = parts.shape

    def body(p_ref, o_ref):
        g = p_ref[0].astype(_F32)
        for i in range(1, p):
            g = g + p_ref[i].astype(_F32)
        o_ref[...] = g

    return pl.pallas_call(body, name=name, out_shape=jax.ShapeDtypeStruct((r, c), _F32))(parts)


def _pack(vals, rows):
    flat = jnp.concatenate([v.reshape(-1) for v in vals])
    return jnp.pad(flat, (0, rows * LANES - flat.shape[0])).reshape(rows, LANES)


def _unpack(packed, shapes):
    flat = packed.reshape(-1)
    out, pos = [], 0
    for shp in shapes:
        size = math.prod(shp)
        out.append(flat[pos:pos + size].reshape(shp))
        pos += size
    return out


def _pack_rows(shapes):
    total = sum(math.prod(s) for s in shapes)
    return -(-total // (LANES * SUBLANES)) * SUBLANES


def kernel(x, mem, norm_mix_g, w_in, ssd_conv_w, ssd_conv_b, ssd_dt_bias, ssd_A_log, ssd_D, ssd_norm_g, cf_conv_w, cf_conv_b, cf_ln_g, cf_ln_b, w_out, norm_xattn_g, norm_mem_g, w_q, w_kv, w_o, norm_ffn_g, w_gate, w_up, w_down, norm_final_g, loss_target, m_norm_mix_g, m_w_in, m_ssd_conv_w, m_ssd_conv_b, m_ssd_dt_bias, m_ssd_A_log, m_ssd_D, m_ssd_norm_g, m_cf_conv_w, m_cf_conv_b, m_cf_ln_g, m_cf_ln_b, m_w_out, m_norm_xattn_g, m_norm_mem_g, m_w_q, m_w_kv, m_w_o, m_norm_ffn_g, m_w_gate, m_w_up, m_w_down, m_norm_final_g, v_norm_mix_g, v_w_in, v_ssd_conv_w, v_ssd_conv_b, v_ssd_dt_bias, v_ssd_A_log, v_ssd_D, v_ssd_norm_g, v_cf_conv_w, v_cf_conv_b, v_cf_ln_g, v_cf_ln_b, v_w_out, v_norm_xattn_g, v_norm_mem_g, v_w_q, v_w_kv, v_w_o, v_norm_ffn_g, v_w_gate, v_w_up, v_w_down, v_norm_final_g):
    args = dict(locals())
    wts = {n: args[n] for n in WEIGHT_NAMES}
    mom = {n: args["m_" + n] for n in WEIGHT_NAMES}
    var = {n: args["v_" + n] for n in WEIGHT_NAMES}
    me = 4 * lax.axis_index("x") + 2 * lax.axis_index("y") + lax.axis_index("c")

    groups = {'in': ['w_in'], 'conv': ['ssd_conv_w', 'cf_conv_w'], 'mid': ['w_out', 'w_q', 'w_kv', 'w_o'],
              'ffn': ['w_gate', 'w_up', 'w_down']}
    def shard(n, a):
        return jnp.transpose(a[0], (1, 0)) if n in TRANSPOSED else a[0]

    gathers, started = {}, None
    for grp, names in groups.items():
        shards = [wts[n][0] if grp == 'conv' else shard(n, wts[n]).astype(_MXU) for n in names]
        gathers[grp] = _send_start(shards, False, after=started, name="gather_%s_start" % grp)
        started = gathers[grp][4]

    def fetch(grp, after):
        srcs, lands = _send_wait(gathers[grp], started if after is None else after, False, name="gather_%s_wait" % grp)
        out = {}
        for n, own, gth in zip(groups[grp], srcs, lands):
            gth = lax.dynamic_update_slice_in_dim(gth, own[None], me, axis=0)
            if n == 'w_kv' or grp == 'conv':
                out[n] = jnp.transpose(gth, (1, 0, 2)).reshape(gth.shape[1], N_DEV * gth.shape[2])
            else:
                out[n] = gth.reshape(N_DEV * gth.shape[1], gth.shape[2])
        return out

    exchanges = []

    def emit(grads):
        blocks = []
        for n, gw in grads.items():
            if n == 'w_kv':
                gw = jnp.transpose(gw.reshape(gw.shape[0], N_DEV, gw.shape[1] // N_DEV), (1, 0, 2))
            else:
                gw = gw.reshape(N_DEV, gw.shape[0] // N_DEV, gw.shape[1])
            blocks.append(gw.astype(jnp.bfloat16))
        first = next(iter(grads))
        exchanges.append((list(grads), _send_start(blocks, True, name="exchange_%s_start" % first), first))
        return exchanges[-1][1][4]

    full = {n: wts[n] for n in WEIGHT_NAMES if n not in BIG and n not in groups['conv']}
    full['norm_mix_g'] = _tie(norm_mix_g, started)

    loss_blk, grad_x, g = _local_step(x[0], mem[0], loss_target[0], full, fetch, emit)
    loss = lax.psum(loss_blk[0, 0], ("x", "y", "c"))

    out_g, out_d, out_m, out_v = {}, {}, {}, {}
    for names, handles, first in exchanges:
        srcs, lands = _send_wait(handles, grad_x, True, name="exchange_%s_wait" % first)
        for n, own, parts in zip(names, srcs, lands):
            res = _adamw(parts, shard(n, wts[n]), shard(n, mom[n]), shard(n, var[n]), own=own, me=me, name="adamw_" + n)
            out_g[n], out_d[n], out_m[n], out_v[n] = [(jnp.transpose(r, (1, 0)) if n in TRANSPOSED else r)[None] for r in res]

    small = [n for n in WEIGHT_NAMES if n not in BIG]
    small_shapes = [g[n].shape for n in small]
    rows = _pack_rows(small_shapes)
    (small_parts,) = _all_gather([_pack([g[n] for n in small], rows)], name="gather_small_grads")
    small_sum = dict(zip(small, _unpack(_sum_parts(small_parts, name="sum_small_grads"), small_shapes)))
    small_sum['ssd_conv_w'] = lax.dynamic_slice_in_dim(small_sum['ssd_conv_w'], me * (XBC_WIDTH // N_DEV), XBC_WIDTH // N_DEV, axis=1)[None]
    small_sum['cf_conv_w'] = lax.dynamic_slice_in_dim(small_sum['cf_conv_w'], me * (CF_WIDTH // N_DEV), CF_WIDTH // N_DEV, axis=1)[None]
    shard_shapes = [wts[n].shape for n in small]
    rows2 = _pack_rows(shard_shapes)
    res = _adamw(_pack([small_sum[n] for n in small], rows2)[None], _pack([wts[n] for n in small], rows2),
                 _pack([mom[n] for n in small], rows2), _pack([var[n] for n in small], rows2), name="adamw_small")
    for dst, packed in zip((out_g, out_d, out_m, out_v), res):
        dst.update(zip(small, _unpack(packed, shard_shapes)))

    return (loss, grad_x[None], *[out_g[n] for n in WEIGHT_NAMES], *[out_d[n] for n in WEIGHT_NAMES],
            *[out_m[n] for n in WEIGHT_NAMES], *[out_v[n] for n in WEIGHT_NAMES])
```

```python
import functools
import math

import jax
import jax.numpy as jnp
from jax import lax
from jax.experimental import pallas as pl
from jax.experimental.pallas import tpu as pltpu

_F32 = jnp.float32
_MXU = jnp.bfloat16
_PREC = None
_VMEM_LIMIT = 56 * 1024 * 1024

D_MODEL = 1024
HEAD_DIM = 64
SSD_HEADS = 16
SSD_WIDTH = 1024
SSD_STATE = 128
SSD_CONV = 4
CHUNK = 128
XBC_WIDTH = 1536
CF_WIDTH = 1024
CF_CONV = 31
X_HEADS = 4
X_HEAD_DIM = 256
D_FF = 2816
EPS = 1e-6
N_DEV = 8
LANES = 128
SUBLANES = 8

ADAM_LR = 0.001
ADAM_B1 = 0.9
ADAM_B2 = 0.999
ADAM_EPS = 1e-08
ADAM_WD = 0.01
ADAM_STEP = 10

MESH = pl.DeviceIdType.MESH
WEIGHT_NAMES = ['norm_mix_g', 'w_in', 'ssd_conv_w', 'ssd_conv_b', 'ssd_dt_bias', 'ssd_A_log', 'ssd_D', 'ssd_norm_g',
                'cf_conv_w', 'cf_conv_b', 'cf_ln_g', 'cf_ln_b', 'w_out', 'norm_xattn_g', 'norm_mem_g', 'w_q', 'w_kv',
                'w_o', 'norm_ffn_g', 'w_gate', 'w_up', 'w_down', 'norm_final_g']
BIG = ['w_in', 'w_out', 'w_q', 'w_kv', 'w_o', 'w_gate', 'w_up', 'w_down']
TRANSPOSED = ('w_in', 'w_gate', 'w_up')


def _params(sem=None):
    return pltpu.CompilerParams(dimension_semantics=sem, vmem_limit_bytes=_VMEM_LIMIT)


def _pick(n, cands):
    for c in cands:
        if n % c == 0:
            return c
    return n


def _mm(a, b, *, ta=False, tb=False, add=None, out_dtype=_F32, name):
    (kdim, m) = a.shape if ta else a.shape[::-1]
    (n, k2) = b.shape if tb else b.shape[::-1]
    assert kdim == k2, (a.shape, b.shape, ta, tb)
    if ta:
        tm = m if m <= 1024 else _pick(m, (1408, 1024, 512, 256, 128))
        tn = n if n <= 1536 else _pick(n, (1408, 1024, 512, 256, 128))
        tk = _pick(kdim, (1024, 512, 256, 128))
    else:
        tm = _pick(m, (512, 256, 128))
        tn = n if n <= 2816 else _pick(n, (1408, 1024, 512, 256, 128))
        tk = kdim if kdim <= 2816 else _pick(kdim, (1408, 1024, 512, 256, 128))
    nk = kdim // tk
    dn = (((0 if ta else 1,), (1 if tb else 0,)), ((), ()))

    def body(*refs):
        a_ref, b_ref = refs[0], refs[1]
        add_ref = refs[2] if add is not None else None
        o_ref = refs[3 if add is not None else 2]
        acc_ref = refs[-1]
        k = pl.program_id(2)
        prod = lax.dot_general(a_ref[...].astype(_MXU), b_ref[...].astype(_MXU), dn,
                               preferred_element_type=_F32, precision=_PREC)

        def finish(r):
            if add_ref is not None:
                r = r + add_ref[...].astype(_F32)
            o_ref[...] = r.astype(o_ref.dtype)

        if nk == 1:
            finish(prod)
            return

        @pl.when(k == 0)
        def _():
            acc_ref[...] = prod

        @pl.when(jnp.logical_and(k > 0, k < nk - 1))
        def _():
            acc_ref[...] += prod

        @pl.when(k == nk - 1)
        def _():
            finish(acc_ref[...] + prod)

    a_spec = pl.BlockSpec((tk, tm), lambda i, j, k: (k, i)) if ta else pl.BlockSpec((tm, tk), lambda i, j, k: (i, k))
    b_spec = pl.BlockSpec((tn, tk), lambda i, j, k: (j, k)) if tb else pl.BlockSpec((tk, tn), lambda i, j, k: (k, j))
    o_spec = pl.BlockSpec((tm, tn), lambda i, j, k: (i, j))
    ins, specs = [a, b], [a_spec, b_spec]
    if add is not None:
        ins.append(add)
        specs.append(o_spec)
    return pl.pallas_call(
        body, name=name, grid=(m // tm, n // tn, nk), in_specs=specs, out_specs=o_spec,
        out_shape=jax.ShapeDtypeStruct((m, n), out_dtype),
        scratch_shapes=[pltpu.VMEM((tm, tn), _F32)] if nk > 1 else [],
        compiler_params=_params(("parallel", "parallel", "arbitrary")),
    )(*ins)


def _resident(shape):
    return pl.BlockSpec(shape, lambda i: (0,) * len(shape), pipeline_mode=pl.Buffered(1))


def _mm_fan_out(a, bs, *, tb, out_dtypes, epilogue=None, extra_outs=(), tm=512, name):
    m, kdim = a.shape
    tm = min(tm, m)
    ns = [b.shape[0] if tb else b.shape[1] for b in bs]
    nb = len(bs)
    kind = "nt" if tb else "nn"

    def body(*refs):
        a_ref, b_refs, o_refs = refs[0], refs[1:1 + nb], refs[1 + nb:]
        av = a_ref[...].astype(_MXU)
        prods = [lax.dot_general(av, b[...].astype(_MXU), _DN[kind], preferred_element_type=_F32, precision=_PREC)
                 for b in b_refs]
        for o_ref, p in zip(o_refs[:nb], prods):
            o_ref[...] = p.astype(o_ref.dtype)
        if epilogue is not None:
            for o_ref, v in zip(o_refs[nb:], _tup(epilogue(*prods))):
                o_ref[...] = v.astype(o_ref.dtype)

    widths = ns + [w for w, _ in extra_outs]
    dtypes = list(out_dtypes) + [dt for _, dt in extra_outs]
    return pl.pallas_call(
        body, name=name, grid=(m // tm,),
        in_specs=[pl.BlockSpec((tm, kdim), lambda i: (i, 0))] + [_resident(b.shape) for b in bs],
        out_specs=[pl.BlockSpec((tm, w), lambda i: (i, 0)) for w in widths],
        out_shape=[jax.ShapeDtypeStruct((m, w), dt) for w, dt in zip(widths, dtypes)],
        compiler_params=_params(("parallel",)),
    )(a, *bs)


def _mm_fan_in(pairs, *, add=None, out_dtype=_F32, prologue=None, pro_ins=(), pro_out_dtypes=(), tm=512, name):
    bs = [b for _, b in pairs]
    nb = len(bs)
    n = bs[0].shape[1]
    rows_in = list(pro_ins) if prologue is not None else [a for a, _ in pairs]
    m = rows_in[0].shape[0]
    tm = min(tm, m)
    n_r = len(rows_in)

    def body(*refs):
        r_refs, b_refs = refs[:n_r], refs[n_r:n_r + nb]
        pos = n_r + nb
        add_ref = refs[pos] if add is not None else None
        pos += add is not None
        o_ref, po_refs = refs[pos], refs[pos + 1:]
        if prologue is not None:
            a_vals = _tup(prologue(*[r[...].astype(_F32) for r in r_refs]))
            for po, v in zip(po_refs, a_vals):
                po[...] = v.astype(po.dtype)
        else:
            a_vals = [r[...] for r in r_refs]
        acc = None
        for av, b in zip(a_vals, b_refs):
            p = lax.dot_general(av.astype(_MXU), b[...].astype(_MXU), _DN["nn"], preferred_element_type=_F32,
                                precision=_PREC)
            acc = p if acc is None else acc + p
        if add_ref is not None:
            acc = acc + add_ref[...].astype(_F32)
        o_ref[...] = acc.astype(o_ref.dtype)

    row = lambda w: pl.BlockSpec((tm, w), lambda i: (i, 0))
    ins = rows_in + bs + ([add] if add is not None else [])
    in_specs = [row(r.shape[1]) for r in rows_in] + [_resident(b.shape) for b in bs] + ([row(n)] if add is not None else [])
    po_widths = [b.shape[0] for b in bs] if prologue is not None else []
    res = pl.pallas_call(
        body, name=name, grid=(m // tm,), in_specs=in_specs,
        out_specs=[row(n)] + [row(w) for w in po_widths],
        out_shape=[jax.ShapeDtypeStruct((m, n), out_dtype)]
        + [jax.ShapeDtypeStruct((m, w), dt) for w, dt in zip(po_widths, pro_out_dtypes)],
        compiler_params=_params(("parallel",)),
    )(*ins)
    return res[0] if prologue is None else res


def _row_spec(r, ts):
    if isinstance(r, tuple):
        arr, width, cblk = r
        return arr, pl.BlockSpec((ts, width), lambda i, cblk=cblk: (i, cblk))
    return r, pl.BlockSpec((ts, r.shape[1]), lambda i: (i, 0))


def _tup(v):
    return tuple(v) if isinstance(v, (tuple, list)) else (v,)


def _row_fwd(f, rows, params, outs, *, name, ts=256):
    arrs, specs = zip(*[_row_spec(r, ts) for r in rows])
    s = arrs[0].shape[0]
    ts = min(ts, s)
    n_r, n_p = len(rows), len(params)

    def body(*refs):
        rv = [r[...].astype(_F32) for r in refs[:n_r]]
        pv = [p[...] for p in refs[n_r:n_r + n_p]]
        res = _tup(f(*rv, *pv))
        for o_ref, v in zip(refs[n_r + n_p:], res):
            o_ref[...] = v.astype(o_ref.dtype)

    res = pl.pallas_call(
        body, name=name, grid=(s // ts,),
        in_specs=list(specs) + [pl.BlockSpec(p.shape, lambda i: (0, 0)) for p in params],
        out_specs=[pl.BlockSpec((ts, w), lambda i: (i, 0)) for w, _ in outs],
        out_shape=[jax.ShapeDtypeStruct((s, w), dt) for w, dt in outs],
        compiler_params=_params(("parallel",)),
    )(*arrs, *params)
    return res[0] if len(outs) == 1 else res


def _row_bwd(f, rows, params, cts, *, need=None, adds=None, row_dtypes=None, name, ts=256):
    arrs, specs = zip(*[_row_spec(r, ts) for r in rows])
    s = arrs[0].shape[0]
    ts = min(ts, s)
    n_r, n_p, n_c = len(rows), len(params), len(cts)
    need = [True] * n_r if need is None else need
    adds = {} if adds is None else adds
    add_keys = sorted(adds)
    row_dtypes = [_F32] * n_r if row_dtypes is None else row_dtypes
    needed = [j for j in range(n_r) if need[j]]
    widths = [specs[j].block_shape[1] for j in range(n_r)]

    def body(*refs):
        pos = 0
        r_refs = refs[pos:pos + n_r]; pos += n_r
        p_refs = refs[pos:pos + n_p]; pos += n_p
        c_refs = refs[pos:pos + n_c]; pos += n_c
        a_refs = refs[pos:pos + len(add_keys)]; pos += len(add_keys)
        dr_refs = refs[pos:pos + len(needed)]; pos += len(needed)
        dp_refs = refs[pos:pos + n_p]
        rv = [r[...].astype(_F32) for r in r_refs]
        pv = [p[...] for p in p_refs]
        _, vjp = jax.vjp(lambda *a: _tup(f(*a)), *rv, *pv)
        g = vjp(tuple(c[...].astype(_F32) for c in c_refs))
        for o_ref, j in zip(dr_refs, needed):
            v = g[j]
            if j in adds:
                v = v + a_refs[add_keys.index(j)][...].astype(_F32)
            o_ref[...] = v.astype(o_ref.dtype)
        if n_p:
            @pl.when(pl.program_id(0) == 0)
            def _():
                for dp in dp_refs:
                    dp[...] = jnp.zeros_like(dp)
            for dp, v in zip(dp_refs, g[n_r:]):
                dp[...] += v

    ct_specs = [pl.BlockSpec((ts, c.shape[1]), lambda i: (i, 0)) for c in cts]
    add_specs = [pl.BlockSpec((ts, adds[j].shape[1]), lambda i: (i, 0)) for j in add_keys]
    res = pl.pallas_call(
        body, name=name, grid=(s // ts,),
        in_specs=list(specs) + [pl.BlockSpec(p.shape, lambda i: (0, 0)) for p in params] + ct_specs + add_specs,
        out_specs=[pl.BlockSpec((ts, widths[j]), lambda i: (i, 0)) for j in needed]
        + [pl.BlockSpec(p.shape, lambda i: (0, 0)) for p in params],
        out_shape=[jax.ShapeDtypeStruct((s, widths[j]), row_dtypes[j]) for j in needed]
        + [jax.ShapeDtypeStruct(p.shape, _F32) for p in params],
        compiler_params=_params(("arbitrary",)),
    )(*arrs, *params, *cts, *[adds[j] for j in add_keys])
    return list(res[:len(needed)]), list(res[len(needed):])


_DN = {"nn": (((1,), (0,)), ((), ())), "nt": (((1,), (1,)), ((), ())), "tn": (((0,), (0,)), ((), ()))}


def _make_dot(passes):
    def raw(a, b, kind):
        dn = _DN[kind]
        if passes == 1 or _MXU == _F32:
            return lax.dot_general(a.astype(_MXU), b.astype(_MXU), dn, preferred_element_type=_F32, precision=_PREC)
        a_hi, b_hi = a.astype(_MXU), b.astype(_MXU)
        a_lo = (a - a_hi.astype(_F32)).astype(_MXU)
        b_lo = (b - b_hi.astype(_F32)).astype(_MXU)
        out = lax.dot_general(a_hi, b_hi, dn, preferred_element_type=_F32)
        out = out + lax.dot_general(a_lo, b_hi, dn, preferred_element_type=_F32)
        return out + lax.dot_general(a_hi, b_lo, dn, preferred_element_type=_F32)

    @functools.partial(jax.custom_vjp, nondiff_argnums=(2,))
    def dot(a, b, kind):
        return raw(a, b, kind)

    def fwd(a, b, kind):
        return raw(a, b, kind), (a, b)

    def bwd(kind, res, ct):
        a, b = res
        if kind == "nn":
            return raw(ct, b, "nt"), raw(a, ct, "tn")
        if kind == "nt":
            return raw(ct, b, "nn"), raw(ct, a, "tn")
        return raw(b, ct, "nt"), raw(a, ct, "nn")

    dot.defvjp(fwd, bwd)
    return dot


_dot1 = _make_dot(1)
_dot3 = _make_dot(3)


def _sig(v):
    return 1.0 / (1.0 + jnp.exp(-v))


def _silu(v):
    return v * _sig(v)


def _f_rms(x, g):
    return x * lax.rsqrt(jnp.mean(x * x, axis=-1, keepdims=True) + EPS) * g


def _f_gate(y, xs, z, dexp, g):
    v = (y + dexp * xs) * _silu(z)
    half = SSD_WIDTH // 2
    parts = []
    for grp in range(2):
        vg = v[:, grp * half:(grp + 1) * half]
        parts.append(vg * lax.rsqrt(jnp.mean(vg * vg, axis=-1, keepdims=True) + EPS) * g[:, grp * half:(grp + 1) * half])
    return jnp.concatenate(parts, axis=1)


def _f_ln(u, g, b):
    mu = jnp.mean(u, axis=-1, keepdims=True)
    var = jnp.mean(jnp.square(u - mu), axis=-1, keepdims=True)
    return _silu((u - mu) * lax.rsqrt(var + EPS) * g + b)


def _f_glu(a, g):
    return a * _sig(g)


def _f_swiglu(gate, up):
    return _silu(gate) * up


def _f_att(q, k, v):
    outs = []
    for h in range(X_HEADS):
        sl = slice(h * X_HEAD_DIM, (h + 1) * X_HEAD_DIM)
        s = _dot1(q[:, sl], k[:, sl], "nt") * (X_HEAD_DIM ** -0.5)
        s = s - lax.stop_gradient(jnp.max(s, axis=-1, keepdims=True))
        p = jnp.exp(s)
        p = p / jnp.sum(p, axis=-1, keepdims=True)
        outs.append(_dot1(p, v[:, sl], "nn"))
    return jnp.concatenate(outs, axis=1)


def _loss_bwd(x3, target, g, *, name, ts=256):
    s, d = x3.shape

    def f(x, t, gv):
        return 0.5 * jnp.sum(jnp.mean(jnp.square(_f_rms(x, gv) - t), axis=-1))

    def body(x_ref, t_ref, g_ref, dx_ref, dg_ref, l_ref):
        @pl.when(pl.program_id(0) == 0)
        def _():
            dg_ref[...] = jnp.zeros_like(dg_ref)
            l_ref[...] = jnp.zeros_like(l_ref)

        lv, (dx, dg) = jax.value_and_grad(f, argnums=(0, 2))(x_ref[...], t_ref[...], g_ref[...])
        dx_ref[...] = dx
        dg_ref[...] += dg
        l_ref[...] += lv

    row = pl.BlockSpec((ts, d), lambda i: (i, 0))
    return pl.pallas_call(
        body, name=name, grid=(s // ts,),
        in_specs=[row, row, pl.BlockSpec((1, d), lambda i: (0, 0))],
        out_specs=[row, pl.BlockSpec((1, d), lambda i: (0, 0)), pl.BlockSpec((SUBLANES, LANES), lambda i: (0, 0))],
        out_shape=[jax.ShapeDtypeStruct((s, d), _F32), jax.ShapeDtypeStruct((1, d), _F32),
                   jax.ShapeDtypeStruct((SUBLANES, LANES), _F32)],
        compiler_params=_params(("arbitrary",)),
    )(x3, target, g)


_CONV_PAD = 32
_CONV_ROWS = 128
_CONV_CB = 128


def _conv_taps(k_taps):
    groups = {}
    for k in range(k_taps):
        j = k_taps - 1 - k
        groups.setdefault(j % SUBLANES, []).append((k, j))
    return groups


def _conv_window(win, wv, groups, init):
    pad, rows = _CONV_PAD, _CONV_ROWS
    acc = init
    for rot, taps in groups.items():
        rolled = win if rot == 0 else pltpu.roll(win, rot, 0)
        for k, j in taps:
            off = pad - (j - rot)
            acc = acc + rolled[off:off + rows, :] * wv[k:k + 1, :]
    return acc


def _conv_fill(x_refs, xp_ref, s, glu):
    pad, cb = _CONV_PAD, _CONV_CB
    step = _pick(s, (512, 256, _CONV_ROWS))
    xp_ref[0:pad, :] = jnp.zeros((pad, cb), _F32)

    def fill(r, carry):
        base = pl.multiple_of(r * step, step)
        v = x_refs[0][pl.ds(base, step), :].astype(_F32)
        if glu:
            v = v * _sig(x_refs[1][pl.ds(base, step), :].astype(_F32))
        xp_ref[pl.ds(pad + base, step), :] = v
        return carry

    lax.fori_loop(0, s // step, fill, 0)


def _conv_fwd(xs, w, b, k_taps, *, glu=False, act=False, name):
    s, c = xs[0].shape
    kp = w.shape[0]
    pad, rows, cb = _CONV_PAD, _CONV_ROWS, _CONV_CB
    groups = _conv_taps(k_taps)
    n_in = len(xs)

    def body(*refs):
        x_refs = refs[:n_in]
        w_ref, b_ref, o_ref, xp_ref = refs[n_in:]
        _conv_fill(x_refs, xp_ref, s, glu)
        wv = w_ref[...]
        bias = jnp.broadcast_to(b_ref[...], (rows, cb))

        def chunk(r, carry):
            base = pl.multiple_of(r * rows, rows)
            acc = _conv_window(xp_ref[pl.ds(base, rows + pad), :], wv, groups, bias)
            o_ref[pl.ds(base, rows), :] = _silu(acc) if act else acc
            return carry

        lax.fori_loop(0, s // rows, chunk, 0)

    col = pl.BlockSpec((s, cb), lambda i: (0, i))
    return pl.pallas_call(
        body, name=name, grid=(c // cb,),
        in_specs=[col] * n_in + [pl.BlockSpec((kp, cb), lambda i: (0, i)), pl.BlockSpec((1, cb), lambda i: (0, i))],
        out_specs=col, out_shape=jax.ShapeDtypeStruct((s, c), _F32),
        scratch_shapes=[pltpu.VMEM((s + pad, cb), _F32)],
        compiler_params=_params(("parallel",)),
    )(*xs, w, b)


def _conv_bwd(xs, w, b, dy, k_taps, *, glu=False, act=False, name):
    s, c = xs[0].shape
    kp = w.shape[0]
    pad, rows, cb = _CONV_PAD, _CONV_ROWS, _CONV_CB
    groups = _conv_taps(k_taps)
    win_rows = rows + pad
    n_in = len(xs)

    def fold(v):
        acc = v[0:SUBLANES, :]
        for i in range(1, rows // SUBLANES):
            acc = acc + v[i * SUBLANES:(i + 1) * SUBLANES, :]
        return acc

    def body(*refs):
        x_refs = refs[:n_in]
        w_ref, b_ref, dy_ref = refs[n_in:n_in + 3]
        dx_refs = refs[n_in + 3:2 * n_in + 3]
        dw_ref, db_ref, xp_ref, dyp_ref, acc_ref, dbacc_ref = refs[2 * n_in + 3:]
        _conv_fill(x_refs, xp_ref, s, glu)
        dyp_ref[s:s + pad, :] = jnp.zeros((pad, cb), _F32)
        acc_ref[...] = jnp.zeros_like(acc_ref)
        dbacc_ref[...] = jnp.zeros_like(dbacc_ref)
        wv = w_ref[...]
        bias = jnp.broadcast_to(b_ref[...], (rows, cb))

        def through_act(r, carry):
            base = pl.multiple_of(r * rows, rows)
            d = dy_ref[pl.ds(base, rows), :]
            if act:
                pre = _conv_window(xp_ref[pl.ds(base, win_rows), :], wv, groups, bias)
                sg = _sig(pre)
                d = d * (sg * (1.0 + pre * (1.0 - sg)))
            dyp_ref[pl.ds(base, rows), :] = d
            return carry

        lax.fori_loop(0, s // rows, through_act, 0)

        def chunk(r, carry):
            base = pl.multiple_of(r * rows, rows)
            xwin = xp_ref[pl.ds(base, win_rows), :]
            dwin = dyp_ref[pl.ds(base, win_rows), :]
            dyc = dwin[0:rows, :]
            dxacc = jnp.zeros((rows, cb), _F32)
            for rot, taps in groups.items():
                xr = xwin if rot == 0 else pltpu.roll(xwin, rot, 0)
                dr = dwin if rot == 0 else pltpu.roll(dwin, win_rows - rot, 0)
                for k, j in taps:
                    a8 = j - rot
                    dxacc = dxacc + dr[a8:a8 + rows, :] * wv[k:k + 1, :]
                    prod = dyc * xr[pad - a8:pad - a8 + rows, :]
                    acc_ref[k * SUBLANES:(k + 1) * SUBLANES, :] += fold(prod)
            dbacc_ref[...] += fold(dyc)
            if glu:
                av = x_refs[0][pl.ds(base, rows), :].astype(_F32)
                sg = _sig(x_refs[1][pl.ds(base, rows), :].astype(_F32))
                dx_refs[0][pl.ds(base, rows), :] = (dxacc * sg).astype(dx_refs[0].dtype)
                dx_refs[1][pl.ds(base, rows), :] = (dxacc * av * sg * (1.0 - sg)).astype(dx_refs[1].dtype)
            else:
                dx_refs[0][pl.ds(base, rows), :] = dxacc
            return carry

        lax.fori_loop(0, s // rows, chunk, 0)
        dw_ref[...] = jnp.zeros_like(dw_ref)
        for k in range(k_taps):
            dw_ref[k:k + 1, :] = jnp.sum(acc_ref[k * SUBLANES:(k + 1) * SUBLANES, :], axis=0, keepdims=True)
        db_ref[...] = jnp.sum(dbacc_ref[...], axis=0, keepdims=True)

    col = pl.BlockSpec((s, cb), lambda i: (0, i))
    wspec = pl.BlockSpec((kp, cb), lambda i: (0, i))
    bspec = pl.BlockSpec((1, cb), lambda i: (0, i))
    dx_dtype = xs[0].dtype if glu else _F32
    res = pl.pallas_call(
        body, name=name, grid=(c // cb,),
        in_specs=[col] * n_in + [wspec, bspec, col], out_specs=[col] * n_in + [wspec, bspec],
        out_shape=[jax.ShapeDtypeStruct((s, c), dx_dtype)] * n_in
        + [jax.ShapeDtypeStruct((kp, c), _F32), jax.ShapeDtypeStruct((1, c), _F32)],
        scratch_shapes=[pltpu.VMEM((s + pad, cb), _F32), pltpu.VMEM((s + pad, cb), _F32),
                        pltpu.VMEM((kp * SUBLANES, cb), _F32), pltpu.VMEM((SUBLANES, cb), _F32)],
        compiler_params=_params(("parallel",)),
    )(*xs, w, b, dy)
    return list(res[:n_in]), res[n_in], res[n_in + 1]


def _tri_sum(v, lower):
    l = v.shape[0]
    r, c = lax.broadcasted_iota(jnp.int32, (l, l), 0), lax.broadcasted_iota(jnp.int32, (l, l), 1)
    tri = ((r >= c) if lower else (r <= c)).astype(jnp.bfloat16)
    hi = v.astype(jnp.bfloat16)
    r1 = v - hi.astype(_F32)
    mid = r1.astype(jnp.bfloat16)
    lo = (r1 - mid.astype(_F32)).astype(jnp.bfloat16)
    out = jnp.zeros_like(v)
    for part in (hi, mid, lo):
        out = out + lax.dot_general(tri, part, _DN["nn"], preferred_element_type=_F32)
    return out


@jax.custom_vjp
def _cumsum_rows(v):
    return _tri_sum(v, True)


_cumsum_rows.defvjp(lambda v: (_tri_sum(v, True), None), lambda _, ct: (_tri_sum(ct, False),))


def _ssd_chunk(xbc, dtraw, prev, bias, alog):
    l = xbc.shape[0]
    xs = xbc[:, :SSD_WIDTH]
    bm = xbc[:, SSD_WIDTH:SSD_WIDTH + 2 * SSD_STATE]
    cm = xbc[:, SSD_WIDTH + 2 * SSD_STATE:]
    v = dtraw + bias
    dt = jnp.maximum(v, 0.0) + jnp.log1p(jnp.exp(-jnp.abs(v)))
    a_neg = -jnp.exp(alog)
    acs = _cumsum_rows(dt * a_neg)
    acs_t = acs.T
    dt_t = dt.T
    total = acs[l - 1:l, :]
    row = lax.broadcasted_iota(jnp.int32, (l, l), 0)
    colv = lax.broadcasted_iota(jnp.int32, (l, l), 1)
    causal = row >= colv
    lane_lo = lax.broadcasted_iota(jnp.int32, (l, LANES), 1) < HEAD_DIM
    row_lo = lax.broadcasted_iota(jnp.int32, (LANES, SSD_STATE), 0) < HEAD_DIM

    def pair_lanes(m, h0):
        return jnp.where(lane_lo, m[:, h0:h0 + 1], m[:, h0 + 1:h0 + 2])

    ys, news = [], []
    cb = {}
    for j in range(SSD_HEADS // 2):
        h0 = 2 * j
        grp = h0 // (SSD_HEADS // 2)
        bg = bm[:, grp * SSD_STATE:(grp + 1) * SSD_STATE]
        cg = cm[:, grp * SSD_STATE:(grp + 1) * SSD_STATE]
        if grp not in cb:
            cb[grp] = _dot1(cg, bg, "nt")
        xp = xs[:, j * LANES:(j + 1) * LANES]
        y = jnp.zeros((l, LANES), _F32)
        for hh, mask in ((h0, lane_lo), (h0 + 1, jnp.logical_not(lane_lo))):
            seg = acs[:, hh:hh + 1] - acs_t[hh:hh + 1, :]
            dec = jnp.exp(jnp.where(causal, seg, -jnp.inf))
            sc = cb[grp] * dec * dt_t[hh:hh + 1, :]
            y = y + _dot1(sc, jnp.where(mask, xp, 0.0), "nn")
        acs_p = pair_lanes(acs, h0)
        prev_p = prev[j * LANES:(j + 1) * LANES, :]
        y = y + _dot1(cg, prev_p, "nt") * jnp.exp(acs_p)
        wgt = jnp.exp(pair_lanes(jnp.broadcast_to(total, (l, LANES)), h0) - acs_p) * pair_lanes(dt, h0)
        st = _dot1(xp * wgt, bg, "tn")
        cdec = jnp.exp(jnp.where(row_lo, total[:, h0:h0 + 1], total[:, h0 + 1:h0 + 2]))
        news.append(prev_p * cdec + st)
        ys.append(y)
    return jnp.concatenate(ys, axis=1), jnp.concatenate(news, axis=0)


def _ssd_fwd(xbc, dtraw, bias, alog, *, name):
    s = xbc.shape[0]
    nc = s // CHUNK
    nstate = SSD_HEADS * HEAD_DIM

    def body(x_ref, dt_ref, b_ref, a_ref, y_ref, st_ref, state_ref):
        @pl.when(pl.program_id(0) == 0)
        def _():
            state_ref[...] = jnp.zeros_like(state_ref)

        prev = state_ref[...]
        st_ref[...] = prev
        y, new = _ssd_chunk(x_ref[...], dt_ref[...], prev, b_ref[...], a_ref[...])
        y_ref[...] = y
        state_ref[...] = new

    small = pl.BlockSpec((1, LANES), lambda i: (0, 0))
    return pl.pallas_call(
        body, name=name, grid=(nc,),
        in_specs=[pl.BlockSpec((CHUNK, XBC_WIDTH), lambda i: (i, 0)), pl.BlockSpec((CHUNK, LANES), lambda i: (i, 0)),
                  small, small],
        out_specs=[pl.BlockSpec((CHUNK, SSD_WIDTH), lambda i: (i, 0)),
                   pl.BlockSpec((None, nstate, SSD_STATE), lambda i: (i, 0, 0))],
        out_shape=[jax.ShapeDtypeStruct((s, SSD_WIDTH), _F32), jax.ShapeDtypeStruct((nc, nstate, SSD_STATE), _F32)],
        scratch_shapes=[pltpu.VMEM((nstate, SSD_STATE), _F32)],
        compiler_params=_params(("arbitrary",)),
    )(xbc, dtraw, bias, alog)


def _ssd_bwd(xbc, dtraw, states, bias, alog, dy, dxs_extra, *, name):
    s = xbc.shape[0]
    nc = s // CHUNK
    nstate = SSD_HEADS * HEAD_DIM

    def body(x_ref, dt_ref, st_ref, b_ref, a_ref, dy_ref, ex_ref, dx_ref, ddt_ref, db_ref, da_ref, dstate_ref):
        @pl.when(pl.program_id(0) == 0)
        def _():
            dstate_ref[...] = jnp.zeros_like(dstate_ref)
            db_ref[...] = jnp.zeros_like(db_ref)
            da_ref[...] = jnp.zeros_like(da_ref)

        _, vjp = jax.vjp(_ssd_chunk, x_ref[...], dt_ref[...], st_ref[...], b_ref[...], a_ref[...])
        dx, ddt, dprev, db, da = vjp((dy_ref[...], dstate_ref[...]))
        dx_ref[:, :SSD_WIDTH] = dx[:, :SSD_WIDTH] + ex_ref[...]
        dx_ref[:, SSD_WIDTH:] = dx[:, SSD_WIDTH:]
        ddt_ref[...] = ddt
        db_ref[...] += db
        da_ref[...] += da
        dstate_ref[...] = dprev

    rev = lambda i: (nc - 1 - i, 0)
    small = pl.BlockSpec((1, LANES), lambda i: (0, 0))
    return pl.pallas_call(
        body, name=name, grid=(nc,),
        in_specs=[pl.BlockSpec((CHUNK, XBC_WIDTH), rev), pl.BlockSpec((CHUNK, LANES), rev),
                  pl.BlockSpec((None, nstate, SSD_STATE), lambda i: (nc - 1 - i, 0, 0)), small, small,
                  pl.BlockSpec((CHUNK, SSD_WIDTH), rev), pl.BlockSpec((CHUNK, SSD_WIDTH), rev)],
        out_specs=[pl.BlockSpec((CHUNK, XBC_WIDTH), rev), pl.BlockSpec((CHUNK, LANES), rev), small, small],
        out_shape=[jax.ShapeDtypeStruct((s, XBC_WIDTH), _F32), jax.ShapeDtypeStruct((s, LANES), _F32),
                   jax.ShapeDtypeStruct((1, LANES), _F32), jax.ShapeDtypeStruct((1, LANES), _F32)],
        scratch_shapes=[pltpu.VMEM((nstate, SSD_STATE), _F32)],
        compiler_params=_params(("arbitrary",)),
    )(xbc, dtraw, states, bias, alog, dy, dxs_extra)


def _pad_cols(a, width):
    return jnp.pad(a, ((0, 0), (0, width - a.shape[1])))


def _pad_rows(a, rows):
    return jnp.pad(a, ((0, rows - a.shape[0]), (0, 0)))


def _tie(a, token):
    return a + token[0:1, 0:1].astype(a.dtype)


def _local_step(x, mem, target, w, fetch, emit):
    bf = _MXU
    d = D_MODEL
    h = _row_fwd(_f_rms, [x], [w['norm_mix_g']], [(d, bf)], name="f_norm_mix")
    w_in = fetch('in', h)['w_in']
    z_end, xbc_end, dt_end = SSD_WIDTH, SSD_WIDTH + XBC_WIDTH, SSD_WIDTH + XBC_WIDTH + SSD_HEADS
    w_z, w_xbc = w_in[:z_end], w_in[z_end:xbc_end]
    w_dt = _pad_rows(w_in[xbc_end:dt_end], LANES)
    w_a, w_g = w_in[dt_end:dt_end + CF_WIDTH], w_in[dt_end + CF_WIDTH:]
    dt_bias = _pad_cols(w['ssd_dt_bias'], LANES)
    a_log = _pad_cols(w['ssd_A_log'], LANES)
    d_exp = jnp.repeat(w['ssd_D'], HEAD_DIM, axis=1)
    g_final = w['norm_final_g'].reshape(1, D_MODEL)

    z, xbc, dtr, ga, gg = _mm_fan_out(h, [w_z, w_xbc, w_dt, w_a, w_g], tb=True, out_dtypes=[bf, _F32, _F32, bf, bf],
                                      name="f_in")
    wc = fetch('conv', xbc)
    ssd_w = _pad_rows(wc['ssd_conv_w'], SUBLANES)
    cf_w = _pad_rows(wc['cf_conv_w'], 32)
    xbc_a = _conv_fwd([xbc], ssd_w, w['ssd_conv_b'], SSD_CONV, act=True, name="f_ssd_conv")
    y_ssd, states = _ssd_fwd(xbc_a, dtr, dt_bias, a_log, name="f_ssd")
    xs_win = (xbc_a, SSD_WIDTH, 0)
    y_n = _row_fwd(_f_gate, [y_ssd, xs_win, z], [d_exp, w['ssd_norm_g']], [(d, bf)], name="f_ssd_gate")
    u_c = _conv_fwd([ga, gg], cf_w, w['cf_conv_b'], CF_CONV, glu=True, name="f_cf_conv")
    u = _row_fwd(_f_ln, [u_c], [w['cf_ln_g'], w['cf_ln_b']], [(d, bf)], name="f_cf_ln")
    wm = fetch('mid', y_n)
    w_out_y, w_out_u = wm['w_out'][:SSD_WIDTH], wm['w_out'][SSD_WIDTH:]
    x1 = _mm_fan_in([(y_n, w_out_y), (u, w_out_u)], add=x, name="f_out")
    hq = _row_fwd(_f_rms, [x1], [w['norm_xattn_g']], [(d, bf)], name="f_norm_xattn")
    q = _mm(hq, wm['w_q'], out_dtype=bf, name="f_q")
    memn = _row_fwd(_f_rms, [mem], [w['norm_mem_g']], [(d, bf)], name="f_norm_mem")
    kv = _mm(memn, wm['w_kv'], name="f_kv")
    k_mat, v_mat = kv[:, :d], kv[:, d:]
    o = _row_fwd(_f_att, [q], [k_mat, v_mat], [(d, bf)], name="f_att")
    x2 = _mm(o, wm['w_o'], add=x1, name="f_o")
    hf = _row_fwd(_f_rms, [x2], [w['norm_ffn_g']], [(d, bf)], name="f_norm_ffn")
    wf = fetch('ffn', hf)
    gate, up, act = _mm_fan_out(hf, [wf['w_gate'], wf['w_up']], tb=True, out_dtypes=[bf, bf], epilogue=_f_swiglu,
                                extra_outs=[(D_FF, bf)], tm=256, name="f_ffn_in")
    x3 = _mm(act, wf['w_down'], add=x2, name="f_down")

    dx3, dg_final, loss = _loss_bwd(x3, target, g_final, name="b_loss")
    g = {'norm_final_g': dg_final.reshape(d)}

    dact = _mm(dx3, wf['w_down'], tb=True, out_dtype=bf, name="b_down_x")
    dw_down = _mm(act, dx3, ta=True, name="b_down_w")
    def swiglu_bwd(gate_t, up_t, dact_t):
        return jax.vjp(_f_swiglu, gate_t, up_t)[1](dact_t)

    dhf, dgate, dup = _mm_fan_in([(None, wf['w_gate']), (None, wf['w_up'])], prologue=swiglu_bwd, pro_ins=[gate, up, dact],
                                 pro_out_dtypes=[bf, bf], out_dtype=bf, tm=256, name="b_ffn_in_x")
    sent = emit({'w_down': dw_down, 'w_gate': _mm(dgate, hf, ta=True, name="b_gate_w"),
                 'w_up': _mm(dup, hf, ta=True, name="b_up_w")})
    (dx2,), (g['norm_ffn_g'],) = _row_bwd(_f_rms, [x2], [_tie(w['norm_ffn_g'], sent)], [dhf], adds={0: dx3}, name="b_norm_ffn")

    do = _mm(dx2, wm['w_o'], tb=True, out_dtype=bf, name="b_o_x")
    dw_o = _mm(o, dx2, ta=True, name="b_o_w")
    (dq,), (dk, dv) = _row_bwd(_f_att, [q], [k_mat, v_mat], [do], row_dtypes=[bf], name="b_att")
    dw_q = _mm(hq, dq, ta=True, name="b_q_w")
    dhq = _mm(dq, wm['w_q'], tb=True, out_dtype=bf, name="b_q_x")
    (dx1,), (g['norm_xattn_g'],) = _row_bwd(_f_rms, [x1], [w['norm_xattn_g']], [dhq], adds={0: dx2}, name="b_norm_xattn")
    dkv = jnp.concatenate([dk, dv], axis=1)
    sent = emit({'w_o': dw_o, 'w_q': dw_q, 'w_kv': _mm(memn, dkv, ta=True, name="b_kv_w")})
    dmemn = _mm(dkv, wm['w_kv'], tb=True, name="b_kv_x")
    _, (g['norm_mem_g'],) = _row_bwd(_f_rms, [mem], [w['norm_mem_g']], [dmemn], need=[False], name="b_norm_mem")

    dyn, du = _mm_fan_out(dx1, [w_out_y, w_out_u], tb=True, out_dtypes=[bf, bf], name="b_out_x")
    (du_c,), (g['cf_ln_g'], g['cf_ln_b']) = _row_bwd(_f_ln, [u_c], [_tie(w['cf_ln_g'], sent), w['cf_ln_b']], [du], name="b_cf_ln")
    sent = emit({'w_out': jnp.concatenate([_mm(y_n, dx1, ta=True, name="b_out_y_w"), _mm(u, dx1, ta=True, name="b_out_u_w")], axis=0)})
    (dga, dgg), dcf_w, g['cf_conv_b'] = _conv_bwd([ga, gg], cf_w, w['cf_conv_b'], du_c, CF_CONV, glu=True, name="b_cf_conv")
    g['cf_conv_w'] = dcf_w[:CF_CONV]
    (dy_ssd, dxs, dz), (dd_exp, g['ssd_norm_g']) = _row_bwd(
        _f_gate, [y_ssd, xs_win, z], [d_exp, _tie(w['ssd_norm_g'], sent)], [dyn], row_dtypes=[_F32, _F32, bf], name="b_ssd_gate")
    g['ssd_D'] = jnp.sum(dd_exp.reshape(SSD_HEADS, HEAD_DIM), axis=1).reshape(1, SSD_HEADS)
    dxbc_a, ddtr, ddt_bias, da_log = _ssd_bwd(xbc_a, dtr, states, dt_bias, a_log, dy_ssd, dxs, name="b_ssd")
    g['ssd_dt_bias'] = ddt_bias[:, :SSD_HEADS]
    g['ssd_A_log'] = da_log[:, :SSD_HEADS]
    (dxbc,), dssd_w, g['ssd_conv_b'] = _conv_bwd([xbc], ssd_w, w['ssd_conv_b'], dxbc_a, SSD_CONV, act=True, name="b_ssd_conv")
    g['ssd_conv_w'] = dssd_w[:SSD_CONV]

    sent = emit({'w_in': jnp.concatenate([
        _mm(dz, h, ta=True, name="b_in_z_w"), _mm(dxbc, h, ta=True, name="b_in_xbc_w"),
        _mm(ddtr, h, ta=True, name="b_in_dt_w")[:SSD_HEADS],
        _mm(dga, h, ta=True, name="b_in_a_w"), _mm(dgg, h, ta=True, name="b_in_g_w")], axis=0)})
    dh = _mm_fan_in([(dz, w_z), (dxbc, w_xbc), (ddtr, _tie(w_dt, sent)), (dga, w_a), (dgg, w_g)], out_dtype=bf,
                    name="b_in_x")
    (dx,), (g['norm_mix_g'],) = _row_bwd(_f_rms, [x], [w['norm_mix_g']], [dh], adds={0: dx1}, name="b_norm_mix")
    return loss, dx, g


_ANY = pl.BlockSpec(memory_space=pl.ANY)


def _place():
    x, y, c = lax.axis_index("x"), lax.axis_index("y"), lax.axis_index("c")
    return x, y, c


def _all_gather(arrs, *, name):
    n = len(arrs)

    def body(*refs):
        ins, outs = refs[:n], refs[n:2 * n]
        send_sems, recv_sems, local_sems = refs[2 * n:]
        x, y, c = _place()
        me, sibling = (x, y, c), (x, y, 1 - c)
        chips = [(1 - x, y), (x, 1 - y), (1 - x, 1 - y)]

        def slot(a, dev):
            return outs[a].at[4 * dev[0] + 2 * dev[1] + dev[2]]

        def copy(a, k, block, to, src=None):
            return pltpu.make_async_remote_copy(
                src_ref=slot(a, block) if src is None else src, dst_ref=slot(a, block),
                send_sem=send_sems.at[a, k], recv_sem=recv_sems.at[a, k], device_id=to, device_id_type=MESH)

        mine = [pltpu.make_async_copy(ins[a], slot(a, me), local_sems.at[a]) for a in range(n)]
        for cp in mine:
            cp.start()
        first = []
        for a in range(n):
            first.append(copy(a, 0, me, sibling, src=ins[a]))
            first += [copy(a, 1 + j, me, (*chip, c), src=ins[a]) for j, chip in enumerate(chips)]
        for cp in first:
            cp.start()
        passed = []
        for a in range(n):
            for j, chip in enumerate(chips):
                copy(a, 1 + j, (*chip, c), me).wait_recv()
                fwd = copy(a, 4 + j, (*chip, c), sibling)
                fwd.start()
                passed.append(fwd)
        for a in range(n):
            copy(a, 0, sibling, me).wait_recv()
            for j, chip in enumerate(chips):
                copy(a, 4 + j, (*chip, 1 - c), me).wait_recv()
        for cp in first + passed:
            cp.wait_send()
        for cp in mine:
            cp.wait()

    return pl.pallas_call(
        body, name=name, in_specs=[_ANY] * n, out_specs=[_ANY] * n,
        out_shape=[jax.ShapeDtypeStruct((N_DEV,) + a.shape, a.dtype) for a in arrs],
        scratch_shapes=[pltpu.SemaphoreType.DMA((n, 7)), pltpu.SemaphoreType.DMA((n, 7)), pltpu.SemaphoreType.DMA((n,))],
    )(*arrs)


_HBM = pl.BlockSpec(memory_space=pltpu.HBM)
_SEM = pl.BlockSpec(memory_space=pltpu.SEMAPHORE)
_EFFECT = pltpu.SideEffectType.DATAFLOW_SIDE_EFFECTING
_FLIPS = [(dx, dy, dc) for dx in (0, 1) for dy in (0, 1) for dc in (0, 1)][1:]


def _peer(flip, x, y, c):
    return (1 - x if flip[0] else x, 1 - y if flip[1] else y, 1 - c if flip[2] else c)


def _send_start(srcs, blocked, *, after=None, name):
    n = len(srcs)
    lands = [jax.ShapeDtypeStruct(s.shape if blocked else (N_DEV,) + s.shape, s.dtype) for s in srcs]
    n_in = 2 * n + (after is not None)

    def body(*refs):
        src_refs, land_refs = refs[:n], refs[n:2 * n]
        send_sems, recv_sems = refs[n_in], refs[n_in + 1]
        token = refs[-1]
        x, y, c = _place()
        me = 4 * x + 2 * y + c
        for a in range(n):
            for k, flip in enumerate(_FLIPS):
                p = _peer(flip, x, y, c)
                src = src_refs[a].at[4 * p[0] + 2 * p[1] + p[2]] if blocked else src_refs[a]
                pltpu.make_async_remote_copy(
                    src_ref=src, dst_ref=land_refs[a].at[me], send_sem=send_sems.at[7 * a + k], recv_sem=recv_sems.at[7 * a + k],
                    device_id=p, device_id_type=MESH).start()
        token[...] = jnp.zeros_like(token)

    res = pl.pallas_call(
        body, name=name,
        out_shape=(pltpu.SemaphoreType.DMA((7 * n,)), pltpu.SemaphoreType.DMA((7 * n,)),
                   *[pltpu.HBM(s.shape, s.dtype) for s in srcs], *[pltpu.HBM(l.shape, l.dtype) for l in lands],
                   jax.ShapeDtypeStruct((SUBLANES, LANES), _F32)),
        in_specs=[_HBM] * (2 * n) + [_ANY] * (after is not None),
        out_specs=(_SEM, _SEM, *[_HBM] * (2 * n), pl.BlockSpec(memory_space=pltpu.VMEM)),
        input_output_aliases={i: 2 + i for i in range(2 * n)},
        compiler_params=pltpu.CompilerParams(has_side_effects=_EFFECT),
    )(*[pltpu.with_memory_space_constraint(s, pltpu.HBM) for s in srcs],
      *[pltpu.with_memory_space_constraint(lax.empty(l.shape, l.dtype), pltpu.HBM) for l in lands],
      *([after] if after is not None else []))
    return res[0], res[1], list(res[2:2 + n]), list(res[2 + n:2 + 2 * n]), res[-1]


def _send_wait(handles, after, blocked, *, name):
    send_sems, recv_sems, srcs, lands, _ = handles
    n = len(srcs)

    def body(*refs):
        src_refs, land_refs = refs[:n], refs[n:2 * n]
        send_sems, recv_sems = refs[2 * n], refs[2 * n + 1]
        x, y, c = _place()
        for a in range(n):
            for k, flip in enumerate(_FLIPS):
                p = _peer(flip, x, y, c)
                pid = 4 * p[0] + 2 * p[1] + p[2]
                cp = pltpu.make_async_remote_copy(
                    src_ref=src_refs[a].at[pid] if blocked else src_refs[a], dst_ref=land_refs[a].at[pid],
                    send_sem=send_sems.at[7 * a + k], recv_sem=recv_sems.at[7 * a + k], device_id=p, device_id_type=MESH)
                cp.wait_send()
                cp.wait_recv()

    res = pl.pallas_call(
        body, name=name,
        out_shape=tuple(pltpu.HBM(s.shape, s.dtype) for s in srcs + lands),
        in_specs=[_HBM] * (2 * n) + [_SEM, _SEM, _ANY], out_specs=tuple([_HBM] * (2 * n)),
        input_output_aliases={i: i for i in range(2 * n)},
        compiler_params=pltpu.CompilerParams(has_side_effects=_EFFECT),
    )(*srcs, *lands, send_sems, recv_sems, after)
    return list(res[:n]), list(res[n:])


def _adamw(parts, w, m, v, *, own=None, me=None, name):
    p, r, c = parts.shape
    tr = _pick(r, (256, 176, 128, 64, 32, 16, 8))
    if own is not None:
        tc = c if tr < r else _pick(c, (256, 128))
        return _adamw_own(parts, own, me, w, m, v, tr, tc, name=name)

    def body(p_ref, w_ref, m_ref, v_ref, g_ref, d_ref, nm_ref, nv_ref):
        g = p_ref[0].astype(_F32)
        for i in range(1, p):
            g = g + p_ref[i].astype(_F32)
        _adamw_math(g, w_ref, m_ref, v_ref, g_ref, d_ref, nm_ref, nv_ref)

    blk = pl.BlockSpec((tr, c), lambda i: (i, 0))
    return pl.pallas_call(
        body, name=name, grid=(r // tr,),
        in_specs=[pl.BlockSpec((p, tr, c), lambda i: (0, i, 0)), blk, blk, blk], out_specs=[blk] * 4,
        out_shape=[jax.ShapeDtypeStruct((r, c), _F32)] * 4,
        compiler_params=_params(("parallel",)),
    )(parts, w, m, v)


def _adamw_math(g, w_ref, m_ref, v_ref, g_ref, d_ref, nm_ref, nv_ref):
    wv = w_ref[...]
    mn = ADAM_B1 * m_ref[...] + (1.0 - ADAM_B1) * g
    vn = ADAM_B2 * v_ref[...] + (1.0 - ADAM_B2) * jnp.square(g)
    m_hat = mn / (1.0 - ADAM_B1 ** ADAM_STEP)
    v_hat = vn / (1.0 - ADAM_B2 ** ADAM_STEP)
    g_ref[...] = g
    d_ref[...] = -ADAM_LR * (m_hat / (jnp.sqrt(v_hat) + ADAM_EPS) + ADAM_WD * wv)
    nm_ref[...] = mn
    nv_ref[...] = vn


def _adamw_own(parts, own, me, w, m, v, tr, tc, *, name):
    p, r, c = parts.shape

    def body(me_ref, p_ref, own_ref, w_ref, m_ref, v_ref, g_ref, d_ref, nm_ref, nv_ref):
        mine = own_ref[...].astype(_F32)
        g = jnp.where(me_ref[0] == 0, mine, p_ref[0].astype(_F32))
        for i in range(1, p):
            g = g + jnp.where(me_ref[0] == i, mine, p_ref[i].astype(_F32))
        _adamw_math(g, w_ref, m_ref, v_ref, g_ref, d_ref, nm_ref, nv_ref)

    blk = pl.BlockSpec((tr, tc), lambda i, j, me_ref: (i, j))
    grid_spec = pltpu.PrefetchScalarGridSpec(
        num_scalar_prefetch=1, grid=(r // tr, c // tc),
        in_specs=[pl.BlockSpec((p, tr, tc), lambda i, j, me_ref: (0, i, j)),
                  pl.BlockSpec((None, tr, tc), lambda i, j, me_ref: (me_ref[0], i, j)), blk, blk, blk],
        out_specs=[blk] * 4)
    return pl.pallas_call(
        body, name=name, grid_spec=grid_spec, out_shape=[jax.ShapeDtypeStruct((r, c), _F32)] * 4,
        compiler_params=_params(("parallel", "parallel")),
    )(me.reshape(1).astype(jnp.int32), parts, own, w, m, v)


def _sum_parts(parts, *, name):
    p, r, c = parts.shape

    def body(p_ref, o_ref):
        g = p_ref[0].astype(_F32)
        for i in range(1, p):
            g = g + p_ref[i].astype(_F32)
        o_ref[...] = g

    return pl.pallas_call(body, name=name, out_shape=jax.ShapeDtypeStruct((r, c), _F32))(parts)


def _pack(vals, rows):
    flat = jnp.concatenate([v.reshape(-1) for v in vals])
    return jnp.pad(flat, (0, rows * LANES - flat.shape[0])).reshape(rows, LANES)


def _unpack(packed, shapes):
    flat = packed.reshape(-1)
    out, pos = [], 0
    for shp in shapes:
        size = math.prod(shp)
        out.append(flat[pos:pos + size].reshape(shp))
        pos += size
    return out


def _pack_rows(shapes):
    total = sum(math.prod(s) for s in shapes)
    return -(-total // (LANES * SUBLANES)) * SUBLANES


def kernel(x, mem, norm_mix_g, w_in, ssd_conv_w, ssd_conv_b, ssd_dt_bias, ssd_A_log, ssd_D, ssd_norm_g, cf_conv_w, cf_conv_b, cf_ln_g, cf_ln_b, w_out, norm_xattn_g, norm_mem_g, w_q, w_kv, w_o, norm_ffn_g, w_gate, w_up, w_down, norm_final_g, loss_target, m_norm_mix_g, m_w_in, m_ssd_conv_w, m_ssd_conv_b, m_ssd_dt_bias, m_ssd_A_log, m_ssd_D, m_ssd_norm_g, m_cf_conv_w, m_cf_conv_b, m_cf_ln_g, m_cf_ln_b, m_w_out, m_norm_xattn_g, m_norm_mem_g, m_w_q, m_w_kv, m_w_o, m_norm_ffn_g, m_w_gate, m_w_up, m_w_down, m_norm_final_g, v_norm_mix_g, v_w_in, v_ssd_conv_w, v_ssd_conv_b, v_ssd_dt_bias, v_ssd_A_log, v_ssd_D, v_ssd_norm_g, v_cf_conv_w, v_cf_conv_b, v_cf_ln_g, v_cf_ln_b, v_w_out, v_norm_xattn_g, v_norm_mem_g, v_w_q, v_w_kv, v_w_o, v_norm_ffn_g, v_w_gate, v_w_up, v_w_down, v_norm_final_g):
    args = dict(locals())
    wts = {n: args[n] for n in WEIGHT_NAMES}
    mom = {n: args["m_" + n] for n in WEIGHT_NAMES}
    var = {n: args["v_" + n] for n in WEIGHT_NAMES}
    me = 4 * lax.axis_index("x") + 2 * lax.axis_index("y") + lax.axis_index("c")

    groups = {'in': ['w_in'], 'conv': ['ssd_conv_w', 'cf_conv_w'], 'mid': ['w_out', 'w_q', 'w_kv', 'w_o'],
              'ffn': ['w_gate', 'w_up', 'w_down']}
    def shard(n, a):
        return jnp.transpose(a[0], (1, 0)) if n in TRANSPOSED else a[0]

    gathers, started = {}, None
    for grp, names in groups.items():
        shards = [wts[n][0] if grp == 'conv' else shard(n, wts[n]).astype(_MXU) for n in names]
        gathers[grp] = _send_start(shards, False, after=started, name="gather_%s_start" % grp)
        started = gathers[grp][4]

    def fetch(grp, after):
        srcs, lands = _send_wait(gathers[grp], started if after is None else after, False, name="gather_%s_wait" % grp)
        out = {}
        for n, own, gth in zip(groups[grp], srcs, lands):
            gth = lax.dynamic_update_slice_in_dim(gth, own[None], me, axis=0)
            if n == 'w_kv' or grp == 'conv':
                out[n] = jnp.transpose(gth, (1, 0, 2)).reshape(gth.shape[1], N_DEV * gth.shape[2])
            else:
                out[n] = gth.reshape(N_DEV * gth.shape[1], gth.shape[2])
        return out

    exchanges = []

    def emit(grads):
        blocks = []
        for n, gw in grads.items():
            if n == 'w_kv':
                gw = jnp.transpose(gw.reshape(gw.shape[0], N_DEV, gw.shape[1] // N_DEV), (1, 0, 2))
            else:
                gw = gw.reshape(N_DEV, gw.shape[0] // N_DEV, gw.shape[1])
            blocks.append(gw.astype(jnp.bfloat16))
        first = next(iter(grads))
        exchanges.append((list(grads), _send_start(blocks, True, name="exchange_%s_start" % first), first))
        return exchanges[-1][1][4]

    full = {n: wts[n] for n in WEIGHT_NAMES if n not in BIG and n not in groups['conv']}
    full['norm_mix_g'] = _tie(norm_mix_g, started)

    loss_blk, grad_x, g = _local_step(x[0], mem[0], loss_target[0], full, fetch, emit)
    loss = lax.psum(loss_blk[0, 0], ("x", "y", "c"))

    out_g, out_d, out_m, out_v = {}, {}, {}, {}
    for names, handles, first in exchanges:
        srcs, lands = _send_wait(handles, grad_x, True, name="exchange_%s_wait" % first)
        for n, own, parts in zip(names, srcs, lands):
            res = _adamw(parts, shard(n, wts[n]), shard(n, mom[n]), shard(n, var[n]), own=own, me=me, name="adamw_" + n)
            out_g[n], out_d[n], out_m[n], out_v[n] = [(jnp.transpose(r, (1, 0)) if n in TRANSPOSED else r)[None] for r in res]

    small = [n for n in WEIGHT_NAMES if n not in BIG]
    small_shapes = [g[n].shape for n in small]
    rows = _pack_rows(small_shapes)
    (small_parts,) = _all_gather([_pack([g[n] for n in small], rows)], name="gather_small_grads")
    small_sum = dict(zip(small, _unpack(_sum_parts(small_parts, name="sum_small_grads"), small_shapes)))
    small_sum['ssd_conv_w'] = lax.dynamic_slice_in_dim(small_sum['ssd_conv_w'], me * (XBC_WIDTH // N_DEV), XBC_WIDTH // N_DEV, axis=1)[None]
    small_sum['cf_conv_w'] = lax.dynamic_slice_in_dim(small_sum['cf_conv_w'], me * (CF_WIDTH // N_DEV), CF_WIDTH // N_DEV, axis=1)[None]
    shard_shapes = [wts[n].shape for n in small]
    rows2 = _pack_rows(shard_shapes)
    res = _adamw(_pack([small_sum[n] for n in small], rows2)[None], _pack([wts[n] for n in small], rows2),
                 _pack([mom[n] for n in small], rows2), _pack([var[n] for n in small], rows2), name="adamw_small")
    for dst, packed in zip((out_g, out_d, out_m, out_v), res):
        dst.update(zip(small, _unpack(packed, shard_shapes)))

    return (loss, grad_x[None], *[out_g[n] for n in WEIGHT_NAMES], *[out_d[n] for n in WEIGHT_NAMES],
            *[out_m[n] for n in WEIGHT_NAMES], *[out_v[n] for n in WEIGHT_NAMES])
```

```python
import functools
import math

import jax
import jax.numpy as jnp
from jax import lax
from jax.experimental import pallas as pl
from jax.experimental.pallas import tpu as pltpu

_F32 = jnp.float32
_MXU = jnp.bfloat16
_PREC = None
_VMEM_LIMIT = 56 * 1024 * 1024

D_MODEL = 1024
HEAD_DIM = 64
SSD_HEADS = 16
SSD_WIDTH = 1024
SSD_STATE = 128
SSD_CONV = 4
CHUNK = 128
XBC_WIDTH = 1536
CF_WIDTH = 1024
CF_CONV = 31
X_HEADS = 4
X_HEAD_DIM = 256
D_FF = 2816
EPS = 1e-6
N_DEV = 8
LANES = 128
SUBLANES = 8

ADAM_LR = 0.001
ADAM_B1 = 0.9
ADAM_B2 = 0.999
ADAM_EPS = 1e-08
ADAM_WD = 0.01
ADAM_STEP = 10

MESH = pl.DeviceIdType.MESH
WEIGHT_NAMES = ['norm_mix_g', 'w_in', 'ssd_conv_w', 'ssd_conv_b', 'ssd_dt_bias', 'ssd_A_log', 'ssd_D', 'ssd_norm_g',
                'cf_conv_w', 'cf_conv_b', 'cf_ln_g', 'cf_ln_b', 'w_out', 'norm_xattn_g', 'norm_mem_g', 'w_q', 'w_kv',
                'w_o', 'norm_ffn_g', 'w_gate', 'w_up', 'w_down', 'norm_final_g']
BIG = ['w_in', 'w_out', 'w_q', 'w_kv', 'w_o', 'w_gate', 'w_up', 'w_down']
TRANSPOSED = ('w_in', 'w_gate', 'w_up')


def _params(sem=None):
    return pltpu.CompilerParams(dimension_semantics=sem, vmem_limit_bytes=_VMEM_LIMIT)


def _pick(n, cands):
    for c in cands:
        if n % c == 0:
            return c
    return n


def _mm(a, b, *, ta=False, tb=False, add=None, out_dtype=_F32, name):
    (kdim, m) = a.shape if ta else a.shape[::-1]
    (n, k2) = b.shape if tb else b.shape[::-1]
    assert kdim == k2, (a.shape, b.shape, ta, tb)
    if ta:
        tm = m if m <= 1024 else _pick(m, (1408, 1024, 512, 256, 128))
        tn = n if n <= 1536 else _pick(n, (1408, 1024, 512, 256, 128))
        tk = _pick(kdim, (1024, 512, 256, 128))
    else:
        tm = _pick(m, (512, 256, 128))
        tn = n if n <= 2816 else _pick(n, (1408, 1024, 512, 256, 128))
        tk = kdim if kdim <= 2816 else _pick(kdim, (1408, 1024, 512, 256, 128))
    nk = kdim // tk
    dn = (((0 if ta else 1,), (1 if tb else 0,)), ((), ()))

    def body(*refs):
        a_ref, b_ref = refs[0], refs[1]
        add_ref = refs[2] if add is not None else None
        o_ref = refs[3 if add is not None else 2]
        acc_ref = refs[-1]
        k = pl.program_id(2)
        prod = lax.dot_general(a_ref[...].astype(_MXU), b_ref[...].astype(_MXU), dn,
                               preferred_element_type=_F32, precision=_PREC)

        def finish(r):
            if add_ref is not None:
                r = r + add_ref[...].astype(_F32)
            o_ref[...] = r.astype(o_ref.dtype)

        if nk == 1:
            finish(prod)
            return

        @pl.when(k == 0)
        def _():
            acc_ref[...] = prod

        @pl.when(jnp.logical_and(k > 0, k < nk - 1))
        def _():
            acc_ref[...] += prod

        @pl.when(k == nk - 1)
        def _():
            finish(acc_ref[...] + prod)

    a_spec = pl.BlockSpec((tk, tm), lambda i, j, k: (k, i)) if ta else pl.BlockSpec((tm, tk), lambda i, j, k: (i, k))
    b_spec = pl.BlockSpec((tn, tk), lambda i, j, k: (j, k)) if tb else pl.BlockSpec((tk, tn), lambda i, j, k: (k, j))
    o_spec = pl.BlockSpec((tm, tn), lambda i, j, k: (i, j))
    ins, specs = [a, b], [a_spec, b_spec]
    if add is not None:
        ins.append(add)
        specs.append(o_spec)
    return pl.pallas_call(
        body, name=name, grid=(m // tm, n // tn, nk), in_specs=specs, out_specs=o_spec,
        out_shape=jax.ShapeDtypeStruct((m, n), out_dtype),
        scratch_shapes=[pltpu.VMEM((tm, tn), _F32)] if nk > 1 else [],
        compiler_params=_params(("parallel", "parallel", "arbitrary")),
    )(*ins)


def _resident(shape):
    return pl.BlockSpec(shape, lambda i: (0,) * len(shape), pipeline_mode=pl.Buffered(1))


def _mm_fan_out(a, bs, *, tb, out_dtypes, epilogue=None, extra_outs=(), tm=512, name):
    m, kdim = a.shape
    tm = min(tm, m)
    ns = [b.shape[0] if tb else b.shape[1] for b in bs]
    nb = len(bs)
    kind = "nt" if tb else "nn"

    def body(*refs):
        a_ref, b_refs, o_refs = refs[0], refs[1:1 + nb], refs[1 + nb:]
        av = a_ref[...].astype(_MXU)
        prods = [lax.dot_general(av, b[...].astype(_MXU), _DN[kind], preferred_element_type=_F32, precision=_PREC)
                 for b in b_refs]
        for o_ref, p in zip(o_refs[:nb], prods):
            o_ref[...] = p.astype(o_ref.dtype)
        if epilogue is not None:
            for o_ref, v in zip(o_refs[nb:], _tup(epilogue(*prods))):
                o_ref[...] = v.astype(o_ref.dtype)

    widths = ns + [w for w, _ in extra_outs]
    dtypes = list(out_dtypes) + [dt for _, dt in extra_outs]
    return pl.pallas_call(
        body, name=name, grid=(m // tm,),
        in_specs=[pl.BlockSpec((tm, kdim), lambda i: (i, 0))] + [_resident(b.shape) for b in bs],
        out_specs=[pl.BlockSpec((tm, w), lambda i: (i, 0)) for w in widths],
        out_shape=[jax.ShapeDtypeStruct((m, w), dt) for w, dt in zip(widths, dtypes)],
        compiler_params=_params(("parallel",)),
    )(a, *bs)


def _mm_fan_in(pairs, *, add=None, out_dtype=_F32, prologue=None, pro_ins=(), pro_out_dtypes=(), epilogue=None,
               tm=512, name):
    bs = [b for _, b in pairs]
    nb = len(bs)
    n = bs[0].shape[1]
    rows_in = list(pro_ins) if prologue is not None else [a for a, _ in pairs]
    m = rows_in[0].shape[0]
    tm = min(tm, m)
    n_r = len(rows_in)

    def body(*refs):
        r_refs, b_refs = refs[:n_r], refs[n_r:n_r + nb]
        pos = n_r + nb
        add_ref = refs[pos] if add is not None else None
        pos += add is not None
        epi_ref = refs[pos] if epilogue is not None else None
        pos += epilogue is not None
        o_ref, po_refs = refs[pos], refs[pos + 1:]
        if prologue is not None:
            a_vals = _tup(prologue(*[r[...].astype(_F32) for r in r_refs]))
            for po, v in zip(po_refs, a_vals):
                po[...] = v.astype(po.dtype)
        else:
            a_vals = [r[...] for r in r_refs]
        acc = None
        for av, b in zip(a_vals, b_refs):
            p = lax.dot_general(av.astype(_MXU), b[...].astype(_MXU), _DN["nn"], preferred_element_type=_F32,
                                precision=_PREC)
            acc = p if acc is None else acc + p
        if add_ref is not None:
            acc = acc + add_ref[...].astype(_F32)
        o_ref[...] = acc.astype(o_ref.dtype)
        if epilogue is not None:
            po_refs[-1][...] = epilogue[0](acc, epi_ref[...]).astype(po_refs[-1].dtype)

    row = lambda w: pl.BlockSpec((tm, w), lambda i: (i, 0))
    ins = rows_in + bs + ([add] if add is not None else []) + ([epilogue[1]] if epilogue is not None else [])
    in_specs = ([row(r.shape[1]) for r in rows_in] + [_resident(b.shape) for b in bs]
                + ([row(n)] if add is not None else []) + ([_resident(epilogue[1].shape)] if epilogue is not None else []))
    extra = [(b.shape[0], dt) for b, dt in zip(bs, pro_out_dtypes)] if prologue is not None else []
    if epilogue is not None:
        extra.append((n, epilogue[2]))
    res = pl.pallas_call(
        body, name=name, grid=(m // tm,), in_specs=in_specs,
        out_specs=[row(n)] + [row(w) for w, _ in extra],
        out_shape=[jax.ShapeDtypeStruct((m, n), out_dtype)] + [jax.ShapeDtypeStruct((m, w), dt) for w, dt in extra],
        compiler_params=_params(("parallel",)),
    )(*ins)
    return res if extra else res[0]


def _row_spec(r, ts):
    if isinstance(r, tuple):
        arr, width, cblk = r
        return arr, pl.BlockSpec((ts, width), lambda i, cblk=cblk: (i, cblk))
    return r, pl.BlockSpec((ts, r.shape[1]), lambda i: (i, 0))


def _tup(v):
    return tuple(v) if isinstance(v, (tuple, list)) else (v,)


def _row_fwd(f, rows, params, outs, *, name, ts=256):
    arrs, specs = zip(*[_row_spec(r, ts) for r in rows])
    s = arrs[0].shape[0]
    ts = min(ts, s)
    n_r, n_p = len(rows), len(params)

    def body(*refs):
        rv = [r[...].astype(_F32) for r in refs[:n_r]]
        pv = [p[...] for p in refs[n_r:n_r + n_p]]
        res = _tup(f(*rv, *pv))
        for o_ref, v in zip(refs[n_r + n_p:], res):
            o_ref[...] = v.astype(o_ref.dtype)

    res = pl.pallas_call(
        body, name=name, grid=(s // ts,),
        in_specs=list(specs) + [pl.BlockSpec(p.shape, lambda i: (0, 0)) for p in params],
        out_specs=[pl.BlockSpec((ts, w), lambda i: (i, 0)) for w, _ in outs],
        out_shape=[jax.ShapeDtypeStruct((s, w), dt) for w, dt in outs],
        compiler_params=_params(("parallel",)),
    )(*arrs, *params)
    return res[0] if len(outs) == 1 else res


def _row_bwd(f, rows, params, cts, *, need=None, adds=None, row_dtypes=None, name, ts=256):
    arrs, specs = zip(*[_row_spec(r, ts) for r in rows])
    s = arrs[0].shape[0]
    ts = min(ts, s)
    n_r, n_p, n_c = len(rows), len(params), len(cts)
    need = [True] * n_r if need is None else need
    adds = {} if adds is None else adds
    add_keys = sorted(adds)
    row_dtypes = [_F32] * n_r if row_dtypes is None else row_dtypes
    needed = [j for j in range(n_r) if need[j]]
    widths = [specs[j].block_shape[1] for j in range(n_r)]

    def body(*refs):
        pos = 0
        r_refs = refs[pos:pos + n_r]; pos += n_r
        p_refs = refs[pos:pos + n_p]; pos += n_p
        c_refs = refs[pos:pos + n_c]; pos += n_c
        a_refs = refs[pos:pos + len(add_keys)]; pos += len(add_keys)
        dr_refs = refs[pos:pos + len(needed)]; pos += len(needed)
        dp_refs = refs[pos:pos + n_p]
        rv = [r[...].astype(_F32) for r in r_refs]
        pv = [p[...] for p in p_refs]
        _, vjp = jax.vjp(lambda *a: _tup(f(*a)), *rv, *pv)
        g = vjp(tuple(c[...].astype(_F32) for c in c_refs))
        for o_ref, j in zip(dr_refs, needed):
            v = g[j]
            if j in adds:
                v = v + a_refs[add_keys.index(j)][...].astype(_F32)
            o_ref[...] = v.astype(o_ref.dtype)
        if n_p:
            @pl.when(pl.program_id(0) == 0)
            def _():
                for dp in dp_refs:
                    dp[...] = jnp.zeros_like(dp)
            for dp, v in zip(dp_refs, g[n_r:]):
                dp[...] += v

    ct_specs = [pl.BlockSpec((ts, c.shape[1]), lambda i: (i, 0)) for c in cts]
    add_specs = [pl.BlockSpec((ts, adds[j].shape[1]), lambda i: (i, 0)) for j in add_keys]
    res = pl.pallas_call(
        body, name=name, grid=(s // ts,),
        in_specs=list(specs) + [pl.BlockSpec(p.shape, lambda i: (0, 0)) for p in params] + ct_specs + add_specs,
        out_specs=[pl.BlockSpec((ts, widths[j]), lambda i: (i, 0)) for j in needed]
        + [pl.BlockSpec(p.shape, lambda i: (0, 0)) for p in params],
        out_shape=[jax.ShapeDtypeStruct((s, widths[j]), row_dtypes[j]) for j in needed]
        + [jax.ShapeDtypeStruct(p.shape, _F32) for p in params],
        compiler_params=_params(("arbitrary",)),
    )(*arrs, *params, *cts, *[adds[j] for j in add_keys])
    return list(res[:len(needed)]), list(res[len(needed):])


_DN = {"nn": (((1,), (0,)), ((), ())), "nt": (((1,), (1,)), ((), ())), "tn": (((0,), (0,)), ((), ()))}


def _make_dot(passes):
    def raw(a, b, kind):
        dn = _DN[kind]
        if passes == 1 or _MXU == _F32:
            return lax.dot_general(a.astype(_MXU), b.astype(_MXU), dn, preferred_element_type=_F32, precision=_PREC)
        a_hi, b_hi = a.astype(_MXU), b.astype(_MXU)
        a_lo = (a - a_hi.astype(_F32)).astype(_MXU)
        b_lo = (b - b_hi.astype(_F32)).astype(_MXU)
        out = lax.dot_general(a_hi, b_hi, dn, preferred_element_type=_F32)
        out = out + lax.dot_general(a_lo, b_hi, dn, preferred_element_type=_F32)
        return out + lax.dot_general(a_hi, b_lo, dn, preferred_element_type=_F32)

    @functools.partial(jax.custom_vjp, nondiff_argnums=(2,))
    def dot(a, b, kind):
        return raw(a, b, kind)

    def fwd(a, b, kind):
        return raw(a, b, kind), (a, b)

    def bwd(kind, res, ct):
        a, b = res
        if kind == "nn":
            return raw(ct, b, "nt"), raw(a, ct, "tn")
        if kind == "nt":
            return raw(ct, b, "nn"), raw(ct, a, "tn")
        return raw(b, ct, "nt"), raw(a, ct, "nn")

    dot.defvjp(fwd, bwd)
    return dot


_dot1 = _make_dot(1)
_dot3 = _make_dot(3)


def _sig(v):
    return 1.0 / (1.0 + jnp.exp(-v))


def _silu(v):
    return v * _sig(v)


def _f_rms(x, g):
    return x * lax.rsqrt(jnp.mean(x * x, axis=-1, keepdims=True) + EPS) * g


def _f_gate(y, xs, z, dexp, g):
    v = (y + dexp * xs) * _silu(z)
    half = SSD_WIDTH // 2
    parts = []
    for grp in range(2):
        vg = v[:, grp * half:(grp + 1) * half]
        parts.append(vg * lax.rsqrt(jnp.mean(vg * vg, axis=-1, keepdims=True) + EPS) * g[:, grp * half:(grp + 1) * half])
    return jnp.concatenate(parts, axis=1)


def _f_ln(u, g, b):
    mu = jnp.mean(u, axis=-1, keepdims=True)
    var = jnp.mean(jnp.square(u - mu), axis=-1, keepdims=True)
    return _silu((u - mu) * lax.rsqrt(var + EPS) * g + b)


def _f_glu(a, g):
    return a * _sig(g)


def _f_swiglu(gate, up):
    return _silu(gate) * up


def _f_att(q, k, v):
    outs = []
    for h in range(X_HEADS):
        sl = slice(h * X_HEAD_DIM, (h + 1) * X_HEAD_DIM)
        s = _dot1(q[:, sl], k[:, sl], "nt") * (X_HEAD_DIM ** -0.5)
        s = s - lax.stop_gradient(jnp.max(s, axis=-1, keepdims=True))
        p = jnp.exp(s)
        p = p / jnp.sum(p, axis=-1, keepdims=True)
        outs.append(_dot1(p, v[:, sl], "nn"))
    return jnp.concatenate(outs, axis=1)


def _loss_bwd(x3, target, g, *, name, ts=256):
    s, d = x3.shape

    def f(x, t, gv):
        return 0.5 * jnp.sum(jnp.mean(jnp.square(_f_rms(x, gv) - t), axis=-1))

    def body(x_ref, t_ref, g_ref, dx_ref, dg_ref, l_ref):
        @pl.when(pl.program_id(0) == 0)
        def _():
            dg_ref[...] = jnp.zeros_like(dg_ref)
            l_ref[...] = jnp.zeros_like(l_ref)

        lv, (dx, dg) = jax.value_and_grad(f, argnums=(0, 2))(x_ref[...], t_ref[...], g_ref[...])
        dx_ref[...] = dx
        dg_ref[...] += dg
        l_ref[...] += lv

    row = pl.BlockSpec((ts, d), lambda i: (i, 0))
    return pl.pallas_call(
        body, name=name, grid=(s // ts,),
        in_specs=[row, row, pl.BlockSpec((1, d), lambda i: (0, 0))],
        out_specs=[row, pl.BlockSpec((1, d), lambda i: (0, 0)), pl.BlockSpec((SUBLANES, LANES), lambda i: (0, 0))],
        out_shape=[jax.ShapeDtypeStruct((s, d), _F32), jax.ShapeDtypeStruct((1, d), _F32),
                   jax.ShapeDtypeStruct((SUBLANES, LANES), _F32)],
        compiler_params=_params(("arbitrary",)),
    )(x3, target, g)


_CONV_PAD = 32
_CONV_ROWS = 128
_CONV_CB = 128


def _conv_taps(k_taps):
    groups = {}
    for k in range(k_taps):
        j = k_taps - 1 - k
        groups.setdefault(j % SUBLANES, []).append((k, j))
    return groups


def _conv_window(win, wv, groups, init):
    pad, rows = _CONV_PAD, _CONV_ROWS
    acc = init
    for rot, taps in groups.items():
        rolled = win if rot == 0 else pltpu.roll(win, rot, 0)
        for k, j in taps:
            off = pad - (j - rot)
            acc = acc + rolled[off:off + rows, :] * wv[k:k + 1, :]
    return acc


def _conv_fill(x_refs, xp_ref, s, glu):
    pad, cb = _CONV_PAD, _CONV_CB
    step = _pick(s, (512, 256, _CONV_ROWS))
    xp_ref[0:pad, :] = jnp.zeros((pad, cb), _F32)

    def fill(r, carry):
        base = pl.multiple_of(r * step, step)
        v = x_refs[0][pl.ds(base, step), :].astype(_F32)
        if glu:
            v = v * _sig(x_refs[1][pl.ds(base, step), :].astype(_F32))
        xp_ref[pl.ds(pad + base, step), :] = v
        return carry

    lax.fori_loop(0, s // step, fill, 0)


def _conv_fwd(xs, w, b, k_taps, *, glu=False, act=False, name):
    s, c = xs[0].shape
    kp = w.shape[0]
    pad, rows, cb = _CONV_PAD, _CONV_ROWS, _CONV_CB
    groups = _conv_taps(k_taps)
    n_in = len(xs)

    def body(*refs):
        x_refs = refs[:n_in]
        w_ref, b_ref, o_ref, xp_ref = refs[n_in:]
        _conv_fill(x_refs, xp_ref, s, glu)
        wv = w_ref[...]
        bias = jnp.broadcast_to(b_ref[...], (rows, cb))

        def chunk(r, carry):
            base = pl.multiple_of(r * rows, rows)
            acc = _conv_window(xp_ref[pl.ds(base, rows + pad), :], wv, groups, bias)
            o_ref[pl.ds(base, rows), :] = _silu(acc) if act else acc
            return carry

        lax.fori_loop(0, s // rows, chunk, 0)

    col = pl.BlockSpec((s, cb), lambda i: (0, i))
    return pl.pallas_call(
        body, name=name, grid=(c // cb,),
        in_specs=[col] * n_in + [pl.BlockSpec((kp, cb), lambda i: (0, i)), pl.BlockSpec((1, cb), lambda i: (0, i))],
        out_specs=col, out_shape=jax.ShapeDtypeStruct((s, c), _F32),
        scratch_shapes=[pltpu.VMEM((s + pad, cb), _F32)],
        compiler_params=_params(("parallel",)),
    )(*xs, w, b)


def _conv_bwd(xs, w, b, dy, k_taps, *, glu=False, act=False, name):
    s, c = xs[0].shape
    kp = w.shape[0]
    pad, rows, cb = _CONV_PAD, _CONV_ROWS, _CONV_CB
    groups = _conv_taps(k_taps)
    win_rows = rows + pad
    n_in = len(xs)

    def fold(v):
        acc = v[0:SUBLANES, :]
        for i in range(1, rows // SUBLANES):
            acc = acc + v[i * SUBLANES:(i + 1) * SUBLANES, :]
        return acc

    def body(*refs):
        x_refs = refs[:n_in]
        w_ref, b_ref, dy_ref = refs[n_in:n_in + 3]
        dx_refs = refs[n_in + 3:2 * n_in + 3]
        dw_ref, db_ref, xp_ref, dyp_ref, acc_ref, dbacc_ref = refs[2 * n_in + 3:]
        _conv_fill(x_refs, xp_ref, s, glu)
        dyp_ref[s:s + pad, :] = jnp.zeros((pad, cb), _F32)
        acc_ref[...] = jnp.zeros_like(acc_ref)
        dbacc_ref[...] = jnp.zeros_like(dbacc_ref)
        wv = w_ref[...]
        bias = jnp.broadcast_to(b_ref[...], (rows, cb))

        def through_act(r, carry):
            base = pl.multiple_of(r * rows, rows)
            d = dy_ref[pl.ds(base, rows), :]
            if act:
                pre = _conv_window(xp_ref[pl.ds(base, win_rows), :], wv, groups, bias)
                sg = _sig(pre)
                d = d * (sg * (1.0 + pre * (1.0 - sg)))
            dyp_ref[pl.ds(base, rows), :] = d
            return carry

        lax.fori_loop(0, s // rows, through_act, 0)

        def chunk(r, carry):
            base = pl.multiple_of(r * rows, rows)
            xwin = xp_ref[pl.ds(base, win_rows), :]
            dwin = dyp_ref[pl.ds(base, win_rows), :]
            dyc = dwin[0:rows, :]
            dxacc = jnp.zeros((rows, cb), _F32)
            for rot, taps in groups.items():
                xr = xwin if rot == 0 else pltpu.roll(xwin, rot, 0)
                dr = dwin if rot == 0 else pltpu.roll(dwin, win_rows - rot, 0)
                for k, j in taps:
                    a8 = j - rot
                    dxacc = dxacc + dr[a8:a8 + rows, :] * wv[k:k + 1, :]
                    prod = dyc * xr[pad - a8:pad - a8 + rows, :]
                    acc_ref[k * SUBLANES:(k + 1) * SUBLANES, :] += fold(prod)
            dbacc_ref[...] += fold(dyc)
            if glu:
                av = x_refs[0][pl.ds(base, rows), :].astype(_F32)
                sg = _sig(x_refs[1][pl.ds(base, rows), :].astype(_F32))
                dx_refs[0][pl.ds(base, rows), :] = (dxacc * sg).astype(dx_refs[0].dtype)
                dx_refs[1][pl.ds(base, rows), :] = (dxacc * av * sg * (1.0 - sg)).astype(dx_refs[1].dtype)
            else:
                dx_refs[0][pl.ds(base, rows), :] = dxacc
            return carry

        lax.fori_loop(0, s // rows, chunk, 0)
        dw_ref[...] = jnp.zeros_like(dw_ref)
        for k in range(k_taps):
            dw_ref[k:k + 1, :] = jnp.sum(acc_ref[k * SUBLANES:(k + 1) * SUBLANES, :], axis=0, keepdims=True)
        db_ref[...] = jnp.sum(dbacc_ref[...], axis=0, keepdims=True)

    col = pl.BlockSpec((s, cb), lambda i: (0, i))
    wspec = pl.BlockSpec((kp, cb), lambda i: (0, i))
    bspec = pl.BlockSpec((1, cb), lambda i: (0, i))
    dx_dtype = xs[0].dtype if glu else _F32
    res = pl.pallas_call(
        body, name=name, grid=(c // cb,),
        in_specs=[col] * n_in + [wspec, bspec, col], out_specs=[col] * n_in + [wspec, bspec],
        out_shape=[jax.ShapeDtypeStruct((s, c), dx_dtype)] * n_in
        + [jax.ShapeDtypeStruct((kp, c), _F32), jax.ShapeDtypeStruct((1, c), _F32)],
        scratch_shapes=[pltpu.VMEM((s + pad, cb), _F32), pltpu.VMEM((s + pad, cb), _F32),
                        pltpu.VMEM((kp * SUBLANES, cb), _F32), pltpu.VMEM((SUBLANES, cb), _F32)],
        compiler_params=_params(("parallel",)),
    )(*xs, w, b, dy)
    return list(res[:n_in]), res[n_in], res[n_in + 1]


def _tri_sum(v, lower):
    l = v.shape[0]
    r, c = lax.broadcasted_iota(jnp.int32, (l, l), 0), lax.broadcasted_iota(jnp.int32, (l, l), 1)
    tri = ((r >= c) if lower else (r <= c)).astype(jnp.bfloat16)
    hi = v.astype(jnp.bfloat16)
    r1 = v - hi.astype(_F32)
    mid = r1.astype(jnp.bfloat16)
    lo = (r1 - mid.astype(_F32)).astype(jnp.bfloat16)
    out = jnp.zeros_like(v)
    for part in (hi, mid, lo):
        out = out + lax.dot_general(tri, part, _DN["nn"], preferred_element_type=_F32)
    return out


@jax.custom_vjp
def _cumsum_rows(v):
    return _tri_sum(v, True)


_cumsum_rows.defvjp(lambda v: (_tri_sum(v, True), None), lambda _, ct: (_tri_sum(ct, False),))


def _ssd_chunk(xbc, dtraw, prev, bias, alog):
    l = xbc.shape[0]
    xs = xbc[:, :SSD_WIDTH]
    bm = xbc[:, SSD_WIDTH:SSD_WIDTH + 2 * SSD_STATE]
    cm = xbc[:, SSD_WIDTH + 2 * SSD_STATE:]
    v = dtraw + bias
    dt = jnp.maximum(v, 0.0) + jnp.log1p(jnp.exp(-jnp.abs(v)))
    a_neg = -jnp.exp(alog)
    acs = _cumsum_rows(dt * a_neg)
    acs_t = acs.T
    dt_t = dt.T
    total = acs[l - 1:l, :]
    row = lax.broadcasted_iota(jnp.int32, (l, l), 0)
    colv = lax.broadcasted_iota(jnp.int32, (l, l), 1)
    causal = row >= colv
    lane_lo = lax.broadcasted_iota(jnp.int32, (l, LANES), 1) < HEAD_DIM
    row_lo = lax.broadcasted_iota(jnp.int32, (LANES, SSD_STATE), 0) < HEAD_DIM

    def pair_lanes(m, h0):
        return jnp.where(lane_lo, m[:, h0:h0 + 1], m[:, h0 + 1:h0 + 2])

    ys, news = [], []
    cb = {}
    for j in range(SSD_HEADS // 2):
        h0 = 2 * j
        grp = h0 // (SSD_HEADS // 2)
        bg = bm[:, grp * SSD_STATE:(grp + 1) * SSD_STATE]
        cg = cm[:, grp * SSD_STATE:(grp + 1) * SSD_STATE]
        if grp not in cb:
            cb[grp] = _dot1(cg, bg, "nt")
        xp = xs[:, j * LANES:(j + 1) * LANES]
        y = jnp.zeros((l, LANES), _F32)
        for hh, mask in ((h0, lane_lo), (h0 + 1, jnp.logical_not(lane_lo))):
            seg = acs[:, hh:hh + 1] - acs_t[hh:hh + 1, :]
            dec = jnp.exp(jnp.where(causal, seg, -jnp.inf))
            sc = cb[grp] * dec * dt_t[hh:hh + 1, :]
            y = y + _dot1(sc, jnp.where(mask, xp, 0.0), "nn")
        acs_p = pair_lanes(acs, h0)
        prev_p = prev[j * LANES:(j + 1) * LANES, :]
        y = y + _dot1(cg, prev_p, "nt") * jnp.exp(acs_p)
        wgt = jnp.exp(pair_lanes(jnp.broadcast_to(total, (l, LANES)), h0) - acs_p) * pair_lanes(dt, h0)
        st = _dot1(xp * wgt, bg, "tn")
        cdec = jnp.exp(jnp.where(row_lo, total[:, h0:h0 + 1], total[:, h0 + 1:h0 + 2]))
        news.append(prev_p * cdec + st)
        ys.append(y)
    return jnp.concatenate(ys, axis=1), jnp.concatenate(news, axis=0)


def _ssd_fwd(xbc, dtraw, bias, alog, *, name):
    s = xbc.shape[0]
    nc = s // CHUNK
    nstate = SSD_HEADS * HEAD_DIM

    def body(x_ref, dt_ref, b_ref, a_ref, y_ref, st_ref, state_ref):
        @pl.when(pl.program_id(0) == 0)
        def _():
            state_ref[...] = jnp.zeros_like(state_ref)

        prev = state_ref[...]
        st_ref[...] = prev
        y, new = _ssd_chunk(x_ref[...], dt_ref[...], prev, b_ref[...], a_ref[...])
        y_ref[...] = y
        state_ref[...] = new

    small = pl.BlockSpec((1, LANES), lambda i: (0, 0))
    return pl.pallas_call(
        body, name=name, grid=(nc,),
        in_specs=[pl.BlockSpec((CHUNK, XBC_WIDTH), lambda i: (i, 0)), pl.BlockSpec((CHUNK, LANES), lambda i: (i, 0)),
                  small, small],
        out_specs=[pl.BlockSpec((CHUNK, SSD_WIDTH), lambda i: (i, 0)),
                   pl.BlockSpec((None, nstate, SSD_STATE), lambda i: (i, 0, 0))],
        out_shape=[jax.ShapeDtypeStruct((s, SSD_WIDTH), _F32), jax.ShapeDtypeStruct((nc, nstate, SSD_STATE), _F32)],
        scratch_shapes=[pltpu.VMEM((nstate, SSD_STATE), _F32)],
        compiler_params=_params(("arbitrary",)),
    )(xbc, dtraw, bias, alog)


def _ssd_bwd(xbc, dtraw, states, bias, alog, dy, dxs_extra, *, name):
    s = xbc.shape[0]
    nc = s // CHUNK
    nstate = SSD_HEADS * HEAD_DIM

    def body(x_ref, dt_ref, st_ref, b_ref, a_ref, dy_ref, ex_ref, dx_ref, ddt_ref, db_ref, da_ref, dstate_ref):
        @pl.when(pl.program_id(0) == 0)
        def _():
            dstate_ref[...] = jnp.zeros_like(dstate_ref)
            db_ref[...] = jnp.zeros_like(db_ref)
            da_ref[...] = jnp.zeros_like(da_ref)

        _, vjp = jax.vjp(_ssd_chunk, x_ref[...], dt_ref[...], st_ref[...], b_ref[...], a_ref[...])
        dx, ddt, dprev, db, da = vjp((dy_ref[...], dstate_ref[...]))
        dx_ref[:, :SSD_WIDTH] = dx[:, :SSD_WIDTH] + ex_ref[...]
        dx_ref[:, SSD_WIDTH:] = dx[:, SSD_WIDTH:]
        ddt_ref[...] = ddt
        db_ref[...] += db
        da_ref[...] += da
        dstate_ref[...] = dprev

    rev = lambda i: (nc - 1 - i, 0)
    small = pl.BlockSpec((1, LANES), lambda i: (0, 0))
    return pl.pallas_call(
        body, name=name, grid=(nc,),
        in_specs=[pl.BlockSpec((CHUNK, XBC_WIDTH), rev), pl.BlockSpec((CHUNK, LANES), rev),
                  pl.BlockSpec((None, nstate, SSD_STATE), lambda i: (nc - 1 - i, 0, 0)), small, small,
                  pl.BlockSpec((CHUNK, SSD_WIDTH), rev), pl.BlockSpec((CHUNK, SSD_WIDTH), rev)],
        out_specs=[pl.BlockSpec((CHUNK, XBC_WIDTH), rev), pl.BlockSpec((CHUNK, LANES), rev), small, small],
        out_shape=[jax.ShapeDtypeStruct((s, XBC_WIDTH), _F32), jax.ShapeDtypeStruct((s, LANES), _F32),
                   jax.ShapeDtypeStruct((1, LANES), _F32), jax.ShapeDtypeStruct((1, LANES), _F32)],
        scratch_shapes=[pltpu.VMEM((nstate, SSD_STATE), _F32)],
        compiler_params=_params(("arbitrary",)),
    )(xbc, dtraw, states, bias, alog, dy, dxs_extra)


def _pad_cols(a, width):
    return jnp.pad(a, ((0, 0), (0, width - a.shape[1])))


def _pad_rows(a, rows):
    return jnp.pad(a, ((0, rows - a.shape[0]), (0, 0)))


def _tie(a, token):
    return a + token[0:1, 0:1].astype(a.dtype)


def _local_step(x, mem, target, w, fetch, emit):
    bf = _MXU
    d = D_MODEL
    h = _row_fwd(_f_rms, [x], [w['norm_mix_g']], [(d, bf)], name="f_norm_mix")
    w_in = fetch('in', h)['w_in']
    z_end, xbc_end, dt_end = SSD_WIDTH, SSD_WIDTH + XBC_WIDTH, SSD_WIDTH + XBC_WIDTH + SSD_HEADS
    w_z, w_xbc = w_in[:z_end], w_in[z_end:xbc_end]
    w_dt = _pad_rows(w_in[xbc_end:dt_end], LANES)
    w_a, w_g = w_in[dt_end:dt_end + CF_WIDTH], w_in[dt_end + CF_WIDTH:]
    dt_bias = _pad_cols(w['ssd_dt_bias'], LANES)
    a_log = _pad_cols(w['ssd_A_log'], LANES)
    d_exp = jnp.repeat(w['ssd_D'], HEAD_DIM, axis=1)
    g_final = w['norm_final_g'].reshape(1, D_MODEL)

    z, xbc, dtr, ga, gg = _mm_fan_out(h, [w_z, w_xbc, w_dt, w_a, w_g], tb=True, out_dtypes=[bf, _F32, _F32, bf, bf],
                                      name="f_in")
    wc = fetch('conv', xbc)
    ssd_w = _pad_rows(wc['ssd_conv_w'], SUBLANES)
    cf_w = _pad_rows(wc['cf_conv_w'], 32)
    xbc_a = _conv_fwd([xbc], ssd_w, w['ssd_conv_b'], SSD_CONV, act=True, name="f_ssd_conv")
    y_ssd, states = _ssd_fwd(xbc_a, dtr, dt_bias, a_log, name="f_ssd")
    xs_win = (xbc_a, SSD_WIDTH, 0)
    y_n = _row_fwd(_f_gate, [y_ssd, xs_win, z], [d_exp, w['ssd_norm_g']], [(d, bf)], name="f_ssd_gate")
    u_c = _conv_fwd([ga, gg], cf_w, w['cf_conv_b'], CF_CONV, glu=True, name="f_cf_conv")
    u = _row_fwd(_f_ln, [u_c], [w['cf_ln_g'], w['cf_ln_b']], [(d, bf)], name="f_cf_ln")
    wm = fetch('mid', y_n)
    w_out_y, w_out_u = wm['w_out'][:SSD_WIDTH], wm['w_out'][SSD_WIDTH:]
    x1, hq = _mm_fan_in([(y_n, w_out_y), (u, w_out_u)], add=x, epilogue=(_f_rms, w['norm_xattn_g'], bf), name="f_out")
    q = _mm(hq, wm['w_q'], out_dtype=bf, name="f_q")
    memn = _row_fwd(_f_rms, [mem], [w['norm_mem_g']], [(d, bf)], name="f_norm_mem")
    kv = _mm(memn, wm['w_kv'], name="f_kv")
    k_mat, v_mat = kv[:, :d], kv[:, d:]
    o = _row_fwd(_f_att, [q], [k_mat, v_mat], [(d, bf)], name="f_att")
    x2, hf = _mm_fan_in([(o, wm['w_o'])], add=x1, epilogue=(_f_rms, w['norm_ffn_g'], bf), name="f_o")
    wf = fetch('ffn', hf)
    gate, up, act = _mm_fan_out(hf, [wf['w_gate'], wf['w_up']], tb=True, out_dtypes=[bf, bf], epilogue=_f_swiglu,
                                extra_outs=[(D_FF, bf)], tm=256, name="f_ffn_in")
    x3 = _mm(act, wf['w_down'], add=x2, name="f_down")

    dx3, dg_final, loss = _loss_bwd(x3, target, g_final, name="b_loss")
    g = {'norm_final_g': dg_final.reshape(d)}

    dact = _mm(dx3, wf['w_down'], tb=True, out_dtype=bf, name="b_down_x")
    dw_down = _mm(act, dx3, ta=True, out_dtype=bf, name="b_down_w")
    def swiglu_bwd(gate_t, up_t, dact_t):
        return jax.vjp(_f_swiglu, gate_t, up_t)[1](dact_t)

    dhf, dgate, dup = _mm_fan_in([(None, wf['w_gate']), (None, wf['w_up'])], prologue=swiglu_bwd, pro_ins=[gate, up, dact],
                                 pro_out_dtypes=[bf, bf], out_dtype=bf, tm=256, name="b_ffn_in_x")
    sent = emit({'w_down': dw_down, 'w_gate': _mm(dgate, hf, ta=True, out_dtype=bf, name="b_gate_w"),
                 'w_up': _mm(dup, hf, ta=True, out_dtype=bf, name="b_up_w")})
    (dx2,), (g['norm_ffn_g'],) = _row_bwd(_f_rms, [x2], [_tie(w['norm_ffn_g'], sent)], [dhf], adds={0: dx3}, name="b_norm_ffn")

    do = _mm(dx2, wm['w_o'], tb=True, out_dtype=bf, name="b_o_x")
    dw_o = _mm(o, dx2, ta=True, out_dtype=bf, name="b_o_w")
    (dq,), (dk, dv) = _row_bwd(_f_att, [q], [k_mat, v_mat], [do], row_dtypes=[bf], name="b_att")
    dw_q = _mm(hq, dq, ta=True, out_dtype=bf, name="b_q_w")
    dhq = _mm(dq, wm['w_q'], tb=True, out_dtype=bf, name="b_q_x")
    (dx1,), (g['norm_xattn_g'],) = _row_bwd(_f_rms, [x1], [w['norm_xattn_g']], [dhq], adds={0: dx2}, name="b_norm_xattn")
    dkv = jnp.concatenate([dk, dv], axis=1)
    sent = emit({'w_o': dw_o, 'w_q': dw_q, 'w_kv': _mm(memn, dkv, ta=True, out_dtype=bf, name="b_kv_w")})
    dmemn = _mm(dkv, wm['w_kv'], tb=True, name="b_kv_x")
    _, (g['norm_mem_g'],) = _row_bwd(_f_rms, [mem], [w['norm_mem_g']], [dmemn], need=[False], name="b_norm_mem")

    dyn, du = _mm_fan_out(dx1, [w_out_y, w_out_u], tb=True, out_dtypes=[bf, bf], name="b_out_x")
    (du_c,), (g['cf_ln_g'], g['cf_ln_b']) = _row_bwd(_f_ln, [u_c], [_tie(w['cf_ln_g'], sent), w['cf_ln_b']], [du], name="b_cf_ln")
    sent = emit({'w_out': jnp.concatenate([_mm(y_n, dx1, ta=True, out_dtype=bf, name="b_out_y_w"), _mm(u, dx1, ta=True, out_dtype=bf, name="b_out_u_w")], axis=0)})
    (dga, dgg), dcf_w, g['cf_conv_b'] = _conv_bwd([ga, gg], cf_w, w['cf_conv_b'], du_c, CF_CONV, glu=True, name="b_cf_conv")
    g['cf_conv_w'] = dcf_w[:CF_CONV]
    (dy_ssd, dxs, dz), (dd_exp, g['ssd_norm_g']) = _row_bwd(
        _f_gate, [y_ssd, xs_win, z], [d_exp, _tie(w['ssd_norm_g'], sent)], [dyn], row_dtypes=[_F32, _F32, bf], name="b_ssd_gate")
    g['ssd_D'] = jnp.sum(dd_exp.reshape(SSD_HEADS, HEAD_DIM), axis=1).reshape(1, SSD_HEADS)
    dxbc_a, ddtr, ddt_bias, da_log = _ssd_bwd(xbc_a, dtr, states, dt_bias, a_log, dy_ssd, dxs, name="b_ssd")
    g['ssd_dt_bias'] = ddt_bias[:, :SSD_HEADS]
    g['ssd_A_log'] = da_log[:, :SSD_HEADS]
    (dxbc,), dssd_w, g['ssd_conv_b'] = _conv_bwd([xbc], ssd_w, w['ssd_conv_b'], dxbc_a, SSD_CONV, act=True, name="b_ssd_conv")
    g['ssd_conv_w'] = dssd_w[:SSD_CONV]

    sent = emit({'w_in': jnp.concatenate([
        _mm(dz, h, ta=True, out_dtype=bf, name="b_in_z_w"), _mm(dxbc, h, ta=True, out_dtype=bf, name="b_in_xbc_w"),
        _mm(ddtr, h, ta=True, out_dtype=bf, name="b_in_dt_w")[:SSD_HEADS],
        _mm(dga, h, ta=True, out_dtype=bf, name="b_in_a_w"), _mm(dgg, h, ta=True, out_dtype=bf, name="b_in_g_w")], axis=0)})
    dh = _mm_fan_in([(dz, w_z), (dxbc, w_xbc), (ddtr, _tie(w_dt, sent)), (dga, w_a), (dgg, w_g)], out_dtype=bf,
                    name="b_in_x")
    (dx,), (g['norm_mix_g'],) = _row_bwd(_f_rms, [x], [w['norm_mix_g']], [dh], adds={0: dx1}, name="b_norm_mix")
    return loss, dx, g


_ANY = pl.BlockSpec(memory_space=pl.ANY)


def _place():
    x, y, c = lax.axis_index("x"), lax.axis_index("y"), lax.axis_index("c")
    return x, y, c


def _all_gather(arrs, *, name):
    n = len(arrs)

    def body(*refs):
        ins, outs = refs[:n], refs[n:2 * n]
        send_sems, recv_sems, local_sems = refs[2 * n:]
        x, y, c = _place()
        me, sibling = (x, y, c), (x, y, 1 - c)
        chips = [(1 - x, y), (x, 1 - y), (1 - x, 1 - y)]

        def slot(a, dev):
            return outs[a].at[4 * dev[0] + 2 * dev[1] + dev[2]]

        def copy(a, k, block, to, src=None):
            return pltpu.make_async_remote_copy(
                src_ref=slot(a, block) if src is None else src, dst_ref=slot(a, block),
                send_sem=send_sems.at[a, k], recv_sem=recv_sems.at[a, k], device_id=to, device_id_type=MESH)

        mine = [pltpu.make_async_copy(ins[a], slot(a, me), local_sems.at[a]) for a in range(n)]
        for cp in mine:
            cp.start()
        first = []
        for a in range(n):
            first.append(copy(a, 0, me, sibling, src=ins[a]))
            first += [copy(a, 1 + j, me, (*chip, c), src=ins[a]) for j, chip in enumerate(chips)]
        for cp in first:
            cp.start()
        passed = []
        for a in range(n):
            for j, chip in enumerate(chips):
                copy(a, 1 + j, (*chip, c), me).wait_recv()
                fwd = copy(a, 4 + j, (*chip, c), sibling)
                fwd.start()
                passed.append(fwd)
        for a in range(n):
            copy(a, 0, sibling, me).wait_recv()
            for j, chip in enumerate(chips):
                copy(a, 4 + j, (*chip, 1 - c), me).wait_recv()
        for cp in first + passed:
            cp.wait_send()
        for cp in mine:
            cp.wait()

    return pl.pallas_call(
        body, name=name, in_specs=[_ANY] * n, out_specs=[_ANY] * n,
        out_shape=[jax.ShapeDtypeStruct((N_DEV,) + a.shape, a.dtype) for a in arrs],
        scratch_shapes=[pltpu.SemaphoreType.DMA((n, 7)), pltpu.SemaphoreType.DMA((n, 7)), pltpu.SemaphoreType.DMA((n,))],
    )(*arrs)


_HBM = pl.BlockSpec(memory_space=pltpu.HBM)
_SEM = pl.BlockSpec(memory_space=pltpu.SEMAPHORE)
_EFFECT = pltpu.SideEffectType.DATAFLOW_SIDE_EFFECTING
_FLIPS = [(dx, dy, dc) for dx in (0, 1) for dy in (0, 1) for dc in (0, 1)][1:]


def _peer(flip, x, y, c):
    return (1 - x if flip[0] else x, 1 - y if flip[1] else y, 1 - c if flip[2] else c)


def _send_start(srcs, blocked, *, after=None, name):
    n = len(srcs)
    lands = [jax.ShapeDtypeStruct(s.shape if blocked else (N_DEV,) + s.shape, s.dtype) for s in srcs]
    n_in = 2 * n + (after is not None)

    def body(*refs):
        src_refs, land_refs = refs[:n], refs[n:2 * n]
        send_sems, recv_sems = refs[n_in], refs[n_in + 1]
        token = refs[-1]
        x, y, c = _place()
        me = 4 * x + 2 * y + c
        for a in range(n):
            for k, flip in enumerate(_FLIPS):
                p = _peer(flip, x, y, c)
                src = src_refs[a].at[4 * p[0] + 2 * p[1] + p[2]] if blocked else src_refs[a]
                pltpu.make_async_remote_copy(
                    src_ref=src, dst_ref=land_refs[a].at[me], send_sem=send_sems.at[7 * a + k], recv_sem=recv_sems.at[7 * a + k],
                    device_id=p, device_id_type=MESH).start()
        token[...] = jnp.zeros_like(token)

    res = pl.pallas_call(
        body, name=name,
        out_shape=(pltpu.SemaphoreType.DMA((7 * n,)), pltpu.SemaphoreType.DMA((7 * n,)),
                   *[pltpu.HBM(s.shape, s.dtype) for s in srcs], *[pltpu.HBM(l.shape, l.dtype) for l in lands],
                   jax.ShapeDtypeStruct((SUBLANES, LANES), _F32)),
        in_specs=[_HBM] * (2 * n) + [_ANY] * (after is not None),
        out_specs=(_SEM, _SEM, *[_HBM] * (2 * n), pl.BlockSpec(memory_space=pltpu.VMEM)),
        input_output_aliases={i: 2 + i for i in range(2 * n)},
        compiler_params=pltpu.CompilerParams(has_side_effects=_EFFECT),
    )(*[pltpu.with_memory_space_constraint(s, pltpu.HBM) for s in srcs],
      *[pltpu.with_memory_space_constraint(lax.empty(l.shape, l.dtype), pltpu.HBM) for l in lands],
      *([after] if after is not None else []))
    return res[0], res[1], list(res[2:2 + n]), list(res[2 + n:2 + 2 * n]), res[-1]


def _send_wait(handles, after, blocked, *, name):
    send_sems, recv_sems, srcs, lands, _ = handles
    n = len(srcs)

    def body(*refs):
        src_refs, land_refs = refs[:n], refs[n:2 * n]
        send_sems, recv_sems = refs[2 * n], refs[2 * n + 1]
        x, y, c = _place()
        for a in range(n):
            for k, flip in enumerate(_FLIPS):
                p = _peer(flip, x, y, c)
                pid = 4 * p[0] + 2 * p[1] + p[2]
                cp = pltpu.make_async_remote_copy(
                    src_ref=src_refs[a].at[pid] if blocked else src_refs[a], dst_ref=land_refs[a].at[pid],
                    send_sem=send_sems.at[7 * a + k], recv_sem=recv_sems.at[7 * a + k], device_id=p, device_id_type=MESH)
                cp.wait_send()
                cp.wait_recv()

    res = pl.pallas_call(
        body, name=name,
        out_shape=tuple(pltpu.HBM(s.shape, s.dtype) for s in srcs + lands),
        in_specs=[_HBM] * (2 * n) + [_SEM, _SEM, _ANY], out_specs=tuple([_HBM] * (2 * n)),
        input_output_aliases={i: i for i in range(2 * n)},
        compiler_params=pltpu.CompilerParams(has_side_effects=_EFFECT),
    )(*srcs, *lands, send_sems, recv_sems, after)
    return list(res[:n]), list(res[n:])


def _adamw(parts, w, m, v, *, own=None, me=None, name):
    p, r, c = parts.shape
    tr = _pick(r, (256, 176, 128, 64, 32, 16, 8))
    if own is not None:
        tc = c if tr < r else _pick(c, (256, 128))
        return _adamw_own(parts, own, me, w, m, v, tr, tc, name=name)

    def body(p_ref, w_ref, m_ref, v_ref, g_ref, d_ref, nm_ref, nv_ref):
        g = p_ref[0].astype(_F32)
        for i in range(1, p):
            g = g + p_ref[i].astype(_F32)
        _adamw_math(g, w_ref, m_ref, v_ref, g_ref, d_ref, nm_ref, nv_ref)

    blk = pl.BlockSpec((tr, c), lambda i: (i, 0))
    return pl.pallas_call(
        body, name=name, grid=(r // tr,),
        in_specs=[pl.BlockSpec((p, tr, c), lambda i: (0, i, 0)), blk, blk, blk], out_specs=[blk] * 4,
        out_shape=[jax.ShapeDtypeStruct((r, c), _F32)] * 4,
        compiler_params=_params(("parallel",)),
    )(parts, w, m, v)


def _adamw_math(g, w_ref, m_ref, v_ref, g_ref, d_ref, nm_ref, nv_ref):
    wv = w_ref[...]
    mn = ADAM_B1 * m_ref[...] + (1.0 - ADAM_B1) * g
    vn = ADAM_B2 * v_ref[...] + (1.0 - ADAM_B2) * jnp.square(g)
    m_hat = mn / (1.0 - ADAM_B1 ** ADAM_STEP)
    v_hat = vn / (1.0 - ADAM_B2 ** ADAM_STEP)
    g_ref[...] = g
    d_ref[...] = -ADAM_LR * (m_hat / (jnp.sqrt(v_hat) + ADAM_EPS) + ADAM_WD * wv)
    nm_ref[...] = mn
    nv_ref[...] = vn


def _adamw_own(parts, own, me, w, m, v, tr, tc, *, name):
    p, r, c = parts.shape

    def body(me_ref, p_ref, own_ref, w_ref, m_ref, v_ref, g_ref, d_ref, nm_ref, nv_ref):
        mine = own_ref[...].astype(_F32)
        g = jnp.where(me_ref[0] == 0, mine, p_ref[0].astype(_F32))
        for i in range(1, p):
            g = g + jnp.where(me_ref[0] == i, mine, p_ref[i].astype(_F32))
        _adamw_math(g, w_ref, m_ref, v_ref, g_ref, d_ref, nm_ref, nv_ref)

    blk = pl.BlockSpec((tr, tc), lambda i, j, me_ref: (i, j))
    grid_spec = pltpu.PrefetchScalarGridSpec(
        num_scalar_prefetch=1, grid=(r // tr, c // tc),
        in_specs=[pl.BlockSpec((p, tr, tc), lambda i, j, me_ref: (0, i, j)),
                  pl.BlockSpec((None, tr, tc), lambda i, j, me_ref: (me_ref[0], i, j)), blk, blk, blk],
        out_specs=[blk] * 4)
    return pl.pallas_call(
        body, name=name, grid_spec=grid_spec, out_shape=[jax.ShapeDtypeStruct((r, c), _F32)] * 4,
        compiler_params=_params(("parallel", "parallel")),
    )(me.reshape(1).astype(jnp.int32), parts, own, w, m, v)


def _sum_parts(parts, *, name):
    p, r, c = parts.shape

    def body(p_ref, o_ref):
        g = p_ref[0].astype(_F32)
        for i in range(1, p):
            g = g + p_ref[i].astype(_F32)
        o_ref[...] = g

    return pl.pallas_call(body, name=name, out_shape=jax.ShapeDtypeStruct((r, c), _F32))(parts)


def _pack(vals, rows):
    flat = jnp.concatenate([v.reshape(-1) for v in vals])
    return jnp.pad(flat, (0, rows * LANES - flat.shape[0])).reshape(rows, LANES)


def _unpack(packed, shapes):
    flat = packed.reshape(-1)
    out, pos = [], 0
    for shp in shapes:
        size = math.prod(shp)
        out.append(flat[pos:pos + size].reshape(shp))
        pos += size
    return out


def _pack_rows(shapes):
    total = sum(math.prod(s) for s in shapes)
    return -(-total // (LANES * SUBLANES)) * SUBLANES


def kernel(x, mem, norm_mix_g, w_in, ssd_conv_w, ssd_conv_b, ssd_dt_bias, ssd_A_log, ssd_D, ssd_norm_g, cf_conv_w, cf_conv_b, cf_ln_g, cf_ln_b, w_out, norm_xattn_g, norm_mem_g, w_q, w_kv, w_o, norm_ffn_g, w_gate, w_up, w_down, norm_final_g, loss_target, m_norm_mix_g, m_w_in, m_ssd_conv_w, m_ssd_conv_b, m_ssd_dt_bias, m_ssd_A_log, m_ssd_D, m_ssd_norm_g, m_cf_conv_w, m_cf_conv_b, m_cf_ln_g, m_cf_ln_b, m_w_out, m_norm_xattn_g, m_norm_mem_g, m_w_q, m_w_kv, m_w_o, m_norm_ffn_g, m_w_gate, m_w_up, m_w_down, m_norm_final_g, v_norm_mix_g, v_w_in, v_ssd_conv_w, v_ssd_conv_b, v_ssd_dt_bias, v_ssd_A_log, v_ssd_D, v_ssd_norm_g, v_cf_conv_w, v_cf_conv_b, v_cf_ln_g, v_cf_ln_b, v_w_out, v_norm_xattn_g, v_norm_mem_g, v_w_q, v_w_kv, v_w_o, v_norm_ffn_g, v_w_gate, v_w_up, v_w_down, v_norm_final_g):
    args = dict(locals())
    wts = {n: args[n] for n in WEIGHT_NAMES}
    mom = {n: args["m_" + n] for n in WEIGHT_NAMES}
    var = {n: args["v_" + n] for n in WEIGHT_NAMES}
    me = 4 * lax.axis_index("x") + 2 * lax.axis_index("y") + lax.axis_index("c")

    groups = {'in': ['w_in'], 'conv': ['ssd_conv_w', 'cf_conv_w'], 'mid': ['w_out', 'w_q', 'w_kv', 'w_o'],
              'ffn': ['w_gate', 'w_up', 'w_down']}
    def shard(n, a):
        return jnp.transpose(a[0], (1, 0)) if n in TRANSPOSED else a[0]

    gathers, started = {}, None
    for grp, names in groups.items():
        shards = [wts[n][0] if grp == 'conv' else shard(n, wts[n]).astype(_MXU) for n in names]
        gathers[grp] = _send_start(shards, False, after=started, name="gather_%s_start" % grp)
        started = gathers[grp][4]

    def fetch(grp, after):
        srcs, lands = _send_wait(gathers[grp], started if after is None else after, False, name="gather_%s_wait" % grp)
        out = {}
        for n, own, gth in zip(groups[grp], srcs, lands):
            gth = lax.dynamic_update_slice_in_dim(gth, own[None], me, axis=0)
            if n == 'w_kv' or grp == 'conv':
                out[n] = jnp.transpose(gth, (1, 0, 2)).reshape(gth.shape[1], N_DEV * gth.shape[2])
            else:
                out[n] = gth.reshape(N_DEV * gth.shape[1], gth.shape[2])
        return out

    exchanges = []

    def emit(grads):
        blocks = []
        for n, gw in grads.items():
            if n == 'w_kv':
                gw = jnp.transpose(gw.reshape(gw.shape[0], N_DEV, gw.shape[1] // N_DEV), (1, 0, 2))
            else:
                gw = gw.reshape(N_DEV, gw.shape[0] // N_DEV, gw.shape[1])
            blocks.append(gw.astype(jnp.bfloat16))
        first = next(iter(grads))
        exchanges.append((list(grads), _send_start(blocks, True, name="exchange_%s_start" % first), first))
        return exchanges[-1][1][4]

    full = {n: wts[n] for n in WEIGHT_NAMES if n not in BIG and n not in groups['conv']}
    full['norm_mix_g'] = _tie(norm_mix_g, started)

    loss_blk, grad_x, g = _local_step(x[0], mem[0], loss_target[0], full, fetch, emit)

    out_g, out_d, out_m, out_v = {}, {}, {}, {}
    for names, handles, first in exchanges:
        srcs, lands = _send_wait(handles, grad_x, True, name="exchange_%s_wait" % first)
        for n, own, parts in zip(names, srcs, lands):
            res = _adamw(parts, shard(n, wts[n]), shard(n, mom[n]), shard(n, var[n]), own=own, me=me, name="adamw_" + n)
            out_g[n], out_d[n], out_m[n], out_v[n] = [(jnp.transpose(r, (1, 0)) if n in TRANSPOSED else r)[None] for r in res]

    small = [n for n in WEIGHT_NAMES if n not in BIG]
    g['loss'] = loss_blk[0:1, 0:1]
    small_shapes = [g[n].shape for n in small + ['loss']]
    rows = _pack_rows(small_shapes)
    (small_parts,) = _all_gather([_pack([g[n] for n in small + ['loss']], rows)], name="gather_small_grads")
    small_sum = dict(zip(small + ['loss'], _unpack(_sum_parts(small_parts, name="sum_small_grads"), small_shapes)))
    loss = small_sum['loss'][0, 0]
    small_sum['ssd_conv_w'] = lax.dynamic_slice_in_dim(small_sum['ssd_conv_w'], me * (XBC_WIDTH // N_DEV), XBC_WIDTH // N_DEV, axis=1)[None]
    small_sum['cf_conv_w'] = lax.dynamic_slice_in_dim(small_sum['cf_conv_w'], me * (CF_WIDTH // N_DEV), CF_WIDTH // N_DEV, axis=1)[None]
    shard_shapes = [wts[n].shape for n in small]
    rows2 = _pack_rows(shard_shapes)
    res = _adamw(_pack([small_sum[n] for n in small], rows2)[None], _pack([wts[n] for n in small], rows2),
                 _pack([mom[n] for n in small], rows2), _pack([var[n] for n in small], rows2), name="adamw_small")
    for dst, packed in zip((out_g, out_d, out_m, out_v), res):
        dst.update(zip(small, _unpack(packed, shard_shapes)))

    return (loss, grad_x[None], *[out_g[n] for n in WEIGHT_NAMES], *[out_d[n] for n in WEIGHT_NAMES],
            *[out_m[n] for n in WEIGHT_NAMES], *[out_v[n] for n in WEIGHT_NAMES])
```

```python
import functools
import math

import jax
import jax.numpy as jnp
from jax import lax
from jax.experimental import pallas as pl
from jax.experimental.pallas import tpu as pltpu

_F32 = jnp.float32
_MXU = jnp.bfloat16
_PREC = None
_VMEM_LIMIT = 56 * 1024 * 1024

D_MODEL = 1024
HEAD_DIM = 64
SSD_HEADS = 16
SSD_WIDTH = 1024
SSD_STATE = 128
SSD_CONV = 4
CHUNK = 128
XBC_WIDTH = 1536
CF_WIDTH = 1024
CF_CONV = 31
X_HEADS = 4
X_HEAD_DIM = 256
D_FF = 2816
EPS = 1e-6
N_DEV = 8
LANES = 128
SUBLANES = 8

ADAM_LR = 0.001
ADAM_B1 = 0.9
ADAM_B2 = 0.999
ADAM_EPS = 1e-08
ADAM_WD = 0.01
ADAM_STEP = 10

MESH = pl.DeviceIdType.MESH
WEIGHT_NAMES = ['norm_mix_g', 'w_in', 'ssd_conv_w', 'ssd_conv_b', 'ssd_dt_bias', 'ssd_A_log', 'ssd_D', 'ssd_norm_g',
                'cf_conv_w', 'cf_conv_b', 'cf_ln_g', 'cf_ln_b', 'w_out', 'norm_xattn_g', 'norm_mem_g', 'w_q', 'w_kv',
                'w_o', 'norm_ffn_g', 'w_gate', 'w_up', 'w_down', 'norm_final_g']
BIG = ['w_in', 'w_out', 'w_q', 'w_kv', 'w_o', 'w_gate', 'w_up', 'w_down']
TRANSPOSED = ('w_in', 'w_gate', 'w_up')


def _params(sem=None):
    return pltpu.CompilerParams(dimension_semantics=sem, vmem_limit_bytes=_VMEM_LIMIT)


def _pick(n, cands):
    for c in cands:
        if n % c == 0:
            return c
    return n


def _mm(a, b, *, ta=False, tb=False, add=None, out_dtype=_F32, name):
    (kdim, m) = a.shape if ta else a.shape[::-1]
    (n, k2) = b.shape if tb else b.shape[::-1]
    assert kdim == k2, (a.shape, b.shape, ta, tb)
    if ta:
        tm = m if m <= 1024 else _pick(m, (1408, 1024, 512, 256, 128))
        tn = n if n <= 1536 else _pick(n, (1408, 1024, 512, 256, 128))
        tk = _pick(kdim, (1024, 512, 256, 128))
    else:
        tm = _pick(m, (512, 256, 128))
        tn = n if n <= 2816 else _pick(n, (1408, 1024, 512, 256, 128))
        tk = kdim if kdim <= 2816 else _pick(kdim, (1408, 1024, 512, 256, 128))
    nk = kdim // tk
    dn = (((0 if ta else 1,), (1 if tb else 0,)), ((), ()))

    def body(*refs):
        a_ref, b_ref = refs[0], refs[1]
        add_ref = refs[2] if add is not None else None
        o_ref = refs[3 if add is not None else 2]
        acc_ref = refs[-1]
        k = pl.program_id(2)
        prod = lax.dot_general(a_ref[...].astype(_MXU), b_ref[...].astype(_MXU), dn,
                               preferred_element_type=_F32, precision=_PREC)

        def finish(r):
            if add_ref is not None:
                r = r + add_ref[...].astype(_F32)
            o_ref[...] = r.astype(o_ref.dtype)

        if nk == 1:
            finish(prod)
            return

        @pl.when(k == 0)
        def _():
            acc_ref[...] = prod

        @pl.when(jnp.logical_and(k > 0, k < nk - 1))
        def _():
            acc_ref[...] += prod

        @pl.when(k == nk - 1)
        def _():
            finish(acc_ref[...] + prod)

    a_spec = pl.BlockSpec((tk, tm), lambda i, j, k: (k, i)) if ta else pl.BlockSpec((tm, tk), lambda i, j, k: (i, k))
    b_spec = pl.BlockSpec((tn, tk), lambda i, j, k: (j, k)) if tb else pl.BlockSpec((tk, tn), lambda i, j, k: (k, j))
    o_spec = pl.BlockSpec((tm, tn), lambda i, j, k: (i, j))
    ins, specs = [a, b], [a_spec, b_spec]
    if add is not None:
        ins.append(add)
        specs.append(o_spec)
    return pl.pallas_call(
        body, name=name, grid=(m // tm, n // tn, nk), in_specs=specs, out_specs=o_spec,
        out_shape=jax.ShapeDtypeStruct((m, n), out_dtype),
        scratch_shapes=[pltpu.VMEM((tm, tn), _F32)] if nk > 1 else [],
        compiler_params=_params(("parallel", "parallel", "arbitrary")),
    )(*ins)


def _resident(shape):
    return pl.BlockSpec(shape, lambda i: (0,) * len(shape), pipeline_mode=pl.Buffered(1))


def _mm_fan_out(a, bs, *, tb, out_dtypes, epilogue=None, extra_outs=(), tm=512, name):
    m, kdim = a.shape
    tm = min(tm, m)
    ns = [b.shape[0] if tb else b.shape[1] for b in bs]
    nb = len(bs)
    kind = "nt" if tb else "nn"

    def body(*refs):
        a_ref, b_refs, o_refs = refs[0], refs[1:1 + nb], refs[1 + nb:]
        av = a_ref[...].astype(_MXU)
        prods = [lax.dot_general(av, b[...].astype(_MXU), _DN[kind], preferred_element_type=_F32, precision=_PREC)
                 for b in b_refs]
        for o_ref, p in zip(o_refs[:nb], prods):
            o_ref[...] = p.astype(o_ref.dtype)
        if epilogue is not None:
            for o_ref, v in zip(o_refs[nb:], _tup(epilogue(*prods))):
                o_ref[...] = v.astype(o_ref.dtype)

    widths = ns + [w for w, _ in extra_outs]
    dtypes = list(out_dtypes) + [dt for _, dt in extra_outs]
    return pl.pallas_call(
        body, name=name, grid=(m // tm,),
        in_specs=[pl.BlockSpec((tm, kdim), lambda i: (i, 0))] + [_resident(b.shape) for b in bs],
        out_specs=[pl.BlockSpec((tm, w), lambda i: (i, 0)) for w in widths],
        out_shape=[jax.ShapeDtypeStruct((m, w), dt) for w, dt in zip(widths, dtypes)],
        compiler_params=_params(("parallel",)),
    )(a, *bs)


def _mm_fan_in(pairs, *, add=None, out_dtype=_F32, prologue=None, pro_ins=(), pro_out_dtypes=(), epilogue=None,
               tm=512, name):
    bs = [b for _, b in pairs]
    nb = len(bs)
    n = bs[0].shape[1]
    rows_in = list(pro_ins) if prologue is not None else [a for a, _ in pairs]
    m = rows_in[0].shape[0]
    tm = min(tm, m)
    n_r = len(rows_in)

    def body(*refs):
        r_refs, b_refs = refs[:n_r], refs[n_r:n_r + nb]
        pos = n_r + nb
        add_ref = refs[pos] if add is not None else None
        pos += add is not None
        epi_ref = refs[pos] if epilogue is not None else None
        pos += epilogue is not None
        o_ref, po_refs = refs[pos], refs[pos + 1:]
        if prologue is not None:
            a_vals = _tup(prologue(*[r[...].astype(_F32) for r in r_refs]))
            for po, v in zip(po_refs, a_vals):
                po[...] = v.astype(po.dtype)
        else:
            a_vals = [r[...] for r in r_refs]
        acc = None
        for av, b in zip(a_vals, b_refs):
            p = lax.dot_general(av.astype(_MXU), b[...].astype(_MXU), _DN["nn"], preferred_element_type=_F32,
                                precision=_PREC)
            acc = p if acc is None else acc + p
        if add_ref is not None:
            acc = acc + add_ref[...].astype(_F32)
        o_ref[...] = acc.astype(o_ref.dtype)
        if epilogue is not None:
            po_refs[-1][...] = epilogue[0](acc, epi_ref[...]).astype(po_refs[-1].dtype)

    row = lambda w: pl.BlockSpec((tm, w), lambda i: (i, 0))
    ins = rows_in + bs + ([add] if add is not None else []) + ([epilogue[1]] if epilogue is not None else [])
    in_specs = ([row(r.shape[1]) for r in rows_in] + [_resident(b.shape) for b in bs]
                + ([row(n)] if add is not None else []) + ([_resident(epilogue[1].shape)] if epilogue is not None else []))
    extra = [(b.shape[0], dt) for b, dt in zip(bs, pro_out_dtypes)] if prologue is not None else []
    if epilogue is not None:
        extra.append((n, epilogue[2]))
    res = pl.pallas_call(
        body, name=name, grid=(m // tm,), in_specs=in_specs,
        out_specs=[row(n)] + [row(w) for w, _ in extra],
        out_shape=[jax.ShapeDtypeStruct((m, n), out_dtype)] + [jax.ShapeDtypeStruct((m, w), dt) for w, dt in extra],
        compiler_params=_params(("parallel",)),
    )(*ins)
    return res if extra else res[0]


def _row_spec(r, ts):
    if isinstance(r, tuple):
        arr, width, cblk = r
        return arr, pl.BlockSpec((ts, width), lambda i, cblk=cblk: (i, cblk))
    return r, pl.BlockSpec((ts, r.shape[1]), lambda i: (i, 0))


def _tup(v):
    return tuple(v) if isinstance(v, (tuple, list)) else (v,)


def _row_fwd(f, rows, params, outs, *, name, ts=256):
    arrs, specs = zip(*[_row_spec(r, ts) for r in rows])
    s = arrs[0].shape[0]
    ts = min(ts, s)
    n_r, n_p = len(rows), len(params)

    def body(*refs):
        rv = [r[...].astype(_F32) for r in refs[:n_r]]
        pv = [p[...] for p in refs[n_r:n_r + n_p]]
        res = _tup(f(*rv, *pv))
        for o_ref, v in zip(refs[n_r + n_p:], res):
            o_ref[...] = v.astype(o_ref.dtype)

    res = pl.pallas_call(
        body, name=name, grid=(s // ts,),
        in_specs=list(specs) + [pl.BlockSpec(p.shape, lambda i: (0, 0)) for p in params],
        out_specs=[pl.BlockSpec((ts, w), lambda i: (i, 0)) for w, _ in outs],
        out_shape=[jax.ShapeDtypeStruct((s, w), dt) for w, dt in outs],
        compiler_params=_params(("parallel",)),
    )(*arrs, *params)
    return res[0] if len(outs) == 1 else res


def _row_bwd(f, rows, params, cts, *, need=None, adds=None, row_dtypes=None, name, ts=256):
    arrs, specs = zip(*[_row_spec(r, ts) for r in rows])
    s = arrs[0].shape[0]
    ts = min(ts, s)
    n_r, n_p, n_c = len(rows), len(params), len(cts)
    need = [True] * n_r if need is None else need
    adds = {} if adds is None else adds
    add_keys = sorted(adds)
    row_dtypes = [_F32] * n_r if row_dtypes is None else row_dtypes
    needed = [j for j in range(n_r) if need[j]]
    widths = [specs[j].block_shape[1] for j in range(n_r)]

    def body(*refs):
        pos = 0
        r_refs = refs[pos:pos + n_r]; pos += n_r
        p_refs = refs[pos:pos + n_p]; pos += n_p
        c_refs = refs[pos:pos + n_c]; pos += n_c
        a_refs = refs[pos:pos + len(add_keys)]; pos += len(add_keys)
        dr_refs = refs[pos:pos + len(needed)]; pos += len(needed)
        dp_refs = refs[pos:pos + n_p]
        rv = [r[...].astype(_F32) for r in r_refs]
        pv = [p[...] for p in p_refs]
        _, vjp = jax.vjp(lambda *a: _tup(f(*a)), *rv, *pv)
        g = vjp(tuple(c[...].astype(_F32) for c in c_refs))
        for o_ref, j in zip(dr_refs, needed):
            v = g[j]
            if j in adds:
                v = v + a_refs[add_keys.index(j)][...].astype(_F32)
            o_ref[...] = v.astype(o_ref.dtype)
        if n_p:
            @pl.when(pl.program_id(0) == 0)
            def _():
                for dp in dp_refs:
                    dp[...] = jnp.zeros_like(dp)
            for dp, v in zip(dp_refs, g[n_r:]):
                dp[...] += v

    ct_specs = [pl.BlockSpec((ts, c.shape[1]), lambda i: (i, 0)) for c in cts]
    add_specs = [pl.BlockSpec((ts, adds[j].shape[1]), lambda i: (i, 0)) for j in add_keys]
    res = pl.pallas_call(
        body, name=name, grid=(s // ts,),
        in_specs=list(specs) + [pl.BlockSpec(p.shape, lambda i: (0, 0)) for p in params] + ct_specs + add_specs,
        out_specs=[pl.BlockSpec((ts, widths[j]), lambda i: (i, 0)) for j in needed]
        + [pl.BlockSpec(p.shape, lambda i: (0, 0)) for p in params],
        out_shape=[jax.ShapeDtypeStruct((s, widths[j]), row_dtypes[j]) for j in needed]
        + [jax.ShapeDtypeStruct(p.shape, _F32) for p in params],
        compiler_params=_params(("arbitrary",)),
    )(*arrs, *params, *cts, *[adds[j] for j in add_keys])
    return list(res[:len(needed)]), list(res[len(needed):])


_DN = {"nn": (((1,), (0,)), ((), ())), "nt": (((1,), (1,)), ((), ())), "tn": (((0,), (0,)), ((), ()))}


def _make_dot(passes):
    def raw(a, b, kind):
        dn = _DN[kind]
        if passes == 1 or _MXU == _F32:
            return lax.dot_general(a.astype(_MXU), b.astype(_MXU), dn, preferred_element_type=_F32, precision=_PREC)
        a_hi, b_hi = a.astype(_MXU), b.astype(_MXU)
        a_lo = (a - a_hi.astype(_F32)).astype(_MXU)
        b_lo = (b - b_hi.astype(_F32)).astype(_MXU)
        out = lax.dot_general(a_hi, b_hi, dn, preferred_element_type=_F32)
        out = out + lax.dot_general(a_lo, b_hi, dn, preferred_element_type=_F32)
        return out + lax.dot_general(a_hi, b_lo, dn, preferred_element_type=_F32)

    @functools.partial(jax.custom_vjp, nondiff_argnums=(2,))
    def dot(a, b, kind):
        return raw(a, b, kind)

    def fwd(a, b, kind):
        return raw(a, b, kind), (a, b)

    def bwd(kind, res, ct):
        a, b = res
        if kind == "nn":
            return raw(ct, b, "nt"), raw(a, ct, "tn")
        if kind == "nt":
            return raw(ct, b, "nn"), raw(ct, a, "tn")
        return raw(b, ct, "nt"), raw(a, ct, "nn")

    dot.defvjp(fwd, bwd)
    return dot


_dot1 = _make_dot(1)
_dot3 = _make_dot(3)


def _sig(v):
    return 1.0 / (1.0 + jnp.exp(-v))


def _silu(v):
    return v * _sig(v)


def _f_rms(x, g):
    return x * lax.rsqrt(jnp.mean(x * x, axis=-1, keepdims=True) + EPS) * g


def _f_gate(y, xs, z, dexp, g):
    v = (y + dexp * xs) * _silu(z)
    half = SSD_WIDTH // 2
    parts = []
    for grp in range(2):
        vg = v[:, grp * half:(grp + 1) * half]
        parts.append(vg * lax.rsqrt(jnp.mean(vg * vg, axis=-1, keepdims=True) + EPS) * g[:, grp * half:(grp + 1) * half])
    return jnp.concatenate(parts, axis=1)


def _f_ln(u, g, b):
    mu = jnp.mean(u, axis=-1, keepdims=True)
    var = jnp.mean(jnp.square(u - mu), axis=-1, keepdims=True)
    return _silu((u - mu) * lax.rsqrt(var + EPS) * g + b)


def _f_glu(a, g):
    return a * _sig(g)


def _f_swiglu(gate, up):
    return _silu(gate) * up


def _f_att(q, k, v):
    outs = []
    for h in range(X_HEADS):
        sl = slice(h * X_HEAD_DIM, (h + 1) * X_HEAD_DIM)
        s = _dot1(q[:, sl], k[:, sl], "nt") * (X_HEAD_DIM ** -0.5)
        s = s - lax.stop_gradient(jnp.max(s, axis=-1, keepdims=True))
        p = jnp.exp(s)
        p = p / jnp.sum(p, axis=-1, keepdims=True)
        outs.append(_dot1(p, v[:, sl], "nn"))
    return jnp.concatenate(outs, axis=1)


def _loss_bwd(x3, target, g, *, name, ts=256):
    s, d = x3.shape

    def f(x, t, gv):
        return 0.5 * jnp.sum(jnp.mean(jnp.square(_f_rms(x, gv) - t), axis=-1))

    def body(x_ref, t_ref, g_ref, dx_ref, dg_ref, l_ref):
        @pl.when(pl.program_id(0) == 0)
        def _():
            dg_ref[...] = jnp.zeros_like(dg_ref)
            l_ref[...] = jnp.zeros_like(l_ref)

        lv, (dx, dg) = jax.value_and_grad(f, argnums=(0, 2))(x_ref[...], t_ref[...], g_ref[...])
        dx_ref[...] = dx
        dg_ref[...] += dg
        l_ref[...] += lv

    row = pl.BlockSpec((ts, d), lambda i: (i, 0))
    return pl.pallas_call(
        body, name=name, grid=(s // ts,),
        in_specs=[row, row, pl.BlockSpec((1, d), lambda i: (0, 0))],
        out_specs=[row, pl.BlockSpec((1, d), lambda i: (0, 0)), pl.BlockSpec((SUBLANES, LANES), lambda i: (0, 0))],
        out_shape=[jax.ShapeDtypeStruct((s, d), _F32), jax.ShapeDtypeStruct((1, d), _F32),
                   jax.ShapeDtypeStruct((SUBLANES, LANES), _F32)],
        compiler_params=_params(("arbitrary",)),
    )(x3, target, g)


_CONV_PAD = 32
_CONV_ROWS = 128
_CONV_CB = 128


def _conv_taps(k_taps):
    groups = {}
    for k in range(k_taps):
        j = k_taps - 1 - k
        groups.setdefault(j % SUBLANES, []).append((k, j))
    return groups


def _conv_window(win, wv, groups, init):
    pad, rows = _CONV_PAD, _CONV_ROWS
    acc = init
    for rot, taps in groups.items():
        rolled = win if rot == 0 else pltpu.roll(win, rot, 0)
        for k, j in taps:
            off = pad - (j - rot)
            acc = acc + rolled[off:off + rows, :] * wv[k:k + 1, :]
    return acc


def _conv_fill(x_refs, xp_ref, s, glu):
    pad, cb = _CONV_PAD, _CONV_CB
    step = _pick(s, (512, 256, _CONV_ROWS))
    xp_ref[0:pad, :] = jnp.zeros((pad, cb), _F32)

    def fill(r, carry):
        base = pl.multiple_of(r * step, step)
        v = x_refs[0][pl.ds(base, step), :].astype(_F32)
        if glu:
            v = v * _sig(x_refs[1][pl.ds(base, step), :].astype(_F32))
        xp_ref[pl.ds(pad + base, step), :] = v
        return carry

    lax.fori_loop(0, s // step, fill, 0)


def _conv_fwd(xs, w, b, k_taps, *, glu=False, act=False, name):
    s, c = xs[0].shape
    kp = w.shape[0]
    pad, rows, cb = _CONV_PAD, _CONV_ROWS, _CONV_CB
    groups = _conv_taps(k_taps)
    n_in = len(xs)

    def body(*refs):
        x_refs = refs[:n_in]
        w_ref, b_ref, o_ref, xp_ref = refs[n_in:]
        _conv_fill(x_refs, xp_ref, s, glu)
        wv = w_ref[...]
        bias = jnp.broadcast_to(b_ref[...], (rows, cb))

        def chunk(r, carry):
            base = pl.multiple_of(r * rows, rows)
            acc = _conv_window(xp_ref[pl.ds(base, rows + pad), :], wv, groups, bias)
            o_ref[pl.ds(base, rows), :] = _silu(acc) if act else acc
            return carry

        lax.fori_loop(0, s // rows, chunk, 0)

    col = pl.BlockSpec((s, cb), lambda i: (0, i))
    return pl.pallas_call(
        body, name=name, grid=(c // cb,),
        in_specs=[col] * n_in + [pl.BlockSpec((kp, cb), lambda i: (0, i)), pl.BlockSpec((1, cb), lambda i: (0, i))],
        out_specs=col, out_shape=jax.ShapeDtypeStruct((s, c), _F32),
        scratch_shapes=[pltpu.VMEM((s + pad, cb), _F32)],
        compiler_params=_params(("parallel",)),
    )(*xs, w, b)


def _conv_bwd(xs, w, b, dy, k_taps, *, glu=False, act=False, name):
    s, c = xs[0].shape
    kp = w.shape[0]
    pad, rows, cb = _CONV_PAD, _CONV_ROWS, _CONV_CB
    groups = _conv_taps(k_taps)
    win_rows = rows + pad
    n_in = len(xs)

    def fold(v):
        acc = v[0:SUBLANES, :]
        for i in range(1, rows // SUBLANES):
            acc = acc + v[i * SUBLANES:(i + 1) * SUBLANES, :]
        return acc

    def body(*refs):
        x_refs = refs[:n_in]
        w_ref, b_ref, dy_ref = refs[n_in:n_in + 3]
        dx_refs = refs[n_in + 3:2 * n_in + 3]
        dw_ref, db_ref, xp_ref, dyp_ref, acc_ref, dbacc_ref = refs[2 * n_in + 3:]
        _conv_fill(x_refs, xp_ref, s, glu)
        dyp_ref[s:s + pad, :] = jnp.zeros((pad, cb), _F32)
        acc_ref[...] = jnp.zeros_like(acc_ref)
        dbacc_ref[...] = jnp.zeros_like(dbacc_ref)
        wv = w_ref[...]
        bias = jnp.broadcast_to(b_ref[...], (rows, cb))

        def through_act(r, carry):
            base = pl.multiple_of(r * rows, rows)
            d = dy_ref[pl.ds(base, rows), :]
            if act:
                pre = _conv_window(xp_ref[pl.ds(base, win_rows), :], wv, groups, bias)
                sg = _sig(pre)
                d = d * (sg * (1.0 + pre * (1.0 - sg)))
            dyp_ref[pl.ds(base, rows), :] = d
            return carry

        lax.fori_loop(0, s // rows, through_act, 0)

        def chunk(r, carry):
            base = pl.multiple_of(r * rows, rows)
            xwin = xp_ref[pl.ds(base, win_rows), :]
            dwin = dyp_ref[pl.ds(base, win_rows), :]
            dyc = dwin[0:rows, :]
            dxacc = jnp.zeros((rows, cb), _F32)
            for rot, taps in groups.items():
                xr = xwin if rot == 0 else pltpu.roll(xwin, rot, 0)
                dr = dwin if rot == 0 else pltpu.roll(dwin, win_rows - rot, 0)
                for k, j in taps:
                    a8 = j - rot
                    dxacc = dxacc + dr[a8:a8 + rows, :] * wv[k:k + 1, :]
                    prod = dyc * xr[pad - a8:pad - a8 + rows, :]
                    acc_ref[k * SUBLANES:(k + 1) * SUBLANES, :] += fold(prod)
            dbacc_ref[...] += fold(dyc)
            if glu:
                av = x_refs[0][pl.ds(base, rows), :].astype(_F32)
                sg = _sig(x_refs[1][pl.ds(base, rows), :].astype(_F32))
                dx_refs[0][pl.ds(base, rows), :] = (dxacc * sg).astype(dx_refs[0].dtype)
                dx_refs[1][pl.ds(base, rows), :] = (dxacc * av * sg * (1.0 - sg)).astype(dx_refs[1].dtype)
            else:
                dx_refs[0][pl.ds(base, rows), :] = dxacc
            return carry

        lax.fori_loop(0, s // rows, chunk, 0)
        dw_ref[...] = jnp.zeros_like(dw_ref)
        for k in range(k_taps):
            dw_ref[k:k + 1, :] = jnp.sum(acc_ref[k * SUBLANES:(k + 1) * SUBLANES, :], axis=0, keepdims=True)
        db_ref[...] = jnp.sum(dbacc_ref[...], axis=0, keepdims=True)

    col = pl.BlockSpec((s, cb), lambda i: (0, i))
    wspec = pl.BlockSpec((kp, cb), lambda i: (0, i))
    bspec = pl.BlockSpec((1, cb), lambda i: (0, i))
    dx_dtype = xs[0].dtype if glu else _F32
    res = pl.pallas_call(
        body, name=name, grid=(c // cb,),
        in_specs=[col] * n_in + [wspec, bspec, col], out_specs=[col] * n_in + [wspec, bspec],
        out_shape=[jax.ShapeDtypeStruct((s, c), dx_dtype)] * n_in
        + [jax.ShapeDtypeStruct((kp, c), _F32), jax.ShapeDtypeStruct((1, c), _F32)],
        scratch_shapes=[pltpu.VMEM((s + pad, cb), _F32), pltpu.VMEM((s + pad, cb), _F32),
                        pltpu.VMEM((kp * SUBLANES, cb), _F32), pltpu.VMEM((SUBLANES, cb), _F32)],
        compiler_params=_params(("parallel",)),
    )(*xs, w, b, dy)
    return list(res[:n_in]), res[n_in], res[n_in + 1]


def _tri_sum(v, lower):
    l = v.shape[0]
    r, c = lax.broadcasted_iota(jnp.int32, (l, l), 0), lax.broadcasted_iota(jnp.int32, (l, l), 1)
    tri = ((r >= c) if lower else (r <= c)).astype(jnp.bfloat16)
    hi = v.astype(jnp.bfloat16)
    r1 = v - hi.astype(_F32)
    mid = r1.astype(jnp.bfloat16)
    lo = (r1 - mid.astype(_F32)).astype(jnp.bfloat16)
    out = jnp.zeros_like(v)
    for part in (hi, mid, lo):
        out = out + lax.dot_general(tri, part, _DN["nn"], preferred_element_type=_F32)
    return out


@jax.custom_vjp
def _cumsum_rows(v):
    return _tri_sum(v, True)


_cumsum_rows.defvjp(lambda v: (_tri_sum(v, True), None), lambda _, ct: (_tri_sum(ct, False),))


def _ssd_chunk(xbc, dtraw, prev, bias, alog):
    l = xbc.shape[0]
    xs = xbc[:, :SSD_WIDTH]
    bm = xbc[:, SSD_WIDTH:SSD_WIDTH + 2 * SSD_STATE]
    cm = xbc[:, SSD_WIDTH + 2 * SSD_STATE:]
    v = dtraw + bias
    dt = jnp.maximum(v, 0.0) + jnp.log1p(jnp.exp(-jnp.abs(v)))
    a_neg = -jnp.exp(alog)
    acs = _cumsum_rows(dt * a_neg)
    acs_t = acs.T
    dt_t = dt.T
    total = acs[l - 1:l, :]
    row = lax.broadcasted_iota(jnp.int32, (l, l), 0)
    colv = lax.broadcasted_iota(jnp.int32, (l, l), 1)
    causal = row >= colv
    lane_lo = lax.broadcasted_iota(jnp.int32, (l, LANES), 1) < HEAD_DIM
    row_lo = lax.broadcasted_iota(jnp.int32, (LANES, SSD_STATE), 0) < HEAD_DIM

    def pair_lanes(m, h0):
        return jnp.where(lane_lo, m[:, h0:h0 + 1], m[:, h0 + 1:h0 + 2])

    ys, news = [], []
    cb = {}
    for j in range(SSD_HEADS // 2):
        h0 = 2 * j
        grp = h0 // (SSD_HEADS // 2)
        bg = bm[:, grp * SSD_STATE:(grp + 1) * SSD_STATE]
        cg = cm[:, grp * SSD_STATE:(grp + 1) * SSD_STATE]
        if grp not in cb:
            cb[grp] = _dot1(cg, bg, "nt")
        xp = xs[:, j * LANES:(j + 1) * LANES]
        y = jnp.zeros((l, LANES), _F32)
        for hh, mask in ((h0, lane_lo), (h0 + 1, jnp.logical_not(lane_lo))):
            seg = acs[:, hh:hh + 1] - acs_t[hh:hh + 1, :]
            dec = jnp.exp(jnp.where(causal, seg, -jnp.inf))
            sc = cb[grp] * dec * dt_t[hh:hh + 1, :]
            y = y + _dot1(sc, jnp.where(mask, xp, 0.0), "nn")
        acs_p = pair_lanes(acs, h0)
        prev_p = prev[j * LANES:(j + 1) * LANES, :]
        y = y + _dot1(cg, prev_p, "nt") * jnp.exp(acs_p)
        wgt = jnp.exp(pair_lanes(jnp.broadcast_to(total, (l, LANES)), h0) - acs_p) * pair_lanes(dt, h0)
        st = _dot1(xp * wgt, bg, "tn")
        cdec = jnp.exp(jnp.where(row_lo, total[:, h0:h0 + 1], total[:, h0 + 1:h0 + 2]))
        news.append(prev_p * cdec + st)
        ys.append(y)
    return jnp.concatenate(ys, axis=1), jnp.concatenate(news, axis=0)


def _ssd_fwd(xbc, dtraw, bias, alog, *, name):
    s = xbc.shape[0]
    nc = s // CHUNK
    nstate = SSD_HEADS * HEAD_DIM

    def body(x_ref, dt_ref, b_ref, a_ref, y_ref, st_ref, state_ref):
        @pl.when(pl.program_id(0) == 0)
        def _():
            state_ref[...] = jnp.zeros_like(state_ref)

        prev = state_ref[...]
        st_ref[...] = prev
        y, new = _ssd_chunk(x_ref[...], dt_ref[...], prev, b_ref[...], a_ref[...])
        y_ref[...] = y
        state_ref[...] = new

    small = pl.BlockSpec((1, LANES), lambda i: (0, 0))
    return pl.pallas_call(
        body, name=name, grid=(nc,),
        in_specs=[pl.BlockSpec((CHUNK, XBC_WIDTH), lambda i: (i, 0)), pl.BlockSpec((CHUNK, LANES), lambda i: (i, 0)),
                  small, small],
        out_specs=[pl.BlockSpec((CHUNK, SSD_WIDTH), lambda i: (i, 0)),
                   pl.BlockSpec((None, nstate, SSD_STATE), lambda i: (i, 0, 0))],
        out_shape=[jax.ShapeDtypeStruct((s, SSD_WIDTH), _F32), jax.ShapeDtypeStruct((nc, nstate, SSD_STATE), _F32)],
        scratch_shapes=[pltpu.VMEM((nstate, SSD_STATE), _F32)],
        compiler_params=_params(("arbitrary",)),
    )(xbc, dtraw, bias, alog)


def _ssd_bwd(xbc, dtraw, states, bias, alog, dy, dxs_extra, *, name):
    s = xbc.shape[0]
    nc = s // CHUNK
    nstate = SSD_HEADS * HEAD_DIM

    def body(x_ref, dt_ref, st_ref, b_ref, a_ref, dy_ref, ex_ref, dx_ref, ddt_ref, db_ref, da_ref, dstate_ref):
        @pl.when(pl.program_id(0) == 0)
        def _():
            dstate_ref[...] = jnp.zeros_like(dstate_ref)
            db_ref[...] = jnp.zeros_like(db_ref)
            da_ref[...] = jnp.zeros_like(da_ref)

        _, vjp = jax.vjp(_ssd_chunk, x_ref[...], dt_ref[...], st_ref[...], b_ref[...], a_ref[...])
        dx, ddt, dprev, db, da = vjp((dy_ref[...], dstate_ref[...]))
        dx_ref[:, :SSD_WIDTH] = dx[:, :SSD_WIDTH] + ex_ref[...]
        dx_ref[:, SSD_WIDTH:] = dx[:, SSD_WIDTH:]
        ddt_ref[...] = ddt
        db_ref[...] += db
        da_ref[...] += da
        dstate_ref[...] = dprev

    rev = lambda i: (nc - 1 - i, 0)
    small = pl.BlockSpec((1, LANES), lambda i: (0, 0))
    return pl.pallas_call(
        body, name=name, grid=(nc,),
        in_specs=[pl.BlockSpec((CHUNK, XBC_WIDTH), rev), pl.BlockSpec((CHUNK, LANES), rev),
                  pl.BlockSpec((None, nstate, SSD_STATE), lambda i: (nc - 1 - i, 0, 0)), small, small,
                  pl.BlockSpec((CHUNK, SSD_WIDTH), rev), pl.BlockSpec((CHUNK, SSD_WIDTH), rev)],
        out_specs=[pl.BlockSpec((CHUNK, XBC_WIDTH), rev), pl.BlockSpec((CHUNK, LANES), rev), small, small],
        out_shape=[jax.ShapeDtypeStruct((s, XBC_WIDTH), _F32), jax.ShapeDtypeStruct((s, LANES), _F32),
                   jax.ShapeDtypeStruct((1, LANES), _F32), jax.ShapeDtypeStruct((1, LANES), _F32)],
        scratch_shapes=[pltpu.VMEM((nstate, SSD_STATE), _F32)],
        compiler_params=_params(("arbitrary",)),
    )(xbc, dtraw, states, bias, alog, dy, dxs_extra)


def _pad_cols(a, width):
    return jnp.pad(a, ((0, 0), (0, width - a.shape[1])))


def _pad_rows(a, rows):
    return jnp.pad(a, ((0, rows - a.shape[0]), (0, 0)))


def _tie(a, token):
    return a + token[0:1, 0:1].astype(a.dtype)


def _local_step(x, mem, target, w, fetch, emit):
    bf = _MXU
    d = D_MODEL
    h = _row_fwd(_f_rms, [x], [w['norm_mix_g']], [(d, bf)], name="f_norm_mix")
    w_in = fetch('in', h)['w_in']
    z_end, xbc_end, dt_end = SSD_WIDTH, SSD_WIDTH + XBC_WIDTH, SSD_WIDTH + XBC_WIDTH + SSD_HEADS
    w_z, w_xbc = w_in[:z_end], w_in[z_end:xbc_end]
    w_dt = _pad_rows(w_in[xbc_end:dt_end], LANES)
    w_a, w_g = w_in[dt_end:dt_end + CF_WIDTH], w_in[dt_end + CF_WIDTH:]
    dt_bias = _pad_cols(w['ssd_dt_bias'], LANES)
    a_log = _pad_cols(w['ssd_A_log'], LANES)
    d_exp = jnp.repeat(w['ssd_D'], HEAD_DIM, axis=1)
    g_final = w['norm_final_g'].reshape(1, D_MODEL)

    z, xbc, dtr, ga, gg = _mm_fan_out(h, [w_z, w_xbc, w_dt, w_a, w_g], tb=True, out_dtypes=[bf, _F32, _F32, bf, bf],
                                      name="f_in")
    wc = fetch('conv', xbc)
    ssd_w = _pad_rows(wc['ssd_conv_w'], SUBLANES)
    cf_w = _pad_rows(wc['cf_conv_w'], 32)
    xbc_a = _conv_fwd([xbc], ssd_w, w['ssd_conv_b'], SSD_CONV, act=True, name="f_ssd_conv")
    y_ssd, states = _ssd_fwd(xbc_a, dtr, dt_bias, a_log, name="f_ssd")
    xs_win = (xbc_a, SSD_WIDTH, 0)
    y_n = _row_fwd(_f_gate, [y_ssd, xs_win, z], [d_exp, w['ssd_norm_g']], [(d, bf)], name="f_ssd_gate")
    u_c = _conv_fwd([ga, gg], cf_w, w['cf_conv_b'], CF_CONV, glu=True, name="f_cf_conv")
    u = _row_fwd(_f_ln, [u_c], [w['cf_ln_g'], w['cf_ln_b']], [(d, bf)], name="f_cf_ln")
    wm = fetch('mid', y_n)
    w_out_y, w_out_u = wm['w_out'][:SSD_WIDTH], wm['w_out'][SSD_WIDTH:]
    x1, hq = _mm_fan_in([(y_n, w_out_y), (u, w_out_u)], add=x, epilogue=(_f_rms, w['norm_xattn_g'], bf), name="f_out")
    q = _mm(hq, wm['w_q'], out_dtype=bf, name="f_q")
    memn = _row_fwd(_f_rms, [mem], [w['norm_mem_g']], [(d, bf)], name="f_norm_mem")
    kv = _mm(memn, wm['w_kv'], name="f_kv")
    k_mat, v_mat = kv[:, :d], kv[:, d:]
    o = _row_fwd(_f_att, [q], [k_mat, v_mat], [(d, bf)], name="f_att")
    x2, hf = _mm_fan_in([(o, wm['w_o'])], add=x1, epilogue=(_f_rms, w['norm_ffn_g'], bf), name="f_o")
    wf = fetch('ffn', hf)
    gate, up, act = _mm_fan_out(hf, [wf['w_gate'], wf['w_up']], tb=True, out_dtypes=[bf, bf], epilogue=_f_swiglu,
                                extra_outs=[(D_FF, bf)], tm=256, name="f_ffn_in")
    x3 = _mm(act, wf['w_down'], add=x2, name="f_down")

    dx3, dg_final, loss = _loss_bwd(x3, target, g_final, name="b_loss")
    g = {'norm_final_g': dg_final.reshape(d)}

    dact = _mm(dx3, wf['w_down'], tb=True, out_dtype=bf, name="b_down_x")
    dw_down = _mm(act, dx3, ta=True, out_dtype=bf, name="b_down_w")
    def swiglu_bwd(gate_t, up_t, dact_t):
        return jax.vjp(_f_swiglu, gate_t, up_t)[1](dact_t)

    dhf, dgate, dup = _mm_fan_in([(None, wf['w_gate']), (None, wf['w_up'])], prologue=swiglu_bwd, pro_ins=[gate, up, dact],
                                 pro_out_dtypes=[bf, bf], out_dtype=bf, tm=256, name="b_ffn_in_x")
    sent = emit({'w_down': dw_down, 'w_gate': _mm(dgate, hf, ta=True, out_dtype=bf, name="b_gate_w"),
                 'w_up': _mm(dup, hf, ta=True, out_dtype=bf, name="b_up_w")})
    (dx2,), (g['norm_ffn_g'],) = _row_bwd(_f_rms, [x2], [_tie(w['norm_ffn_g'], sent)], [dhf], adds={0: dx3}, name="b_norm_ffn")

    do = _mm(dx2, wm['w_o'], tb=True, out_dtype=bf, name="b_o_x")
    dw_o = _mm(o, dx2, ta=True, out_dtype=bf, name="b_o_w")
    (dq,), (dk, dv) = _row_bwd(_f_att, [q], [k_mat, v_mat], [do], row_dtypes=[bf], name="b_att")
    dw_q = _mm(hq, dq, ta=True, out_dtype=bf, name="b_q_w")
    dhq = _mm(dq, wm['w_q'], tb=True, out_dtype=bf, name="b_q_x")
    (dx1,), (g['norm_xattn_g'],) = _row_bwd(_f_rms, [x1], [w['norm_xattn_g']], [dhq], adds={0: dx2}, name="b_norm_xattn")
    dkv = jnp.concatenate([dk, dv], axis=1)
    dmemn = _mm(dkv, wm['w_kv'], tb=True, name="b_kv_x")
    _, (g['norm_mem_g'],) = _row_bwd(_f_rms, [mem], [w['norm_mem_g']], [dmemn], need=[False], name="b_norm_mem")
    sent = emit({'w_o': dw_o, 'w_q': dw_q, 'w_kv': _mm(memn, dkv, ta=True, out_dtype=bf, name="b_kv_w")},
                after=g['norm_mem_g'])

    dyn, du = _mm_fan_out(dx1, [w_out_y, w_out_u], tb=True, out_dtypes=[bf, bf], name="b_out_x")
    (du_c,), (g['cf_ln_g'], g['cf_ln_b']) = _row_bwd(_f_ln, [u_c], [_tie(w['cf_ln_g'], sent), w['cf_ln_b']], [du], name="b_cf_ln")
    sent = emit({'w_out': jnp.concatenate([_mm(y_n, dx1, ta=True, out_dtype=bf, name="b_out_y_w"), _mm(u, dx1, ta=True, out_dtype=bf, name="b_out_u_w")], axis=0)})
    (dga, dgg), dcf_w, g['cf_conv_b'] = _conv_bwd([ga, gg], cf_w, w['cf_conv_b'], du_c, CF_CONV, glu=True, name="b_cf_conv")
    g['cf_conv_w'] = dcf_w[:CF_CONV]
    (dy_ssd, dxs, dz), (dd_exp, g['ssd_norm_g']) = _row_bwd(
        _f_gate, [y_ssd, xs_win, z], [d_exp, _tie(w['ssd_norm_g'], sent)], [dyn], row_dtypes=[_F32, _F32, bf], name="b_ssd_gate")
    g['ssd_D'] = jnp.sum(dd_exp.reshape(SSD_HEADS, HEAD_DIM), axis=1).reshape(1, SSD_HEADS)
    dxbc_a, ddtr, ddt_bias, da_log = _ssd_bwd(xbc_a, dtr, states, dt_bias, a_log, dy_ssd, dxs, name="b_ssd")
    g['ssd_dt_bias'] = ddt_bias[:, :SSD_HEADS]
    g['ssd_A_log'] = da_log[:, :SSD_HEADS]
    (dxbc,), dssd_w, g['ssd_conv_b'] = _conv_bwd([xbc], ssd_w, w['ssd_conv_b'], dxbc_a, SSD_CONV, act=True, name="b_ssd_conv")
    g['ssd_conv_w'] = dssd_w[:SSD_CONV]

    sent = emit({'w_in': jnp.concatenate([
        _mm(dz, h, ta=True, out_dtype=bf, name="b_in_z_w"), _mm(dxbc, h, ta=True, out_dtype=bf, name="b_in_xbc_w"),
        _mm(ddtr, h, ta=True, out_dtype=bf, name="b_in_dt_w")[:SSD_HEADS],
        _mm(dga, h, ta=True, out_dtype=bf, name="b_in_a_w"), _mm(dgg, h, ta=True, out_dtype=bf, name="b_in_g_w")], axis=0)})
    dh = _mm_fan_in([(dz, w_z), (dxbc, w_xbc), (ddtr, _tie(w_dt, sent)), (dga, w_a), (dgg, w_g)], out_dtype=bf,
                    name="b_in_x")
    (dx,), (g['norm_mix_g'],) = _row_bwd(_f_rms, [x], [w['norm_mix_g']], [dh], adds={0: dx1}, name="b_norm_mix")
    return loss, dx, g


_ANY = pl.BlockSpec(memory_space=pl.ANY)


def _place():
    x, y, c = lax.axis_index("x"), lax.axis_index("y"), lax.axis_index("c")
    return x, y, c


def _all_gather(arrs, *, name):
    n = len(arrs)

    def body(*refs):
        ins, outs = refs[:n], refs[n:2 * n]
        send_sems, recv_sems, local_sems = refs[2 * n:]
        x, y, c = _place()
        me, sibling = (x, y, c), (x, y, 1 - c)
        chips = [(1 - x, y), (x, 1 - y), (1 - x, 1 - y)]

        def slot(a, dev):
            return outs[a].at[4 * dev[0] + 2 * dev[1] + dev[2]]

        def copy(a, k, block, to, src=None):
            return pltpu.make_async_remote_copy(
                src_ref=slot(a, block) if src is None else src, dst_ref=slot(a, block),
                send_sem=send_sems.at[a, k], recv_sem=recv_sems.at[a, k], device_id=to, device_id_type=MESH)

        mine = [pltpu.make_async_copy(ins[a], slot(a, me), local_sems.at[a]) for a in range(n)]
        for cp in mine:
            cp.start()
        first = []
        for a in range(n):
            first.append(copy(a, 0, me, sibling, src=ins[a]))
            first += [copy(a, 1 + j, me, (*chip, c), src=ins[a]) for j, chip in enumerate(chips)]
        for cp in first:
            cp.start()
        passed = []
        for a in range(n):
            for j, chip in enumerate(chips):
                copy(a, 1 + j, (*chip, c), me).wait_recv()
                fwd = copy(a, 4 + j, (*chip, c), sibling)
                fwd.start()
                passed.append(fwd)
        for a in range(n):
            copy(a, 0, sibling, me).wait_recv()
            for j, chip in enumerate(chips):
                copy(a, 4 + j, (*chip, 1 - c), me).wait_recv()
        for cp in first + passed:
            cp.wait_send()
        for cp in mine:
            cp.wait()

    return pl.pallas_call(
        body, name=name, in_specs=[_ANY] * n, out_specs=[_ANY] * n,
        out_shape=[jax.ShapeDtypeStruct((N_DEV,) + a.shape, a.dtype) for a in arrs],
        scratch_shapes=[pltpu.SemaphoreType.DMA((n, 7)), pltpu.SemaphoreType.DMA((n, 7)), pltpu.SemaphoreType.DMA((n,))],
    )(*arrs)


_HBM = pl.BlockSpec(memory_space=pltpu.HBM)
_SEM = pl.BlockSpec(memory_space=pltpu.SEMAPHORE)
_EFFECT = pltpu.SideEffectType.DATAFLOW_SIDE_EFFECTING
_FLIPS = [(dx, dy, dc) for dx in (0, 1) for dy in (0, 1) for dc in (0, 1)][1:]


def _peer(flip, x, y, c):
    return (1 - x if flip[0] else x, 1 - y if flip[1] else y, 1 - c if flip[2] else c)


def _send_start(srcs, blocked, *, after=None, name):
    n = len(srcs)
    lands = [jax.ShapeDtypeStruct(s.shape if blocked else (N_DEV,) + s.shape, s.dtype) for s in srcs]
    n_in = 2 * n + (after is not None)

    def body(*refs):
        src_refs, land_refs = refs[:n], refs[n:2 * n]
        send_sems, recv_sems = refs[n_in], refs[n_in + 1]
        token = refs[-1]
        x, y, c = _place()
        me = 4 * x + 2 * y + c
        for a in range(n):
            for k, flip in enumerate(_FLIPS):
                p = _peer(flip, x, y, c)
                src = src_refs[a].at[4 * p[0] + 2 * p[1] + p[2]] if blocked else src_refs[a]
                pltpu.make_async_remote_copy(
                    src_ref=src, dst_ref=land_refs[a].at[me], send_sem=send_sems.at[7 * a + k], recv_sem=recv_sems.at[7 * a + k],
                    device_id=p, device_id_type=MESH).start()
        token[...] = jnp.zeros_like(token)

    res = pl.pallas_call(
        body, name=name,
        out_shape=(pltpu.SemaphoreType.DMA((7 * n,)), pltpu.SemaphoreType.DMA((7 * n,)),
                   *[pltpu.HBM(s.shape, s.dtype) for s in srcs], *[pltpu.HBM(l.shape, l.dtype) for l in lands],
                   jax.ShapeDtypeStruct((SUBLANES, LANES), _F32)),
        in_specs=[_HBM] * (2 * n) + [_ANY] * (after is not None),
        out_specs=(_SEM, _SEM, *[_HBM] * (2 * n), pl.BlockSpec(memory_space=pltpu.VMEM)),
        input_output_aliases={i: 2 + i for i in range(2 * n)},
        compiler_params=pltpu.CompilerParams(has_side_effects=_EFFECT),
    )(*[pltpu.with_memory_space_constraint(s, pltpu.HBM) for s in srcs],
      *[pltpu.with_memory_space_constraint(lax.empty(l.shape, l.dtype), pltpu.HBM) for l in lands],
      *([after] if after is not None else []))
    return res[0], res[1], list(res[2:2 + n]), list(res[2 + n:2 + 2 * n]), res[-1]


def _send_wait(handles, after, blocked, *, name):
    send_sems, recv_sems, srcs, lands, _ = handles
    n = len(srcs)

    def body(*refs):
        src_refs, land_refs = refs[:n], refs[n:2 * n]
        send_sems, recv_sems = refs[2 * n], refs[2 * n + 1]
        x, y, c = _place()
        for a in range(n):
            for k, flip in enumerate(_FLIPS):
                p = _peer(flip, x, y, c)
                pid = 4 * p[0] + 2 * p[1] + p[2]
                cp = pltpu.make_async_remote_copy(
                    src_ref=src_refs[a].at[pid] if blocked else src_refs[a], dst_ref=land_refs[a].at[pid],
                    send_sem=send_sems.at[7 * a + k], recv_sem=recv_sems.at[7 * a + k], device_id=p, device_id_type=MESH)
                cp.wait_send()
                cp.wait_recv()

    res = pl.pallas_call(
        body, name=name,
        out_shape=tuple(pltpu.HBM(s.shape, s.dtype) for s in srcs + lands),
        in_specs=[_HBM] * (2 * n) + [_SEM, _SEM, _ANY], out_specs=tuple([_HBM] * (2 * n)),
        input_output_aliases={i: i for i in range(2 * n)},
        compiler_params=pltpu.CompilerParams(has_side_effects=_EFFECT),
    )(*srcs, *lands, send_sems, recv_sems, after)
    return list(res[:n]), list(res[n:])


def _adamw(parts, w, m, v, *, own=None, me=None, name):
    p, r, c = parts.shape
    tr = _pick(r, (256, 176, 128, 64, 32, 16, 8))
    if own is not None:
        tc = c if tr < r else _pick(c, (256, 128))
        return _adamw_own(parts, own, me, w, m, v, tr, tc, name=name)

    def body(p_ref, w_ref, m_ref, v_ref, g_ref, d_ref, nm_ref, nv_ref):
        g = p_ref[0].astype(_F32)
        for i in range(1, p):
            g = g + p_ref[i].astype(_F32)
        _adamw_math(g, w_ref, m_ref, v_ref, g_ref, d_ref, nm_ref, nv_ref)

    blk = pl.BlockSpec((tr, c), lambda i: (i, 0))
    return pl.pallas_call(
        body, name=name, grid=(r // tr,),
        in_specs=[pl.BlockSpec((p, tr, c), lambda i: (0, i, 0)), blk, blk, blk], out_specs=[blk] * 4,
        out_shape=[jax.ShapeDtypeStruct((r, c), _F32)] * 4,
        compiler_params=_params(("parallel",)),
    )(parts, w, m, v)


def _adamw_math(g, w_ref, m_ref, v_ref, g_ref, d_ref, nm_ref, nv_ref):
    wv = w_ref[...]
    mn = ADAM_B1 * m_ref[...] + (1.0 - ADAM_B1) * g
    vn = ADAM_B2 * v_ref[...] + (1.0 - ADAM_B2) * jnp.square(g)
    m_hat = mn / (1.0 - ADAM_B1 ** ADAM_STEP)
    v_hat = vn / (1.0 - ADAM_B2 ** ADAM_STEP)
    g_ref[...] = g
    d_ref[...] = -ADAM_LR * (m_hat / (jnp.sqrt(v_hat) + ADAM_EPS) + ADAM_WD * wv)
    nm_ref[...] = mn
    nv_ref[...] = vn


def _adamw_own(parts, own, me, w, m, v, tr, tc, *, name):
    p, r, c = parts.shape

    def body(me_ref, p_ref, own_ref, w_ref, m_ref, v_ref, g_ref, d_ref, nm_ref, nv_ref):
        mine = own_ref[...].astype(_F32)
        g = jnp.where(me_ref[0] == 0, mine, p_ref[0].astype(_F32))
        for i in range(1, p):
            g = g + jnp.where(me_ref[0] == i, mine, p_ref[i].astype(_F32))
        _adamw_math(g, w_ref, m_ref, v_ref, g_ref, d_ref, nm_ref, nv_ref)

    blk = pl.BlockSpec((tr, tc), lambda i, j, me_ref: (i, j))
    grid_spec = pltpu.PrefetchScalarGridSpec(
        num_scalar_prefetch=1, grid=(r // tr, c // tc),
        in_specs=[pl.BlockSpec((p, tr, tc), lambda i, j, me_ref: (0, i, j)),
                  pl.BlockSpec((None, tr, tc), lambda i, j, me_ref: (me_ref[0], i, j)), blk, blk, blk],
        out_specs=[blk] * 4)
    return pl.pallas_call(
        body, name=name, grid_spec=grid_spec, out_shape=[jax.ShapeDtypeStruct((r, c), _F32)] * 4,
        compiler_params=_params(("parallel", "parallel")),
    )(me.reshape(1).astype(jnp.int32), parts, own, w, m, v)


def _adamw_rows(g_row, offsets, ws, ms, vs, *, name):
    k = len(ws)

    def body(*refs):
        g_ref, w_refs, m_refs, v_refs = refs[0], refs[1:1 + k], refs[1 + k:1 + 2 * k], refs[1 + 2 * k:1 + 3 * k]
        outs = refs[1 + 3 * k:]
        for i in range(k):
            gi = g_ref[:, offsets[i]:offsets[i] + ws[i].shape[1]]
            _adamw_math(gi, w_refs[i], m_refs[i], v_refs[i], *outs[4 * i:4 * i + 4])

    return pl.pallas_call(
        body, name=name, out_shape=[jax.ShapeDtypeStruct(w.shape, _F32) for w in ws for _ in range(4)],
    )(g_row, *ws, *ms, *vs)


def _sum_parts(parts, *, name):
    p, r, c = parts.shape

    def body(p_ref, o_ref):
        g = p_ref[0].astype(_F32)
        for i in range(1, p):
            g = g + p_ref[i].astype(_F32)
        o_ref[...] = g

    return pl.pallas_call(body, name=name, out_shape=jax.ShapeDtypeStruct((r, c), _F32))(parts)


def _pack(vals, rows):
    flat = jnp.concatenate([v.reshape(-1) for v in vals])
    return jnp.pad(flat, (0, rows * LANES - flat.shape[0])).reshape(rows, LANES)


def _unpack(packed, shapes):
    flat = packed.reshape(-1)
    out, pos = [], 0
    for shp in shapes:
        size = math.prod(shp)
        out.append(flat[pos:pos + size].reshape(shp))
        pos += size
    return out


def _pack_rows(shapes):
    total = sum(math.prod(s) for s in shapes)
    return -(-total // (LANES * SUBLANES)) * SUBLANES


def kernel(x, mem, norm_mix_g, w_in, ssd_conv_w, ssd_conv_b, ssd_dt_bias, ssd_A_log, ssd_D, ssd_norm_g, cf_conv_w, cf_conv_b, cf_ln_g, cf_ln_b, w_out, norm_xattn_g, norm_mem_g, w_q, w_kv, w_o, norm_ffn_g, w_gate, w_up, w_down, norm_final_g, loss_target, m_norm_mix_g, m_w_in, m_ssd_conv_w, m_ssd_conv_b, m_ssd_dt_bias, m_ssd_A_log, m_ssd_D, m_ssd_norm_g, m_cf_conv_w, m_cf_conv_b, m_cf_ln_g, m_cf_ln_b, m_w_out, m_norm_xattn_g, m_norm_mem_g, m_w_q, m_w_kv, m_w_o, m_norm_ffn_g, m_w_gate, m_w_up, m_w_down, m_norm_final_g, v_norm_mix_g, v_w_in, v_ssd_conv_w, v_ssd_conv_b, v_ssd_dt_bias, v_ssd_A_log, v_ssd_D, v_ssd_norm_g, v_cf_conv_w, v_cf_conv_b, v_cf_ln_g, v_cf_ln_b, v_w_out, v_norm_xattn_g, v_norm_mem_g, v_w_q, v_w_kv, v_w_o, v_norm_ffn_g, v_w_gate, v_w_up, v_w_down, v_norm_final_g):
    args = dict(locals())
    wts = {n: args[n] for n in WEIGHT_NAMES}
    mom = {n: args["m_" + n] for n in WEIGHT_NAMES}
    var = {n: args["v_" + n] for n in WEIGHT_NAMES}
    me = 4 * lax.axis_index("x") + 2 * lax.axis_index("y") + lax.axis_index("c")

    groups = {'in': ['w_in'], 'conv': ['ssd_conv_w', 'cf_conv_w'], 'mid': ['w_out', 'w_q', 'w_kv', 'w_o'],
              'ffn': ['w_gate', 'w_up', 'w_down']}
    def shard(n, a):
        return jnp.transpose(a[0], (1, 0)) if n in TRANSPOSED else a[0]

    gathers, started = {}, None
    for grp, names in groups.items():
        shards = [wts[n][0] if grp == 'conv' else shard(n, wts[n]).astype(_MXU) for n in names]
        gathers[grp] = _send_start(shards, False, after=started, name="gather_%s_start" % grp)
        started = gathers[grp][4]

    def fetch(grp, after):
        srcs, lands = _send_wait(gathers[grp], started if after is None else after, False, name="gather_%s_wait" % grp)
        out = {}
        for n, own, gth in zip(groups[grp], srcs, lands):
            gth = lax.dynamic_update_slice_in_dim(gth, own[None], me, axis=0)
            if n == 'w_kv' or grp == 'conv':
                out[n] = jnp.transpose(gth, (1, 0, 2)).reshape(gth.shape[1], N_DEV * gth.shape[2])
            else:
                out[n] = gth.reshape(N_DEV * gth.shape[1], gth.shape[2])
        return out

    exchanges = []

    def emit(grads, after=None):
        blocks = []
        for n, gw in grads.items():
            if n == 'w_kv':
                gw = jnp.transpose(gw.reshape(gw.shape[0], N_DEV, gw.shape[1] // N_DEV), (1, 0, 2))
            else:
                gw = gw.reshape(N_DEV, gw.shape[0] // N_DEV, gw.shape[1])
            blocks.append(gw.astype(jnp.bfloat16))
        first = next(iter(grads))
        exchanges.append((list(grads), _send_start(blocks, True, after=after, name="exchange_%s_start" % first), first))
        return exchanges[-1][1][4]

    full = {n: wts[n] for n in WEIGHT_NAMES if n not in BIG and n not in groups['conv']}
    full['norm_mix_g'] = _tie(norm_mix_g, started)

    loss_blk, grad_x, g = _local_step(x[0], mem[0], loss_target[0], full, fetch, emit)

    small = [n for n in WEIGHT_NAMES if n not in BIG]
    g['loss'] = loss_blk[0:1, 0:1]
    items = small + ['loss']
    size = {n: math.prod(g[n].shape) for n in items}
    seg = {n: -(-size[n] // LANES) * LANES for n in items}
    off, pos = {}, 0
    for n in items:
        off[n], pos = pos, pos + seg[n]
    rows = -(-pos // (LANES * SUBLANES)) * SUBLANES
    flat = jnp.concatenate([jnp.pad(g[n].reshape(-1), (0, seg[n] - size[n])) for n in items]
                           + [jnp.zeros((rows * LANES - pos,), _F32)])
    small_sent = _send_start([flat.reshape(rows, LANES)], False, name="gather_small_start")

    out_g, out_d, out_m, out_v = {}, {}, {}, {}
    for names, handles, first in exchanges:
        srcs, lands = _send_wait(handles, small_sent[4], True, name="exchange_%s_wait" % first)
        for n, own, parts in zip(names, srcs, lands):
            res = _adamw(parts, shard(n, wts[n]), shard(n, mom[n]), shard(n, var[n]), own=own, me=me, name="adamw_" + n)
            out_g[n], out_d[n], out_m[n], out_v[n] = [(jnp.transpose(r, (1, 0)) if n in TRANSPOSED else r)[None] for r in res]

    srcs, lands = _send_wait(small_sent, out_g[exchanges[-1][0][-1]], False, name="gather_small_wait")
    small_parts = lax.dynamic_update_slice_in_dim(lands[0], srcs[0][None], me, axis=0)
    g_row = _sum_parts(small_parts, name="sum_small_grads").reshape(1, rows * LANES)
    loss = g_row[0, off['loss']]
    rep = [n for n in small if n not in groups['conv']]
    as_row = lambda a: a.reshape(1, -1)
    res = _adamw_rows(g_row, [off[n] for n in rep], [as_row(wts[n]) for n in rep], [as_row(mom[n]) for n in rep],
                      [as_row(var[n]) for n in rep], name="adamw_small")
    for i, n in enumerate(rep):
        out_g[n], out_d[n], out_m[n], out_v[n] = [r.reshape(wts[n].shape) for r in res[4 * i:4 * i + 4]]
    for n in groups['conv']:
        k_taps, width = g[n].shape
        g_full = g_row[0, off[n]:off[n] + size[n]].reshape(k_taps, width)
        g_mine = lax.dynamic_slice_in_dim(g_full, me * (width // N_DEV), width // N_DEV, axis=1)
        res = _adamw(g_mine[None], wts[n][0], mom[n][0], var[n][0], name="adamw_" + n)
        out_g[n], out_d[n], out_m[n], out_v[n] = [r[None] for r in res]

    return (loss, grad_x[None], *[out_g[n] for n in WEIGHT_NAMES], *[out_d[n] for n in WEIGHT_NAMES],
            *[out_m[n] for n in WEIGHT_NAMES], *[out_v[n] for n in WEIGHT_NAMES])
```

```python
import functools
import math

import jax
import jax.numpy as jnp
from jax import lax
from jax.experimental import pallas as pl
from jax.experimental.pallas import tpu as pltpu

_F32 = jnp.float32
_MXU = jnp.bfloat16
_PREC = None
_VMEM_LIMIT = 56 * 1024 * 1024

D_MODEL = 1024
HEAD_DIM = 64
SSD_HEADS = 16
SSD_WIDTH = 1024
SSD_STATE = 128
SSD_CONV = 4
CHUNK = 128
XBC_WIDTH = 1536
CF_WIDTH = 1024
CF_CONV = 31
X_HEADS = 4
X_HEAD_DIM = 256
D_FF = 2816
EPS = 1e-6
N_DEV = 8
LANES = 128
SUBLANES = 8

ADAM_LR = 0.001
ADAM_B1 = 0.9
ADAM_B2 = 0.999
ADAM_EPS = 1e-08
ADAM_WD = 0.01
ADAM_STEP = 10

MESH = pl.DeviceIdType.MESH
WEIGHT_NAMES = ['norm_mix_g', 'w_in', 'ssd_conv_w', 'ssd_conv_b', 'ssd_dt_bias', 'ssd_A_log', 'ssd_D', 'ssd_norm_g',
                'cf_conv_w', 'cf_conv_b', 'cf_ln_g', 'cf_ln_b', 'w_out', 'norm_xattn_g', 'norm_mem_g', 'w_q', 'w_kv',
                'w_o', 'norm_ffn_g', 'w_gate', 'w_up', 'w_down', 'norm_final_g']
BIG = ['w_in', 'w_out', 'w_q', 'w_kv', 'w_o', 'w_gate', 'w_up', 'w_down']
TRANSPOSED = ('w_in', 'w_gate', 'w_up')


def _params(sem=None):
    return pltpu.CompilerParams(dimension_semantics=sem, vmem_limit_bytes=_VMEM_LIMIT)


def _pick(n, cands):
    for c in cands:
        if n % c == 0:
            return c
    return n


def _mm(a, b, *, ta=False, tb=False, add=None, out_dtype=_F32, name):
    (kdim, m) = a.shape if ta else a.shape[::-1]
    (n, k2) = b.shape if tb else b.shape[::-1]
    assert kdim == k2, (a.shape, b.shape, ta, tb)
    if ta:
        tm = m if m <= 1024 else _pick(m, (1408, 1024, 512, 256, 128))
        tn = n if n <= 1536 else _pick(n, (1408, 1024, 512, 256, 128))
        tk = _pick(kdim, (1024, 512, 256, 128))
    else:
        tm = _pick(m, (512, 256, 128))
        tn = n if n <= 2816 else _pick(n, (1408, 1024, 512, 256, 128))
        tk = kdim if kdim <= 2816 else _pick(kdim, (1408, 1024, 512, 256, 128))
    nk = kdim // tk
    dn = (((0 if ta else 1,), (1 if tb else 0,)), ((), ()))

    def body(*refs):
        a_ref, b_ref = refs[0], refs[1]
        add_ref = refs[2] if add is not None else None
        o_ref = refs[3 if add is not None else 2]
        acc_ref = refs[-1]
        k = pl.program_id(2)
        prod = lax.dot_general(a_ref[...].astype(_MXU), b_ref[...].astype(_MXU), dn,
                               preferred_element_type=_F32, precision=_PREC)

        def finish(r):
            if add_ref is not None:
                r = r + add_ref[...].astype(_F32)
            o_ref[...] = r.astype(o_ref.dtype)

        if nk == 1:
            finish(prod)
            return

        @pl.when(k == 0)
        def _():
            acc_ref[...] = prod

        @pl.when(jnp.logical_and(k > 0, k < nk - 1))
        def _():
            acc_ref[...] += prod

        @pl.when(k == nk - 1)
        def _():
            finish(acc_ref[...] + prod)

    a_spec = pl.BlockSpec((tk, tm), lambda i, j, k: (k, i)) if ta else pl.BlockSpec((tm, tk), lambda i, j, k: (i, k))
    b_spec = pl.BlockSpec((tn, tk), lambda i, j, k: (j, k)) if tb else pl.BlockSpec((tk, tn), lambda i, j, k: (k, j))
    o_spec = pl.BlockSpec((tm, tn), lambda i, j, k: (i, j))
    ins, specs = [a, b], [a_spec, b_spec]
    if add is not None:
        ins.append(add)
        specs.append(o_spec)
    return pl.pallas_call(
        body, name=name, grid=(m // tm, n // tn, nk), in_specs=specs, out_specs=o_spec,
        out_shape=jax.ShapeDtypeStruct((m, n), out_dtype),
        scratch_shapes=[pltpu.VMEM((tm, tn), _F32)] if nk > 1 else [],
        compiler_params=_params(("parallel", "parallel", "arbitrary")),
    )(*ins)


def _resident(shape):
    return pl.BlockSpec(shape, lambda i: (0,) * len(shape), pipeline_mode=pl.Buffered(1))


def _mm_fan_out(a, bs, *, tb, out_dtypes, epilogue=None, extra_outs=(), tm=512, name):
    m, kdim = a.shape
    tm = min(tm, m)
    ns = [b.shape[0] if tb else b.shape[1] for b in bs]
    nb = len(bs)
    kind = "nt" if tb else "nn"

    def body(*refs):
        a_ref, b_refs, o_refs = refs[0], refs[1:1 + nb], refs[1 + nb:]
        av = a_ref[...].astype(_MXU)
        prods = [lax.dot_general(av, b[...].astype(_MXU), _DN[kind], preferred_element_type=_F32, precision=_PREC)
                 for b in b_refs]
        for o_ref, p in zip(o_refs[:nb], prods):
            o_ref[...] = p.astype(o_ref.dtype)
        if epilogue is not None:
            for o_ref, v in zip(o_refs[nb:], _tup(epilogue(*prods))):
                o_ref[...] = v.astype(o_ref.dtype)

    widths = ns + [w for w, _ in extra_outs]
    dtypes = list(out_dtypes) + [dt for _, dt in extra_outs]
    return pl.pallas_call(
        body, name=name, grid=(m // tm,),
        in_specs=[pl.BlockSpec((tm, kdim), lambda i: (i, 0))] + [_resident(b.shape) for b in bs],
        out_specs=[pl.BlockSpec((tm, w), lambda i: (i, 0)) for w in widths],
        out_shape=[jax.ShapeDtypeStruct((m, w), dt) for w, dt in zip(widths, dtypes)],
        compiler_params=_params(("parallel",)),
    )(a, *bs)


def _mm_fan_in(pairs, *, add=None, out_dtype=_F32, prologue=None, pro_ins=(), pro_out_dtypes=(), epilogue=None,
               tm=512, name):
    bs = [b for _, b in pairs]
    nb = len(bs)
    n = bs[0].shape[1]
    rows_in = list(pro_ins) if prologue is not None else [a for a, _ in pairs]
    m = rows_in[0].shape[0]
    tm = min(tm, m)
    n_r = len(rows_in)

    def body(*refs):
        r_refs, b_refs = refs[:n_r], refs[n_r:n_r + nb]
        pos = n_r + nb
        add_ref = refs[pos] if add is not None else None
        pos += add is not None
        epi_ref = refs[pos] if epilogue is not None else None
        pos += epilogue is not None
        o_ref, po_refs = refs[pos], refs[pos + 1:]
        if prologue is not None:
            a_vals = _tup(prologue(*[r[...].astype(_F32) for r in r_refs]))
            for po, v in zip(po_refs, a_vals):
                po[...] = v.astype(po.dtype)
        else:
            a_vals = [r[...] for r in r_refs]
        acc = None
        for av, b in zip(a_vals, b_refs):
            p = lax.dot_general(av.astype(_MXU), b[...].astype(_MXU), _DN["nn"], preferred_element_type=_F32,
                                precision=_PREC)
            acc = p if acc is None else acc + p
        if add_ref is not None:
            acc = acc + add_ref[...].astype(_F32)
        o_ref[...] = acc.astype(o_ref.dtype)
        if epilogue is not None:
            po_refs[-1][...] = epilogue[0](acc, epi_ref[...]).astype(po_refs[-1].dtype)

    row = lambda w: pl.BlockSpec((tm, w), lambda i: (i, 0))
    ins = rows_in + bs + ([add] if add is not None else []) + ([epilogue[1]] if epilogue is not None else [])
    in_specs = ([row(r.shape[1]) for r in rows_in] + [_resident(b.shape) for b in bs]
                + ([row(n)] if add is not None else []) + ([_resident(epilogue[1].shape)] if epilogue is not None else []))
    extra = [(b.shape[0], dt) for b, dt in zip(bs, pro_out_dtypes)] if prologue is not None else []
    if epilogue is not None:
        extra.append((n, epilogue[2]))
    res = pl.pallas_call(
        body, name=name, grid=(m // tm,), in_specs=in_specs,
        out_specs=[row(n)] + [row(w) for w, _ in extra],
        out_shape=[jax.ShapeDtypeStruct((m, n), out_dtype)] + [jax.ShapeDtypeStruct((m, w), dt) for w, dt in extra],
        compiler_params=_params(("parallel",)),
    )(*ins)
    return res if extra else res[0]


def _row_spec(r, ts):
    if isinstance(r, tuple):
        arr, width, cblk = r
        return arr, pl.BlockSpec((ts, width), lambda i, cblk=cblk: (i, cblk))
    return r, pl.BlockSpec((ts, r.shape[1]), lambda i: (i, 0))


def _tup(v):
    return tuple(v) if isinstance(v, (tuple, list)) else (v,)


def _row_fwd(f, rows, params, outs, *, name, ts=256):
    arrs, specs = zip(*[_row_spec(r, ts) for r in rows])
    s = arrs[0].shape[0]
    ts = min(ts, s)
    n_r, n_p = len(rows), len(params)

    def body(*refs):
        rv = [r[...].astype(_F32) for r in refs[:n_r]]
        pv = [p[...] for p in refs[n_r:n_r + n_p]]
        res = _tup(f(*rv, *pv))
        for o_ref, v in zip(refs[n_r + n_p:], res):
            o_ref[...] = v.astype(o_ref.dtype)

    res = pl.pallas_call(
        body, name=name, grid=(s // ts,),
        in_specs=list(specs) + [pl.BlockSpec(p.shape, lambda i: (0, 0)) for p in params],
        out_specs=[pl.BlockSpec((ts, w), lambda i: (i, 0)) for w, _ in outs],
        out_shape=[jax.ShapeDtypeStruct((s, w), dt) for w, dt in outs],
        compiler_params=_params(("parallel",)),
    )(*arrs, *params)
    return res[0] if len(outs) == 1 else res


def _row_bwd(f, rows, params, cts, *, need=None, adds=None, row_dtypes=None, name, ts=256):
    arrs, specs = zip(*[_row_spec(r, ts) for r in rows])
    s = arrs[0].shape[0]
    ts = min(ts, s)
    n_r, n_p, n_c = len(rows), len(params), len(cts)
    need = [True] * n_r if need is None else need
    adds = {} if adds is None else adds
    add_keys = sorted(adds)
    row_dtypes = [_F32] * n_r if row_dtypes is None else row_dtypes
    needed = [j for j in range(n_r) if need[j]]
    widths = [specs[j].block_shape[1] for j in range(n_r)]

    def body(*refs):
        pos = 0
        r_refs = refs[pos:pos + n_r]; pos += n_r
        p_refs = refs[pos:pos + n_p]; pos += n_p
        c_refs = refs[pos:pos + n_c]; pos += n_c
        a_refs = refs[pos:pos + len(add_keys)]; pos += len(add_keys)
        dr_refs = refs[pos:pos + len(needed)]; pos += len(needed)
        dp_refs = refs[pos:pos + n_p]
        rv = [r[...].astype(_F32) for r in r_refs]
        pv = [p[...] for p in p_refs]
        _, vjp = jax.vjp(lambda *a: _tup(f(*a)), *rv, *pv)
        g = vjp(tuple(c[...].astype(_F32) for c in c_refs))
        for o_ref, j in zip(dr_refs, needed):
            v = g[j]
            if j in adds:
                v = v + a_refs[add_keys.index(j)][...].astype(_F32)
            o_ref[...] = v.astype(o_ref.dtype)
        if n_p:
            @pl.when(pl.program_id(0) == 0)
            def _():
                for dp in dp_refs:
                    dp[...] = jnp.zeros_like(dp)
            for dp, v in zip(dp_refs, g[n_r:]):
                dp[...] += v

    ct_specs = [pl.BlockSpec((ts, c.shape[1]), lambda i: (i, 0)) for c in cts]
    add_specs = [pl.BlockSpec((ts, adds[j].shape[1]), lambda i: (i, 0)) for j in add_keys]
    res = pl.pallas_call(
        body, name=name, grid=(s // ts,),
        in_specs=list(specs) + [pl.BlockSpec(p.shape, lambda i: (0, 0)) for p in params] + ct_specs + add_specs,
        out_specs=[pl.BlockSpec((ts, widths[j]), lambda i: (i, 0)) for j in needed]
        + [pl.BlockSpec(p.shape, lambda i: (0, 0)) for p in params],
        out_shape=[jax.ShapeDtypeStruct((s, widths[j]), row_dtypes[j]) for j in needed]
        + [jax.ShapeDtypeStruct(p.shape, _F32) for p in params],
        compiler_params=_params(("arbitrary",)),
    )(*arrs, *params, *cts, *[adds[j] for j in add_keys])
    return list(res[:len(needed)]), list(res[len(needed):])


_DN = {"nn": (((1,), (0,)), ((), ())), "nt": (((1,), (1,)), ((), ())), "tn": (((0,), (0,)), ((), ()))}


def _make_dot(passes):
    def raw(a, b, kind):
        dn = _DN[kind]
        if passes == 1 or _MXU == _F32:
            return lax.dot_general(a.astype(_MXU), b.astype(_MXU), dn, preferred_element_type=_F32, precision=_PREC)
        a_hi, b_hi = a.astype(_MXU), b.astype(_MXU)
        a_lo = (a - a_hi.astype(_F32)).astype(_MXU)
        b_lo = (b - b_hi.astype(_F32)).astype(_MXU)
        out = lax.dot_general(a_hi, b_hi, dn, preferred_element_type=_F32)
        out = out + lax.dot_general(a_lo, b_hi, dn, preferred_element_type=_F32)
        return out + lax.dot_general(a_hi, b_lo, dn, preferred_element_type=_F32)

    @functools.partial(jax.custom_vjp, nondiff_argnums=(2,))
    def dot(a, b, kind):
        return raw(a, b, kind)

    def fwd(a, b, kind):
        return raw(a, b, kind), (a, b)

    def bwd(kind, res, ct):
        a, b = res
        if kind == "nn":
            return raw(ct, b, "nt"), raw(a, ct, "tn")
        if kind == "nt":
            return raw(ct, b, "nn"), raw(ct, a, "tn")
        return raw(b, ct, "nt"), raw(a, ct, "nn")

    dot.defvjp(fwd, bwd)
    return dot


_dot1 = _make_dot(1)
_dot3 = _make_dot(3)


def _sig(v):
    return 1.0 / (1.0 + jnp.exp(-v))


def _silu(v):
    return v * _sig(v)


def _f_rms(x, g):
    return x * lax.rsqrt(jnp.mean(x * x, axis=-1, keepdims=True) + EPS) * g


def _f_gate(y, xs, z, dexp, g):
    v = (y + dexp * xs) * _silu(z)
    half = SSD_WIDTH // 2
    parts = []
    for grp in range(2):
        vg = v[:, grp * half:(grp + 1) * half]
        parts.append(vg * lax.rsqrt(jnp.mean(vg * vg, axis=-1, keepdims=True) + EPS) * g[:, grp * half:(grp + 1) * half])
    return jnp.concatenate(parts, axis=1)


def _f_ln(u, g, b):
    mu = jnp.mean(u, axis=-1, keepdims=True)
    var = jnp.mean(jnp.square(u - mu), axis=-1, keepdims=True)
    return _silu((u - mu) * lax.rsqrt(var + EPS) * g + b)


def _f_glu(a, g):
    return a * _sig(g)


def _f_swiglu(gate, up):
    return _silu(gate) * up


def _f_att(q, k, v):
    outs = []
    for h in range(X_HEADS):
        sl = slice(h * X_HEAD_DIM, (h + 1) * X_HEAD_DIM)
        s = _dot1(q[:, sl], k[:, sl], "nt") * (X_HEAD_DIM ** -0.5)
        s = s - lax.stop_gradient(jnp.max(s, axis=-1, keepdims=True))
        p = jnp.exp(s)
        p = p / jnp.sum(p, axis=-1, keepdims=True)
        outs.append(_dot1(p, v[:, sl], "nn"))
    return jnp.concatenate(outs, axis=1)


def _loss_bwd(x3, target, g, *, dx_dtype=_F32, name, ts=256):
    s, d = x3.shape

    def f(x, t, gv):
        return 0.5 * jnp.sum(jnp.mean(jnp.square(_f_rms(x, gv) - t), axis=-1))

    def body(x_ref, t_ref, g_ref, dx_ref, dg_ref, l_ref):
        @pl.when(pl.program_id(0) == 0)
        def _():
            dg_ref[...] = jnp.zeros_like(dg_ref)
            l_ref[...] = jnp.zeros_like(l_ref)

        lv, (dx, dg) = jax.value_and_grad(f, argnums=(0, 2))(x_ref[...].astype(_F32), t_ref[...], g_ref[...])
        dx_ref[...] = dx.astype(dx_ref.dtype)
        dg_ref[...] += dg
        l_ref[...] += lv

    row = pl.BlockSpec((ts, d), lambda i: (i, 0))
    return pl.pallas_call(
        body, name=name, grid=(s // ts,),
        in_specs=[row, row, pl.BlockSpec((1, d), lambda i: (0, 0))],
        out_specs=[row, pl.BlockSpec((1, d), lambda i: (0, 0)), pl.BlockSpec((SUBLANES, LANES), lambda i: (0, 0))],
        out_shape=[jax.ShapeDtypeStruct((s, d), dx_dtype), jax.ShapeDtypeStruct((1, d), _F32),
                   jax.ShapeDtypeStruct((SUBLANES, LANES), _F32)],
        compiler_params=_params(("arbitrary",)),
    )(x3, target, g)


_CONV_PAD = 32
_CONV_ROWS = 128
_CONV_CB = 128


def _conv_taps(k_taps):
    groups = {}
    for k in range(k_taps):
        j = k_taps - 1 - k
        groups.setdefault(j % SUBLANES, []).append((k, j))
    return groups


def _conv_window(win, wv, groups, init):
    pad, rows = _CONV_PAD, _CONV_ROWS
    acc = init
    for rot, taps in groups.items():
        rolled = win if rot == 0 else pltpu.roll(win, rot, 0)
        for k, j in taps:
            off = pad - (j - rot)
            acc = acc + rolled[off:off + rows, :] * wv[k:k + 1, :]
    return acc


def _conv_fill(x_refs, xp_ref, s, glu):
    pad, cb = _CONV_PAD, _CONV_CB
    step = _pick(s, (512, 256, _CONV_ROWS))
    xp_ref[0:pad, :] = jnp.zeros((pad, cb), _F32)

    def fill(r, carry):
        base = pl.multiple_of(r * step, step)
        v = x_refs[0][pl.ds(base, step), :].astype(_F32)
        if glu:
            v = v * _sig(x_refs[1][pl.ds(base, step), :].astype(_F32))
        xp_ref[pl.ds(pad + base, step), :] = v
        return carry

    lax.fori_loop(0, s // step, fill, 0)


def _conv_fwd(xs, w, b, k_taps, *, glu=False, act=False, name):
    s, c = xs[0].shape
    kp = w.shape[0]
    pad, rows, cb = _CONV_PAD, _CONV_ROWS, _CONV_CB
    groups = _conv_taps(k_taps)
    n_in = len(xs)

    def body(*refs):
        x_refs = refs[:n_in]
        w_ref, b_ref, o_ref, xp_ref = refs[n_in:]
        _conv_fill(x_refs, xp_ref, s, glu)
        wv = w_ref[...]
        bias = jnp.broadcast_to(b_ref[...], (rows, cb))

        def chunk(r, carry):
            base = pl.multiple_of(r * rows, rows)
            acc = _conv_window(xp_ref[pl.ds(base, rows + pad), :], wv, groups, bias)
            o_ref[pl.ds(base, rows), :] = _silu(acc) if act else acc
            return carry

        lax.fori_loop(0, s // rows, chunk, 0)

    col = pl.BlockSpec((s, cb), lambda i: (0, i))
    return pl.pallas_call(
        body, name=name, grid=(c // cb,),
        in_specs=[col] * n_in + [pl.BlockSpec((kp, cb), lambda i: (0, i)), pl.BlockSpec((1, cb), lambda i: (0, i))],
        out_specs=col, out_shape=jax.ShapeDtypeStruct((s, c), _F32),
        scratch_shapes=[pltpu.VMEM((s + pad, cb), _F32)],
        compiler_params=_params(("parallel",)),
    )(*xs, w, b)


def _conv_bwd(xs, w, b, dy, k_taps, *, glu=False, act=False, name):
    s, c = xs[0].shape
    kp = w.shape[0]
    pad, rows, cb = _CONV_PAD, _CONV_ROWS, _CONV_CB
    groups = _conv_taps(k_taps)
    win_rows = rows + pad
    n_in = len(xs)

    def fold(v):
        acc = v[0:SUBLANES, :]
        for i in range(1, rows // SUBLANES):
            acc = acc + v[i * SUBLANES:(i + 1) * SUBLANES, :]
        return acc

    def body(*refs):
        x_refs = refs[:n_in]
        w_ref, b_ref, dy_ref = refs[n_in:n_in + 3]
        dx_refs = refs[n_in + 3:2 * n_in + 3]
        dw_ref, db_ref, xp_ref, dyp_ref, acc_ref, dbacc_ref = refs[2 * n_in + 3:]
        _conv_fill(x_refs, xp_ref, s, glu)
        dyp_ref[s:s + pad, :] = jnp.zeros((pad, cb), _F32)
        acc_ref[...] = jnp.zeros_like(acc_ref)
        dbacc_ref[...] = jnp.zeros_like(dbacc_ref)
        wv = w_ref[...]
        bias = jnp.broadcast_to(b_ref[...], (rows, cb))

        def through_act(r, carry):
            base = pl.multiple_of(r * rows, rows)
            d = dy_ref[pl.ds(base, rows), :]
            if act:
                pre = _conv_window(xp_ref[pl.ds(base, win_rows), :], wv, groups, bias)
                sg = _sig(pre)
                d = d * (sg * (1.0 + pre * (1.0 - sg)))
            dyp_ref[pl.ds(base, rows), :] = d
            return carry

        lax.fori_loop(0, s // rows, through_act, 0)

        def chunk(r, carry):
            base = pl.multiple_of(r * rows, rows)
            xwin = xp_ref[pl.ds(base, win_rows), :]
            dwin = dyp_ref[pl.ds(base, win_rows), :]
            dyc = dwin[0:rows, :]
            dxacc = jnp.zeros((rows, cb), _F32)
            for rot, taps in groups.items():
                xr = xwin if rot == 0 else pltpu.roll(xwin, rot, 0)
                dr = dwin if rot == 0 else pltpu.roll(dwin, win_rows - rot, 0)
                for k, j in taps:
                    a8 = j - rot
                    dxacc = dxacc + dr[a8:a8 + rows, :] * wv[k:k + 1, :]
                    prod = dyc * xr[pad - a8:pad - a8 + rows, :]
                    acc_ref[k * SUBLANES:(k + 1) * SUBLANES, :] += fold(prod)
            dbacc_ref[...] += fold(dyc)
            if glu:
                av = x_refs[0][pl.ds(base, rows), :].astype(_F32)
                sg = _sig(x_refs[1][pl.ds(base, rows), :].astype(_F32))
                dx_refs[0][pl.ds(base, rows), :] = (dxacc * sg).astype(dx_refs[0].dtype)
                dx_refs[1][pl.ds(base, rows), :] = (dxacc * av * sg * (1.0 - sg)).astype(dx_refs[1].dtype)
            else:
                dx_refs[0][pl.ds(base, rows), :] = dxacc
            return carry

        lax.fori_loop(0, s // rows, chunk, 0)
        dw_ref[...] = jnp.zeros_like(dw_ref)
        for k in range(k_taps):
            dw_ref[k:k + 1, :] = jnp.sum(acc_ref[k * SUBLANES:(k + 1) * SUBLANES, :], axis=0, keepdims=True)
        db_ref[...] = jnp.sum(dbacc_ref[...], axis=0, keepdims=True)

    col = pl.BlockSpec((s, cb), lambda i: (0, i))
    wspec = pl.BlockSpec((kp, cb), lambda i: (0, i))
    bspec = pl.BlockSpec((1, cb), lambda i: (0, i))
    dx_dtype = xs[0].dtype if glu else _F32
    res = pl.pallas_call(
        body, name=name, grid=(c // cb,),
        in_specs=[col] * n_in + [wspec, bspec, col], out_specs=[col] * n_in + [wspec, bspec],
        out_shape=[jax.ShapeDtypeStruct((s, c), dx_dtype)] * n_in
        + [jax.ShapeDtypeStruct((kp, c), _F32), jax.ShapeDtypeStruct((1, c), _F32)],
        scratch_shapes=[pltpu.VMEM((s + pad, cb), _F32), pltpu.VMEM((s + pad, cb), _F32),
                        pltpu.VMEM((kp * SUBLANES, cb), _F32), pltpu.VMEM((SUBLANES, cb), _F32)],
        compiler_params=_params(("parallel",)),
    )(*xs, w, b, dy)
    return list(res[:n_in]), res[n_in], res[n_in + 1]


def _tri_sum(v, lower):
    l = v.shape[0]
    r, c = lax.broadcasted_iota(jnp.int32, (l, l), 0), lax.broadcasted_iota(jnp.int32, (l, l), 1)
    tri = ((r >= c) if lower else (r <= c)).astype(jnp.bfloat16)
    hi = v.astype(jnp.bfloat16)
    r1 = v - hi.astype(_F32)
    mid = r1.astype(jnp.bfloat16)
    lo = (r1 - mid.astype(_F32)).astype(jnp.bfloat16)
    out = jnp.zeros_like(v)
    for part in (hi, mid, lo):
        out = out + lax.dot_general(tri, part, _DN["nn"], preferred_element_type=_F32)
    return out


@jax.custom_vjp
def _cumsum_rows(v):
    return _tri_sum(v, True)


_cumsum_rows.defvjp(lambda v: (_tri_sum(v, True), None), lambda _, ct: (_tri_sum(ct, False),))


def _ssd_chunk(xbc, dtraw, prev, bias, alog):
    l = xbc.shape[0]
    xs = xbc[:, :SSD_WIDTH]
    bm = xbc[:, SSD_WIDTH:SSD_WIDTH + 2 * SSD_STATE]
    cm = xbc[:, SSD_WIDTH + 2 * SSD_STATE:]
    v = dtraw + bias
    dt = jnp.maximum(v, 0.0) + jnp.log1p(jnp.exp(-jnp.abs(v)))
    a_neg = -jnp.exp(alog)
    acs = _cumsum_rows(dt * a_neg)
    acs_t = acs.T
    total = acs[l - 1:l, :]
    row = lax.broadcasted_iota(jnp.int32, (l, l), 0)
    colv = lax.broadcasted_iota(jnp.int32, (l, l), 1)
    causal = row >= colv
    lane_lo = lax.broadcasted_iota(jnp.int32, (l, LANES), 1) < HEAD_DIM
    row_lo = lax.broadcasted_iota(jnp.int32, (LANES, SSD_STATE), 0) < HEAD_DIM

    def pair_lanes(m, h0):
        return jnp.where(lane_lo, m[:, h0:h0 + 1], m[:, h0 + 1:h0 + 2])

    ys, news = [], []
    cb = {}
    for j in range(SSD_HEADS // 2):
        h0 = 2 * j
        grp = h0 // (SSD_HEADS // 2)
        bg = bm[:, grp * SSD_STATE:(grp + 1) * SSD_STATE]
        cg = cm[:, grp * SSD_STATE:(grp + 1) * SSD_STATE]
        if grp not in cb:
            cb[grp] = _dot1(cg, bg, "nt")
        xdt = xs[:, j * LANES:(j + 1) * LANES] * pair_lanes(dt, h0)
        y = jnp.zeros((l, LANES), _F32)
        for hh, mask in ((h0, lane_lo), (h0 + 1, jnp.logical_not(lane_lo))):
            seg = acs[:, hh:hh + 1] - acs_t[hh:hh + 1, :]
            dec = jnp.exp(jnp.where(causal, seg, -jnp.inf))
            y = y + _dot1(cb[grp] * dec, jnp.where(mask, xdt, 0.0), "nn")
        acs_p = pair_lanes(acs, h0)
        prev_p = prev[j * LANES:(j + 1) * LANES, :]
        y = y + _dot1(cg, prev_p, "nt") * jnp.exp(acs_p)
        wgt = jnp.exp(pair_lanes(jnp.broadcast_to(total, (l, LANES)), h0) - acs_p)
        st = _dot1(xdt * wgt, bg, "tn")
        cdec = jnp.exp(jnp.where(row_lo, total[:, h0:h0 + 1], total[:, h0 + 1:h0 + 2]))
        news.append(prev_p * cdec + st)
        ys.append(y)
    return jnp.concatenate(ys, axis=1), jnp.concatenate(news, axis=0)


def _ssd_fwd(xbc, dtraw, bias, alog, *, name):
    s = xbc.shape[0]
    nc = s // CHUNK
    nstate = SSD_HEADS * HEAD_DIM

    def body(x_ref, dt_ref, b_ref, a_ref, y_ref, st_ref, state_ref):
        @pl.when(pl.program_id(0) == 0)
        def _():
            state_ref[...] = jnp.zeros_like(state_ref)

        prev = state_ref[...]
        st_ref[...] = prev
        y, new = _ssd_chunk(x_ref[...], dt_ref[...], prev, b_ref[...], a_ref[...])
        y_ref[...] = y
        state_ref[...] = new

    small = pl.BlockSpec((1, LANES), lambda i: (0, 0))
    return pl.pallas_call(
        body, name=name, grid=(nc,),
        in_specs=[pl.BlockSpec((CHUNK, XBC_WIDTH), lambda i: (i, 0)), pl.BlockSpec((CHUNK, LANES), lambda i: (i, 0)),
                  small, small],
        out_specs=[pl.BlockSpec((CHUNK, SSD_WIDTH), lambda i: (i, 0)),
                   pl.BlockSpec((None, nstate, SSD_STATE), lambda i: (i, 0, 0))],
        out_shape=[jax.ShapeDtypeStruct((s, SSD_WIDTH), _F32), jax.ShapeDtypeStruct((nc, nstate, SSD_STATE), _F32)],
        scratch_shapes=[pltpu.VMEM((nstate, SSD_STATE), _F32)],
        compiler_params=_params(("arbitrary",)),
    )(xbc, dtraw, bias, alog)


def _ssd_bwd(xbc, dtraw, states, bias, alog, dy, dxs_extra, *, name):
    s = xbc.shape[0]
    nc = s // CHUNK
    nstate = SSD_HEADS * HEAD_DIM

    def body(x_ref, dt_ref, st_ref, b_ref, a_ref, dy_ref, ex_ref, dx_ref, ddt_ref, db_ref, da_ref, dstate_ref):
        @pl.when(pl.program_id(0) == 0)
        def _():
            dstate_ref[...] = jnp.zeros_like(dstate_ref)
            db_ref[...] = jnp.zeros_like(db_ref)
            da_ref[...] = jnp.zeros_like(da_ref)

        _, vjp = jax.vjp(_ssd_chunk, x_ref[...], dt_ref[...], st_ref[...], b_ref[...], a_ref[...])
        dx, ddt, dprev, db, da = vjp((dy_ref[...], dstate_ref[...]))
        dx_ref[:, :SSD_WIDTH] = dx[:, :SSD_WIDTH] + ex_ref[...]
        dx_ref[:, SSD_WIDTH:] = dx[:, SSD_WIDTH:]
        ddt_ref[...] = ddt
        db_ref[...] += db
        da_ref[...] += da
        dstate_ref[...] = dprev

    rev = lambda i: (nc - 1 - i, 0)
    small = pl.BlockSpec((1, LANES), lambda i: (0, 0))
    return pl.pallas_call(
        body, name=name, grid=(nc,),
        in_specs=[pl.BlockSpec((CHUNK, XBC_WIDTH), rev), pl.BlockSpec((CHUNK, LANES), rev),
                  pl.BlockSpec((None, nstate, SSD_STATE), lambda i: (nc - 1 - i, 0, 0)), small, small,
                  pl.BlockSpec((CHUNK, SSD_WIDTH), rev), pl.BlockSpec((CHUNK, SSD_WIDTH), rev)],
        out_specs=[pl.BlockSpec((CHUNK, XBC_WIDTH), rev), pl.BlockSpec((CHUNK, LANES), rev), small, small],
        out_shape=[jax.ShapeDtypeStruct((s, XBC_WIDTH), _F32), jax.ShapeDtypeStruct((s, LANES), _F32),
                   jax.ShapeDtypeStruct((1, LANES), _F32), jax.ShapeDtypeStruct((1, LANES), _F32)],
        scratch_shapes=[pltpu.VMEM((nstate, SSD_STATE), _F32)],
        compiler_params=_params(("arbitrary",)),
    )(xbc, dtraw, states, bias, alog, dy, dxs_extra)


def _pad_cols(a, width):
    return jnp.pad(a, ((0, 0), (0, width - a.shape[1])))


def _pad_rows(a, rows):
    return jnp.pad(a, ((0, rows - a.shape[0]), (0, 0)))


def _tie(a, token):
    return a + token[0:1, 0:1].astype(a.dtype)


def _local_step(x, mem, target, w, fetch, emit):
    bf = _MXU
    d = D_MODEL
    h = _row_fwd(_f_rms, [x], [w['norm_mix_g']], [(d, bf)], name="f_norm_mix")
    w_in = fetch('in', h)['w_in']
    z_end, xbc_end, dt_end = SSD_WIDTH, SSD_WIDTH + XBC_WIDTH, SSD_WIDTH + XBC_WIDTH + SSD_HEADS
    w_z, w_xbc = w_in[:z_end], w_in[z_end:xbc_end]
    w_dt = _pad_rows(w_in[xbc_end:dt_end], LANES)
    w_a, w_g = w_in[dt_end:dt_end + CF_WIDTH], w_in[dt_end + CF_WIDTH:]
    dt_bias = _pad_cols(w['ssd_dt_bias'], LANES)
    a_log = _pad_cols(w['ssd_A_log'], LANES)
    d_exp = jnp.repeat(w['ssd_D'], HEAD_DIM, axis=1)
    g_final = w['norm_final_g'].reshape(1, D_MODEL)

    z, xbc, dtr, ga, gg = _mm_fan_out(h, [w_z, w_xbc, w_dt, w_a, w_g], tb=True, out_dtypes=[bf, _F32, _F32, bf, bf],
                                      name="f_in")
    wc = fetch('conv', xbc)
    ssd_w = _pad_rows(wc['ssd_conv_w'], SUBLANES)
    cf_w = _pad_rows(wc['cf_conv_w'], 32)
    xbc_a = _conv_fwd([xbc], ssd_w, w['ssd_conv_b'], SSD_CONV, act=True, name="f_ssd_conv")
    y_ssd, states = _ssd_fwd(xbc_a, dtr, dt_bias, a_log, name="f_ssd")
    xs_win = (xbc_a, SSD_WIDTH, 0)
    y_n = _row_fwd(_f_gate, [y_ssd, xs_win, z], [d_exp, w['ssd_norm_g']], [(d, bf)], name="f_ssd_gate")
    u_c = _conv_fwd([ga, gg], cf_w, w['cf_conv_b'], CF_CONV, glu=True, name="f_cf_conv")
    u = _row_fwd(_f_ln, [u_c], [w['cf_ln_g'], w['cf_ln_b']], [(d, bf)], name="f_cf_ln")
    wm = fetch('mid', y_n)
    w_out_y, w_out_u = wm['w_out'][:SSD_WIDTH], wm['w_out'][SSD_WIDTH:]
    x1, hq = _mm_fan_in([(y_n, w_out_y), (u, w_out_u)], add=x, out_dtype=bf, epilogue=(_f_rms, w['norm_xattn_g'], bf),
                        name="f_out")
    q = _mm(hq, wm['w_q'], out_dtype=bf, name="f_q")
    memn = _row_fwd(_f_rms, [mem], [w['norm_mem_g']], [(d, bf)], name="f_norm_mem")
    kv = _mm(memn, wm['w_kv'], name="f_kv")
    k_mat, v_mat = kv[:, :d], kv[:, d:]
    o = _row_fwd(_f_att, [q], [k_mat, v_mat], [(d, bf)], name="f_att")
    x2, hf = _mm_fan_in([(o, wm['w_o'])], add=x1, out_dtype=bf, epilogue=(_f_rms, w['norm_ffn_g'], bf), name="f_o")
    wf = fetch('ffn', hf)
    gate, up, act = _mm_fan_out(hf, [wf['w_gate'], wf['w_up']], tb=True, out_dtypes=[bf, bf], epilogue=_f_swiglu,
                                extra_outs=[(D_FF, bf)], tm=256, name="f_ffn_in")
    x3 = _mm(act, wf['w_down'], add=x2, out_dtype=bf, name="f_down")

    dx3, dg_final, loss = _loss_bwd(x3, target, g_final, dx_dtype=bf, name="b_loss")
    g = {'norm_final_g': dg_final.reshape(d)}

    dact = _mm(dx3, wf['w_down'], tb=True, out_dtype=bf, name="b_down_x")
    dw_down = _mm(act, dx3, ta=True, out_dtype=bf, name="b_down_w")
    def swiglu_bwd(gate_t, up_t, dact_t):
        return jax.vjp(_f_swiglu, gate_t, up_t)[1](dact_t)

    dhf, dgate, dup = _mm_fan_in([(None, wf['w_gate']), (None, wf['w_up'])], prologue=swiglu_bwd, pro_ins=[gate, up, dact],
                                 pro_out_dtypes=[bf, bf], out_dtype=bf, tm=256, name="b_ffn_in_x")
    sent = emit({'w_down': dw_down, 'w_gate': _mm(dgate, hf, ta=True, out_dtype=bf, name="b_gate_w"),
                 'w_up': _mm(dup, hf, ta=True, out_dtype=bf, name="b_up_w")})
    (dx2,), (g['norm_ffn_g'],) = _row_bwd(_f_rms, [x2], [_tie(w['norm_ffn_g'], sent)], [dhf], adds={0: dx3}, row_dtypes=[bf], name="b_norm_ffn")

    do = _mm(dx2, wm['w_o'], tb=True, out_dtype=bf, name="b_o_x")
    dw_o = _mm(o, dx2, ta=True, out_dtype=bf, name="b_o_w")
    (dq,), (dk, dv) = _row_bwd(_f_att, [q], [k_mat, v_mat], [do], row_dtypes=[bf], name="b_att")
    dw_q = _mm(hq, dq, ta=True, out_dtype=bf, name="b_q_w")
    dhq = _mm(dq, wm['w_q'], tb=True, out_dtype=bf, name="b_q_x")
    (dx1,), (g['norm_xattn_g'],) = _row_bwd(_f_rms, [x1], [w['norm_xattn_g']], [dhq], adds={0: dx2}, row_dtypes=[bf], name="b_norm_xattn")
    dkv = jnp.concatenate([dk, dv], axis=1)
    dmemn = _mm(dkv, wm['w_kv'], tb=True, name="b_kv_x")
    _, (g['norm_mem_g'],) = _row_bwd(_f_rms, [mem], [w['norm_mem_g']], [dmemn], need=[False], name="b_norm_mem")
    sent = emit({'w_o': dw_o, 'w_q': dw_q, 'w_kv': _mm(memn, dkv, ta=True, out_dtype=bf, name="b_kv_w")},
                after=g['norm_mem_g'])

    dyn, du = _mm_fan_out(dx1, [w_out_y, w_out_u], tb=True, out_dtypes=[bf, bf], name="b_out_x")
    (du_c,), (g['cf_ln_g'], g['cf_ln_b']) = _row_bwd(_f_ln, [u_c], [_tie(w['cf_ln_g'], sent), w['cf_ln_b']], [du], name="b_cf_ln")
    sent = emit({'w_out': jnp.concatenate([_mm(y_n, dx1, ta=True, out_dtype=bf, name="b_out_y_w"), _mm(u, dx1, ta=True, out_dtype=bf, name="b_out_u_w")], axis=0)})
    (dga, dgg), dcf_w, g['cf_conv_b'] = _conv_bwd([ga, gg], cf_w, w['cf_conv_b'], du_c, CF_CONV, glu=True, name="b_cf_conv")
    g['cf_conv_w'] = dcf_w[:CF_CONV]
    (dy_ssd, dxs, dz), (dd_exp, g['ssd_norm_g']) = _row_bwd(
        _f_gate, [y_ssd, xs_win, z], [d_exp, _tie(w['ssd_norm_g'], sent)], [dyn], row_dtypes=[_F32, _F32, bf], name="b_ssd_gate")
    g['ssd_D'] = jnp.sum(dd_exp.reshape(SSD_HEADS, HEAD_DIM), axis=1).reshape(1, SSD_HEADS)
    dxbc_a, ddtr, ddt_bias, da_log = _ssd_bwd(xbc_a, dtr, states, dt_bias, a_log, dy_ssd, dxs, name="b_ssd")
    g['ssd_dt_bias'] = ddt_bias[:, :SSD_HEADS]
    g['ssd_A_log'] = da_log[:, :SSD_HEADS]
    (dxbc,), dssd_w, g['ssd_conv_b'] = _conv_bwd([xbc], ssd_w, w['ssd_conv_b'], dxbc_a, SSD_CONV, act=True, name="b_ssd_conv")
    g['ssd_conv_w'] = dssd_w[:SSD_CONV]

    sent = emit({'w_in': jnp.concatenate([
        _mm(dz, h, ta=True, out_dtype=bf, name="b_in_z_w"), _mm(dxbc, h, ta=True, out_dtype=bf, name="b_in_xbc_w"),
        _mm(ddtr, h, ta=True, out_dtype=bf, name="b_in_dt_w")[:SSD_HEADS],
        _mm(dga, h, ta=True, out_dtype=bf, name="b_in_a_w"), _mm(dgg, h, ta=True, out_dtype=bf, name="b_in_g_w")], axis=0)})
    dh = _mm_fan_in([(dz, w_z), (dxbc, w_xbc), (ddtr, _tie(w_dt, sent)), (dga, w_a), (dgg, w_g)], out_dtype=bf,
                    name="b_in_x")
    (dx,), (g['norm_mix_g'],) = _row_bwd(_f_rms, [x], [w['norm_mix_g']], [dh], adds={0: dx1}, name="b_norm_mix")
    return loss, dx, g


_ANY = pl.BlockSpec(memory_space=pl.ANY)


def _place():
    x, y, c = lax.axis_index("x"), lax.axis_index("y"), lax.axis_index("c")
    return x, y, c


def _all_gather(arrs, *, name):
    n = len(arrs)

    def body(*refs):
        ins, outs = refs[:n], refs[n:2 * n]
        send_sems, recv_sems, local_sems = refs[2 * n:]
        x, y, c = _place()
        me, sibling = (x, y, c), (x, y, 1 - c)
        chips = [(1 - x, y), (x, 1 - y), (1 - x, 1 - y)]

        def slot(a, dev):
            return outs[a].at[4 * dev[0] + 2 * dev[1] + dev[2]]

        def copy(a, k, block, to, src=None):
            return pltpu.make_async_remote_copy(
                src_ref=slot(a, block) if src is None else src, dst_ref=slot(a, block),
                send_sem=send_sems.at[a, k], recv_sem=recv_sems.at[a, k], device_id=to, device_id_type=MESH)

        mine = [pltpu.make_async_copy(ins[a], slot(a, me), local_sems.at[a]) for a in range(n)]
        for cp in mine:
            cp.start()
        first = []
        for a in range(n):
            first.append(copy(a, 0, me, sibling, src=ins[a]))
            first += [copy(a, 1 + j, me, (*chip, c), src=ins[a]) for j, chip in enumerate(chips)]
        for cp in first:
            cp.start()
        passed = []
        for a in range(n):
            for j, chip in enumerate(chips):
                copy(a, 1 + j, (*chip, c), me).wait_recv()
                fwd = copy(a, 4 + j, (*chip, c), sibling)
                fwd.start()
                passed.append(fwd)
        for a in range(n):
            copy(a, 0, sibling, me).wait_recv()
            for j, chip in enumerate(chips):
                copy(a, 4 + j, (*chip, 1 - c), me).wait_recv()
        for cp in first + passed:
            cp.wait_send()
        for cp in mine:
            cp.wait()

    return pl.pallas_call(
        body, name=name, in_specs=[_ANY] * n, out_specs=[_ANY] * n,
        out_shape=[jax.ShapeDtypeStruct((N_DEV,) + a.shape, a.dtype) for a in arrs],
        scratch_shapes=[pltpu.SemaphoreType.DMA((n, 7)), pltpu.SemaphoreType.DMA((n, 7)), pltpu.SemaphoreType.DMA((n,))],
    )(*arrs)


_HBM = pl.BlockSpec(memory_space=pltpu.HBM)
_SEM = pl.BlockSpec(memory_space=pltpu.SEMAPHORE)
_EFFECT = pltpu.SideEffectType.DATAFLOW_SIDE_EFFECTING
_FLIPS = [(dx, dy, dc) for dx in (0, 1) for dy in (0, 1) for dc in (0, 1)][1:]


def _peer(flip, x, y, c):
    return (1 - x if flip[0] else x, 1 - y if flip[1] else y, 1 - c if flip[2] else c)


def _send_start(srcs, blocked, *, after=None, name):
    n = len(srcs)
    lands = [jax.ShapeDtypeStruct(s.shape if blocked else (N_DEV,) + s.shape, s.dtype) for s in srcs]
    n_in = 2 * n + (after is not None)

    def body(*refs):
        src_refs, land_refs = refs[:n], refs[n:2 * n]
        send_sems, recv_sems = refs[n_in], refs[n_in + 1]
        token = refs[-1]
        x, y, c = _place()
        me = 4 * x + 2 * y + c
        for a in range(n):
            for k, flip in enumerate(_FLIPS):
                p = _peer(flip, x, y, c)
                src = src_refs[a].at[4 * p[0] + 2 * p[1] + p[2]] if blocked else src_refs[a]
                pltpu.make_async_remote_copy(
                    src_ref=src, dst_ref=land_refs[a].at[me], send_sem=send_sems.at[7 * a + k], recv_sem=recv_sems.at[7 * a + k],
                    device_id=p, device_id_type=MESH).start()
        token[...] = jnp.zeros_like(token)

    res = pl.pallas_call(
        body, name=name,
        out_shape=(pltpu.SemaphoreType.DMA((7 * n,)), pltpu.SemaphoreType.DMA((7 * n,)),
                   *[pltpu.HBM(s.shape, s.dtype) for s in srcs], *[pltpu.HBM(l.shape, l.dtype) for l in lands],
                   jax.ShapeDtypeStruct((SUBLANES, LANES), _F32)),
        in_specs=[_HBM] * (2 * n) + [_ANY] * (after is not None),
        out_specs=(_SEM, _SEM, *[_HBM] * (2 * n), pl.BlockSpec(memory_space=pltpu.VMEM)),
        input_output_aliases={i: 2 + i for i in range(2 * n)},
        compiler_params=pltpu.CompilerParams(has_side_effects=_EFFECT),
    )(*[pltpu.with_memory_space_constraint(s, pltpu.HBM) for s in srcs],
      *[pltpu.with_memory_space_constraint(lax.empty(l.shape, l.dtype), pltpu.HBM) for l in lands],
      *([after] if after is not None else []))
    return res[0], res[1], list(res[2:2 + n]), list(res[2 + n:2 + 2 * n]), res[-1]


def _send_wait(handles, after, blocked, *, name):
    send_sems, recv_sems, srcs, lands, _ = handles
    n = len(srcs)

    def body(*refs):
        src_refs, land_refs = refs[:n], refs[n:2 * n]
        send_sems, recv_sems = refs[2 * n], refs[2 * n + 1]
        x, y, c = _place()
        for a in range(n):
            for k, flip in enumerate(_FLIPS):
                p = _peer(flip, x, y, c)
                pid = 4 * p[0] + 2 * p[1] + p[2]
                cp = pltpu.make_async_remote_copy(
                    src_ref=src_refs[a].at[pid] if blocked else src_refs[a], dst_ref=land_refs[a].at[pid],
                    send_sem=send_sems.at[7 * a + k], recv_sem=recv_sems.at[7 * a + k], device_id=p, device_id_type=MESH)
                cp.wait_send()
                cp.wait_recv()

    res = pl.pallas_call(
        body, name=name,
        out_shape=tuple(pltpu.HBM(s.shape, s.dtype) for s in srcs + lands),
        in_specs=[_HBM] * (2 * n) + [_SEM, _SEM, _ANY], out_specs=tuple([_HBM] * (2 * n)),
        input_output_aliases={i: i for i in range(2 * n)},
        compiler_params=pltpu.CompilerParams(has_side_effects=_EFFECT),
    )(*srcs, *lands, send_sems, recv_sems, after)
    return list(res[:n]), list(res[n:])


def _adamw(parts, w, m, v, *, own=None, me=None, name):
    p, r, c = parts.shape
    tr = _pick(r, (256, 176, 128, 64, 32, 16, 8))
    if own is not None:
        tc = c if tr < r else _pick(c, (256, 128))
        return _adamw_own(parts, own, me, w, m, v, tr, tc, name=name)

    def body(p_ref, w_ref, m_ref, v_ref, g_ref, d_ref, nm_ref, nv_ref):
        g = p_ref[0].astype(_F32)
        for i in range(1, p):
            g = g + p_ref[i].astype(_F32)
        _adamw_math(g, w_ref, m_ref, v_ref, g_ref, d_ref, nm_ref, nv_ref)

    blk = pl.BlockSpec((tr, c), lambda i: (i, 0))
    return pl.pallas_call(
        body, name=name, grid=(r // tr,),
        in_specs=[pl.BlockSpec((p, tr, c), lambda i: (0, i, 0)), blk, blk, blk], out_specs=[blk] * 4,
        out_shape=[jax.ShapeDtypeStruct((r, c), _F32)] * 4,
        compiler_params=_params(("parallel",)),
    )(parts, w, m, v)


def _adamw_math(g, w_ref, m_ref, v_ref, g_ref, d_ref, nm_ref, nv_ref):
    wv = w_ref[...]
    mn = ADAM_B1 * m_ref[...] + (1.0 - ADAM_B1) * g
    vn = ADAM_B2 * v_ref[...] + (1.0 - ADAM_B2) * jnp.square(g)
    m_hat = mn / (1.0 - ADAM_B1 ** ADAM_STEP)
    v_hat = vn / (1.0 - ADAM_B2 ** ADAM_STEP)
    g_ref[...] = g
    d_ref[...] = -ADAM_LR * (m_hat / (jnp.sqrt(v_hat) + ADAM_EPS) + ADAM_WD * wv)
    nm_ref[...] = mn
    nv_ref[...] = vn


def _adamw_own(parts, own, me, w, m, v, tr, tc, *, name):
    p, r, c = parts.shape

    def body(me_ref, p_ref, own_ref, w_ref, m_ref, v_ref, g_ref, d_ref, nm_ref, nv_ref):
        mine = own_ref[...].astype(_F32)
        g = jnp.where(me_ref[0] == 0, mine, p_ref[0].astype(_F32))
        for i in range(1, p):
            g = g + jnp.where(me_ref[0] == i, mine, p_ref[i].astype(_F32))
        _adamw_math(g, w_ref, m_ref, v_ref, g_ref, d_ref, nm_ref, nv_ref)

    blk = pl.BlockSpec((tr, tc), lambda i, j, me_ref: (i, j))
    grid_spec = pltpu.PrefetchScalarGridSpec(
        num_scalar_prefetch=1, grid=(r // tr, c // tc),
        in_specs=[pl.BlockSpec((p, tr, tc), lambda i, j, me_ref: (0, i, j)),
                  pl.BlockSpec((None, tr, tc), lambda i, j, me_ref: (me_ref[0], i, j)), blk, blk, blk],
        out_specs=[blk] * 4)
    return pl.pallas_call(
        body, name=name, grid_spec=grid_spec, out_shape=[jax.ShapeDtypeStruct((r, c), _F32)] * 4,
        compiler_params=_params(("parallel", "parallel")),
    )(me.reshape(1).astype(jnp.int32), parts, own, w, m, v)


def _adamw_rows(g_row, offsets, ws, ms, vs, *, name):
    k = len(ws)

    def body(*refs):
        g_ref, w_refs, m_refs, v_refs = refs[0], refs[1:1 + k], refs[1 + k:1 + 2 * k], refs[1 + 2 * k:1 + 3 * k]
        outs = refs[1 + 3 * k:]
        for i in range(k):
            gi = g_ref[:, offsets[i]:offsets[i] + ws[i].shape[1]]
            _adamw_math(gi, w_refs[i], m_refs[i], v_refs[i], *outs[4 * i:4 * i + 4])

    return pl.pallas_call(
        body, name=name, out_shape=[jax.ShapeDtypeStruct(w.shape, _F32) for w in ws for _ in range(4)],
    )(g_row, *ws, *ms, *vs)


def _sum_parts(parts, *, name):
    p, r, c = parts.shape

    def body(p_ref, o_ref):
        g = p_ref[0].astype(_F32)
        for i in range(1, p):
            g = g + p_ref[i].astype(_F32)
        o_ref[...] = g

    return pl.pallas_call(body, name=name, out_shape=jax.ShapeDtypeStruct((r, c), _F32))(parts)


def _pack(vals, rows):
    flat = jnp.concatenate([v.reshape(-1) for v in vals])
    return jnp.pad(flat, (0, rows * LANES - flat.shape[0])).reshape(rows, LANES)


def _unpack(packed, shapes):
    flat = packed.reshape(-1)
    out, pos = [], 0
    for shp in shapes:
        size = math.prod(shp)
        out.append(flat[pos:pos + size].reshape(shp))
        pos += size
    return out


def _pack_rows(shapes):
    total = sum(math.prod(s) for s in shapes)
    return -(-total // (LANES * SUBLANES)) * SUBLANES


def kernel(x, mem, norm_mix_g, w_in, ssd_conv_w, ssd_conv_b, ssd_dt_bias, ssd_A_log, ssd_D, ssd_norm_g, cf_conv_w, cf_conv_b, cf_ln_g, cf_ln_b, w_out, norm_xattn_g, norm_mem_g, w_q, w_kv, w_o, norm_ffn_g, w_gate, w_up, w_down, norm_final_g, loss_target, m_norm_mix_g, m_w_in, m_ssd_conv_w, m_ssd_conv_b, m_ssd_dt_bias, m_ssd_A_log, m_ssd_D, m_ssd_norm_g, m_cf_conv_w, m_cf_conv_b, m_cf_ln_g, m_cf_ln_b, m_w_out, m_norm_xattn_g, m_norm_mem_g, m_w_q, m_w_kv, m_w_o, m_norm_ffn_g, m_w_gate, m_w_up, m_w_down, m_norm_final_g, v_norm_mix_g, v_w_in, v_ssd_conv_w, v_ssd_conv_b, v_ssd_dt_bias, v_ssd_A_log, v_ssd_D, v_ssd_norm_g, v_cf_conv_w, v_cf_conv_b, v_cf_ln_g, v_cf_ln_b, v_w_out, v_norm_xattn_g, v_norm_mem_g, v_w_q, v_w_kv, v_w_o, v_norm_ffn_g, v_w_gate, v_w_up, v_w_down, v_norm_final_g):
    args = dict(locals())
    wts = {n: args[n] for n in WEIGHT_NAMES}
    mom = {n: args["m_" + n] for n in WEIGHT_NAMES}
    var = {n: args["v_" + n] for n in WEIGHT_NAMES}
    me = 4 * lax.axis_index("x") + 2 * lax.axis_index("y") + lax.axis_index("c")

    groups = {'in': ['w_in'], 'conv': ['ssd_conv_w', 'cf_conv_w'], 'mid': ['w_out', 'w_q', 'w_kv', 'w_o'],
              'ffn': ['w_gate', 'w_up', 'w_down']}
    def shard(n, a):
        return jnp.transpose(a[0], (1, 0)) if n in TRANSPOSED else a[0]

    gathers, started = {}, None
    for grp, names in groups.items():
        shards = [wts[n][0] if grp == 'conv' else shard(n, wts[n]).astype(_MXU) for n in names]
        gathers[grp] = _send_start(shards, False, after=started, name="gather_%s_start" % grp)
        started = gathers[grp][4]

    def fetch(grp, after):
        srcs, lands = _send_wait(gathers[grp], started if after is None else after, False, name="gather_%s_wait" % grp)
        out = {}
        for n, own, gth in zip(groups[grp], srcs, lands):
            gth = lax.dynamic_update_slice_in_dim(gth, own[None], me, axis=0)
            if n == 'w_kv' or grp == 'conv':
                out[n] = jnp.transpose(gth, (1, 0, 2)).reshape(gth.shape[1], N_DEV * gth.shape[2])
            else:
                out[n] = gth.reshape(N_DEV * gth.shape[1], gth.shape[2])
        return out

    exchanges = []

    def emit(grads, after=None):
        blocks = []
        for n, gw in grads.items():
            if n == 'w_kv':
                gw = jnp.transpose(gw.reshape(gw.shape[0], N_DEV, gw.shape[1] // N_DEV), (1, 0, 2))
            else:
                gw = gw.reshape(N_DEV, gw.shape[0] // N_DEV, gw.shape[1])
            blocks.append(gw.astype(jnp.bfloat16))
        first = next(iter(grads))
        exchanges.append((list(grads), _send_start(blocks, True, after=after, name="exchange_%s_start" % first), first))
        return exchanges[-1][1][4]

    full = {n: wts[n] for n in WEIGHT_NAMES if n not in BIG and n not in groups['conv']}
    full['norm_mix_g'] = _tie(norm_mix_g, started)

    loss_blk, grad_x, g = _local_step(x[0], mem[0], loss_target[0], full, fetch, emit)

    small = [n for n in WEIGHT_NAMES if n not in BIG]
    g['loss'] = loss_blk[0:1, 0:1]
    items = small + ['loss']
    size = {n: math.prod(g[n].shape) for n in items}
    seg = {n: -(-size[n] // LANES) * LANES for n in items}
    off, pos = {}, 0
    for n in items:
        off[n], pos = pos, pos + seg[n]
    rows = -(-pos // (LANES * SUBLANES)) * SUBLANES
    flat = jnp.concatenate([jnp.pad(g[n].reshape(-1), (0, seg[n] - size[n])) for n in items]
                           + [jnp.zeros((rows * LANES - pos,), _F32)])
    small_sent = _send_start([flat.reshape(rows, LANES)], False, name="gather_small_start")

    out_g, out_d, out_m, out_v = {}, {}, {}, {}
    for names, handles, first in exchanges:
        srcs, lands = _send_wait(handles, small_sent[4], True, name="exchange_%s_wait" % first)
        for n, own, parts in zip(names, srcs, lands):
            res = _adamw(parts, shard(n, wts[n]), shard(n, mom[n]), shard(n, var[n]), own=own, me=me, name="adamw_" + n)
            out_g[n], out_d[n], out_m[n], out_v[n] = [(jnp.transpose(r, (1, 0)) if n in TRANSPOSED else r)[None] for r in res]

    srcs, lands = _send_wait(small_sent, out_g[exchanges[-1][0][-1]], False, name="gather_small_wait")
    small_parts = lax.dynamic_update_slice_in_dim(lands[0], srcs[0][None], me, axis=0)
    g_row = _sum_parts(small_parts, name="sum_small_grads").reshape(1, rows * LANES)
    loss = g_row[0, off['loss']]
    rep = [n for n in small if n not in groups['conv']]
    as_row = lambda a: a.reshape(1, -1)
    res = _adamw_rows(g_row, [off[n] for n in rep], [as_row(wts[n]) for n in rep], [as_row(mom[n]) for n in rep],
                      [as_row(var[n]) for n in rep], name="adamw_small")
    for i, n in enumerate(rep):
        out_g[n], out_d[n], out_m[n], out_v[n] = [r.reshape(wts[n].shape) for r in res[4 * i:4 * i + 4]]
    for n in groups['conv']:
        k_taps, width = g[n].shape
        g_full = g_row[0, off[n]:off[n] + size[n]].reshape(k_taps, width)
        g_mine = lax.dynamic_slice_in_dim(g_full, me * (width // N_DEV), width // N_DEV, axis=1)
        res = _adamw(g_mine[None], wts[n][0], mom[n][0], var[n][0], name="adamw_" + n)
        out_g[n], out_d[n], out_m[n], out_v[n] = [r[None] for r in res]

    return (loss, grad_x[None], *[out_g[n] for n in WEIGHT_NAMES], *[out_d[n] for n in WEIGHT_NAMES],
            *[out_m[n] for n in WEIGHT_NAMES], *[out_v[n] for n in WEIGHT_NAMES])
```

```python
import functools
import math

import jax
import jax.numpy as jnp
from jax import lax
from jax.experimental import pallas as pl
from jax.experimental.pallas import tpu as pltpu

_F32 = jnp.float32
_MXU = jnp.bfloat16
_PREC = None
_VMEM_LIMIT = 56 * 1024 * 1024

D_MODEL = 1024
HEAD_DIM = 64
SSD_HEADS = 16
SSD_WIDTH = 1024
SSD_STATE = 128
SSD_CONV = 4
CHUNK = 128
XBC_WIDTH = 1536
CF_WIDTH = 1024
CF_CONV = 31
X_HEADS = 4
X_HEAD_DIM = 256
D_FF = 2816
EPS = 1e-6
N_DEV = 8
LANES = 128
SUBLANES = 8

ADAM_LR = 0.001
ADAM_B1 = 0.9
ADAM_B2 = 0.999
ADAM_EPS = 1e-08
ADAM_WD = 0.01
ADAM_STEP = 10

MESH = pl.DeviceIdType.MESH
WEIGHT_NAMES = ['norm_mix_g', 'w_in', 'ssd_conv_w', 'ssd_conv_b', 'ssd_dt_bias', 'ssd_A_log', 'ssd_D', 'ssd_norm_g',
                'cf_conv_w', 'cf_conv_b', 'cf_ln_g', 'cf_ln_b', 'w_out', 'norm_xattn_g', 'norm_mem_g', 'w_q', 'w_kv',
                'w_o', 'norm_ffn_g', 'w_gate', 'w_up', 'w_down', 'norm_final_g']
BIG = ['w_in', 'w_out', 'w_q', 'w_kv', 'w_o', 'w_gate', 'w_up', 'w_down']
TRANSPOSED = ('w_in', 'w_gate', 'w_up')


def _params(sem=None):
    return pltpu.CompilerParams(dimension_semantics=sem, vmem_limit_bytes=_VMEM_LIMIT)


def _pick(n, cands):
    for c in cands:
        if n % c == 0:
            return c
    return n


def _mm(a, b, *, ta=False, tb=False, add=None, out_dtype=_F32, name):
    (kdim, m) = a.shape if ta else a.shape[::-1]
    (n, k2) = b.shape if tb else b.shape[::-1]
    assert kdim == k2, (a.shape, b.shape, ta, tb)
    if ta:
        tm = m if m <= 1024 else _pick(m, (1408, 1024, 512, 256, 128))
        tn = n if n <= 1536 else _pick(n, (1408, 1024, 512, 256, 128))
        tk = _pick(kdim, (1024, 512, 256, 128))
    else:
        tm = _pick(m, (512, 256, 128))
        tn = n if n <= 2816 else _pick(n, (1408, 1024, 512, 256, 128))
        tk = kdim if kdim <= 2816 else _pick(kdim, (1408, 1024, 512, 256, 128))
    nk = kdim // tk
    dn = (((0 if ta else 1,), (1 if tb else 0,)), ((), ()))

    def body(*refs):
        a_ref, b_ref = refs[0], refs[1]
        add_ref = refs[2] if add is not None else None
        o_ref = refs[3 if add is not None else 2]
        acc_ref = refs[-1]
        k = pl.program_id(2)
        prod = lax.dot_general(a_ref[...].astype(_MXU), b_ref[...].astype(_MXU), dn,
                               preferred_element_type=_F32, precision=_PREC)

        def finish(r):
            if add_ref is not None:
                r = r + add_ref[...].astype(_F32)
            o_ref[...] = r.astype(o_ref.dtype)

        if nk == 1:
            finish(prod)
            return

        @pl.when(k == 0)
        def _():
            acc_ref[...] = prod

        @pl.when(jnp.logical_and(k > 0, k < nk - 1))
        def _():
            acc_ref[...] += prod

        @pl.when(k == nk - 1)
        def _():
            finish(acc_ref[...] + prod)

    a_spec = pl.BlockSpec((tk, tm), lambda i, j, k: (k, i)) if ta else pl.BlockSpec((tm, tk), lambda i, j, k: (i, k))
    b_spec = pl.BlockSpec((tn, tk), lambda i, j, k: (j, k)) if tb else pl.BlockSpec((tk, tn), lambda i, j, k: (k, j))
    o_spec = pl.BlockSpec((tm, tn), lambda i, j, k: (i, j))
    ins, specs = [a, b], [a_spec, b_spec]
    if add is not None:
        ins.append(add)
        specs.append(o_spec)
    return pl.pallas_call(
        body, name=name, grid=(m // tm, n // tn, nk), in_specs=specs, out_specs=o_spec,
        out_shape=jax.ShapeDtypeStruct((m, n), out_dtype),
        scratch_shapes=[pltpu.VMEM((tm, tn), _F32)] if nk > 1 else [],
        compiler_params=_params(("parallel", "parallel", "arbitrary")),
    )(*ins)


def _resident(shape):
    return pl.BlockSpec(shape, lambda i: (0,) * len(shape), pipeline_mode=pl.Buffered(1))


def _mm_fan_out(a, bs, *, tb, out_dtypes, epilogue=None, extra_outs=(), tm=512, name):
    m, kdim = a.shape
    tm = min(tm, m)
    ns = [b.shape[0] if tb else b.shape[1] for b in bs]
    nb = len(bs)
    kind = "nt" if tb else "nn"

    def body(*refs):
        a_ref, b_refs, o_refs = refs[0], refs[1:1 + nb], refs[1 + nb:]
        av = a_ref[...].astype(_MXU)
        prods = [lax.dot_general(av, b[...].astype(_MXU), _DN[kind], preferred_element_type=_F32, precision=_PREC)
                 for b in b_refs]
        for o_ref, p in zip(o_refs[:nb], prods):
            o_ref[...] = p.astype(o_ref.dtype)
        if epilogue is not None:
            for o_ref, v in zip(o_refs[nb:], _tup(epilogue(*prods))):
                o_ref[...] = v.astype(o_ref.dtype)

    widths = ns + [w for w, _ in extra_outs]
    dtypes = list(out_dtypes) + [dt for _, dt in extra_outs]
    return pl.pallas_call(
        body, name=name, grid=(m // tm,),
        in_specs=[pl.BlockSpec((tm, kdim), lambda i: (i, 0))] + [_resident(b.shape) for b in bs],
        out_specs=[pl.BlockSpec((tm, w), lambda i: (i, 0)) for w in widths],
        out_shape=[jax.ShapeDtypeStruct((m, w), dt) for w, dt in zip(widths, dtypes)],
        compiler_params=_params(("parallel",)),
    )(a, *bs)


def _mm_fan_in(pairs, *, add=None, out_dtype=_F32, prologue=None, pro_ins=(), pro_out_dtypes=(), epilogue=None,
               tm=512, name):
    bs = [b for _, b in pairs]
    nb = len(bs)
    n = bs[0].shape[1]
    rows_in = list(pro_ins) if prologue is not None else [a for a, _ in pairs]
    m = rows_in[0].shape[0]
    tm = min(tm, m)
    n_r = len(rows_in)

    def body(*refs):
        r_refs, b_refs = refs[:n_r], refs[n_r:n_r + nb]
        pos = n_r + nb
        add_ref = refs[pos] if add is not None else None
        pos += add is not None
        epi_ref = refs[pos] if epilogue is not None else None
        pos += epilogue is not None
        o_ref, po_refs = refs[pos], refs[pos + 1:]
        if prologue is not None:
            a_vals = _tup(prologue(*[r[...].astype(_F32) for r in r_refs]))
            for po, v in zip(po_refs, a_vals):
                po[...] = v.astype(po.dtype)
        else:
            a_vals = [r[...] for r in r_refs]
        acc = None
        for av, b in zip(a_vals, b_refs):
            p = lax.dot_general(av.astype(_MXU), b[...].astype(_MXU), _DN["nn"], preferred_element_type=_F32,
                                precision=_PREC)
            acc = p if acc is None else acc + p
        if add_ref is not None:
            acc = acc + add_ref[...].astype(_F32)
        o_ref[...] = acc.astype(o_ref.dtype)
        if epilogue is not None:
            po_refs[-1][...] = epilogue[0](acc, epi_ref[...]).astype(po_refs[-1].dtype)

    row = lambda w: pl.BlockSpec((tm, w), lambda i: (i, 0))
    ins = rows_in + bs + ([add] if add is not None else []) + ([epilogue[1]] if epilogue is not None else [])
    in_specs = ([row(r.shape[1]) for r in rows_in] + [_resident(b.shape) for b in bs]
                + ([row(n)] if add is not None else []) + ([_resident(epilogue[1].shape)] if epilogue is not None else []))
    extra = [(b.shape[0], dt) for b, dt in zip(bs, pro_out_dtypes)] if prologue is not None else []
    if epilogue is not None:
        extra.append((n, epilogue[2]))
    res = pl.pallas_call(
        body, name=name, grid=(m // tm,), in_specs=in_specs,
        out_specs=[row(n)] + [row(w) for w, _ in extra],
        out_shape=[jax.ShapeDtypeStruct((m, n), out_dtype)] + [jax.ShapeDtypeStruct((m, w), dt) for w, dt in extra],
        compiler_params=_params(("parallel",)),
    )(*ins)
    return res if extra else res[0]


def _row_spec(r, ts):
    if isinstance(r, tuple):
        arr, width, cblk = r
        return arr, pl.BlockSpec((ts, width), lambda i, cblk=cblk: (i, cblk))
    return r, pl.BlockSpec((ts, r.shape[1]), lambda i: (i, 0))


def _tup(v):
    return tuple(v) if isinstance(v, (tuple, list)) else (v,)


def _row_fwd(f, rows, params, outs, *, name, ts=256):
    arrs, specs = zip(*[_row_spec(r, ts) for r in rows])
    s = arrs[0].shape[0]
    ts = min(ts, s)
    n_r, n_p = len(rows), len(params)

    def body(*refs):
        rv = [r[...].astype(_F32) for r in refs[:n_r]]
        pv = [p[...] for p in refs[n_r:n_r + n_p]]
        res = _tup(f(*rv, *pv))
        for o_ref, v in zip(refs[n_r + n_p:], res):
            o_ref[...] = v.astype(o_ref.dtype)

    res = pl.pallas_call(
        body, name=name, grid=(s // ts,),
        in_specs=list(specs) + [pl.BlockSpec(p.shape, lambda i: (0, 0)) for p in params],
        out_specs=[pl.BlockSpec((ts, w), lambda i: (i, 0)) for w, _ in outs],
        out_shape=[jax.ShapeDtypeStruct((s, w), dt) for w, dt in outs],
        compiler_params=_params(("parallel",)),
    )(*arrs, *params)
    return res[0] if len(outs) == 1 else res


def _row_bwd(f, rows, params, cts, *, need=None, adds=None, row_dtypes=None, name, ts=256):
    arrs, specs = zip(*[_row_spec(r, ts) for r in rows])
    s = arrs[0].shape[0]
    ts = min(ts, s)
    n_r, n_p, n_c = len(rows), len(params), len(cts)
    need = [True] * n_r if need is None else need
    adds = {} if adds is None else adds
    add_keys = sorted(adds)
    row_dtypes = [_F32] * n_r if row_dtypes is None else row_dtypes
    needed = [j for j in range(n_r) if need[j]]
    widths = [specs[j].block_shape[1] for j in range(n_r)]

    def body(*refs):
        pos = 0
        r_refs = refs[pos:pos + n_r]; pos += n_r
        p_refs = refs[pos:pos + n_p]; pos += n_p
        c_refs = refs[pos:pos + n_c]; pos += n_c
        a_refs = refs[pos:pos + len(add_keys)]; pos += len(add_keys)
        dr_refs = refs[pos:pos + len(needed)]; pos += len(needed)
        dp_refs = refs[pos:pos + n_p]
        rv = [r[...].astype(_F32) for r in r_refs]
        pv = [p[...] for p in p_refs]
        _, vjp = jax.vjp(lambda *a: _tup(f(*a)), *rv, *pv)
        g = vjp(tuple(c[...].astype(_F32) for c in c_refs))
        for o_ref, j in zip(dr_refs, needed):
            v = g[j]
            if j in adds:
                v = v + a_refs[add_keys.index(j)][...].astype(_F32)
            o_ref[...] = v.astype(o_ref.dtype)
        if n_p:
            @pl.when(pl.program_id(0) == 0)
            def _():
                for dp in dp_refs:
                    dp[...] = jnp.zeros_like(dp)
            for dp, v in zip(dp_refs, g[n_r:]):
                dp[...] += v

    ct_specs = [pl.BlockSpec((ts, c.shape[1]), lambda i: (i, 0)) for c in cts]
    add_specs = [pl.BlockSpec((ts, adds[j].shape[1]), lambda i: (i, 0)) for j in add_keys]
    res = pl.pallas_call(
        body, name=name, grid=(s // ts,),
        in_specs=list(specs) + [pl.BlockSpec(p.shape, lambda i: (0, 0)) for p in params] + ct_specs + add_specs,
        out_specs=[pl.BlockSpec((ts, widths[j]), lambda i: (i, 0)) for j in needed]
        + [pl.BlockSpec(p.shape, lambda i: (0, 0)) for p in params],
        out_shape=[jax.ShapeDtypeStruct((s, widths[j]), row_dtypes[j]) for j in needed]
        + [jax.ShapeDtypeStruct(p.shape, _F32) for p in params],
        compiler_params=_params(("arbitrary",)),
    )(*arrs, *params, *cts, *[adds[j] for j in add_keys])
    return list(res[:len(needed)]), list(res[len(needed):])


_DN = {"nn": (((1,), (0,)), ((), ())), "nt": (((1,), (1,)), ((), ())), "tn": (((0,), (0,)), ((), ()))}


def _make_dot(passes):
    def raw(a, b, kind):
        dn = _DN[kind]
        if passes == 1 or _MXU == _F32:
            return lax.dot_general(a.astype(_MXU), b.astype(_MXU), dn, preferred_element_type=_F32, precision=_PREC)
        a_hi, b_hi = a.astype(_MXU), b.astype(_MXU)
        a_lo = (a - a_hi.astype(_F32)).astype(_MXU)
        b_lo = (b - b_hi.astype(_F32)).astype(_MXU)
        out = lax.dot_general(a_hi, b_hi, dn, preferred_element_type=_F32)
        out = out + lax.dot_general(a_lo, b_hi, dn, preferred_element_type=_F32)
        return out + lax.dot_general(a_hi, b_lo, dn, preferred_element_type=_F32)

    @functools.partial(jax.custom_vjp, nondiff_argnums=(2,))
    def dot(a, b, kind):
        return raw(a, b, kind)

    def fwd(a, b, kind):
        return raw(a, b, kind), (a, b)

    def bwd(kind, res, ct):
        a, b = res
        if kind == "nn":
            return raw(ct, b, "nt"), raw(a, ct, "tn")
        if kind == "nt":
            return raw(ct, b, "nn"), raw(ct, a, "tn")
        return raw(b, ct, "nt"), raw(a, ct, "nn")

    dot.defvjp(fwd, bwd)
    return dot


_dot1 = _make_dot(1)
_dot3 = _make_dot(3)


def _sig(v):
    return 1.0 / (1.0 + jnp.exp(-v))


def _silu(v):
    return v * _sig(v)


def _f_rms(x, g):
    return x * lax.rsqrt(jnp.mean(x * x, axis=-1, keepdims=True) + EPS) * g


def _f_gate(y, xs, z, dexp, g):
    v = (y + dexp * xs) * _silu(z)
    half = SSD_WIDTH // 2
    parts = []
    for grp in range(2):
        vg = v[:, grp * half:(grp + 1) * half]
        parts.append(vg * lax.rsqrt(jnp.mean(vg * vg, axis=-1, keepdims=True) + EPS) * g[:, grp * half:(grp + 1) * half])
    return jnp.concatenate(parts, axis=1)


def _f_ln(u, g, b):
    mu = jnp.mean(u, axis=-1, keepdims=True)
    var = jnp.mean(jnp.square(u - mu), axis=-1, keepdims=True)
    return _silu((u - mu) * lax.rsqrt(var + EPS) * g + b)


def _f_glu(a, g):
    return a * _sig(g)


def _f_swiglu(gate, up):
    return _silu(gate) * up


def _f_att(q, k, v):
    outs = []
    for h in range(X_HEADS):
        sl = slice(h * X_HEAD_DIM, (h + 1) * X_HEAD_DIM)
        s = _dot1(q[:, sl], k[:, sl], "nt") * (X_HEAD_DIM ** -0.5)
        s = s - lax.stop_gradient(jnp.max(s, axis=-1, keepdims=True))
        p = jnp.exp(s)
        p = p / jnp.sum(p, axis=-1, keepdims=True)
        outs.append(_dot1(p, v[:, sl], "nn"))
    return jnp.concatenate(outs, axis=1)


def _loss_bwd(x3, target, g, *, dx_dtype=_F32, name, ts=256):
    s, d = x3.shape

    def f(x, t, gv):
        return 0.5 * jnp.sum(jnp.mean(jnp.square(_f_rms(x, gv) - t), axis=-1))

    def body(x_ref, t_ref, g_ref, dx_ref, dg_ref, l_ref):
        @pl.when(pl.program_id(0) == 0)
        def _():
            dg_ref[...] = jnp.zeros_like(dg_ref)
            l_ref[...] = jnp.zeros_like(l_ref)

        lv, (dx, dg) = jax.value_and_grad(f, argnums=(0, 2))(x_ref[...].astype(_F32), t_ref[...], g_ref[...])
        dx_ref[...] = dx.astype(dx_ref.dtype)
        dg_ref[...] += dg
        l_ref[...] += lv

    row = pl.BlockSpec((ts, d), lambda i: (i, 0))
    return pl.pallas_call(
        body, name=name, grid=(s // ts,),
        in_specs=[row, row, pl.BlockSpec((1, d), lambda i: (0, 0))],
        out_specs=[row, pl.BlockSpec((1, d), lambda i: (0, 0)), pl.BlockSpec((SUBLANES, LANES), lambda i: (0, 0))],
        out_shape=[jax.ShapeDtypeStruct((s, d), dx_dtype), jax.ShapeDtypeStruct((1, d), _F32),
                   jax.ShapeDtypeStruct((SUBLANES, LANES), _F32)],
        compiler_params=_params(("arbitrary",)),
    )(x3, target, g)


_CONV_PAD = 32
_CONV_ROWS = 128
_CONV_CB = 128


def _conv_taps(k_taps):
    groups = {}
    for k in range(k_taps):
        j = k_taps - 1 - k
        groups.setdefault(j % SUBLANES, []).append((k, j))
    return groups


def _conv_window(win, wv, groups, init):
    pad, rows = _CONV_PAD, _CONV_ROWS
    acc = init
    for rot, taps in groups.items():
        rolled = win if rot == 0 else pltpu.roll(win, rot, 0)
        for k, j in taps:
            off = pad - (j - rot)
            acc = acc + rolled[off:off + rows, :] * wv[k:k + 1, :]
    return acc


def _conv_fill(x_refs, xp_ref, s, glu):
    pad, cb = _CONV_PAD, _CONV_CB
    step = _pick(s, (512, 256, _CONV_ROWS))
    xp_ref[0:pad, :] = jnp.zeros((pad, cb), _F32)

    def fill(r, carry):
        base = pl.multiple_of(r * step, step)
        v = x_refs[0][pl.ds(base, step), :].astype(_F32)
        if glu:
            v = v * _sig(x_refs[1][pl.ds(base, step), :].astype(_F32))
        xp_ref[pl.ds(pad + base, step), :] = v
        return carry

    lax.fori_loop(0, s // step, fill, 0)


def _conv_fwd(xs, w, b, k_taps, *, glu=False, act=False, name):
    s, c = xs[0].shape
    kp = w.shape[0]
    pad, rows, cb = _CONV_PAD, _CONV_ROWS, _CONV_CB
    groups = _conv_taps(k_taps)
    n_in = len(xs)

    def body(*refs):
        x_refs = refs[:n_in]
        w_ref, b_ref, o_ref, xp_ref = refs[n_in:]
        _conv_fill(x_refs, xp_ref, s, glu)
        wv = w_ref[...]
        bias = jnp.broadcast_to(b_ref[...], (rows, cb))

        def chunk(r, carry):
            base = pl.multiple_of(r * rows, rows)
            acc = _conv_window(xp_ref[pl.ds(base, rows + pad), :], wv, groups, bias)
            o_ref[pl.ds(base, rows), :] = _silu(acc) if act else acc
            return carry

        lax.fori_loop(0, s // rows, chunk, 0)

    col = pl.BlockSpec((s, cb), lambda i: (0, i))
    return pl.pallas_call(
        body, name=name, grid=(c // cb,),
        in_specs=[col] * n_in + [pl.BlockSpec((kp, cb), lambda i: (0, i)), pl.BlockSpec((1, cb), lambda i: (0, i))],
        out_specs=col, out_shape=jax.ShapeDtypeStruct((s, c), _F32),
        scratch_shapes=[pltpu.VMEM((s + pad, cb), _F32)],
        compiler_params=_params(("parallel",)),
    )(*xs, w, b)


def _conv_bwd(xs, w, b, dy, k_taps, *, glu=False, act=False, name):
    s, c = xs[0].shape
    kp = w.shape[0]
    pad, rows, cb = _CONV_PAD, _CONV_ROWS, _CONV_CB
    groups = _conv_taps(k_taps)
    win_rows = rows + pad
    n_in = len(xs)

    def fold(v):
        acc = v[0:SUBLANES, :]
        for i in range(1, rows // SUBLANES):
            acc = acc + v[i * SUBLANES:(i + 1) * SUBLANES, :]
        return acc

    def body(*refs):
        x_refs = refs[:n_in]
        w_ref, b_ref, dy_ref = refs[n_in:n_in + 3]
        dx_refs = refs[n_in + 3:2 * n_in + 3]
        dw_ref, db_ref, xp_ref, dyp_ref, acc_ref, dbacc_ref = refs[2 * n_in + 3:]
        _conv_fill(x_refs, xp_ref, s, glu)
        dyp_ref[s:s + pad, :] = jnp.zeros((pad, cb), _F32)
        acc_ref[...] = jnp.zeros_like(acc_ref)
        dbacc_ref[...] = jnp.zeros_like(dbacc_ref)
        wv = w_ref[...]
        bias = jnp.broadcast_to(b_ref[...], (rows, cb))

        def through_act(r, carry):
            base = pl.multiple_of(r * rows, rows)
            d = dy_ref[pl.ds(base, rows), :]
            if act:
                pre = _conv_window(xp_ref[pl.ds(base, win_rows), :], wv, groups, bias)
                sg = _sig(pre)
                d = d * (sg * (1.0 + pre * (1.0 - sg)))
            dyp_ref[pl.ds(base, rows), :] = d
            return carry

        lax.fori_loop(0, s // rows, through_act, 0)

        def chunk(r, carry):
            base = pl.multiple_of(r * rows, rows)
            xwin = xp_ref[pl.ds(base, win_rows), :]
            dwin = dyp_ref[pl.ds(base, win_rows), :]
            dyc = dwin[0:rows, :]
            dxacc = jnp.zeros((rows, cb), _F32)
            for rot, taps in groups.items():
                xr = xwin if rot == 0 else pltpu.roll(xwin, rot, 0)
                dr = dwin if rot == 0 else pltpu.roll(dwin, win_rows - rot, 0)
                for k, j in taps:
                    a8 = j - rot
                    dxacc = dxacc + dr[a8:a8 + rows, :] * wv[k:k + 1, :]
                    prod = dyc * xr[pad - a8:pad - a8 + rows, :]
                    acc_ref[k * SUBLANES:(k + 1) * SUBLANES, :] += fold(prod)
            dbacc_ref[...] += fold(dyc)
            if glu:
                av = x_refs[0][pl.ds(base, rows), :].astype(_F32)
                sg = _sig(x_refs[1][pl.ds(base, rows), :].astype(_F32))
                dx_refs[0][pl.ds(base, rows), :] = (dxacc * sg).astype(dx_refs[0].dtype)
                dx_refs[1][pl.ds(base, rows), :] = (dxacc * av * sg * (1.0 - sg)).astype(dx_refs[1].dtype)
            else:
                dx_refs[0][pl.ds(base, rows), :] = dxacc
            return carry

        lax.fori_loop(0, s // rows, chunk, 0)
        dw_ref[...] = jnp.zeros_like(dw_ref)
        for k in range(k_taps):
            dw_ref[k:k + 1, :] = jnp.sum(acc_ref[k * SUBLANES:(k + 1) * SUBLANES, :], axis=0, keepdims=True)
        db_ref[...] = jnp.sum(dbacc_ref[...], axis=0, keepdims=True)

    col = pl.BlockSpec((s, cb), lambda i: (0, i))
    wspec = pl.BlockSpec((kp, cb), lambda i: (0, i))
    bspec = pl.BlockSpec((1, cb), lambda i: (0, i))
    dx_dtype = xs[0].dtype if glu else _F32
    res = pl.pallas_call(
        body, name=name, grid=(c // cb,),
        in_specs=[col] * n_in + [wspec, bspec, col], out_specs=[col] * n_in + [wspec, bspec],
        out_shape=[jax.ShapeDtypeStruct((s, c), dx_dtype)] * n_in
        + [jax.ShapeDtypeStruct((kp, c), _F32), jax.ShapeDtypeStruct((1, c), _F32)],
        scratch_shapes=[pltpu.VMEM((s + pad, cb), _F32), pltpu.VMEM((s + pad, cb), _F32),
                        pltpu.VMEM((kp * SUBLANES, cb), _F32), pltpu.VMEM((SUBLANES, cb), _F32)],
        compiler_params=_params(("parallel",)),
    )(*xs, w, b, dy)
    return list(res[:n_in]), res[n_in], res[n_in + 1]


def _tri_sum(v, lower):
    l = v.shape[0]
    r, c = lax.broadcasted_iota(jnp.int32, (l, l), 0), lax.broadcasted_iota(jnp.int32, (l, l), 1)
    tri = ((r >= c) if lower else (r <= c)).astype(jnp.bfloat16)
    hi = v.astype(jnp.bfloat16)
    r1 = v - hi.astype(_F32)
    mid = r1.astype(jnp.bfloat16)
    lo = (r1 - mid.astype(_F32)).astype(jnp.bfloat16)
    out = jnp.zeros_like(v)
    for part in (hi, mid, lo):
        out = out + lax.dot_general(tri, part, _DN["nn"], preferred_element_type=_F32)
    return out


@jax.custom_vjp
def _cumsum_rows(v):
    return _tri_sum(v, True)


_cumsum_rows.defvjp(lambda v: (_tri_sum(v, True), None), lambda _, ct: (_tri_sum(ct, False),))


def _ssd_chunk(xbc, dtraw, prev, bias, alog):
    l = xbc.shape[0]
    xs = xbc[:, :SSD_WIDTH]
    bm = xbc[:, SSD_WIDTH:SSD_WIDTH + 2 * SSD_STATE]
    cm = xbc[:, SSD_WIDTH + 2 * SSD_STATE:]
    v = dtraw + bias
    dt = jnp.maximum(v, 0.0) + jnp.log1p(jnp.exp(-jnp.abs(v)))
    a_neg = -jnp.exp(alog)
    acs = _cumsum_rows(dt * a_neg)
    acs_t = acs.T
    total = acs[l - 1:l, :]
    row = lax.broadcasted_iota(jnp.int32, (l, l), 0)
    colv = lax.broadcasted_iota(jnp.int32, (l, l), 1)
    causal = row >= colv
    lane_lo = lax.broadcasted_iota(jnp.int32, (l, LANES), 1) < HEAD_DIM
    row_lo = lax.broadcasted_iota(jnp.int32, (LANES, SSD_STATE), 0) < HEAD_DIM

    def pair_lanes(m, h0):
        return jnp.where(lane_lo, m[:, h0:h0 + 1], m[:, h0 + 1:h0 + 2])

    ys, news = [], []
    cb = {}
    for j in range(SSD_HEADS // 2):
        h0 = 2 * j
        grp = h0 // (SSD_HEADS // 2)
        bg = bm[:, grp * SSD_STATE:(grp + 1) * SSD_STATE]
        cg = cm[:, grp * SSD_STATE:(grp + 1) * SSD_STATE]
        if grp not in cb:
            cb[grp] = _dot1(cg, bg, "nt")
        xdt = xs[:, j * LANES:(j + 1) * LANES] * pair_lanes(dt, h0)
        y = jnp.zeros((l, LANES), _F32)
        for hh, mask in ((h0, lane_lo), (h0 + 1, jnp.logical_not(lane_lo))):
            seg = acs[:, hh:hh + 1] - acs_t[hh:hh + 1, :]
            dec = jnp.exp(jnp.where(causal, seg, -jnp.inf))
            y = y + _dot1(cb[grp] * dec, jnp.where(mask, xdt, 0.0), "nn")
        acs_p = pair_lanes(acs, h0)
        prev_p = prev[j * LANES:(j + 1) * LANES, :]
        y = y + _dot1(cg, prev_p, "nt") * jnp.exp(acs_p)
        wgt = jnp.exp(pair_lanes(jnp.broadcast_to(total, (l, LANES)), h0) - acs_p)
        st = _dot1(xdt * wgt, bg, "tn")
        cdec = jnp.exp(jnp.where(row_lo, total[:, h0:h0 + 1], total[:, h0 + 1:h0 + 2]))
        news.append(prev_p * cdec + st)
        ys.append(y)
    return jnp.concatenate(ys, axis=1), jnp.concatenate(news, axis=0)


def _ssd_gate_chunk(xbc, dtraw, prev, bias, alog, z, dexp, g):
    y, new = _ssd_chunk(xbc, dtraw, prev, bias, alog)
    return _f_gate(y, xbc[:, :SSD_WIDTH], z, dexp, g), new


def _ssd_fwd(xbc, dtraw, bias, alog, z, dexp, g, *, name):
    s = xbc.shape[0]
    nc = s // CHUNK
    nstate = SSD_HEADS * HEAD_DIM

    def body(x_ref, dt_ref, b_ref, a_ref, z_ref, d_ref, g_ref, y_ref, st_ref, state_ref):
        @pl.when(pl.program_id(0) == 0)
        def _():
            state_ref[...] = jnp.zeros_like(state_ref)

        prev = state_ref[...]
        st_ref[...] = prev
        y, new = _ssd_gate_chunk(x_ref[...], dt_ref[...], prev, b_ref[...], a_ref[...], z_ref[...].astype(_F32),
                                 d_ref[...], g_ref[...])
        y_ref[...] = y.astype(y_ref.dtype)
        state_ref[...] = new

    small = pl.BlockSpec((1, LANES), lambda i: (0, 0))
    wide = pl.BlockSpec((1, SSD_WIDTH), lambda i: (0, 0))
    rows = pl.BlockSpec((CHUNK, SSD_WIDTH), lambda i: (i, 0))
    return pl.pallas_call(
        body, name=name, grid=(nc,),
        in_specs=[pl.BlockSpec((CHUNK, XBC_WIDTH), lambda i: (i, 0)), pl.BlockSpec((CHUNK, LANES), lambda i: (i, 0)),
                  small, small, rows, wide, wide],
        out_specs=[rows, pl.BlockSpec((None, nstate, SSD_STATE), lambda i: (i, 0, 0))],
        out_shape=[jax.ShapeDtypeStruct((s, SSD_WIDTH), _MXU), jax.ShapeDtypeStruct((nc, nstate, SSD_STATE), _F32)],
        scratch_shapes=[pltpu.VMEM((nstate, SSD_STATE), _F32)],
        compiler_params=_params(("arbitrary",)),
    )(xbc, dtraw, bias, alog, z, dexp, g)


def _ssd_bwd(xbc, dtraw, states, bias, alog, z, dexp, g, dy, *, name):
    s = xbc.shape[0]
    nc = s // CHUNK
    nstate = SSD_HEADS * HEAD_DIM

    def body(x_ref, dt_ref, st_ref, b_ref, a_ref, z_ref, d_ref, g_ref, dy_ref,
             dx_ref, ddt_ref, db_ref, da_ref, dz_ref, dd_ref, dg_ref, dstate_ref):
        @pl.when(pl.program_id(0) == 0)
        def _():
            dstate_ref[...] = jnp.zeros_like(dstate_ref)
            for acc in (db_ref, da_ref, dd_ref, dg_ref):
                acc[...] = jnp.zeros_like(acc)

        _, vjp = jax.vjp(_ssd_gate_chunk, x_ref[...], dt_ref[...], st_ref[...], b_ref[...], a_ref[...],
                         z_ref[...].astype(_F32), d_ref[...], g_ref[...])
        dx, ddt, dprev, db, da, dz, dd, dg = vjp((dy_ref[...].astype(_F32), dstate_ref[...]))
        dx_ref[...] = dx
        ddt_ref[...] = ddt
        dz_ref[...] = dz.astype(dz_ref.dtype)
        db_ref[...] += db
        da_ref[...] += da
        dd_ref[...] += dd
        dg_ref[...] += dg
        dstate_ref[...] = dprev

    rev = lambda i: (nc - 1 - i, 0)
    small = pl.BlockSpec((1, LANES), lambda i: (0, 0))
    wide = pl.BlockSpec((1, SSD_WIDTH), lambda i: (0, 0))
    rows = pl.BlockSpec((CHUNK, SSD_WIDTH), rev)
    return pl.pallas_call(
        body, name=name, grid=(nc,),
        in_specs=[pl.BlockSpec((CHUNK, XBC_WIDTH), rev), pl.BlockSpec((CHUNK, LANES), rev),
                  pl.BlockSpec((None, nstate, SSD_STATE), lambda i: (nc - 1 - i, 0, 0)), small, small, rows, wide, wide,
                  rows],
        out_specs=[pl.BlockSpec((CHUNK, XBC_WIDTH), rev), pl.BlockSpec((CHUNK, LANES), rev), small, small, rows, wide, wide],
        out_shape=[jax.ShapeDtypeStruct((s, XBC_WIDTH), _F32), jax.ShapeDtypeStruct((s, LANES), _F32),
                   jax.ShapeDtypeStruct((1, LANES), _F32), jax.ShapeDtypeStruct((1, LANES), _F32),
                   jax.ShapeDtypeStruct((s, SSD_WIDTH), z.dtype), jax.ShapeDtypeStruct((1, SSD_WIDTH), _F32),
                   jax.ShapeDtypeStruct((1, SSD_WIDTH), _F32)],
        scratch_shapes=[pltpu.VMEM((nstate, SSD_STATE), _F32)],
        compiler_params=_params(("arbitrary",)),
    )(xbc, dtraw, states, bias, alog, z, dexp, g, dy)


def _pad_cols(a, width):
    return jnp.pad(a, ((0, 0), (0, width - a.shape[1])))


def _pad_rows(a, rows):
    return jnp.pad(a, ((0, rows - a.shape[0]), (0, 0)))


def _tie(a, token):
    return a + token[0:1, 0:1].astype(a.dtype)


def _local_step(x, mem, target, w, fetch, emit):
    bf = _MXU
    d = D_MODEL
    h = _row_fwd(_f_rms, [x], [w['norm_mix_g']], [(d, bf)], name="f_norm_mix")
    w_in = fetch('in', h)['w_in']
    z_end, xbc_end, dt_end = SSD_WIDTH, SSD_WIDTH + XBC_WIDTH, SSD_WIDTH + XBC_WIDTH + SSD_HEADS
    w_z, w_xbc = w_in[:z_end], w_in[z_end:xbc_end]
    w_dt = _pad_rows(w_in[xbc_end:dt_end], LANES)
    w_a, w_g = w_in[dt_end:dt_end + CF_WIDTH], w_in[dt_end + CF_WIDTH:]
    dt_bias = _pad_cols(w['ssd_dt_bias'], LANES)
    a_log = _pad_cols(w['ssd_A_log'], LANES)
    d_exp = jnp.repeat(w['ssd_D'], HEAD_DIM, axis=1)
    g_final = w['norm_final_g'].reshape(1, D_MODEL)

    z, xbc, dtr, ga, gg = _mm_fan_out(h, [w_z, w_xbc, w_dt, w_a, w_g], tb=True, out_dtypes=[bf, _F32, _F32, bf, bf],
                                      name="f_in")
    wc = fetch('conv', xbc)
    ssd_w = _pad_rows(wc['ssd_conv_w'], SUBLANES)
    cf_w = _pad_rows(wc['cf_conv_w'], 32)
    xbc_a = _conv_fwd([xbc], ssd_w, w['ssd_conv_b'], SSD_CONV, act=True, name="f_ssd_conv")
    y_n, states = _ssd_fwd(xbc_a, dtr, dt_bias, a_log, z, d_exp, w['ssd_norm_g'], name="f_ssd")
    u_c = _conv_fwd([ga, gg], cf_w, w['cf_conv_b'], CF_CONV, glu=True, name="f_cf_conv")
    u = _row_fwd(_f_ln, [u_c], [w['cf_ln_g'], w['cf_ln_b']], [(d, bf)], name="f_cf_ln")
    wm = fetch('mid', y_n)
    w_out_y, w_out_u = wm['w_out'][:SSD_WIDTH], wm['w_out'][SSD_WIDTH:]
    x1, hq = _mm_fan_in([(y_n, w_out_y), (u, w_out_u)], add=x, out_dtype=bf, epilogue=(_f_rms, w['norm_xattn_g'], bf),
                        name="f_out")
    q = _mm(hq, wm['w_q'], out_dtype=bf, name="f_q")
    memn = _row_fwd(_f_rms, [mem], [w['norm_mem_g']], [(d, bf)], name="f_norm_mem")
    kv = _mm(memn, wm['w_kv'], name="f_kv")
    k_mat, v_mat = kv[:, :d], kv[:, d:]
    o = _row_fwd(_f_att, [q], [k_mat, v_mat], [(d, bf)], name="f_att")
    x2, hf = _mm_fan_in([(o, wm['w_o'])], add=x1, out_dtype=bf, epilogue=(_f_rms, w['norm_ffn_g'], bf), name="f_o")
    wf = fetch('ffn', hf)
    gate, up, act = _mm_fan_out(hf, [wf['w_gate'], wf['w_up']], tb=True, out_dtypes=[bf, bf], epilogue=_f_swiglu,
                                extra_outs=[(D_FF, bf)], tm=256, name="f_ffn_in")
    x3 = _mm(act, wf['w_down'], add=x2, out_dtype=bf, name="f_down")

    dx3, dg_final, loss = _loss_bwd(x3, target, g_final, dx_dtype=bf, name="b_loss")
    g = {'norm_final_g': dg_final.reshape(d)}

    dact = _mm(dx3, wf['w_down'], tb=True, out_dtype=bf, name="b_down_x")
    dw_down = _mm(act, dx3, ta=True, out_dtype=bf, name="b_down_w")
    def swiglu_bwd(gate_t, up_t, dact_t):
        return jax.vjp(_f_swiglu, gate_t, up_t)[1](dact_t)

    dhf, dgate, dup = _mm_fan_in([(None, wf['w_gate']), (None, wf['w_up'])], prologue=swiglu_bwd, pro_ins=[gate, up, dact],
                                 pro_out_dtypes=[bf, bf], out_dtype=bf, tm=256, name="b_ffn_in_x")
    sent = emit({'w_down': dw_down, 'w_gate': _mm(dgate, hf, ta=True, out_dtype=bf, name="b_gate_w"),
                 'w_up': _mm(dup, hf, ta=True, out_dtype=bf, name="b_up_w")})
    (dx2,), (g['norm_ffn_g'],) = _row_bwd(_f_rms, [x2], [_tie(w['norm_ffn_g'], sent)], [dhf], adds={0: dx3}, row_dtypes=[bf], name="b_norm_ffn")

    do = _mm(dx2, wm['w_o'], tb=True, out_dtype=bf, name="b_o_x")
    dw_o = _mm(o, dx2, ta=True, out_dtype=bf, name="b_o_w")
    (dq,), (dk, dv) = _row_bwd(_f_att, [q], [k_mat, v_mat], [do], row_dtypes=[bf], name="b_att")
    dw_q = _mm(hq, dq, ta=True, out_dtype=bf, name="b_q_w")
    dhq = _mm(dq, wm['w_q'], tb=True, out_dtype=bf, name="b_q_x")
    (dx1,), (g['norm_xattn_g'],) = _row_bwd(_f_rms, [x1], [w['norm_xattn_g']], [dhq], adds={0: dx2}, row_dtypes=[bf], name="b_norm_xattn")
    dkv = jnp.concatenate([dk, dv], axis=1)
    dmemn = _mm(dkv, wm['w_kv'], tb=True, name="b_kv_x")
    _, (g['norm_mem_g'],) = _row_bwd(_f_rms, [mem], [w['norm_mem_g']], [dmemn], need=[False], name="b_norm_mem")
    sent = emit({'w_o': dw_o, 'w_q': dw_q, 'w_kv': _mm(memn, dkv, ta=True, out_dtype=bf, name="b_kv_w")},
                after=g['norm_mem_g'])

    dyn, du = _mm_fan_out(dx1, [w_out_y, w_out_u], tb=True, out_dtypes=[bf, bf], name="b_out_x")
    (du_c,), (g['cf_ln_g'], g['cf_ln_b']) = _row_bwd(_f_ln, [u_c], [_tie(w['cf_ln_g'], sent), w['cf_ln_b']], [du], name="b_cf_ln")
    sent = emit({'w_out': jnp.concatenate([_mm(y_n, dx1, ta=True, out_dtype=bf, name="b_out_y_w"), _mm(u, dx1, ta=True, out_dtype=bf, name="b_out_u_w")], axis=0)})
    (dga, dgg), dcf_w, g['cf_conv_b'] = _conv_bwd([ga, gg], cf_w, w['cf_conv_b'], du_c, CF_CONV, glu=True, name="b_cf_conv")
    g['cf_conv_w'] = dcf_w[:CF_CONV]
    dxbc_a, ddtr, ddt_bias, da_log, dz, dd_exp, g['ssd_norm_g'] = _ssd_bwd(
        xbc_a, dtr, states, dt_bias, a_log, z, d_exp, _tie(w['ssd_norm_g'], sent), dyn, name="b_ssd")
    g['ssd_D'] = jnp.sum(dd_exp.reshape(SSD_HEADS, HEAD_DIM), axis=1).reshape(1, SSD_HEADS)
    g['ssd_dt_bias'] = ddt_bias[:, :SSD_HEADS]
    g['ssd_A_log'] = da_log[:, :SSD_HEADS]
    (dxbc,), dssd_w, g['ssd_conv_b'] = _conv_bwd([xbc], ssd_w, w['ssd_conv_b'], dxbc_a, SSD_CONV, act=True, name="b_ssd_conv")
    g['ssd_conv_w'] = dssd_w[:SSD_CONV]

    sent = emit({'w_in': jnp.concatenate([
        _mm(dz, h, ta=True, out_dtype=bf, name="b_in_z_w"), _mm(dxbc, h, ta=True, out_dtype=bf, name="b_in_xbc_w"),
        _mm(ddtr, h, ta=True, out_dtype=bf, name="b_in_dt_w")[:SSD_HEADS],
        _mm(dga, h, ta=True, out_dtype=bf, name="b_in_a_w"), _mm(dgg, h, ta=True, out_dtype=bf, name="b_in_g_w")], axis=0)})
    dh = _mm_fan_in([(dz, w_z), (dxbc, w_xbc), (ddtr, _tie(w_dt, sent)), (dga, w_a), (dgg, w_g)], out_dtype=bf,
                    name="b_in_x")
    (dx,), (g['norm_mix_g'],) = _row_bwd(_f_rms, [x], [w['norm_mix_g']], [dh], adds={0: dx1}, name="b_norm_mix")
    return loss, dx, g


_ANY = pl.BlockSpec(memory_space=pl.ANY)


def _place():
    x, y, c = lax.axis_index("x"), lax.axis_index("y"), lax.axis_index("c")
    return x, y, c


def _all_gather(arrs, *, name):
    n = len(arrs)

    def body(*refs):
        ins, outs = refs[:n], refs[n:2 * n]
        send_sems, recv_sems, local_sems = refs[2 * n:]
        x, y, c = _place()
        me, sibling = (x, y, c), (x, y, 1 - c)
        chips = [(1 - x, y), (x, 1 - y), (1 - x, 1 - y)]

        def slot(a, dev):
            return outs[a].at[4 * dev[0] + 2 * dev[1] + dev[2]]

        def copy(a, k, block, to, src=None):
            return pltpu.make_async_remote_copy(
                src_ref=slot(a, block) if src is None else src, dst_ref=slot(a, block),
                send_sem=send_sems.at[a, k], recv_sem=recv_sems.at[a, k], device_id=to, device_id_type=MESH)

        mine = [pltpu.make_async_copy(ins[a], slot(a, me), local_sems.at[a]) for a in range(n)]
        for cp in mine:
            cp.start()
        first = []
        for a in range(n):
            first.append(copy(a, 0, me, sibling, src=ins[a]))
            first += [copy(a, 1 + j, me, (*chip, c), src=ins[a]) for j, chip in enumerate(chips)]
        for cp in first:
            cp.start()
        passed = []
        for a in range(n):
            for j, chip in enumerate(chips):
                copy(a, 1 + j, (*chip, c), me).wait_recv()
                fwd = copy(a, 4 + j, (*chip, c), sibling)
                fwd.start()
                passed.append(fwd)
        for a in range(n):
            copy(a, 0, sibling, me).wait_recv()
            for j, chip in enumerate(chips):
                copy(a, 4 + j, (*chip, 1 - c), me).wait_recv()
        for cp in first + passed:
            cp.wait_send()
        for cp in mine:
            cp.wait()

    return pl.pallas_call(
        body, name=name, in_specs=[_ANY] * n, out_specs=[_ANY] * n,
        out_shape=[jax.ShapeDtypeStruct((N_DEV,) + a.shape, a.dtype) for a in arrs],
        scratch_shapes=[pltpu.SemaphoreType.DMA((n, 7)), pltpu.SemaphoreType.DMA((n, 7)), pltpu.SemaphoreType.DMA((n,))],
    )(*arrs)


_HBM = pl.BlockSpec(memory_space=pltpu.HBM)
_SEM = pl.BlockSpec(memory_space=pltpu.SEMAPHORE)
_EFFECT = pltpu.SideEffectType.DATAFLOW_SIDE_EFFECTING
_FLIPS = [(dx, dy, dc) for dx in (0, 1) for dy in (0, 1) for dc in (0, 1)][1:]


def _peer(flip, x, y, c):
    return (1 - x if flip[0] else x, 1 - y if flip[1] else y, 1 - c if flip[2] else c)


def _send_start(srcs, blocked, *, after=None, name):
    n = len(srcs)
    lands = [jax.ShapeDtypeStruct(s.shape if blocked else (N_DEV,) + s.shape, s.dtype) for s in srcs]
    n_in = 2 * n + (after is not None)

    def body(*refs):
        src_refs, land_refs = refs[:n], refs[n:2 * n]
        send_sems, recv_sems = refs[n_in], refs[n_in + 1]
        token = refs[-1]
        x, y, c = _place()
        me = 4 * x + 2 * y + c
        for a in range(n):
            for k, flip in enumerate(_FLIPS):
                p = _peer(flip, x, y, c)
                src = src_refs[a].at[4 * p[0] + 2 * p[1] + p[2]] if blocked else src_refs[a]
                pltpu.make_async_remote_copy(
                    src_ref=src, dst_ref=land_refs[a].at[me], send_sem=send_sems.at[7 * a + k], recv_sem=recv_sems.at[7 * a + k],
                    device_id=p, device_id_type=MESH).start()
        token[...] = jnp.zeros_like(token)

    res = pl.pallas_call(
        body, name=name,
        out_shape=(pltpu.SemaphoreType.DMA((7 * n,)), pltpu.SemaphoreType.DMA((7 * n,)),
                   *[pltpu.HBM(s.shape, s.dtype) for s in srcs], *[pltpu.HBM(l.shape, l.dtype) for l in lands],
                   jax.ShapeDtypeStruct((SUBLANES, LANES), _F32)),
        in_specs=[_HBM] * (2 * n) + [_ANY] * (after is not None),
        out_specs=(_SEM, _SEM, *[_HBM] * (2 * n), pl.BlockSpec(memory_space=pltpu.VMEM)),
        input_output_aliases={i: 2 + i for i in range(2 * n)},
        compiler_params=pltpu.CompilerParams(has_side_effects=_EFFECT),
    )(*[pltpu.with_memory_space_constraint(s, pltpu.HBM) for s in srcs],
      *[pltpu.with_memory_space_constraint(lax.empty(l.shape, l.dtype), pltpu.HBM) for l in lands],
      *([after] if after is not None else []))
    return res[0], res[1], list(res[2:2 + n]), list(res[2 + n:2 + 2 * n]), res[-1]


def _send_wait(handles, after, blocked, *, name):
    send_sems, recv_sems, srcs, lands, _ = handles
    n = len(srcs)

    def body(*refs):
        src_refs, land_refs = refs[:n], refs[n:2 * n]
        send_sems, recv_sems = refs[2 * n], refs[2 * n + 1]
        x, y, c = _place()
        for a in range(n):
            for k, flip in enumerate(_FLIPS):
                p = _peer(flip, x, y, c)
                pid = 4 * p[0] + 2 * p[1] + p[2]
                cp = pltpu.make_async_remote_copy(
                    src_ref=src_refs[a].at[pid] if blocked else src_refs[a], dst_ref=land_refs[a].at[pid],
                    send_sem=send_sems.at[7 * a + k], recv_sem=recv_sems.at[7 * a + k], device_id=p, device_id_type=MESH)
                cp.wait_send()
                cp.wait_recv()

    res = pl.pallas_call(
        body, name=name,
        out_shape=tuple(pltpu.HBM(s.shape, s.dtype) for s in srcs + lands),
        in_specs=[_HBM] * (2 * n) + [_SEM, _SEM, _ANY], out_specs=tuple([_HBM] * (2 * n)),
        input_output_aliases={i: i for i in range(2 * n)},
        compiler_params=pltpu.CompilerParams(has_side_effects=_EFFECT),
    )(*srcs, *lands, send_sems, recv_sems, after)
    return list(res[:n]), list(res[n:])


def _adamw(parts, w, m, v, *, own=None, me=None, name):
    p, r, c = parts.shape
    tr = _pick(r, (256, 176, 128, 64, 32, 16, 8))
    if own is not None:
        tc = c if tr < r else _pick(c, (256, 128))
        return _adamw_own(parts, own, me, w, m, v, tr, tc, name=name)

    def body(p_ref, w_ref, m_ref, v_ref, g_ref, d_ref, nm_ref, nv_ref):
        g = p_ref[0].astype(_F32)
        for i in range(1, p):
            g = g + p_ref[i].astype(_F32)
        _adamw_math(g, w_ref, m_ref, v_ref, g_ref, d_ref, nm_ref, nv_ref)

    blk = pl.BlockSpec((tr, c), lambda i: (i, 0))
    return pl.pallas_call(
        body, name=name, grid=(r // tr,),
        in_specs=[pl.BlockSpec((p, tr, c), lambda i: (0, i, 0)), blk, blk, blk], out_specs=[blk] * 4,
        out_shape=[jax.ShapeDtypeStruct((r, c), _F32)] * 4,
        compiler_params=_params(("parallel",)),
    )(parts, w, m, v)


def _adamw_math(g, w_ref, m_ref, v_ref, g_ref, d_ref, nm_ref, nv_ref):
    wv = w_ref[...]
    mn = ADAM_B1 * m_ref[...] + (1.0 - ADAM_B1) * g
    vn = ADAM_B2 * v_ref[...] + (1.0 - ADAM_B2) * jnp.square(g)
    m_hat = mn / (1.0 - ADAM_B1 ** ADAM_STEP)
    v_hat = vn / (1.0 - ADAM_B2 ** ADAM_STEP)
    g_ref[...] = g
    d_ref[...] = -ADAM_LR * (m_hat / (jnp.sqrt(v_hat) + ADAM_EPS) + ADAM_WD * wv)
    nm_ref[...] = mn
    nv_ref[...] = vn


def _adamw_own(parts, own, me, w, m, v, tr, tc, *, name):
    p, r, c = parts.shape

    def body(me_ref, p_ref, own_ref, w_ref, m_ref, v_ref, g_ref, d_ref, nm_ref, nv_ref):
        mine = own_ref[...].astype(_F32)
        g = jnp.where(me_ref[0] == 0, mine, p_ref[0].astype(_F32))
        for i in range(1, p):
            g = g + jnp.where(me_ref[0] == i, mine, p_ref[i].astype(_F32))
        _adamw_math(g, w_ref, m_ref, v_ref, g_ref, d_ref, nm_ref, nv_ref)

    blk = pl.BlockSpec((tr, tc), lambda i, j, me_ref: (i, j))
    grid_spec = pltpu.PrefetchScalarGridSpec(
        num_scalar_prefetch=1, grid=(r // tr, c // tc),
        in_specs=[pl.BlockSpec((p, tr, tc), lambda i, j, me_ref: (0, i, j)),
                  pl.BlockSpec((None, tr, tc), lambda i, j, me_ref: (me_ref[0], i, j)), blk, blk, blk],
        out_specs=[blk] * 4)
    return pl.pallas_call(
        body, name=name, grid_spec=grid_spec, out_shape=[jax.ShapeDtypeStruct((r, c), _F32)] * 4,
        compiler_params=_params(("parallel", "parallel")),
    )(me.reshape(1).astype(jnp.int32), parts, own, w, m, v)


def _adamw_rows(g_row, offsets, ws, ms, vs, *, name):
    k = len(ws)

    def body(*refs):
        g_ref, w_refs, m_refs, v_refs = refs[0], refs[1:1 + k], refs[1 + k:1 + 2 * k], refs[1 + 2 * k:1 + 3 * k]
        outs = refs[1 + 3 * k:]
        for i in range(k):
            gi = g_ref[:, offsets[i]:offsets[i] + ws[i].shape[1]]
            _adamw_math(gi, w_refs[i], m_refs[i], v_refs[i], *outs[4 * i:4 * i + 4])

    return pl.pallas_call(
        body, name=name, out_shape=[jax.ShapeDtypeStruct(w.shape, _F32) for w in ws for _ in range(4)],
    )(g_row, *ws, *ms, *vs)


def _sum_parts(parts, *, name):
    p, r, c = parts.shape

    def body(p_ref, o_ref):
        g = p_ref[0].astype(_F32)
        for i in range(1, p):
            g = g + p_ref[i].astype(_F32)
        o_ref[...] = g

    return pl.pallas_call(body, name=name, out_shape=jax.ShapeDtypeStruct((r, c), _F32))(parts)


def _pack(vals, rows):
    flat = jnp.concatenate([v.reshape(-1) for v in vals])
    return jnp.pad(flat, (0, rows * LANES - flat.shape[0])).reshape(rows, LANES)


def _unpack(packed, shapes):
    flat = packed.reshape(-1)
    out, pos = [], 0
    for shp in shapes:
        size = math.prod(shp)
        out.append(flat[pos:pos + size].reshape(shp))
        pos += size
    return out


def _pack_rows(shapes):
    total = sum(math.prod(s) for s in shapes)
    return -(-total // (LANES * SUBLANES)) * SUBLANES


def kernel(x, mem, norm_mix_g, w_in, ssd_conv_w, ssd_conv_b, ssd_dt_bias, ssd_A_log, ssd_D, ssd_norm_g, cf_conv_w, cf_conv_b, cf_ln_g, cf_ln_b, w_out, norm_xattn_g, norm_mem_g, w_q, w_kv, w_o, norm_ffn_g, w_gate, w_up, w_down, norm_final_g, loss_target, m_norm_mix_g, m_w_in, m_ssd_conv_w, m_ssd_conv_b, m_ssd_dt_bias, m_ssd_A_log, m_ssd_D, m_ssd_norm_g, m_cf_conv_w, m_cf_conv_b, m_cf_ln_g, m_cf_ln_b, m_w_out, m_norm_xattn_g, m_norm_mem_g, m_w_q, m_w_kv, m_w_o, m_norm_ffn_g, m_w_gate, m_w_up, m_w_down, m_norm_final_g, v_norm_mix_g, v_w_in, v_ssd_conv_w, v_ssd_conv_b, v_ssd_dt_bias, v_ssd_A_log, v_ssd_D, v_ssd_norm_g, v_cf_conv_w, v_cf_conv_b, v_cf_ln_g, v_cf_ln_b, v_w_out, v_norm_xattn_g, v_norm_mem_g, v_w_q, v_w_kv, v_w_o, v_norm_ffn_g, v_w_gate, v_w_up, v_w_down, v_norm_final_g):
    args = dict(locals())
    wts = {n: args[n] for n in WEIGHT_NAMES}
    mom = {n: args["m_" + n] for n in WEIGHT_NAMES}
    var = {n: args["v_" + n] for n in WEIGHT_NAMES}
    me = 4 * lax.axis_index("x") + 2 * lax.axis_index("y") + lax.axis_index("c")

    groups = {'in': ['w_in'], 'conv': ['ssd_conv_w', 'cf_conv_w'], 'mid': ['w_out', 'w_q', 'w_kv', 'w_o'],
              'ffn': ['w_gate', 'w_up', 'w_down']}
    def shard(n, a):
        return jnp.transpose(a[0], (1, 0)) if n in TRANSPOSED else a[0]

    gathers, started = {}, None
    for grp, names in groups.items():
        shards = [wts[n][0] if grp == 'conv' else shard(n, wts[n]).astype(_MXU) for n in names]
        gathers[grp] = _send_start(shards, False, after=started, name="gather_%s_start" % grp)
        started = gathers[grp][4]

    def fetch(grp, after):
        srcs, lands = _send_wait(gathers[grp], started if after is None else after, False, name="gather_%s_wait" % grp)
        out = {}
        for n, own, gth in zip(groups[grp], srcs, lands):
            gth = lax.dynamic_update_slice_in_dim(gth, own[None], me, axis=0)
            if n == 'w_kv' or grp == 'conv':
                out[n] = jnp.transpose(gth, (1, 0, 2)).reshape(gth.shape[1], N_DEV * gth.shape[2])
            else:
                out[n] = gth.reshape(N_DEV * gth.shape[1], gth.shape[2])
        return out

    exchanges = []

    def emit(grads, after=None):
        blocks = []
        for n, gw in grads.items():
            if n == 'w_kv':
                gw = jnp.transpose(gw.reshape(gw.shape[0], N_DEV, gw.shape[1] // N_DEV), (1, 0, 2))
            else:
                gw = gw.reshape(N_DEV, gw.shape[0] // N_DEV, gw.shape[1])
            blocks.append(gw.astype(jnp.bfloat16))
        first = next(iter(grads))
        exchanges.append((list(grads), _send_start(blocks, True, after=after, name="exchange_%s_start" % first), first))
        return exchanges[-1][1][4]

    full = {n: wts[n] for n in WEIGHT_NAMES if n not in BIG and n not in groups['conv']}
    full['norm_mix_g'] = _tie(norm_mix_g, started)

    loss_blk, grad_x, g = _local_step(x[0], mem[0], loss_target[0], full, fetch, emit)

    small = [n for n in WEIGHT_NAMES if n not in BIG]
    g['loss'] = loss_blk[0:1, 0:1]
    items = small + ['loss']
    size = {n: math.prod(g[n].shape) for n in items}
    seg = {n: -(-size[n] // LANES) * LANES for n in items}
    off, pos = {}, 0
    for n in items:
        off[n], pos = pos, pos + seg[n]
    rows = -(-pos // (LANES * SUBLANES)) * SUBLANES
    flat = jnp.concatenate([jnp.pad(g[n].reshape(-1), (0, seg[n] - size[n])) for n in items]
                           + [jnp.zeros((rows * LANES - pos,), _F32)])
    small_sent = _send_start([flat.reshape(rows, LANES)], False, name="gather_small_start")

    out_g, out_d, out_m, out_v = {}, {}, {}, {}
    done = small_sent[4]
    for names, handles, first in exchanges:
        srcs, lands = _send_wait(handles, done, True, name="exchange_%s_wait" % first)
        for n, own, parts in zip(names, srcs, lands):
            res = _adamw(parts, shard(n, wts[n]), shard(n, mom[n]), shard(n, var[n]), own=own, me=me, name="adamw_" + n)
            out_g[n], out_d[n], out_m[n], out_v[n] = [(jnp.transpose(r, (1, 0)) if n in TRANSPOSED else r)[None] for r in res]
            done = res[0]

    srcs, lands = _send_wait(small_sent, out_g[exchanges[-1][0][-1]], False, name="gather_small_wait")
    small_parts = lax.dynamic_update_slice_in_dim(lands[0], srcs[0][None], me, axis=0)
    g_row = _sum_parts(small_parts, name="sum_small_grads").reshape(1, rows * LANES)
    loss = g_row[0, off['loss']]
    rep = [n for n in small if n not in groups['conv']]
    as_row = lambda a: a.reshape(1, -1)
    res = _adamw_rows(g_row, [off[n] for n in rep], [as_row(wts[n]) for n in rep], [as_row(mom[n]) for n in rep],
                      [as_row(var[n]) for n in rep], name="adamw_small")
    for i, n in enumerate(rep):
        out_g[n], out_d[n], out_m[n], out_v[n] = [r.reshape(wts[n].shape) for r in res[4 * i:4 * i + 4]]
    for n in groups['conv']:
        k_taps, width = g[n].shape
        g_full = g_row[0, off[n]:off[n] + size[n]].reshape(k_taps, width)
        g_mine = lax.dynamic_slice_in_dim(g_full, me * (width // N_DEV), width // N_DEV, axis=1)
        res = _adamw(g_mine[None], wts[n][0], mom[n][0], var[n][0], name="adamw_" + n)
        out_g[n], out_d[n], out_m[n], out_v[n] = [r[None] for r in res]

    return (loss, grad_x[None], *[out_g[n] for n in WEIGHT_NAMES], *[out_d[n] for n in WEIGHT_NAMES],
            *[out_m[n] for n in WEIGHT_NAMES], *[out_v[n] for n in WEIGHT_NAMES])
```

```python
import functools
import math

import jax
import jax.numpy as jnp
from jax import lax
from jax.experimental import pallas as pl
from jax.experimental.pallas import tpu as pltpu

_F32 = jnp.float32
_MXU = jnp.bfloat16
_PREC = None
_VMEM_LIMIT = 56 * 1024 * 1024

D_MODEL = 1024
HEAD_DIM = 64
SSD_HEADS = 16
SSD_WIDTH = 1024
SSD_STATE = 128
SSD_CONV = 4
CHUNK = 128
XBC_WIDTH = 1536
CF_WIDTH = 1024
CF_CONV = 31
X_HEADS = 4
X_HEAD_DIM = 256
D_FF = 2816
EPS = 1e-6
N_DEV = 8
LANES = 128
SUBLANES = 8

ADAM_LR = 0.001
ADAM_B1 = 0.9
ADAM_B2 = 0.999
ADAM_EPS = 1e-08
ADAM_WD = 0.01
ADAM_STEP = 10

MESH = pl.DeviceIdType.MESH
WEIGHT_NAMES = ['norm_mix_g', 'w_in', 'ssd_conv_w', 'ssd_conv_b', 'ssd_dt_bias', 'ssd_A_log', 'ssd_D', 'ssd_norm_g',
                'cf_conv_w', 'cf_conv_b', 'cf_ln_g', 'cf_ln_b', 'w_out', 'norm_xattn_g', 'norm_mem_g', 'w_q', 'w_kv',
                'w_o', 'norm_ffn_g', 'w_gate', 'w_up', 'w_down', 'norm_final_g']
BIG = ['w_in', 'w_out', 'w_q', 'w_kv', 'w_o', 'w_gate', 'w_up', 'w_down']
TRANSPOSED = ('w_in', 'w_gate', 'w_up')


def _params(sem=None):
    return pltpu.CompilerParams(dimension_semantics=sem, vmem_limit_bytes=_VMEM_LIMIT)


def _pick(n, cands):
    for c in cands:
        if n % c == 0:
            return c
    return n


def _mm(a, b, *, ta=False, tb=False, add=None, out_dtype=_F32, name):
    (kdim, m) = a.shape if ta else a.shape[::-1]
    (n, k2) = b.shape if tb else b.shape[::-1]
    assert kdim == k2, (a.shape, b.shape, ta, tb)
    if ta:
        tm = m if m <= 1024 else _pick(m, (1408, 1024, 512, 256, 128))
        tn = n if n <= 1536 else _pick(n, (1408, 1024, 512, 256, 128))
        tk = _pick(kdim, (1024, 512, 256, 128))
    else:
        tm = _pick(m, (512, 256, 128))
        tn = n if n <= 2816 else _pick(n, (1408, 1024, 512, 256, 128))
        tk = kdim if kdim <= 2816 else _pick(kdim, (1408, 1024, 512, 256, 128))
    nk = kdim // tk
    dn = (((0 if ta else 1,), (1 if tb else 0,)), ((), ()))

    def body(*refs):
        a_ref, b_ref = refs[0], refs[1]
        add_ref = refs[2] if add is not None else None
        o_ref = refs[3 if add is not None else 2]
        acc_ref = refs[-1]
        k = pl.program_id(2)
        prod = lax.dot_general(a_ref[...].astype(_MXU), b_ref[...].astype(_MXU), dn,
                               preferred_element_type=_F32, precision=_PREC)

        def finish(r):
            if add_ref is not None:
                r = r + add_ref[...].astype(_F32)
            o_ref[...] = r.astype(o_ref.dtype)

        if nk == 1:
            finish(prod)
            return

        @pl.when(k == 0)
        def _():
            acc_ref[...] = prod

        @pl.when(jnp.logical_and(k > 0, k < nk - 1))
        def _():
            acc_ref[...] += prod

        @pl.when(k == nk - 1)
        def _():
            finish(acc_ref[...] + prod)

    a_spec = pl.BlockSpec((tk, tm), lambda i, j, k: (k, i)) if ta else pl.BlockSpec((tm, tk), lambda i, j, k: (i, k))
    b_spec = pl.BlockSpec((tn, tk), lambda i, j, k: (j, k)) if tb else pl.BlockSpec((tk, tn), lambda i, j, k: (k, j))
    o_spec = pl.BlockSpec((tm, tn), lambda i, j, k: (i, j))
    ins, specs = [a, b], [a_spec, b_spec]
    if add is not None:
        ins.append(add)
        specs.append(o_spec)
    return pl.pallas_call(
        body, name=name, grid=(m // tm, n // tn, nk), in_specs=specs, out_specs=o_spec,
        out_shape=jax.ShapeDtypeStruct((m, n), out_dtype),
        scratch_shapes=[pltpu.VMEM((tm, tn), _F32)] if nk > 1 else [],
        compiler_params=_params(("parallel", "parallel", "arbitrary")),
    )(*ins)


def _resident(shape):
    return pl.BlockSpec(shape, lambda i: (0,) * len(shape), pipeline_mode=pl.Buffered(1))


def _mm_fan_out(a, bs, *, tb, out_dtypes, epilogue=None, extra_outs=(), tm=512, name):
    m, kdim = a.shape
    tm = min(tm, m)
    ns = [b.shape[0] if tb else b.shape[1] for b in bs]
    nb = len(bs)
    kind = "nt" if tb else "nn"

    def body(*refs):
        a_ref, b_refs, o_refs = refs[0], refs[1:1 + nb], refs[1 + nb:]
        av = a_ref[...].astype(_MXU)
        prods = [lax.dot_general(av, b[...].astype(_MXU), _DN[kind], preferred_element_type=_F32, precision=_PREC)
                 for b in b_refs]
        for o_ref, p in zip(o_refs[:nb], prods):
            o_ref[...] = p.astype(o_ref.dtype)
        if epilogue is not None:
            for o_ref, v in zip(o_refs[nb:], _tup(epilogue(*prods))):
                o_ref[...] = v.astype(o_ref.dtype)

    widths = ns + [w for w, _ in extra_outs]
    dtypes = list(out_dtypes) + [dt for _, dt in extra_outs]
    return pl.pallas_call(
        body, name=name, grid=(m // tm,),
        in_specs=[pl.BlockSpec((tm, kdim), lambda i: (i, 0))] + [_resident(b.shape) for b in bs],
        out_specs=[pl.BlockSpec((tm, w), lambda i: (i, 0)) for w in widths],
        out_shape=[jax.ShapeDtypeStruct((m, w), dt) for w, dt in zip(widths, dtypes)],
        compiler_params=_params(("parallel",)),
    )(a, *bs)


def _mm_fan_in(pairs, *, add=None, out_dtype=_F32, prologue=None, pro_ins=(), pro_out_dtypes=(), epilogue=None,
               tm=512, name):
    bs = [b for _, b in pairs]
    nb = len(bs)
    n = bs[0].shape[1]
    rows_in = list(pro_ins) if prologue is not None else [a for a, _ in pairs]
    m = rows_in[0].shape[0]
    tm = min(tm, m)
    n_r = len(rows_in)

    def body(*refs):
        r_refs, b_refs = refs[:n_r], refs[n_r:n_r + nb]
        pos = n_r + nb
        add_ref = refs[pos] if add is not None else None
        pos += add is not None
        epi_ref = refs[pos] if epilogue is not None else None
        pos += epilogue is not None
        o_ref, po_refs = refs[pos], refs[pos + 1:]
        if prologue is not None:
            a_vals = _tup(prologue(*[r[...].astype(_F32) for r in r_refs]))
            for po, v in zip(po_refs, a_vals):
                po[...] = v.astype(po.dtype)
        else:
            a_vals = [r[...] for r in r_refs]
        acc = None
        for av, b in zip(a_vals, b_refs):
            p = lax.dot_general(av.astype(_MXU), b[...].astype(_MXU), _DN["nn"], preferred_element_type=_F32,
                                precision=_PREC)
            acc = p if acc is None else acc + p
        if add_ref is not None:
            acc = acc + add_ref[...].astype(_F32)
        o_ref[...] = acc.astype(o_ref.dtype)
        if epilogue is not None:
            po_refs[-1][...] = epilogue[0](acc, epi_ref[...]).astype(po_refs[-1].dtype)

    row = lambda w: pl.BlockSpec((tm, w), lambda i: (i, 0))
    ins = rows_in + bs + ([add] if add is not None else []) + ([epilogue[1]] if epilogue is not None else [])
    in_specs = ([row(r.shape[1]) for r in rows_in] + [_resident(b.shape) for b in bs]
                + ([row(n)] if add is not None else []) + ([_resident(epilogue[1].shape)] if epilogue is not None else []))
    extra = [(b.shape[0], dt) for b, dt in zip(bs, pro_out_dtypes)] if prologue is not None else []
    if epilogue is not None:
        extra.append((n, epilogue[2]))
    res = pl.pallas_call(
        body, name=name, grid=(m // tm,), in_specs=in_specs,
        out_specs=[row(n)] + [row(w) for w, _ in extra],
        out_shape=[jax.ShapeDtypeStruct((m, n), out_dtype)] + [jax.ShapeDtypeStruct((m, w), dt) for w, dt in extra],
        compiler_params=_params(("parallel",)),
    )(*ins)
    return res if extra else res[0]


def _row_spec(r, ts):
    if isinstance(r, tuple):
        arr, width, cblk = r
        return arr, pl.BlockSpec((ts, width), lambda i, cblk=cblk: (i, cblk))
    return r, pl.BlockSpec((ts, r.shape[1]), lambda i: (i, 0))


def _tup(v):
    return tuple(v) if isinstance(v, (tuple, list)) else (v,)


def _row_fwd(f, rows, params, outs, *, name, ts=256):
    arrs, specs = zip(*[_row_spec(r, ts) for r in rows])
    s = arrs[0].shape[0]
    ts = min(ts, s)
    n_r, n_p = len(rows), len(params)

    def body(*refs):
        rv = [r[...].astype(_F32) for r in refs[:n_r]]
        pv = [p[...] for p in refs[n_r:n_r + n_p]]
        res = _tup(f(*rv, *pv))
        for o_ref, v in zip(refs[n_r + n_p:], res):
            o_ref[...] = v.astype(o_ref.dtype)

    res = pl.pallas_call(
        body, name=name, grid=(s // ts,),
        in_specs=list(specs) + [pl.BlockSpec(p.shape, lambda i: (0, 0)) for p in params],
        out_specs=[pl.BlockSpec((ts, w), lambda i: (i, 0)) for w, _ in outs],
        out_shape=[jax.ShapeDtypeStruct((s, w), dt) for w, dt in outs],
        compiler_params=_params(("parallel",)),
    )(*arrs, *params)
    return res[0] if len(outs) == 1 else res


def _row_bwd(f, rows, params, cts, *, need=None, adds=None, row_dtypes=None, name, ts=256):
    arrs, specs = zip(*[_row_spec(r, ts) for r in rows])
    s = arrs[0].shape[0]
    ts = min(ts, s)
    n_r, n_p, n_c = len(rows), len(params), len(cts)
    need = [True] * n_r if need is None else need
    adds = {} if adds is None else adds
    add_keys = sorted(adds)
    row_dtypes = [_F32] * n_r if row_dtypes is None else row_dtypes
    needed = [j for j in range(n_r) if need[j]]
    widths = [specs[j].block_shape[1] for j in range(n_r)]

    def body(*refs):
        pos = 0
        r_refs = refs[pos:pos + n_r]; pos += n_r
        p_refs = refs[pos:pos + n_p]; pos += n_p
        c_refs = refs[pos:pos + n_c]; pos += n_c
        a_refs = refs[pos:pos + len(add_keys)]; pos += len(add_keys)
        dr_refs = refs[pos:pos + len(needed)]; pos += len(needed)
        dp_refs = refs[pos:pos + n_p]
        rv = [r[...].astype(_F32) for r in r_refs]
        pv = [p[...] for p in p_refs]
        _, vjp = jax.vjp(lambda *a: _tup(f(*a)), *rv, *pv)
        g = vjp(tuple(c[...].astype(_F32) for c in c_refs))
        for o_ref, j in zip(dr_refs, needed):
            v = g[j]
            if j in adds:
                v = v + a_refs[add_keys.index(j)][...].astype(_F32)
            o_ref[...] = v.astype(o_ref.dtype)
        if n_p:
            @pl.when(pl.program_id(0) == 0)
            def _():
                for dp in dp_refs:
                    dp[...] = jnp.zeros_like(dp)
            for dp, v in zip(dp_refs, g[n_r:]):
                dp[...] += v

    ct_specs = [pl.BlockSpec((ts, c.shape[1]), lambda i: (i, 0)) for c in cts]
    add_specs = [pl.BlockSpec((ts, adds[j].shape[1]), lambda i: (i, 0)) for j in add_keys]
    res = pl.pallas_call(
        body, name=name, grid=(s // ts,),
        in_specs=list(specs) + [pl.BlockSpec(p.shape, lambda i: (0, 0)) for p in params] + ct_specs + add_specs,
        out_specs=[pl.BlockSpec((ts, widths[j]), lambda i: (i, 0)) for j in needed]
        + [pl.BlockSpec(p.shape, lambda i: (0, 0)) for p in params],
        out_shape=[jax.ShapeDtypeStruct((s, widths[j]), row_dtypes[j]) for j in needed]
        + [jax.ShapeDtypeStruct(p.shape, _F32) for p in params],
        compiler_params=_params(("arbitrary",)),
    )(*arrs, *params, *cts, *[adds[j] for j in add_keys])
    return list(res[:len(needed)]), list(res[len(needed):])


_DN = {"nn": (((1,), (0,)), ((), ())), "nt": (((1,), (1,)), ((), ())), "tn": (((0,), (0,)), ((), ()))}


def _make_dot(passes):
    def raw(a, b, kind):
        dn = _DN[kind]
        if passes == 1 or _MXU == _F32:
            return lax.dot_general(a.astype(_MXU), b.astype(_MXU), dn, preferred_element_type=_F32, precision=_PREC)
        a_hi, b_hi = a.astype(_MXU), b.astype(_MXU)
        a_lo = (a - a_hi.astype(_F32)).astype(_MXU)
        b_lo = (b - b_hi.astype(_F32)).astype(_MXU)
        out = lax.dot_general(a_hi, b_hi, dn, preferred_element_type=_F32)
        out = out + lax.dot_general(a_lo, b_hi, dn, preferred_element_type=_F32)
        return out + lax.dot_general(a_hi, b_lo, dn, preferred_element_type=_F32)

    @functools.partial(jax.custom_vjp, nondiff_argnums=(2,))
    def dot(a, b, kind):
        return raw(a, b, kind)

    def fwd(a, b, kind):
        return raw(a, b, kind), (a, b)

    def bwd(kind, res, ct):
        a, b = res
        if kind == "nn":
            return raw(ct, b, "nt"), raw(a, ct, "tn")
        if kind == "nt":
            return raw(ct, b, "nn"), raw(ct, a, "tn")
        return raw(b, ct, "nt"), raw(a, ct, "nn")

    dot.defvjp(fwd, bwd)
    return dot


_dot1 = _make_dot(1)
_dot3 = _make_dot(3)


def _sig(v):
    return 1.0 / (1.0 + jnp.exp(-v))


def _silu(v):
    return v * _sig(v)


def _f_rms(x, g):
    return x * lax.rsqrt(jnp.mean(x * x, axis=-1, keepdims=True) + EPS) * g


def _f_gate(y, xs, z, dexp, g):
    v = (y + dexp * xs) * _silu(z)
    half = SSD_WIDTH // 2
    parts = []
    for grp in range(2):
        vg = v[:, grp * half:(grp + 1) * half]
        parts.append(vg * lax.rsqrt(jnp.mean(vg * vg, axis=-1, keepdims=True) + EPS) * g[:, grp * half:(grp + 1) * half])
    return jnp.concatenate(parts, axis=1)


def _f_ln(u, g, b):
    mu = jnp.mean(u, axis=-1, keepdims=True)
    var = jnp.mean(jnp.square(u - mu), axis=-1, keepdims=True)
    return _silu((u - mu) * lax.rsqrt(var + EPS) * g + b)


def _f_glu(a, g):
    return a * _sig(g)


def _f_swiglu(gate, up):
    return _silu(gate) * up


def _f_att(q, k, v):
    outs = []
    for h in range(X_HEADS):
        sl = slice(h * X_HEAD_DIM, (h + 1) * X_HEAD_DIM)
        s = _dot1(q[:, sl], k[:, sl], "nt") * (X_HEAD_DIM ** -0.5)
        s = s - lax.stop_gradient(jnp.max(s, axis=-1, keepdims=True))
        p = jnp.exp(s)
        p = p / jnp.sum(p, axis=-1, keepdims=True)
        outs.append(_dot1(p, v[:, sl], "nn"))
    return jnp.concatenate(outs, axis=1)


def _loss_bwd(x3, target, g, *, dx_dtype=_F32, name, ts=256):
    s, d = x3.shape

    def f(x, t, gv):
        return 0.5 * jnp.sum(jnp.mean(jnp.square(_f_rms(x, gv) - t), axis=-1))

    def body(x_ref, t_ref, g_ref, dx_ref, dg_ref, l_ref):
        @pl.when(pl.program_id(0) == 0)
        def _():
            dg_ref[...] = jnp.zeros_like(dg_ref)
            l_ref[...] = jnp.zeros_like(l_ref)

        lv, (dx, dg) = jax.value_and_grad(f, argnums=(0, 2))(x_ref[...].astype(_F32), t_ref[...], g_ref[...])
        dx_ref[...] = dx.astype(dx_ref.dtype)
        dg_ref[...] += dg
        l_ref[...] += lv

    row = pl.BlockSpec((ts, d), lambda i: (i, 0))
    return pl.pallas_call(
        body, name=name, grid=(s // ts,),
        in_specs=[row, row, pl.BlockSpec((1, d), lambda i: (0, 0))],
        out_specs=[row, pl.BlockSpec((1, d), lambda i: (0, 0)), pl.BlockSpec((SUBLANES, LANES), lambda i: (0, 0))],
        out_shape=[jax.ShapeDtypeStruct((s, d), dx_dtype), jax.ShapeDtypeStruct((1, d), _F32),
                   jax.ShapeDtypeStruct((SUBLANES, LANES), _F32)],
        compiler_params=_params(("arbitrary",)),
    )(x3, target, g)


_CONV_PAD = 32
_CONV_ROWS = 128
_CONV_CB = 128


def _conv_taps(k_taps):
    groups = {}
    for k in range(k_taps):
        j = k_taps - 1 - k
        groups.setdefault(j % SUBLANES, []).append((k, j))
    return groups


def _conv_window(win, wv, groups, init):
    pad, rows = _CONV_PAD, _CONV_ROWS
    acc = init
    for rot, taps in groups.items():
        rolled = win if rot == 0 else pltpu.roll(win, rot, 0)
        for k, j in taps:
            off = pad - (j - rot)
            acc = acc + rolled[off:off + rows, :] * wv[k:k + 1, :]
    return acc


def _conv_fill(x_refs, xp_ref, s, glu):
    pad, cb = _CONV_PAD, _CONV_CB
    step = _pick(s, (512, 256, _CONV_ROWS))
    xp_ref[0:pad, :] = jnp.zeros((pad, cb), _F32)

    def fill(r, carry):
        base = pl.multiple_of(r * step, step)
        v = x_refs[0][pl.ds(base, step), :].astype(_F32)
        if glu:
            v = v * _sig(x_refs[1][pl.ds(base, step), :].astype(_F32))
        xp_ref[pl.ds(pad + base, step), :] = v
        return carry

    lax.fori_loop(0, s // step, fill, 0)


def _conv_fwd(xs, w, b, k_taps, *, glu=False, act=False, out_dtype=_F32, name):
    s, c = xs[0].shape
    kp = w.shape[0]
    pad, rows, cb = _CONV_PAD, _CONV_ROWS, _CONV_CB
    groups = _conv_taps(k_taps)
    n_in = len(xs)

    def body(*refs):
        x_refs = refs[:n_in]
        w_ref, b_ref, o_ref, xp_ref = refs[n_in:]
        _conv_fill(x_refs, xp_ref, s, glu)
        wv = w_ref[...]
        bias = jnp.broadcast_to(b_ref[...], (rows, cb))

        def chunk(r, carry):
            base = pl.multiple_of(r * rows, rows)
            acc = _conv_window(xp_ref[pl.ds(base, rows + pad), :], wv, groups, bias)
            o_ref[pl.ds(base, rows), :] = (_silu(acc) if act else acc).astype(o_ref.dtype)
            return carry

        lax.fori_loop(0, s // rows, chunk, 0)

    col = pl.BlockSpec((s, cb), lambda i: (0, i))
    return pl.pallas_call(
        body, name=name, grid=(c // cb,),
        in_specs=[col] * n_in + [pl.BlockSpec((kp, cb), lambda i: (0, i)), pl.BlockSpec((1, cb), lambda i: (0, i))],
        out_specs=col, out_shape=jax.ShapeDtypeStruct((s, c), out_dtype),
        scratch_shapes=[pltpu.VMEM((s + pad, cb), _F32)],
        compiler_params=_params(("parallel",)),
    )(*xs, w, b)


def _conv_bwd(xs, w, b, dy, k_taps, *, glu=False, act=False, name):
    s, c = xs[0].shape
    kp = w.shape[0]
    pad, rows, cb = _CONV_PAD, _CONV_ROWS, _CONV_CB
    groups = _conv_taps(k_taps)
    win_rows = rows + pad
    n_in = len(xs)

    def fold(v):
        acc = v[0:SUBLANES, :]
        for i in range(1, rows // SUBLANES):
            acc = acc + v[i * SUBLANES:(i + 1) * SUBLANES, :]
        return acc

    def body(*refs):
        x_refs = refs[:n_in]
        w_ref, b_ref, dy_ref = refs[n_in:n_in + 3]
        dx_refs = refs[n_in + 3:2 * n_in + 3]
        dw_ref, db_ref, xp_ref, dyp_ref, acc_ref, dbacc_ref = refs[2 * n_in + 3:]
        _conv_fill(x_refs, xp_ref, s, glu)
        dyp_ref[s:s + pad, :] = jnp.zeros((pad, cb), _F32)
        acc_ref[...] = jnp.zeros_like(acc_ref)
        dbacc_ref[...] = jnp.zeros_like(dbacc_ref)
        wv = w_ref[...]
        bias = jnp.broadcast_to(b_ref[...], (rows, cb))

        def through_act(r, carry):
            base = pl.multiple_of(r * rows, rows)
            d = dy_ref[pl.ds(base, rows), :].astype(_F32)
            if act:
                pre = _conv_window(xp_ref[pl.ds(base, win_rows), :], wv, groups, bias)
                sg = _sig(pre)
                d = d * (sg * (1.0 + pre * (1.0 - sg)))
            dyp_ref[pl.ds(base, rows), :] = d
            return carry

        lax.fori_loop(0, s // rows, through_act, 0)

        def chunk(r, carry):
            base = pl.multiple_of(r * rows, rows)
            xwin = xp_ref[pl.ds(base, win_rows), :]
            dwin = dyp_ref[pl.ds(base, win_rows), :]
            dyc = dwin[0:rows, :]
            dxacc = jnp.zeros((rows, cb), _F32)
            for rot, taps in groups.items():
                xr = xwin if rot == 0 else pltpu.roll(xwin, rot, 0)
                dr = dwin if rot == 0 else pltpu.roll(dwin, win_rows - rot, 0)
                for k, j in taps:
                    a8 = j - rot
                    dxacc = dxacc + dr[a8:a8 + rows, :] * wv[k:k + 1, :]
                    prod = dyc * xr[pad - a8:pad - a8 + rows, :]
                    acc_ref[k * SUBLANES:(k + 1) * SUBLANES, :] += fold(prod)
            dbacc_ref[...] += fold(dyc)
            if glu:
                av = x_refs[0][pl.ds(base, rows), :].astype(_F32)
                sg = _sig(x_refs[1][pl.ds(base, rows), :].astype(_F32))
                dx_refs[0][pl.ds(base, rows), :] = (dxacc * sg).astype(dx_refs[0].dtype)
                dx_refs[1][pl.ds(base, rows), :] = (dxacc * av * sg * (1.0 - sg)).astype(dx_refs[1].dtype)
            else:
                dx_refs[0][pl.ds(base, rows), :] = dxacc.astype(dx_refs[0].dtype)
            return carry

        lax.fori_loop(0, s // rows, chunk, 0)
        dw_ref[...] = jnp.zeros_like(dw_ref)
        for k in range(k_taps):
            dw_ref[k:k + 1, :] = jnp.sum(acc_ref[k * SUBLANES:(k + 1) * SUBLANES, :], axis=0, keepdims=True)
        db_ref[...] = jnp.sum(dbacc_ref[...], axis=0, keepdims=True)

    col = pl.BlockSpec((s, cb), lambda i: (0, i))
    wspec = pl.BlockSpec((kp, cb), lambda i: (0, i))
    bspec = pl.BlockSpec((1, cb), lambda i: (0, i))
    dx_dtype = xs[0].dtype
    res = pl.pallas_call(
        body, name=name, grid=(c // cb,),
        in_specs=[col] * n_in + [wspec, bspec, col], out_specs=[col] * n_in + [wspec, bspec],
        out_shape=[jax.ShapeDtypeStruct((s, c), dx_dtype)] * n_in
        + [jax.ShapeDtypeStruct((kp, c), _F32), jax.ShapeDtypeStruct((1, c), _F32)],
        scratch_shapes=[pltpu.VMEM((s + pad, cb), _F32), pltpu.VMEM((s + pad, cb), _F32),
                        pltpu.VMEM((kp * SUBLANES, cb), _F32), pltpu.VMEM((SUBLANES, cb), _F32)],
        compiler_params=_params(("parallel",)),
    )(*xs, w, b, dy)
    return list(res[:n_in]), res[n_in], res[n_in + 1]


def _tri_sum(v, lower):
    l = v.shape[0]
    r, c = lax.broadcasted_iota(jnp.int32, (l, l), 0), lax.broadcasted_iota(jnp.int32, (l, l), 1)
    tri = ((r >= c) if lower else (r <= c)).astype(jnp.bfloat16)
    hi = v.astype(jnp.bfloat16)
    r1 = v - hi.astype(_F32)
    mid = r1.astype(jnp.bfloat16)
    lo = (r1 - mid.astype(_F32)).astype(jnp.bfloat16)
    out = jnp.zeros_like(v)
    for part in (hi, mid, lo):
        out = out + lax.dot_general(tri, part, _DN["nn"], preferred_element_type=_F32)
    return out


@jax.custom_vjp
def _cumsum_rows(v):
    return _tri_sum(v, True)


_cumsum_rows.defvjp(lambda v: (_tri_sum(v, True), None), lambda _, ct: (_tri_sum(ct, False),))


def _ssd_chunk(xbc, dtraw, prev, bias, alog):
    l = xbc.shape[0]
    xs = xbc[:, :SSD_WIDTH]
    bm = xbc[:, SSD_WIDTH:SSD_WIDTH + 2 * SSD_STATE]
    cm = xbc[:, SSD_WIDTH + 2 * SSD_STATE:]
    v = dtraw + bias
    dt = jnp.maximum(v, 0.0) + jnp.log1p(jnp.exp(-jnp.abs(v)))
    a_neg = -jnp.exp(alog)
    acs = _cumsum_rows(dt * a_neg)
    acs_t = acs.T
    total = acs[l - 1:l, :]
    row = lax.broadcasted_iota(jnp.int32, (l, l), 0)
    colv = lax.broadcasted_iota(jnp.int32, (l, l), 1)
    causal = row >= colv
    lane_lo = lax.broadcasted_iota(jnp.int32, (l, LANES), 1) < HEAD_DIM
    row_lo = lax.broadcasted_iota(jnp.int32, (LANES, SSD_STATE), 0) < HEAD_DIM

    def pair_lanes(m, h0):
        return jnp.where(lane_lo, m[:, h0:h0 + 1], m[:, h0 + 1:h0 + 2])

    ys, news = [], []
    cb = {}
    for j in range(SSD_HEADS // 2):
        h0 = 2 * j
        grp = h0 // (SSD_HEADS // 2)
        bg = bm[:, grp * SSD_STATE:(grp + 1) * SSD_STATE]
        cg = cm[:, grp * SSD_STATE:(grp + 1) * SSD_STATE]
        if grp not in cb:
            cb[grp] = _dot1(cg, bg, "nt")
        xdt = xs[:, j * LANES:(j + 1) * LANES] * pair_lanes(dt, h0)
        y = jnp.zeros((l, LANES), _F32)
        for hh, mask in ((h0, lane_lo), (h0 + 1, jnp.logical_not(lane_lo))):
            seg = acs[:, hh:hh + 1] - acs_t[hh:hh + 1, :]
            dec = jnp.exp(jnp.where(causal, seg, -jnp.inf))
            y = y + _dot1(cb[grp] * dec, jnp.where(mask, xdt, 0.0), "nn")
        acs_p = pair_lanes(acs, h0)
        prev_p = prev[j * LANES:(j + 1) * LANES, :]
        y = y + _dot1(cg, prev_p, "nt") * jnp.exp(acs_p)
        wgt = jnp.exp(pair_lanes(jnp.broadcast_to(total, (l, LANES)), h0) - acs_p)
        st = _dot1(xdt * wgt, bg, "tn")
        cdec = jnp.exp(jnp.where(row_lo, total[:, h0:h0 + 1], total[:, h0 + 1:h0 + 2]))
        news.append(prev_p * cdec + st)
        ys.append(y)
    return jnp.concatenate(ys, axis=1), jnp.concatenate(news, axis=0)


def _ssd_gate_chunk(xbc, dtraw, prev, bias, alog, z, dexp, g):
    y, new = _ssd_chunk(xbc, dtraw, prev, bias, alog)
    return _f_gate(y, xbc[:, :SSD_WIDTH], z, dexp, g), new


def _ssd_fwd(xbc, dtraw, bias, alog, z, dexp, g, *, name):
    s = xbc.shape[0]
    nc = s // CHUNK
    nstate = SSD_HEADS * HEAD_DIM

    def body(x_ref, dt_ref, b_ref, a_ref, z_ref, d_ref, g_ref, y_ref, st_ref, state_ref):
        @pl.when(pl.program_id(0) == 0)
        def _():
            state_ref[...] = jnp.zeros_like(state_ref)

        prev = state_ref[...]
        st_ref[...] = prev
        y, new = _ssd_gate_chunk(x_ref[...].astype(_F32), dt_ref[...], prev, b_ref[...], a_ref[...], z_ref[...].astype(_F32),
                                 d_ref[...], g_ref[...])
        y_ref[...] = y.astype(y_ref.dtype)
        state_ref[...] = new

    small = pl.BlockSpec((1, LANES), lambda i: (0, 0))
    wide = pl.BlockSpec((1, SSD_WIDTH), lambda i: (0, 0))
    rows = pl.BlockSpec((CHUNK, SSD_WIDTH), lambda i: (i, 0))
    return pl.pallas_call(
        body, name=name, grid=(nc,),
        in_specs=[pl.BlockSpec((CHUNK, XBC_WIDTH), lambda i: (i, 0)), pl.BlockSpec((CHUNK, LANES), lambda i: (i, 0)),
                  small, small, rows, wide, wide],
        out_specs=[rows, pl.BlockSpec((None, nstate, SSD_STATE), lambda i: (i, 0, 0))],
        out_shape=[jax.ShapeDtypeStruct((s, SSD_WIDTH), _MXU), jax.ShapeDtypeStruct((nc, nstate, SSD_STATE), _F32)],
        scratch_shapes=[pltpu.VMEM((nstate, SSD_STATE), _F32)],
        compiler_params=_params(("arbitrary",)),
    )(xbc, dtraw, bias, alog, z, dexp, g)


def _ssd_bwd(xbc, dtraw, states, bias, alog, z, dexp, g, dy, *, name):
    s = xbc.shape[0]
    nc = s // CHUNK
    nstate = SSD_HEADS * HEAD_DIM

    def body(x_ref, dt_ref, st_ref, b_ref, a_ref, z_ref, d_ref, g_ref, dy_ref,
             dx_ref, ddt_ref, db_ref, da_ref, dz_ref, dd_ref, dg_ref, dstate_ref):
        @pl.when(pl.program_id(0) == 0)
        def _():
            dstate_ref[...] = jnp.zeros_like(dstate_ref)
            for acc in (db_ref, da_ref, dd_ref, dg_ref):
                acc[...] = jnp.zeros_like(acc)

        _, vjp = jax.vjp(_ssd_gate_chunk, x_ref[...].astype(_F32), dt_ref[...], st_ref[...], b_ref[...], a_ref[...],
                         z_ref[...].astype(_F32), d_ref[...], g_ref[...])
        dx, ddt, dprev, db, da, dz, dd, dg = vjp((dy_ref[...].astype(_F32), dstate_ref[...]))
        dx_ref[...] = dx.astype(dx_ref.dtype)
        ddt_ref[...] = ddt
        dz_ref[...] = dz.astype(dz_ref.dtype)
        db_ref[...] += db
        da_ref[...] += da
        dd_ref[...] += dd
        dg_ref[...] += dg
        dstate_ref[...] = dprev

    rev = lambda i: (nc - 1 - i, 0)
    small = pl.BlockSpec((1, LANES), lambda i: (0, 0))
    wide = pl.BlockSpec((1, SSD_WIDTH), lambda i: (0, 0))
    rows = pl.BlockSpec((CHUNK, SSD_WIDTH), rev)
    return pl.pallas_call(
        body, name=name, grid=(nc,),
        in_specs=[pl.BlockSpec((CHUNK, XBC_WIDTH), rev), pl.BlockSpec((CHUNK, LANES), rev),
                  pl.BlockSpec((None, nstate, SSD_STATE), lambda i: (nc - 1 - i, 0, 0)), small, small, rows, wide, wide,
                  rows],
        out_specs=[pl.BlockSpec((CHUNK, XBC_WIDTH), rev), pl.BlockSpec((CHUNK, LANES), rev), small, small, rows, wide, wide],
        out_shape=[jax.ShapeDtypeStruct((s, XBC_WIDTH), xbc.dtype), jax.ShapeDtypeStruct((s, LANES), _F32),
                   jax.ShapeDtypeStruct((1, LANES), _F32), jax.ShapeDtypeStruct((1, LANES), _F32),
                   jax.ShapeDtypeStruct((s, SSD_WIDTH), z.dtype), jax.ShapeDtypeStruct((1, SSD_WIDTH), _F32),
                   jax.ShapeDtypeStruct((1, SSD_WIDTH), _F32)],
        scratch_shapes=[pltpu.VMEM((nstate, SSD_STATE), _F32)],
        compiler_params=_params(("arbitrary",)),
    )(xbc, dtraw, states, bias, alog, z, dexp, g, dy)


def _pad_cols(a, width):
    return jnp.pad(a, ((0, 0), (0, width - a.shape[1])))


def _pad_rows(a, rows):
    return jnp.pad(a, ((0, rows - a.shape[0]), (0, 0)))


def _tie(a, token):
    return a + token[0:1, 0:1].astype(a.dtype)


def _local_step(x, mem, target, w, fetch, emit):
    bf = _MXU
    d = D_MODEL
    h = _row_fwd(_f_rms, [x], [w['norm_mix_g']], [(d, bf)], name="f_norm_mix")
    w_in = fetch('in', h)['w_in']
    z_end, xbc_end, dt_end = SSD_WIDTH, SSD_WIDTH + XBC_WIDTH, SSD_WIDTH + XBC_WIDTH + SSD_HEADS
    w_z, w_xbc = w_in[:z_end], w_in[z_end:xbc_end]
    w_dt = _pad_rows(w_in[xbc_end:dt_end], LANES)
    w_a, w_g = w_in[dt_end:dt_end + CF_WIDTH], w_in[dt_end + CF_WIDTH:]
    dt_bias = _pad_cols(w['ssd_dt_bias'], LANES)
    a_log = _pad_cols(w['ssd_A_log'], LANES)
    d_exp = jnp.repeat(w['ssd_D'], HEAD_DIM, axis=1)
    g_final = w['norm_final_g'].reshape(1, D_MODEL)

    z, xbc, dtr, ga, gg = _mm_fan_out(h, [w_z, w_xbc, w_dt, w_a, w_g], tb=True, out_dtypes=[bf, bf, _F32, bf, bf],
                                      name="f_in")
    wc = fetch('conv', xbc)
    ssd_w = _pad_rows(wc['ssd_conv_w'], SUBLANES)
    cf_w = _pad_rows(wc['cf_conv_w'], 32)
    xbc_a = _conv_fwd([xbc], ssd_w, w['ssd_conv_b'], SSD_CONV, act=True, out_dtype=bf, name="f_ssd_conv")
    y_n, states = _ssd_fwd(xbc_a, dtr, dt_bias, a_log, z, d_exp, w['ssd_norm_g'], name="f_ssd")
    u_c = _conv_fwd([ga, gg], cf_w, w['cf_conv_b'], CF_CONV, glu=True, out_dtype=bf, name="f_cf_conv")
    u = _row_fwd(_f_ln, [u_c], [w['cf_ln_g'], w['cf_ln_b']], [(d, bf)], name="f_cf_ln")
    wm = fetch('mid', y_n)
    w_out_y, w_out_u = wm['w_out'][:SSD_WIDTH], wm['w_out'][SSD_WIDTH:]
    x1, hq = _mm_fan_in([(y_n, w_out_y), (u, w_out_u)], add=x, out_dtype=bf, epilogue=(_f_rms, w['norm_xattn_g'], bf),
                        name="f_out")
    q = _mm(hq, wm['w_q'], out_dtype=bf, name="f_q")
    memn = _row_fwd(_f_rms, [mem], [w['norm_mem_g']], [(d, bf)], name="f_norm_mem")
    kv = _mm(memn, wm['w_kv'], name="f_kv")
    k_mat, v_mat = kv[:, :d], kv[:, d:]
    o = _row_fwd(_f_att, [q], [k_mat, v_mat], [(d, bf)], name="f_att")
    x2, hf = _mm_fan_in([(o, wm['w_o'])], add=x1, out_dtype=bf, epilogue=(_f_rms, w['norm_ffn_g'], bf), name="f_o")
    wf = fetch('ffn', hf)
    gate, up, act = _mm_fan_out(hf, [wf['w_gate'], wf['w_up']], tb=True, out_dtypes=[bf, bf], epilogue=_f_swiglu,
                                extra_outs=[(D_FF, bf)], tm=256, name="f_ffn_in")
    x3 = _mm(act, wf['w_down'], add=x2, out_dtype=bf, name="f_down")

    dx3, dg_final, loss = _loss_bwd(x3, target, g_final, dx_dtype=bf, name="b_loss")
    g = {'norm_final_g': dg_final.reshape(d)}

    dact = _mm(dx3, wf['w_down'], tb=True, out_dtype=bf, name="b_down_x")
    dw_down = _mm(act, dx3, ta=True, out_dtype=bf, name="b_down_w")
    def swiglu_bwd(gate_t, up_t, dact_t):
        return jax.vjp(_f_swiglu, gate_t, up_t)[1](dact_t)

    dhf, dgate, dup = _mm_fan_in([(None, wf['w_gate']), (None, wf['w_up'])], prologue=swiglu_bwd, pro_ins=[gate, up, dact],
                                 pro_out_dtypes=[bf, bf], out_dtype=bf, tm=256, name="b_ffn_in_x")
    sent = emit({'w_down': dw_down, 'w_gate': _mm(dgate, hf, ta=True, out_dtype=bf, name="b_gate_w"),
                 'w_up': _mm(dup, hf, ta=True, out_dtype=bf, name="b_up_w")})
    (dx2,), (g['norm_ffn_g'],) = _row_bwd(_f_rms, [x2], [_tie(w['norm_ffn_g'], sent)], [dhf], adds={0: dx3}, row_dtypes=[bf], name="b_norm_ffn")

    do = _mm(dx2, wm['w_o'], tb=True, out_dtype=bf, name="b_o_x")
    dw_o = _mm(o, dx2, ta=True, out_dtype=bf, name="b_o_w")
    (dq,), (dk, dv) = _row_bwd(_f_att, [q], [k_mat, v_mat], [do], row_dtypes=[bf], name="b_att")
    dw_q = _mm(hq, dq, ta=True, out_dtype=bf, name="b_q_w")
    dhq = _mm(dq, wm['w_q'], tb=True, out_dtype=bf, name="b_q_x")
    (dx1,), (g['norm_xattn_g'],) = _row_bwd(_f_rms, [x1], [w['norm_xattn_g']], [dhq], adds={0: dx2}, row_dtypes=[bf], name="b_norm_xattn")
    dkv = jnp.concatenate([dk, dv], axis=1)
    dmemn = _mm(dkv, wm['w_kv'], tb=True, name="b_kv_x")
    _, (g['norm_mem_g'],) = _row_bwd(_f_rms, [mem], [w['norm_mem_g']], [dmemn], need=[False], name="b_norm_mem")
    sent = emit({'w_o': dw_o, 'w_q': dw_q, 'w_kv': _mm(memn, dkv, ta=True, out_dtype=bf, name="b_kv_w")},
                after=g['norm_mem_g'])

    dyn, du = _mm_fan_out(dx1, [w_out_y, w_out_u], tb=True, out_dtypes=[bf, bf], name="b_out_x")
    (du_c,), (g['cf_ln_g'], g['cf_ln_b']) = _row_bwd(_f_ln, [u_c], [_tie(w['cf_ln_g'], sent), w['cf_ln_b']], [du], row_dtypes=[bf], name="b_cf_ln")
    sent = emit({'w_out': jnp.concatenate([_mm(y_n, dx1, ta=True, out_dtype=bf, name="b_out_y_w"), _mm(u, dx1, ta=True, out_dtype=bf, name="b_out_u_w")], axis=0)})
    (dga, dgg), dcf_w, g['cf_conv_b'] = _conv_bwd([ga, gg], cf_w, w['cf_conv_b'], du_c, CF_CONV, glu=True, name="b_cf_conv")
    g['cf_conv_w'] = dcf_w[:CF_CONV]
    dxbc_a, ddtr, ddt_bias, da_log, dz, dd_exp, g['ssd_norm_g'] = _ssd_bwd(
        xbc_a, dtr, states, dt_bias, a_log, z, d_exp, _tie(w['ssd_norm_g'], sent), dyn, name="b_ssd")
    g['ssd_D'] = jnp.sum(dd_exp.reshape(SSD_HEADS, HEAD_DIM), axis=1).reshape(1, SSD_HEADS)
    g['ssd_dt_bias'] = ddt_bias[:, :SSD_HEADS]
    g['ssd_A_log'] = da_log[:, :SSD_HEADS]
    (dxbc,), dssd_w, g['ssd_conv_b'] = _conv_bwd([xbc], ssd_w, w['ssd_conv_b'], dxbc_a, SSD_CONV, act=True, name="b_ssd_conv")
    g['ssd_conv_w'] = dssd_w[:SSD_CONV]

    sent = emit({'w_in': jnp.concatenate([
        _mm(dz, h, ta=True, out_dtype=bf, name="b_in_z_w"), _mm(dxbc, h, ta=True, out_dtype=bf, name="b_in_xbc_w"),
        _mm(ddtr, h, ta=True, out_dtype=bf, name="b_in_dt_w")[:SSD_HEADS],
        _mm(dga, h, ta=True, out_dtype=bf, name="b_in_a_w"), _mm(dgg, h, ta=True, out_dtype=bf, name="b_in_g_w")], axis=0)})
    dh = _mm_fan_in([(dz, w_z), (dxbc, w_xbc), (ddtr, _tie(w_dt, sent)), (dga, w_a), (dgg, w_g)], out_dtype=bf,
                    name="b_in_x")
    (dx,), (g['norm_mix_g'],) = _row_bwd(_f_rms, [x], [w['norm_mix_g']], [dh], adds={0: dx1}, name="b_norm_mix")
    return loss, dx, g


_ANY = pl.BlockSpec(memory_space=pl.ANY)


def _place():
    x, y, c = lax.axis_index("x"), lax.axis_index("y"), lax.axis_index("c")
    return x, y, c


def _all_gather(arrs, *, name):
    n = len(arrs)

    def body(*refs):
        ins, outs = refs[:n], refs[n:2 * n]
        send_sems, recv_sems, local_sems = refs[2 * n:]
        x, y, c = _place()
        me, sibling = (x, y, c), (x, y, 1 - c)
        chips = [(1 - x, y), (x, 1 - y), (1 - x, 1 - y)]

        def slot(a, dev):
            return outs[a].at[4 * dev[0] + 2 * dev[1] + dev[2]]

        def copy(a, k, block, to, src=None):
            return pltpu.make_async_remote_copy(
                src_ref=slot(a, block) if src is None else src, dst_ref=slot(a, block),
                send_sem=send_sems.at[a, k], recv_sem=recv_sems.at[a, k], device_id=to, device_id_type=MESH)

        mine = [pltpu.make_async_copy(ins[a], slot(a, me), local_sems.at[a]) for a in range(n)]
        for cp in mine:
            cp.start()
        first = []
        for a in range(n):
            first.append(copy(a, 0, me, sibling, src=ins[a]))
            first += [copy(a, 1 + j, me, (*chip, c), src=ins[a]) for j, chip in enumerate(chips)]
        for cp in first:
            cp.start()
        passed = []
        for a in range(n):
            for j, chip in enumerate(chips):
                copy(a, 1 + j, (*chip, c), me).wait_recv()
                fwd = copy(a, 4 + j, (*chip, c), sibling)
                fwd.start()
                passed.append(fwd)
        for a in range(n):
            copy(a, 0, sibling, me).wait_recv()
            for j, chip in enumerate(chips):
                copy(a, 4 + j, (*chip, 1 - c), me).wait_recv()
        for cp in first + passed:
            cp.wait_send()
        for cp in mine:
            cp.wait()

    return pl.pallas_call(
        body, name=name, in_specs=[_ANY] * n, out_specs=[_ANY] * n,
        out_shape=[jax.ShapeDtypeStruct((N_DEV,) + a.shape, a.dtype) for a in arrs],
        scratch_shapes=[pltpu.SemaphoreType.DMA((n, 7)), pltpu.SemaphoreType.DMA((n, 7)), pltpu.SemaphoreType.DMA((n,))],
    )(*arrs)


_HBM = pl.BlockSpec(memory_space=pltpu.HBM)
_SEM = pl.BlockSpec(memory_space=pltpu.SEMAPHORE)
_EFFECT = pltpu.SideEffectType.DATAFLOW_SIDE_EFFECTING
_FLIPS = [(dx, dy, dc) for dx in (0, 1) for dy in (0, 1) for dc in (0, 1)][1:]


def _peer(flip, x, y, c):
    return (1 - x if flip[0] else x, 1 - y if flip[1] else y, 1 - c if flip[2] else c)


def _send_start(srcs, blocked, *, after=None, name):
    n = len(srcs)
    lands = [jax.ShapeDtypeStruct(s.shape if blocked else (N_DEV,) + s.shape, s.dtype) for s in srcs]
    n_in = 2 * n + (after is not None)

    def body(*refs):
        src_refs, land_refs = refs[:n], refs[n:2 * n]
        send_sems, recv_sems = refs[n_in], refs[n_in + 1]
        token = refs[-1]
        x, y, c = _place()
        me = 4 * x + 2 * y + c
        for a in range(n):
            for k, flip in enumerate(_FLIPS):
                p = _peer(flip, x, y, c)
                src = src_refs[a].at[4 * p[0] + 2 * p[1] + p[2]] if blocked else src_refs[a]
                pltpu.make_async_remote_copy(
                    src_ref=src, dst_ref=land_refs[a].at[me], send_sem=send_sems.at[7 * a + k], recv_sem=recv_sems.at[7 * a + k],
                    device_id=p, device_id_type=MESH).start()
        token[...] = jnp.zeros_like(token)

    res = pl.pallas_call(
        body, name=name,
        out_shape=(pltpu.SemaphoreType.DMA((7 * n,)), pltpu.SemaphoreType.DMA((7 * n,)),
                   *[pltpu.HBM(s.shape, s.dtype) for s in srcs], *[pltpu.HBM(l.shape, l.dtype) for l in lands],
                   jax.ShapeDtypeStruct((SUBLANES, LANES), _F32)),
        in_specs=[_HBM] * (2 * n) + [_ANY] * (after is not None),
        out_specs=(_SEM, _SEM, *[_HBM] * (2 * n), pl.BlockSpec(memory_space=pltpu.VMEM)),
        input_output_aliases={i: 2 + i for i in range(2 * n)},
        compiler_params=pltpu.CompilerParams(has_side_effects=_EFFECT),
    )(*[pltpu.with_memory_space_constraint(s, pltpu.HBM) for s in srcs],
      *[pltpu.with_memory_space_constraint(lax.empty(l.shape, l.dtype), pltpu.HBM) for l in lands],
      *([after] if after is not None else []))
    return res[0], res[1], list(res[2:2 + n]), list(res[2 + n:2 + 2 * n]), res[-1]


def _send_wait(handles, after, blocked, *, name):
    send_sems, recv_sems, srcs, lands, _ = handles
    n = len(srcs)

    def body(*refs):
        src_refs, land_refs = refs[:n], refs[n:2 * n]
        send_sems, recv_sems = refs[2 * n], refs[2 * n + 1]
        x, y, c = _place()
        for a in range(n):
            for k, flip in enumerate(_FLIPS):
                p = _peer(flip, x, y, c)
                pid = 4 * p[0] + 2 * p[1] + p[2]
                cp = pltpu.make_async_remote_copy(
                    src_ref=src_refs[a].at[pid] if blocked else src_refs[a], dst_ref=land_refs[a].at[pid],
                    send_sem=send_sems.at[7 * a + k], recv_sem=recv_sems.at[7 * a + k], device_id=p, device_id_type=MESH)
                cp.wait_send()
                cp.wait_recv()

    res = pl.pallas_call(
        body, name=name,
        out_shape=tuple(pltpu.HBM(s.shape, s.dtype) for s in srcs + lands),
        in_specs=[_HBM] * (2 * n) + [_SEM, _SEM, _ANY], out_specs=tuple([_HBM] * (2 * n)),
        input_output_aliases={i: i for i in range(2 * n)},
        compiler_params=pltpu.CompilerParams(has_side_effects=_EFFECT),
    )(*srcs, *lands, send_sems, recv_sems, after)
    return list(res[:n]), list(res[n:])


def _adamw(parts, w, m, v, *, own=None, me=None, name):
    p, r, c = parts.shape
    tr = _pick(r, (256, 176, 128, 64, 32, 16, 8))
    if own is not None:
        tc = c if tr < r else _pick(c, (256, 128))
        return _adamw_own(parts, own, me, w, m, v, tr, tc, name=name)

    def body(p_ref, w_ref, m_ref, v_ref, g_ref, d_ref, nm_ref, nv_ref):
        g = p_ref[0].astype(_F32)
        for i in range(1, p):
            g = g + p_ref[i].astype(_F32)
        _adamw_math(g, w_ref, m_ref, v_ref, g_ref, d_ref, nm_ref, nv_ref)

    blk = pl.BlockSpec((tr, c), lambda i: (i, 0))
    return pl.pallas_call(
        body, name=name, grid=(r // tr,),
        in_specs=[pl.BlockSpec((p, tr, c), lambda i: (0, i, 0)), blk, blk, blk], out_specs=[blk] * 4,
        out_shape=[jax.ShapeDtypeStruct((r, c), _F32)] * 4,
        compiler_params=_params(("parallel",)),
    )(parts, w, m, v)


def _adamw_math(g, w_ref, m_ref, v_ref, g_ref, d_ref, nm_ref, nv_ref):
    wv = w_ref[...]
    mn = ADAM_B1 * m_ref[...] + (1.0 - ADAM_B1) * g
    vn = ADAM_B2 * v_ref[...] + (1.0 - ADAM_B2) * jnp.square(g)
    m_hat = mn / (1.0 - ADAM_B1 ** ADAM_STEP)
    v_hat = vn / (1.0 - ADAM_B2 ** ADAM_STEP)
    g_ref[...] = g
    d_ref[...] = -ADAM_LR * (m_hat / (jnp.sqrt(v_hat) + ADAM_EPS) + ADAM_WD * wv)
    nm_ref[...] = mn
    nv_ref[...] = vn


def _adamw_own(parts, own, me, w, m, v, tr, tc, *, name):
    p, r, c = parts.shape

    def body(me_ref, p_ref, own_ref, w_ref, m_ref, v_ref, g_ref, d_ref, nm_ref, nv_ref):
        mine = own_ref[...].astype(_F32)
        g = jnp.where(me_ref[0] == 0, mine, p_ref[0].astype(_F32))
        for i in range(1, p):
            g = g + jnp.where(me_ref[0] == i, mine, p_ref[i].astype(_F32))
        _adamw_math(g, w_ref, m_ref, v_ref, g_ref, d_ref, nm_ref, nv_ref)

    blk = pl.BlockSpec((tr, tc), lambda i, j, me_ref: (i, j))
    grid_spec = pltpu.PrefetchScalarGridSpec(
        num_scalar_prefetch=1, grid=(r // tr, c // tc),
        in_specs=[pl.BlockSpec((p, tr, tc), lambda i, j, me_ref: (0, i, j)),
                  pl.BlockSpec((None, tr, tc), lambda i, j, me_ref: (me_ref[0], i, j)), blk, blk, blk],
        out_specs=[blk] * 4)
    return pl.pallas_call(
        body, name=name, grid_spec=grid_spec, out_shape=[jax.ShapeDtypeStruct((r, c), _F32)] * 4,
        compiler_params=_params(("parallel", "parallel")),
    )(me.reshape(1).astype(jnp.int32), parts, own, w, m, v)


def _adamw_rows(g_row, offsets, ws, ms, vs, *, name):
    k = len(ws)

    def body(*refs):
        g_ref, w_refs, m_refs, v_refs = refs[0], refs[1:1 + k], refs[1 + k:1 + 2 * k], refs[1 + 2 * k:1 + 3 * k]
        outs = refs[1 + 3 * k:]
        for i in range(k):
            gi = g_ref[:, offsets[i]:offsets[i] + ws[i].shape[1]]
            _adamw_math(gi, w_refs[i], m_refs[i], v_refs[i], *outs[4 * i:4 * i + 4])

    return pl.pallas_call(
        body, name=name, out_shape=[jax.ShapeDtypeStruct(w.shape, _F32) for w in ws for _ in range(4)],
    )(g_row, *ws, *ms, *vs)


def _sum_parts(parts, *, name):
    p, r, c = parts.shape

    def body(p_ref, o_ref):
        g = p_ref[0].astype(_F32)
        for i in range(1, p):
            g = g + p_ref[i].astype(_F32)
        o_ref[...] = g

    return pl.pallas_call(body, name=name, out_shape=jax.ShapeDtypeStruct((r, c), _F32))(parts)


def _pack(vals, rows):
    flat = jnp.concatenate([v.reshape(-1) for v in vals])
    return jnp.pad(flat, (0, rows * LANES - flat.shape[0])).reshape(rows, LANES)


def _unpack(packed, shapes):
    flat = packed.reshape(-1)
    out, pos = [], 0
    for shp in shapes:
        size = math.prod(shp)
        out.append(flat[pos:pos + size].reshape(shp))
        pos += size
    return out


def _pack_rows(shapes):
    total = sum(math.prod(s) for s in shapes)
    return -(-total // (LANES * SUBLANES)) * SUBLANES


def kernel(x, mem, norm_mix_g, w_in, ssd_conv_w, ssd_conv_b, ssd_dt_bias, ssd_A_log, ssd_D, ssd_norm_g, cf_conv_w, cf_conv_b, cf_ln_g, cf_ln_b, w_out, norm_xattn_g, norm_mem_g, w_q, w_kv, w_o, norm_ffn_g, w_gate, w_up, w_down, norm_final_g, loss_target, m_norm_mix_g, m_w_in, m_ssd_conv_w, m_ssd_conv_b, m_ssd_dt_bias, m_ssd_A_log, m_ssd_D, m_ssd_norm_g, m_cf_conv_w, m_cf_conv_b, m_cf_ln_g, m_cf_ln_b, m_w_out, m_norm_xattn_g, m_norm_mem_g, m_w_q, m_w_kv, m_w_o, m_norm_ffn_g, m_w_gate, m_w_up, m_w_down, m_norm_final_g, v_norm_mix_g, v_w_in, v_ssd_conv_w, v_ssd_conv_b, v_ssd_dt_bias, v_ssd_A_log, v_ssd_D, v_ssd_norm_g, v_cf_conv_w, v_cf_conv_b, v_cf_ln_g, v_cf_ln_b, v_w_out, v_norm_xattn_g, v_norm_mem_g, v_w_q, v_w_kv, v_w_o, v_norm_ffn_g, v_w_gate, v_w_up, v_w_down, v_norm_final_g):
    args = dict(locals())
    wts = {n: args[n] for n in WEIGHT_NAMES}
    mom = {n: args["m_" + n] for n in WEIGHT_NAMES}
    var = {n: args["v_" + n] for n in WEIGHT_NAMES}
    me = 4 * lax.axis_index("x") + 2 * lax.axis_index("y") + lax.axis_index("c")

    groups = {'in': ['w_in'], 'conv': ['ssd_conv_w', 'cf_conv_w'], 'mid': ['w_out', 'w_q', 'w_kv', 'w_o'],
              'ffn': ['w_gate', 'w_up', 'w_down']}
    def shard(n, a):
        return jnp.transpose(a[0], (1, 0)) if n in TRANSPOSED else a[0]

    gathers, started = {}, None
    for grp, names in groups.items():
        shards = [wts[n][0] if grp == 'conv' else shard(n, wts[n]).astype(_MXU) for n in names]
        gathers[grp] = _send_start(shards, False, after=started, name="gather_%s_start" % grp)
        started = gathers[grp][4]

    def fetch(grp, after):
        srcs, lands = _send_wait(gathers[grp], started if after is None else after, False, name="gather_%s_wait" % grp)
        out = {}
        for n, own, gth in zip(groups[grp], srcs, lands):
            gth = lax.dynamic_update_slice_in_dim(gth, own[None], me, axis=0)
            if n == 'w_kv' or grp == 'conv':
                out[n] = jnp.transpose(gth, (1, 0, 2)).reshape(gth.shape[1], N_DEV * gth.shape[2])
            else:
                out[n] = gth.reshape(N_DEV * gth.shape[1], gth.shape[2])
        return out

    exchanges = []

    def emit(grads, after=None):
        blocks = []
        for n, gw in grads.items():
            if n == 'w_kv':
                gw = jnp.transpose(gw.reshape(gw.shape[0], N_DEV, gw.shape[1] // N_DEV), (1, 0, 2))
            else:
                gw = gw.reshape(N_DEV, gw.shape[0] // N_DEV, gw.shape[1])
            blocks.append(gw.astype(jnp.bfloat16))
        first = next(iter(grads))
        exchanges.append((list(grads), _send_start(blocks, True, after=after, name="exchange_%s_start" % first), first))
        return exchanges[-1][1][4]

    full = {n: wts[n] for n in WEIGHT_NAMES if n not in BIG and n not in groups['conv']}
    full['norm_mix_g'] = _tie(norm_mix_g, started)

    loss_blk, grad_x, g = _local_step(x[0], mem[0], loss_target[0], full, fetch, emit)

    small = [n for n in WEIGHT_NAMES if n not in BIG]
    g['loss'] = loss_blk[0:1, 0:1]
    items = small + ['loss']
    size = {n: math.prod(g[n].shape) for n in items}
    seg = {n: -(-size[n] // LANES) * LANES for n in items}
    off, pos = {}, 0
    for n in items:
        off[n], pos = pos, pos + seg[n]
    rows = -(-pos // (LANES * SUBLANES)) * SUBLANES
    flat = jnp.concatenate([jnp.pad(g[n].reshape(-1), (0, seg[n] - size[n])) for n in items]
                           + [jnp.zeros((rows * LANES - pos,), _F32)])
    small_sent = _send_start([flat.reshape(rows, LANES)], False, name="gather_small_start")

    out_g, out_d, out_m, out_v = {}, {}, {}, {}
    done = small_sent[4]
    for names, handles, first in exchanges:
        srcs, lands = _send_wait(handles, done, True, name="exchange_%s_wait" % first)
        for n, own, parts in zip(names, srcs, lands):
            res = _adamw(parts, shard(n, wts[n]), shard(n, mom[n]), shard(n, var[n]), own=own, me=me, name="adamw_" + n)
            out_g[n], out_d[n], out_m[n], out_v[n] = [(jnp.transpose(r, (1, 0)) if n in TRANSPOSED else r)[None] for r in res]
            done = res[0]

    srcs, lands = _send_wait(small_sent, out_g[exchanges[-1][0][-1]], False, name="gather_small_wait")
    small_parts = lax.dynamic_update_slice_in_dim(lands[0], srcs[0][None], me, axis=0)
    g_row = _sum_parts(small_parts, name="sum_small_grads").reshape(1, rows * LANES)
    loss = g_row[0, off['loss']]
    rep = [n for n in small if n not in groups['conv']]
    as_row = lambda a: a.reshape(1, -1)
    res = _adamw_rows(g_row, [off[n] for n in rep], [as_row(wts[n]) for n in rep], [as_row(mom[n]) for n in rep],
                      [as_row(var[n]) for n in rep], name="adamw_small")
    for i, n in enumerate(rep):
        out_g[n], out_d[n], out_m[n], out_v[n] = [r.reshape(wts[n].shape) for r in res[4 * i:4 * i + 4]]
    for n in groups['conv']:
        k_taps, width = g[n].shape
        g_full = g_row[0, off[n]:off[n] + size[n]].reshape(k_taps, width)
        g_mine = lax.dynamic_slice_in_dim(g_full, me * (width // N_DEV), width // N_DEV, axis=1)
        res = _adamw(g_mine[None], wts[n][0], mom[n][0], var[n][0], name="adamw_" + n)
        out_g[n], out_d[n], out_m[n], out_v[n] = [r[None] for r in res]

    return (loss, grad_x[None], *[out_g[n] for n in WEIGHT_NAMES], *[out_d[n] for n in WEIGHT_NAMES],
            *[out_m[n] for n in WEIGHT_NAMES], *[out_v[n] for n in WEIGHT_NAMES])
```

```python
import functools
import math

import jax
import jax.numpy as jnp
from jax import lax
from jax.experimental import pallas as pl
from jax.experimental.pallas import tpu as pltpu

_F32 = jnp.float32
_MXU = jnp.bfloat16
_PREC = None
_VMEM_LIMIT = 56 * 1024 * 1024

D_MODEL = 1024
HEAD_DIM = 64
SSD_HEADS = 16
SSD_WIDTH = 1024
SSD_STATE = 128
SSD_CONV = 4
CHUNK = 128
XBC_WIDTH = 1536
CF_WIDTH = 1024
CF_CONV = 31
X_HEADS = 4
X_HEAD_DIM = 256
D_FF = 2816
EPS = 1e-6
N_DEV = 8
LANES = 128
SUBLANES = 8

ADAM_LR = 0.001
ADAM_B1 = 0.9
ADAM_B2 = 0.999
ADAM_EPS = 1e-08
ADAM_WD = 0.01
ADAM_STEP = 10

MESH = pl.DeviceIdType.MESH
WEIGHT_NAMES = ['norm_mix_g', 'w_in', 'ssd_conv_w', 'ssd_conv_b', 'ssd_dt_bias', 'ssd_A_log', 'ssd_D', 'ssd_norm_g',
                'cf_conv_w', 'cf_conv_b', 'cf_ln_g', 'cf_ln_b', 'w_out', 'norm_xattn_g', 'norm_mem_g', 'w_q', 'w_kv',
                'w_o', 'norm_ffn_g', 'w_gate', 'w_up', 'w_down', 'norm_final_g']
BIG = ['w_in', 'w_out', 'w_q', 'w_kv', 'w_o', 'w_gate', 'w_up', 'w_down']
TRANSPOSED = ('w_in', 'w_gate', 'w_up')


def _params(sem=None):
    return pltpu.CompilerParams(dimension_semantics=sem, vmem_limit_bytes=_VMEM_LIMIT)


def _pick(n, cands):
    for c in cands:
        if n % c == 0:
            return c
    return n


def _mm(a, b, *, ta=False, tb=False, add=None, out_dtype=_F32, name):
    (kdim, m) = a.shape if ta else a.shape[::-1]
    (n, k2) = b.shape if tb else b.shape[::-1]
    assert kdim == k2, (a.shape, b.shape, ta, tb)
    if ta:
        tm = m if m <= 1024 else _pick(m, (1408, 1024, 512, 256, 128))
        tn = n if n <= 1536 else _pick(n, (1408, 1024, 512, 256, 128))
        size = lambda arr: jnp.dtype(arr.dtype).itemsize
        fits = lambda t: (2 * t * (tm * size(a) + tn * size(b)) + tm * tn * (4 + 2 * jnp.dtype(out_dtype).itemsize)
                          <= _VMEM_LIMIT * 3 // 4)
        tk = next((t for t in (2048, 1024, 512, 256, 128) if kdim % t == 0 and fits(t)), _pick(kdim, (128,)))
    else:
        tm = _pick(m, (512, 256, 128))
        tn = n if n <= 2816 else _pick(n, (1408, 1024, 512, 256, 128))
        tk = kdim if kdim <= 2816 else _pick(kdim, (1408, 1024, 512, 256, 128))
    nk = kdim // tk
    dn = (((0 if ta else 1,), (1 if tb else 0,)), ((), ()))

    def body(*refs):
        a_ref, b_ref = refs[0], refs[1]
        add_ref = refs[2] if add is not None else None
        o_ref = refs[3 if add is not None else 2]
        acc_ref = refs[-1]
        k = pl.program_id(2)
        prod = lax.dot_general(a_ref[...].astype(_MXU), b_ref[...].astype(_MXU), dn,
                               preferred_element_type=_F32, precision=_PREC)

        def finish(r):
            if add_ref is not None:
                r = r + add_ref[...].astype(_F32)
            o_ref[...] = r.astype(o_ref.dtype)

        if nk == 1:
            finish(prod)
            return

        @pl.when(k == 0)
        def _():
            acc_ref[...] = prod

        @pl.when(jnp.logical_and(k > 0, k < nk - 1))
        def _():
            acc_ref[...] += prod

        @pl.when(k == nk - 1)
        def _():
            finish(acc_ref[...] + prod)

    a_spec = pl.BlockSpec((tk, tm), lambda i, j, k: (k, i)) if ta else pl.BlockSpec((tm, tk), lambda i, j, k: (i, k))
    b_spec = pl.BlockSpec((tn, tk), lambda i, j, k: (j, k)) if tb else pl.BlockSpec((tk, tn), lambda i, j, k: (k, j))
    o_spec = pl.BlockSpec((tm, tn), lambda i, j, k: (i, j))
    ins, specs = [a, b], [a_spec, b_spec]
    if add is not None:
        ins.append(add)
        specs.append(o_spec)
    return pl.pallas_call(
        body, name=name, grid=(m // tm, n // tn, nk), in_specs=specs, out_specs=o_spec,
        out_shape=jax.ShapeDtypeStruct((m, n), out_dtype),
        scratch_shapes=[pltpu.VMEM((tm, tn), _F32)] if nk > 1 else [],
        compiler_params=_params(("parallel", "parallel", "arbitrary")),
    )(*ins)


def _resident(shape):
    return pl.BlockSpec(shape, lambda i: (0,) * len(shape), pipeline_mode=pl.Buffered(1))


def _mm_fan_out(a, bs, *, tb, out_dtypes, epilogue=None, extra_outs=(), tm=512, name):
    m, kdim = a.shape
    tm = min(tm, m)
    ns = [b.shape[0] if tb else b.shape[1] for b in bs]
    nb = len(bs)
    kind = "nt" if tb else "nn"

    def body(*refs):
        a_ref, b_refs, o_refs = refs[0], refs[1:1 + nb], refs[1 + nb:]
        av = a_ref[...].astype(_MXU)
        prods = [lax.dot_general(av, b[...].astype(_MXU), _DN[kind], preferred_element_type=_F32, precision=_PREC)
                 for b in b_refs]
        for o_ref, p in zip(o_refs[:nb], prods):
            o_ref[...] = p.astype(o_ref.dtype)
        if epilogue is not None:
            for o_ref, v in zip(o_refs[nb:], _tup(epilogue(*prods))):
                o_ref[...] = v.astype(o_ref.dtype)

    widths = ns + [w for w, _ in extra_outs]
    dtypes = list(out_dtypes) + [dt for _, dt in extra_outs]
    return pl.pallas_call(
        body, name=name, grid=(m // tm,),
        in_specs=[pl.BlockSpec((tm, kdim), lambda i: (i, 0))] + [_resident(b.shape) for b in bs],
        out_specs=[pl.BlockSpec((tm, w), lambda i: (i, 0)) for w in widths],
        out_shape=[jax.ShapeDtypeStruct((m, w), dt) for w, dt in zip(widths, dtypes)],
        compiler_params=_params(("parallel",)),
    )(a, *bs)


def _mm_fan_in(pairs, *, add=None, out_dtype=_F32, prologue=None, pro_ins=(), pro_out_dtypes=(), epilogue=None,
               tm=512, name):
    bs = [b for _, b in pairs]
    nb = len(bs)
    n = bs[0].shape[1]
    rows_in = list(pro_ins) if prologue is not None else [a for a, _ in pairs]
    m = rows_in[0].shape[0]
    tm = min(tm, m)
    n_r = len(rows_in)

    def body(*refs):
        r_refs, b_refs = refs[:n_r], refs[n_r:n_r + nb]
        pos = n_r + nb
        add_ref = refs[pos] if add is not None else None
        pos += add is not None
        epi_ref = refs[pos] if epilogue is not None else None
        pos += epilogue is not None
        o_ref, po_refs = refs[pos], refs[pos + 1:]
        if prologue is not None:
            a_vals = _tup(prologue(*[r[...].astype(_F32) for r in r_refs]))
            for po, v in zip(po_refs, a_vals):
                po[...] = v.astype(po.dtype)
        else:
            a_vals = [r[...] for r in r_refs]
        acc = None
        for av, b in zip(a_vals, b_refs):
            p = lax.dot_general(av.astype(_MXU), b[...].astype(_MXU), _DN["nn"], preferred_element_type=_F32,
                                precision=_PREC)
            acc = p if acc is None else acc + p
        if add_ref is not None:
            acc = acc + add_ref[...].astype(_F32)
        o_ref[...] = acc.astype(o_ref.dtype)
        if epilogue is not None:
            po_refs[-1][...] = epilogue[0](acc, epi_ref[...]).astype(po_refs[-1].dtype)

    row = lambda w: pl.BlockSpec((tm, w), lambda i: (i, 0))
    ins = rows_in + bs + ([add] if add is not None else []) + ([epilogue[1]] if epilogue is not None else [])
    in_specs = ([row(r.shape[1]) for r in rows_in] + [_resident(b.shape) for b in bs]
                + ([row(n)] if add is not None else []) + ([_resident(epilogue[1].shape)] if epilogue is not None else []))
    extra = [(b.shape[0], dt) for b, dt in zip(bs, pro_out_dtypes)] if prologue is not None else []
    if epilogue is not None:
        extra.append((n, epilogue[2]))
    res = pl.pallas_call(
        body, name=name, grid=(m // tm,), in_specs=in_specs,
        out_specs=[row(n)] + [row(w) for w, _ in extra],
        out_shape=[jax.ShapeDtypeStruct((m, n), out_dtype)] + [jax.ShapeDtypeStruct((m, w), dt) for w, dt in extra],
        compiler_params=_params(("parallel",)),
    )(*ins)
    return res if extra else res[0]


def _row_spec(r, ts):
    if isinstance(r, tuple):
        arr, width, cblk = r
        return arr, pl.BlockSpec((ts, width), lambda i, cblk=cblk: (i, cblk))
    return r, pl.BlockSpec((ts, r.shape[1]), lambda i: (i, 0))


def _tup(v):
    return tuple(v) if isinstance(v, (tuple, list)) else (v,)


def _row_fwd(f, rows, params, outs, *, name, ts=256):
    s = (rows[0][0] if isinstance(rows[0], tuple) else rows[0]).shape[0]
    ts = min(ts, s)
    arrs, specs = zip(*[_row_spec(r, ts) for r in rows])
    n_r, n_p = len(rows), len(params)

    def body(*refs):
        rv = [r[...].astype(_F32) for r in refs[:n_r]]
        pv = [p[...] for p in refs[n_r:n_r + n_p]]
        res = _tup(f(*rv, *pv))
        for o_ref, v in zip(refs[n_r + n_p:], res):
            o_ref[...] = v.astype(o_ref.dtype)

    res = pl.pallas_call(
        body, name=name, grid=(s // ts,),
        in_specs=list(specs) + [pl.BlockSpec(p.shape, lambda i: (0, 0)) for p in params],
        out_specs=[pl.BlockSpec((ts, w), lambda i: (i, 0)) for w, _ in outs],
        out_shape=[jax.ShapeDtypeStruct((s, w), dt) for w, dt in outs],
        compiler_params=_params(("parallel",)),
    )(*arrs, *params)
    return res[0] if len(outs) == 1 else res


def _row_bwd(f, rows, params, cts, *, need=None, adds=None, row_dtypes=None, name, ts=256):
    s = (rows[0][0] if isinstance(rows[0], tuple) else rows[0]).shape[0]
    ts = min(ts, s)
    arrs, specs = zip(*[_row_spec(r, ts) for r in rows])
    n_r, n_p, n_c = len(rows), len(params), len(cts)
    need = [True] * n_r if need is None else need
    adds = {} if adds is None else adds
    add_keys = sorted(adds)
    row_dtypes = [_F32] * n_r if row_dtypes is None else row_dtypes
    needed = [j for j in range(n_r) if need[j]]
    widths = [specs[j].block_shape[1] for j in range(n_r)]

    def body(*refs):
        pos = 0
        r_refs = refs[pos:pos + n_r]; pos += n_r
        p_refs = refs[pos:pos + n_p]; pos += n_p
        c_refs = refs[pos:pos + n_c]; pos += n_c
        a_refs = refs[pos:pos + len(add_keys)]; pos += len(add_keys)
        dr_refs = refs[pos:pos + len(needed)]; pos += len(needed)
        dp_refs = refs[pos:pos + n_p]
        rv = [r[...].astype(_F32) for r in r_refs]
        pv = [p[...] for p in p_refs]
        _, vjp = jax.vjp(lambda *a: _tup(f(*a)), *rv, *pv)
        g = vjp(tuple(c[...].astype(_F32) for c in c_refs))
        for o_ref, j in zip(dr_refs, needed):
            v = g[j]
            if j in adds:
                v = v + a_refs[add_keys.index(j)][...].astype(_F32)
            o_ref[...] = v.astype(o_ref.dtype)
        if n_p:
            @pl.when(pl.program_id(0) == 0)
            def _():
                for dp in dp_refs:
                    dp[...] = jnp.zeros_like(dp)
            for dp, v in zip(dp_refs, g[n_r:]):
                dp[...] += v

    ct_specs = [pl.BlockSpec((ts, c.shape[1]), lambda i: (i, 0)) for c in cts]
    add_specs = [pl.BlockSpec((ts, adds[j].shape[1]), lambda i: (i, 0)) for j in add_keys]
    res = pl.pallas_call(
        body, name=name, grid=(s // ts,),
        in_specs=list(specs) + [pl.BlockSpec(p.shape, lambda i: (0, 0)) for p in params] + ct_specs + add_specs,
        out_specs=[pl.BlockSpec((ts, widths[j]), lambda i: (i, 0)) for j in needed]
        + [pl.BlockSpec(p.shape, lambda i: (0, 0)) for p in params],
        out_shape=[jax.ShapeDtypeStruct((s, widths[j]), row_dtypes[j]) for j in needed]
        + [jax.ShapeDtypeStruct(p.shape, _F32) for p in params],
        compiler_params=_params(("arbitrary",)),
    )(*arrs, *params, *cts, *[adds[j] for j in add_keys])
    return list(res[:len(needed)]), list(res[len(needed):])


_DN = {"nn": (((1,), (0,)), ((), ())), "nt": (((1,), (1,)), ((), ())), "tn": (((0,), (0,)), ((), ()))}


def _make_dot(passes):
    def raw(a, b, kind):
        dn = _DN[kind]
        if passes == 1 or _MXU == _F32:
            return lax.dot_general(a.astype(_MXU), b.astype(_MXU), dn, preferred_element_type=_F32, precision=_PREC)
        a_hi, b_hi = a.astype(_MXU), b.astype(_MXU)
        a_lo = (a - a_hi.astype(_F32)).astype(_MXU)
        b_lo = (b - b_hi.astype(_F32)).astype(_MXU)
        out = lax.dot_general(a_hi, b_hi, dn, preferred_element_type=_F32)
        out = out + lax.dot_general(a_lo, b_hi, dn, preferred_element_type=_F32)
        return out + lax.dot_general(a_hi, b_lo, dn, preferred_element_type=_F32)

    @functools.partial(jax.custom_vjp, nondiff_argnums=(2,))
    def dot(a, b, kind):
        return raw(a, b, kind)

    def fwd(a, b, kind):
        return raw(a, b, kind), (a, b)

    def bwd(kind, res, ct):
        a, b = res
        if kind == "nn":
            return raw(ct, b, "nt"), raw(a, ct, "tn")
        if kind == "nt":
            return raw(ct, b, "nn"), raw(ct, a, "tn")
        return raw(b, ct, "nt"), raw(a, ct, "nn")

    dot.defvjp(fwd, bwd)
    return dot


_dot1 = _make_dot(1)
_dot3 = _make_dot(3)


def _sig(v):
    return 1.0 / (1.0 + jnp.exp(-v))


def _silu(v):
    return v * _sig(v)


def _f_rms(x, g):
    return x * lax.rsqrt(jnp.mean(x * x, axis=-1, keepdims=True) + EPS) * g


def _f_gate(y, xs, z, dexp, g):
    v = (y + dexp * xs) * _silu(z)
    half = SSD_WIDTH // 2
    parts = []
    for grp in range(2):
        vg = v[:, grp * half:(grp + 1) * half]
        parts.append(vg * lax.rsqrt(jnp.mean(vg * vg, axis=-1, keepdims=True) + EPS) * g[:, grp * half:(grp + 1) * half])
    return jnp.concatenate(parts, axis=1)


def _f_ln(u, g, b):
    mu = jnp.mean(u, axis=-1, keepdims=True)
    var = jnp.mean(jnp.square(u - mu), axis=-1, keepdims=True)
    return _silu((u - mu) * lax.rsqrt(var + EPS) * g + b)


def _f_glu(a, g):
    return a * _sig(g)


def _f_swiglu(gate, up):
    return _silu(gate) * up


def _f_att(q, k, v):
    outs = []
    for h in range(X_HEADS):
        sl = slice(h * X_HEAD_DIM, (h + 1) * X_HEAD_DIM)
        s = _dot1(q[:, sl], k[:, sl], "nt") * (X_HEAD_DIM ** -0.5)
        s = s - lax.stop_gradient(jnp.max(s, axis=-1, keepdims=True))
        p = jnp.exp(s)
        p = p / jnp.sum(p, axis=-1, keepdims=True)
        outs.append(_dot1(p, v[:, sl], "nn"))
    return jnp.concatenate(outs, axis=1)


def _loss_bwd(x3, target, g, *, dx_dtype=_F32, name, ts=256):
    s, d = x3.shape

    def f(x, t, gv):
        return 0.5 * jnp.sum(jnp.mean(jnp.square(_f_rms(x, gv) - t), axis=-1))

    def body(x_ref, t_ref, g_ref, dx_ref, dg_ref, l_ref):
        @pl.when(pl.program_id(0) == 0)
        def _():
            dg_ref[...] = jnp.zeros_like(dg_ref)
            l_ref[...] = jnp.zeros_like(l_ref)

        lv, (dx, dg) = jax.value_and_grad(f, argnums=(0, 2))(x_ref[...].astype(_F32), t_ref[...], g_ref[...])
        dx_ref[...] = dx.astype(dx_ref.dtype)
        dg_ref[...] += dg
        l_ref[...] += lv

    row = pl.BlockSpec((ts, d), lambda i: (i, 0))
    return pl.pallas_call(
        body, name=name, grid=(s // ts,),
        in_specs=[row, row, pl.BlockSpec((1, d), lambda i: (0, 0))],
        out_specs=[row, pl.BlockSpec((1, d), lambda i: (0, 0)), pl.BlockSpec((SUBLANES, LANES), lambda i: (0, 0))],
        out_shape=[jax.ShapeDtypeStruct((s, d), dx_dtype), jax.ShapeDtypeStruct((1, d), _F32),
                   jax.ShapeDtypeStruct((SUBLANES, LANES), _F32)],
        compiler_params=_params(("arbitrary",)),
    )(x3, target, g)


_CONV_PAD = 32
_CONV_ROWS = 128
_CONV_CB = 128


def _conv_taps(k_taps):
    groups = {}
    for k in range(k_taps):
        j = k_taps - 1 - k
        groups.setdefault(j % SUBLANES, []).append((k, j))
    return groups


def _conv_window(win, wv, groups, init):
    pad, rows = _CONV_PAD, _CONV_ROWS
    acc = init
    for rot, taps in groups.items():
        rolled = win if rot == 0 else pltpu.roll(win, rot, 0)
        for k, j in taps:
            off = pad - (j - rot)
            acc = acc + rolled[off:off + rows, :] * wv[k:k + 1, :]
    return acc


def _conv_fill(x_refs, xp_ref, s, glu):
    pad, cb = _CONV_PAD, _CONV_CB
    step = _pick(s, (512, 256, _CONV_ROWS))
    xp_ref[0:pad, :] = jnp.zeros((pad, cb), _F32)

    def fill(r, carry):
        base = pl.multiple_of(r * step, step)
        v = x_refs[0][pl.ds(base, step), :].astype(_F32)
        if glu:
            v = v * _sig(x_refs[1][pl.ds(base, step), :].astype(_F32))
        xp_ref[pl.ds(pad + base, step), :] = v
        return carry

    lax.fori_loop(0, s // step, fill, 0)


def _conv_fwd(xs, w, b, k_taps, *, glu=False, act=False, out_dtype=_F32, name):
    s, c = xs[0].shape
    kp = w.shape[0]
    pad, rows, cb = _CONV_PAD, _CONV_ROWS, _CONV_CB
    groups = _conv_taps(k_taps)
    n_in = len(xs)

    def body(*refs):
        x_refs = refs[:n_in]
        w_ref, b_ref, o_ref, xp_ref = refs[n_in:]
        _conv_fill(x_refs, xp_ref, s, glu)
        wv = w_ref[...]
        bias = jnp.broadcast_to(b_ref[...], (rows, cb))

        def chunk(r, carry):
            base = pl.multiple_of(r * rows, rows)
            acc = _conv_window(xp_ref[pl.ds(base, rows + pad), :], wv, groups, bias)
            o_ref[pl.ds(base, rows), :] = (_silu(acc) if act else acc).astype(o_ref.dtype)
            return carry

        lax.fori_loop(0, s // rows, chunk, 0)

    col = pl.BlockSpec((s, cb), lambda i: (0, i))
    return pl.pallas_call(
        body, name=name, grid=(c // cb,),
        in_specs=[col] * n_in + [pl.BlockSpec((kp, cb), lambda i: (0, i)), pl.BlockSpec((1, cb), lambda i: (0, i))],
        out_specs=col, out_shape=jax.ShapeDtypeStruct((s, c), out_dtype),
        scratch_shapes=[pltpu.VMEM((s + pad, cb), _F32)],
        compiler_params=_params(("parallel",)),
    )(*xs, w, b)


def _conv_bwd(xs, w, b, dy, k_taps, *, glu=False, act=False, name):
    s, c = xs[0].shape
    kp = w.shape[0]
    pad, rows, cb = _CONV_PAD, _CONV_ROWS, _CONV_CB
    groups = _conv_taps(k_taps)
    win_rows = rows + pad
    n_in = len(xs)

    def fold(v):
        acc = v[0:SUBLANES, :]
        for i in range(1, rows // SUBLANES):
            acc = acc + v[i * SUBLANES:(i + 1) * SUBLANES, :]
        return acc

    def body(*refs):
        x_refs = refs[:n_in]
        w_ref, b_ref, dy_ref = refs[n_in:n_in + 3]
        dx_refs = refs[n_in + 3:2 * n_in + 3]
        dw_ref, db_ref, xp_ref, dyp_ref, acc_ref, dbacc_ref = refs[2 * n_in + 3:]
        _conv_fill(x_refs, xp_ref, s, glu)
        dyp_ref[s:s + pad, :] = jnp.zeros((pad, cb), _F32)
        acc_ref[...] = jnp.zeros_like(acc_ref)
        dbacc_ref[...] = jnp.zeros_like(dbacc_ref)
        wv = w_ref[...]
        bias = jnp.broadcast_to(b_ref[...], (rows, cb))

        def through_act(r, carry):
            base = pl.multiple_of(r * rows, rows)
            d = dy_ref[pl.ds(base, rows), :].astype(_F32)
            if act:
                pre = _conv_window(xp_ref[pl.ds(base, win_rows), :], wv, groups, bias)
                sg = _sig(pre)
                d = d * (sg * (1.0 + pre * (1.0 - sg)))
            dyp_ref[pl.ds(base, rows), :] = d
            return carry

        lax.fori_loop(0, s // rows, through_act, 0)

        def chunk(r, carry):
            base = pl.multiple_of(r * rows, rows)
            xwin = xp_ref[pl.ds(base, win_rows), :]
            dwin = dyp_ref[pl.ds(base, win_rows), :]
            dyc = dwin[0:rows, :]
            dxacc = jnp.zeros((rows, cb), _F32)
            for rot, taps in groups.items():
                xr = xwin if rot == 0 else pltpu.roll(xwin, rot, 0)
                dr = dwin if rot == 0 else pltpu.roll(dwin, win_rows - rot, 0)
                for k, j in taps:
                    a8 = j - rot
                    dxacc = dxacc + dr[a8:a8 + rows, :] * wv[k:k + 1, :]
                    prod = dyc * xr[pad - a8:pad - a8 + rows, :]
                    acc_ref[k * SUBLANES:(k + 1) * SUBLANES, :] += fold(prod)
            dbacc_ref[...] += fold(dyc)
            if glu:
                av = x_refs[0][pl.ds(base, rows), :].astype(_F32)
                sg = _sig(x_refs[1][pl.ds(base, rows), :].astype(_F32))
                dx_refs[0][pl.ds(base, rows), :] = (dxacc * sg).astype(dx_refs[0].dtype)
                dx_refs[1][pl.ds(base, rows), :] = (dxacc * av * sg * (1.0 - sg)).astype(dx_refs[1].dtype)
            else:
                dx_refs[0][pl.ds(base, rows), :] = dxacc.astype(dx_refs[0].dtype)
            return carry

        lax.fori_loop(0, s // rows, chunk, 0)
        dw_ref[...] = jnp.zeros_like(dw_ref)
        for k in range(k_taps):
            dw_ref[k:k + 1, :] = jnp.sum(acc_ref[k * SUBLANES:(k + 1) * SUBLANES, :], axis=0, keepdims=True)
        db_ref[...] = jnp.sum(dbacc_ref[...], axis=0, keepdims=True)

    col = pl.BlockSpec((s, cb), lambda i: (0, i))
    wspec = pl.BlockSpec((kp, cb), lambda i: (0, i))
    bspec = pl.BlockSpec((1, cb), lambda i: (0, i))
    dx_dtype = xs[0].dtype
    res = pl.pallas_call(
        body, name=name, grid=(c // cb,),
        in_specs=[col] * n_in + [wspec, bspec, col], out_specs=[col] * n_in + [wspec, bspec],
        out_shape=[jax.ShapeDtypeStruct((s, c), dx_dtype)] * n_in
        + [jax.ShapeDtypeStruct((kp, c), _F32), jax.ShapeDtypeStruct((1, c), _F32)],
        scratch_shapes=[pltpu.VMEM((s + pad, cb), _F32), pltpu.VMEM((s + pad, cb), _F32),
                        pltpu.VMEM((kp * SUBLANES, cb), _F32), pltpu.VMEM((SUBLANES, cb), _F32)],
        compiler_params=_params(("parallel",)),
    )(*xs, w, b, dy)
    return list(res[:n_in]), res[n_in], res[n_in + 1]


def _tri_sum(v, lower):
    l = v.shape[0]
    r, c = lax.broadcasted_iota(jnp.int32, (l, l), 0), lax.broadcasted_iota(jnp.int32, (l, l), 1)
    tri = ((r >= c) if lower else (r <= c)).astype(jnp.bfloat16)
    hi = v.astype(jnp.bfloat16)
    r1 = v - hi.astype(_F32)
    mid = r1.astype(jnp.bfloat16)
    lo = (r1 - mid.astype(_F32)).astype(jnp.bfloat16)
    out = jnp.zeros_like(v)
    for part in (hi, mid, lo):
        out = out + lax.dot_general(tri, part, _DN["nn"], preferred_element_type=_F32)
    return out


@jax.custom_vjp
def _cumsum_rows(v):
    return _tri_sum(v, True)


_cumsum_rows.defvjp(lambda v: (_tri_sum(v, True), None), lambda _, ct: (_tri_sum(ct, False),))


def _ssd_chunk(xbc, dtraw, prev, bias, alog):
    l = xbc.shape[0]
    xs = xbc[:, :SSD_WIDTH]
    bm = xbc[:, SSD_WIDTH:SSD_WIDTH + 2 * SSD_STATE]
    cm = xbc[:, SSD_WIDTH + 2 * SSD_STATE:]
    v = dtraw + bias
    dt = jnp.maximum(v, 0.0) + jnp.log1p(jnp.exp(-jnp.abs(v)))
    a_neg = -jnp.exp(alog)
    acs = _cumsum_rows(dt * a_neg)
    acs_t = acs.T
    total = acs[l - 1:l, :]
    row = lax.broadcasted_iota(jnp.int32, (l, l), 0)
    colv = lax.broadcasted_iota(jnp.int32, (l, l), 1)
    causal = row >= colv
    lane_lo = lax.broadcasted_iota(jnp.int32, (l, LANES), 1) < HEAD_DIM
    row_lo = lax.broadcasted_iota(jnp.int32, (LANES, SSD_STATE), 0) < HEAD_DIM

    def pair_lanes(m, h0):
        return jnp.where(lane_lo, m[:, h0:h0 + 1], m[:, h0 + 1:h0 + 2])

    ys, news = [], []
    cb = {}
    for j in range(SSD_HEADS // 2):
        h0 = 2 * j
        grp = h0 // (SSD_HEADS // 2)
        bg = bm[:, grp * SSD_STATE:(grp + 1) * SSD_STATE]
        cg = cm[:, grp * SSD_STATE:(grp + 1) * SSD_STATE]
        if grp not in cb:
            cb[grp] = _dot1(cg, bg, "nt")
        xdt = xs[:, j * LANES:(j + 1) * LANES] * pair_lanes(dt, h0)
        y = jnp.zeros((l, LANES), _F32)
        for hh, mask in ((h0, lane_lo), (h0 + 1, jnp.logical_not(lane_lo))):
            seg = acs[:, hh:hh + 1] - acs_t[hh:hh + 1, :]
            dec = jnp.exp(jnp.where(causal, seg, -jnp.inf))
            y = y + _dot1(cb[grp] * dec, jnp.where(mask, xdt, 0.0), "nn")
        acs_p = pair_lanes(acs, h0)
        prev_p = prev[j * LANES:(j + 1) * LANES, :]
        y = y + _dot1(cg, prev_p, "nt") * jnp.exp(acs_p)
        wgt = jnp.exp(pair_lanes(jnp.broadcast_to(total, (l, LANES)), h0) - acs_p)
        st = _dot1(xdt * wgt, bg, "tn")
        cdec = jnp.exp(jnp.where(row_lo, total[:, h0:h0 + 1], total[:, h0 + 1:h0 + 2]))
        news.append(prev_p * cdec + st)
        ys.append(y)
    return jnp.concatenate(ys, axis=1), jnp.concatenate(news, axis=0)


def _ssd_gate_chunk(xbc, dtraw, prev, bias, alog, z, dexp, g):
    y, new = _ssd_chunk(xbc, dtraw, prev, bias, alog)
    return _f_gate(y, xbc[:, :SSD_WIDTH], z, dexp, g), new


def _ssd_fwd(xbc, dtraw, bias, alog, z, dexp, g, *, name):
    s = xbc.shape[0]
    nc = s // CHUNK
    nstate = SSD_HEADS * HEAD_DIM

    def body(x_ref, dt_ref, b_ref, a_ref, z_ref, d_ref, g_ref, y_ref, st_ref, state_ref):
        @pl.when(pl.program_id(0) == 0)
        def _():
            state_ref[...] = jnp.zeros_like(state_ref)

        prev = state_ref[...]
        st_ref[...] = prev
        y, new = _ssd_gate_chunk(x_ref[...].astype(_F32), dt_ref[...], prev, b_ref[...], a_ref[...], z_ref[...].astype(_F32),
                                 d_ref[...], g_ref[...])
        y_ref[...] = y.astype(y_ref.dtype)
        state_ref[...] = new

    small = pl.BlockSpec((1, LANES), lambda i: (0, 0))
    wide = pl.BlockSpec((1, SSD_WIDTH), lambda i: (0, 0))
    rows = pl.BlockSpec((CHUNK, SSD_WIDTH), lambda i: (i, 0))
    return pl.pallas_call(
        body, name=name, grid=(nc,),
        in_specs=[pl.BlockSpec((CHUNK, XBC_WIDTH), lambda i: (i, 0)), pl.BlockSpec((CHUNK, LANES), lambda i: (i, 0)),
                  small, small, rows, wide, wide],
        out_specs=[rows, pl.BlockSpec((None, nstate, SSD_STATE), lambda i: (i, 0, 0))],
        out_shape=[jax.ShapeDtypeStruct((s, SSD_WIDTH), _MXU), jax.ShapeDtypeStruct((nc, nstate, SSD_STATE), _F32)],
        scratch_shapes=[pltpu.VMEM((nstate, SSD_STATE), _F32)],
        compiler_params=_params(("arbitrary",)),
    )(xbc, dtraw, bias, alog, z, dexp, g)


def _ssd_bwd(xbc, dtraw, states, bias, alog, z, dexp, g, dy, *, name):
    s = xbc.shape[0]
    nc = s // CHUNK
    nstate = SSD_HEADS * HEAD_DIM

    def body(x_ref, dt_ref, st_ref, b_ref, a_ref, z_ref, d_ref, g_ref, dy_ref,
             dx_ref, ddt_ref, db_ref, da_ref, dz_ref, dd_ref, dg_ref, dstate_ref):
        @pl.when(pl.program_id(0) == 0)
        def _():
            dstate_ref[...] = jnp.zeros_like(dstate_ref)
            for acc in (db_ref, da_ref, dd_ref, dg_ref):
                acc[...] = jnp.zeros_like(acc)

        _, vjp = jax.vjp(_ssd_gate_chunk, x_ref[...].astype(_F32), dt_ref[...], st_ref[...], b_ref[...], a_ref[...],
                         z_ref[...].astype(_F32), d_ref[...], g_ref[...])
        dx, ddt, dprev, db, da, dz, dd, dg = vjp((dy_ref[...].astype(_F32), dstate_ref[...]))
        dx_ref[...] = dx.astype(dx_ref.dtype)
        ddt_ref[...] = ddt
        dz_ref[...] = dz.astype(dz_ref.dtype)
        db_ref[...] += db
        da_ref[...] += da
        dd_ref[...] += dd
        dg_ref[...] += dg
        dstate_ref[...] = dprev

    rev = lambda i: (nc - 1 - i, 0)
    small = pl.BlockSpec((1, LANES), lambda i: (0, 0))
    wide = pl.BlockSpec((1, SSD_WIDTH), lambda i: (0, 0))
    rows = pl.BlockSpec((CHUNK, SSD_WIDTH), rev)
    return pl.pallas_call(
        body, name=name, grid=(nc,),
        in_specs=[pl.BlockSpec((CHUNK, XBC_WIDTH), rev), pl.BlockSpec((CHUNK, LANES), rev),
                  pl.BlockSpec((None, nstate, SSD_STATE), lambda i: (nc - 1 - i, 0, 0)), small, small, rows, wide, wide,
                  rows],
        out_specs=[pl.BlockSpec((CHUNK, XBC_WIDTH), rev), pl.BlockSpec((CHUNK, LANES), rev), small, small, rows, wide, wide],
        out_shape=[jax.ShapeDtypeStruct((s, XBC_WIDTH), xbc.dtype), jax.ShapeDtypeStruct((s, LANES), _F32),
                   jax.ShapeDtypeStruct((1, LANES), _F32), jax.ShapeDtypeStruct((1, LANES), _F32),
                   jax.ShapeDtypeStruct((s, SSD_WIDTH), z.dtype), jax.ShapeDtypeStruct((1, SSD_WIDTH), _F32),
                   jax.ShapeDtypeStruct((1, SSD_WIDTH), _F32)],
        scratch_shapes=[pltpu.VMEM((nstate, SSD_STATE), _F32)],
        compiler_params=_params(("arbitrary",)),
    )(xbc, dtraw, states, bias, alog, z, dexp, g, dy)


def _pad_cols(a, width):
    return jnp.pad(a, ((0, 0), (0, width - a.shape[1])))


def _pad_rows(a, rows):
    return jnp.pad(a, ((0, rows - a.shape[0]), (0, 0)))


def _tie(a, token):
    return a + token[0:1, 0:1].astype(a.dtype)


def _local_step(x, mem, target, w, fetch, emit):
    bf = _MXU
    d = D_MODEL
    h = _row_fwd(_f_rms, [x], [w['norm_mix_g']], [(d, bf)], name="f_norm_mix")
    w_in = fetch('in', h)['w_in']
    z_end, xbc_end, dt_end = SSD_WIDTH, SSD_WIDTH + XBC_WIDTH, SSD_WIDTH + XBC_WIDTH + SSD_HEADS
    w_z, w_xbc = w_in[:z_end], w_in[z_end:xbc_end]
    w_dt = _pad_rows(w_in[xbc_end:dt_end], LANES)
    w_a, w_g = w_in[dt_end:dt_end + CF_WIDTH], w_in[dt_end + CF_WIDTH:]
    dt_bias = _pad_cols(w['ssd_dt_bias'], LANES)
    a_log = _pad_cols(w['ssd_A_log'], LANES)
    d_exp = jnp.repeat(w['ssd_D'], HEAD_DIM, axis=1)
    g_final = w['norm_final_g'].reshape(1, D_MODEL)

    z, xbc, dtr, ga, gg = _mm_fan_out(h, [w_z, w_xbc, w_dt, w_a, w_g], tb=True, out_dtypes=[bf, bf, _F32, bf, bf],
                                      name="f_in")
    wc = fetch('conv', xbc)
    ssd_w = _pad_rows(wc['ssd_conv_w'], SUBLANES)
    cf_w = _pad_rows(wc['cf_conv_w'], 32)
    xbc_a = _conv_fwd([xbc], ssd_w, w['ssd_conv_b'], SSD_CONV, act=True, out_dtype=bf, name="f_ssd_conv")
    y_n, states = _ssd_fwd(xbc_a, dtr, dt_bias, a_log, z, d_exp, w['ssd_norm_g'], name="f_ssd")
    u_c = _conv_fwd([ga, gg], cf_w, w['cf_conv_b'], CF_CONV, glu=True, out_dtype=bf, name="f_cf_conv")
    u = _row_fwd(_f_ln, [u_c], [w['cf_ln_g'], w['cf_ln_b']], [(d, bf)], name="f_cf_ln")
    wm = fetch('mid', y_n)
    w_out_y, w_out_u = wm['w_out'][:SSD_WIDTH], wm['w_out'][SSD_WIDTH:]
    x1, hq = _mm_fan_in([(y_n, w_out_y), (u, w_out_u)], add=x, out_dtype=bf, epilogue=(_f_rms, w['norm_xattn_g'], bf),
                        name="f_out")
    q = _mm(hq, wm['w_q'], out_dtype=bf, name="f_q")
    memn = _row_fwd(_f_rms, [mem], [w['norm_mem_g']], [(d, bf)], name="f_norm_mem")
    kv = _mm(memn, wm['w_kv'], name="f_kv")
    k_mat, v_mat = kv[:, :d], kv[:, d:]
    o = _row_fwd(_f_att, [q], [k_mat, v_mat], [(d, bf)], ts=512, name="f_att")
    x2, hf = _mm_fan_in([(o, wm['w_o'])], add=x1, out_dtype=bf, epilogue=(_f_rms, w['norm_ffn_g'], bf), name="f_o")
    wf = fetch('ffn', hf)
    gate, up, act = _mm_fan_out(hf, [wf['w_gate'], wf['w_up']], tb=True, out_dtypes=[bf, bf], epilogue=_f_swiglu,
                                extra_outs=[(D_FF, bf)], tm=256, name="f_ffn_in")
    x3 = _mm(act, wf['w_down'], add=x2, out_dtype=bf, name="f_down")

    dx3, dg_final, loss = _loss_bwd(x3, target, g_final, dx_dtype=bf, name="b_loss")
    g = {'norm_final_g': dg_final.reshape(d)}

    dact = _mm(dx3, wf['w_down'], tb=True, out_dtype=bf, name="b_down_x")
    dw_down = _mm(act, dx3, ta=True, out_dtype=bf, name="b_down_w")
    def swiglu_bwd(gate_t, up_t, dact_t):
        return jax.vjp(_f_swiglu, gate_t, up_t)[1](dact_t)

    dhf, dgate, dup = _mm_fan_in([(None, wf['w_gate']), (None, wf['w_up'])], prologue=swiglu_bwd, pro_ins=[gate, up, dact],
                                 pro_out_dtypes=[bf, bf], out_dtype=bf, tm=256, name="b_ffn_in_x")
    sent = emit({'w_down': dw_down, 'w_gate': _mm(dgate, hf, ta=True, out_dtype=bf, name="b_gate_w"),
                 'w_up': _mm(dup, hf, ta=True, out_dtype=bf, name="b_up_w")})
    (dx2,), (g['norm_ffn_g'],) = _row_bwd(_f_rms, [x2], [_tie(w['norm_ffn_g'], sent)], [dhf], adds={0: dx3}, row_dtypes=[bf], name="b_norm_ffn")

    do = _mm(dx2, wm['w_o'], tb=True, out_dtype=bf, name="b_o_x")
    dw_o = _mm(o, dx2, ta=True, out_dtype=bf, name="b_o_w")
    (dq,), (dk, dv) = _row_bwd(_f_att, [q], [k_mat, v_mat], [do], row_dtypes=[bf], ts=512, name="b_att")
    dw_q = _mm(hq, dq, ta=True, out_dtype=bf, name="b_q_w")
    dhq = _mm(dq, wm['w_q'], tb=True, out_dtype=bf, name="b_q_x")
    (dx1,), (g['norm_xattn_g'],) = _row_bwd(_f_rms, [x1], [w['norm_xattn_g']], [dhq], adds={0: dx2}, row_dtypes=[bf], name="b_norm_xattn")
    dkv = jnp.concatenate([dk, dv], axis=1)
    dmemn = _mm(dkv, wm['w_kv'], tb=True, name="b_kv_x")
    _, (g['norm_mem_g'],) = _row_bwd(_f_rms, [mem], [w['norm_mem_g']], [dmemn], need=[False], name="b_norm_mem")
    sent = emit({'w_o': dw_o, 'w_q': dw_q, 'w_kv': _mm(memn, dkv, ta=True, out_dtype=bf, name="b_kv_w")},
                after=g['norm_mem_g'])

    dyn, du = _mm_fan_out(dx1, [w_out_y, w_out_u], tb=True, out_dtypes=[bf, bf], name="b_out_x")
    (du_c,), (g['cf_ln_g'], g['cf_ln_b']) = _row_bwd(_f_ln, [u_c], [_tie(w['cf_ln_g'], sent), w['cf_ln_b']], [du], row_dtypes=[bf], name="b_cf_ln")
    sent = emit({'w_out': jnp.concatenate([_mm(y_n, dx1, ta=True, out_dtype=bf, name="b_out_y_w"), _mm(u, dx1, ta=True, out_dtype=bf, name="b_out_u_w")], axis=0)})
    (dga, dgg), dcf_w, g['cf_conv_b'] = _conv_bwd([ga, gg], cf_w, w['cf_conv_b'], du_c, CF_CONV, glu=True, name="b_cf_conv")
    g['cf_conv_w'] = dcf_w[:CF_CONV]
    dxbc_a, ddtr, ddt_bias, da_log, dz, dd_exp, g['ssd_norm_g'] = _ssd_bwd(
        xbc_a, dtr, states, dt_bias, a_log, z, d_exp, _tie(w['ssd_norm_g'], sent), dyn, name="b_ssd")
    g['ssd_D'] = jnp.sum(dd_exp.reshape(SSD_HEADS, HEAD_DIM), axis=1).reshape(1, SSD_HEADS)
    g['ssd_dt_bias'] = ddt_bias[:, :SSD_HEADS]
    g['ssd_A_log'] = da_log[:, :SSD_HEADS]
    (dxbc,), dssd_w, g['ssd_conv_b'] = _conv_bwd([xbc], ssd_w, w['ssd_conv_b'], dxbc_a, SSD_CONV, act=True, name="b_ssd_conv")
    g['ssd_conv_w'] = dssd_w[:SSD_CONV]

    sent = emit({'w_in': jnp.concatenate([
        _mm(dz, h, ta=True, out_dtype=bf, name="b_in_z_w"), _mm(dxbc, h, ta=True, out_dtype=bf, name="b_in_xbc_w"),
        _mm(ddtr, h, ta=True, out_dtype=bf, name="b_in_dt_w")[:SSD_HEADS],
        _mm(dga, h, ta=True, out_dtype=bf, name="b_in_a_w"), _mm(dgg, h, ta=True, out_dtype=bf, name="b_in_g_w")], axis=0)})
    dh = _mm_fan_in([(dz, w_z), (dxbc, w_xbc), (ddtr, _tie(w_dt, sent)), (dga, w_a), (dgg, w_g)], out_dtype=bf,
                    name="b_in_x")
    (dx,), (g['norm_mix_g'],) = _row_bwd(_f_rms, [x], [w['norm_mix_g']], [dh], adds={0: dx1}, name="b_norm_mix")
    return loss, dx, g


_ANY = pl.BlockSpec(memory_space=pl.ANY)


def _place():
    x, y, c = lax.axis_index("x"), lax.axis_index("y"), lax.axis_index("c")
    return x, y, c


def _all_gather(arrs, *, name):
    n = len(arrs)

    def body(*refs):
        ins, outs = refs[:n], refs[n:2 * n]
        send_sems, recv_sems, local_sems = refs[2 * n:]
        x, y, c = _place()
        me, sibling = (x, y, c), (x, y, 1 - c)
        chips = [(1 - x, y), (x, 1 - y), (1 - x, 1 - y)]

        def slot(a, dev):
            return outs[a].at[4 * dev[0] + 2 * dev[1] + dev[2]]

        def copy(a, k, block, to, src=None):
            return pltpu.make_async_remote_copy(
                src_ref=slot(a, block) if src is None else src, dst_ref=slot(a, block),
                send_sem=send_sems.at[a, k], recv_sem=recv_sems.at[a, k], device_id=to, device_id_type=MESH)

        mine = [pltpu.make_async_copy(ins[a], slot(a, me), local_sems.at[a]) for a in range(n)]
        for cp in mine:
            cp.start()
        first = []
        for a in range(n):
            first.append(copy(a, 0, me, sibling, src=ins[a]))
            first += [copy(a, 1 + j, me, (*chip, c), src=ins[a]) for j, chip in enumerate(chips)]
        for cp in first:
            cp.start()
        passed = []
        for a in range(n):
            for j, chip in enumerate(chips):
                copy(a, 1 + j, (*chip, c), me).wait_recv()
                fwd = copy(a, 4 + j, (*chip, c), sibling)
                fwd.start()
                passed.append(fwd)
        for a in range(n):
            copy(a, 0, sibling, me).wait_recv()
            for j, chip in enumerate(chips):
                copy(a, 4 + j, (*chip, 1 - c), me).wait_recv()
        for cp in first + passed:
            cp.wait_send()
        for cp in mine:
            cp.wait()

    return pl.pallas_call(
        body, name=name, in_specs=[_ANY] * n, out_specs=[_ANY] * n,
        out_shape=[jax.ShapeDtypeStruct((N_DEV,) + a.shape, a.dtype) for a in arrs],
        scratch_shapes=[pltpu.SemaphoreType.DMA((n, 7)), pltpu.SemaphoreType.DMA((n, 7)), pltpu.SemaphoreType.DMA((n,))],
    )(*arrs)


_HBM = pl.BlockSpec(memory_space=pltpu.HBM)
_SEM = pl.BlockSpec(memory_space=pltpu.SEMAPHORE)
_EFFECT = pltpu.SideEffectType.DATAFLOW_SIDE_EFFECTING
_FLIPS = [(dx, dy, dc) for dx in (0, 1) for dy in (0, 1) for dc in (0, 1)][1:]


def _peer(flip, x, y, c):
    return (1 - x if flip[0] else x, 1 - y if flip[1] else y, 1 - c if flip[2] else c)


_CHIP_FLIPS = [f for f in _FLIPS if f[2] == 0]


def _send_start(srcs, blocked, *, flips=None, after=None, name):
    n = len(srcs)
    flips = _FLIPS if flips is None else flips
    nf = len(flips)
    lands = [jax.ShapeDtypeStruct(s.shape if blocked else (N_DEV,) + s.shape, s.dtype) for s in srcs]
    n_in = 2 * n + (after is not None)

    def body(*refs):
        src_refs, land_refs = refs[:n], refs[n:2 * n]
        send_sems, recv_sems = refs[n_in], refs[n_in + 1]
        token = refs[-1]
        x, y, c = _place()
        me = 4 * x + 2 * y + c
        for a in range(n):
            for k, flip in enumerate(flips):
                p = _peer(flip, x, y, c)
                src = src_refs[a].at[4 * p[0] + 2 * p[1] + p[2]] if blocked else src_refs[a]
                pltpu.make_async_remote_copy(
                    src_ref=src, dst_ref=land_refs[a].at[me], send_sem=send_sems.at[nf * a + k],
                    recv_sem=recv_sems.at[nf * a + k], device_id=p, device_id_type=MESH).start()
        token[...] = jnp.zeros_like(token)

    res = pl.pallas_call(
        body, name=name,
        out_shape=(pltpu.SemaphoreType.DMA((nf * n,)), pltpu.SemaphoreType.DMA((nf * n,)),
                   *[pltpu.HBM(s.shape, s.dtype) for s in srcs], *[pltpu.HBM(l.shape, l.dtype) for l in lands],
                   jax.ShapeDtypeStruct((SUBLANES, LANES), _F32)),
        in_specs=[_HBM] * (2 * n) + [_ANY] * (after is not None),
        out_specs=(_SEM, _SEM, *[_HBM] * (2 * n), pl.BlockSpec(memory_space=pltpu.VMEM)),
        input_output_aliases={i: 2 + i for i in range(2 * n)},
        compiler_params=pltpu.CompilerParams(has_side_effects=_EFFECT),
    )(*[pltpu.with_memory_space_constraint(s, pltpu.HBM) for s in srcs],
      *[pltpu.with_memory_space_constraint(lax.empty(l.shape, l.dtype), pltpu.HBM) for l in lands],
      *([after] if after is not None else []))
    return res[0], res[1], list(res[2:2 + n]), list(res[2 + n:2 + 2 * n]), res[-1]


def _send_wait(handles, after, blocked, *, flips=None, name):
    send_sems, recv_sems, srcs, lands, _ = handles
    n = len(srcs)
    flips = _FLIPS if flips is None else flips
    nf = len(flips)

    def body(*refs):
        src_refs, land_refs = refs[:n], refs[n:2 * n]
        send_sems, recv_sems = refs[2 * n], refs[2 * n + 1]
        x, y, c = _place()
        for a in range(n):
            for k, flip in enumerate(flips):
                p = _peer(flip, x, y, c)
                pid = 4 * p[0] + 2 * p[1] + p[2]
                cp = pltpu.make_async_remote_copy(
                    src_ref=src_refs[a].at[pid] if blocked else src_refs[a], dst_ref=land_refs[a].at[pid],
                    send_sem=send_sems.at[nf * a + k], recv_sem=recv_sems.at[nf * a + k], device_id=p, device_id_type=MESH)
                cp.wait_send()
                cp.wait_recv()

    res = pl.pallas_call(
        body, name=name,
        out_shape=tuple(pltpu.HBM(s.shape, s.dtype) for s in srcs + lands),
        in_specs=[_HBM] * (2 * n) + [_SEM, _SEM, _ANY], out_specs=tuple([_HBM] * (2 * n)),
        input_output_aliases={i: i for i in range(2 * n)},
        compiler_params=pltpu.CompilerParams(has_side_effects=_EFFECT),
    )(*srcs, *lands, send_sems, recv_sems, after)
    return list(res[:n]), list(res[n:])


def _sibling_share(lands, owns, *, name):
    n = len(lands)

    def body(*refs):
        own_refs, land_refs = refs[n:2 * n], refs[2 * n:3 * n]
        send_sems, recv_sems = refs[3 * n], refs[3 * n + 1]
        x, y, c = _place()
        sibling = (x, y, 1 - c)
        chips = [(1 - x, y), (x, 1 - y), (1 - x, 1 - y)]

        def copy(a, k, block, src=None):
            slot = land_refs[a].at[block]
            return pltpu.make_async_remote_copy(
                src_ref=slot if src is None else src, dst_ref=slot, send_sem=send_sems.at[4 * a + k],
                recv_sem=recv_sems.at[4 * a + k], device_id=sibling, device_id_type=MESH)

        sends = []
        for a in range(n):
            sends.append(copy(a, 0, 4 * x + 2 * y + c, src=own_refs[a]))
            sends += [copy(a, 1 + j, 4 * cx + 2 * cy + c) for j, (cx, cy) in enumerate(chips)]
        for cp in sends:
            cp.start()
        for a in range(n):
            copy(a, 0, 4 * x + 2 * y + (1 - c)).wait_recv()
            for j, (cx, cy) in enumerate(chips):
                copy(a, 1 + j, 4 * cx + 2 * cy + (1 - c)).wait_recv()
        for cp in sends:
            cp.wait_send()

    return pl.pallas_call(
        body, name=name, in_specs=[_ANY] * (2 * n), out_specs=[_ANY] * n,
        out_shape=[jax.ShapeDtypeStruct(l.shape, l.dtype) for l in lands],
        input_output_aliases={i: i for i in range(n)},
        scratch_shapes=[pltpu.SemaphoreType.DMA((4 * n,)), pltpu.SemaphoreType.DMA((4 * n,))],
    )(*lands, *owns)


def _adamw(parts, w, m, v, *, own=None, me=None, name):
    p, r, c = parts.shape
    tr = _pick(r, (256, 176, 128, 64, 32, 16, 8))
    if own is not None:
        tc = c if tr < r else _pick(c, (256, 128))
        return _adamw_own(parts, own, me, w, m, v, tr, tc, name=name)

    def body(p_ref, w_ref, m_ref, v_ref, g_ref, d_ref, nm_ref, nv_ref):
        g = p_ref[0].astype(_F32)
        for i in range(1, p):
            g = g + p_ref[i].astype(_F32)
        _adamw_math(g, w_ref, m_ref, v_ref, g_ref, d_ref, nm_ref, nv_ref)

    blk = pl.BlockSpec((tr, c), lambda i: (i, 0))
    return pl.pallas_call(
        body, name=name, grid=(r // tr,),
        in_specs=[pl.BlockSpec((p, tr, c), lambda i: (0, i, 0)), blk, blk, blk], out_specs=[blk] * 4,
        out_shape=[jax.ShapeDtypeStruct((r, c), _F32)] * 4,
        compiler_params=_params(("parallel",)),
    )(parts, w, m, v)


def _adamw_math(g, w_ref, m_ref, v_ref, g_ref, d_ref, nm_ref, nv_ref):
    wv = w_ref[...]
    mn = ADAM_B1 * m_ref[...] + (1.0 - ADAM_B1) * g
    vn = ADAM_B2 * v_ref[...] + (1.0 - ADAM_B2) * jnp.square(g)
    m_hat = mn / (1.0 - ADAM_B1 ** ADAM_STEP)
    v_hat = vn / (1.0 - ADAM_B2 ** ADAM_STEP)
    g_ref[...] = g
    d_ref[...] = -ADAM_LR * (m_hat / (jnp.sqrt(v_hat) + ADAM_EPS) + ADAM_WD * wv)
    nm_ref[...] = mn
    nv_ref[...] = vn


def _adamw_own(parts, own, me, w, m, v, tr, tc, *, name):
    p, r, c = parts.shape

    def body(me_ref, p_ref, own_ref, w_ref, m_ref, v_ref, g_ref, d_ref, nm_ref, nv_ref):
        mine = own_ref[...].astype(_F32)
        g = jnp.where(me_ref[0] == 0, mine, p_ref[0].astype(_F32))
        for i in range(1, p):
            g = g + jnp.where(me_ref[0] == i, mine, p_ref[i].astype(_F32))
        _adamw_math(g, w_ref, m_ref, v_ref, g_ref, d_ref, nm_ref, nv_ref)

    blk = pl.BlockSpec((tr, tc), lambda i, j, me_ref: (i, j))
    grid_spec = pltpu.PrefetchScalarGridSpec(
        num_scalar_prefetch=1, grid=(r // tr, c // tc),
        in_specs=[pl.BlockSpec((p, tr, tc), lambda i, j, me_ref: (0, i, j)),
                  pl.BlockSpec((None, tr, tc), lambda i, j, me_ref: (me_ref[0], i, j)), blk, blk, blk],
        out_specs=[blk] * 4)
    return pl.pallas_call(
        body, name=name, grid_spec=grid_spec, out_shape=[jax.ShapeDtypeStruct((r, c), _F32)] * 4,
        compiler_params=_params(("parallel", "parallel")),
    )(me.reshape(1).astype(jnp.int32), parts, own, w, m, v)


def _adamw_rows(g_row, offsets, ws, ms, vs, *, name):
    k = len(ws)

    def body(*refs):
        g_ref, w_refs, m_refs, v_refs = refs[0], refs[1:1 + k], refs[1 + k:1 + 2 * k], refs[1 + 2 * k:1 + 3 * k]
        outs = refs[1 + 3 * k:]
        for i in range(k):
            gi = g_ref[:, offsets[i]:offsets[i] + ws[i].shape[1]]
            _adamw_math(gi, w_refs[i], m_refs[i], v_refs[i], *outs[4 * i:4 * i + 4])

    return pl.pallas_call(
        body, name=name, out_shape=[jax.ShapeDtypeStruct(w.shape, _F32) for w in ws for _ in range(4)],
    )(g_row, *ws, *ms, *vs)


def _sum_parts(parts, *, name):
    p, r, c = parts.shape

    def body(p_ref, o_ref):
        g = p_ref[0].astype(_F32)
        for i in range(1, p):
            g = g + p_ref[i].astype(_F32)
        o_ref[...] = g

    return pl.pallas_call(body, name=name, out_shape=jax.ShapeDtypeStruct((r, c), _F32))(parts)


def _pack(vals, rows):
    flat = jnp.concatenate([v.reshape(-1) for v in vals])
    return jnp.pad(flat, (0, rows * LANES - flat.shape[0])).reshape(rows, LANES)


def _unpack(packed, shapes):
    flat = packed.reshape(-1)
    out, pos = [], 0
    for shp in shapes:
        size = math.prod(shp)
        out.append(flat[pos:pos + size].reshape(shp))
        pos += size
    return out


def _pack_rows(shapes):
    total = sum(math.prod(s) for s in shapes)
    return -(-total // (LANES * SUBLANES)) * SUBLANES


def kernel(x, mem, norm_mix_g, w_in, ssd_conv_w, ssd_conv_b, ssd_dt_bias, ssd_A_log, ssd_D, ssd_norm_g, cf_conv_w, cf_conv_b, cf_ln_g, cf_ln_b, w_out, norm_xattn_g, norm_mem_g, w_q, w_kv, w_o, norm_ffn_g, w_gate, w_up, w_down, norm_final_g, loss_target, m_norm_mix_g, m_w_in, m_ssd_conv_w, m_ssd_conv_b, m_ssd_dt_bias, m_ssd_A_log, m_ssd_D, m_ssd_norm_g, m_cf_conv_w, m_cf_conv_b, m_cf_ln_g, m_cf_ln_b, m_w_out, m_norm_xattn_g, m_norm_mem_g, m_w_q, m_w_kv, m_w_o, m_norm_ffn_g, m_w_gate, m_w_up, m_w_down, m_norm_final_g, v_norm_mix_g, v_w_in, v_ssd_conv_w, v_ssd_conv_b, v_ssd_dt_bias, v_ssd_A_log, v_ssd_D, v_ssd_norm_g, v_cf_conv_w, v_cf_conv_b, v_cf_ln_g, v_cf_ln_b, v_w_out, v_norm_xattn_g, v_norm_mem_g, v_w_q, v_w_kv, v_w_o, v_norm_ffn_g, v_w_gate, v_w_up, v_w_down, v_norm_final_g):
    args = dict(locals())
    wts = {n: args[n] for n in WEIGHT_NAMES}
    mom = {n: args["m_" + n] for n in WEIGHT_NAMES}
    var = {n: args["v_" + n] for n in WEIGHT_NAMES}
    me = 4 * lax.axis_index("x") + 2 * lax.axis_index("y") + lax.axis_index("c")

    groups = {'in': ['w_in'], 'conv': ['ssd_conv_w', 'cf_conv_w'], 'mid': ['w_out', 'w_q', 'w_kv', 'w_o'],
              'ffn': ['w_gate', 'w_up', 'w_down']}
    def shard(n, a):
        return jnp.transpose(a[0], (1, 0)) if n in TRANSPOSED else a[0]

    two_level = {'in': _CHIP_FLIPS}
    gathers, started = {}, None
    for grp, names in groups.items():
        shards = [wts[n][0] if grp == 'conv' else shard(n, wts[n]).astype(_MXU) for n in names]
        gathers[grp] = _send_start(shards, False, flips=two_level.get(grp), after=started, name="gather_%s_start" % grp)
        started = gathers[grp][4]

    def fetch(grp, after):
        srcs, lands = _send_wait(gathers[grp], started if after is None else after, False, flips=two_level.get(grp),
                                 name="gather_%s_wait" % grp)
        if grp in two_level:
            lands = _sibling_share(lands, srcs, name="gather_%s_share" % grp)
        out = {}
        for n, own, gth in zip(groups[grp], srcs, lands):
            gth = lax.dynamic_update_slice_in_dim(gth, own[None], me, axis=0)
            if n == 'w_kv' or grp == 'conv':
                out[n] = jnp.transpose(gth, (1, 0, 2)).reshape(gth.shape[1], N_DEV * gth.shape[2])
            else:
                out[n] = gth.reshape(N_DEV * gth.shape[1], gth.shape[2])
        return out

    exchanges = []

    def emit(grads, after=None):
        blocks = []
        for n, gw in grads.items():
            if n == 'w_kv':
                gw = jnp.transpose(gw.reshape(gw.shape[0], N_DEV, gw.shape[1] // N_DEV), (1, 0, 2))
            else:
                gw = gw.reshape(N_DEV, gw.shape[0] // N_DEV, gw.shape[1])
            blocks.append(gw.astype(jnp.bfloat16))
        first = next(iter(grads))
        exchanges.append((list(grads), _send_start(blocks, True, after=after, name="exchange_%s_start" % first), first))
        return exchanges[-1][1][4]

    full = {n: wts[n] for n in WEIGHT_NAMES if n not in BIG and n not in groups['conv']}
    full['norm_mix_g'] = _tie(norm_mix_g, started)

    loss_blk, grad_x, g = _local_step(x[0], mem[0], loss_target[0], full, fetch, emit)

    small = [n for n in WEIGHT_NAMES if n not in BIG]
    g['loss'] = loss_blk[0:1, 0:1]
    items = small + ['loss']
    size = {n: math.prod(g[n].shape) for n in items}
    seg = {n: -(-size[n] // LANES) * LANES for n in items}
    off, pos = {}, 0
    for n in items:
        off[n], pos = pos, pos + seg[n]
    rows = -(-pos // (LANES * SUBLANES)) * SUBLANES
    flat = jnp.concatenate([jnp.pad(g[n].reshape(-1), (0, seg[n] - size[n])) for n in items]
                           + [jnp.zeros((rows * LANES - pos,), _F32)])
    small_sent = _send_start([flat.reshape(rows, LANES)], False, name="gather_small_start")

    out_g, out_d, out_m, out_v = {}, {}, {}, {}
    done = small_sent[4]
    for names, handles, first in exchanges:
        srcs, lands = _send_wait(handles, done, True, name="exchange_%s_wait" % first)
        for n, own, parts in zip(names, srcs, lands):
            res = _adamw(parts, shard(n, wts[n]), shard(n, mom[n]), shard(n, var[n]), own=own, me=me, name="adamw_" + n)
            out_g[n], out_d[n], out_m[n], out_v[n] = [(jnp.transpose(r, (1, 0)) if n in TRANSPOSED else r)[None] for r in res]
            done = res[0]

    srcs, lands = _send_wait(small_sent, out_g[exchanges[-1][0][-1]], False, name="gather_small_wait")
    small_parts = lax.dynamic_update_slice_in_dim(lands[0], srcs[0][None], me, axis=0)
    g_row = _sum_parts(small_parts, name="sum_small_grads").reshape(1, rows * LANES)
    loss = g_row[0, off['loss']]
    rep = [n for n in small if n not in groups['conv']]
    as_row = lambda a: a.reshape(1, -1)
    res = _adamw_rows(g_row, [off[n] for n in rep], [as_row(wts[n]) for n in rep], [as_row(mom[n]) for n in rep],
                      [as_row(var[n]) for n in rep], name="adamw_small")
    for i, n in enumerate(rep):
        out_g[n], out_d[n], out_m[n], out_v[n] = [r.reshape(wts[n].shape) for r in res[4 * i:4 * i + 4]]
    for n in groups['conv']:
        k_taps, width = g[n].shape
        g_full = g_row[0, off[n]:off[n] + size[n]].reshape(k_taps, width)
        g_mine = lax.dynamic_slice_in_dim(g_full, me * (width // N_DEV), width // N_DEV, axis=1)
        res = _adamw(g_mine[None], wts[n][0], mom[n][0], var[n][0], name="adamw_" + n)
        out_g[n], out_d[n], out_m[n], out_v[n] = [r[None] for r in res]

    return (loss, grad_x[None], *[out_g[n] for n in WEIGHT_NAMES], *[out_d[n] for n in WEIGHT_NAMES],
            *[out_m[n] for n in WEIGHT_NAMES], *[out_v[n] for n in WEIGHT_NAMES])
```

```python
import functools
import math

import jax
import jax.numpy as jnp
from jax import lax
from jax.experimental import pallas as pl
from jax.experimental.pallas import tpu as pltpu

_F32 = jnp.float32
_MXU = jnp.bfloat16
_PREC = None
_VMEM_LIMIT = 56 * 1024 * 1024

D_MODEL = 1024
HEAD_DIM = 64
SSD_HEADS = 16
SSD_WIDTH = 1024
SSD_STATE = 128
SSD_CONV = 4
CHUNK = 128
XBC_WIDTH = 1536
CF_WIDTH = 1024
CF_CONV = 31
X_HEADS = 4
X_HEAD_DIM = 256
D_FF = 2816
EPS = 1e-6
N_DEV = 8
LANES = 128
SUBLANES = 8

ADAM_LR = 0.001
ADAM_B1 = 0.9
ADAM_B2 = 0.999
ADAM_EPS = 1e-08
ADAM_WD = 0.01
ADAM_STEP = 10

MESH = pl.DeviceIdType.MESH
WEIGHT_NAMES = ['norm_mix_g', 'w_in', 'ssd_conv_w', 'ssd_conv_b', 'ssd_dt_bias', 'ssd_A_log', 'ssd_D', 'ssd_norm_g',
                'cf_conv_w', 'cf_conv_b', 'cf_ln_g', 'cf_ln_b', 'w_out', 'norm_xattn_g', 'norm_mem_g', 'w_q', 'w_kv',
                'w_o', 'norm_ffn_g', 'w_gate', 'w_up', 'w_down', 'norm_final_g']
BIG = ['w_in', 'w_out', 'w_q', 'w_kv', 'w_o', 'w_gate', 'w_up', 'w_down']
TRANSPOSED = ('w_in', 'w_gate', 'w_up')


def _params(sem=None):
    return pltpu.CompilerParams(dimension_semantics=sem, vmem_limit_bytes=_VMEM_LIMIT)


def _pick(n, cands):
    for c in cands:
        if n % c == 0:
            return c
    return n


def _mm(a, b, *, ta=False, tb=False, add=None, out_dtype=_F32, name):
    (kdim, m) = a.shape if ta else a.shape[::-1]
    (n, k2) = b.shape if tb else b.shape[::-1]
    assert kdim == k2, (a.shape, b.shape, ta, tb)
    if ta:
        tm = m if m <= 1024 else _pick(m, (1408, 1024, 512, 256, 128))
        tn = n if n <= 1536 else _pick(n, (1408, 1024, 512, 256, 128))
        size = lambda arr: jnp.dtype(arr.dtype).itemsize
        fits = lambda t: (2 * t * (tm * size(a) + tn * size(b)) + tm * tn * (4 + 2 * jnp.dtype(out_dtype).itemsize)
                          <= _VMEM_LIMIT * 3 // 4)
        tk = next((t for t in (2048, 1024, 512, 256, 128) if kdim % t == 0 and fits(t)), _pick(kdim, (128,)))
    else:
        tm = _pick(m, (512, 256, 128))
        tn = n if n <= 2816 else _pick(n, (1408, 1024, 512, 256, 128))
        tk = kdim if kdim <= 2816 else _pick(kdim, (1408, 1024, 512, 256, 128))
    nk = kdim // tk
    dn = (((0 if ta else 1,), (1 if tb else 0,)), ((), ()))

    def body(*refs):
        a_ref, b_ref = refs[0], refs[1]
        add_ref = refs[2] if add is not None else None
        o_ref = refs[3 if add is not None else 2]
        acc_ref = refs[-1]
        k = pl.program_id(2)
        prod = lax.dot_general(a_ref[...].astype(_MXU), b_ref[...].astype(_MXU), dn,
                               preferred_element_type=_F32, precision=_PREC)

        def finish(r):
            if add_ref is not None:
                r = r + add_ref[...].astype(_F32)
            o_ref[...] = r.astype(o_ref.dtype)

        if nk == 1:
            finish(prod)
            return

        @pl.when(k == 0)
        def _():
            acc_ref[...] = prod

        @pl.when(jnp.logical_and(k > 0, k < nk - 1))
        def _():
            acc_ref[...] += prod

        @pl.when(k == nk - 1)
        def _():
            finish(acc_ref[...] + prod)

    a_spec = pl.BlockSpec((tk, tm), lambda i, j, k: (k, i)) if ta else pl.BlockSpec((tm, tk), lambda i, j, k: (i, k))
    b_spec = pl.BlockSpec((tn, tk), lambda i, j, k: (j, k)) if tb else pl.BlockSpec((tk, tn), lambda i, j, k: (k, j))
    o_spec = pl.BlockSpec((tm, tn), lambda i, j, k: (i, j))
    ins, specs = [a, b], [a_spec, b_spec]
    if add is not None:
        ins.append(add)
        specs.append(o_spec)
    return pl.pallas_call(
        body, name=name, grid=(m // tm, n // tn, nk), in_specs=specs, out_specs=o_spec,
        out_shape=jax.ShapeDtypeStruct((m, n), out_dtype),
        scratch_shapes=[pltpu.VMEM((tm, tn), _F32)] if nk > 1 else [],
        compiler_params=_params(("parallel", "parallel", "arbitrary")),
    )(*ins)


def _resident(shape):
    return pl.BlockSpec(shape, lambda i: (0,) * len(shape), pipeline_mode=pl.Buffered(1))


def _mm_fan_out(a, bs, *, tb, out_dtypes, epilogue=None, extra_outs=(), tm=512, name):
    m, kdim = a.shape
    tm = min(tm, m)
    ns = [b.shape[0] if tb else b.shape[1] for b in bs]
    nb = len(bs)
    kind = "nt" if tb else "nn"

    def body(*refs):
        a_ref, b_refs, o_refs = refs[0], refs[1:1 + nb], refs[1 + nb:]
        av = a_ref[...].astype(_MXU)
        prods = [lax.dot_general(av, b[...].astype(_MXU), _DN[kind], preferred_element_type=_F32, precision=_PREC)
                 for b in b_refs]
        for o_ref, p in zip(o_refs[:nb], prods):
            o_ref[...] = p.astype(o_ref.dtype)
        if epilogue is not None:
            for o_ref, v in zip(o_refs[nb:], _tup(epilogue(*prods))):
                o_ref[...] = v.astype(o_ref.dtype)

    widths = ns + [w for w, _ in extra_outs]
    dtypes = list(out_dtypes) + [dt for _, dt in extra_outs]
    return pl.pallas_call(
        body, name=name, grid=(m // tm,),
        in_specs=[pl.BlockSpec((tm, kdim), lambda i: (i, 0))] + [_resident(b.shape) for b in bs],
        out_specs=[pl.BlockSpec((tm, w), lambda i: (i, 0)) for w in widths],
        out_shape=[jax.ShapeDtypeStruct((m, w), dt) for w, dt in zip(widths, dtypes)],
        compiler_params=_params(("parallel",)),
    )(a, *bs)


def _mm_fan_in(pairs, *, add=None, out_dtype=_F32, prologue=None, pro_ins=(), pro_out_dtypes=(), epilogue=None,
               tm=512, name):
    bs = [b for _, b in pairs]
    nb = len(bs)
    n = bs[0].shape[1]
    rows_in = list(pro_ins) if prologue is not None else [a for a, _ in pairs]
    m = rows_in[0].shape[0]
    tm = min(tm, m)
    n_r = len(rows_in)

    def body(*refs):
        r_refs, b_refs = refs[:n_r], refs[n_r:n_r + nb]
        pos = n_r + nb
        add_ref = refs[pos] if add is not None else None
        pos += add is not None
        epi_ref = refs[pos] if epilogue is not None else None
        pos += epilogue is not None
        o_ref, po_refs = refs[pos], refs[pos + 1:]
        if prologue is not None:
            a_vals = _tup(prologue(*[r[...].astype(_F32) for r in r_refs]))
            for po, v in zip(po_refs, a_vals):
                po[...] = v.astype(po.dtype)
        else:
            a_vals = [r[...] for r in r_refs]
        acc = None
        for av, b in zip(a_vals, b_refs):
            p = lax.dot_general(av.astype(_MXU), b[...].astype(_MXU), _DN["nn"], preferred_element_type=_F32,
                                precision=_PREC)
            acc = p if acc is None else acc + p
        if add_ref is not None:
            acc = acc + add_ref[...].astype(_F32)
        o_ref[...] = acc.astype(o_ref.dtype)
        if epilogue is not None:
            po_refs[-1][...] = epilogue[0](acc, epi_ref[...]).astype(po_refs[-1].dtype)

    row = lambda w: pl.BlockSpec((tm, w), lambda i: (i, 0))
    ins = rows_in + bs + ([add] if add is not None else []) + ([epilogue[1]] if epilogue is not None else [])
    in_specs = ([row(r.shape[1]) for r in rows_in] + [_resident(b.shape) for b in bs]
                + ([row(n)] if add is not None else []) + ([_resident(epilogue[1].shape)] if epilogue is not None else []))
    extra = [(b.shape[0], dt) for b, dt in zip(bs, pro_out_dtypes)] if prologue is not None else []
    if epilogue is not None:
        extra.append((n, epilogue[2]))
    res = pl.pallas_call(
        body, name=name, grid=(m // tm,), in_specs=in_specs,
        out_specs=[row(n)] + [row(w) for w, _ in extra],
        out_shape=[jax.ShapeDtypeStruct((m, n), out_dtype)] + [jax.ShapeDtypeStruct((m, w), dt) for w, dt in extra],
        compiler_params=_params(("parallel",)),
    )(*ins)
    return res if extra else res[0]


def _row_spec(r, ts):
    if isinstance(r, tuple):
        arr, width, cblk = r
        return arr, pl.BlockSpec((ts, width), lambda i, cblk=cblk: (i, cblk))
    return r, pl.BlockSpec((ts, r.shape[1]), lambda i: (i, 0))


def _tup(v):
    return tuple(v) if isinstance(v, (tuple, list)) else (v,)


def _row_fwd(f, rows, params, outs, *, name, ts=512):
    s = (rows[0][0] if isinstance(rows[0], tuple) else rows[0]).shape[0]
    ts = min(ts, s)
    arrs, specs = zip(*[_row_spec(r, ts) for r in rows])
    n_r, n_p = len(rows), len(params)

    def body(*refs):
        rv = [r[...].astype(_F32) for r in refs[:n_r]]
        pv = [p[...] for p in refs[n_r:n_r + n_p]]
        res = _tup(f(*rv, *pv))
        for o_ref, v in zip(refs[n_r + n_p:], res):
            o_ref[...] = v.astype(o_ref.dtype)

    res = pl.pallas_call(
        body, name=name, grid=(s // ts,),
        in_specs=list(specs) + [pl.BlockSpec(p.shape, lambda i: (0, 0)) for p in params],
        out_specs=[pl.BlockSpec((ts, w), lambda i: (i, 0)) for w, _ in outs],
        out_shape=[jax.ShapeDtypeStruct((s, w), dt) for w, dt in outs],
        compiler_params=_params(("parallel",)),
    )(*arrs, *params)
    return res[0] if len(outs) == 1 else res


def _row_bwd(f, rows, params, cts, *, need=None, adds=None, row_dtypes=None, name, ts=512):
    s = (rows[0][0] if isinstance(rows[0], tuple) else rows[0]).shape[0]
    ts = min(ts, s)
    arrs, specs = zip(*[_row_spec(r, ts) for r in rows])
    n_r, n_p, n_c = len(rows), len(params), len(cts)
    need = [True] * n_r if need is None else need
    adds = {} if adds is None else adds
    add_keys = sorted(adds)
    row_dtypes = [_F32] * n_r if row_dtypes is None else row_dtypes
    needed = [j for j in range(n_r) if need[j]]
    widths = [specs[j].block_shape[1] for j in range(n_r)]

    def body(*refs):
        pos = 0
        r_refs = refs[pos:pos + n_r]; pos += n_r
        p_refs = refs[pos:pos + n_p]; pos += n_p
        c_refs = refs[pos:pos + n_c]; pos += n_c
        a_refs = refs[pos:pos + len(add_keys)]; pos += len(add_keys)
        dr_refs = refs[pos:pos + len(needed)]; pos += len(needed)
        dp_refs = refs[pos:pos + n_p]
        rv = [r[...].astype(_F32) for r in r_refs]
        pv = [p[...] for p in p_refs]
        _, vjp = jax.vjp(lambda *a: _tup(f(*a)), *rv, *pv)
        g = vjp(tuple(c[...].astype(_F32) for c in c_refs))
        for o_ref, j in zip(dr_refs, needed):
            v = g[j]
            if j in adds:
                v = v + a_refs[add_keys.index(j)][...].astype(_F32)
            o_ref[...] = v.astype(o_ref.dtype)
        if n_p:
            @pl.when(pl.program_id(0) == 0)
            def _():
                for dp in dp_refs:
                    dp[...] = jnp.zeros_like(dp)
            for dp, v in zip(dp_refs, g[n_r:]):
                dp[...] += v

    ct_specs = [pl.BlockSpec((ts, c.shape[1]), lambda i: (i, 0)) for c in cts]
    add_specs = [pl.BlockSpec((ts, adds[j].shape[1]), lambda i: (i, 0)) for j in add_keys]
    res = pl.pallas_call(
        body, name=name, grid=(s // ts,),
        in_specs=list(specs) + [pl.BlockSpec(p.shape, lambda i: (0, 0)) for p in params] + ct_specs + add_specs,
        out_specs=[pl.BlockSpec((ts, widths[j]), lambda i: (i, 0)) for j in needed]
        + [pl.BlockSpec(p.shape, lambda i: (0, 0)) for p in params],
        out_shape=[jax.ShapeDtypeStruct((s, widths[j]), row_dtypes[j]) for j in needed]
        + [jax.ShapeDtypeStruct(p.shape, _F32) for p in params],
        compiler_params=_params(("arbitrary",)),
    )(*arrs, *params, *cts, *[adds[j] for j in add_keys])
    return list(res[:len(needed)]), list(res[len(needed):])


_DN = {"nn": (((1,), (0,)), ((), ())), "nt": (((1,), (1,)), ((), ())), "tn": (((0,), (0,)), ((), ()))}


def _dot_raw(a, b, kind):
    return lax.dot_general(a.astype(_MXU), b.astype(_MXU), _DN[kind], preferred_element_type=_F32, precision=_PREC)


@functools.partial(jax.custom_vjp, nondiff_argnums=(2,))
def _dot1(a, b, kind):
    return _dot_raw(a, b, kind)


def _dot1_bwd(kind, res, ct):
    a, b = res
    if kind == "nn":
        return _dot_raw(ct, b, "nt"), _dot_raw(a, ct, "tn")
    if kind == "nt":
        return _dot_raw(ct, b, "nn"), _dot_raw(ct, a, "tn")
    return _dot_raw(b, ct, "nt"), _dot_raw(a, ct, "nn")


_dot1.defvjp(lambda a, b, kind: (_dot_raw(a, b, kind), (a, b)), _dot1_bwd)


def _sig(v):
    return 1.0 / (1.0 + jnp.exp(-v))


def _silu(v):
    return v * _sig(v)


def _f_rms(x, g):
    return x * lax.rsqrt(jnp.mean(x * x, axis=-1, keepdims=True) + EPS) * g


def _f_gate(y, xs, z, dexp, g):
    v = (y + dexp * xs) * _silu(z)
    half = SSD_WIDTH // 2
    parts = []
    for grp in range(2):
        vg = v[:, grp * half:(grp + 1) * half]
        parts.append(vg * lax.rsqrt(jnp.mean(vg * vg, axis=-1, keepdims=True) + EPS) * g[:, grp * half:(grp + 1) * half])
    return jnp.concatenate(parts, axis=1)


def _f_ln(u, g, b):
    mu = jnp.mean(u, axis=-1, keepdims=True)
    var = jnp.mean(jnp.square(u - mu), axis=-1, keepdims=True)
    return _silu((u - mu) * lax.rsqrt(var + EPS) * g + b)


def _f_swiglu(gate, up):
    return _silu(gate) * up


def _f_att(q, k, v):
    outs = []
    for h in range(X_HEADS):
        sl = slice(h * X_HEAD_DIM, (h + 1) * X_HEAD_DIM)
        s = _dot1(q[:, sl], k[:, sl], "nt") * (X_HEAD_DIM ** -0.5)
        s = s - lax.stop_gradient(jnp.max(s, axis=-1, keepdims=True))
        p = jnp.exp(s)
        p = p / jnp.sum(p, axis=-1, keepdims=True)
        outs.append(_dot1(p, v[:, sl], "nn"))
    return jnp.concatenate(outs, axis=1)


def _loss_bwd(x3, target, g, *, dx_dtype=_F32, name, ts=512):
    s, d = x3.shape

    def f(x, t, gv):
        return 0.5 * jnp.sum(jnp.mean(jnp.square(_f_rms(x, gv) - t), axis=-1))

    def body(x_ref, t_ref, g_ref, dx_ref, dg_ref, l_ref):
        @pl.when(pl.program_id(0) == 0)
        def _():
            dg_ref[...] = jnp.zeros_like(dg_ref)
            l_ref[...] = jnp.zeros_like(l_ref)

        lv, (dx, dg) = jax.value_and_grad(f, argnums=(0, 2))(x_ref[...].astype(_F32), t_ref[...], g_ref[...])
        dx_ref[...] = dx.astype(dx_ref.dtype)
        dg_ref[...] += dg
        l_ref[...] += lv

    row = pl.BlockSpec((ts, d), lambda i: (i, 0))
    return pl.pallas_call(
        body, name=name, grid=(s // ts,),
        in_specs=[row, row, pl.BlockSpec((1, d), lambda i: (0, 0))],
        out_specs=[row, pl.BlockSpec((1, d), lambda i: (0, 0)), pl.BlockSpec((SUBLANES, LANES), lambda i: (0, 0))],
        out_shape=[jax.ShapeDtypeStruct((s, d), dx_dtype), jax.ShapeDtypeStruct((1, d), _F32),
                   jax.ShapeDtypeStruct((SUBLANES, LANES), _F32)],
        compiler_params=_params(("arbitrary",)),
    )(x3, target, g)


_CONV_PAD = 32
_CONV_ROWS = 128
_CONV_CB = 128


def _conv_taps(k_taps):
    groups = {}
    for k in range(k_taps):
        j = k_taps - 1 - k
        groups.setdefault(j % SUBLANES, []).append((k, j))
    return groups


def _conv_window(win, wv, groups, init):
    pad, rows = _CONV_PAD, _CONV_ROWS
    acc = init
    for rot, taps in groups.items():
        rolled = win if rot == 0 else pltpu.roll(win, rot, 0)
        for k, j in taps:
            off = pad - (j - rot)
            acc = acc + rolled[off:off + rows, :] * wv[k:k + 1, :]
    return acc


def _conv_fill(x_refs, xp_ref, s, glu):
    pad, cb = _CONV_PAD, _CONV_CB
    step = _pick(s, (512, 256, _CONV_ROWS))
    xp_ref[0:pad, :] = jnp.zeros((pad, cb), _F32)

    def fill(r, carry):
        base = pl.multiple_of(r * step, step)
        v = x_refs[0][pl.ds(base, step), :].astype(_F32)
        if glu:
            v = v * _sig(x_refs[1][pl.ds(base, step), :].astype(_F32))
        xp_ref[pl.ds(pad + base, step), :] = v
        return carry

    lax.fori_loop(0, s // step, fill, 0)


def _conv_fwd(xs, w, b, k_taps, *, glu=False, act=False, out_dtype=_F32, name):
    s, c = xs[0].shape
    kp = w.shape[0]
    pad, rows, cb = _CONV_PAD, _CONV_ROWS, _CONV_CB
    groups = _conv_taps(k_taps)
    n_in = len(xs)

    def body(*refs):
        x_refs = refs[:n_in]
        w_ref, b_ref, o_ref, xp_ref = refs[n_in:]
        _conv_fill(x_refs, xp_ref, s, glu)
        wv = w_ref[...]
        bias = jnp.broadcast_to(b_ref[...], (rows, cb))

        def chunk(r, carry):
            base = pl.multiple_of(r * rows, rows)
            acc = _conv_window(xp_ref[pl.ds(base, rows + pad), :], wv, groups, bias)
            o_ref[pl.ds(base, rows), :] = (_silu(acc) if act else acc).astype(o_ref.dtype)
            return carry

        lax.fori_loop(0, s // rows, chunk, 0)

    col = pl.BlockSpec((s, cb), lambda i: (0, i))
    return pl.pallas_call(
        body, name=name, grid=(c // cb,),
        in_specs=[col] * n_in + [pl.BlockSpec((kp, cb), lambda i: (0, i)), pl.BlockSpec((1, cb), lambda i: (0, i))],
        out_specs=col, out_shape=jax.ShapeDtypeStruct((s, c), out_dtype),
        scratch_shapes=[pltpu.VMEM((s + pad, cb), _F32)],
        compiler_params=_params(("parallel",)),
    )(*xs, w, b)


def _conv_bwd(xs, w, b, dy, k_taps, *, glu=False, act=False, name):
    s, c = xs[0].shape
    kp = w.shape[0]
    pad, rows, cb = _CONV_PAD, _CONV_ROWS, _CONV_CB
    groups = _conv_taps(k_taps)
    win_rows = rows + pad
    n_in = len(xs)

    def fold(v):
        acc = v[0:SUBLANES, :]
        for i in range(1, rows // SUBLANES):
            acc = acc + v[i * SUBLANES:(i + 1) * SUBLANES, :]
        return acc

    def body(*refs):
        x_refs = refs[:n_in]
        w_ref, b_ref, dy_ref = refs[n_in:n_in + 3]
        dx_refs = refs[n_in + 3:2 * n_in + 3]
        dw_ref, db_ref, xp_ref, dyp_ref, acc_ref, dbacc_ref = refs[2 * n_in + 3:]
        _conv_fill(x_refs, xp_ref, s, glu)
        dyp_ref[s:s + pad, :] = jnp.zeros((pad, cb), _F32)
        acc_ref[...] = jnp.zeros_like(acc_ref)
        dbacc_ref[...] = jnp.zeros_like(dbacc_ref)
        wv = w_ref[...]
        bias = jnp.broadcast_to(b_ref[...], (rows, cb))

        def through_act(r, carry):
            base = pl.multiple_of(r * rows, rows)
            d = dy_ref[pl.ds(base, rows), :].astype(_F32)
            if act:
                pre = _conv_window(xp_ref[pl.ds(base, win_rows), :], wv, groups, bias)
                sg = _sig(pre)
                d = d * (sg * (1.0 + pre * (1.0 - sg)))
            dyp_ref[pl.ds(base, rows), :] = d
            return carry

        lax.fori_loop(0, s // rows, through_act, 0)

        def chunk(r, carry):
            base = pl.multiple_of(r * rows, rows)
            xwin = xp_ref[pl.ds(base, win_rows), :]
            dwin = dyp_ref[pl.ds(base, win_rows), :]
            dyc = dwin[0:rows, :]
            dxacc = jnp.zeros((rows, cb), _F32)
            for rot, taps in groups.items():
                xr = xwin if rot == 0 else pltpu.roll(xwin, rot, 0)
                dr = dwin if rot == 0 else pltpu.roll(dwin, win_rows - rot, 0)
                for k, j in taps:
                    a8 = j - rot
                    dxacc = dxacc + dr[a8:a8 + rows, :] * wv[k:k + 1, :]
                    prod = dyc * xr[pad - a8:pad - a8 + rows, :]
                    acc_ref[k * SUBLANES:(k + 1) * SUBLANES, :] += fold(prod)
            dbacc_ref[...] += fold(dyc)
            if glu:
                av = x_refs[0][pl.ds(base, rows), :].astype(_F32)
                sg = _sig(x_refs[1][pl.ds(base, rows), :].astype(_F32))
                dx_refs[0][pl.ds(base, rows), :] = (dxacc * sg).astype(dx_refs[0].dtype)
                dx_refs[1][pl.ds(base, rows), :] = (dxacc * av * sg * (1.0 - sg)).astype(dx_refs[1].dtype)
            else:
                dx_refs[0][pl.ds(base, rows), :] = dxacc.astype(dx_refs[0].dtype)
            return carry

        lax.fori_loop(0, s // rows, chunk, 0)
        dw_ref[...] = jnp.zeros_like(dw_ref)
        for k in range(k_taps):
            dw_ref[k:k + 1, :] = jnp.sum(acc_ref[k * SUBLANES:(k + 1) * SUBLANES, :], axis=0, keepdims=True)
        db_ref[...] = jnp.sum(dbacc_ref[...], axis=0, keepdims=True)

    col = pl.BlockSpec((s, cb), lambda i: (0, i))
    wspec = pl.BlockSpec((kp, cb), lambda i: (0, i))
    bspec = pl.BlockSpec((1, cb), lambda i: (0, i))
    dx_dtype = xs[0].dtype
    res = pl.pallas_call(
        body, name=name, grid=(c // cb,),
        in_specs=[col] * n_in + [wspec, bspec, col], out_specs=[col] * n_in + [wspec, bspec],
        out_shape=[jax.ShapeDtypeStruct((s, c), dx_dtype)] * n_in
        + [jax.ShapeDtypeStruct((kp, c), _F32), jax.ShapeDtypeStruct((1, c), _F32)],
        scratch_shapes=[pltpu.VMEM((s + pad, cb), _F32), pltpu.VMEM((s + pad, cb), _F32),
                        pltpu.VMEM((kp * SUBLANES, cb), _F32), pltpu.VMEM((SUBLANES, cb), _F32)],
        compiler_params=_params(("parallel",)),
    )(*xs, w, b, dy)
    return list(res[:n_in]), res[n_in], res[n_in + 1]


def _tri_sum(v, lower):
    l = v.shape[0]
    r, c = lax.broadcasted_iota(jnp.int32, (l, l), 0), lax.broadcasted_iota(jnp.int32, (l, l), 1)
    tri = ((r >= c) if lower else (r <= c)).astype(jnp.bfloat16)
    hi = v.astype(jnp.bfloat16)
    r1 = v - hi.astype(_F32)
    mid = r1.astype(jnp.bfloat16)
    lo = (r1 - mid.astype(_F32)).astype(jnp.bfloat16)
    out = jnp.zeros_like(v)
    for part in (hi, mid, lo):
        out = out + lax.dot_general(tri, part, _DN["nn"], preferred_element_type=_F32)
    return out


@jax.custom_vjp
def _cumsum_rows(v):
    return _tri_sum(v, True)


_cumsum_rows.defvjp(lambda v: (_tri_sum(v, True), None), lambda _, ct: (_tri_sum(ct, False),))


def _ssd_chunk(xbc, dtraw, prev, bias, alog):
    l = xbc.shape[0]
    xs = xbc[:, :SSD_WIDTH]
    bm = xbc[:, SSD_WIDTH:SSD_WIDTH + 2 * SSD_STATE]
    cm = xbc[:, SSD_WIDTH + 2 * SSD_STATE:]
    v = dtraw + bias
    dt = jnp.maximum(v, 0.0) + jnp.log1p(jnp.exp(-jnp.abs(v)))
    a_neg = -jnp.exp(alog)
    acs = _cumsum_rows(dt * a_neg)
    acs_t = acs.T
    total = acs[l - 1:l, :]
    row = lax.broadcasted_iota(jnp.int32, (l, l), 0)
    colv = lax.broadcasted_iota(jnp.int32, (l, l), 1)
    causal = row >= colv
    lane_lo = lax.broadcasted_iota(jnp.int32, (l, LANES), 1) < HEAD_DIM
    row_lo = lax.broadcasted_iota(jnp.int32, (LANES, SSD_STATE), 0) < HEAD_DIM

    def pair_lanes(m, h0):
        return jnp.where(lane_lo, m[:, h0:h0 + 1], m[:, h0 + 1:h0 + 2])

    ys, news = [], []
    cb = {}
    for j in range(SSD_HEADS // 2):
        h0 = 2 * j
        grp = h0 // (SSD_HEADS // 2)
        bg = bm[:, grp * SSD_STATE:(grp + 1) * SSD_STATE]
        cg = cm[:, grp * SSD_STATE:(grp + 1) * SSD_STATE]
        if grp not in cb:
            cb[grp] = _dot1(cg, bg, "nt")
        xdt = xs[:, j * LANES:(j + 1) * LANES] * pair_lanes(dt, h0)
        y = jnp.zeros((l, LANES), _F32)
        for hh, mask in ((h0, lane_lo), (h0 + 1, jnp.logical_not(lane_lo))):
            seg = acs[:, hh:hh + 1] - acs_t[hh:hh + 1, :]
            dec = jnp.exp(jnp.where(causal, seg, -jnp.inf))
            y = y + _dot1(cb[grp] * dec, jnp.where(mask, xdt, 0.0), "nn")
        acs_p = pair_lanes(acs, h0)
        prev_p = prev[j * LANES:(j + 1) * LANES, :]
        y = y + _dot1(cg, prev_p, "nt") * jnp.exp(acs_p)
        wgt = jnp.exp(pair_lanes(jnp.broadcast_to(total, (l, LANES)), h0) - acs_p)
        st = _dot1(xdt * wgt, bg, "tn")
        cdec = jnp.exp(jnp.where(row_lo, total[:, h0:h0 + 1], total[:, h0 + 1:h0 + 2]))
        news.append(prev_p * cdec + st)
        ys.append(y)
    return jnp.concatenate(ys, axis=1), jnp.concatenate(news, axis=0)


def _ssd_gate_chunk(xbc, dtraw, prev, bias, alog, z, dexp, g):
    y, new = _ssd_chunk(xbc, dtraw, prev, bias, alog)
    return _f_gate(y, xbc[:, :SSD_WIDTH], z, dexp, g), new


def _ssd_fwd(xbc, dtraw, bias, alog, z, dexp, g, *, name):
    s = xbc.shape[0]
    nc = s // CHUNK
    nstate = SSD_HEADS * HEAD_DIM

    def body(x_ref, dt_ref, b_ref, a_ref, z_ref, d_ref, g_ref, y_ref, st_ref, state_ref):
        @pl.when(pl.program_id(0) == 0)
        def _():
            state_ref[...] = jnp.zeros_like(state_ref)

        prev = state_ref[...]
        st_ref[...] = prev
        y, new = _ssd_gate_chunk(x_ref[...].astype(_F32), dt_ref[...], prev, b_ref[...], a_ref[...], z_ref[...].astype(_F32),
                                 d_ref[...], g_ref[...])
        y_ref[...] = y.astype(y_ref.dtype)
        state_ref[...] = new

    small = pl.BlockSpec((1, LANES), lambda i: (0, 0))
    wide = pl.BlockSpec((1, SSD_WIDTH), lambda i: (0, 0))
    rows = pl.BlockSpec((CHUNK, SSD_WIDTH), lambda i: (i, 0))
    return pl.pallas_call(
        body, name=name, grid=(nc,),
        in_specs=[pl.BlockSpec((CHUNK, XBC_WIDTH), lambda i: (i, 0)), pl.BlockSpec((CHUNK, LANES), lambda i: (i, 0)),
                  small, small, rows, wide, wide],
        out_specs=[rows, pl.BlockSpec((None, nstate, SSD_STATE), lambda i: (i, 0, 0))],
        out_shape=[jax.ShapeDtypeStruct((s, SSD_WIDTH), _MXU), jax.ShapeDtypeStruct((nc, nstate, SSD_STATE), _F32)],
        scratch_shapes=[pltpu.VMEM((nstate, SSD_STATE), _F32)],
        compiler_params=_params(("arbitrary",)),
    )(xbc, dtraw, bias, alog, z, dexp, g)


def _ssd_bwd(xbc, dtraw, states, bias, alog, z, dexp, g, dy, *, name):
    s = xbc.shape[0]
    nc = s // CHUNK
    nstate = SSD_HEADS * HEAD_DIM

    def body(x_ref, dt_ref, st_ref, b_ref, a_ref, z_ref, d_ref, g_ref, dy_ref,
             dx_ref, ddt_ref, db_ref, da_ref, dz_ref, dd_ref, dg_ref, dstate_ref):
        @pl.when(pl.program_id(0) == 0)
        def _():
            dstate_ref[...] = jnp.zeros_like(dstate_ref)
            for acc in (db_ref, da_ref, dd_ref, dg_ref):
                acc[...] = jnp.zeros_like(acc)

        _, vjp = jax.vjp(_ssd_gate_chunk, x_ref[...].astype(_F32), dt_ref[...], st_ref[...], b_ref[...], a_ref[...],
                         z_ref[...].astype(_F32), d_ref[...], g_ref[...])
        dx, ddt, dprev, db, da, dz, dd, dg = vjp((dy_ref[...].astype(_F32), dstate_ref[...]))
        dx_ref[...] = dx.astype(dx_ref.dtype)
        ddt_ref[...] = ddt
        dz_ref[...] = dz.astype(dz_ref.dtype)
        db_ref[...] += db
        da_ref[...] += da
        dd_ref[...] += dd
        dg_ref[...] += dg
        dstate_ref[...] = dprev

    rev = lambda i: (nc - 1 - i, 0)
    small = pl.BlockSpec((1, LANES), lambda i: (0, 0))
    wide = pl.BlockSpec((1, SSD_WIDTH), lambda i: (0, 0))
    rows = pl.BlockSpec((CHUNK, SSD_WIDTH), rev)
    return pl.pallas_call(
        body, name=name, grid=(nc,),
        in_specs=[pl.BlockSpec((CHUNK, XBC_WIDTH), rev), pl.BlockSpec((CHUNK, LANES), rev),
                  pl.BlockSpec((None, nstate, SSD_STATE), lambda i: (nc - 1 - i, 0, 0)), small, small, rows, wide, wide,
                  rows],
        out_specs=[pl.BlockSpec((CHUNK, XBC_WIDTH), rev), pl.BlockSpec((CHUNK, LANES), rev), small, small, rows, wide, wide],
        out_shape=[jax.ShapeDtypeStruct((s, XBC_WIDTH), xbc.dtype), jax.ShapeDtypeStruct((s, LANES), _F32),
                   jax.ShapeDtypeStruct((1, LANES), _F32), jax.ShapeDtypeStruct((1, LANES), _F32),
                   jax.ShapeDtypeStruct((s, SSD_WIDTH), z.dtype), jax.ShapeDtypeStruct((1, SSD_WIDTH), _F32),
                   jax.ShapeDtypeStruct((1, SSD_WIDTH), _F32)],
        scratch_shapes=[pltpu.VMEM((nstate, SSD_STATE), _F32)],
        compiler_params=_params(("arbitrary",)),
    )(xbc, dtraw, states, bias, alog, z, dexp, g, dy)


def _pad_cols(a, width):
    return jnp.pad(a, ((0, 0), (0, width - a.shape[1])))


def _pad_rows(a, rows):
    return jnp.pad(a, ((0, rows - a.shape[0]), (0, 0)))


def _tie(a, token):
    return a + token[0:1, 0:1].astype(a.dtype)


def _local_step(x, mem, target, w, fetch, emit):
    bf = _MXU
    d = D_MODEL
    h = _row_fwd(_f_rms, [x], [w['norm_mix_g']], [(d, bf)], name="f_norm_mix")
    w_in = fetch('in', h)['w_in']
    z_end, xbc_end, dt_end = SSD_WIDTH, SSD_WIDTH + XBC_WIDTH, SSD_WIDTH + XBC_WIDTH + SSD_HEADS
    w_z, w_xbc = w_in[:z_end], w_in[z_end:xbc_end]
    w_dt = _pad_rows(w_in[xbc_end:dt_end], LANES)
    w_a, w_g = w_in[dt_end:dt_end + CF_WIDTH], w_in[dt_end + CF_WIDTH:]
    dt_bias = _pad_cols(w['ssd_dt_bias'], LANES)
    a_log = _pad_cols(w['ssd_A_log'], LANES)
    d_exp = jnp.repeat(w['ssd_D'], HEAD_DIM, axis=1)
    g_final = w['norm_final_g'].reshape(1, D_MODEL)

    z, xbc, dtr, ga, gg = _mm_fan_out(h, [w_z, w_xbc, w_dt, w_a, w_g], tb=True, out_dtypes=[bf, bf, _F32, bf, bf],
                                      name="f_in")
    wc = fetch('conv', xbc)
    ssd_w = _pad_rows(wc['ssd_conv_w'], SUBLANES)
    cf_w = _pad_rows(wc['cf_conv_w'], 32)
    xbc_a = _conv_fwd([xbc], ssd_w, w['ssd_conv_b'], SSD_CONV, act=True, out_dtype=bf, name="f_ssd_conv")
    y_n, states = _ssd_fwd(xbc_a, dtr, dt_bias, a_log, z, d_exp, w['ssd_norm_g'], name="f_ssd")
    u_c = _conv_fwd([ga, gg], cf_w, w['cf_conv_b'], CF_CONV, glu=True, out_dtype=bf, name="f_cf_conv")
    u = _row_fwd(_f_ln, [u_c], [w['cf_ln_g'], w['cf_ln_b']], [(d, bf)], name="f_cf_ln")
    wm = fetch('mid', y_n)
    w_out_y, w_out_u = wm['w_out'][:SSD_WIDTH], wm['w_out'][SSD_WIDTH:]
    x1, hq = _mm_fan_in([(y_n, w_out_y), (u, w_out_u)], add=x, out_dtype=bf, epilogue=(_f_rms, w['norm_xattn_g'], bf),
                        name="f_out")
    q = _mm(hq, wm['w_q'], out_dtype=bf, name="f_q")
    memn = _row_fwd(_f_rms, [mem], [w['norm_mem_g']], [(d, bf)], name="f_norm_mem")
    kv = _mm(memn, wm['w_kv'], name="f_kv")
    k_mat, v_mat = kv[:, :d], kv[:, d:]
    o = _row_fwd(_f_att, [q], [k_mat, v_mat], [(d, bf)], ts=512, name="f_att")
    x2, hf = _mm_fan_in([(o, wm['w_o'])], add=x1, out_dtype=bf, epilogue=(_f_rms, w['norm_ffn_g'], bf), name="f_o")
    wf = fetch('ffn', hf)
    gate, up, act = _mm_fan_out(hf, [wf['w_gate'], wf['w_up']], tb=True, out_dtypes=[bf, bf], epilogue=_f_swiglu,
                                extra_outs=[(D_FF, bf)], tm=256, name="f_ffn_in")
    x3 = _mm(act, wf['w_down'], add=x2, out_dtype=bf, name="f_down")

    dx3, dg_final, loss = _loss_bwd(x3, target, g_final, dx_dtype=bf, name="b_loss")
    g = {'norm_final_g': dg_final.reshape(d)}

    dact = _mm(dx3, wf['w_down'], tb=True, out_dtype=bf, name="b_down_x")
    dw_down = _mm(act, dx3, ta=True, out_dtype=bf, name="b_down_w")
    def swiglu_bwd(gate_t, up_t, dact_t):
        return jax.vjp(_f_swiglu, gate_t, up_t)[1](dact_t)

    dhf, dgate, dup = _mm_fan_in([(None, wf['w_gate']), (None, wf['w_up'])], prologue=swiglu_bwd, pro_ins=[gate, up, dact],
                                 pro_out_dtypes=[bf, bf], out_dtype=bf, tm=256, name="b_ffn_in_x")
    sent = emit({'w_down': dw_down, 'w_gate': _mm(dgate, hf, ta=True, out_dtype=bf, name="b_gate_w"),
                 'w_up': _mm(dup, hf, ta=True, out_dtype=bf, name="b_up_w")})
    (dx2,), (g['norm_ffn_g'],) = _row_bwd(_f_rms, [x2], [_tie(w['norm_ffn_g'], sent)], [dhf], adds={0: dx3}, row_dtypes=[bf], name="b_norm_ffn")

    do = _mm(dx2, wm['w_o'], tb=True, out_dtype=bf, name="b_o_x")
    dw_o = _mm(o, dx2, ta=True, out_dtype=bf, name="b_o_w")
    (dq,), (dk, dv) = _row_bwd(_f_att, [q], [k_mat, v_mat], [do], row_dtypes=[bf], ts=512, name="b_att")
    dw_q = _mm(hq, dq, ta=True, out_dtype=bf, name="b_q_w")
    dhq = _mm(dq, wm['w_q'], tb=True, out_dtype=bf, name="b_q_x")
    (dx1,), (g['norm_xattn_g'],) = _row_bwd(_f_rms, [x1], [w['norm_xattn_g']], [dhq], adds={0: dx2}, row_dtypes=[bf], name="b_norm_xattn")
    dkv = jnp.concatenate([dk, dv], axis=1)
    dmemn = _mm(dkv, wm['w_kv'], tb=True, name="b_kv_x")
    _, (g['norm_mem_g'],) = _row_bwd(_f_rms, [mem], [w['norm_mem_g']], [dmemn], need=[False], name="b_norm_mem")
    sent = emit({'w_o': dw_o, 'w_q': dw_q, 'w_kv': _mm(memn, dkv, ta=True, out_dtype=bf, name="b_kv_w")},
                after=g['norm_mem_g'])

    dyn, du = _mm_fan_out(dx1, [w_out_y, w_out_u], tb=True, out_dtypes=[bf, bf], name="b_out_x")
    (du_c,), (g['cf_ln_g'], g['cf_ln_b']) = _row_bwd(_f_ln, [u_c], [_tie(w['cf_ln_g'], sent), w['cf_ln_b']], [du], row_dtypes=[bf], name="b_cf_ln")
    sent = emit({'w_out': jnp.concatenate([_mm(y_n, dx1, ta=True, out_dtype=bf, name="b_out_y_w"), _mm(u, dx1, ta=True, out_dtype=bf, name="b_out_u_w")], axis=0)})
    (dga, dgg), dcf_w, g['cf_conv_b'] = _conv_bwd([ga, gg], cf_w, w['cf_conv_b'], du_c, CF_CONV, glu=True, name="b_cf_conv")
    g['cf_conv_w'] = dcf_w[:CF_CONV]
    dxbc_a, ddtr, ddt_bias, da_log, dz, dd_exp, g['ssd_norm_g'] = _ssd_bwd(
        xbc_a, dtr, states, dt_bias, a_log, z, d_exp, _tie(w['ssd_norm_g'], sent), dyn, name="b_ssd")
    g['ssd_D'] = jnp.sum(dd_exp.reshape(SSD_HEADS, HEAD_DIM), axis=1).reshape(1, SSD_HEADS)
    g['ssd_dt_bias'] = ddt_bias[:, :SSD_HEADS]
    g['ssd_A_log'] = da_log[:, :SSD_HEADS]
    (dxbc,), dssd_w, g['ssd_conv_b'] = _conv_bwd([xbc], ssd_w, w['ssd_conv_b'], dxbc_a, SSD_CONV, act=True, name="b_ssd_conv")
    g['ssd_conv_w'] = dssd_w[:SSD_CONV]

    sent = emit({'w_in': jnp.concatenate([
        _mm(dz, h, ta=True, out_dtype=bf, name="b_in_z_w"), _mm(dxbc, h, ta=True, out_dtype=bf, name="b_in_xbc_w"),
        _mm(ddtr, h, ta=True, out_dtype=bf, name="b_in_dt_w")[:SSD_HEADS],
        _mm(dga, h, ta=True, out_dtype=bf, name="b_in_a_w"), _mm(dgg, h, ta=True, out_dtype=bf, name="b_in_g_w")], axis=0)})
    dh = _mm_fan_in([(dz, w_z), (dxbc, w_xbc), (ddtr, _tie(w_dt, sent)), (dga, w_a), (dgg, w_g)], out_dtype=bf,
                    name="b_in_x")
    (dx,), (g['norm_mix_g'],) = _row_bwd(_f_rms, [x], [w['norm_mix_g']], [dh], adds={0: dx1}, name="b_norm_mix")
    return loss, dx, g


_ANY = pl.BlockSpec(memory_space=pl.ANY)


def _place():
    x, y, c = lax.axis_index("x"), lax.axis_index("y"), lax.axis_index("c")
    return x, y, c


_HBM =pl.BlockSpec(memory_space=pltpu.HBM)
_SEM = pl.BlockSpec(memory_space=pltpu.SEMAPHORE)
_EFFECT = pltpu.SideEffectType.DATAFLOW_SIDE_EFFECTING
_FLIPS = [(dx, dy, dc) for dx in (0, 1) for dy in (0, 1) for dc in (0, 1)][1:]


def _peer(flip, x, y, c):
    return (1 - x if flip[0] else x, 1 - y if flip[1] else y, 1 - c if flip[2] else c)


_CHIP_FLIPS = [f for f in _FLIPS if f[2] == 0]


def _send_start(srcs, blocked, *, flips=None, after=None, name):
    n = len(srcs)
    flips = _FLIPS if flips is None else flips
    nf = len(flips)
    lands = [jax.ShapeDtypeStruct(s.shape if blocked else (N_DEV,) + s.shape, s.dtype) for s in srcs]
    n_in = 2 * n + (after is not None)

    def body(*refs):
        src_refs, land_refs = refs[:n], refs[n:2 * n]
        send_sems, recv_sems = refs[n_in], refs[n_in + 1]
        token = refs[-1]
        x, y, c = _place()
        me = 4 * x + 2 * y + c
        for a in range(n):
            for k, flip in enumerate(flips):
                p = _peer(flip, x, y, c)
                src = src_refs[a].at[4 * p[0] + 2 * p[1] + p[2]] if blocked else src_refs[a]
                pltpu.make_async_remote_copy(
                    src_ref=src, dst_ref=land_refs[a].at[me], send_sem=send_sems.at[nf * a + k],
                    recv_sem=recv_sems.at[nf * a + k], device_id=p, device_id_type=MESH).start()
        token[...] = jnp.zeros_like(token)

    res = pl.pallas_call(
        body, name=name,
        out_shape=(pltpu.SemaphoreType.DMA((nf * n,)), pltpu.SemaphoreType.DMA((nf * n,)),
                   *[pltpu.HBM(s.shape, s.dtype) for s in srcs], *[pltpu.HBM(l.shape, l.dtype) for l in lands],
                   jax.ShapeDtypeStruct((SUBLANES, LANES), _F32)),
        in_specs=[_HBM] * (2 * n) + [_ANY] * (after is not None),
        out_specs=(_SEM, _SEM, *[_HBM] * (2 * n), pl.BlockSpec(memory_space=pltpu.VMEM)),
        input_output_aliases={i: 2 + i for i in range(2 * n)},
        compiler_params=pltpu.CompilerParams(has_side_effects=_EFFECT),
    )(*[pltpu.with_memory_space_constraint(s, pltpu.HBM) for s in srcs],
      *[pltpu.with_memory_space_constraint(lax.empty(l.shape, l.dtype), pltpu.HBM) for l in lands],
      *([after] if after is not None else []))
    return res[0], res[1], list(res[2:2 + n]), list(res[2 + n:2 + 2 * n]), res[-1]


def _send_wait(handles, after, blocked, *, flips=None, name):
    send_sems, recv_sems, srcs, lands, _ = handles
    n = len(srcs)
    flips = _FLIPS if flips is None else flips
    nf = len(flips)

    def body(*refs):
        src_refs, land_refs = refs[:n], refs[n:2 * n]
        send_sems, recv_sems = refs[2 * n], refs[2 * n + 1]
        x, y, c = _place()
        for a in range(n):
            for k, flip in enumerate(flips):
                p = _peer(flip, x, y, c)
                pid = 4 * p[0] + 2 * p[1] + p[2]
                cp = pltpu.make_async_remote_copy(
                    src_ref=src_refs[a].at[pid] if blocked else src_refs[a], dst_ref=land_refs[a].at[pid],
                    send_sem=send_sems.at[nf * a + k], recv_sem=recv_sems.at[nf * a + k], device_id=p, device_id_type=MESH)
                cp.wait_send()
                cp.wait_recv()

    res = pl.pallas_call(
        body, name=name,
        out_shape=tuple(pltpu.HBM(s.shape, s.dtype) for s in srcs + lands),
        in_specs=[_HBM] * (2 * n) + [_SEM, _SEM, _ANY], out_specs=tuple([_HBM] * (2 * n)),
        input_output_aliases={i: i for i in range(2 * n)},
        compiler_params=pltpu.CompilerParams(has_side_effects=_EFFECT),
    )(*srcs, *lands, send_sems, recv_sems, after)
    return list(res[:n]), list(res[n:])


def _sibling_share(lands, owns, *, name):
    n = len(lands)

    def body(*refs):
        own_refs, land_refs = refs[n:2 * n], refs[2 * n:3 * n]
        send_sems, recv_sems = refs[3 * n], refs[3 * n + 1]
        x, y, c = _place()
        sibling = (x, y, 1 - c)
        chips = [(1 - x, y), (x, 1 - y), (1 - x, 1 - y)]

        def copy(a, k, block, src=None):
            slot = land_refs[a].at[block]
            return pltpu.make_async_remote_copy(
                src_ref=slot if src is None else src, dst_ref=slot, send_sem=send_sems.at[4 * a + k],
                recv_sem=recv_sems.at[4 * a + k], device_id=sibling, device_id_type=MESH)

        sends = []
        for a in range(n):
            sends.append(copy(a, 0, 4 * x + 2 * y + c, src=own_refs[a]))
            sends += [copy(a, 1 + j, 4 * cx + 2 * cy + c) for j, (cx, cy) in enumerate(chips)]
        for cp in sends:
            cp.start()
        for a in range(n):
            copy(a, 0, 4 * x + 2 * y + (1 - c)).wait_recv()
            for j, (cx, cy) in enumerate(chips):
                copy(a, 1 + j, 4 * cx + 2 * cy + (1 - c)).wait_recv()
        for cp in sends:
            cp.wait_send()

    return pl.pallas_call(
        body, name=name, in_specs=[_ANY] * (2 * n), out_specs=[_ANY] * n,
        out_shape=[jax.ShapeDtypeStruct(l.shape, l.dtype) for l in lands],
        input_output_aliases={i: i for i in range(n)},
        scratch_shapes=[pltpu.SemaphoreType.DMA((4 * n,)), pltpu.SemaphoreType.DMA((4 * n,))],
    )(*lands, *owns)


def _adamw(parts, w, m, v, *, own=None, me=None, name):
    p, r, c = parts.shape
    tr = _pick(r, (256, 176, 128, 64, 32, 16, 8))
    if own is not None:
        tc = c if tr < r else _pick(c, (256, 128))
        return _adamw_own(parts, own, me, w, m, v, tr, tc, name=name)

    def body(p_ref, w_ref, m_ref, v_ref, g_ref, d_ref, nm_ref, nv_ref):
        g = p_ref[0].astype(_F32)
        for i in range(1, p):
            g = g + p_ref[i].astype(_F32)
        _adamw_math(g, w_ref, m_ref, v_ref, g_ref, d_ref, nm_ref, nv_ref)

    blk = pl.BlockSpec((tr, c), lambda i: (i, 0))
    return pl.pallas_call(
        body, name=name, grid=(r // tr,),
        in_specs=[pl.BlockSpec((p, tr, c), lambda i: (0, i, 0)), blk, blk, blk], out_specs=[blk] * 4,
        out_shape=[jax.ShapeDtypeStruct((r, c), _F32)] * 4,
        compiler_params=_params(("parallel",)),
    )(parts, w, m, v)


def _adamw_math(g, w_ref, m_ref, v_ref, g_ref, d_ref, nm_ref, nv_ref):
    wv = w_ref[...]
    mn = ADAM_B1 * m_ref[...] + (1.0 - ADAM_B1) * g
    vn = ADAM_B2 * v_ref[...] + (1.0 - ADAM_B2) * jnp.square(g)
    m_hat = mn / (1.0 - ADAM_B1 ** ADAM_STEP)
    v_hat = vn / (1.0 - ADAM_B2 ** ADAM_STEP)
    g_ref[...] = g
    d_ref[...] = -ADAM_LR * (m_hat / (jnp.sqrt(v_hat) + ADAM_EPS) + ADAM_WD * wv)
    nm_ref[...] = mn
    nv_ref[...] = vn


def _adamw_own(parts, own, me, w, m, v, tr, tc, *, name):
    p, r, c = parts.shape

    def body(me_ref, p_ref, own_ref, w_ref, m_ref, v_ref, g_ref, d_ref, nm_ref, nv_ref):
        mine = own_ref[...].astype(_F32)
        g = jnp.where(me_ref[0] == 0, mine, p_ref[0].astype(_F32))
        for i in range(1, p):
            g = g + jnp.where(me_ref[0] == i, mine, p_ref[i].astype(_F32))
        _adamw_math(g, w_ref, m_ref, v_ref, g_ref, d_ref, nm_ref, nv_ref)

    blk = pl.BlockSpec((tr, tc), lambda i, j, me_ref: (i, j))
    grid_spec = pltpu.PrefetchScalarGridSpec(
        num_scalar_prefetch=1, grid=(r // tr, c // tc),
        in_specs=[pl.BlockSpec((p, tr, tc), lambda i, j, me_ref: (0, i, j)),
                  pl.BlockSpec((None, tr, tc), lambda i, j, me_ref: (me_ref[0], i, j)), blk, blk, blk],
        out_specs=[blk] * 4)
    return pl.pallas_call(
        body, name=name, grid_spec=grid_spec, out_shape=[jax.ShapeDtypeStruct((r, c), _F32)] * 4,
        compiler_params=_params(("parallel", "parallel")),
    )(me.reshape(1).astype(jnp.int32), parts, own, w, m, v)


def _adamw_rows(g_row, offsets, ws, ms, vs, *, name):
    k = len(ws)

    def body(*refs):
        g_ref, w_refs, m_refs, v_refs = refs[0], refs[1:1 + k], refs[1 + k:1 + 2 * k], refs[1 + 2 * k:1 + 3 * k]
        outs = refs[1 + 3 * k:]
        for i in range(k):
            gi = g_ref[:, offsets[i]:offsets[i] + ws[i].shape[1]]
            _adamw_math(gi, w_refs[i], m_refs[i], v_refs[i], *outs[4 * i:4 * i + 4])

    return pl.pallas_call(
        body, name=name, out_shape=[jax.ShapeDtypeStruct(w.shape, _F32) for w in ws for _ in range(4)],
    )(g_row, *ws, *ms, *vs)


def _sum_parts(parts, *, name):
    p, r, c = parts.shape

    def body(p_ref, o_ref):
        g = p_ref[0].astype(_F32)
        for i in range(1, p):
            g = g + p_ref[i].astype(_F32)
        o_ref[...] = g

    return pl.pallas_call(body, name=name, out_shape=jax.ShapeDtypeStruct((r, c), _F32))(parts)


def kernel(x, mem, norm_mix_g, w_in, ssd_conv_w, ssd_conv_b, ssd_dt_bias, ssd_A_log, ssd_D, ssd_norm_g, cf_conv_w, cf_conv_b, cf_ln_g, cf_ln_b, w_out, norm_xattn_g, norm_mem_g, w_q, w_kv, w_o, norm_ffn_g, w_gate, w_up, w_down, norm_final_g, loss_target, m_norm_mix_g, m_w_in, m_ssd_conv_w, m_ssd_conv_b, m_ssd_dt_bias, m_ssd_A_log, m_ssd_D, m_ssd_norm_g, m_cf_conv_w, m_cf_conv_b, m_cf_ln_g, m_cf_ln_b, m_w_out, m_norm_xattn_g, m_norm_mem_g, m_w_q, m_w_kv, m_w_o, m_norm_ffn_g, m_w_gate, m_w_up, m_w_down, m_norm_final_g, v_norm_mix_g, v_w_in, v_ssd_conv_w, v_ssd_conv_b, v_ssd_dt_bias, v_ssd_A_log, v_ssd_D, v_ssd_norm_g, v_cf_conv_w, v_cf_conv_b, v_cf_ln_g, v_cf_ln_b, v_w_out, v_norm_xattn_g, v_norm_mem_g, v_w_q, v_w_kv, v_w_o, v_norm_ffn_g, v_w_gate, v_w_up, v_w_down, v_norm_final_g):
    args = dict(locals())
    wts = {n: args[n] for n in WEIGHT_NAMES}
    mom = {n: args["m_" + n] for n in WEIGHT_NAMES}
    var = {n: args["v_" + n] for n in WEIGHT_NAMES}
    me = 4 * lax.axis_index("x") + 2 * lax.axis_index("y") + lax.axis_index("c")

    groups = {'in': ['w_in'], 'conv': ['ssd_conv_w', 'cf_conv_w'], 'mid': ['w_out', 'w_q', 'w_kv', 'w_o'],
              'ffn': ['w_gate', 'w_up', 'w_down']}
    def shard(n, a):
        return jnp.transpose(a[0], (1, 0)) if n in TRANSPOSED else a[0]

    two_level = {'in': _CHIP_FLIPS}
    gathers, started = {}, None
    for grp, names in groups.items():
        shards = [wts[n][0] if grp == 'conv' else shard(n, wts[n]).astype(_MXU) for n in names]
        gathers[grp] = _send_start(shards, False, flips=two_level.get(grp), after=started, name="gather_%s_start" % grp)
        started = gathers[grp][4]

    def fetch(grp, after):
        srcs, lands = _send_wait(gathers[grp], started if after is None else after, False, flips=two_level.get(grp),
                                 name="gather_%s_wait" % grp)
        if grp in two_level:
            lands = _sibling_share(lands, srcs, name="gather_%s_share" % grp)
        out = {}
        for n, own, gth in zip(groups[grp], srcs, lands):
            gth = lax.dynamic_update_slice_in_dim(gth, own[None], me, axis=0)
            if n == 'w_kv' or grp == 'conv':
                out[n] = jnp.transpose(gth, (1, 0, 2)).reshape(gth.shape[1], N_DEV * gth.shape[2])
            else:
                out[n] = gth.reshape(N_DEV * gth.shape[1], gth.shape[2])
        return out

    exchanges = []

    def emit(grads, after=None):
        blocks = []
        for n, gw in grads.items():
            if n == 'w_kv':
                gw = jnp.transpose(gw.reshape(gw.shape[0], N_DEV, gw.shape[1] // N_DEV), (1, 0, 2))
            else:
                gw = gw.reshape(N_DEV, gw.shape[0] // N_DEV, gw.shape[1])
            blocks.append(gw.astype(jnp.bfloat16))
        first = next(iter(grads))
        exchanges.append((list(grads), _send_start(blocks, True, after=after, name="exchange_%s_start" % first), first))
        return exchanges[-1][1][4]

    full = {n: wts[n] for n in WEIGHT_NAMES if n not in BIG and n not in groups['conv']}
    full['norm_mix_g'] = _tie(norm_mix_g, started)

    loss_blk, grad_x, g = _local_step(x[0], mem[0], loss_target[0], full, fetch, emit)

    small = [n for n in WEIGHT_NAMES if n not in BIG]
    g['loss'] = loss_blk[0:1, 0:1]
    items = small + ['loss']
    size = {n: math.prod(g[n].shape) for n in items}
    seg = {n: -(-size[n] // LANES) * LANES for n in items}
    off, pos = {}, 0
    for n in items:
        off[n], pos = pos, pos + seg[n]
    rows = -(-pos // (LANES * SUBLANES)) * SUBLANES
    flat = jnp.concatenate([jnp.pad(g[n].reshape(-1), (0, seg[n] - size[n])) for n in items]
                           + [jnp.zeros((rows * LANES - pos,), _F32)])
    small_sent = _send_start([flat.reshape(rows, LANES)], False, name="gather_small_start")

    out_g, out_d, out_m, out_v = {}, {}, {}, {}
    done = small_sent[4]
    for names, handles, first in exchanges:
        srcs, lands = _send_wait(handles, done, True, name="exchange_%s_wait" % first)
        for n, own, parts in zip(names, srcs, lands):
            res = _adamw(parts, shard(n, wts[n]), shard(n, mom[n]), shard(n, var[n]), own=own, me=me, name="adamw_" + n)
            out_g[n], out_d[n], out_m[n], out_v[n] = [(jnp.transpose(r, (1, 0)) if n in TRANSPOSED else r)[None] for r in res]
            done = res[0]

    srcs, lands = _send_wait(small_sent, out_g[exchanges[-1][0][-1]], False, name="gather_small_wait")
    small_parts = lax.dynamic_update_slice_in_dim(lands[0], srcs[0][None], me, axis=0)
    g_row = _sum_parts(small_parts, name="sum_small_grads").reshape(1, rows * LANES)
    loss = g_row[0, off['loss']]
    rep = [n for n in small if n not in groups['conv']]
    as_row = lambda a: a.reshape(1, -1)
    res = _adamw_rows(g_row, [off[n] for n in rep], [as_row(wts[n]) for n in rep], [as_row(mom[n]) for n in rep],
                      [as_row(var[n]) for n in rep], name="adamw_small")
    for i, n in enumerate(rep):
        out_g[n], out_d[n], out_m[n], out_v[n] = [r.reshape(wts[n].shape) for r in res[4 * i:4 * i + 4]]
    for n in groups['conv']:
        k_taps, width = g[n].shape
        g_full = g_row[0, off[n]:off[n] + size[n]].reshape(k_taps, width)
        g_mine = lax.dynamic_slice_in_dim(g_full, me * (width // N_DEV), width // N_DEV, axis=1)
        res = _adamw(g_mine[None], wts[n][0], mom[n][0], var[n][0], name="adamw_" + n)
        out_g[n], out_d[n], out_m[n], out_v[n] = [r[None] for r in res]

    return (loss, grad_x[None], *[out_g[n] for n in WEIGHT_NAMES], *[out_d[n] for n in WEIGHT_NAMES],
            *[out_m[n] for n in WEIGHT_NAMES], *[out_v[n] for n in WEIGHT_NAMES])
```

```python
import functools
import math

import jax
import jax.numpy as jnp
from jax import lax
from jax.experimental import pallas as pl
from jax.experimental.pallas import tpu as pltpu

_F32 = jnp.float32
_MXU = jnp.bfloat16
_PREC = None
_VMEM_LIMIT = 56 * 1024 * 1024

D_MODEL = 1024
HEAD_DIM = 64
SSD_HEADS = 16
SSD_WIDTH = 1024
SSD_STATE = 128
SSD_CONV = 4
CHUNK = 128
XBC_WIDTH = 1536
CF_WIDTH = 1024
CF_CONV = 31
X_HEADS = 4
X_HEAD_DIM = 256
D_FF = 2816
EPS = 1e-6
N_DEV = 8
LANES = 128
SUBLANES = 8

ADAM_LR = 0.001
ADAM_B1 = 0.9
ADAM_B2 = 0.999
ADAM_EPS = 1e-08
ADAM_WD = 0.01
ADAM_STEP = 10

MESH = pl.DeviceIdType.MESH
WEIGHT_NAMES = ['norm_mix_g', 'w_in', 'ssd_conv_w', 'ssd_conv_b', 'ssd_dt_bias', 'ssd_A_log', 'ssd_D', 'ssd_norm_g',
                'cf_conv_w', 'cf_conv_b', 'cf_ln_g', 'cf_ln_b', 'w_out', 'norm_xattn_g', 'norm_mem_g', 'w_q', 'w_kv',
                'w_o', 'norm_ffn_g', 'w_gate', 'w_up', 'w_down', 'norm_final_g']
BIG = ['w_in', 'w_out', 'w_q', 'w_kv', 'w_o', 'w_gate', 'w_up', 'w_down']
TRANSPOSED = ('w_in', 'w_gate', 'w_up')


def _params(sem=None):
    return pltpu.CompilerParams(dimension_semantics=sem, vmem_limit_bytes=_VMEM_LIMIT)


def _pick(n, cands):
    for c in cands:
        if n % c == 0:
            return c
    return n


def _mm(a, b, *, ta=False, tb=False, add=None, out_dtype=_F32, name):
    (kdim, m) = a.shape if ta else a.shape[::-1]
    (n, k2) = b.shape if tb else b.shape[::-1]
    assert kdim == k2, (a.shape, b.shape, ta, tb)
    if ta:
        tm = m if m <= 1024 else _pick(m, (1408, 1024, 512, 256, 128))
        tn = n if n <= 1536 else _pick(n, (1408, 1024, 512, 256, 128))
        size = lambda arr: jnp.dtype(arr.dtype).itemsize
        fits = lambda t: (2 * t * (tm * size(a) + tn * size(b)) + tm * tn * (4 + 2 * jnp.dtype(out_dtype).itemsize)
                          <= _VMEM_LIMIT * 3 // 4)
        tk = next((t for t in (2048, 1024, 512, 256, 128) if kdim % t == 0 and fits(t)), _pick(kdim, (128,)))
    else:
        tm = _pick(m, (512, 256, 128))
        tn = n if n <= 2816 else _pick(n, (1408, 1024, 512, 256, 128))
        tk = kdim if kdim <= 2816 else _pick(kdim, (1408, 1024, 512, 256, 128))
    nk = kdim // tk
    dn = (((0 if ta else 1,), (1 if tb else 0,)), ((), ()))

    def body(*refs):
        a_ref, b_ref = refs[0], refs[1]
        add_ref = refs[2] if add is not None else None
        o_ref = refs[3 if add is not None else 2]
        acc_ref = refs[-1]
        k = pl.program_id(2)
        prod = lax.dot_general(a_ref[...].astype(_MXU), b_ref[...].astype(_MXU), dn,
                               preferred_element_type=_F32, precision=_PREC)

        def finish(r):
            if add_ref is not None:
                r = r + add_ref[...].astype(_F32)
            o_ref[...] = r.astype(o_ref.dtype)

        if nk == 1:
            finish(prod)
            return

        @pl.when(k == 0)
        def _():
            acc_ref[...] = prod

        @pl.when(jnp.logical_and(k > 0, k < nk - 1))
        def _():
            acc_ref[...] += prod

        @pl.when(k == nk - 1)
        def _():
            finish(acc_ref[...] + prod)

    a_spec = pl.BlockSpec((tk, tm), lambda i, j, k: (k, i)) if ta else pl.BlockSpec((tm, tk), lambda i, j, k: (i, k))
    b_spec = pl.BlockSpec((tn, tk), lambda i, j, k: (j, k)) if tb else pl.BlockSpec((tk, tn), lambda i, j, k: (k, j))
    o_spec = pl.BlockSpec((tm, tn), lambda i, j, k: (i, j))
    ins, specs = [a, b], [a_spec, b_spec]
    if add is not None:
        ins.append(add)
        specs.append(o_spec)
    return pl.pallas_call(
        body, name=name, grid=(m // tm, n // tn, nk), in_specs=specs, out_specs=o_spec,
        out_shape=jax.ShapeDtypeStruct((m, n), out_dtype),
        scratch_shapes=[pltpu.VMEM((tm, tn), _F32)] if nk > 1 else [],
        compiler_params=_params(("parallel", "parallel", "arbitrary")),
    )(*ins)


def _resident(shape):
    return pl.BlockSpec(shape, lambda i: (0,) * len(shape), pipeline_mode=pl.Buffered(1))


def _mm_fan_out(a, bs, *, tb, out_dtypes, epilogue=None, extra_outs=(), tm=512, name):
    m, kdim = a.shape
    tm = min(tm, m)
    ns = [b.shape[0] if tb else b.shape[1] for b in bs]
    nb = len(bs)
    kind = "nt" if tb else "nn"

    def body(*refs):
        a_ref, b_refs, o_refs = refs[0], refs[1:1 + nb], refs[1 + nb:]
        av = a_ref[...].astype(_MXU)
        prods = [lax.dot_general(av, b[...].astype(_MXU), _DN[kind], preferred_element_type=_F32, precision=_PREC)
                 for b in b_refs]
        for o_ref, p in zip(o_refs[:nb], prods):
            o_ref[...] = p.astype(o_ref.dtype)
        if epilogue is not None:
            for o_ref, v in zip(o_refs[nb:], _tup(epilogue(*prods))):
                o_ref[...] = v.astype(o_ref.dtype)

    widths = ns + [w for w, _ in extra_outs]
    dtypes = list(out_dtypes) + [dt for _, dt in extra_outs]
    return pl.pallas_call(
        body, name=name, grid=(m // tm,),
        in_specs=[pl.BlockSpec((tm, kdim), lambda i: (i, 0))] + [_resident(b.shape) for b in bs],
        out_specs=[pl.BlockSpec((tm, w), lambda i: (i, 0)) for w in widths],
        out_shape=[jax.ShapeDtypeStruct((m, w), dt) for w, dt in zip(widths, dtypes)],
        compiler_params=_params(("parallel",)),
    )(a, *bs)


def _mm_fan_in(pairs, *, add=None, out_dtype=_F32, prologue=None, pro_ins=(), pro_out_dtypes=(), epilogue=None,
               tm=512, name):
    bs = [b for _, b in pairs]
    nb = len(bs)
    n = bs[0].shape[1]
    rows_in = list(pro_ins) if prologue is not None else [a for a, _ in pairs]
    m = rows_in[0].shape[0]
    tm = min(tm, m)
    n_r = len(rows_in)

    def body(*refs):
        r_refs, b_refs = refs[:n_r], refs[n_r:n_r + nb]
        pos = n_r + nb
        add_ref = refs[pos] if add is not None else None
        pos += add is not None
        epi_ref = refs[pos] if epilogue is not None else None
        pos += epilogue is not None
        o_ref, po_refs = refs[pos], refs[pos + 1:]
        if prologue is not None:
            a_vals = _tup(prologue(*[r[...].astype(_F32) for r in r_refs]))
            for po, v in zip(po_refs, a_vals):
                po[...] = v.astype(po.dtype)
        else:
            a_vals = [r[...] for r in r_refs]
        acc = None
        for av, b in zip(a_vals, b_refs):
            p = lax.dot_general(av.astype(_MXU), b[...].astype(_MXU), _DN["nn"], preferred_element_type=_F32,
                                precision=_PREC)
            acc = p if acc is None else acc + p
        if add_ref is not None:
            acc = acc + add_ref[...].astype(_F32)
        o_ref[...] = acc.astype(o_ref.dtype)
        if epilogue is not None:
            po_refs[-1][...] = epilogue[0](acc, epi_ref[...]).astype(po_refs[-1].dtype)

    row = lambda w: pl.BlockSpec((tm, w), lambda i: (i, 0))
    ins = rows_in + bs + ([add] if add is not None else []) + ([epilogue[1]] if epilogue is not None else [])
    in_specs = ([row(r.shape[1]) for r in rows_in] + [_resident(b.shape) for b in bs]
                + ([row(n)] if add is not None else []) + ([_resident(epilogue[1].shape)] if epilogue is not None else []))
    extra = [(b.shape[0], dt) for b, dt in zip(bs, pro_out_dtypes)] if prologue is not None else []
    if epilogue is not None:
        extra.append((n, epilogue[2]))
    res = pl.pallas_call(
        body, name=name, grid=(m // tm,), in_specs=in_specs,
        out_specs=[row(n)] + [row(w) for w, _ in extra],
        out_shape=[jax.ShapeDtypeStruct((m, n), out_dtype)] + [jax.ShapeDtypeStruct((m, w), dt) for w, dt in extra],
        compiler_params=_params(("parallel",)),
    )(*ins)
    return res if extra else res[0]


def _row_spec(r, ts):
    if isinstance(r, tuple):
        arr, width, cblk = r
        return arr, pl.BlockSpec((ts, width), lambda i, cblk=cblk: (i, cblk))
    return r, pl.BlockSpec((ts, r.shape[1]), lambda i: (i, 0))


def _tup(v):
    return tuple(v) if isinstance(v, (tuple, list)) else (v,)


def _row_fwd(f, rows, params, outs, *, name, ts=512):
    s = (rows[0][0] if isinstance(rows[0], tuple) else rows[0]).shape[0]
    ts = min(ts, s)
    arrs, specs = zip(*[_row_spec(r, ts) for r in rows])
    n_r, n_p = len(rows), len(params)

    def body(*refs):
        rv = [r[...].astype(_F32) for r in refs[:n_r]]
        pv = [p[...] for p in refs[n_r:n_r + n_p]]
        res = _tup(f(*rv, *pv))
        for o_ref, v in zip(refs[n_r + n_p:], res):
            o_ref[...] = v.astype(o_ref.dtype)

    res = pl.pallas_call(
        body, name=name, grid=(s // ts,),
        in_specs=list(specs) + [pl.BlockSpec(p.shape, lambda i: (0, 0)) for p in params],
        out_specs=[pl.BlockSpec((ts, w), lambda i: (i, 0)) for w, _ in outs],
        out_shape=[jax.ShapeDtypeStruct((s, w), dt) for w, dt in outs],
        compiler_params=_params(("parallel",)),
    )(*arrs, *params)
    return res[0] if len(outs) == 1 else res


def _row_bwd(f, rows, params, cts, *, need=None, adds=None, row_dtypes=None, name, ts=512):
    s = (rows[0][0] if isinstance(rows[0], tuple) else rows[0]).shape[0]
    ts = min(ts, s)
    arrs, specs = zip(*[_row_spec(r, ts) for r in rows])
    n_r, n_p, n_c = len(rows), len(params), len(cts)
    need = [True] * n_r if need is None else need
    adds = {} if adds is None else adds
    add_keys = sorted(adds)
    row_dtypes = [_F32] * n_r if row_dtypes is None else row_dtypes
    needed = [j for j in range(n_r) if need[j]]
    widths = [specs[j].block_shape[1] for j in range(n_r)]

    def body(*refs):
        pos = 0
        r_refs = refs[pos:pos + n_r]; pos += n_r
        p_refs = refs[pos:pos + n_p]; pos += n_p
        c_refs = refs[pos:pos + n_c]; pos += n_c
        a_refs = refs[pos:pos + len(add_keys)]; pos += len(add_keys)
        dr_refs = refs[pos:pos + len(needed)]; pos += len(needed)
        dp_refs = refs[pos:pos + n_p]
        rv = [r[...].astype(_F32) for r in r_refs]
        pv = [p[...] for p in p_refs]
        _, vjp = jax.vjp(lambda *a: _tup(f(*a)), *rv, *pv)
        g = vjp(tuple(c[...].astype(_F32) for c in c_refs))
        for o_ref, j in zip(dr_refs, needed):
            v = g[j]
            if j in adds:
                v = v + a_refs[add_keys.index(j)][...].astype(_F32)
            o_ref[...] = v.astype(o_ref.dtype)
        if n_p:
            @pl.when(pl.program_id(0) == 0)
            def _():
                for dp in dp_refs:
                    dp[...] = jnp.zeros_like(dp)
            for dp, v in zip(dp_refs, g[n_r:]):
                dp[...] += v

    ct_specs = [pl.BlockSpec((ts, c.shape[1]), lambda i: (i, 0)) for c in cts]
    add_specs = [pl.BlockSpec((ts, adds[j].shape[1]), lambda i: (i, 0)) for j in add_keys]
    res = pl.pallas_call(
        body, name=name, grid=(s // ts,),
        in_specs=list(specs) + [pl.BlockSpec(p.shape, lambda i: (0, 0)) for p in params] + ct_specs + add_specs,
        out_specs=[pl.BlockSpec((ts, widths[j]), lambda i: (i, 0)) for j in needed]
        + [pl.BlockSpec(p.shape, lambda i: (0, 0)) for p in params],
        out_shape=[jax.ShapeDtypeStruct((s, widths[j]), row_dtypes[j]) for j in needed]
        + [jax.ShapeDtypeStruct(p.shape, _F32) for p in params],
        compiler_params=_params(("arbitrary",)),
    )(*arrs, *params, *cts, *[adds[j] for j in add_keys])
    return list(res[:len(needed)]), list(res[len(needed):])


_DN = {"nn": (((1,), (0,)), ((), ())), "nt": (((1,), (1,)), ((), ())), "tn": (((0,), (0,)), ((), ()))}


def _dot_raw(a, b, kind):
    return lax.dot_general(a.astype(_MXU), b.astype(_MXU), _DN[kind], preferred_element_type=_F32, precision=_PREC)


@functools.partial(jax.custom_vjp, nondiff_argnums=(2,))
def _dot1(a, b, kind):
    return _dot_raw(a, b, kind)


def _dot1_bwd(kind, res, ct):
    a, b = res
    if kind == "nn":
        return _dot_raw(ct, b, "nt"), _dot_raw(a, ct, "tn")
    if kind == "nt":
        return _dot_raw(ct, b, "nn"), _dot_raw(ct, a, "tn")
    return _dot_raw(b, ct, "nt"), _dot_raw(a, ct, "nn")


_dot1.defvjp(lambda a, b, kind: (_dot_raw(a, b, kind), (a, b)), _dot1_bwd)


def _sig(v):
    return 1.0 / (1.0 + jnp.exp(-v))


def _silu(v):
    return v * _sig(v)


def _f_rms(x, g):
    return x * lax.rsqrt(jnp.mean(x * x, axis=-1, keepdims=True) + EPS) * g


def _f_gate(y, xs, z, dexp, g):
    v = (y + dexp * xs) * _silu(z)
    half = SSD_WIDTH // 2
    parts = []
    for grp in range(2):
        vg = v[:, grp * half:(grp + 1) * half]
        parts.append(vg * lax.rsqrt(jnp.mean(vg * vg, axis=-1, keepdims=True) + EPS) * g[:, grp * half:(grp + 1) * half])
    return jnp.concatenate(parts, axis=1)


def _f_ln(u, g, b):
    mu = jnp.mean(u, axis=-1, keepdims=True)
    var = jnp.mean(jnp.square(u - mu), axis=-1, keepdims=True)
    return _silu((u - mu) * lax.rsqrt(var + EPS) * g + b)


def _f_swiglu(gate, up):
    return _silu(gate) * up


def _f_att(q, k, v):
    outs = []
    for h in range(X_HEADS):
        sl = slice(h * X_HEAD_DIM, (h + 1) * X_HEAD_DIM)
        s = _dot1(q[:, sl], k[:, sl], "nt") * (X_HEAD_DIM ** -0.5)
        s = s - lax.stop_gradient(jnp.max(s, axis=-1, keepdims=True))
        p = jnp.exp(s)
        p = p / jnp.sum(p, axis=-1, keepdims=True)
        outs.append(_dot1(p, v[:, sl], "nn"))
    return jnp.concatenate(outs, axis=1)


def _loss_bwd(x3, target, g, *, dx_dtype=_F32, name, ts=512):
    s, d = x3.shape

    def f(x, t, gv):
        return 0.5 * jnp.sum(jnp.mean(jnp.square(_f_rms(x, gv) - t), axis=-1))

    def body(x_ref, t_ref, g_ref, dx_ref, dg_ref, l_ref):
        @pl.when(pl.program_id(0) == 0)
        def _():
            dg_ref[...] = jnp.zeros_like(dg_ref)
            l_ref[...] = jnp.zeros_like(l_ref)

        lv, (dx, dg) = jax.value_and_grad(f, argnums=(0, 2))(x_ref[...].astype(_F32), t_ref[...], g_ref[...])
        dx_ref[...] = dx.astype(dx_ref.dtype)
        dg_ref[...] += dg
        l_ref[...] += lv

    row = pl.BlockSpec((ts, d), lambda i: (i, 0))
    return pl.pallas_call(
        body, name=name, grid=(s // ts,),
        in_specs=[row, row, pl.BlockSpec((1, d), lambda i: (0, 0))],
        out_specs=[row, pl.BlockSpec((1, d), lambda i: (0, 0)), pl.BlockSpec((SUBLANES, LANES), lambda i: (0, 0))],
        out_shape=[jax.ShapeDtypeStruct((s, d), dx_dtype), jax.ShapeDtypeStruct((1, d), _F32),
                   jax.ShapeDtypeStruct((SUBLANES, LANES), _F32)],
        compiler_params=_params(("arbitrary",)),
    )(x3, target, g)


_CONV_PAD = 32
_CONV_ROWS = 128
_CONV_CB = 128


def _conv_taps(k_taps):
    groups = {}
    for k in range(k_taps):
        j = k_taps - 1 - k
        groups.setdefault(j % SUBLANES, []).append((k, j))
    return groups


def _conv_window(win, wv, groups, init):
    pad, rows = _CONV_PAD, _CONV_ROWS
    acc = init
    for rot, taps in groups.items():
        rolled = win if rot == 0 else pltpu.roll(win, rot, 0)
        for k, j in taps:
            off = pad - (j - rot)
            acc = acc + rolled[off:off + rows, :] * wv[k:k + 1, :]
    return acc


def _conv_fill(x_refs, xp_ref, s, glu):
    pad, cb = _CONV_PAD, _CONV_CB
    step = _pick(s, (512, 256, _CONV_ROWS))
    xp_ref[0:pad, :] = jnp.zeros((pad, cb), _F32)

    def fill(r, carry):
        base = pl.multiple_of(r * step, step)
        v = x_refs[0][pl.ds(base, step), :].astype(_F32)
        if glu:
            v = v * _sig(x_refs[1][pl.ds(base, step), :].astype(_F32))
        xp_ref[pl.ds(pad + base, step), :] = v
        return carry

    lax.fori_loop(0, s // step, fill, 0)


def _conv_fwd(xs, w, b, k_taps, *, glu=False, act=False, out_dtype=_F32, name):
    s, c = xs[0].shape
    kp = w.shape[0]
    pad, rows, cb = _CONV_PAD, _CONV_ROWS, _CONV_CB
    groups = _conv_taps(k_taps)
    n_in = len(xs)

    def body(*refs):
        x_refs = refs[:n_in]
        w_ref, b_ref, o_ref, xp_ref = refs[n_in:]
        _conv_fill(x_refs, xp_ref, s, glu)
        wv = w_ref[...]
        bias = jnp.broadcast_to(b_ref[...], (rows, cb))

        def chunk(r, carry):
            base = pl.multiple_of(r * rows, rows)
            acc = _conv_window(xp_ref[pl.ds(base, rows + pad), :], wv, groups, bias)
            o_ref[pl.ds(base, rows), :] = (_silu(acc) if act else acc).astype(o_ref.dtype)
            return carry

        lax.fori_loop(0, s // rows, chunk, 0)

    col = pl.BlockSpec((s, cb), lambda i: (0, i))
    return pl.pallas_call(
        body, name=name, grid=(c // cb,),
        in_specs=[col] * n_in + [pl.BlockSpec((kp, cb), lambda i: (0, i)), pl.BlockSpec((1, cb), lambda i: (0, i))],
        out_specs=col, out_shape=jax.ShapeDtypeStruct((s, c), out_dtype),
        scratch_shapes=[pltpu.VMEM((s + pad, cb), _F32)],
        compiler_params=_params(("parallel",)),
    )(*xs, w, b)


def _conv_bwd(xs, w, b, dy, k_taps, *, glu=False, act=False, name):
    s, c = xs[0].shape
    kp = w.shape[0]
    pad, rows, cb = _CONV_PAD, _CONV_ROWS, _CONV_CB
    groups = _conv_taps(k_taps)
    win_rows = rows + pad
    n_in = len(xs)

    def fold(v):
        acc = v[0:SUBLANES, :]
        for i in range(1, rows // SUBLANES):
            acc = acc + v[i * SUBLANES:(i + 1) * SUBLANES, :]
        return acc

    def body(*refs):
        x_refs = refs[:n_in]
        w_ref, b_ref, dy_ref = refs[n_in:n_in + 3]
        dx_refs = refs[n_in + 3:2 * n_in + 3]
        dw_ref, db_ref, xp_ref, dyp_ref, acc_ref, dbacc_ref = refs[2 * n_in + 3:]
        _conv_fill(x_refs, xp_ref, s, glu)
        dyp_ref[s:s + pad, :] = jnp.zeros((pad, cb), _F32)
        acc_ref[...] = jnp.zeros_like(acc_ref)
        dbacc_ref[...] = jnp.zeros_like(dbacc_ref)
        wv = w_ref[...]
        bias = jnp.broadcast_to(b_ref[...], (rows, cb))

        def through_act(r, carry):
            base = pl.multiple_of(r * rows, rows)
            d = dy_ref[pl.ds(base, rows), :].astype(_F32)
            if act:
                pre = _conv_window(xp_ref[pl.ds(base, win_rows), :], wv, groups, bias)
                sg = _sig(pre)
                d = d * (sg * (1.0 + pre * (1.0 - sg)))
            dyp_ref[pl.ds(base, rows), :] = d
            return carry

        lax.fori_loop(0, s // rows, through_act, 0)

        def chunk(r, carry):
            base = pl.multiple_of(r * rows, rows)
            xwin = xp_ref[pl.ds(base, win_rows), :]
            dwin = dyp_ref[pl.ds(base, win_rows), :]
            dyc = dwin[0:rows, :]
            dxacc = jnp.zeros((rows, cb), _F32)
            for rot, taps in groups.items():
                xr = xwin if rot == 0 else pltpu.roll(xwin, rot, 0)
                dr = dwin if rot == 0 else pltpu.roll(dwin, win_rows - rot, 0)
                for k, j in taps:
                    a8 = j - rot
                    dxacc = dxacc + dr[a8:a8 + rows, :] * wv[k:k + 1, :]
                    prod = dyc * xr[pad - a8:pad - a8 + rows, :]
                    acc_ref[k * SUBLANES:(k + 1) * SUBLANES, :] += fold(prod)
            dbacc_ref[...] += fold(dyc)
            if glu:
                av = x_refs[0][pl.ds(base, rows), :].astype(_F32)
                sg = _sig(x_refs[1][pl.ds(base, rows), :].astype(_F32))
                dx_refs[0][pl.ds(base, rows), :] = (dxacc * sg).astype(dx_refs[0].dtype)
                dx_refs[1][pl.ds(base, rows), :] = (dxacc * av * sg * (1.0 - sg)).astype(dx_refs[1].dtype)
            else:
                dx_refs[0][pl.ds(base, rows), :] = dxacc.astype(dx_refs[0].dtype)
            return carry

        lax.fori_loop(0, s // rows, chunk, 0)
        dw_ref[...] = jnp.zeros_like(dw_ref)
        for k in range(k_taps):
            dw_ref[k:k + 1, :] = jnp.sum(acc_ref[k * SUBLANES:(k + 1) * SUBLANES, :], axis=0, keepdims=True)
        db_ref[...] = jnp.sum(dbacc_ref[...], axis=0, keepdims=True)

    col = pl.BlockSpec((s, cb), lambda i: (0, i))
    wspec = pl.BlockSpec((kp, cb), lambda i: (0, i))
    bspec = pl.BlockSpec((1, cb), lambda i: (0, i))
    dx_dtype = xs[0].dtype
    res = pl.pallas_call(
        body, name=name, grid=(c // cb,),
        in_specs=[col] * n_in + [wspec, bspec, col], out_specs=[col] * n_in + [wspec, bspec],
        out_shape=[jax.ShapeDtypeStruct((s, c), dx_dtype)] * n_in
        + [jax.ShapeDtypeStruct((kp, c), _F32), jax.ShapeDtypeStruct((1, c), _F32)],
        scratch_shapes=[pltpu.VMEM((s + pad, cb), _F32), pltpu.VMEM((s + pad, cb), _F32),
                        pltpu.VMEM((kp * SUBLANES, cb), _F32), pltpu.VMEM((SUBLANES, cb), _F32)],
        compiler_params=_params(("parallel",)),
    )(*xs, w, b, dy)
    return list(res[:n_in]), res[n_in], res[n_in + 1]


def _tri_sum(v, lower):
    l = v.shape[0]
    r, c = lax.broadcasted_iota(jnp.int32, (l, l), 0), lax.broadcasted_iota(jnp.int32, (l, l), 1)
    tri = ((r >= c) if lower else (r <= c)).astype(jnp.bfloat16)
    hi = v.astype(jnp.bfloat16)
    r1 = v - hi.astype(_F32)
    mid = r1.astype(jnp.bfloat16)
    lo = (r1 - mid.astype(_F32)).astype(jnp.bfloat16)
    out = jnp.zeros_like(v)
    for part in (hi, mid, lo):
        out = out + lax.dot_general(tri, part, _DN["nn"], preferred_element_type=_F32)
    return out


@jax.custom_vjp
def _cumsum_rows(v):
    return _tri_sum(v, True)


_cumsum_rows.defvjp(lambda v: (_tri_sum(v, True), None), lambda _, ct: (_tri_sum(ct, False),))


def _ssd_chunk(xbc, dtraw, prev, bias, alog):
    l = xbc.shape[0]
    xs = xbc[:, :SSD_WIDTH]
    bm = xbc[:, SSD_WIDTH:SSD_WIDTH + 2 * SSD_STATE]
    cm = xbc[:, SSD_WIDTH + 2 * SSD_STATE:]
    v = dtraw + bias
    dt = jnp.maximum(v, 0.0) + jnp.log1p(jnp.exp(-jnp.abs(v)))
    a_neg = -jnp.exp(alog)
    acs = _cumsum_rows(dt * a_neg)
    acs_t = acs.T
    total = acs[l - 1:l, :]
    row = lax.broadcasted_iota(jnp.int32, (l, l), 0)
    colv = lax.broadcasted_iota(jnp.int32, (l, l), 1)
    causal = row >= colv
    lane_lo = lax.broadcasted_iota(jnp.int32, (l, LANES), 1) < HEAD_DIM
    row_lo = lax.broadcasted_iota(jnp.int32, (LANES, SSD_STATE), 0) < HEAD_DIM

    def pair_lanes(m, h0):
        return jnp.where(lane_lo, m[:, h0:h0 + 1], m[:, h0 + 1:h0 + 2])

    ys, news = [], []
    cb = {}
    for j in range(SSD_HEADS // 2):
        h0 = 2 * j
        grp = h0 // (SSD_HEADS // 2)
        bg = bm[:, grp * SSD_STATE:(grp + 1) * SSD_STATE]
        cg = cm[:, grp * SSD_STATE:(grp + 1) * SSD_STATE]
        if grp not in cb:
            cb[grp] = _dot1(cg, bg, "nt")
        xdt = xs[:, j * LANES:(j + 1) * LANES] * pair_lanes(dt, h0)
        y = jnp.zeros((l, LANES), _F32)
        for hh, mask in ((h0, lane_lo), (h0 + 1, jnp.logical_not(lane_lo))):
            seg = acs[:, hh:hh + 1] - acs_t[hh:hh + 1, :]
            dec = jnp.exp(jnp.where(causal, seg, -jnp.inf))
            y = y + _dot1(cb[grp] * dec, jnp.where(mask, xdt, 0.0), "nn")
        acs_p = pair_lanes(acs, h0)
        prev_p = prev[j * LANES:(j + 1) * LANES, :]
        y = y + _dot1(cg, prev_p, "nt") * jnp.exp(acs_p)
        total_p = jnp.where(lane_lo[0:1, :], total[:, h0:h0 + 1], total[:, h0 + 1:h0 + 2])
        wgt = jnp.exp(total_p - acs_p)
        st = _dot1(xdt * wgt, bg, "tn")
        cdec = jnp.exp(jnp.where(row_lo, total[:, h0:h0 + 1], total[:, h0 + 1:h0 + 2]))
        news.append(prev_p * cdec + st)
        ys.append(y)
    return jnp.concatenate(ys, axis=1), jnp.concatenate(news, axis=0)


def _ssd_gate_chunk(xbc, dtraw, prev, bias, alog, z, dexp, g):
    y, new = _ssd_chunk(xbc, dtraw, prev, bias, alog)
    return _f_gate(y, xbc[:, :SSD_WIDTH], z, dexp, g), new


def _ssd_fwd(xbc, dtraw, bias, alog, z, dexp, g, *, name):
    s = xbc.shape[0]
    nc = s // CHUNK
    nstate = SSD_HEADS * HEAD_DIM

    def body(x_ref, dt_ref, b_ref, a_ref, z_ref, d_ref, g_ref, y_ref, st_ref, state_ref):
        @pl.when(pl.program_id(0) == 0)
        def _():
            state_ref[...] = jnp.zeros_like(state_ref)

        prev = state_ref[...]
        st_ref[...] = prev
        y, new = _ssd_gate_chunk(x_ref[...].astype(_F32), dt_ref[...], prev, b_ref[...], a_ref[...], z_ref[...].astype(_F32),
                                 d_ref[...], g_ref[...])
        y_ref[...] = y.astype(y_ref.dtype)
        state_ref[...] = new

    small = pl.BlockSpec((1, LANES), lambda i: (0, 0))
    wide = pl.BlockSpec((1, SSD_WIDTH), lambda i: (0, 0))
    rows = pl.BlockSpec((CHUNK, SSD_WIDTH), lambda i: (i, 0))
    return pl.pallas_call(
        body, name=name, grid=(nc,),
        in_specs=[pl.BlockSpec((CHUNK, XBC_WIDTH), lambda i: (i, 0)), pl.BlockSpec((CHUNK, LANES), lambda i: (i, 0)),
                  small, small, rows, wide, wide],
        out_specs=[rows, pl.BlockSpec((None, nstate, SSD_STATE), lambda i: (i, 0, 0))],
        out_shape=[jax.ShapeDtypeStruct((s, SSD_WIDTH), _MXU), jax.ShapeDtypeStruct((nc, nstate, SSD_STATE), _F32)],
        scratch_shapes=[pltpu.VMEM((nstate, SSD_STATE), _F32)],
        compiler_params=_params(("arbitrary",)),
    )(xbc, dtraw, bias, alog, z, dexp, g)


def _ssd_bwd(xbc, dtraw, states, bias, alog, z, dexp, g, dy, *, name):
    s = xbc.shape[0]
    nc = s // CHUNK
    nstate = SSD_HEADS * HEAD_DIM

    def body(x_ref, dt_ref, st_ref, b_ref, a_ref, z_ref, d_ref, g_ref, dy_ref,
             dx_ref, ddt_ref, db_ref, da_ref, dz_ref, dd_ref, dg_ref, dstate_ref):
        @pl.when(pl.program_id(0) == 0)
        def _():
            dstate_ref[...] = jnp.zeros_like(dstate_ref)
            for acc in (db_ref, da_ref, dd_ref, dg_ref):
                acc[...] = jnp.zeros_like(acc)

        _, vjp = jax.vjp(_ssd_gate_chunk, x_ref[...].astype(_F32), dt_ref[...], st_ref[...], b_ref[...], a_ref[...],
                         z_ref[...].astype(_F32), d_ref[...], g_ref[...])
        dx, ddt, dprev, db, da, dz, dd, dg = vjp((dy_ref[...].astype(_F32), dstate_ref[...]))
        dx_ref[...] = dx.astype(dx_ref.dtype)
        ddt_ref[...] = ddt
        dz_ref[...] = dz.astype(dz_ref.dtype)
        db_ref[...] += db
        da_ref[...] += da
        dd_ref[...] += dd
        dg_ref[...] += dg
        dstate_ref[...] = dprev

    rev = lambda i: (nc - 1 - i, 0)
    small = pl.BlockSpec((1, LANES), lambda i: (0, 0))
    wide = pl.BlockSpec((1, SSD_WIDTH), lambda i: (0, 0))
    rows = pl.BlockSpec((CHUNK, SSD_WIDTH), rev)
    return pl.pallas_call(
        body, name=name, grid=(nc,),
        in_specs=[pl.BlockSpec((CHUNK, XBC_WIDTH), rev), pl.BlockSpec((CHUNK, LANES), rev),
                  pl.BlockSpec((None, nstate, SSD_STATE), lambda i: (nc - 1 - i, 0, 0)), small, small, rows, wide, wide,
                  rows],
        out_specs=[pl.BlockSpec((CHUNK, XBC_WIDTH), rev), pl.BlockSpec((CHUNK, LANES), rev), small, small, rows, wide, wide],
        out_shape=[jax.ShapeDtypeStruct((s, XBC_WIDTH), xbc.dtype), jax.ShapeDtypeStruct((s, LANES), _F32),
                   jax.ShapeDtypeStruct((1, LANES), _F32), jax.ShapeDtypeStruct((1, LANES), _F32),
                   jax.ShapeDtypeStruct((s, SSD_WIDTH), z.dtype), jax.ShapeDtypeStruct((1, SSD_WIDTH), _F32),
                   jax.ShapeDtypeStruct((1, SSD_WIDTH), _F32)],
        scratch_shapes=[pltpu.VMEM((nstate, SSD_STATE), _F32)],
        compiler_params=_params(("arbitrary",)),
    )(xbc, dtraw, states, bias, alog, z, dexp, g, dy)


def _pad_cols(a, width):
    return jnp.pad(a, ((0, 0), (0, width - a.shape[1])))


def _pad_rows(a, rows):
    return jnp.pad(a, ((0, rows - a.shape[0]), (0, 0)))


def _tie(a, token):
    return a + token[0:1, 0:1].astype(a.dtype)


def _local_step(x, mem, target, w, fetch, emit):
    bf = _MXU
    d = D_MODEL
    h = _row_fwd(_f_rms, [x], [w['norm_mix_g']], [(d, bf)], name="f_norm_mix")
    w_in = fetch('in', h)['w_in']
    z_end, xbc_end, dt_end = SSD_WIDTH, SSD_WIDTH + XBC_WIDTH, SSD_WIDTH + XBC_WIDTH + SSD_HEADS
    w_z, w_xbc = w_in[:z_end], w_in[z_end:xbc_end]
    w_dt = _pad_rows(w_in[xbc_end:dt_end], LANES)
    w_a, w_g = w_in[dt_end:dt_end + CF_WIDTH], w_in[dt_end + CF_WIDTH:]
    dt_bias = _pad_cols(w['ssd_dt_bias'], LANES)
    a_log = _pad_cols(w['ssd_A_log'], LANES)
    d_exp = jnp.repeat(w['ssd_D'], HEAD_DIM, axis=1)
    g_final = w['norm_final_g'].reshape(1, D_MODEL)

    z, xbc, dtr, ga, gg = _mm_fan_out(h, [w_z, w_xbc, w_dt, w_a, w_g], tb=True, out_dtypes=[bf, bf, _F32, bf, bf],
                                      name="f_in")
    wc = fetch('conv', xbc)
    ssd_w = _pad_rows(wc['ssd_conv_w'], SUBLANES)
    cf_w = _pad_rows(wc['cf_conv_w'], 32)
    xbc_a = _conv_fwd([xbc], ssd_w, w['ssd_conv_b'], SSD_CONV, act=True, out_dtype=bf, name="f_ssd_conv")
    y_n, states = _ssd_fwd(xbc_a, dtr, dt_bias, a_log, z, d_exp, w['ssd_norm_g'], name="f_ssd")
    u_c = _conv_fwd([ga, gg], cf_w, w['cf_conv_b'], CF_CONV, glu=True, out_dtype=bf, name="f_cf_conv")
    u = _row_fwd(_f_ln, [u_c], [w['cf_ln_g'], w['cf_ln_b']], [(d, bf)], name="f_cf_ln")
    wm = fetch('mid', y_n)
    w_out_y, w_out_u = wm['w_out'][:SSD_WIDTH], wm['w_out'][SSD_WIDTH:]
    x1, hq = _mm_fan_in([(y_n, w_out_y), (u, w_out_u)], add=x, out_dtype=bf, epilogue=(_f_rms, w['norm_xattn_g'], bf),
                        name="f_out")
    q = _mm(hq, wm['w_q'], out_dtype=bf, name="f_q")
    memn = _row_fwd(_f_rms, [mem], [w['norm_mem_g']], [(d, bf)], name="f_norm_mem")
    kv = _mm(memn, wm['w_kv'], name="f_kv")
    k_mat, v_mat = kv[:, :d], kv[:, d:]
    o = _row_fwd(_f_att, [q], [k_mat, v_mat], [(d, bf)], ts=512, name="f_att")
    x2, hf = _mm_fan_in([(o, wm['w_o'])], add=x1, out_dtype=bf, epilogue=(_f_rms, w['norm_ffn_g'], bf), name="f_o")
    wf = fetch('ffn', hf)
    gate, up, act = _mm_fan_out(hf, [wf['w_gate'], wf['w_up']], tb=True, out_dtypes=[bf, bf], epilogue=_f_swiglu,
                                extra_outs=[(D_FF, bf)], tm=256, name="f_ffn_in")
    x3 = _mm(act, wf['w_down'], add=x2, out_dtype=bf, name="f_down")

    dx3, dg_final, loss = _loss_bwd(x3, target, g_final, dx_dtype=bf, name="b_loss")
    g = {'norm_final_g': dg_final.reshape(d)}

    dact = _mm(dx3, wf['w_down'], tb=True, out_dtype=bf, name="b_down_x")
    dw_down = _mm(act, dx3, ta=True, out_dtype=bf, name="b_down_w")
    def swiglu_bwd(gate_t, up_t, dact_t):
        return jax.vjp(_f_swiglu, gate_t, up_t)[1](dact_t)

    dhf, dgate, dup = _mm_fan_in([(None, wf['w_gate']), (None, wf['w_up'])], prologue=swiglu_bwd, pro_ins=[gate, up, dact],
                                 pro_out_dtypes=[bf, bf], out_dtype=bf, tm=256, name="b_ffn_in_x")
    sent = emit({'w_down': dw_down, 'w_gate': _mm(dgate, hf, ta=True, out_dtype=bf, name="b_gate_w"),
                 'w_up': _mm(dup, hf, ta=True, out_dtype=bf, name="b_up_w")})
    (dx2,), (g['norm_ffn_g'],) = _row_bwd(_f_rms, [x2], [_tie(w['norm_ffn_g'], sent)], [dhf], adds={0: dx3}, row_dtypes=[bf], name="b_norm_ffn")

    do = _mm(dx2, wm['w_o'], tb=True, out_dtype=bf, name="b_o_x")
    dw_o = _mm(o, dx2, ta=True, out_dtype=bf, name="b_o_w")
    (dq,), (dk, dv) = _row_bwd(_f_att, [q], [k_mat, v_mat], [do], row_dtypes=[bf], ts=512, name="b_att")
    dw_q = _mm(hq, dq, ta=True, out_dtype=bf, name="b_q_w")
    dhq = _mm(dq, wm['w_q'], tb=True, out_dtype=bf, name="b_q_x")
    (dx1,), (g['norm_xattn_g'],) = _row_bwd(_f_rms, [x1], [w['norm_xattn_g']], [dhq], adds={0: dx2}, row_dtypes=[bf], name="b_norm_xattn")
    dkv = jnp.concatenate([dk, dv], axis=1)
    dmemn = _mm(dkv, wm['w_kv'], tb=True, name="b_kv_x")
    _, (g['norm_mem_g'],) = _row_bwd(_f_rms, [mem], [w['norm_mem_g']], [dmemn], need=[False], name="b_norm_mem")
    sent = emit({'w_o': dw_o, 'w_q': dw_q, 'w_kv': _mm(memn, dkv, ta=True, out_dtype=bf, name="b_kv_w")},
                after=g['norm_mem_g'])

    dyn, du = _mm_fan_out(dx1, [w_out_y, w_out_u], tb=True, out_dtypes=[bf, bf], name="b_out_x")
    (du_c,), (g['cf_ln_g'], g['cf_ln_b']) = _row_bwd(_f_ln, [u_c], [_tie(w['cf_ln_g'], sent), w['cf_ln_b']], [du], row_dtypes=[bf], name="b_cf_ln")
    sent = emit({'w_out': jnp.concatenate([_mm(y_n, dx1, ta=True, out_dtype=bf, name="b_out_y_w"), _mm(u, dx1, ta=True, out_dtype=bf, name="b_out_u_w")], axis=0)})
    (dga, dgg), dcf_w, g['cf_conv_b'] = _conv_bwd([ga, gg], cf_w, w['cf_conv_b'], du_c, CF_CONV, glu=True, name="b_cf_conv")
    g['cf_conv_w'] = dcf_w[:CF_CONV]
    dxbc_a, ddtr, ddt_bias, da_log, dz, dd_exp, g['ssd_norm_g'] = _ssd_bwd(
        xbc_a, dtr, states, dt_bias, a_log, z, d_exp, _tie(w['ssd_norm_g'], sent), dyn, name="b_ssd")
    g['ssd_D'] = jnp.sum(dd_exp.reshape(SSD_HEADS, HEAD_DIM), axis=1).reshape(1, SSD_HEADS)
    g['ssd_dt_bias'] = ddt_bias[:, :SSD_HEADS]
    g['ssd_A_log'] = da_log[:, :SSD_HEADS]
    (dxbc,), dssd_w, g['ssd_conv_b'] = _conv_bwd([xbc], ssd_w, w['ssd_conv_b'], dxbc_a, SSD_CONV, act=True, name="b_ssd_conv")
    g['ssd_conv_w'] = dssd_w[:SSD_CONV]

    sent = emit({'w_in': jnp.concatenate([
        _mm(dz, h, ta=True, out_dtype=bf, name="b_in_z_w"), _mm(dxbc, h, ta=True, out_dtype=bf, name="b_in_xbc_w"),
        _mm(ddtr, h, ta=True, out_dtype=bf, name="b_in_dt_w")[:SSD_HEADS],
        _mm(dga, h, ta=True, out_dtype=bf, name="b_in_a_w"), _mm(dgg, h, ta=True, out_dtype=bf, name="b_in_g_w")], axis=0)})
    dh = _mm_fan_in([(dz, w_z), (dxbc, w_xbc), (ddtr, _tie(w_dt, sent)), (dga, w_a), (dgg, w_g)], out_dtype=bf,
                    name="b_in_x")
    (dx,), (g['norm_mix_g'],) = _row_bwd(_f_rms, [x], [w['norm_mix_g']], [dh], adds={0: dx1}, name="b_norm_mix")
    return loss, dx, g


_ANY = pl.BlockSpec(memory_space=pl.ANY)


def _place():
    x, y, c = lax.axis_index("x"), lax.axis_index("y"), lax.axis_index("c")
    return x, y, c


_HBM =pl.BlockSpec(memory_space=pltpu.HBM)
_SEM = pl.BlockSpec(memory_space=pltpu.SEMAPHORE)
_EFFECT = pltpu.SideEffectType.DATAFLOW_SIDE_EFFECTING
_FLIPS = [(dx, dy, dc) for dx in (0, 1) for dy in (0, 1) for dc in (0, 1)][1:]


def _peer(flip, x, y, c):
    return (1 - x if flip[0] else x, 1 - y if flip[1] else y, 1 - c if flip[2] else c)


_CHIP_FLIPS = [f for f in _FLIPS if f[2] == 0]


def _send_start(srcs, blocked, *, flips=None, after=None, name):
    n = len(srcs)
    flips = _FLIPS if flips is None else flips
    nf = len(flips)
    lands = [jax.ShapeDtypeStruct(s.shape if blocked else (N_DEV,) + s.shape, s.dtype) for s in srcs]
    n_in = 2 * n + (after is not None)

    def body(*refs):
        src_refs, land_refs = refs[:n], refs[n:2 * n]
        send_sems, recv_sems = refs[n_in], refs[n_in + 1]
        token = refs[-1]
        x, y, c = _place()
        me = 4 * x + 2 * y + c
        for a in range(n):
            for k, flip in enumerate(flips):
                p = _peer(flip, x, y, c)
                src = src_refs[a].at[4 * p[0] + 2 * p[1] + p[2]] if blocked else src_refs[a]
                pltpu.make_async_remote_copy(
                    src_ref=src, dst_ref=land_refs[a].at[me], send_sem=send_sems.at[nf * a + k],
                    recv_sem=recv_sems.at[nf * a + k], device_id=p, device_id_type=MESH).start()
        token[...] = jnp.zeros_like(token)

    res = pl.pallas_call(
        body, name=name,
        out_shape=(pltpu.SemaphoreType.DMA((nf * n,)), pltpu.SemaphoreType.DMA((nf * n,)),
                   *[pltpu.HBM(s.shape, s.dtype) for s in srcs], *[pltpu.HBM(l.shape, l.dtype) for l in lands],
                   jax.ShapeDtypeStruct((SUBLANES, LANES), _F32)),
        in_specs=[_HBM] * (2 * n) + [_ANY] * (after is not None),
        out_specs=(_SEM, _SEM, *[_HBM] * (2 * n), pl.BlockSpec(memory_space=pltpu.VMEM)),
        input_output_aliases={i: 2 + i for i in range(2 * n)},
        compiler_params=pltpu.CompilerParams(has_side_effects=_EFFECT),
    )(*[pltpu.with_memory_space_constraint(s, pltpu.HBM) for s in srcs],
      *[pltpu.with_memory_space_constraint(lax.empty(l.shape, l.dtype), pltpu.HBM) for l in lands],
      *([after] if after is not None else []))
    return res[0], res[1], list(res[2:2 + n]), list(res[2 + n:2 + 2 * n]), res[-1]


def _send_wait(handles, after, blocked, *, flips=None, name):
    send_sems, recv_sems, srcs, lands, _ = handles
    n = len(srcs)
    flips = _FLIPS if flips is None else flips
    nf = len(flips)

    def body(*refs):
        src_refs, land_refs = refs[:n], refs[n:2 * n]
        send_sems, recv_sems = refs[2 * n], refs[2 * n + 1]
        x, y, c = _place()
        for a in range(n):
            for k, flip in enumerate(flips):
                p = _peer(flip, x, y, c)
                pid = 4 * p[0] + 2 * p[1] + p[2]
                cp = pltpu.make_async_remote_copy(
                    src_ref=src_refs[a].at[pid] if blocked else src_refs[a], dst_ref=land_refs[a].at[pid],
                    send_sem=send_sems.at[nf * a + k], recv_sem=recv_sems.at[nf * a + k], device_id=p, device_id_type=MESH)
                cp.wait_send()
                cp.wait_recv()

    res = pl.pallas_call(
        body, name=name,
        out_shape=tuple(pltpu.HBM(s.shape, s.dtype) for s in srcs + lands),
        in_specs=[_HBM] * (2 * n) + [_SEM, _SEM, _ANY], out_specs=tuple([_HBM] * (2 * n)),
        input_output_aliases={i: i for i in range(2 * n)},
        compiler_params=pltpu.CompilerParams(has_side_effects=_EFFECT),
    )(*srcs, *lands, send_sems, recv_sems, after)
    return list(res[:n]), list(res[n:])


def _sibling_share(lands, owns, *, name):
    n = len(lands)

    def body(*refs):
        own_refs, land_refs = refs[n:2 * n], refs[2 * n:3 * n]
        send_sems, recv_sems = refs[3 * n], refs[3 * n + 1]
        x, y, c = _place()
        sibling = (x, y, 1 - c)
        chips = [(1 - x, y), (x, 1 - y), (1 - x, 1 - y)]

        def copy(a, k, block, src=None):
            slot = land_refs[a].at[block]
            return pltpu.make_async_remote_copy(
                src_ref=slot if src is None else src, dst_ref=slot, send_sem=send_sems.at[4 * a + k],
                recv_sem=recv_sems.at[4 * a + k], device_id=sibling, device_id_type=MESH)

        sends = []
        for a in range(n):
            sends.append(copy(a, 0, 4 * x + 2 * y + c, src=own_refs[a]))
            sends += [copy(a, 1 + j, 4 * cx + 2 * cy + c) for j, (cx, cy) in enumerate(chips)]
        for cp in sends:
            cp.start()
        for a in range(n):
            copy(a, 0, 4 * x + 2 * y + (1 - c)).wait_recv()
            for j, (cx, cy) in enumerate(chips):
                copy(a, 1 + j, 4 * cx + 2 * cy + (1 - c)).wait_recv()
        for cp in sends:
            cp.wait_send()

    return pl.pallas_call(
        body, name=name, in_specs=[_ANY] * (2 * n), out_specs=[_ANY] * n,
        out_shape=[jax.ShapeDtypeStruct(l.shape, l.dtype) for l in lands],
        input_output_aliases={i: i for i in range(n)},
        scratch_shapes=[pltpu.SemaphoreType.DMA((4 * n,)), pltpu.SemaphoreType.DMA((4 * n,))],
    )(*lands, *owns)


def _adamw(parts, w, m, v, *, own=None, me=None, name):
    p, r, c = parts.shape
    tr = _pick(r, (256, 176, 128, 64, 32, 16, 8))
    if own is not None:
        tc = c if tr < r else _pick(c, (256, 128))
        return _adamw_own(parts, own, me, w, m, v, tr, tc, name=name)

    def body(p_ref, w_ref, m_ref, v_ref, g_ref, d_ref, nm_ref, nv_ref):
        g = p_ref[0].astype(_F32)
        for i in range(1, p):
            g = g + p_ref[i].astype(_F32)
        _adamw_math(g, w_ref, m_ref, v_ref, g_ref, d_ref, nm_ref, nv_ref)

    blk = pl.BlockSpec((tr, c), lambda i: (i, 0))
    return pl.pallas_call(
        body, name=name, grid=(r // tr,),
        in_specs=[pl.BlockSpec((p, tr, c), lambda i: (0, i, 0)), blk, blk, blk], out_specs=[blk] * 4,
        out_shape=[jax.ShapeDtypeStruct((r, c), _F32)] * 4,
        compiler_params=_params(("parallel",)),
    )(parts, w, m, v)


def _adamw_math(g, w_ref, m_ref, v_ref, g_ref, d_ref, nm_ref, nv_ref):
    wv = w_ref[...]
    mn = ADAM_B1 * m_ref[...] + (1.0 - ADAM_B1) * g
    vn = ADAM_B2 * v_ref[...] + (1.0 - ADAM_B2) * jnp.square(g)
    m_hat = mn / (1.0 - ADAM_B1 ** ADAM_STEP)
    v_hat = vn / (1.0 - ADAM_B2 ** ADAM_STEP)
    g_ref[...] = g
    d_ref[...] = -ADAM_LR * (m_hat / (jnp.sqrt(v_hat) + ADAM_EPS) + ADAM_WD * wv)
    nm_ref[...] = mn
    nv_ref[...] = vn


def _adamw_own(parts, own, me, w, m, v, tr, tc, *, name):
    p, r, c = parts.shape

    def body(me_ref, p_ref, own_ref, w_ref, m_ref, v_ref, g_ref, d_ref, nm_ref, nv_ref):
        mine = own_ref[...].astype(_F32)
        g = jnp.where(me_ref[0] == 0, mine, p_ref[0].astype(_F32))
        for i in range(1, p):
            g = g + jnp.where(me_ref[0] == i, mine, p_ref[i].astype(_F32))
        _adamw_math(g, w_ref, m_ref, v_ref, g_ref, d_ref, nm_ref, nv_ref)

    blk = pl.BlockSpec((tr, tc), lambda i, j, me_ref: (i, j))
    grid_spec = pltpu.PrefetchScalarGridSpec(
        num_scalar_prefetch=1, grid=(r // tr, c // tc),
        in_specs=[pl.BlockSpec((p, tr, tc), lambda i, j, me_ref: (0, i, j)),
                  pl.BlockSpec((None, tr, tc), lambda i, j, me_ref: (me_ref[0], i, j)), blk, blk, blk],
        out_specs=[blk] * 4)
    return pl.pallas_call(
        body, name=name, grid_spec=grid_spec, out_shape=[jax.ShapeDtypeStruct((r, c), _F32)] * 4,
        compiler_params=_params(("parallel", "parallel")),
    )(me.reshape(1).astype(jnp.int32), parts, own, w, m, v)


def _adamw_rows(g_row, offsets, ws, ms, vs, *, name):
    k = len(ws)

    def body(*refs):
        g_ref, w_refs, m_refs, v_refs = refs[0], refs[1:1 + k], refs[1 + k:1 + 2 * k], refs[1 + 2 * k:1 + 3 * k]
        outs = refs[1 + 3 * k:]
        for i in range(k):
            gi = g_ref[:, offsets[i]:offsets[i] + ws[i].shape[1]]
            _adamw_math(gi, w_refs[i], m_refs[i], v_refs[i], *outs[4 * i:4 * i + 4])

    return pl.pallas_call(
        body, name=name, out_shape=[jax.ShapeDtypeStruct(w.shape, _F32) for w in ws for _ in range(4)],
    )(g_row, *ws, *ms, *vs)


def _sum_parts(parts, *, name):
    p, r, c = parts.shape

    def body(p_ref, o_ref):
        g = p_ref[0].astype(_F32)
        for i in range(1, p):
            g = g + p_ref[i].astype(_F32)
        o_ref[...] = g

    return pl.pallas_call(body, name=name, out_shape=jax.ShapeDtypeStruct((r, c), _F32))(parts)


def kernel(x, mem, norm_mix_g, w_in, ssd_conv_w, ssd_conv_b, ssd_dt_bias, ssd_A_log, ssd_D, ssd_norm_g, cf_conv_w, cf_conv_b, cf_ln_g, cf_ln_b, w_out, norm_xattn_g, norm_mem_g, w_q, w_kv, w_o, norm_ffn_g, w_gate, w_up, w_down, norm_final_g, loss_target, m_norm_mix_g, m_w_in, m_ssd_conv_w, m_ssd_conv_b, m_ssd_dt_bias, m_ssd_A_log, m_ssd_D, m_ssd_norm_g, m_cf_conv_w, m_cf_conv_b, m_cf_ln_g, m_cf_ln_b, m_w_out, m_norm_xattn_g, m_norm_mem_g, m_w_q, m_w_kv, m_w_o, m_norm_ffn_g, m_w_gate, m_w_up, m_w_down, m_norm_final_g, v_norm_mix_g, v_w_in, v_ssd_conv_w, v_ssd_conv_b, v_ssd_dt_bias, v_ssd_A_log, v_ssd_D, v_ssd_norm_g, v_cf_conv_w, v_cf_conv_b, v_cf_ln_g, v_cf_ln_b, v_w_out, v_norm_xattn_g, v_norm_mem_g, v_w_q, v_w_kv, v_w_o, v_norm_ffn_g, v_w_gate, v_w_up, v_w_down, v_norm_final_g):
    args = dict(locals())
    wts = {n: args[n] for n in WEIGHT_NAMES}
    mom = {n: args["m_" + n] for n in WEIGHT_NAMES}
    var = {n: args["v_" + n] for n in WEIGHT_NAMES}
    me = 4 * lax.axis_index("x") + 2 * lax.axis_index("y") + lax.axis_index("c")

    groups = {'in': ['w_in'], 'conv': ['ssd_conv_w', 'cf_conv_w'], 'mid': ['w_out', 'w_q', 'w_kv', 'w_o'],
              'ffn': ['w_gate', 'w_up', 'w_down']}
    def shard(n, a):
        return jnp.transpose(a[0], (1, 0)) if n in TRANSPOSED else a[0]

    two_level = {'in': _CHIP_FLIPS}
    gathers, started = {}, None
    for grp, names in groups.items():
        shards = [wts[n][0] if grp == 'conv' else shard(n, wts[n]).astype(_MXU) for n in names]
        gathers[grp] = _send_start(shards, False, flips=two_level.get(grp), after=started, name="gather_%s_start" % grp)
        started = gathers[grp][4]

    def fetch(grp, after):
        srcs, lands = _send_wait(gathers[grp], started if after is None else after, False, flips=two_level.get(grp),
                                 name="gather_%s_wait" % grp)
        if grp in two_level:
            lands = _sibling_share(lands, srcs, name="gather_%s_share" % grp)
        out = {}
        for n, own, gth in zip(groups[grp], srcs, lands):
            gth = lax.dynamic_update_slice_in_dim(gth, own[None], me, axis=0)
            if n == 'w_kv' or grp == 'conv':
                out[n] = jnp.transpose(gth, (1, 0, 2)).reshape(gth.shape[1], N_DEV * gth.shape[2])
            else:
                out[n] = gth.reshape(N_DEV * gth.shape[1], gth.shape[2])
        return out

    exchanges = []

    def emit(grads, after=None):
        blocks = []
        for n, gw in grads.items():
            if n == 'w_kv':
                gw = jnp.transpose(gw.reshape(gw.shape[0], N_DEV, gw.shape[1] // N_DEV), (1, 0, 2))
            else:
                gw = gw.reshape(N_DEV, gw.shape[0] // N_DEV, gw.shape[1])
            blocks.append(gw.astype(jnp.bfloat16))
        first = next(iter(grads))
        exchanges.append((list(grads), _send_start(blocks, True, after=after, name="exchange_%s_start" % first), first))
        return exchanges[-1][1][4]

    full = {n: wts[n] for n in WEIGHT_NAMES if n not in BIG and n not in groups['conv']}
    full['norm_mix_g'] = _tie(norm_mix_g, started)

    loss_blk, grad_x, g = _local_step(x[0], mem[0], loss_target[0], full, fetch, emit)

    small = [n for n in WEIGHT_NAMES if n not in BIG]
    g['loss'] = loss_blk[0:1, 0:1]
    items = small + ['loss']
    size = {n: math.prod(g[n].shape) for n in items}
    seg = {n: -(-size[n] // LANES) * LANES for n in items}
    off, pos = {}, 0
    for n in items:
        off[n], pos = pos, pos + seg[n]
    rows = -(-pos // (LANES * SUBLANES)) * SUBLANES
    flat = jnp.concatenate([jnp.pad(g[n].reshape(-1), (0, seg[n] - size[n])) for n in items]
                           + [jnp.zeros((rows * LANES - pos,), _F32)])
    small_sent = _send_start([flat.reshape(rows, LANES)], False, name="gather_small_start")

    out_g, out_d, out_m, out_v = {}, {}, {}, {}
    done = small_sent[4]
    for names, handles, first in exchanges:
        srcs, lands = _send_wait(handles, done, True, name="exchange_%s_wait" % first)
        for n, own, parts in zip(names, srcs, lands):
            res = _adamw(parts, shard(n, wts[n]), shard(n, mom[n]), shard(n, var[n]), own=own, me=me, name="adamw_" + n)
            out_g[n], out_d[n], out_m[n], out_v[n] = [(jnp.transpose(r, (1, 0)) if n in TRANSPOSED else r)[None] for r in res]
            done = res[0]

    srcs, lands = _send_wait(small_sent, out_g[exchanges[-1][0][-1]], False, name="gather_small_wait")
    small_parts = lax.dynamic_update_slice_in_dim(lands[0], srcs[0][None], me, axis=0)
    g_row = _sum_parts(small_parts, name="sum_small_grads").reshape(1, rows * LANES)
    loss = g_row[0, off['loss']]
    rep = [n for n in small if n not in groups['conv']]
    as_row = lambda a: a.reshape(1, -1)
    res = _adamw_rows(g_row, [off[n] for n in rep], [as_row(wts[n]) for n in rep], [as_row(mom[n]) for n in rep],
                      [as_row(var[n]) for n in rep], name="adamw_small")
    for i, n in enumerate(rep):
        out_g[n], out_d[n], out_m[n], out_v[n] = [r.reshape(wts[n].shape) for r in res[4 * i:4 * i + 4]]
    for n in groups['conv']:
        k_taps, width = g[n].shape
        g_full = g_row[0, off[n]:off[n] + size[n]].reshape(k_taps, width)
        g_mine = lax.dynamic_slice_in_dim(g_full, me * (width // N_DEV), width // N_DEV, axis=1)
        res = _adamw(g_mine[None], wts[n][0], mom[n][0], var[n][0], name="adamw_" + n)
        out_g[n], out_d[n], out_m[n], out_v[n] = [r[None] for r in res]

    return (loss, grad_x[None], *[out_g[n] for n in WEIGHT_NAMES], *[out_d[n] for n in WEIGHT_NAMES],
            *[out_m[n] for n in WEIGHT_NAMES], *[out_v[n] for n in WEIGHT_NAMES])
```

```python
import functools
import math

import jax
import jax.numpy as jnp
from jax import lax
from jax.experimental import pallas as pl
from jax.experimental.pallas import tpu as pltpu

_F32 = jnp.float32
_MXU = jnp.bfloat16
_PREC = None
_VMEM_LIMIT = 56 * 1024 * 1024

D_MODEL = 1024
HEAD_DIM = 64
SSD_HEADS = 16
SSD_WIDTH = 1024
SSD_STATE = 128
SSD_CONV = 4
CHUNK = 128
XBC_WIDTH = 1536
CF_WIDTH = 1024
CF_CONV = 31
X_HEADS = 4
X_HEAD_DIM = 256
D_FF = 2816
EPS = 1e-6
N_DEV = 8
LANES = 128
SUBLANES = 8

ADAM_LR = 0.001
ADAM_B1 = 0.9
ADAM_B2 = 0.999
ADAM_EPS = 1e-08
ADAM_WD = 0.01
ADAM_STEP = 10

MESH = pl.DeviceIdType.MESH
WEIGHT_NAMES = ['norm_mix_g', 'w_in', 'ssd_conv_w', 'ssd_conv_b', 'ssd_dt_bias', 'ssd_A_log', 'ssd_D', 'ssd_norm_g',
                'cf_conv_w', 'cf_conv_b', 'cf_ln_g', 'cf_ln_b', 'w_out', 'norm_xattn_g', 'norm_mem_g', 'w_q', 'w_kv',
                'w_o', 'norm_ffn_g', 'w_gate', 'w_up', 'w_down', 'norm_final_g']
BIG = ['w_in', 'w_out', 'w_q', 'w_kv', 'w_o', 'w_gate', 'w_up', 'w_down']
TRANSPOSED = ('w_in', 'w_gate', 'w_up')


def _params(sem=None):
    return pltpu.CompilerParams(dimension_semantics=sem, vmem_limit_bytes=_VMEM_LIMIT)


def _pick(n, cands):
    for c in cands:
        if n % c == 0:
            return c
    return n


def _mm(a, b, *, ta=False, tb=False, add=None, out_dtype=_F32, name):
    (kdim, m) = a.shape if ta else a.shape[::-1]
    (n, k2) = b.shape if tb else b.shape[::-1]
    assert kdim == k2, (a.shape, b.shape, ta, tb)
    if ta:
        tm = m if m <= 1024 else _pick(m, (1408, 1024, 512, 256, 128))
        tn = n if n <= 1536 else _pick(n, (1408, 1024, 512, 256, 128))
        size = lambda arr: jnp.dtype(arr.dtype).itemsize
        fits = lambda t: (2 * t * (tm * size(a) + tn * size(b)) + tm * tn * (4 + 2 * jnp.dtype(out_dtype).itemsize)
                          <= _VMEM_LIMIT * 3 // 4)
        tk = next((t for t in (2048, 1024, 512, 256, 128) if kdim % t == 0 and fits(t)), _pick(kdim, (128,)))
    else:
        tm = _pick(m, (512, 256, 128))
        tn = n if n <= 2816 else _pick(n, (1408, 1024, 512, 256, 128))
        tk = kdim if kdim <= 2816 else _pick(kdim, (1408, 1024, 512, 256, 128))
    nk = kdim // tk
    dn = (((0 if ta else 1,), (1 if tb else 0,)), ((), ()))

    def body(*refs):
        a_ref, b_ref = refs[0], refs[1]
        add_ref = refs[2] if add is not None else None
        o_ref = refs[3 if add is not None else 2]
        acc_ref = refs[-1]
        k = pl.program_id(2)
        prod = lax.dot_general(a_ref[...].astype(_MXU), b_ref[...].astype(_MXU), dn,
                               preferred_element_type=_F32, precision=_PREC)

        def finish(r):
            if add_ref is not None:
                r = r + add_ref[...].astype(_F32)
            o_ref[...] = r.astype(o_ref.dtype)

        if nk == 1:
            finish(prod)
            return

        @pl.when(k == 0)
        def _():
            acc_ref[...] = prod

        @pl.when(jnp.logical_and(k > 0, k < nk - 1))
        def _():
            acc_ref[...] += prod

        @pl.when(k == nk - 1)
        def _():
            finish(acc_ref[...] + prod)

    a_spec = pl.BlockSpec((tk, tm), lambda i, j, k: (k, i)) if ta else pl.BlockSpec((tm, tk), lambda i, j, k: (i, k))
    b_spec = pl.BlockSpec((tn, tk), lambda i, j, k: (j, k)) if tb else pl.BlockSpec((tk, tn), lambda i, j, k: (k, j))
    o_spec = pl.BlockSpec((tm, tn), lambda i, j, k: (i, j))
    ins, specs = [a, b], [a_spec, b_spec]
    if add is not None:
        ins.append(add)
        specs.append(o_spec)
    return pl.pallas_call(
        body, name=name, grid=(m // tm, n // tn, nk), in_specs=specs, out_specs=o_spec,
        out_shape=jax.ShapeDtypeStruct((m, n), out_dtype),
        scratch_shapes=[pltpu.VMEM((tm, tn), _F32)] if nk > 1 else [],
        compiler_params=_params(("parallel", "parallel", "arbitrary")),
    )(*ins)


def _mm_tn_blocked(a, b, nblk, *, out_dtype, name):
    kdim, m = a.shape
    w = b.shape[1] // nblk

    def body(a_ref, b_ref, o_ref):
        o_ref[...] = lax.dot_general(a_ref[...].astype(_MXU), b_ref[...].astype(_MXU), _DN["tn"],
                                     preferred_element_type=_F32, precision=_PREC).astype(o_ref.dtype)

    return pl.pallas_call(
        body, name=name, grid=(nblk,),
        in_specs=[pl.BlockSpec((kdim, m), lambda j: (0, 0)), pl.BlockSpec((kdim, w), lambda j: (0, j))],
        out_specs=pl.BlockSpec((None, m, w), lambda j: (j, 0, 0)),
        out_shape=jax.ShapeDtypeStruct((nblk, m, w), out_dtype),
        compiler_params=_params(("parallel",)),
    )(a, b)


def _resident(shape):
    return pl.BlockSpec(shape, lambda i: (0,) * len(shape), pipeline_mode=pl.Buffered(1))


def _mm_fan_out(a, bs, *, tb, out_dtypes, epilogue=None, extra_outs=(), tm=512, name):
    m, kdim = a.shape
    tm = min(tm, m)
    ns = [b.shape[0] if tb else b.shape[1] for b in bs]
    nb = len(bs)
    kind = "nt" if tb else "nn"

    def body(*refs):
        a_ref, b_refs, o_refs = refs[0], refs[1:1 + nb], refs[1 + nb:]
        av = a_ref[...].astype(_MXU)
        prods = [lax.dot_general(av, b[...].astype(_MXU), _DN[kind], preferred_element_type=_F32, precision=_PREC)
                 for b in b_refs]
        for o_ref, p in zip(o_refs[:nb], prods):
            o_ref[...] = p.astype(o_ref.dtype)
        if epilogue is not None:
            for o_ref, v in zip(o_refs[nb:], _tup(epilogue(*prods))):
                o_ref[...] = v.astype(o_ref.dtype)

    widths = ns + [w for w, _ in extra_outs]
    dtypes = list(out_dtypes) + [dt for _, dt in extra_outs]
    return pl.pallas_call(
        body, name=name, grid=(m // tm,),
        in_specs=[pl.BlockSpec((tm, kdim), lambda i: (i, 0))] + [_resident(b.shape) for b in bs],
        out_specs=[pl.BlockSpec((tm, w), lambda i: (i, 0)) for w in widths],
        out_shape=[jax.ShapeDtypeStruct((m, w), dt) for w, dt in zip(widths, dtypes)],
        compiler_params=_params(("parallel",)),
    )(a, *bs)


def _mm_fan_in(pairs, *, add=None, out_dtype=_F32, prologue=None, pro_ins=(), pro_out_dtypes=(), epilogue=None,
               tm=512, name):
    bs = [b for _, b in pairs]
    nb = len(bs)
    n = bs[0].shape[1]
    rows_in = list(pro_ins) if prologue is not None else [a for a, _ in pairs]
    m = rows_in[0].shape[0]
    tm = min(tm, m)
    n_r = len(rows_in)

    def body(*refs):
        r_refs, b_refs = refs[:n_r], refs[n_r:n_r + nb]
        pos = n_r + nb
        add_ref = refs[pos] if add is not None else None
        pos += add is not None
        epi_ref = refs[pos] if epilogue is not None else None
        pos += epilogue is not None
        o_ref, po_refs = refs[pos], refs[pos + 1:]
        if prologue is not None:
            a_vals = _tup(prologue(*[r[...].astype(_F32) for r in r_refs]))
            for po, v in zip(po_refs, a_vals):
                po[...] = v.astype(po.dtype)
        else:
            a_vals = [r[...] for r in r_refs]
        acc = None
        for av, b in zip(a_vals, b_refs):
            p = lax.dot_general(av.astype(_MXU), b[...].astype(_MXU), _DN["nn"], preferred_element_type=_F32,
                                precision=_PREC)
            acc = p if acc is None else acc + p
        if add_ref is not None:
            acc = acc + add_ref[...].astype(_F32)
        o_ref[...] = acc.astype(o_ref.dtype)
        if epilogue is not None:
            po_refs[-1][...] = epilogue[0](acc, epi_ref[...]).astype(po_refs[-1].dtype)

    row = lambda w: pl.BlockSpec((tm, w), lambda i: (i, 0))
    ins = rows_in + bs + ([add] if add is not None else []) + ([epilogue[1]] if epilogue is not None else [])
    in_specs = ([row(r.shape[1]) for r in rows_in] + [_resident(b.shape) for b in bs]
                + ([row(n)] if add is not None else []) + ([_resident(epilogue[1].shape)] if epilogue is not None else []))
    extra = [(b.shape[0], dt) for b, dt in zip(bs, pro_out_dtypes)] if prologue is not None else []
    if epilogue is not None:
        extra.append((n, epilogue[2]))
    res = pl.pallas_call(
        body, name=name, grid=(m // tm,), in_specs=in_specs,
        out_specs=[row(n)] + [row(w) for w, _ in extra],
        out_shape=[jax.ShapeDtypeStruct((m, n), out_dtype)] + [jax.ShapeDtypeStruct((m, w), dt) for w, dt in extra],
        compiler_params=_params(("parallel",)),
    )(*ins)
    return res if extra else res[0]


def _row_spec(r, ts):
    if isinstance(r, tuple):
        arr, width, cblk = r
        return arr, pl.BlockSpec((ts, width), lambda i, cblk=cblk: (i, cblk))
    return r, pl.BlockSpec((ts, r.shape[1]), lambda i: (i, 0))


def _tup(v):
    return tuple(v) if isinstance(v, (tuple, list)) else (v,)


def _row_fwd(f, rows, params, outs, *, name, ts=512):
    s = (rows[0][0] if isinstance(rows[0], tuple) else rows[0]).shape[0]
    ts = min(ts, s)
    arrs, specs = zip(*[_row_spec(r, ts) for r in rows])
    n_r, n_p = len(rows), len(params)

    def body(*refs):
        rv = [r[...].astype(_F32) for r in refs[:n_r]]
        pv = [p[...] for p in refs[n_r:n_r + n_p]]
        res = _tup(f(*rv, *pv))
        for o_ref, v in zip(refs[n_r + n_p:], res):
            o_ref[...] = v.astype(o_ref.dtype)

    res = pl.pallas_call(
        body, name=name, grid=(s // ts,),
        in_specs=list(specs) + [pl.BlockSpec(p.shape, lambda i: (0, 0)) for p in params],
        out_specs=[pl.BlockSpec((ts, w), lambda i: (i, 0)) for w, _ in outs],
        out_shape=[jax.ShapeDtypeStruct((s, w), dt) for w, dt in outs],
        compiler_params=_params(("parallel",)),
    )(*arrs, *params)
    return res[0] if len(outs) == 1 else res


def _row_bwd(f, rows, params, cts, *, need=None, adds=None, row_dtypes=None, name, ts=512):
    s = (rows[0][0] if isinstance(rows[0], tuple) else rows[0]).shape[0]
    ts = min(ts, s)
    arrs, specs = zip(*[_row_spec(r, ts) for r in rows])
    n_r, n_p, n_c = len(rows), len(params), len(cts)
    need = [True] * n_r if need is None else need
    adds = {} if adds is None else adds
    add_keys = sorted(adds)
    row_dtypes = [_F32] * n_r if row_dtypes is None else row_dtypes
    needed = [j for j in range(n_r) if need[j]]
    widths = [specs[j].block_shape[1] for j in range(n_r)]

    def body(*refs):
        pos = 0
        r_refs = refs[pos:pos + n_r]; pos += n_r
        p_refs = refs[pos:pos + n_p]; pos += n_p
        c_refs = refs[pos:pos + n_c]; pos += n_c
        a_refs = refs[pos:pos + len(add_keys)]; pos += len(add_keys)
        dr_refs = refs[pos:pos + len(needed)]; pos += len(needed)
        dp_refs = refs[pos:pos + n_p]
        rv = [r[...].astype(_F32) for r in r_refs]
        pv = [p[...] for p in p_refs]
        _, vjp = jax.vjp(lambda *a: _tup(f(*a)), *rv, *pv)
        g = vjp(tuple(c[...].astype(_F32) for c in c_refs))
        for o_ref, j in zip(dr_refs, needed):
            v = g[j]
            if j in adds:
                v = v + a_refs[add_keys.index(j)][...].astype(_F32)
            o_ref[...] = v.astype(o_ref.dtype)
        if n_p:
            @pl.when(pl.program_id(0) == 0)
            def _():
                for dp in dp_refs:
                    dp[...] = jnp.zeros_like(dp)
            for dp, v in zip(dp_refs, g[n_r:]):
                dp[...] += v

    ct_specs = [pl.BlockSpec((ts, c.shape[1]), lambda i: (i, 0)) for c in cts]
    add_specs = [pl.BlockSpec((ts, adds[j].shape[1]), lambda i: (i, 0)) for j in add_keys]
    res = pl.pallas_call(
        body, name=name, grid=(s // ts,),
        in_specs=list(specs) + [pl.BlockSpec(p.shape, lambda i: (0, 0)) for p in params] + ct_specs + add_specs,
        out_specs=[pl.BlockSpec((ts, widths[j]), lambda i: (i, 0)) for j in needed]
        + [pl.BlockSpec(p.shape, lambda i: (0, 0)) for p in params],
        out_shape=[jax.ShapeDtypeStruct((s, widths[j]), row_dtypes[j]) for j in needed]
        + [jax.ShapeDtypeStruct(p.shape, _F32) for p in params],
        compiler_params=_params(("arbitrary",)),
    )(*arrs, *params, *cts, *[adds[j] for j in add_keys])
    return list(res[:len(needed)]), list(res[len(needed):])


_DN = {"nn": (((1,), (0,)), ((), ())), "nt": (((1,), (1,)), ((), ())), "tn": (((0,), (0,)), ((), ()))}


def _dot_raw(a, b, kind):
    return lax.dot_general(a.astype(_MXU), b.astype(_MXU), _DN[kind], preferred_element_type=_F32, precision=_PREC)


@functools.partial(jax.custom_vjp, nondiff_argnums=(2,))
def _dot1(a, b, kind):
    return _dot_raw(a, b, kind)


def _dot1_bwd(kind, res, ct):
    a, b = res
    if kind == "nn":
        return _dot_raw(ct, b, "nt"), _dot_raw(a, ct, "tn")
    if kind == "nt":
        return _dot_raw(ct, b, "nn"), _dot_raw(ct, a, "tn")
    return _dot_raw(b, ct, "nt"), _dot_raw(a, ct, "nn")


_dot1.defvjp(lambda a, b, kind: (_dot_raw(a, b, kind), (a, b)), _dot1_bwd)


def _sig(v):
    return 1.0 / (1.0 + jnp.exp(-v))


def _silu(v):
    return v * _sig(v)


def _f_rms(x, g):
    return x * lax.rsqrt(jnp.mean(x * x, axis=-1, keepdims=True) + EPS) * g


def _f_gate(y, xs, z, dexp, g):
    v = (y + dexp * xs) * _silu(z)
    half = SSD_WIDTH // 2
    parts = []
    for grp in range(2):
        vg = v[:, grp * half:(grp + 1) * half]
        parts.append(vg * lax.rsqrt(jnp.mean(vg * vg, axis=-1, keepdims=True) + EPS) * g[:, grp * half:(grp + 1) * half])
    return jnp.concatenate(parts, axis=1)


def _f_ln(u, g, b):
    mu = jnp.mean(u, axis=-1, keepdims=True)
    var = jnp.mean(jnp.square(u - mu), axis=-1, keepdims=True)
    return _silu((u - mu) * lax.rsqrt(var + EPS) * g + b)


def _f_swiglu(gate, up):
    return _silu(gate) * up


def _f_att(q, k, v):
    outs = []
    for h in range(X_HEADS):
        sl = slice(h * X_HEAD_DIM, (h + 1) * X_HEAD_DIM)
        s = _dot1(q[:, sl], k[:, sl], "nt") * (X_HEAD_DIM ** -0.5)
        s = s - lax.stop_gradient(jnp.max(s, axis=-1, keepdims=True))
        p = jnp.exp(s)
        p = p / jnp.sum(p, axis=-1, keepdims=True)
        outs.append(_dot1(p, v[:, sl], "nn"))
    return jnp.concatenate(outs, axis=1)


def _loss_bwd(x3, target, g, *, dx_dtype=_F32, name, ts=512):
    s, d = x3.shape

    def f(x, t, gv):
        return 0.5 * jnp.sum(jnp.mean(jnp.square(_f_rms(x, gv) - t), axis=-1))

    def body(x_ref, t_ref, g_ref, dx_ref, dg_ref, l_ref):
        @pl.when(pl.program_id(0) == 0)
        def _():
            dg_ref[...] = jnp.zeros_like(dg_ref)
            l_ref[...] = jnp.zeros_like(l_ref)

        lv, (dx, dg) = jax.value_and_grad(f, argnums=(0, 2))(x_ref[...].astype(_F32), t_ref[...], g_ref[...])
        dx_ref[...] = dx.astype(dx_ref.dtype)
        dg_ref[...] += dg
        l_ref[...] += lv

    row = pl.BlockSpec((ts, d), lambda i: (i, 0))
    return pl.pallas_call(
        body, name=name, grid=(s // ts,),
        in_specs=[row, row, pl.BlockSpec((1, d), lambda i: (0, 0))],
        out_specs=[row, pl.BlockSpec((1, d), lambda i: (0, 0)), pl.BlockSpec((SUBLANES, LANES), lambda i: (0, 0))],
        out_shape=[jax.ShapeDtypeStruct((s, d), dx_dtype), jax.ShapeDtypeStruct((1, d), _F32),
                   jax.ShapeDtypeStruct((SUBLANES, LANES), _F32)],
        compiler_params=_params(("arbitrary",)),
    )(x3, target, g)


_CONV_PAD = 32
_CONV_ROWS = 128
_CONV_CB = 128


def _conv_taps(k_taps):
    groups = {}
    for k in range(k_taps):
        j = k_taps - 1 - k
        groups.setdefault(j % SUBLANES, []).append((k, j))
    return groups


def _conv_window(win, wv, groups, init):
    pad, rows = _CONV_PAD, _CONV_ROWS
    acc = init
    for rot, taps in groups.items():
        rolled = win if rot == 0 else pltpu.roll(win, rot, 0)
        for k, j in taps:
            off = pad - (j - rot)
            acc = acc + rolled[off:off + rows, :] * wv[k:k + 1, :]
    return acc


def _conv_fill(x_refs, xp_ref, s, glu):
    pad, cb = _CONV_PAD, _CONV_CB
    step = _pick(s, (512, 256, _CONV_ROWS))
    xp_ref[0:pad, :] = jnp.zeros((pad, cb), _F32)

    def fill(r, carry):
        base = pl.multiple_of(r * step, step)
        v = x_refs[0][pl.ds(base, step), :].astype(_F32)
        if glu:
            v = v * _sig(x_refs[1][pl.ds(base, step), :].astype(_F32))
        xp_ref[pl.ds(pad + base, step), :] = v
        return carry

    lax.fori_loop(0, s // step, fill, 0)


def _conv_fwd(xs, w, b, k_taps, *, glu=False, act=False, out_dtype=_F32, name):
    s, c = xs[0].shape
    kp = w.shape[0]
    pad, rows, cb = _CONV_PAD, _CONV_ROWS, _CONV_CB
    groups = _conv_taps(k_taps)
    n_in = len(xs)

    def body(*refs):
        x_refs = refs[:n_in]
        w_ref, b_ref, o_ref, xp_ref = refs[n_in:]
        _conv_fill(x_refs, xp_ref, s, glu)
        wv = w_ref[...]
        bias = jnp.broadcast_to(b_ref[...], (rows, cb))

        def chunk(r, carry):
            base = pl.multiple_of(r * rows, rows)
            acc = _conv_window(xp_ref[pl.ds(base, rows + pad), :], wv, groups, bias)
            o_ref[pl.ds(base, rows), :] = (_silu(acc) if act else acc).astype(o_ref.dtype)
            return carry

        lax.fori_loop(0, s // rows, chunk, 0)

    col = pl.BlockSpec((s, cb), lambda i: (0, i))
    return pl.pallas_call(
        body, name=name, grid=(c // cb,),
        in_specs=[col] * n_in + [pl.BlockSpec((kp, cb), lambda i: (0, i)), pl.BlockSpec((1, cb), lambda i: (0, i))],
        out_specs=col, out_shape=jax.ShapeDtypeStruct((s, c), out_dtype),
        scratch_shapes=[pltpu.VMEM((s + pad, cb), _F32)],
        compiler_params=_params(("parallel",)),
    )(*xs, w, b)


def _conv_bwd(xs, w, b, dy, k_taps, *, glu=False, act=False, name):
    s, c = xs[0].shape
    kp = w.shape[0]
    pad, rows, cb = _CONV_PAD, _CONV_ROWS, _CONV_CB
    groups = _conv_taps(k_taps)
    win_rows = rows + pad
    n_in = len(xs)

    def fold(v):
        acc = v[0:SUBLANES, :]
        for i in range(1, rows // SUBLANES):
            acc = acc + v[i * SUBLANES:(i + 1) * SUBLANES, :]
        return acc

    def body(*refs):
        x_refs = refs[:n_in]
        w_ref, b_ref, dy_ref = refs[n_in:n_in + 3]
        dx_refs = refs[n_in + 3:2 * n_in + 3]
        dw_ref, db_ref, xp_ref, dyp_ref, acc_ref, dbacc_ref = refs[2 * n_in + 3:]
        _conv_fill(x_refs, xp_ref, s, glu)
        dyp_ref[s:s + pad, :] = jnp.zeros((pad, cb), _F32)
        acc_ref[...] = jnp.zeros_like(acc_ref)
        dbacc_ref[...] = jnp.zeros_like(dbacc_ref)
        wv = w_ref[...]
        bias = jnp.broadcast_to(b_ref[...], (rows, cb))

        def through_act(r, carry):
            base = pl.multiple_of(r * rows, rows)
            d = dy_ref[pl.ds(base, rows), :].astype(_F32)
            if act:
                pre = _conv_window(xp_ref[pl.ds(base, win_rows), :], wv, groups, bias)
                sg = _sig(pre)
                d = d * (sg * (1.0 + pre * (1.0 - sg)))
            dyp_ref[pl.ds(base, rows), :] = d
            return carry

        lax.fori_loop(0, s // rows, through_act, 0)

        def chunk(r, carry):
            base = pl.multiple_of(r * rows, rows)
            xwin = xp_ref[pl.ds(base, win_rows), :]
            dwin = dyp_ref[pl.ds(base, win_rows), :]
            dyc = dwin[0:rows, :]
            dxacc = jnp.zeros((rows, cb), _F32)
            for rot, taps in groups.items():
                xr = xwin if rot == 0 else pltpu.roll(xwin, rot, 0)
                dr = dwin if rot == 0 else pltpu.roll(dwin, win_rows - rot, 0)
                for k, j in taps:
                    a8 = j - rot
                    dxacc = dxacc + dr[a8:a8 + rows, :] * wv[k:k + 1, :]
                    prod = dyc * xr[pad - a8:pad - a8 + rows, :]
                    acc_ref[k * SUBLANES:(k + 1) * SUBLANES, :] += fold(prod)
            dbacc_ref[...] += fold(dyc)
            if glu:
                av = x_refs[0][pl.ds(base, rows), :].astype(_F32)
                sg = _sig(x_refs[1][pl.ds(base, rows), :].astype(_F32))
                dx_refs[0][pl.ds(base, rows), :] = (dxacc * sg).astype(dx_refs[0].dtype)
                dx_refs[1][pl.ds(base, rows), :] = (dxacc * av * sg * (1.0 - sg)).astype(dx_refs[1].dtype)
            else:
                dx_refs[0][pl.ds(base, rows), :] = dxacc.astype(dx_refs[0].dtype)
            return carry

        lax.fori_loop(0, s // rows, chunk, 0)
        dw_ref[...] = jnp.zeros_like(dw_ref)
        for k in range(k_taps):
            dw_ref[k:k + 1, :] = jnp.sum(acc_ref[k * SUBLANES:(k + 1) * SUBLANES, :], axis=0, keepdims=True)
        db_ref[...] = jnp.sum(dbacc_ref[...], axis=0, keepdims=True)

    col = pl.BlockSpec((s, cb), lambda i: (0, i))
    wspec = pl.BlockSpec((kp, cb), lambda i: (0, i))
    bspec = pl.BlockSpec((1, cb), lambda i: (0, i))
    dx_dtype = xs[0].dtype
    res = pl.pallas_call(
        body, name=name, grid=(c // cb,),
        in_specs=[col] * n_in + [wspec, bspec, col], out_specs=[col] * n_in + [wspec, bspec],
        out_shape=[jax.ShapeDtypeStruct((s, c), dx_dtype)] * n_in
        + [jax.ShapeDtypeStruct((kp, c), _F32), jax.ShapeDtypeStruct((1, c), _F32)],
        scratch_shapes=[pltpu.VMEM((s + pad, cb), _F32), pltpu.VMEM((s + pad, cb), _F32),
                        pltpu.VMEM((kp * SUBLANES, cb), _F32), pltpu.VMEM((SUBLANES, cb), _F32)],
        compiler_params=_params(("parallel",)),
    )(*xs, w, b, dy)
    return list(res[:n_in]), res[n_in], res[n_in + 1]


def _tri_sum(v, lower):
    l = v.shape[0]
    r, c = lax.broadcasted_iota(jnp.int32, (l, l), 0), lax.broadcasted_iota(jnp.int32, (l, l), 1)
    tri = ((r >= c) if lower else (r <= c)).astype(jnp.bfloat16)
    hi = v.astype(jnp.bfloat16)
    r1 = v - hi.astype(_F32)
    mid = r1.astype(jnp.bfloat16)
    lo = (r1 - mid.astype(_F32)).astype(jnp.bfloat16)
    out = jnp.zeros_like(v)
    for part in (hi, mid, lo):
        out = out + lax.dot_general(tri, part, _DN["nn"], preferred_element_type=_F32)
    return out


@jax.custom_vjp
def _cumsum_rows(v):
    return _tri_sum(v, True)


_cumsum_rows.defvjp(lambda v: (_tri_sum(v, True), None), lambda _, ct: (_tri_sum(ct, False),))


def _ssd_chunk(xbc, dtraw, prev, bias, alog):
    l = xbc.shape[0]
    xs = xbc[:, :SSD_WIDTH]
    bm = xbc[:, SSD_WIDTH:SSD_WIDTH + 2 * SSD_STATE]
    cm = xbc[:, SSD_WIDTH + 2 * SSD_STATE:]
    v = dtraw + bias
    dt = jnp.maximum(v, 0.0) + jnp.log1p(jnp.exp(-jnp.abs(v)))
    a_neg = -jnp.exp(alog)
    acs = _cumsum_rows(dt * a_neg)
    acs_t = acs.T
    total = acs[l - 1:l, :]
    row = lax.broadcasted_iota(jnp.int32, (l, l), 0)
    colv = lax.broadcasted_iota(jnp.int32, (l, l), 1)
    causal = row >= colv
    lane_lo = lax.broadcasted_iota(jnp.int32, (l, LANES), 1) < HEAD_DIM
    row_lo = lax.broadcasted_iota(jnp.int32, (LANES, SSD_STATE), 0) < HEAD_DIM

    def pair_lanes(m, h0):
        return jnp.where(lane_lo, m[:, h0:h0 + 1], m[:, h0 + 1:h0 + 2])

    ys, news = [], []
    cb = {}
    for j in range(SSD_HEADS // 2):
        h0 = 2 * j
        grp = h0 // (SSD_HEADS // 2)
        bg = bm[:, grp * SSD_STATE:(grp + 1) * SSD_STATE]
        cg = cm[:, grp * SSD_STATE:(grp + 1) * SSD_STATE]
        if grp not in cb:
            cb[grp] = _dot1(cg, bg, "nt")
        xdt = xs[:, j * LANES:(j + 1) * LANES] * pair_lanes(dt, h0)
        y = jnp.zeros((l, LANES), _F32)
        for hh, mask in ((h0, lane_lo), (h0 + 1, jnp.logical_not(lane_lo))):
            seg = acs[:, hh:hh + 1] - acs_t[hh:hh + 1, :]
            dec = jnp.exp(jnp.where(causal, seg, -jnp.inf))
            y = y + _dot1(cb[grp] * dec, jnp.where(mask, xdt, 0.0), "nn")
        acs_p = pair_lanes(acs, h0)
        prev_p = prev[j * LANES:(j + 1) * LANES, :]
        y = y + _dot1(cg, prev_p, "nt") * jnp.exp(acs_p)
        total_p = jnp.where(lane_lo[0:1, :], total[:, h0:h0 + 1], total[:, h0 + 1:h0 + 2])
        wgt = jnp.exp(total_p - acs_p)
        st = _dot1(xdt * wgt, bg, "tn")
        cdec = jnp.exp(jnp.where(row_lo, total[:, h0:h0 + 1], total[:, h0 + 1:h0 + 2]))
        news.append(prev_p * cdec + st)
        ys.append(y)
    return jnp.concatenate(ys, axis=1), jnp.concatenate(news, axis=0)


def _ssd_gate_chunk(xbc, dtraw, prev, bias, alog, z, dexp, g):
    y, new = _ssd_chunk(xbc, dtraw, prev, bias, alog)
    return _f_gate(y, xbc[:, :SSD_WIDTH], z, dexp, g), new


def _ssd_fwd(xbc, dtraw, bias, alog, z, dexp, g, *, name):
    s = xbc.shape[0]
    nc = s // CHUNK
    nstate = SSD_HEADS * HEAD_DIM

    def body(x_ref, dt_ref, b_ref, a_ref, z_ref, d_ref, g_ref, y_ref, st_ref, state_ref):
        @pl.when(pl.program_id(0) == 0)
        def _():
            state_ref[...] = jnp.zeros_like(state_ref)

        prev = state_ref[...]
        st_ref[...] = prev
        y, new = _ssd_gate_chunk(x_ref[...].astype(_F32), dt_ref[...], prev, b_ref[...], a_ref[...], z_ref[...].astype(_F32),
                                 d_ref[...], g_ref[...])
        y_ref[...] = y.astype(y_ref.dtype)
        state_ref[...] = new

    small = pl.BlockSpec((1, LANES), lambda i: (0, 0))
    wide = pl.BlockSpec((1, SSD_WIDTH), lambda i: (0, 0))
    rows = pl.BlockSpec((CHUNK, SSD_WIDTH), lambda i: (i, 0))
    return pl.pallas_call(
        body, name=name, grid=(nc,),
        in_specs=[pl.BlockSpec((CHUNK, XBC_WIDTH), lambda i: (i, 0)), pl.BlockSpec((CHUNK, LANES), lambda i: (i, 0)),
                  small, small, rows, wide, wide],
        out_specs=[rows, pl.BlockSpec((None, nstate, SSD_STATE), lambda i: (i, 0, 0))],
        out_shape=[jax.ShapeDtypeStruct((s, SSD_WIDTH), _MXU), jax.ShapeDtypeStruct((nc, nstate, SSD_STATE), _F32)],
        scratch_shapes=[pltpu.VMEM((nstate, SSD_STATE), _F32)],
        compiler_params=_params(("arbitrary",)),
    )(xbc, dtraw, bias, alog, z, dexp, g)


def _ssd_bwd(xbc, dtraw, states, bias, alog, z, dexp, g, dy, *, name):
    s = xbc.shape[0]
    nc = s // CHUNK
    nstate = SSD_HEADS * HEAD_DIM

    def body(x_ref, dt_ref, st_ref, b_ref, a_ref, z_ref, d_ref, g_ref, dy_ref,
             dx_ref, ddt_ref, db_ref, da_ref, dz_ref, dd_ref, dg_ref, dstate_ref):
        @pl.when(pl.program_id(0) == 0)
        def _():
            dstate_ref[...] = jnp.zeros_like(dstate_ref)
            for acc in (db_ref, da_ref, dd_ref, dg_ref):
                acc[...] = jnp.zeros_like(acc)

        _, vjp = jax.vjp(_ssd_gate_chunk, x_ref[...].astype(_F32), dt_ref[...], st_ref[...], b_ref[...], a_ref[...],
                         z_ref[...].astype(_F32), d_ref[...], g_ref[...])
        dx, ddt, dprev, db, da, dz, dd, dg = vjp((dy_ref[...].astype(_F32), dstate_ref[...]))
        dx_ref[...] = dx.astype(dx_ref.dtype)
        ddt_ref[...] = ddt
        dz_ref[...] = dz.astype(dz_ref.dtype)
        db_ref[...] += db
        da_ref[...] += da
        dd_ref[...] += dd
        dg_ref[...] += dg
        dstate_ref[...] = dprev

    rev = lambda i: (nc - 1 - i, 0)
    small = pl.BlockSpec((1, LANES), lambda i: (0, 0))
    wide = pl.BlockSpec((1, SSD_WIDTH), lambda i: (0, 0))
    rows = pl.BlockSpec((CHUNK, SSD_WIDTH), rev)
    return pl.pallas_call(
        body, name=name, grid=(nc,),
        in_specs=[pl.BlockSpec((CHUNK, XBC_WIDTH), rev), pl.BlockSpec((CHUNK, LANES), rev),
                  pl.BlockSpec((None, nstate, SSD_STATE), lambda i: (nc - 1 - i, 0, 0)), small, small, rows, wide, wide,
                  rows],
        out_specs=[pl.BlockSpec((CHUNK, XBC_WIDTH), rev), pl.BlockSpec((CHUNK, LANES), rev), small, small, rows, wide, wide],
        out_shape=[jax.ShapeDtypeStruct((s, XBC_WIDTH), xbc.dtype), jax.ShapeDtypeStruct((s, LANES), _F32),
                   jax.ShapeDtypeStruct((1, LANES), _F32), jax.ShapeDtypeStruct((1, LANES), _F32),
                   jax.ShapeDtypeStruct((s, SSD_WIDTH), z.dtype), jax.ShapeDtypeStruct((1, SSD_WIDTH), _F32),
                   jax.ShapeDtypeStruct((1, SSD_WIDTH), _F32)],
        scratch_shapes=[pltpu.VMEM((nstate, SSD_STATE), _F32)],
        compiler_params=_params(("arbitrary",)),
    )(xbc, dtraw, states, bias, alog, z, dexp, g, dy)


def _pad_cols(a, width):
    return jnp.pad(a, ((0, 0), (0, width - a.shape[1])))


def _pad_rows(a, rows):
    return jnp.pad(a, ((0, rows - a.shape[0]), (0, 0)))


def _tie(a, token):
    return a + token[0:1, 0:1].astype(a.dtype)


def _local_step(x, mem, target, w, fetch, emit):
    bf = _MXU
    d = D_MODEL
    h = _row_fwd(_f_rms, [x], [w['norm_mix_g']], [(d, bf)], name="f_norm_mix")
    w_in = fetch('in', h)['w_in']
    z_end, xbc_end, dt_end = SSD_WIDTH, SSD_WIDTH + XBC_WIDTH, SSD_WIDTH + XBC_WIDTH + SSD_HEADS
    w_z, w_xbc = w_in[:z_end], w_in[z_end:xbc_end]
    w_dt = _pad_rows(w_in[xbc_end:dt_end], LANES)
    w_a, w_g = w_in[dt_end:dt_end + CF_WIDTH], w_in[dt_end + CF_WIDTH:]
    dt_bias = _pad_cols(w['ssd_dt_bias'], LANES)
    a_log = _pad_cols(w['ssd_A_log'], LANES)
    d_exp = jnp.repeat(w['ssd_D'], HEAD_DIM, axis=1)
    g_final = w['norm_final_g'].reshape(1, D_MODEL)

    z, xbc, dtr, ga, gg = _mm_fan_out(h, [w_z, w_xbc, w_dt, w_a, w_g], tb=True, out_dtypes=[bf, bf, _F32, bf, bf],
                                      name="f_in")
    wc = fetch('conv', xbc)
    ssd_w = _pad_rows(wc['ssd_conv_w'], SUBLANES)
    cf_w = _pad_rows(wc['cf_conv_w'], 32)
    xbc_a = _conv_fwd([xbc], ssd_w, w['ssd_conv_b'], SSD_CONV, act=True, out_dtype=bf, name="f_ssd_conv")
    y_n, states = _ssd_fwd(xbc_a, dtr, dt_bias, a_log, z, d_exp, w['ssd_norm_g'], name="f_ssd")
    u_c = _conv_fwd([ga, gg], cf_w, w['cf_conv_b'], CF_CONV, glu=True, out_dtype=bf, name="f_cf_conv")
    u = _row_fwd(_f_ln, [u_c], [w['cf_ln_g'], w['cf_ln_b']], [(d, bf)], name="f_cf_ln")
    wm = fetch('mid', y_n)
    w_out_y, w_out_u = wm['w_out'][:SSD_WIDTH], wm['w_out'][SSD_WIDTH:]
    x1, hq = _mm_fan_in([(y_n, w_out_y), (u, w_out_u)], add=x, out_dtype=bf, epilogue=(_f_rms, w['norm_xattn_g'], bf),
                        name="f_out")
    q = _mm(hq, wm['w_q'], out_dtype=bf, name="f_q")
    memn = _row_fwd(_f_rms, [mem], [w['norm_mem_g']], [(d, bf)], name="f_norm_mem")
    kv = _mm(memn, wm['w_kv'], name="f_kv")
    k_mat, v_mat = kv[:, :d], kv[:, d:]
    o = _row_fwd(_f_att, [q], [k_mat, v_mat], [(d, bf)], ts=512, name="f_att")
    x2, hf = _mm_fan_in([(o, wm['w_o'])], add=x1, out_dtype=bf, epilogue=(_f_rms, w['norm_ffn_g'], bf), name="f_o")
    wf = fetch('ffn', hf)
    gate, up, act = _mm_fan_out(hf, [wf['w_gate'], wf['w_up']], tb=True, out_dtypes=[bf, bf], epilogue=_f_swiglu,
                                extra_outs=[(D_FF, bf)], tm=256, name="f_ffn_in")
    x3 = _mm(act, wf['w_down'], add=x2, out_dtype=bf, name="f_down")

    dx3, dg_final, loss = _loss_bwd(x3, target, g_final, dx_dtype=bf, name="b_loss")
    g = {'norm_final_g': dg_final.reshape(d)}

    dact = _mm(dx3, wf['w_down'], tb=True, out_dtype=bf, name="b_down_x")
    dw_down = _mm(act, dx3, ta=True, out_dtype=bf, name="b_down_w")
    def swiglu_bwd(gate_t, up_t, dact_t):
        return jax.vjp(_f_swiglu, gate_t, up_t)[1](dact_t)

    dhf, dgate, dup = _mm_fan_in([(None, wf['w_gate']), (None, wf['w_up'])], prologue=swiglu_bwd, pro_ins=[gate, up, dact],
                                 pro_out_dtypes=[bf, bf], out_dtype=bf, tm=256, name="b_ffn_in_x")
    sent = emit({'w_down': dw_down, 'w_gate': _mm(dgate, hf, ta=True, out_dtype=bf, name="b_gate_w"),
                 'w_up': _mm(dup, hf, ta=True, out_dtype=bf, name="b_up_w")})
    (dx2,), (g['norm_ffn_g'],) = _row_bwd(_f_rms, [x2], [_tie(w['norm_ffn_g'], sent)], [dhf], adds={0: dx3}, row_dtypes=[bf], name="b_norm_ffn")

    do = _mm(dx2, wm['w_o'], tb=True, out_dtype=bf, name="b_o_x")
    dw_o = _mm(o, dx2, ta=True, out_dtype=bf, name="b_o_w")
    (dq,), (dk, dv) = _row_bwd(_f_att, [q], [k_mat, v_mat], [do], row_dtypes=[bf], ts=512, name="b_att")
    dw_q = _mm(hq, dq, ta=True, out_dtype=bf, name="b_q_w")
    dhq = _mm(dq, wm['w_q'], tb=True, out_dtype=bf, name="b_q_x")
    (dx1,), (g['norm_xattn_g'],) = _row_bwd(_f_rms, [x1], [w['norm_xattn_g']], [dhq], adds={0: dx2}, row_dtypes=[bf], name="b_norm_xattn")
    dkv = jnp.concatenate([dk, dv], axis=1)
    dmemn = _mm(dkv, wm['w_kv'], tb=True, name="b_kv_x")
    _, (g['norm_mem_g'],) = _row_bwd(_f_rms, [mem], [w['norm_mem_g']], [dmemn], need=[False], name="b_norm_mem")
    sent = emit({'w_o': dw_o, 'w_q': dw_q, 'w_kv': _mm_tn_blocked(memn, dkv, N_DEV, out_dtype=bf, name="b_kv_w")},
                after=g['norm_mem_g'])

    dyn, du = _mm_fan_out(dx1, [w_out_y, w_out_u], tb=True, out_dtypes=[bf, bf], name="b_out_x")
    (du_c,), (g['cf_ln_g'], g['cf_ln_b']) = _row_bwd(_f_ln, [u_c], [_tie(w['cf_ln_g'], sent), w['cf_ln_b']], [du], row_dtypes=[bf], name="b_cf_ln")
    sent = emit({'w_out': jnp.concatenate([_mm(y_n, dx1, ta=True, out_dtype=bf, name="b_out_y_w"), _mm(u, dx1, ta=True, out_dtype=bf, name="b_out_u_w")], axis=0)})
    (dga, dgg), dcf_w, g['cf_conv_b'] = _conv_bwd([ga, gg], cf_w, w['cf_conv_b'], du_c, CF_CONV, glu=True, name="b_cf_conv")
    g['cf_conv_w'] = dcf_w[:CF_CONV]
    dxbc_a, ddtr, ddt_bias, da_log, dz, dd_exp, g['ssd_norm_g'] = _ssd_bwd(
        xbc_a, dtr, states, dt_bias, a_log, z, d_exp, _tie(w['ssd_norm_g'], sent), dyn, name="b_ssd")
    g['ssd_D'] = jnp.sum(dd_exp.reshape(SSD_HEADS, HEAD_DIM), axis=1).reshape(1, SSD_HEADS)
    g['ssd_dt_bias'] = ddt_bias[:, :SSD_HEADS]
    g['ssd_A_log'] = da_log[:, :SSD_HEADS]
    (dxbc,), dssd_w, g['ssd_conv_b'] = _conv_bwd([xbc], ssd_w, w['ssd_conv_b'], dxbc_a, SSD_CONV, act=True, name="b_ssd_conv")
    g['ssd_conv_w'] = dssd_w[:SSD_CONV]

    sent = emit({'w_in': jnp.concatenate([
        _mm(dz, h, ta=True, out_dtype=bf, name="b_in_z_w"), _mm(dxbc, h, ta=True, out_dtype=bf, name="b_in_xbc_w"),
        _mm(ddtr, h, ta=True, out_dtype=bf, name="b_in_dt_w")[:SSD_HEADS],
        _mm(dga, h, ta=True, out_dtype=bf, name="b_in_a_w"), _mm(dgg, h, ta=True, out_dtype=bf, name="b_in_g_w")], axis=0)})
    dh = _mm_fan_in([(dz, w_z), (dxbc, w_xbc), (ddtr, _tie(w_dt, sent)), (dga, w_a), (dgg, w_g)], out_dtype=bf,
                    name="b_in_x")
    (dx,), (g['norm_mix_g'],) = _row_bwd(_f_rms, [x], [w['norm_mix_g']], [dh], adds={0: dx1}, name="b_norm_mix")
    return loss, dx, g


_ANY = pl.BlockSpec(memory_space=pl.ANY)


def _place():
    x, y, c = lax.axis_index("x"), lax.axis_index("y"), lax.axis_index("c")
    return x, y, c


_HBM =pl.BlockSpec(memory_space=pltpu.HBM)
_SEM = pl.BlockSpec(memory_space=pltpu.SEMAPHORE)
_EFFECT = pltpu.SideEffectType.DATAFLOW_SIDE_EFFECTING
_FLIPS = [(dx, dy, dc) for dx in (0, 1) for dy in (0, 1) for dc in (0, 1)][1:]


def _peer(flip, x, y, c):
    return (1 - x if flip[0] else x, 1 - y if flip[1] else y, 1 - c if flip[2] else c)


_CHIP_FLIPS = [f for f in _FLIPS if f[2] == 0]


def _send_start(srcs, blocked, *, flips=None, after=None, name):
    n = len(srcs)
    flips = _FLIPS if flips is None else flips
    nf = len(flips)
    lands = [jax.ShapeDtypeStruct(s.shape if blocked else (N_DEV,) + s.shape, s.dtype) for s in srcs]
    n_in = 2 * n + (after is not None)

    def body(*refs):
        src_refs, land_refs = refs[:n], refs[n:2 * n]
        send_sems, recv_sems = refs[n_in], refs[n_in + 1]
        token = refs[-1]
        x, y, c = _place()
        me = 4 * x + 2 * y + c
        for a in range(n):
            for k, flip in enumerate(flips):
                p = _peer(flip, x, y, c)
                src = src_refs[a].at[4 * p[0] + 2 * p[1] + p[2]] if blocked else src_refs[a]
                pltpu.make_async_remote_copy(
                    src_ref=src, dst_ref=land_refs[a].at[me], send_sem=send_sems.at[nf * a + k],
                    recv_sem=recv_sems.at[nf * a + k], device_id=p, device_id_type=MESH).start()
        token[...] = jnp.zeros_like(token)

    res = pl.pallas_call(
        body, name=name,
        out_shape=(pltpu.SemaphoreType.DMA((nf * n,)), pltpu.SemaphoreType.DMA((nf * n,)),
                   *[pltpu.HBM(s.shape, s.dtype) for s in srcs], *[pltpu.HBM(l.shape, l.dtype) for l in lands],
                   jax.ShapeDtypeStruct((SUBLANES, LANES), _F32)),
        in_specs=[_HBM] * (2 * n) + [_ANY] * (after is not None),
        out_specs=(_SEM, _SEM, *[_HBM] * (2 * n), pl.BlockSpec(memory_space=pltpu.VMEM)),
        input_output_aliases={i: 2 + i for i in range(2 * n)},
        compiler_params=pltpu.CompilerParams(has_side_effects=_EFFECT),
    )(*[pltpu.with_memory_space_constraint(s, pltpu.HBM) for s in srcs],
      *[pltpu.with_memory_space_constraint(lax.empty(l.shape, l.dtype), pltpu.HBM) for l in lands],
      *([after] if after is not None else []))
    return res[0], res[1], list(res[2:2 + n]), list(res[2 + n:2 + 2 * n]), res[-1]


def _send_wait(handles, after, blocked, *, flips=None, name):
    send_sems, recv_sems, srcs, lands, _ = handles
    n = len(srcs)
    flips = _FLIPS if flips is None else flips
    nf = len(flips)

    def body(*refs):
        src_refs, land_refs = refs[:n], refs[n:2 * n]
        send_sems, recv_sems = refs[2 * n], refs[2 * n + 1]
        x, y, c = _place()
        for a in range(n):
            for k, flip in enumerate(flips):
                p = _peer(flip, x, y, c)
                pid = 4 * p[0] + 2 * p[1] + p[2]
                cp = pltpu.make_async_remote_copy(
                    src_ref=src_refs[a].at[pid] if blocked else src_refs[a], dst_ref=land_refs[a].at[pid],
                    send_sem=send_sems.at[nf * a + k], recv_sem=recv_sems.at[nf * a + k], device_id=p, device_id_type=MESH)
                cp.wait_send()
                cp.wait_recv()

    res = pl.pallas_call(
        body, name=name,
        out_shape=tuple(pltpu.HBM(s.shape, s.dtype) for s in srcs + lands),
        in_specs=[_HBM] * (2 * n) + [_SEM, _SEM, _ANY], out_specs=tuple([_HBM] * (2 * n)),
        input_output_aliases={i: i for i in range(2 * n)},
        compiler_params=pltpu.CompilerParams(has_side_effects=_EFFECT),
    )(*srcs, *lands, send_sems, recv_sems, after)
    return list(res[:n]), list(res[n:])


def _sibling_share(lands, owns, *, name):
    n = len(lands)

    def body(*refs):
        own_refs, land_refs = refs[n:2 * n], refs[2 * n:3 * n]
        send_sems, recv_sems = refs[3 * n], refs[3 * n + 1]
        x, y, c = _place()
        sibling = (x, y, 1 - c)
        chips = [(1 - x, y), (x, 1 - y), (1 - x, 1 - y)]

        def copy(a, k, block, src=None):
            slot = land_refs[a].at[block]
            return pltpu.make_async_remote_copy(
                src_ref=slot if src is None else src, dst_ref=slot, send_sem=send_sems.at[4 * a + k],
                recv_sem=recv_sems.at[4 * a + k], device_id=sibling, device_id_type=MESH)

        sends = []
        for a in range(n):
            sends.append(copy(a, 0, 4 * x + 2 * y + c, src=own_refs[a]))
            sends += [copy(a, 1 + j, 4 * cx + 2 * cy + c) for j, (cx, cy) in enumerate(chips)]
        for cp in sends:
            cp.start()
        for a in range(n):
            copy(a, 0, 4 * x + 2 * y + (1 - c)).wait_recv()
            for j, (cx, cy) in enumerate(chips):
                copy(a, 1 + j, 4 * cx + 2 * cy + (1 - c)).wait_recv()
        for cp in sends:
            cp.wait_send()

    return pl.pallas_call(
        body, name=name, in_specs=[_ANY] * (2 * n), out_specs=[_ANY] * n,
        out_shape=[jax.ShapeDtypeStruct(l.shape, l.dtype) for l in lands],
        input_output_aliases={i: i for i in range(n)},
        scratch_shapes=[pltpu.SemaphoreType.DMA((4 * n,)), pltpu.SemaphoreType.DMA((4 * n,))],
    )(*lands, *owns)


def _adamw(parts, w, m, v, *, own=None, me=None, name):
    p, r, c = parts.shape
    tr = _pick(r, (256, 176, 128, 64, 32, 16, 8))
    if own is not None:
        tc = c if tr < r else _pick(c, (256, 128))
        return _adamw_own(parts, own, me, w, m, v, tr, tc, name=name)

    def body(p_ref, w_ref, m_ref, v_ref, g_ref, d_ref, nm_ref, nv_ref):
        g = p_ref[0].astype(_F32)
        for i in range(1, p):
            g = g + p_ref[i].astype(_F32)
        _adamw_math(g, w_ref, m_ref, v_ref, g_ref, d_ref, nm_ref, nv_ref)

    blk = pl.BlockSpec((tr, c), lambda i: (i, 0))
    return pl.pallas_call(
        body, name=name, grid=(r // tr,),
        in_specs=[pl.BlockSpec((p, tr, c), lambda i: (0, i, 0)), blk, blk, blk], out_specs=[blk] * 4,
        out_shape=[jax.ShapeDtypeStruct((r, c), _F32)] * 4,
        compiler_params=_params(("parallel",)),
    )(parts, w, m, v)


def _adamw_math(g, w_ref, m_ref, v_ref, g_ref, d_ref, nm_ref, nv_ref):
    wv = w_ref[...]
    mn = ADAM_B1 * m_ref[...] + (1.0 - ADAM_B1) * g
    vn = ADAM_B2 * v_ref[...] + (1.0 - ADAM_B2) * jnp.square(g)
    m_hat = mn / (1.0 - ADAM_B1 ** ADAM_STEP)
    v_hat = vn / (1.0 - ADAM_B2 ** ADAM_STEP)
    g_ref[...] = g
    d_ref[...] = -ADAM_LR * (m_hat / (jnp.sqrt(v_hat) + ADAM_EPS) + ADAM_WD * wv)
    nm_ref[...] = mn
    nv_ref[...] = vn


def _adamw_own(parts, own, me, w, m, v, tr, tc, *, name):
    p, r, c = parts.shape

    def body(me_ref, p_ref, own_ref, w_ref, m_ref, v_ref, g_ref, d_ref, nm_ref, nv_ref):
        mine = own_ref[...].astype(_F32)
        g = jnp.where(me_ref[0] == 0, mine, p_ref[0].astype(_F32))
        for i in range(1, p):
            g = g + jnp.where(me_ref[0] == i, mine, p_ref[i].astype(_F32))
        _adamw_math(g, w_ref, m_ref, v_ref, g_ref, d_ref, nm_ref, nv_ref)

    blk = pl.BlockSpec((tr, tc), lambda i, j, me_ref: (i, j))
    grid_spec = pltpu.PrefetchScalarGridSpec(
        num_scalar_prefetch=1, grid=(r // tr, c // tc),
        in_specs=[pl.BlockSpec((p, tr, tc), lambda i, j, me_ref: (0, i, j)),
                  pl.BlockSpec((None, tr, tc), lambda i, j, me_ref: (me_ref[0], i, j)), blk, blk, blk],
        out_specs=[blk] * 4)
    return pl.pallas_call(
        body, name=name, grid_spec=grid_spec, out_shape=[jax.ShapeDtypeStruct((r, c), _F32)] * 4,
        compiler_params=_params(("parallel", "parallel")),
    )(me.reshape(1).astype(jnp.int32), parts, own, w, m, v)


def _adamw_rows(g_row, offsets, ws, ms, vs, *, name):
    k = len(ws)

    def body(*refs):
        g_ref, w_refs, m_refs, v_refs = refs[0], refs[1:1 + k], refs[1 + k:1 + 2 * k], refs[1 + 2 * k:1 + 3 * k]
        outs = refs[1 + 3 * k:]
        for i in range(k):
            gi = g_ref[:, offsets[i]:offsets[i] + ws[i].shape[1]]
            _adamw_math(gi, w_refs[i], m_refs[i], v_refs[i], *outs[4 * i:4 * i + 4])

    return pl.pallas_call(
        body, name=name, out_shape=[jax.ShapeDtypeStruct(w.shape, _F32) for w in ws for _ in range(4)],
    )(g_row, *ws, *ms, *vs)


def _sum_parts(parts, *, name):
    p, r, c = parts.shape

    def body(p_ref, o_ref):
        g = p_ref[0].astype(_F32)
        for i in range(1, p):
            g = g + p_ref[i].astype(_F32)
        o_ref[...] = g

    return pl.pallas_call(body, name=name, out_shape=jax.ShapeDtypeStruct((r, c), _F32))(parts)


def kernel(x, mem, norm_mix_g, w_in, ssd_conv_w, ssd_conv_b, ssd_dt_bias, ssd_A_log, ssd_D, ssd_norm_g, cf_conv_w, cf_conv_b, cf_ln_g, cf_ln_b, w_out, norm_xattn_g, norm_mem_g, w_q, w_kv, w_o, norm_ffn_g, w_gate, w_up, w_down, norm_final_g, loss_target, m_norm_mix_g, m_w_in, m_ssd_conv_w, m_ssd_conv_b, m_ssd_dt_bias, m_ssd_A_log, m_ssd_D, m_ssd_norm_g, m_cf_conv_w, m_cf_conv_b, m_cf_ln_g, m_cf_ln_b, m_w_out, m_norm_xattn_g, m_norm_mem_g, m_w_q, m_w_kv, m_w_o, m_norm_ffn_g, m_w_gate, m_w_up, m_w_down, m_norm_final_g, v_norm_mix_g, v_w_in, v_ssd_conv_w, v_ssd_conv_b, v_ssd_dt_bias, v_ssd_A_log, v_ssd_D, v_ssd_norm_g, v_cf_conv_w, v_cf_conv_b, v_cf_ln_g, v_cf_ln_b, v_w_out, v_norm_xattn_g, v_norm_mem_g, v_w_q, v_w_kv, v_w_o, v_norm_ffn_g, v_w_gate, v_w_up, v_w_down, v_norm_final_g):
    args = dict(locals())
    wts = {n: args[n] for n in WEIGHT_NAMES}
    mom = {n: args["m_" + n] for n in WEIGHT_NAMES}
    var = {n: args["v_" + n] for n in WEIGHT_NAMES}
    me = 4 * lax.axis_index("x") + 2 * lax.axis_index("y") + lax.axis_index("c")

    groups = {'in': ['w_in'], 'conv': ['ssd_conv_w', 'cf_conv_w'], 'mid': ['w_out', 'w_q', 'w_kv', 'w_o'],
              'ffn': ['w_gate', 'w_up', 'w_down']}
    def shard(n, a):
        return jnp.transpose(a[0], (1, 0)) if n in TRANSPOSED else a[0]

    two_level = {'in': _CHIP_FLIPS}
    gathers, started = {}, None
    for grp, names in groups.items():
        shards = [wts[n][0] if grp == 'conv' else shard(n, wts[n]).astype(_MXU) for n in names]
        gathers[grp] = _send_start(shards, False, flips=two_level.get(grp), after=started, name="gather_%s_start" % grp)
        started = gathers[grp][4]

    def fetch(grp, after):
        srcs, lands = _send_wait(gathers[grp], started if after is None else after, False, flips=two_level.get(grp),
                                 name="gather_%s_wait" % grp)
        if grp in two_level:
            lands = _sibling_share(lands, srcs, name="gather_%s_share" % grp)
        out = {}
        for n, own, gth in zip(groups[grp], srcs, lands):
            gth = lax.dynamic_update_slice_in_dim(gth, own[None], me, axis=0)
            if n == 'w_kv' or grp == 'conv':
                out[n] = jnp.transpose(gth, (1, 0, 2)).reshape(gth.shape[1], N_DEV * gth.shape[2])
            else:
                out[n] = gth.reshape(N_DEV * gth.shape[1], gth.shape[2])
        return out

    exchanges = []

    def emit(grads, after=None):
        blocks = []
        for n, gw in grads.items():
            if gw.ndim == 2:
                gw = gw.reshape(N_DEV, gw.shape[0] // N_DEV, gw.shape[1])
            blocks.append(gw.astype(jnp.bfloat16))
        first = next(iter(grads))
        exchanges.append((list(grads), _send_start(blocks, True, after=after, name="exchange_%s_start" % first), first))
        return exchanges[-1][1][4]

    full = {n: wts[n] for n in WEIGHT_NAMES if n not in BIG and n not in groups['conv']}
    full['norm_mix_g'] = _tie(norm_mix_g, started)

    loss_blk, grad_x, g = _local_step(x[0], mem[0], loss_target[0], full, fetch, emit)

    small = [n for n in WEIGHT_NAMES if n not in BIG]
    g['loss'] = loss_blk[0:1, 0:1]
    items = small + ['loss']
    size = {n: math.prod(g[n].shape) for n in items}
    seg = {n: -(-size[n] // LANES) * LANES for n in items}
    off, pos = {}, 0
    for n in items:
        off[n], pos = pos, pos + seg[n]
    rows = -(-pos // (LANES * SUBLANES)) * SUBLANES
    flat = jnp.concatenate([jnp.pad(g[n].reshape(-1), (0, seg[n] - size[n])) for n in items]
                           + [jnp.zeros((rows * LANES - pos,), _F32)])
    small_sent = _send_start([flat.reshape(rows, LANES)], False, name="gather_small_start")

    out_g, out_d, out_m, out_v = {}, {}, {}, {}
    done = small_sent[4]
    for names, handles, first in exchanges:
        srcs, lands = _send_wait(handles, done, True, name="exchange_%s_wait" % first)
        for n, own, parts in zip(names, srcs, lands):
            res = _adamw(parts, shard(n, wts[n]), shard(n, mom[n]), shard(n, var[n]), own=own, me=me, name="adamw_" + n)
            out_g[n], out_d[n], out_m[n], out_v[n] = [(jnp.transpose(r, (1, 0)) if n in TRANSPOSED else r)[None] for r in res]
            done = res[0]

    srcs, lands = _send_wait(small_sent, out_g[exchanges[-1][0][-1]], False, name="gather_small_wait")
    small_parts = lax.dynamic_update_slice_in_dim(lands[0], srcs[0][None], me, axis=0)
    g_row = _sum_parts(small_parts, name="sum_small_grads").reshape(1, rows * LANES)
    loss = g_row[0, off['loss']]
    rep = [n for n in small if n not in groups['conv']]
    as_row = lambda a: a.reshape(1, -1)
    res = _adamw_rows(g_row, [off[n] for n in rep], [as_row(wts[n]) for n in rep], [as_row(mom[n]) for n in rep],
                      [as_row(var[n]) for n in rep], name="adamw_small")
    for i, n in enumerate(rep):
        out_g[n], out_d[n], out_m[n], out_v[n] = [r.reshape(wts[n].shape) for r in res[4 * i:4 * i + 4]]
    for n in groups['conv']:
        k_taps, width = g[n].shape
        g_full = g_row[0, off[n]:off[n] + size[n]].reshape(k_taps, width)
        g_mine = lax.dynamic_slice_in_dim(g_full, me * (width // N_DEV), width // N_DEV, axis=1)
        res = _adamw(g_mine[None], wts[n][0], mom[n][0], var[n][0], name="adamw_" + n)
        out_g[n], out_d[n], out_m[n], out_v[n] = [r[None] for r in res]

    return (loss, grad_x[None], *[out_g[n] for n in WEIGHT_NAMES], *[out_d[n] for n in WEIGHT_NAMES],
            *[out_m[n] for n in WEIGHT_NAMES], *[out_v[n] for n in WEIGHT_NAMES])
```

```python
import functools
import math

import jax
import jax.numpy as jnp
from jax import lax
from jax.experimental import pallas as pl
from jax.experimental.pallas import tpu as pltpu

_F32 = jnp.float32
_MXU = jnp.bfloat16
_PREC = None
_VMEM_LIMIT = 56 * 1024 * 1024

D_MODEL = 1024
HEAD_DIM = 64
SSD_HEADS = 16
SSD_WIDTH = 1024
SSD_STATE = 128
SSD_CONV = 4
CHUNK = 128
XBC_WIDTH = 1536
CF_WIDTH = 1024
CF_CONV = 31
X_HEADS = 4
X_HEAD_DIM = 256
D_FF = 2816
EPS = 1e-6
N_DEV = 8
LANES = 128
SUBLANES = 8

ADAM_LR = 0.001
ADAM_B1 = 0.9
ADAM_B2 = 0.999
ADAM_EPS = 1e-08
ADAM_WD = 0.01
ADAM_STEP = 10

MESH = pl.DeviceIdType.MESH
WEIGHT_NAMES = ['norm_mix_g', 'w_in', 'ssd_conv_w', 'ssd_conv_b', 'ssd_dt_bias', 'ssd_A_log', 'ssd_D', 'ssd_norm_g',
                'cf_conv_w', 'cf_conv_b', 'cf_ln_g', 'cf_ln_b', 'w_out', 'norm_xattn_g', 'norm_mem_g', 'w_q', 'w_kv',
                'w_o', 'norm_ffn_g', 'w_gate', 'w_up', 'w_down', 'norm_final_g']
BIG = ['w_in', 'w_out', 'w_q', 'w_kv', 'w_o', 'w_gate', 'w_up', 'w_down']
TRANSPOSED = ('w_in', 'w_gate', 'w_up')


def _params(sem=None):
    return pltpu.CompilerParams(dimension_semantics=sem, vmem_limit_bytes=_VMEM_LIMIT)


def _pick(n, cands):
    for c in cands:
        if n % c == 0:
            return c
    return n


def _mm(a, b, *, ta=False, tb=False, add=None, out_dtype=_F32, name):
    (kdim, m) = a.shape if ta else a.shape[::-1]
    (n, k2) = b.shape if tb else b.shape[::-1]
    assert kdim == k2, (a.shape, b.shape, ta, tb)
    if ta:
        tm = m if m <= 1024 else _pick(m, (1408, 1024, 512, 256, 128))
        tn = n if n <= 1536 else _pick(n, (1408, 1024, 512, 256, 128))
        size = lambda arr: jnp.dtype(arr.dtype).itemsize
        fits = lambda t: (2 * t * (tm * size(a) + tn * size(b)) + tm * tn * (4 + 2 * jnp.dtype(out_dtype).itemsize)
                          <= _VMEM_LIMIT * 3 // 4)
        tk = next((t for t in (2048, 1024, 512, 256, 128) if kdim % t == 0 and fits(t)), _pick(kdim, (128,)))
    else:
        tm = _pick(m, (512, 256, 128))
        tn = n if n <= 2816 else _pick(n, (1408, 1024, 512, 256, 128))
        tk = kdim if kdim <= 2816 else _pick(kdim, (1408, 1024, 512, 256, 128))
    nk = kdim // tk
    dn = (((0 if ta else 1,), (1 if tb else 0,)), ((), ()))

    def body(*refs):
        a_ref, b_ref = refs[0], refs[1]
        add_ref = refs[2] if add is not None else None
        o_ref = refs[3 if add is not None else 2]
        acc_ref = refs[-1]
        k = pl.program_id(2)
        prod = lax.dot_general(a_ref[...].astype(_MXU), b_ref[...].astype(_MXU), dn,
                               preferred_element_type=_F32, precision=_PREC)

        def finish(r):
            if add_ref is not None:
                r = r + add_ref[...].astype(_F32)
            o_ref[...] = r.astype(o_ref.dtype)

        if nk == 1:
            finish(prod)
            return

        @pl.when(k == 0)
        def _():
            acc_ref[...] = prod

        @pl.when(jnp.logical_and(k > 0, k < nk - 1))
        def _():
            acc_ref[...] += prod

        @pl.when(k == nk - 1)
        def _():
            finish(acc_ref[...] + prod)

    a_spec = pl.BlockSpec((tk, tm), lambda i, j, k: (k, i)) if ta else pl.BlockSpec((tm, tk), lambda i, j, k: (i, k))
    b_spec = pl.BlockSpec((tn, tk), lambda i, j, k: (j, k)) if tb else pl.BlockSpec((tk, tn), lambda i, j, k: (k, j))
    o_spec = pl.BlockSpec((tm, tn), lambda i, j, k: (i, j))
    ins, specs = [a, b], [a_spec, b_spec]
    if add is not None:
        ins.append(add)
        specs.append(o_spec)
    return pl.pallas_call(
        body, name=name, grid=(m // tm, n // tn, nk), in_specs=specs, out_specs=o_spec,
        out_shape=jax.ShapeDtypeStruct((m, n), out_dtype),
        scratch_shapes=[pltpu.VMEM((tm, tn), _F32)] if nk > 1 else [],
        compiler_params=_params(("parallel", "parallel", "arbitrary")),
    )(*ins)


def _mm_tn_blocked(a, b, nblk, *, out_dtype, name):
    kdim, m = a.shape
    w = b.shape[1] // nblk

    def body(a_ref, b_ref, o_ref):
        o_ref[...] = lax.dot_general(a_ref[...].astype(_MXU), b_ref[...].astype(_MXU), _DN["tn"],
                                     preferred_element_type=_F32, precision=_PREC).astype(o_ref.dtype)

    return pl.pallas_call(
        body, name=name, grid=(nblk,),
        in_specs=[pl.BlockSpec((kdim, m), lambda j: (0, 0)), pl.BlockSpec((kdim, w), lambda j: (0, j))],
        out_specs=pl.BlockSpec((None, m, w), lambda j: (j, 0, 0)),
        out_shape=jax.ShapeDtypeStruct((nblk, m, w), out_dtype),
        compiler_params=_params(("parallel",)),
    )(a, b)


def _resident(shape):
    return pl.BlockSpec(shape, lambda i: (0,) * len(shape), pipeline_mode=pl.Buffered(1))


def _windows(bs):
    arrays, wins = [], []
    for b in bs:
        arr, lo, hi = b if isinstance(b, tuple) else (b, 0, b.shape[0])
        idx = next((i for i, x in enumerate(arrays) if x is arr), None)
        if idx is None:
            arrays.append(arr)
            idx = len(arrays) - 1
        wins.append((idx, lo, hi))
    return arrays, wins


def _mm_fan_out(a, bs, *, tb, out_dtypes, epilogue=None, extra_outs=(), tm=512, name):
    m, kdim = a.shape
    tm = min(tm, m)
    bs, wins = _windows(bs)
    ns = [hi - lo if tb else bs[i].shape[1] for i, lo, hi in wins]
    nb = len(bs)
    kind = "nt" if tb else "nn"

    def body(*refs):
        a_ref, b_refs, o_refs = refs[0], refs[1:1 + nb], refs[1 + nb:]
        av = a_ref[...].astype(_MXU)
        prods = [lax.dot_general(av, b_refs[i][lo:hi, :].astype(_MXU), _DN[kind], preferred_element_type=_F32,
                                 precision=_PREC) for i, lo, hi in wins]
        for o_ref, p in zip(o_refs[:len(wins)], prods):
            o_ref[...] = p.astype(o_ref.dtype)
        if epilogue is not None:
            for o_ref, v in zip(o_refs[len(wins):], _tup(epilogue(*prods))):
                o_ref[...] = v.astype(o_ref.dtype)

    widths = ns + [w for w, _ in extra_outs]
    dtypes = list(out_dtypes) + [dt for _, dt in extra_outs]
    return pl.pallas_call(
        body, name=name, grid=(m // tm,),
        in_specs=[pl.BlockSpec((tm, kdim), lambda i: (i, 0))] + [_resident(b.shape) for b in bs],
        out_specs=[pl.BlockSpec((tm, w), lambda i: (i, 0)) for w in widths],
        out_shape=[jax.ShapeDtypeStruct((m, w), dt) for w, dt in zip(widths, dtypes)],
        compiler_params=_params(("parallel",)),
    )(a, *bs)


def _mm_fan_in(pairs, *, add=None, out_dtype=_F32, prologue=None, pro_ins=(), pro_out_dtypes=(), epilogue=None,
               tm=512, name):
    bs, wins = _windows([b for _, b in pairs])
    nb = len(bs)
    n = bs[0].shape[1]
    rows_in = list(pro_ins) if prologue is not None else [a for a, _ in pairs]
    m = rows_in[0].shape[0]
    tm = min(tm, m)
    n_r = len(rows_in)

    def body(*refs):
        r_refs, b_refs = refs[:n_r], refs[n_r:n_r + nb]
        pos = n_r + nb
        add_ref = refs[pos] if add is not None else None
        pos += add is not None
        epi_ref = refs[pos] if epilogue is not None else None
        pos += epilogue is not None
        o_ref, po_refs = refs[pos], refs[pos + 1:]
        if prologue is not None:
            a_vals = _tup(prologue(*[r[...].astype(_F32) for r in r_refs]))
            for po, v in zip(po_refs, a_vals):
                po[...] = v.astype(po.dtype)
        else:
            a_vals = [r[...] for r in r_refs]
        acc = None
        for av, (i, lo, hi) in zip(a_vals, wins):
            p = lax.dot_general(av.astype(_MXU), b_refs[i][lo:hi, :].astype(_MXU), _DN["nn"], preferred_element_type=_F32,
                                precision=_PREC)
            acc = p if acc is None else acc + p
        if add_ref is not None:
            acc = acc + add_ref[...].astype(_F32)
        o_ref[...] = acc.astype(o_ref.dtype)
        if epilogue is not None:
            po_refs[-1][...] = epilogue[0](acc, epi_ref[...]).astype(po_refs[-1].dtype)

    row = lambda w: pl.BlockSpec((tm, w), lambda i: (i, 0))
    ins = rows_in + bs + ([add] if add is not None else []) + ([epilogue[1]] if epilogue is not None else [])
    in_specs = ([row(r.shape[1]) for r in rows_in] + [_resident(b.shape) for b in bs]
                + ([row(n)] if add is not None else []) + ([_resident(epilogue[1].shape)] if epilogue is not None else []))
    extra = [(hi - lo, dt) for (_, lo, hi), dt in zip(wins, pro_out_dtypes)] if prologue is not None else []
    if epilogue is not None:
        extra.append((n, epilogue[2]))
    res = pl.pallas_call(
        body, name=name, grid=(m // tm,), in_specs=in_specs,
        out_specs=[row(n)] + [row(w) for w, _ in extra],
        out_shape=[jax.ShapeDtypeStruct((m, n), out_dtype)] + [jax.ShapeDtypeStruct((m, w), dt) for w, dt in extra],
        compiler_params=_params(("parallel",)),
    )(*ins)
    return res if extra else res[0]


def _row_spec(r, ts):
    if isinstance(r, tuple):
        arr, width, cblk = r
        return arr, pl.BlockSpec((ts, width), lambda i, cblk=cblk: (i, cblk))
    return r, pl.BlockSpec((ts, r.shape[1]), lambda i: (i, 0))


def _tup(v):
    return tuple(v) if isinstance(v, (tuple, list)) else (v,)


def _row_fwd(f, rows, params, outs, *, name, ts=512):
    s = (rows[0][0] if isinstance(rows[0], tuple) else rows[0]).shape[0]
    ts = min(ts, s)
    arrs, specs = zip(*[_row_spec(r, ts) for r in rows])
    n_r, n_p = len(rows), len(params)

    def body(*refs):
        rv = [r[...].astype(_F32) for r in refs[:n_r]]
        pv = [p[...] for p in refs[n_r:n_r + n_p]]
        res = _tup(f(*rv, *pv))
        for o_ref, v in zip(refs[n_r + n_p:], res):
            o_ref[...] = v.astype(o_ref.dtype)

    res = pl.pallas_call(
        body, name=name, grid=(s // ts,),
        in_specs=list(specs) + [pl.BlockSpec(p.shape, lambda i: (0, 0)) for p in params],
        out_specs=[pl.BlockSpec((ts, w), lambda i: (i, 0)) for w, _ in outs],
        out_shape=[jax.ShapeDtypeStruct((s, w), dt) for w, dt in outs],
        compiler_params=_params(("parallel",)),
    )(*arrs, *params)
    return res[0] if len(outs) == 1 else res


def _row_bwd(f, rows, params, cts, *, need=None, adds=None, row_dtypes=None, name, ts=512):
    s = (rows[0][0] if isinstance(rows[0], tuple) else rows[0]).shape[0]
    ts = min(ts, s)
    arrs, specs = zip(*[_row_spec(r, ts) for r in rows])
    n_r, n_p, n_c = len(rows), len(params), len(cts)
    need = [True] * n_r if need is None else need
    adds = {} if adds is None else adds
    add_keys = sorted(adds)
    row_dtypes = [_F32] * n_r if row_dtypes is None else row_dtypes
    needed = [j for j in range(n_r) if need[j]]
    widths = [specs[j].block_shape[1] for j in range(n_r)]

    def body(*refs):
        pos = 0
        r_refs = refs[pos:pos + n_r]; pos += n_r
        p_refs = refs[pos:pos + n_p]; pos += n_p
        c_refs = refs[pos:pos + n_c]; pos += n_c
        a_refs = refs[pos:pos + len(add_keys)]; pos += len(add_keys)
        dr_refs = refs[pos:pos + len(needed)]; pos += len(needed)
        dp_refs = refs[pos:pos + n_p]
        rv = [r[...].astype(_F32) for r in r_refs]
        pv = [p[...] for p in p_refs]
        _, vjp = jax.vjp(lambda *a: _tup(f(*a)), *rv, *pv)
        g = vjp(tuple(c[...].astype(_F32) for c in c_refs))
        for o_ref, j in zip(dr_refs, needed):
            v = g[j]
            if j in adds:
                v = v + a_refs[add_keys.index(j)][...].astype(_F32)
            o_ref[...] = v.astype(o_ref.dtype)
        if n_p:
            @pl.when(pl.program_id(0) == 0)
            def _():
                for dp in dp_refs:
                    dp[...] = jnp.zeros_like(dp)
            for dp, v in zip(dp_refs, g[n_r:]):
                dp[...] += v

    ct_specs = [pl.BlockSpec((ts, c.shape[1]), lambda i: (i, 0)) for c in cts]
    add_specs = [pl.BlockSpec((ts, adds[j].shape[1]), lambda i: (i, 0)) for j in add_keys]
    res = pl.pallas_call(
        body, name=name, grid=(s // ts,),
        in_specs=list(specs) + [pl.BlockSpec(p.shape, lambda i: (0, 0)) for p in params] + ct_specs + add_specs,
        out_specs=[pl.BlockSpec((ts, widths[j]), lambda i: (i, 0)) for j in needed]
        + [pl.BlockSpec(p.shape, lambda i: (0, 0)) for p in params],
        out_shape=[jax.ShapeDtypeStruct((s, widths[j]), row_dtypes[j]) for j in needed]
        + [jax.ShapeDtypeStruct(p.shape, _F32) for p in params],
        compiler_params=_params(("arbitrary",)),
    )(*arrs, *params, *cts, *[adds[j] for j in add_keys])
    return list(res[:len(needed)]), list(res[len(needed):])


_DN = {"nn": (((1,), (0,)), ((), ())), "nt": (((1,), (1,)), ((), ())), "tn": (((0,), (0,)), ((), ()))}


def _dot_raw(a, b, kind):
    return lax.dot_general(a.astype(_MXU), b.astype(_MXU), _DN[kind], preferred_element_type=_F32, precision=_PREC)


@functools.partial(jax.custom_vjp, nondiff_argnums=(2,))
def _dot1(a, b, kind):
    return _dot_raw(a, b, kind)


def _dot1_bwd(kind, res, ct):
    a, b = res
    if kind == "nn":
        return _dot_raw(ct, b, "nt"), _dot_raw(a, ct, "tn")
    if kind == "nt":
        return _dot_raw(ct, b, "nn"), _dot_raw(ct, a, "tn")
    return _dot_raw(b, ct, "nt"), _dot_raw(a, ct, "nn")


_dot1.defvjp(lambda a, b, kind: (_dot_raw(a, b, kind), (a, b)), _dot1_bwd)


def _sig(v):
    return 1.0 / (1.0 + jnp.exp(-v))


def _silu(v):
    return v * _sig(v)


def _f_rms(x, g):
    return x * lax.rsqrt(jnp.mean(x * x, axis=-1, keepdims=True) + EPS) * g


def _f_gate(y, xs, z, dexp, g):
    v = (y + dexp * xs) * _silu(z)
    half = SSD_WIDTH // 2
    parts = []
    for grp in range(2):
        vg = v[:, grp * half:(grp + 1) * half]
        parts.append(vg * lax.rsqrt(jnp.mean(vg * vg, axis=-1, keepdims=True) + EPS) * g[:, grp * half:(grp + 1) * half])
    return jnp.concatenate(parts, axis=1)


def _f_ln(u, g, b):
    mu = jnp.mean(u, axis=-1, keepdims=True)
    var = jnp.mean(jnp.square(u - mu), axis=-1, keepdims=True)
    return _silu((u - mu) * lax.rsqrt(var + EPS) * g + b)


def _f_swiglu(gate, up):
    return _silu(gate) * up


def _f_att(q, k, v):
    outs = []
    for h in range(X_HEADS):
        sl = slice(h * X_HEAD_DIM, (h + 1) * X_HEAD_DIM)
        s = _dot1(q[:, sl], k[:, sl], "nt") * (X_HEAD_DIM ** -0.5)
        s = s - lax.stop_gradient(jnp.max(s, axis=-1, keepdims=True))
        p = jnp.exp(s)
        p = p / jnp.sum(p, axis=-1, keepdims=True)
        outs.append(_dot1(p, v[:, sl], "nn"))
    return jnp.concatenate(outs, axis=1)


def _loss_bwd(x3, target, g, *, dx_dtype=_F32, name, ts=512):
    s, d = x3.shape

    def f(x, t, gv):
        return 0.5 * jnp.sum(jnp.mean(jnp.square(_f_rms(x, gv) - t), axis=-1))

    def body(x_ref, t_ref, g_ref, dx_ref, dg_ref, l_ref):
        @pl.when(pl.program_id(0) == 0)
        def _():
            dg_ref[...] = jnp.zeros_like(dg_ref)
            l_ref[...] = jnp.zeros_like(l_ref)

        lv, (dx, dg) = jax.value_and_grad(f, argnums=(0, 2))(x_ref[...].astype(_F32), t_ref[...], g_ref[...])
        dx_ref[...] = dx.astype(dx_ref.dtype)
        dg_ref[...] += dg
        l_ref[...] += lv

    row = pl.BlockSpec((ts, d), lambda i: (i, 0))
    return pl.pallas_call(
        body, name=name, grid=(s // ts,),
        in_specs=[row, row, pl.BlockSpec((1, d), lambda i: (0, 0))],
        out_specs=[row, pl.BlockSpec((1, d), lambda i: (0, 0)), pl.BlockSpec((SUBLANES, LANES), lambda i: (0, 0))],
        out_shape=[jax.ShapeDtypeStruct((s, d), dx_dtype), jax.ShapeDtypeStruct((1, d), _F32),
                   jax.ShapeDtypeStruct((SUBLANES, LANES), _F32)],
        compiler_params=_params(("arbitrary",)),
    )(x3, target, g)


_CONV_PAD = 32
_CONV_ROWS = 128
_CONV_CB = 128


def _conv_taps(k_taps):
    groups = {}
    for k in range(k_taps):
        j = k_taps - 1 - k
        groups.setdefault(j % SUBLANES, []).append((k, j))
    return groups


def _conv_window(win, wv, groups, init):
    pad, rows = _CONV_PAD, _CONV_ROWS
    acc = init
    for rot, taps in groups.items():
        rolled = win if rot == 0 else pltpu.roll(win, rot, 0)
        for k, j in taps:
            off = pad - (j - rot)
            acc = acc + rolled[off:off + rows, :] * wv[k:k + 1, :]
    return acc


def _conv_fill(x_refs, xp_ref, s, glu):
    pad, cb = _CONV_PAD, _CONV_CB
    step = _pick(s, (512, 256, _CONV_ROWS))
    xp_ref[0:pad, :] = jnp.zeros((pad, cb), _F32)

    def fill(r, carry):
        base = pl.multiple_of(r * step, step)
        v = x_refs[0][pl.ds(base, step), :].astype(_F32)
        if glu:
            v = v * _sig(x_refs[1][pl.ds(base, step), :].astype(_F32))
        xp_ref[pl.ds(pad + base, step), :] = v
        return carry

    lax.fori_loop(0, s // step, fill, 0)


def _conv_fwd(xs, w, b, k_taps, *, glu=False, act=False, out_dtype=_F32, name):
    s, c = xs[0].shape
    kp = w.shape[0]
    pad, rows, cb = _CONV_PAD, _CONV_ROWS, _CONV_CB
    groups = _conv_taps(k_taps)
    n_in = len(xs)

    def body(*refs):
        x_refs = refs[:n_in]
        w_ref, b_ref, o_ref, xp_ref = refs[n_in:]
        _conv_fill(x_refs, xp_ref, s, glu)
        wv = w_ref[...]
        bias = jnp.broadcast_to(b_ref[...], (rows, cb))

        def chunk(r, carry):
            base = pl.multiple_of(r * rows, rows)
            acc = _conv_window(xp_ref[pl.ds(base, rows + pad), :], wv, groups, bias)
            o_ref[pl.ds(base, rows), :] = (_silu(acc) if act else acc).astype(o_ref.dtype)
            return carry

        lax.fori_loop(0, s // rows, chunk, 0)

    col = pl.BlockSpec((s, cb), lambda i: (0, i))
    return pl.pallas_call(
        body, name=name, grid=(c // cb,),
        in_specs=[col] * n_in + [pl.BlockSpec((kp, cb), lambda i: (0, i)), pl.BlockSpec((1, cb), lambda i: (0, i))],
        out_specs=col, out_shape=jax.ShapeDtypeStruct((s, c), out_dtype),
        scratch_shapes=[pltpu.VMEM((s + pad, cb), _F32)],
        compiler_params=_params(("parallel",)),
    )(*xs, w, b)


def _conv_bwd(xs, w, b, dy, k_taps, *, glu=False, act=False, name):
    s, c = xs[0].shape
    kp = w.shape[0]
    pad, rows, cb = _CONV_PAD, _CONV_ROWS, _CONV_CB
    groups = _conv_taps(k_taps)
    win_rows = rows + pad
    n_in = len(xs)

    def fold(v):
        acc = v[0:SUBLANES, :]
        for i in range(1, rows // SUBLANES):
            acc = acc + v[i * SUBLANES:(i + 1) * SUBLANES, :]
        return acc

    def body(*refs):
        x_refs = refs[:n_in]
        w_ref, b_ref, dy_ref = refs[n_in:n_in + 3]
        dx_refs = refs[n_in + 3:2 * n_in + 3]
        dw_ref, db_ref, xp_ref, dyp_ref, acc_ref, dbacc_ref = refs[2 * n_in + 3:]
        _conv_fill(x_refs, xp_ref, s, glu)
        dyp_ref[s:s + pad, :] = jnp.zeros((pad, cb), _F32)
        acc_ref[...] = jnp.zeros_like(acc_ref)
        dbacc_ref[...] = jnp.zeros_like(dbacc_ref)
        wv = w_ref[...]
        bias = jnp.broadcast_to(b_ref[...], (rows, cb))

        def through_act(r, carry):
            base = pl.multiple_of(r * rows, rows)
            d = dy_ref[pl.ds(base, rows), :].astype(_F32)
            if act:
                pre = _conv_window(xp_ref[pl.ds(base, win_rows), :], wv, groups, bias)
                sg = _sig(pre)
                d = d * (sg * (1.0 + pre * (1.0 - sg)))
            dyp_ref[pl.ds(base, rows), :] = d
            return carry

        lax.fori_loop(0, s // rows, through_act, 0)

        def chunk(r, carry):
            base = pl.multiple_of(r * rows, rows)
            xwin = xp_ref[pl.ds(base, win_rows), :]
            dwin = dyp_ref[pl.ds(base, win_rows), :]
            dyc = dwin[0:rows, :]
            dxacc = jnp.zeros((rows, cb), _F32)
            for rot, taps in groups.items():
                xr = xwin if rot == 0 else pltpu.roll(xwin, rot, 0)
                dr = dwin if rot == 0 else pltpu.roll(dwin, win_rows - rot, 0)
                for k, j in taps:
                    a8 = j - rot
                    dxacc = dxacc + dr[a8:a8 + rows, :] * wv[k:k + 1, :]
                    prod = dyc * xr[pad - a8:pad - a8 + rows, :]
                    acc_ref[k * SUBLANES:(k + 1) * SUBLANES, :] += fold(prod)
            dbacc_ref[...] += fold(dyc)
            if glu:
                av = x_refs[0][pl.ds(base, rows), :].astype(_F32)
                sg = _sig(x_refs[1][pl.ds(base, rows), :].astype(_F32))
                dx_refs[0][pl.ds(base, rows), :] = (dxacc * sg).astype(dx_refs[0].dtype)
                dx_refs[1][pl.ds(base, rows), :] = (dxacc * av * sg * (1.0 - sg)).astype(dx_refs[1].dtype)
            else:
                dx_refs[0][pl.ds(base, rows), :] = dxacc.astype(dx_refs[0].dtype)
            return carry

        lax.fori_loop(0, s // rows, chunk, 0)
        dw_ref[...] = jnp.zeros_like(dw_ref)
        for k in range(k_taps):
            dw_ref[k:k + 1, :] = jnp.sum(acc_ref[k * SUBLANES:(k + 1) * SUBLANES, :], axis=0, keepdims=True)
        db_ref[...] = jnp.sum(dbacc_ref[...], axis=0, keepdims=True)

    col = pl.BlockSpec((s, cb), lambda i: (0, i))
    wspec = pl.BlockSpec((kp, cb), lambda i: (0, i))
    bspec = pl.BlockSpec((1, cb), lambda i: (0, i))
    dx_dtype = xs[0].dtype
    res = pl.pallas_call(
        body, name=name, grid=(c // cb,),
        in_specs=[col] * n_in + [wspec, bspec, col], out_specs=[col] * n_in + [wspec, bspec],
        out_shape=[jax.ShapeDtypeStruct((s, c), dx_dtype)] * n_in
        + [jax.ShapeDtypeStruct((kp, c), _F32), jax.ShapeDtypeStruct((1, c), _F32)],
        scratch_shapes=[pltpu.VMEM((s + pad, cb), _F32), pltpu.VMEM((s + pad, cb), _F32),
                        pltpu.VMEM((kp * SUBLANES, cb), _F32), pltpu.VMEM((SUBLANES, cb), _F32)],
        compiler_params=_params(("parallel",)),
    )(*xs, w, b, dy)
    return list(res[:n_in]), res[n_in], res[n_in + 1]


def _tri_sum(v, lower):
    l = v.shape[0]
    r, c = lax.broadcasted_iota(jnp.int32, (l, l), 0), lax.broadcasted_iota(jnp.int32, (l, l), 1)
    tri = ((r >= c) if lower else (r <= c)).astype(jnp.bfloat16)
    hi = v.astype(jnp.bfloat16)
    r1 = v - hi.astype(_F32)
    mid = r1.astype(jnp.bfloat16)
    lo = (r1 - mid.astype(_F32)).astype(jnp.bfloat16)
    out = jnp.zeros_like(v)
    for part in (hi, mid, lo):
        out = out + lax.dot_general(tri, part, _DN["nn"], preferred_element_type=_F32)
    return out


@jax.custom_vjp
def _cumsum_rows(v):
    return _tri_sum(v, True)


_cumsum_rows.defvjp(lambda v: (_tri_sum(v, True), None), lambda _, ct: (_tri_sum(ct, False),))


def _ssd_chunk(xbc, dtraw, prev, bias, alog):
    l = xbc.shape[0]
    xs = xbc[:, :SSD_WIDTH]
    bm = xbc[:, SSD_WIDTH:SSD_WIDTH + 2 * SSD_STATE]
    cm = xbc[:, SSD_WIDTH + 2 * SSD_STATE:]
    v = dtraw + bias
    dt = jnp.maximum(v, 0.0) + jnp.log1p(jnp.exp(-jnp.abs(v)))
    a_neg = -jnp.exp(alog)
    acs = _cumsum_rows(dt * a_neg)
    acs_t = acs.T
    total = acs[l - 1:l, :]
    row = lax.broadcasted_iota(jnp.int32, (l, l), 0)
    colv = lax.broadcasted_iota(jnp.int32, (l, l), 1)
    causal = row >= colv
    lane_lo = lax.broadcasted_iota(jnp.int32, (l, LANES), 1) < HEAD_DIM
    row_lo = lax.broadcasted_iota(jnp.int32, (LANES, SSD_STATE), 0) < HEAD_DIM

    def pair_lanes(m, h0):
        return jnp.where(lane_lo, m[:, h0:h0 + 1], m[:, h0 + 1:h0 + 2])

    ys, news = [], []
    cb = {}
    for j in range(SSD_HEADS // 2):
        h0 = 2 * j
        grp = h0 // (SSD_HEADS // 2)
        bg = bm[:, grp * SSD_STATE:(grp + 1) * SSD_STATE]
        cg = cm[:, grp * SSD_STATE:(grp + 1) * SSD_STATE]
        if grp not in cb:
            cb[grp] = _dot1(cg, bg, "nt")
        xdt = xs[:, j * LANES:(j + 1) * LANES] * pair_lanes(dt, h0)
        y = jnp.zeros((l, LANES), _F32)
        for hh, mask in ((h0, lane_lo), (h0 + 1, jnp.logical_not(lane_lo))):
            seg = acs[:, hh:hh + 1] - acs_t[hh:hh + 1, :]
            dec = jnp.exp(jnp.where(causal, seg, -jnp.inf))
            y = y + _dot1(cb[grp] * dec, jnp.where(mask, xdt, 0.0), "nn")
        acs_p = pair_lanes(acs, h0)
        prev_p = prev[j * LANES:(j + 1) * LANES, :]
        y = y + _dot1(cg, prev_p, "nt") * jnp.exp(acs_p)
        total_p = jnp.where(lane_lo[0:1, :], total[:, h0:h0 + 1], total[:, h0 + 1:h0 + 2])
        wgt = jnp.exp(total_p - acs_p)
        st = _dot1(xdt * wgt, bg, "tn")
        cdec = jnp.exp(jnp.where(row_lo, total[:, h0:h0 + 1], total[:, h0 + 1:h0 + 2]))
        news.append(prev_p * cdec + st)
        ys.append(y)
    return jnp.concatenate(ys, axis=1), jnp.concatenate(news, axis=0)


def _ssd_gate_chunk(xbc, dtraw, prev, bias, alog, z, dexp, g):
    y, new = _ssd_chunk(xbc, dtraw, prev, bias, alog)
    return _f_gate(y, xbc[:, :SSD_WIDTH], z, dexp, g), new


def _ssd_fwd(xbc, dtraw, bias, alog, z, dexp, g, *, name):
    s = xbc.shape[0]
    nc = s // CHUNK
    nstate = SSD_HEADS * HEAD_DIM

    def body(x_ref, dt_ref, b_ref, a_ref, z_ref, d_ref, g_ref, y_ref, st_ref, state_ref):
        @pl.when(pl.program_id(0) == 0)
        def _():
            state_ref[...] = jnp.zeros_like(state_ref)

        prev = state_ref[...]
        st_ref[...] = prev
        y, new = _ssd_gate_chunk(x_ref[...].astype(_F32), dt_ref[...], prev, b_ref[...], a_ref[...], z_ref[...].astype(_F32),
                                 d_ref[...], g_ref[...])
        y_ref[...] = y.astype(y_ref.dtype)
        state_ref[...] = new

    small = pl.BlockSpec((1, LANES), lambda i: (0, 0))
    wide = pl.BlockSpec((1, SSD_WIDTH), lambda i: (0, 0))
    rows = pl.BlockSpec((CHUNK, SSD_WIDTH), lambda i: (i, 0))
    return pl.pallas_call(
        body, name=name, grid=(nc,),
        in_specs=[pl.BlockSpec((CHUNK, XBC_WIDTH), lambda i: (i, 0)), pl.BlockSpec((CHUNK, LANES), lambda i: (i, 0)),
                  small, small, rows, wide, wide],
        out_specs=[rows, pl.BlockSpec((None, nstate, SSD_STATE), lambda i: (i, 0, 0))],
        out_shape=[jax.ShapeDtypeStruct((s, SSD_WIDTH), _MXU), jax.ShapeDtypeStruct((nc, nstate, SSD_STATE), _F32)],
        scratch_shapes=[pltpu.VMEM((nstate, SSD_STATE), _F32)],
        compiler_params=_params(("arbitrary",)),
    )(xbc, dtraw, bias, alog, z, dexp, g)


def _ssd_bwd(xbc, dtraw, states, bias, alog, z, dexp, g, dy, *, name):
    s = xbc.shape[0]
    nc = s // CHUNK
    nstate = SSD_HEADS * HEAD_DIM

    def body(x_ref, dt_ref, st_ref, b_ref, a_ref, z_ref, d_ref, g_ref, dy_ref,
             dx_ref, ddt_ref, db_ref, da_ref, dz_ref, dd_ref, dg_ref, dstate_ref):
        @pl.when(pl.program_id(0) == 0)
        def _():
            dstate_ref[...] = jnp.zeros_like(dstate_ref)
            for acc in (db_ref, da_ref, dd_ref, dg_ref):
                acc[...] = jnp.zeros_like(acc)

        _, vjp = jax.vjp(_ssd_gate_chunk, x_ref[...].astype(_F32), dt_ref[...], st_ref[...], b_ref[...], a_ref[...],
                         z_ref[...].astype(_F32), d_ref[...], g_ref[...])
        dx, ddt, dprev, db, da, dz, dd, dg = vjp((dy_ref[...].astype(_F32), dstate_ref[...]))
        dx_ref[...] = dx.astype(dx_ref.dtype)
        ddt_ref[...] = ddt
        dz_ref[...] = dz.astype(dz_ref.dtype)
        db_ref[...] += db
        da_ref[...] += da
        dd_ref[...] += dd
        dg_ref[...] += dg
        dstate_ref[...] = dprev

    rev = lambda i: (nc - 1 - i, 0)
    small = pl.BlockSpec((1, LANES), lambda i: (0, 0))
    wide = pl.BlockSpec((1, SSD_WIDTH), lambda i: (0, 0))
    rows = pl.BlockSpec((CHUNK, SSD_WIDTH), rev)
    return pl.pallas_call(
        body, name=name, grid=(nc,),
        in_specs=[pl.BlockSpec((CHUNK, XBC_WIDTH), rev), pl.BlockSpec((CHUNK, LANES), rev),
                  pl.BlockSpec((None, nstate, SSD_STATE), lambda i: (nc - 1 - i, 0, 0)), small, small, rows, wide, wide,
                  rows],
        out_specs=[pl.BlockSpec((CHUNK, XBC_WIDTH), rev), pl.BlockSpec((CHUNK, LANES), rev), small, small, rows, wide, wide],
        out_shape=[jax.ShapeDtypeStruct((s, XBC_WIDTH), xbc.dtype), jax.ShapeDtypeStruct((s, LANES), _F32),
                   jax.ShapeDtypeStruct((1, LANES), _F32), jax.ShapeDtypeStruct((1, LANES), _F32),
                   jax.ShapeDtypeStruct((s, SSD_WIDTH), z.dtype), jax.ShapeDtypeStruct((1, SSD_WIDTH), _F32),
                   jax.ShapeDtypeStruct((1, SSD_WIDTH), _F32)],
        scratch_shapes=[pltpu.VMEM((nstate, SSD_STATE), _F32)],
        compiler_params=_params(("arbitrary",)),
    )(xbc, dtraw, states, bias, alog, z, dexp, g, dy)


def _pad_cols(a, width):
    return jnp.pad(a, ((0, 0), (0, width - a.shape[1])))


def _pad_rows(a, rows):
    return jnp.pad(a, ((0, rows - a.shape[0]), (0, 0)))


def _tie(a, token):
    return a + token[0:1, 0:1].astype(a.dtype)


def _local_step(x, mem, target, w, fetch, emit):
    bf = _MXU
    d = D_MODEL
    h = _row_fwd(_f_rms, [x], [w['norm_mix_g']], [(d, bf)], name="f_norm_mix")
    w_in = fetch('in', h)['w_in']
    z_end, xbc_end, dt_end = SSD_WIDTH, SSD_WIDTH + XBC_WIDTH, SSD_WIDTH + XBC_WIDTH + SSD_HEADS
    w_z, w_xbc = (w_in, 0, z_end), (w_in, z_end, xbc_end)
    w_dt = _pad_rows(w_in[xbc_end:dt_end], LANES)
    w_a, w_g = (w_in, dt_end, dt_end + CF_WIDTH), (w_in, dt_end + CF_WIDTH, w_in.shape[0])
    dt_bias = _pad_cols(w['ssd_dt_bias'], LANES)
    a_log = _pad_cols(w['ssd_A_log'], LANES)
    d_exp = jnp.repeat(w['ssd_D'], HEAD_DIM, axis=1)
    g_final = w['norm_final_g'].reshape(1, D_MODEL)

    z, xbc, dtr, ga, gg = _mm_fan_out(h, [w_z, w_xbc, w_dt, w_a, w_g], tb=True, out_dtypes=[bf, bf, _F32, bf, bf],
                                      name="f_in")
    wc = fetch('conv', xbc)
    ssd_w = _pad_rows(wc['ssd_conv_w'], SUBLANES)
    cf_w = _pad_rows(wc['cf_conv_w'], 32)
    xbc_a = _conv_fwd([xbc], ssd_w, w['ssd_conv_b'], SSD_CONV, act=True, out_dtype=bf, name="f_ssd_conv")
    y_n, states = _ssd_fwd(xbc_a, dtr, dt_bias, a_log, z, d_exp, w['ssd_norm_g'], name="f_ssd")
    u_c = _conv_fwd([ga, gg], cf_w, w['cf_conv_b'], CF_CONV, glu=True, out_dtype=bf, name="f_cf_conv")
    u = _row_fwd(_f_ln, [u_c], [w['cf_ln_g'], w['cf_ln_b']], [(d, bf)], name="f_cf_ln")
    wm = fetch('mid', y_n)
    w_out_y, w_out_u = (wm['w_out'], 0, SSD_WIDTH), (wm['w_out'], SSD_WIDTH, wm['w_out'].shape[0])
    x1, hq = _mm_fan_in([(y_n, w_out_y), (u, w_out_u)], add=x, out_dtype=bf, epilogue=(_f_rms, w['norm_xattn_g'], bf),
                        name="f_out")
    q = _mm(hq, wm['w_q'], out_dtype=bf, name="f_q")
    memn = _row_fwd(_f_rms, [mem], [w['norm_mem_g']], [(d, bf)], name="f_norm_mem")
    kv = _mm(memn, wm['w_kv'], name="f_kv")
    k_mat, v_mat = kv[:, :d], kv[:, d:]
    o = _row_fwd(_f_att, [q], [k_mat, v_mat], [(d, bf)], ts=512, name="f_att")
    x2, hf = _mm_fan_in([(o, wm['w_o'])], add=x1, out_dtype=bf, epilogue=(_f_rms, w['norm_ffn_g'], bf), name="f_o")
    wf = fetch('ffn', hf)
    gate, up, act = _mm_fan_out(hf, [wf['w_gate'], wf['w_up']], tb=True, out_dtypes=[bf, bf], epilogue=_f_swiglu,
                                extra_outs=[(D_FF, bf)], tm=256, name="f_ffn_in")
    x3 = _mm(act, wf['w_down'], add=x2, out_dtype=bf, name="f_down")

    dx3, dg_final, loss = _loss_bwd(x3, target, g_final, dx_dtype=bf, name="b_loss")
    g = {'norm_final_g': dg_final.reshape(d)}

    dact = _mm(dx3, wf['w_down'], tb=True, out_dtype=bf, name="b_down_x")
    dw_down = _mm(act, dx3, ta=True, out_dtype=bf, name="b_down_w")
    def swiglu_bwd(gate_t, up_t, dact_t):
        return jax.vjp(_f_swiglu, gate_t, up_t)[1](dact_t)

    dhf, dgate, dup = _mm_fan_in([(None, wf['w_gate']), (None, wf['w_up'])], prologue=swiglu_bwd, pro_ins=[gate, up, dact],
                                 pro_out_dtypes=[bf, bf], out_dtype=bf, tm=256, name="b_ffn_in_x")
    sent = emit({'w_down': dw_down, 'w_gate': _mm(dgate, hf, ta=True, out_dtype=bf, name="b_gate_w"),
                 'w_up': _mm(dup, hf, ta=True, out_dtype=bf, name="b_up_w")})
    (dx2,), (g['norm_ffn_g'],) = _row_bwd(_f_rms, [x2], [_tie(w['norm_ffn_g'], sent)], [dhf], adds={0: dx3}, row_dtypes=[bf], name="b_norm_ffn")

    do = _mm(dx2, wm['w_o'], tb=True, out_dtype=bf, name="b_o_x")
    dw_o = _mm(o, dx2, ta=True, out_dtype=bf, name="b_o_w")
    (dq,), (dk, dv) = _row_bwd(_f_att, [q], [k_mat, v_mat], [do], row_dtypes=[bf], ts=512, name="b_att")
    dw_q = _mm(hq, dq, ta=True, out_dtype=bf, name="b_q_w")
    dhq = _mm(dq, wm['w_q'], tb=True, out_dtype=bf, name="b_q_x")
    (dx1,), (g['norm_xattn_g'],) = _row_bwd(_f_rms, [x1], [w['norm_xattn_g']], [dhq], adds={0: dx2}, row_dtypes=[bf], name="b_norm_xattn")
    dkv = jnp.concatenate([dk, dv], axis=1)
    dmemn = _mm(dkv, wm['w_kv'], tb=True, name="b_kv_x")
    _, (g['norm_mem_g'],) = _row_bwd(_f_rms, [mem], [w['norm_mem_g']], [dmemn], need=[False], name="b_norm_mem")
    sent = emit({'w_o': dw_o, 'w_q': dw_q, 'w_kv': _mm_tn_blocked(memn, dkv, N_DEV, out_dtype=bf, name="b_kv_w")},
                after=g['norm_mem_g'])

    dyn, du = _mm_fan_out(dx1, [w_out_y, w_out_u], tb=True, out_dtypes=[bf, bf], name="b_out_x")
    (du_c,), (g['cf_ln_g'], g['cf_ln_b']) = _row_bwd(_f_ln, [u_c], [_tie(w['cf_ln_g'], sent), w['cf_ln_b']], [du], row_dtypes=[bf], name="b_cf_ln")
    sent = emit({'w_out': jnp.concatenate([_mm(y_n, dx1, ta=True, out_dtype=bf, name="b_out_y_w"), _mm(u, dx1, ta=True, out_dtype=bf, name="b_out_u_w")], axis=0)})
    (dga, dgg), dcf_w, g['cf_conv_b'] = _conv_bwd([ga, gg], cf_w, w['cf_conv_b'], du_c, CF_CONV, glu=True, name="b_cf_conv")
    g['cf_conv_w'] = dcf_w[:CF_CONV]
    dxbc_a, ddtr, ddt_bias, da_log, dz, dd_exp, g['ssd_norm_g'] = _ssd_bwd(
        xbc_a, dtr, states, dt_bias, a_log, z, d_exp, _tie(w['ssd_norm_g'], sent), dyn, name="b_ssd")
    g['ssd_D'] = jnp.sum(dd_exp.reshape(SSD_HEADS, HEAD_DIM), axis=1).reshape(1, SSD_HEADS)
    g['ssd_dt_bias'] = ddt_bias[:, :SSD_HEADS]
    g['ssd_A_log'] = da_log[:, :SSD_HEADS]
    (dxbc,), dssd_w, g['ssd_conv_b'] = _conv_bwd([xbc], ssd_w, w['ssd_conv_b'], dxbc_a, SSD_CONV, act=True, name="b_ssd_conv")
    g['ssd_conv_w'] = dssd_w[:SSD_CONV]

    sent = emit({'w_in': jnp.concatenate([
        _mm(dz, h, ta=True, out_dtype=bf, name="b_in_z_w"), _mm(dxbc, h, ta=True, out_dtype=bf, name="b_in_xbc_w"),
        _mm(ddtr, h, ta=True, out_dtype=bf, name="b_in_dt_w")[:SSD_HEADS],
        _mm(dga, h, ta=True, out_dtype=bf, name="b_in_a_w"), _mm(dgg, h, ta=True, out_dtype=bf, name="b_in_g_w")], axis=0)})
    dh = _mm_fan_in([(dz, w_z), (dxbc, w_xbc), (ddtr, _tie(w_dt, sent)), (dga, w_a), (dgg, w_g)], out_dtype=bf,
                    name="b_in_x")
    (dx,), (g['norm_mix_g'],) = _row_bwd(_f_rms, [x], [w['norm_mix_g']], [dh], adds={0: dx1}, name="b_norm_mix")
    return loss, dx, g


_ANY = pl.BlockSpec(memory_space=pl.ANY)


def _place():
    x, y, c = lax.axis_index("x"), lax.axis_index("y"), lax.axis_index("c")
    return x, y, c


_HBM =pl.BlockSpec(memory_space=pltpu.HBM)
_SEM = pl.BlockSpec(memory_space=pltpu.SEMAPHORE)
_EFFECT = pltpu.SideEffectType.DATAFLOW_SIDE_EFFECTING
_FLIPS = [(dx, dy, dc) for dx in (0, 1) for dy in (0, 1) for dc in (0, 1)][1:]


def _peer(flip, x, y, c):
    return (1 - x if flip[0] else x, 1 - y if flip[1] else y, 1 - c if flip[2] else c)


_CHIP_FLIPS = [f for f in _FLIPS if f[2] == 0]


def _send_start(srcs, blocked, *, flips=None, after=None, name):
    n = len(srcs)
    flips = _FLIPS if flips is None else flips
    nf = len(flips)
    lands = [jax.ShapeDtypeStruct(s.shape if blocked else (N_DEV,) + s.shape, s.dtype) for s in srcs]
    n_in = 2 * n + (after is not None)

    def body(*refs):
        src_refs, land_refs = refs[:n], refs[n:2 * n]
        send_sems, recv_sems = refs[n_in], refs[n_in + 1]
        token = refs[-1]
        x, y, c = _place()
        me = 4 * x + 2 * y + c
        for a in range(n):
            for k, flip in enumerate(flips):
                p = _peer(flip, x, y, c)
                src = src_refs[a].at[4 * p[0] + 2 * p[1] + p[2]] if blocked else src_refs[a]
                pltpu.make_async_remote_copy(
                    src_ref=src, dst_ref=land_refs[a].at[me], send_sem=send_sems.at[nf * a + k],
                    recv_sem=recv_sems.at[nf * a + k], device_id=p, device_id_type=MESH).start()
        token[...] = jnp.zeros_like(token)

    res = pl.pallas_call(
        body, name=name,
        out_shape=(pltpu.SemaphoreType.DMA((nf * n,)), pltpu.SemaphoreType.DMA((nf * n,)),
                   *[pltpu.HBM(s.shape, s.dtype) for s in srcs], *[pltpu.HBM(l.shape, l.dtype) for l in lands],
                   jax.ShapeDtypeStruct((SUBLANES, LANES), _F32)),
        in_specs=[_HBM] * (2 * n) + [_ANY] * (after is not None),
        out_specs=(_SEM, _SEM, *[_HBM] * (2 * n), pl.BlockSpec(memory_space=pltpu.VMEM)),
        input_output_aliases={i: 2 + i for i in range(2 * n)},
        compiler_params=pltpu.CompilerParams(has_side_effects=_EFFECT),
    )(*[pltpu.with_memory_space_constraint(s, pltpu.HBM) for s in srcs],
      *[pltpu.with_memory_space_constraint(lax.empty(l.shape, l.dtype), pltpu.HBM) for l in lands],
      *([after] if after is not None else []))
    return res[0], res[1], list(res[2:2 + n]), list(res[2 + n:2 + 2 * n]), res[-1]


def _send_wait(handles, after, blocked, *, flips=None, name):
    send_sems, recv_sems, srcs, lands, _ = handles
    n = len(srcs)
    flips = _FLIPS if flips is None else flips
    nf = len(flips)

    def body(*refs):
        src_refs, land_refs = refs[:n], refs[n:2 * n]
        send_sems, recv_sems = refs[2 * n], refs[2 * n + 1]
        x, y, c = _place()
        for a in range(n):
            for k, flip in enumerate(flips):
                p = _peer(flip, x, y, c)
                pid = 4 * p[0] + 2 * p[1] + p[2]
                cp = pltpu.make_async_remote_copy(
                    src_ref=src_refs[a].at[pid] if blocked else src_refs[a], dst_ref=land_refs[a].at[pid],
                    send_sem=send_sems.at[nf * a + k], recv_sem=recv_sems.at[nf * a + k], device_id=p, device_id_type=MESH)
                cp.wait_send()
                cp.wait_recv()

    res = pl.pallas_call(
        body, name=name,
        out_shape=tuple(pltpu.HBM(s.shape, s.dtype) for s in srcs + lands),
        in_specs=[_HBM] * (2 * n) + [_SEM, _SEM, _ANY], out_specs=tuple([_HBM] * (2 * n)),
        input_output_aliases={i: i for i in range(2 * n)},
        compiler_params=pltpu.CompilerParams(has_side_effects=_EFFECT),
    )(*srcs, *lands, send_sems, recv_sems, after)
    return list(res[:n]), list(res[n:])


def _sibling_share(lands, owns, *, name):
    n = len(lands)

    def body(*refs):
        own_refs, land_refs = refs[n:2 * n], refs[2 * n:3 * n]
        send_sems, recv_sems = refs[3 * n], refs[3 * n + 1]
        x, y, c = _place()
        sibling = (x, y, 1 - c)
        chips = [(1 - x, y), (x, 1 - y), (1 - x, 1 - y)]

        def copy(a, k, block, src=None):
            slot = land_refs[a].at[block]
            return pltpu.make_async_remote_copy(
                src_ref=slot if src is None else src, dst_ref=slot, send_sem=send_sems.at[4 * a + k],
                recv_sem=recv_sems.at[4 * a + k], device_id=sibling, device_id_type=MESH)

        sends = []
        for a in range(n):
            sends.append(copy(a, 0, 4 * x + 2 * y + c, src=own_refs[a]))
            sends += [copy(a, 1 + j, 4 * cx + 2 * cy + c) for j, (cx, cy) in enumerate(chips)]
        for cp in sends:
            cp.start()
        for a in range(n):
            copy(a, 0, 4 * x + 2 * y + (1 - c)).wait_recv()
            for j, (cx, cy) in enumerate(chips):
                copy(a, 1 + j, 4 * cx + 2 * cy + (1 - c)).wait_recv()
        for cp in sends:
            cp.wait_send()

    return pl.pallas_call(
        body, name=name, in_specs=[_ANY] * (2 * n), out_specs=[_ANY] * n,
        out_shape=[jax.ShapeDtypeStruct(l.shape, l.dtype) for l in lands],
        input_output_aliases={i: i for i in range(n)},
        scratch_shapes=[pltpu.SemaphoreType.DMA((4 * n,)), pltpu.SemaphoreType.DMA((4 * n,))],
    )(*lands, *owns)


def _adamw(parts, w, m, v, *, own=None, me=None, name):
    p, r, c = parts.shape
    tr = _pick(r, (256, 176, 128, 64, 32, 16, 8))
    if own is not None:
        tc = c if tr < r else _pick(c, (256, 128))
        return _adamw_own(parts, own, me, w, m, v, tr, tc, name=name)

    def body(p_ref, w_ref, m_ref, v_ref, g_ref, d_ref, nm_ref, nv_ref):
        g = p_ref[0].astype(_F32)
        for i in range(1, p):
            g = g + p_ref[i].astype(_F32)
        _adamw_math(g, w_ref, m_ref, v_ref, g_ref, d_ref, nm_ref, nv_ref)

    blk = pl.BlockSpec((tr, c), lambda i: (i, 0))
    return pl.pallas_call(
        body, name=name, grid=(r // tr,),
        in_specs=[pl.BlockSpec((p, tr, c), lambda i: (0, i, 0)), blk, blk, blk], out_specs=[blk] * 4,
        out_shape=[jax.ShapeDtypeStruct((r, c), _F32)] * 4,
        compiler_params=_params(("parallel",)),
    )(parts, w, m, v)


def _adamw_math(g, w_ref, m_ref, v_ref, g_ref, d_ref, nm_ref, nv_ref):
    wv = w_ref[...]
    mn = ADAM_B1 * m_ref[...] + (1.0 - ADAM_B1) * g
    vn = ADAM_B2 * v_ref[...] + (1.0 - ADAM_B2) * jnp.square(g)
    m_hat = mn / (1.0 - ADAM_B1 ** ADAM_STEP)
    v_hat = vn / (1.0 - ADAM_B2 ** ADAM_STEP)
    g_ref[...] = g
    d_ref[...] = -ADAM_LR * (m_hat / (jnp.sqrt(v_hat) + ADAM_EPS) + ADAM_WD * wv)
    nm_ref[...] = mn
    nv_ref[...] = vn


def _adamw_own(parts, own, me, w, m, v, tr, tc, *, name):
    p, r, c = parts.shape

    def body(me_ref, p_ref, own_ref, w_ref, m_ref, v_ref, g_ref, d_ref, nm_ref, nv_ref):
        mine = own_ref[...].astype(_F32)
        g = jnp.where(me_ref[0] == 0, mine, p_ref[0].astype(_F32))
        for i in range(1, p):
            g = g + jnp.where(me_ref[0] == i, mine, p_ref[i].astype(_F32))
        _adamw_math(g, w_ref, m_ref, v_ref, g_ref, d_ref, nm_ref, nv_ref)

    blk = pl.BlockSpec((tr, tc), lambda i, j, me_ref: (i, j))
    grid_spec = pltpu.PrefetchScalarGridSpec(
        num_scalar_prefetch=1, grid=(r // tr, c // tc),
        in_specs=[pl.BlockSpec((p, tr, tc), lambda i, j, me_ref: (0, i, j)),
                  pl.BlockSpec((None, tr, tc), lambda i, j, me_ref: (me_ref[0], i, j)), blk, blk, blk],
        out_specs=[blk] * 4)
    return pl.pallas_call(
        body, name=name, grid_spec=grid_spec, out_shape=[jax.ShapeDtypeStruct((r, c), _F32)] * 4,
        compiler_params=_params(("parallel", "parallel")),
    )(me.reshape(1).astype(jnp.int32), parts, own, w, m, v)


def _adamw_rows(g_row, offsets, ws, ms, vs, *, name):
    k = len(ws)

    def body(*refs):
        g_ref, w_refs, m_refs, v_refs = refs[0], refs[1:1 + k], refs[1 + k:1 + 2 * k], refs[1 + 2 * k:1 + 3 * k]
        outs = refs[1 + 3 * k:]
        for i in range(k):
            gi = g_ref[:, offsets[i]:offsets[i] + ws[i].shape[1]]
            _adamw_math(gi, w_refs[i], m_refs[i], v_refs[i], *outs[4 * i:4 * i + 4])

    return pl.pallas_call(
        body, name=name, out_shape=[jax.ShapeDtypeStruct(w.shape, _F32) for w in ws for _ in range(4)],
    )(g_row, *ws, *ms, *vs)


def _sum_parts(parts, *, name):
    p, r, c = parts.shape

    def body(p_ref, o_ref):
        g = p_ref[0].astype(_F32)
        for i in range(1, p):
            g = g + p_ref[i].astype(_F32)
        o_ref[...] = g

    return pl.pallas_call(body, name=name, out_shape=jax.ShapeDtypeStruct((r, c), _F32))(parts)


def kernel(x, mem, norm_mix_g, w_in, ssd_conv_w, ssd_conv_b, ssd_dt_bias, ssd_A_log, ssd_D, ssd_norm_g, cf_conv_w, cf_conv_b, cf_ln_g, cf_ln_b, w_out, norm_xattn_g, norm_mem_g, w_q, w_kv, w_o, norm_ffn_g, w_gate, w_up, w_down, norm_final_g, loss_target, m_norm_mix_g, m_w_in, m_ssd_conv_w, m_ssd_conv_b, m_ssd_dt_bias, m_ssd_A_log, m_ssd_D, m_ssd_norm_g, m_cf_conv_w, m_cf_conv_b, m_cf_ln_g, m_cf_ln_b, m_w_out, m_norm_xattn_g, m_norm_mem_g, m_w_q, m_w_kv, m_w_o, m_norm_ffn_g, m_w_gate, m_w_up, m_w_down, m_norm_final_g, v_norm_mix_g, v_w_in, v_ssd_conv_w, v_ssd_conv_b, v_ssd_dt_bias, v_ssd_A_log, v_ssd_D, v_ssd_norm_g, v_cf_conv_w, v_cf_conv_b, v_cf_ln_g, v_cf_ln_b, v_w_out, v_norm_xattn_g, v_norm_mem_g, v_w_q, v_w_kv, v_w_o, v_norm_ffn_g, v_w_gate, v_w_up, v_w_down, v_norm_final_g):
    args = dict(locals())
    wts = {n: args[n] for n in WEIGHT_NAMES}
    mom = {n: args["m_" + n] for n in WEIGHT_NAMES}
    var = {n: args["v_" + n] for n in WEIGHT_NAMES}
    me = 4 * lax.axis_index("x") + 2 * lax.axis_index("y") + lax.axis_index("c")

    groups = {'in': ['w_in'], 'conv': ['ssd_conv_w', 'cf_conv_w'], 'mid': ['w_out', 'w_q', 'w_kv', 'w_o'],
              'ffn': ['w_gate', 'w_up', 'w_down']}
    def shard(n, a):
        return jnp.transpose(a[0], (1, 0)) if n in TRANSPOSED else a[0]

    two_level = {'in': _CHIP_FLIPS}
    gathers, started = {}, None
    for grp, names in groups.items():
        shards = [wts[n][0] if grp == 'conv' else shard(n, wts[n]).astype(_MXU) for n in names]
        gathers[grp] = _send_start(shards, False, flips=two_level.get(grp), after=started, name="gather_%s_start" % grp)
        started = gathers[grp][4]

    def fetch(grp, after):
        srcs, lands = _send_wait(gathers[grp], started if after is None else after, False, flips=two_level.get(grp),
                                 name="gather_%s_wait" % grp)
        if grp in two_level:
            lands = _sibling_share(lands, srcs, name="gather_%s_share" % grp)
        out = {}
        for n, own, gth in zip(groups[grp], srcs, lands):
            gth = lax.dynamic_update_slice_in_dim(gth, own[None], me, axis=0)
            if n == 'w_kv' or grp == 'conv':
                out[n] = jnp.transpose(gth, (1, 0, 2)).reshape(gth.shape[1], N_DEV * gth.shape[2])
            else:
                out[n] = gth.reshape(N_DEV * gth.shape[1], gth.shape[2])
        return out

    exchanges = []

    def emit(grads, after=None):
        blocks = []
        for n, gw in grads.items():
            if gw.ndim == 2:
                gw = gw.reshape(N_DEV, gw.shape[0] // N_DEV, gw.shape[1])
            blocks.append(gw.astype(jnp.bfloat16))
        first = next(iter(grads))
        exchanges.append((list(grads), _send_start(blocks, True, after=after, name="exchange_%s_start" % first), first))
        return exchanges[-1][1][4]

    full = {n: wts[n] for n in WEIGHT_NAMES if n not in BIG and n not in groups['conv']}
    full['norm_mix_g'] = _tie(norm_mix_g, started)

    loss_blk, grad_x, g = _local_step(x[0], mem[0], loss_target[0], full, fetch, emit)

    small = [n for n in WEIGHT_NAMES if n not in BIG]
    g['loss'] = loss_blk[0:1, 0:1]
    items = small + ['loss']
    size = {n: math.prod(g[n].shape) for n in items}
    seg = {n: -(-size[n] // LANES) * LANES for n in items}
    off, pos = {}, 0
    for n in items:
        off[n], pos = pos, pos + seg[n]
    rows = -(-pos // (LANES * SUBLANES)) * SUBLANES
    flat = jnp.concatenate([jnp.pad(g[n].reshape(-1), (0, seg[n] - size[n])) for n in items]
                           + [jnp.zeros((rows * LANES - pos,), _F32)])
    small_sent = _send_start([flat.reshape(rows, LANES)], False, name="gather_small_start")

    out_g, out_d, out_m, out_v = {}, {}, {}, {}
    done = small_sent[4]
    for names, handles, first in exchanges:
        srcs, lands = _send_wait(handles, done, True, name="exchange_%s_wait" % first)
        for n, own, parts in zip(names, srcs, lands):
            res = _adamw(parts, shard(n, wts[n]), shard(n, mom[n]), shard(n, var[n]), own=own, me=me, name="adamw_" + n)
            out_g[n], out_d[n], out_m[n], out_v[n] = [(jnp.transpose(r, (1, 0)) if n in TRANSPOSED else r)[None] for r in res]
            done = res[0]

    srcs, lands = _send_wait(small_sent, out_g[exchanges[-1][0][-1]], False, name="gather_small_wait")
    small_parts = lax.dynamic_update_slice_in_dim(lands[0], srcs[0][None], me, axis=0)
    g_row = _sum_parts(small_parts, name="sum_small_grads").reshape(1, rows * LANES)
    loss = g_row[0, off['loss']]
    rep = [n for n in small if n not in groups['conv']]
    as_row = lambda a: a.reshape(1, -1)
    res = _adamw_rows(g_row, [off[n] for n in rep], [as_row(wts[n]) for n in rep], [as_row(mom[n]) for n in rep],
                      [as_row(var[n]) for n in rep], name="adamw_small")
    for i, n in enumerate(rep):
        out_g[n], out_d[n], out_m[n], out_v[n] = [r.reshape(wts[n].shape) for r in res[4 * i:4 * i + 4]]
    for n in groups['conv']:
        k_taps, width = g[n].shape
        g_full = g_row[0, off[n]:off[n] + size[n]].reshape(k_taps, width)
        g_mine = lax.dynamic_slice_in_dim(g_full, me * (width // N_DEV), width // N_DEV, axis=1)
        res = _adamw(g_mine[None], wts[n][0], mom[n][0], var[n][0], name="adamw_" + n)
        out_g[n], out_d[n], out_m[n], out_v[n] = [r[None] for r in res]

    return (loss, grad_x[None], *[out_g[n] for n in WEIGHT_NAMES], *[out_d[n] for n in WEIGHT_NAMES],
            *[out_m[n] for n in WEIGHT_NAMES], *[out_v[n] for n in WEIGHT_NAMES])
```

```python
import functools
import math

import jax
import jax.numpy as jnp
from jax import lax
from jax.experimental import pallas as pl
from jax.experimental.pallas import tpu as pltpu

_F32 = jnp.float32
_MXU = jnp.bfloat16
_PREC = None
_VMEM_LIMIT = 56 * 1024 * 1024

D_MODEL = 1024
HEAD_DIM = 64
SSD_HEADS = 16
SSD_WIDTH = 1024
SSD_STATE = 128
SSD_CONV = 4
CHUNK = 128
XBC_WIDTH = 1536
CF_WIDTH = 1024
CF_CONV = 31
X_HEADS = 4
X_HEAD_DIM = 256
D_FF = 2816
EPS = 1e-6
N_DEV = 8
LANES = 128
SUBLANES = 8

ADAM_LR = 0.001
ADAM_B1 = 0.9
ADAM_B2 = 0.999
ADAM_EPS = 1e-08
ADAM_WD = 0.01
ADAM_STEP = 10

MESH = pl.DeviceIdType.MESH
WEIGHT_NAMES = ['norm_mix_g', 'w_in', 'ssd_conv_w', 'ssd_conv_b', 'ssd_dt_bias', 'ssd_A_log', 'ssd_D', 'ssd_norm_g',
                'cf_conv_w', 'cf_conv_b', 'cf_ln_g', 'cf_ln_b', 'w_out', 'norm_xattn_g', 'norm_mem_g', 'w_q', 'w_kv',
                'w_o', 'norm_ffn_g', 'w_gate', 'w_up', 'w_down', 'norm_final_g']
BIG = ['w_in', 'w_out', 'w_q', 'w_kv', 'w_o', 'w_gate', 'w_up', 'w_down']
TRANSPOSED = ('w_in', 'w_gate', 'w_up')


def _params(sem=None):
    return pltpu.CompilerParams(dimension_semantics=sem, vmem_limit_bytes=_VMEM_LIMIT)


def _pick(n, cands):
    for c in cands:
        if n % c == 0:
            return c
    return n


def _mm(a, b, *, ta=False, tb=False, add=None, out_dtype=_F32, name):
    (kdim, m) = a.shape if ta else a.shape[::-1]
    (n, k2) = b.shape if tb else b.shape[::-1]
    assert kdim == k2, (a.shape, b.shape, ta, tb)
    if ta:
        tm = m if m <= 1024 else _pick(m, (1408, 1024, 512, 256, 128))
        tn = n if n <= 1536 else _pick(n, (1408, 1024, 512, 256, 128))
        size = lambda arr: jnp.dtype(arr.dtype).itemsize
        fits = lambda t: (2 * t * (tm * size(a) + tn * size(b)) + tm * tn * (4 + 2 * jnp.dtype(out_dtype).itemsize)
                          <= _VMEM_LIMIT * 3 // 4)
        tk = next((t for t in (2048, 1024, 512, 256, 128) if kdim % t == 0 and fits(t)), _pick(kdim, (128,)))
    else:
        tm = _pick(m, (512, 256, 128))
        tn = n if n <= 2816 else _pick(n, (1408, 1024, 512, 256, 128))
        tk = kdim if kdim <= 2816 else _pick(kdim, (1408, 1024, 512, 256, 128))
    nk = kdim // tk
    dn = (((0 if ta else 1,), (1 if tb else 0,)), ((), ()))

    def body(*refs):
        a_ref, b_ref = refs[0], refs[1]
        add_ref = refs[2] if add is not None else None
        o_ref = refs[3 if add is not None else 2]
        acc_ref = refs[-1]
        k = pl.program_id(2)
        prod = lax.dot_general(a_ref[...].astype(_MXU), b_ref[...].astype(_MXU), dn,
                               preferred_element_type=_F32, precision=_PREC)

        def finish(r):
            if add_ref is not None:
                r = r + add_ref[...].astype(_F32)
            o_ref[...] = r.astype(o_ref.dtype)

        if nk == 1:
            finish(prod)
            return

        @pl.when(k == 0)
        def _():
            acc_ref[...] = prod

        @pl.when(jnp.logical_and(k > 0, k < nk - 1))
        def _():
            acc_ref[...] += prod

        @pl.when(k == nk - 1)
        def _():
            finish(acc_ref[...] + prod)

    a_spec = pl.BlockSpec((tk, tm), lambda i, j, k: (k, i)) if ta else pl.BlockSpec((tm, tk), lambda i, j, k: (i, k))
    b_spec = pl.BlockSpec((tn, tk), lambda i, j, k: (j, k)) if tb else pl.BlockSpec((tk, tn), lambda i, j, k: (k, j))
    o_spec = pl.BlockSpec((tm, tn), lambda i, j, k: (i, j))
    ins, specs = [a, b], [a_spec, b_spec]
    if add is not None:
        ins.append(add)
        specs.append(o_spec)
    return pl.pallas_call(
        body, name=name, grid=(m // tm, n // tn, nk), in_specs=specs, out_specs=o_spec,
        out_shape=jax.ShapeDtypeStruct((m, n), out_dtype),
        scratch_shapes=[pltpu.VMEM((tm, tn), _F32)] if nk > 1 else [],
        compiler_params=_params(("parallel", "parallel", "arbitrary")),
    )(*ins)


def _mm_tn_fan(a_list, b, *, out_dtype, name, tk=1024):
    kdim, n = b.shape
    tk = min(tk, kdim)
    nk = kdim // tk
    na = len(a_list)

    def body(*refs):
        a_refs, b_ref, o_refs, acc_refs = refs[:na], refs[na], refs[na + 1:2 * na + 1], refs[2 * na + 1:]
        k = pl.program_id(0)
        bv = b_ref[...].astype(_MXU)
        for a_ref, o_ref, acc_ref in zip(a_refs, o_refs, acc_refs):
            prod = lax.dot_general(a_ref[...].astype(_MXU), bv, _DN["tn"], preferred_element_type=_F32, precision=_PREC)

            @pl.when(k == 0)
            def _(acc_ref=acc_ref, prod=prod):
                acc_ref[...] = prod

            @pl.when(k > 0)
            def _(acc_ref=acc_ref, prod=prod):
                acc_ref[...] += prod

            @pl.when(k == nk - 1)
            def _(acc_ref=acc_ref, o_ref=o_ref):
                o_ref[...] = acc_ref[...].astype(o_ref.dtype)

    return pl.pallas_call(
        body, name=name, grid=(nk,),
        in_specs=[pl.BlockSpec((tk, a.shape[1]), lambda k: (k, 0)) for a in a_list] + [pl.BlockSpec((tk, n), lambda k: (k, 0))],
        out_specs=[pl.BlockSpec((a.shape[1], n), lambda k: (0, 0)) for a in a_list],
        out_shape=[jax.ShapeDtypeStruct((a.shape[1], n), out_dtype) for a in a_list],
        scratch_shapes=[pltpu.VMEM((a.shape[1], n), _F32) for a in a_list],
        compiler_params=_params(("arbitrary",)),
    )(*a_list, b)


def _mm_tn_blocked(a, b, nblk, *, out_dtype, name):
    kdim, m = a.shape
    w = b.shape[1] // nblk

    def body(a_ref, b_ref, o_ref):
        o_ref[...] = lax.dot_general(a_ref[...].astype(_MXU), b_ref[...].astype(_MXU), _DN["tn"],
                                     preferred_element_type=_F32, precision=_PREC).astype(o_ref.dtype)

    return pl.pallas_call(
        body, name=name, grid=(nblk,),
        in_specs=[pl.BlockSpec((kdim, m), lambda j: (0, 0)), pl.BlockSpec((kdim, w), lambda j: (0, j))],
        out_specs=pl.BlockSpec((None, m, w), lambda j: (j, 0, 0)),
        out_shape=jax.ShapeDtypeStruct((nblk, m, w), out_dtype),
        compiler_params=_params(("parallel",)),
    )(a, b)


def _resident(shape):
    return pl.BlockSpec(shape, lambda i: (0,) * len(shape), pipeline_mode=pl.Buffered(1))


def _windows(bs):
    arrays, wins = [], []
    for b in bs:
        arr, lo, hi = b if isinstance(b, tuple) else (b, 0, b.shape[0])
        idx = next((i for i, x in enumerate(arrays) if x is arr), None)
        if idx is None:
            arrays.append(arr)
            idx = len(arrays) - 1
        wins.append((idx, lo, hi))
    return arrays, wins


def _mm_fan_out(a, bs, *, tb, out_dtypes, epilogue=None, extra_outs=(), tm=512, name):
    m, kdim = a.shape
    tm = min(tm, m)
    bs, wins = _windows(bs)
    ns = [hi - lo if tb else bs[i].shape[1] for i, lo, hi in wins]
    nb = len(bs)
    kind = "nt" if tb else "nn"

    def body(*refs):
        a_ref, b_refs, o_refs = refs[0], refs[1:1 + nb], refs[1 + nb:]
        av = a_ref[...].astype(_MXU)
        prods = [lax.dot_general(av, b_refs[i][lo:hi, :].astype(_MXU), _DN[kind], preferred_element_type=_F32,
                                 precision=_PREC) for i, lo, hi in wins]
        for o_ref, p in zip(o_refs[:len(wins)], prods):
            o_ref[...] = p.astype(o_ref.dtype)
        if epilogue is not None:
            for o_ref, v in zip(o_refs[len(wins):], _tup(epilogue(*prods))):
                o_ref[...] = v.astype(o_ref.dtype)

    widths = ns + [w for w, _ in extra_outs]
    dtypes = list(out_dtypes) + [dt for _, dt in extra_outs]
    return pl.pallas_call(
        body, name=name, grid=(m // tm,),
        in_specs=[pl.BlockSpec((tm, kdim), lambda i: (i, 0))] + [_resident(b.shape) for b in bs],
        out_specs=[pl.BlockSpec((tm, w), lambda i: (i, 0)) for w in widths],
        out_shape=[jax.ShapeDtypeStruct((m, w), dt) for w, dt in zip(widths, dtypes)],
        compiler_params=_params(("parallel",)),
    )(a, *bs)


def _mm_fan_in(pairs, *, add=None, out_dtype=_F32, prologue=None, pro_ins=(), pro_out_dtypes=(), epilogue=None,
               tm=512, name):
    bs, wins = _windows([b for _, b in pairs])
    nb = len(bs)
    n = bs[0].shape[1]
    rows_in = list(pro_ins) if prologue is not None else [a for a, _ in pairs]
    m = rows_in[0].shape[0]
    tm = min(tm, m)
    n_r = len(rows_in)

    def body(*refs):
        r_refs, b_refs = refs[:n_r], refs[n_r:n_r + nb]
        pos = n_r + nb
        add_ref = refs[pos] if add is not None else None
        pos += add is not None
        epi_ref = refs[pos] if epilogue is not None else None
        pos += epilogue is not None
        o_ref, po_refs = refs[pos], refs[pos + 1:]
        if prologue is not None:
            a_vals = _tup(prologue(*[r[...].astype(_F32) for r in r_refs]))
            for po, v in zip(po_refs, a_vals):
                po[...] = v.astype(po.dtype)
        else:
            a_vals = [r[...] for r in r_refs]
        acc = None
        for av, (i, lo, hi) in zip(a_vals, wins):
            p = lax.dot_general(av.astype(_MXU), b_refs[i][lo:hi, :].astype(_MXU), _DN["nn"], preferred_element_type=_F32,
                                precision=_PREC)
            acc = p if acc is None else acc + p
        if add_ref is not None:
            acc = acc + add_ref[...].astype(_F32)
        o_ref[...] = acc.astype(o_ref.dtype)
        if epilogue is not None:
            po_refs[-1][...] = epilogue[0](acc, epi_ref[...]).astype(po_refs[-1].dtype)

    row = lambda w: pl.BlockSpec((tm, w), lambda i: (i, 0))
    ins = rows_in + bs + ([add] if add is not None else []) + ([epilogue[1]] if epilogue is not None else [])
    in_specs = ([row(r.shape[1]) for r in rows_in] + [_resident(b.shape) for b in bs]
                + ([row(n)] if add is not None else []) + ([_resident(epilogue[1].shape)] if epilogue is not None else []))
    extra = [(hi - lo, dt) for (_, lo, hi), dt in zip(wins, pro_out_dtypes)] if prologue is not None else []
    if epilogue is not None:
        extra.append((n, epilogue[2]))
    res = pl.pallas_call(
        body, name=name, grid=(m // tm,), in_specs=in_specs,
        out_specs=[row(n)] + [row(w) for w, _ in extra],
        out_shape=[jax.ShapeDtypeStruct((m, n), out_dtype)] + [jax.ShapeDtypeStruct((m, w), dt) for w, dt in extra],
        compiler_params=_params(("parallel",)),
    )(*ins)
    return res if extra else res[0]


def _row_spec(r, ts):
    if isinstance(r, tuple):
        arr, width, cblk = r
        return arr, pl.BlockSpec((ts, width), lambda i, cblk=cblk: (i, cblk))
    return r, pl.BlockSpec((ts, r.shape[1]), lambda i: (i, 0))


def _tup(v):
    return tuple(v) if isinstance(v, (tuple, list)) else (v,)


def _row_fwd(f, rows, params, outs, *, name, ts=512):
    s = (rows[0][0] if isinstance(rows[0], tuple) else rows[0]).shape[0]
    ts = min(ts, s)
    arrs, specs = zip(*[_row_spec(r, ts) for r in rows])
    n_r, n_p = len(rows), len(params)

    def body(*refs):
        rv = [r[...].astype(_F32) for r in refs[:n_r]]
        pv = [p[...] for p in refs[n_r:n_r + n_p]]
        res = _tup(f(*rv, *pv))
        for o_ref, v in zip(refs[n_r + n_p:], res):
            o_ref[...] = v.astype(o_ref.dtype)

    res = pl.pallas_call(
        body, name=name, grid=(s // ts,),
        in_specs=list(specs) + [pl.BlockSpec(p.shape, lambda i: (0, 0)) for p in params],
        out_specs=[pl.BlockSpec((ts, w), lambda i: (i, 0)) for w, _ in outs],
        out_shape=[jax.ShapeDtypeStruct((s, w), dt) for w, dt in outs],
        compiler_params=_params(("parallel",)),
    )(*arrs, *params)
    return res[0] if len(outs) == 1 else res


def _row_bwd(f, rows, params, cts, *, need=None, adds=None, row_dtypes=None, name, ts=512):
    s = (rows[0][0] if isinstance(rows[0], tuple) else rows[0]).shape[0]
    ts = min(ts, s)
    arrs, specs = zip(*[_row_spec(r, ts) for r in rows])
    n_r, n_p, n_c = len(rows), len(params), len(cts)
    need = [True] * n_r if need is None else need
    adds = {} if adds is None else adds
    add_keys = sorted(adds)
    row_dtypes = [_F32] * n_r if row_dtypes is None else row_dtypes
    needed = [j for j in range(n_r) if need[j]]
    widths = [specs[j].block_shape[1] for j in range(n_r)]

    def body(*refs):
        pos = 0
        r_refs = refs[pos:pos + n_r]; pos += n_r
        p_refs = refs[pos:pos + n_p]; pos += n_p
        c_refs = refs[pos:pos + n_c]; pos += n_c
        a_refs = refs[pos:pos + len(add_keys)]; pos += len(add_keys)
        dr_refs = refs[pos:pos + len(needed)]; pos += len(needed)
        dp_refs = refs[pos:pos + n_p]
        rv = [r[...].astype(_F32) for r in r_refs]
        pv = [p[...] for p in p_refs]
        _, vjp = jax.vjp(lambda *a: _tup(f(*a)), *rv, *pv)
        g = vjp(tuple(c[...].astype(_F32) for c in c_refs))
        for o_ref, j in zip(dr_refs, needed):
            v = g[j]
            if j in adds:
                v = v + a_refs[add_keys.index(j)][...].astype(_F32)
            o_ref[...] = v.astype(o_ref.dtype)
        if n_p:
            @pl.when(pl.program_id(0) == 0)
            def _():
                for dp in dp_refs:
                    dp[...] = jnp.zeros_like(dp)
            for dp, v in zip(dp_refs, g[n_r:]):
                dp[...] += v

    ct_specs = [pl.BlockSpec((ts, c.shape[1]), lambda i: (i, 0)) for c in cts]
    add_specs = [pl.BlockSpec((ts, adds[j].shape[1]), lambda i: (i, 0)) for j in add_keys]
    res = pl.pallas_call(
        body, name=name, grid=(s // ts,),
        in_specs=list(specs) + [pl.BlockSpec(p.shape, lambda i: (0, 0)) for p in params] + ct_specs + add_specs,
        out_specs=[pl.BlockSpec((ts, widths[j]), lambda i: (i, 0)) for j in needed]
        + [pl.BlockSpec(p.shape, lambda i: (0, 0)) for p in params],
        out_shape=[jax.ShapeDtypeStruct((s, widths[j]), row_dtypes[j]) for j in needed]
        + [jax.ShapeDtypeStruct(p.shape, _F32) for p in params],
        compiler_params=_params(("arbitrary",)),
    )(*arrs, *params, *cts, *[adds[j] for j in add_keys])
    return list(res[:len(needed)]), list(res[len(needed):])


_DN = {"nn": (((1,), (0,)), ((), ())), "nt": (((1,), (1,)), ((), ())), "tn": (((0,), (0,)), ((), ()))}


def _dot_raw(a, b, kind):
    return lax.dot_general(a.astype(_MXU), b.astype(_MXU), _DN[kind], preferred_element_type=_F32, precision=_PREC)


@functools.partial(jax.custom_vjp, nondiff_argnums=(2,))
def _dot1(a, b, kind):
    return _dot_raw(a, b, kind)


def _dot1_bwd(kind, res, ct):
    a, b = res
    if kind == "nn":
        return _dot_raw(ct, b, "nt"), _dot_raw(a, ct, "tn")
    if kind == "nt":
        return _dot_raw(ct, b, "nn"), _dot_raw(ct, a, "tn")
    return _dot_raw(b, ct, "nt"), _dot_raw(a, ct, "nn")


_dot1.defvjp(lambda a, b, kind: (_dot_raw(a, b, kind), (a, b)), _dot1_bwd)


def _sig(v):
    return 1.0 / (1.0 + jnp.exp(-v))


def _silu(v):
    return v * _sig(v)


def _f_rms(x, g):
    return x * lax.rsqrt(jnp.mean(x * x, axis=-1, keepdims=True) + EPS) * g


def _f_gate(y, xs, z, dexp, g):
    v = (y + dexp * xs) * _silu(z)
    half = SSD_WIDTH // 2
    parts = []
    for grp in range(2):
        vg = v[:, grp * half:(grp + 1) * half]
        parts.append(vg * lax.rsqrt(jnp.mean(vg * vg, axis=-1, keepdims=True) + EPS) * g[:, grp * half:(grp + 1) * half])
    return jnp.concatenate(parts, axis=1)


def _f_ln(u, g, b):
    mu = jnp.mean(u, axis=-1, keepdims=True)
    var = jnp.mean(jnp.square(u - mu), axis=-1, keepdims=True)
    return _silu((u - mu) * lax.rsqrt(var + EPS) * g + b)


def _f_swiglu(gate, up):
    return _silu(gate) * up


def _f_att(q, k, v):
    outs = []
    for h in range(X_HEADS):
        sl = slice(h * X_HEAD_DIM, (h + 1) * X_HEAD_DIM)
        s = _dot1(q[:, sl], k[:, sl], "nt") * (X_HEAD_DIM ** -0.5)
        s = s - lax.stop_gradient(jnp.max(s, axis=-1, keepdims=True))
        p = jnp.exp(s)
        p = p / jnp.sum(p, axis=-1, keepdims=True)
        outs.append(_dot1(p, v[:, sl], "nn"))
    return jnp.concatenate(outs, axis=1)


def _loss_bwd(x3, target, g, *, dx_dtype=_F32, name, ts=512):
    s, d = x3.shape

    def f(x, t, gv):
        return 0.5 * jnp.sum(jnp.mean(jnp.square(_f_rms(x, gv) - t), axis=-1))

    def body(x_ref, t_ref, g_ref, dx_ref, dg_ref, l_ref):
        @pl.when(pl.program_id(0) == 0)
        def _():
            dg_ref[...] = jnp.zeros_like(dg_ref)
            l_ref[...] = jnp.zeros_like(l_ref)

        lv, (dx, dg) = jax.value_and_grad(f, argnums=(0, 2))(x_ref[...].astype(_F32), t_ref[...], g_ref[...])
        dx_ref[...] = dx.astype(dx_ref.dtype)
        dg_ref[...] += dg
        l_ref[...] += lv

    row = pl.BlockSpec((ts, d), lambda i: (i, 0))
    return pl.pallas_call(
        body, name=name, grid=(s // ts,),
        in_specs=[row, row, pl.BlockSpec((1, d), lambda i: (0, 0))],
        out_specs=[row, pl.BlockSpec((1, d), lambda i: (0, 0)), pl.BlockSpec((SUBLANES, LANES), lambda i: (0, 0))],
        out_shape=[jax.ShapeDtypeStruct((s, d), dx_dtype), jax.ShapeDtypeStruct((1, d), _F32),
                   jax.ShapeDtypeStruct((SUBLANES, LANES), _F32)],
        compiler_params=_params(("arbitrary",)),
    )(x3, target, g)


_CONV_PAD = 32
_CONV_ROWS = 128
_CONV_CB = 128


def _conv_taps(k_taps):
    groups = {}
    for k in range(k_taps):
        j = k_taps - 1 - k
        groups.setdefault(j % SUBLANES, []).append((k, j))
    return groups


def _conv_window(win, wv, groups, init):
    pad, rows = _CONV_PAD, _CONV_ROWS
    acc = init
    for rot, taps in groups.items():
        rolled = win if rot == 0 else pltpu.roll(win, rot, 0)
        for k, j in taps:
            off = pad - (j - rot)
            acc = acc + rolled[off:off + rows, :] * wv[k:k + 1, :]
    return acc


def _conv_fill(x_refs, xp_ref, s, glu):
    pad, cb = _CONV_PAD, _CONV_CB
    step = _pick(s, (512, 256, _CONV_ROWS))
    xp_ref[0:pad, :] = jnp.zeros((pad, cb), _F32)

    def fill(r, carry):
        base = pl.multiple_of(r * step, step)
        v = x_refs[0][pl.ds(base, step), :].astype(_F32)
        if glu:
            v = v * _sig(x_refs[1][pl.ds(base, step), :].astype(_F32))
        xp_ref[pl.ds(pad + base, step), :] = v
        return carry

    lax.fori_loop(0, s // step, fill, 0)


def _conv_fwd(xs, w, b, k_taps, *, glu=False, act=False, out_dtype=_F32, name):
    s, c = xs[0].shape
    kp = w.shape[0]
    pad, rows, cb = _CONV_PAD, _CONV_ROWS, _CONV_CB
    groups = _conv_taps(k_taps)
    n_in = len(xs)

    def body(*refs):
        x_refs = refs[:n_in]
        w_ref, b_ref, o_ref, xp_ref = refs[n_in:]
        _conv_fill(x_refs, xp_ref, s, glu)
        wv = w_ref[...]
        bias = jnp.broadcast_to(b_ref[...], (rows, cb))

        def chunk(r, carry):
            base = pl.multiple_of(r * rows, rows)
            acc = _conv_window(xp_ref[pl.ds(base, rows + pad), :], wv, groups, bias)
            o_ref[pl.ds(base, rows), :] = (_silu(acc) if act else acc).astype(o_ref.dtype)
            return carry

        lax.fori_loop(0, s // rows, chunk, 0)

    col = pl.BlockSpec((s, cb), lambda i: (0, i))
    return pl.pallas_call(
        body, name=name, grid=(c // cb,),
        in_specs=[col] * n_in + [pl.BlockSpec((kp, cb), lambda i: (0, i)), pl.BlockSpec((1, cb), lambda i: (0, i))],
        out_specs=col, out_shape=jax.ShapeDtypeStruct((s, c), out_dtype),
        scratch_shapes=[pltpu.VMEM((s + pad, cb), _F32)],
        compiler_params=_params(("parallel",)),
    )(*xs, w, b)


def _conv_bwd(xs, w, b, dy, k_taps, *, glu=False, act=False, name):
    s, c = xs[0].shape
    kp = w.shape[0]
    pad, rows, cb = _CONV_PAD, _CONV_ROWS, _CONV_CB
    groups = _conv_taps(k_taps)
    win_rows = rows + pad
    n_in = len(xs)

    def fold(v):
        acc = v[0:SUBLANES, :]
        for i in range(1, rows // SUBLANES):
            acc = acc + v[i * SUBLANES:(i + 1) * SUBLANES, :]
        return acc

    def body(*refs):
        x_refs = refs[:n_in]
        w_ref, b_ref, dy_ref = refs[n_in:n_in + 3]
        dx_refs = refs[n_in + 3:2 * n_in + 3]
        dw_ref, db_ref, xp_ref, dyp_ref, acc_ref, dbacc_ref = refs[2 * n_in + 3:]
        _conv_fill(x_refs, xp_ref, s, glu)
        dyp_ref[s:s + pad, :] = jnp.zeros((pad, cb), _F32)
        acc_ref[...] = jnp.zeros_like(acc_ref)
        dbacc_ref[...] = jnp.zeros_like(dbacc_ref)
        wv = w_ref[...]
        bias = jnp.broadcast_to(b_ref[...], (rows, cb))

        def through_act(r, carry):
            base = pl.multiple_of(r * rows, rows)
            d = dy_ref[pl.ds(base, rows), :].astype(_F32)
            if act:
                pre = _conv_window(xp_ref[pl.ds(base, win_rows), :], wv, groups, bias)
                sg = _sig(pre)
                d = d * (sg * (1.0 + pre * (1.0 - sg)))
            dyp_ref[pl.ds(base, rows), :] = d
            return carry

        lax.fori_loop(0, s // rows, through_act, 0)

        def chunk(r, carry):
            base = pl.multiple_of(r * rows, rows)
            xwin = xp_ref[pl.ds(base, win_rows), :]
            dwin = dyp_ref[pl.ds(base, win_rows), :]
            dyc = dwin[0:rows, :]
            dxacc = jnp.zeros((rows, cb), _F32)
            for rot, taps in groups.items():
                xr = xwin if rot == 0 else pltpu.roll(xwin, rot, 0)
                dr = dwin if rot == 0 else pltpu.roll(dwin, win_rows - rot, 0)
                for k, j in taps:
                    a8 = j - rot
                    dxacc = dxacc + dr[a8:a8 + rows, :] * wv[k:k + 1, :]
                    prod = dyc * xr[pad - a8:pad - a8 + rows, :]
                    acc_ref[k * SUBLANES:(k + 1) * SUBLANES, :] += fold(prod)
            dbacc_ref[...] += fold(dyc)
            if glu:
                av = x_refs[0][pl.ds(base, rows), :].astype(_F32)
                sg = _sig(x_refs[1][pl.ds(base, rows), :].astype(_F32))
                dx_refs[0][pl.ds(base, rows), :] = (dxacc * sg).astype(dx_refs[0].dtype)
                dx_refs[1][pl.ds(base, rows), :] = (dxacc * av * sg * (1.0 - sg)).astype(dx_refs[1].dtype)
            else:
                dx_refs[0][pl.ds(base, rows), :] = dxacc.astype(dx_refs[0].dtype)
            return carry

        lax.fori_loop(0, s // rows, chunk, 0)
        dw_ref[...] = jnp.zeros_like(dw_ref)
        for k in range(k_taps):
            dw_ref[k:k + 1, :] = jnp.sum(acc_ref[k * SUBLANES:(k + 1) * SUBLANES, :], axis=0, keepdims=True)
        db_ref[...] = jnp.sum(dbacc_ref[...], axis=0, keepdims=True)

    col = pl.BlockSpec((s, cb), lambda i: (0, i))
    wspec = pl.BlockSpec((kp, cb), lambda i: (0, i))
    bspec = pl.BlockSpec((1, cb), lambda i: (0, i))
    dx_dtype = xs[0].dtype
    res = pl.pallas_call(
        body, name=name, grid=(c // cb,),
        in_specs=[col] * n_in + [wspec, bspec, col], out_specs=[col] * n_in + [wspec, bspec],
        out_shape=[jax.ShapeDtypeStruct((s, c), dx_dtype)] * n_in
        + [jax.ShapeDtypeStruct((kp, c), _F32), jax.ShapeDtypeStruct((1, c), _F32)],
        scratch_shapes=[pltpu.VMEM((s + pad, cb), _F32), pltpu.VMEM((s + pad, cb), _F32),
                        pltpu.VMEM((kp * SUBLANES, cb), _F32), pltpu.VMEM((SUBLANES, cb), _F32)],
        compiler_params=_params(("parallel",)),
    )(*xs, w, b, dy)
    return list(res[:n_in]), res[n_in], res[n_in + 1]


def _tri_sum(v, lower):
    l = v.shape[0]
    r, c = lax.broadcasted_iota(jnp.int32, (l, l), 0), lax.broadcasted_iota(jnp.int32, (l, l), 1)
    tri = ((r >= c) if lower else (r <= c)).astype(jnp.bfloat16)
    hi = v.astype(jnp.bfloat16)
    r1 = v - hi.astype(_F32)
    mid = r1.astype(jnp.bfloat16)
    lo = (r1 - mid.astype(_F32)).astype(jnp.bfloat16)
    out = jnp.zeros_like(v)
    for part in (hi, mid, lo):
        out = out + lax.dot_general(tri, part, _DN["nn"], preferred_element_type=_F32)
    return out


@jax.custom_vjp
def _cumsum_rows(v):
    return _tri_sum(v, True)


_cumsum_rows.defvjp(lambda v: (_tri_sum(v, True), None), lambda _, ct: (_tri_sum(ct, False),))


def _ssd_chunk(xbc, dtraw, prev, bias, alog):
    l = xbc.shape[0]
    xs = xbc[:, :SSD_WIDTH]
    bm = xbc[:, SSD_WIDTH:SSD_WIDTH + 2 * SSD_STATE]
    cm = xbc[:, SSD_WIDTH + 2 * SSD_STATE:]
    v = dtraw + bias
    dt = jnp.maximum(v, 0.0) + jnp.log1p(jnp.exp(-jnp.abs(v)))
    a_neg = -jnp.exp(alog)
    acs = _cumsum_rows(dt * a_neg)
    acs_t = acs.T
    total = acs[l - 1:l, :]
    row = lax.broadcasted_iota(jnp.int32, (l, l), 0)
    colv = lax.broadcasted_iota(jnp.int32, (l, l), 1)
    causal = row >= colv
    lane_lo = lax.broadcasted_iota(jnp.int32, (l, LANES), 1) < HEAD_DIM
    row_lo = lax.broadcasted_iota(jnp.int32, (LANES, SSD_STATE), 0) < HEAD_DIM

    def pair_lanes(m, h0):
        return jnp.where(lane_lo, m[:, h0:h0 + 1], m[:, h0 + 1:h0 + 2])

    ys, news = [], []
    cb = {}
    for j in range(SSD_HEADS // 2):
        h0 = 2 * j
        grp = h0 // (SSD_HEADS // 2)
        bg = bm[:, grp * SSD_STATE:(grp + 1) * SSD_STATE]
        cg = cm[:, grp * SSD_STATE:(grp + 1) * SSD_STATE]
        if grp not in cb:
            cb[grp] = _dot1(cg, bg, "nt")
        xdt = xs[:, j * LANES:(j + 1) * LANES] * pair_lanes(dt, h0)
        y = jnp.zeros((l, LANES), _F32)
        for hh, mask in ((h0, lane_lo), (h0 + 1, jnp.logical_not(lane_lo))):
            seg = acs[:, hh:hh + 1] - acs_t[hh:hh + 1, :]
            dec = jnp.exp(jnp.where(causal, seg, -jnp.inf))
            y = y + _dot1(cb[grp] * dec, jnp.where(mask, xdt, 0.0), "nn")
        acs_p = pair_lanes(acs, h0)
        prev_p = prev[j * LANES:(j + 1) * LANES, :]
        y = y + _dot1(cg, prev_p, "nt") * jnp.exp(acs_p)
        total_p = jnp.where(lane_lo[0:1, :], total[:, h0:h0 + 1], total[:, h0 + 1:h0 + 2])
        wgt = jnp.exp(total_p - acs_p)
        st = _dot1(xdt * wgt, bg, "tn")
        cdec = jnp.exp(jnp.where(row_lo, total[:, h0:h0 + 1], total[:, h0 + 1:h0 + 2]))
        news.append(prev_p * cdec + st)
        ys.append(y)
    return jnp.concatenate(ys, axis=1), jnp.concatenate(news, axis=0)


def _ssd_gate_chunk(xbc, dtraw, prev, bias, alog, z, dexp, g):
    y, new = _ssd_chunk(xbc, dtraw, prev, bias, alog)
    return _f_gate(y, xbc[:, :SSD_WIDTH], z, dexp, g), new


def _ssd_fwd(xbc, dtraw, bias, alog, z, dexp, g, *, name):
    s = xbc.shape[0]
    nc = s // CHUNK
    nstate = SSD_HEADS * HEAD_DIM

    def body(x_ref, dt_ref, b_ref, a_ref, z_ref, d_ref, g_ref, y_ref, st_ref, state_ref):
        @pl.when(pl.program_id(0) == 0)
        def _():
            state_ref[...] = jnp.zeros_like(state_ref)

        prev = state_ref[...]
        st_ref[...] = prev
        y, new = _ssd_gate_chunk(x_ref[...].astype(_F32), dt_ref[...], prev, b_ref[...], a_ref[...], z_ref[...].astype(_F32),
                                 d_ref[...], g_ref[...])
        y_ref[...] = y.astype(y_ref.dtype)
        state_ref[...] = new

    small = pl.BlockSpec((1, LANES), lambda i: (0, 0))
    wide = pl.BlockSpec((1, SSD_WIDTH), lambda i: (0, 0))
    rows = pl.BlockSpec((CHUNK, SSD_WIDTH), lambda i: (i, 0))
    return pl.pallas_call(
        body, name=name, grid=(nc,),
        in_specs=[pl.BlockSpec((CHUNK, XBC_WIDTH), lambda i: (i, 0)), pl.BlockSpec((CHUNK, LANES), lambda i: (i, 0)),
                  small, small, rows, wide, wide],
        out_specs=[rows, pl.BlockSpec((None, nstate, SSD_STATE), lambda i: (i, 0, 0))],
        out_shape=[jax.ShapeDtypeStruct((s, SSD_WIDTH), _MXU), jax.ShapeDtypeStruct((nc, nstate, SSD_STATE), _F32)],
        scratch_shapes=[pltpu.VMEM((nstate, SSD_STATE), _F32)],
        compiler_params=_params(("arbitrary",)),
    )(xbc, dtraw, bias, alog, z, dexp, g)


def _ssd_bwd(xbc, dtraw, states, bias, alog, z, dexp, g, dy, *, name):
    s = xbc.shape[0]
    nc = s // CHUNK
    nstate = SSD_HEADS * HEAD_DIM

    def body(x_ref, dt_ref, st_ref, b_ref, a_ref, z_ref, d_ref, g_ref, dy_ref,
             dx_ref, ddt_ref, db_ref, da_ref, dz_ref, dd_ref, dg_ref, dstate_ref):
        @pl.when(pl.program_id(0) == 0)
        def _():
            dstate_ref[...] = jnp.zeros_like(dstate_ref)
            for acc in (db_ref, da_ref, dd_ref, dg_ref):
                acc[...] = jnp.zeros_like(acc)

        _, vjp = jax.vjp(_ssd_gate_chunk, x_ref[...].astype(_F32), dt_ref[...], st_ref[...], b_ref[...], a_ref[...],
                         z_ref[...].astype(_F32), d_ref[...], g_ref[...])
        dx, ddt, dprev, db, da, dz, dd, dg = vjp((dy_ref[...].astype(_F32), dstate_ref[...]))
        dx_ref[...] = dx.astype(dx_ref.dtype)
        ddt_ref[...] = ddt
        dz_ref[...] = dz.astype(dz_ref.dtype)
        db_ref[...] += db
        da_ref[...] += da
        dd_ref[...] += dd
        dg_ref[...] += dg
        dstate_ref[...] = dprev

    rev = lambda i: (nc - 1 - i, 0)
    small = pl.BlockSpec((1, LANES), lambda i: (0, 0))
    wide = pl.BlockSpec((1, SSD_WIDTH), lambda i: (0, 0))
    rows = pl.BlockSpec((CHUNK, SSD_WIDTH), rev)
    return pl.pallas_call(
        body, name=name, grid=(nc,),
        in_specs=[pl.BlockSpec((CHUNK, XBC_WIDTH), rev), pl.BlockSpec((CHUNK, LANES), rev),
                  pl.BlockSpec((None, nstate, SSD_STATE), lambda i: (nc - 1 - i, 0, 0)), small, small, rows, wide, wide,
                  rows],
        out_specs=[pl.BlockSpec((CHUNK, XBC_WIDTH), rev), pl.BlockSpec((CHUNK, LANES), rev), small, small, rows, wide, wide],
        out_shape=[jax.ShapeDtypeStruct((s, XBC_WIDTH), xbc.dtype), jax.ShapeDtypeStruct((s, LANES), _F32),
                   jax.ShapeDtypeStruct((1, LANES), _F32), jax.ShapeDtypeStruct((1, LANES), _F32),
                   jax.ShapeDtypeStruct((s, SSD_WIDTH), z.dtype), jax.ShapeDtypeStruct((1, SSD_WIDTH), _F32),
                   jax.ShapeDtypeStruct((1, SSD_WIDTH), _F32)],
        scratch_shapes=[pltpu.VMEM((nstate, SSD_STATE), _F32)],
        compiler_params=_params(("arbitrary",)),
    )(xbc, dtraw, states, bias, alog, z, dexp, g, dy)


def _pad_cols(a, width):
    return jnp.pad(a, ((0, 0), (0, width - a.shape[1])))


def _pad_rows(a, rows):
    return jnp.pad(a, ((0, rows - a.shape[0]), (0, 0)))


def _tie(a, token):
    return a + token[0:1, 0:1].astype(a.dtype)


def _local_step(x, mem, target, w, fetch, emit):
    bf = _MXU
    d = D_MODEL
    h = _row_fwd(_f_rms, [x], [w['norm_mix_g']], [(d, bf)], name="f_norm_mix")
    w_in = fetch('in', h)['w_in']
    z_end, xbc_end, dt_end = SSD_WIDTH, SSD_WIDTH + XBC_WIDTH, SSD_WIDTH + XBC_WIDTH + SSD_HEADS
    w_z, w_xbc = (w_in, 0, z_end), (w_in, z_end, xbc_end)
    w_dt = _pad_rows(w_in[xbc_end:dt_end], LANES)
    w_a, w_g = (w_in, dt_end, dt_end + CF_WIDTH), (w_in, dt_end + CF_WIDTH, w_in.shape[0])
    dt_bias = _pad_cols(w['ssd_dt_bias'], LANES)
    a_log = _pad_cols(w['ssd_A_log'], LANES)
    d_exp = jnp.repeat(w['ssd_D'], HEAD_DIM, axis=1)
    g_final = w['norm_final_g'].reshape(1, D_MODEL)

    z, xbc, dtr, ga, gg = _mm_fan_out(h, [w_z, w_xbc, w_dt, w_a, w_g], tb=True, out_dtypes=[bf, bf, _F32, bf, bf],
                                      name="f_in")
    wc = fetch('conv', xbc)
    ssd_w = _pad_rows(wc['ssd_conv_w'], SUBLANES)
    cf_w = _pad_rows(wc['cf_conv_w'], 32)
    xbc_a = _conv_fwd([xbc], ssd_w, w['ssd_conv_b'], SSD_CONV, act=True, out_dtype=bf, name="f_ssd_conv")
    y_n, states = _ssd_fwd(xbc_a, dtr, dt_bias, a_log, z, d_exp, w['ssd_norm_g'], name="f_ssd")
    u_c = _conv_fwd([ga, gg], cf_w, w['cf_conv_b'], CF_CONV, glu=True, out_dtype=bf, name="f_cf_conv")
    u = _row_fwd(_f_ln, [u_c], [w['cf_ln_g'], w['cf_ln_b']], [(d, bf)], name="f_cf_ln")
    wm = fetch('mid', y_n)
    w_out_y, w_out_u = (wm['w_out'], 0, SSD_WIDTH), (wm['w_out'], SSD_WIDTH, wm['w_out'].shape[0])
    x1, hq = _mm_fan_in([(y_n, w_out_y), (u, w_out_u)], add=x, out_dtype=bf, epilogue=(_f_rms, w['norm_xattn_g'], bf),
                        name="f_out")
    q = _mm(hq, wm['w_q'], out_dtype=bf, name="f_q")
    memn = _row_fwd(_f_rms, [mem], [w['norm_mem_g']], [(d, bf)], name="f_norm_mem")
    kv = _mm(memn, wm['w_kv'], name="f_kv")
    k_mat, v_mat = kv[:, :d], kv[:, d:]
    o = _row_fwd(_f_att, [q], [k_mat, v_mat], [(d, bf)], ts=512, name="f_att")
    x2, hf = _mm_fan_in([(o, wm['w_o'])], add=x1, out_dtype=bf, epilogue=(_f_rms, w['norm_ffn_g'], bf), name="f_o")
    wf = fetch('ffn', hf)
    gate, up, act = _mm_fan_out(hf, [wf['w_gate'], wf['w_up']], tb=True, out_dtypes=[bf, bf], epilogue=_f_swiglu,
                                extra_outs=[(D_FF, bf)], tm=256, name="f_ffn_in")
    x3 = _mm(act, wf['w_down'], add=x2, out_dtype=bf, name="f_down")

    dx3, dg_final, loss = _loss_bwd(x3, target, g_final, dx_dtype=bf, name="b_loss")
    g = {'norm_final_g': dg_final.reshape(d)}

    dact = _mm(dx3, wf['w_down'], tb=True, out_dtype=bf, name="b_down_x")
    dw_down = _mm(act, dx3, ta=True, out_dtype=bf, name="b_down_w")
    def swiglu_bwd(gate_t, up_t, dact_t):
        return jax.vjp(_f_swiglu, gate_t, up_t)[1](dact_t)

    dhf, dgate, dup = _mm_fan_in([(None, wf['w_gate']), (None, wf['w_up'])], prologue=swiglu_bwd, pro_ins=[gate, up, dact],
                                 pro_out_dtypes=[bf, bf], out_dtype=bf, tm=256, name="b_ffn_in_x")
    sent = emit({'w_down': dw_down, 'w_gate': _mm(dgate, hf, ta=True, out_dtype=bf, name="b_gate_w"),
                 'w_up': _mm(dup, hf, ta=True, out_dtype=bf, name="b_up_w")})
    (dx2,), (g['norm_ffn_g'],) = _row_bwd(_f_rms, [x2], [_tie(w['norm_ffn_g'], sent)], [dhf], adds={0: dx3}, row_dtypes=[bf], name="b_norm_ffn")

    do = _mm(dx2, wm['w_o'], tb=True, out_dtype=bf, name="b_o_x")
    dw_o = _mm(o, dx2, ta=True, out_dtype=bf, name="b_o_w")
    (dq,), (dk, dv) = _row_bwd(_f_att, [q], [k_mat, v_mat], [do], row_dtypes=[bf], ts=512, name="b_att")
    dw_q = _mm(hq, dq, ta=True, out_dtype=bf, name="b_q_w")
    dhq = _mm(dq, wm['w_q'], tb=True, out_dtype=bf, name="b_q_x")
    (dx1,), (g['norm_xattn_g'],) = _row_bwd(_f_rms, [x1], [w['norm_xattn_g']], [dhq], adds={0: dx2}, row_dtypes=[bf], name="b_norm_xattn")
    dkv = jnp.concatenate([dk, dv], axis=1)
    dmemn = _mm(dkv, wm['w_kv'], tb=True, name="b_kv_x")
    _, (g['norm_mem_g'],) = _row_bwd(_f_rms, [mem], [w['norm_mem_g']], [dmemn], need=[False], name="b_norm_mem")
    sent = emit({'w_o': dw_o, 'w_q': dw_q, 'w_kv': _mm_tn_blocked(memn, dkv, N_DEV, out_dtype=bf, name="b_kv_w")},
                after=g['norm_mem_g'])

    dyn, du = _mm_fan_out(dx1, [w_out_y, w_out_u], tb=True, out_dtypes=[bf, bf], name="b_out_x")
    (du_c,), (g['cf_ln_g'], g['cf_ln_b']) = _row_bwd(_f_ln, [u_c], [_tie(w['cf_ln_g'], sent), w['cf_ln_b']], [du], row_dtypes=[bf], name="b_cf_ln")
    sent = emit({'w_out': jnp.concatenate(_mm_tn_fan([y_n, u], dx1, out_dtype=bf, name="b_out_w"), axis=0)})
    (dga, dgg), dcf_w, g['cf_conv_b'] = _conv_bwd([ga, gg], cf_w, w['cf_conv_b'], du_c, CF_CONV, glu=True, name="b_cf_conv")
    g['cf_conv_w'] = dcf_w[:CF_CONV]
    dxbc_a, ddtr, ddt_bias, da_log, dz, dd_exp, g['ssd_norm_g'] = _ssd_bwd(
        xbc_a, dtr, states, dt_bias, a_log, z, d_exp, _tie(w['ssd_norm_g'], sent), dyn, name="b_ssd")
    g['ssd_D'] = jnp.sum(dd_exp.reshape(SSD_HEADS, HEAD_DIM), axis=1).reshape(1, SSD_HEADS)
    g['ssd_dt_bias'] = ddt_bias[:, :SSD_HEADS]
    g['ssd_A_log'] = da_log[:, :SSD_HEADS]
    (dxbc,), dssd_w, g['ssd_conv_b'] = _conv_bwd([xbc], ssd_w, w['ssd_conv_b'], dxbc_a, SSD_CONV, act=True, name="b_ssd_conv")
    g['ssd_conv_w'] = dssd_w[:SSD_CONV]

    dw_z, dw_dt, dw_a, dw_g = _mm_tn_fan([dz, ddtr, dga, dgg], h, out_dtype=bf, name="b_in_zdag_w")
    sent = emit({'w_in': jnp.concatenate([
        dw_z, _mm(dxbc, h, ta=True, out_dtype=bf, name="b_in_xbc_w"), dw_dt[:SSD_HEADS], dw_a, dw_g], axis=0)})
    dh = _mm_fan_in([(dz, w_z), (dxbc, w_xbc), (ddtr, _tie(w_dt, sent)), (dga, w_a), (dgg, w_g)], out_dtype=bf,
                    name="b_in_x")
    (dx,), (g['norm_mix_g'],) = _row_bwd(_f_rms, [x], [w['norm_mix_g']], [dh], adds={0: dx1}, name="b_norm_mix")
    return loss, dx, g


_ANY = pl.BlockSpec(memory_space=pl.ANY)


def _place():
    x, y, c = lax.axis_index("x"), lax.axis_index("y"), lax.axis_index("c")
    return x, y, c


_HBM =pl.BlockSpec(memory_space=pltpu.HBM)
_SEM = pl.BlockSpec(memory_space=pltpu.SEMAPHORE)
_EFFECT = pltpu.SideEffectType.DATAFLOW_SIDE_EFFECTING
_FLIPS = [(dx, dy, dc) for dx in (0, 1) for dy in (0, 1) for dc in (0, 1)][1:]


def _peer(flip, x, y, c):
    return (1 - x if flip[0] else x, 1 - y if flip[1] else y, 1 - c if flip[2] else c)


_CHIP_FLIPS = [f for f in _FLIPS if f[2] == 0]


def _send_start(srcs, blocked, *, flips=None, after=None, name):
    n = len(srcs)
    flips = _FLIPS if flips is None else flips
    nf = len(flips)
    lands = [jax.ShapeDtypeStruct(s.shape if blocked else (N_DEV,) + s.shape, s.dtype) for s in srcs]
    n_in = 2 * n + (after is not None)

    def body(*refs):
        src_refs, land_refs = refs[:n], refs[n:2 * n]
        send_sems, recv_sems = refs[n_in], refs[n_in + 1]
        token = refs[-1]
        x, y, c = _place()
        me = 4 * x + 2 * y + c
        for a in range(n):
            for k, flip in enumerate(flips):
                p = _peer(flip, x, y, c)
                src = src_refs[a].at[4 * p[0] + 2 * p[1] + p[2]] if blocked else src_refs[a]
                pltpu.make_async_remote_copy(
                    src_ref=src, dst_ref=land_refs[a].at[me], send_sem=send_sems.at[nf * a + k],
                    recv_sem=recv_sems.at[nf * a + k], device_id=p, device_id_type=MESH).start()
        token[...] = jnp.zeros_like(token)

    res = pl.pallas_call(
        body, name=name,
        out_shape=(pltpu.SemaphoreType.DMA((nf * n,)), pltpu.SemaphoreType.DMA((nf * n,)),
                   *[pltpu.HBM(s.shape, s.dtype) for s in srcs], *[pltpu.HBM(l.shape, l.dtype) for l in lands],
                   jax.ShapeDtypeStruct((SUBLANES, LANES), _F32)),
        in_specs=[_HBM] * (2 * n) + [_ANY] * (after is not None),
        out_specs=(_SEM, _SEM, *[_HBM] * (2 * n), pl.BlockSpec(memory_space=pltpu.VMEM)),
        input_output_aliases={i: 2 + i for i in range(2 * n)},
        compiler_params=pltpu.CompilerParams(has_side_effects=_EFFECT),
    )(*[pltpu.with_memory_space_constraint(s, pltpu.HBM) for s in srcs],
      *[pltpu.with_memory_space_constraint(lax.empty(l.shape, l.dtype), pltpu.HBM) for l in lands],
      *([after] if after is not None else []))
    return res[0], res[1], list(res[2:2 + n]), list(res[2 + n:2 + 2 * n]), res[-1]


def _send_wait(handles, after, blocked, *, flips=None, name):
    send_sems, recv_sems, srcs, lands, _ = handles
    n = len(srcs)
    flips = _FLIPS if flips is None else flips
    nf = len(flips)

    def body(*refs):
        src_refs, land_refs = refs[:n], refs[n:2 * n]
        send_sems, recv_sems = refs[2 * n], refs[2 * n + 1]
        x, y, c = _place()
        for a in range(n):
            for k, flip in enumerate(flips):
                p = _peer(flip, x, y, c)
                pid = 4 * p[0] + 2 * p[1] + p[2]
                cp = pltpu.make_async_remote_copy(
                    src_ref=src_refs[a].at[pid] if blocked else src_refs[a], dst_ref=land_refs[a].at[pid],
                    send_sem=send_sems.at[nf * a + k], recv_sem=recv_sems.at[nf * a + k], device_id=p, device_id_type=MESH)
                cp.wait_send()
                cp.wait_recv()

    res = pl.pallas_call(
        body, name=name,
        out_shape=tuple(pltpu.HBM(s.shape, s.dtype) for s in srcs + lands),
        in_specs=[_HBM] * (2 * n) + [_SEM, _SEM, _ANY], out_specs=tuple([_HBM] * (2 * n)),
        input_output_aliases={i: i for i in range(2 * n)},
        compiler_params=pltpu.CompilerParams(has_side_effects=_EFFECT),
    )(*srcs, *lands, send_sems, recv_sems, after)
    return list(res[:n]), list(res[n:])


def _sibling_share(lands, owns, *, name):
    n = len(lands)

    def body(*refs):
        own_refs, land_refs = refs[n:2 * n], refs[2 * n:3 * n]
        send_sems, recv_sems = refs[3 * n], refs[3 * n + 1]
        x, y, c = _place()
        sibling = (x, y, 1 - c)
        chips = [(1 - x, y), (x, 1 - y), (1 - x, 1 - y)]

        def copy(a, k, block, src=None):
            slot = land_refs[a].at[block]
            return pltpu.make_async_remote_copy(
                src_ref=slot if src is None else src, dst_ref=slot, send_sem=send_sems.at[4 * a + k],
                recv_sem=recv_sems.at[4 * a + k], device_id=sibling, device_id_type=MESH)

        sends = []
        for a in range(n):
            sends.append(copy(a, 0, 4 * x + 2 * y + c, src=own_refs[a]))
            sends += [copy(a, 1 + j, 4 * cx + 2 * cy + c) for j, (cx, cy) in enumerate(chips)]
        for cp in sends:
            cp.start()
        for a in range(n):
            copy(a, 0, 4 * x + 2 * y + (1 - c)).wait_recv()
            for j, (cx, cy) in enumerate(chips):
                copy(a, 1 + j, 4 * cx + 2 * cy + (1 - c)).wait_recv()
        for cp in sends:
            cp.wait_send()

    return pl.pallas_call(
        body, name=name, in_specs=[_ANY] * (2 * n), out_specs=[_ANY] * n,
        out_shape=[jax.ShapeDtypeStruct(l.shape, l.dtype) for l in lands],
        input_output_aliases={i: i for i in range(n)},
        scratch_shapes=[pltpu.SemaphoreType.DMA((4 * n,)), pltpu.SemaphoreType.DMA((4 * n,))],
    )(*lands, *owns)


def _adamw(parts, w, m, v, *, own=None, me=None, name):
    p, r, c = parts.shape
    tr = _pick(r, (256, 176, 128, 64, 32, 16, 8))
    if own is not None:
        tc = c if tr < r else _pick(c, (256, 128))
        return _adamw_own(parts, own, me, w, m, v, tr, tc, name=name)

    def body(p_ref, w_ref, m_ref, v_ref, g_ref, d_ref, nm_ref, nv_ref):
        g = p_ref[0].astype(_F32)
        for i in range(1, p):
            g = g + p_ref[i].astype(_F32)
        _adamw_math(g, w_ref, m_ref, v_ref, g_ref, d_ref, nm_ref, nv_ref)

    blk = pl.BlockSpec((tr, c), lambda i: (i, 0))
    return pl.pallas_call(
        body, name=name, grid=(r // tr,),
        in_specs=[pl.BlockSpec((p, tr, c), lambda i: (0, i, 0)), blk, blk, blk], out_specs=[blk] * 4,
        out_shape=[jax.ShapeDtypeStruct((r, c), _F32)] * 4,
        compiler_params=_params(("parallel",)),
    )(parts, w, m, v)


def _adamw_math(g, w_ref, m_ref, v_ref, g_ref, d_ref, nm_ref, nv_ref):
    wv = w_ref[...]
    mn = ADAM_B1 * m_ref[...] + (1.0 - ADAM_B1) * g
    vn = ADAM_B2 * v_ref[...] + (1.0 - ADAM_B2) * jnp.square(g)
    m_hat = mn / (1.0 - ADAM_B1 ** ADAM_STEP)
    v_hat = vn / (1.0 - ADAM_B2 ** ADAM_STEP)
    g_ref[...] = g
    d_ref[...] = -ADAM_LR * (m_hat / (jnp.sqrt(v_hat) + ADAM_EPS) + ADAM_WD * wv)
    nm_ref[...] = mn
    nv_ref[...] = vn


def _adamw_own(parts, own, me, w, m, v, tr, tc, *, name):
    p, r, c = parts.shape

    def body(me_ref, p_ref, own_ref, w_ref, m_ref, v_ref, g_ref, d_ref, nm_ref, nv_ref):
        mine = own_ref[...].astype(_F32)
        g = jnp.where(me_ref[0] == 0, mine, p_ref[0].astype(_F32))
        for i in range(1, p):
            g = g + jnp.where(me_ref[0] == i, mine, p_ref[i].astype(_F32))
        _adamw_math(g, w_ref, m_ref, v_ref, g_ref, d_ref, nm_ref, nv_ref)

    blk = pl.BlockSpec((tr, tc), lambda i, j, me_ref: (i, j))
    grid_spec = pltpu.PrefetchScalarGridSpec(
        num_scalar_prefetch=1, grid=(r // tr, c // tc),
        in_specs=[pl.BlockSpec((p, tr, tc), lambda i, j, me_ref: (0, i, j)),
                  pl.BlockSpec((None, tr, tc), lambda i, j, me_ref: (me_ref[0], i, j)), blk, blk, blk],
        out_specs=[blk] * 4)
    return pl.pallas_call(
        body, name=name, grid_spec=grid_spec, out_shape=[jax.ShapeDtypeStruct((r, c), _F32)] * 4,
        compiler_params=_params(("parallel", "parallel")),
    )(me.reshape(1).astype(jnp.int32), parts, own, w, m, v)


def _adamw_rows(g_row, offsets, ws, ms, vs, *, name):
    k = len(ws)

    def body(*refs):
        g_ref, w_refs, m_refs, v_refs = refs[0], refs[1:1 + k], refs[1 + k:1 + 2 * k], refs[1 + 2 * k:1 + 3 * k]
        outs = refs[1 + 3 * k:]
        for i in range(k):
            gi = g_ref[:, offsets[i]:offsets[i] + ws[i].shape[1]]
            _adamw_math(gi, w_refs[i], m_refs[i], v_refs[i], *outs[4 * i:4 * i + 4])

    return pl.pallas_call(
        body, name=name, out_shape=[jax.ShapeDtypeStruct(w.shape, _F32) for w in ws for _ in range(4)],
    )(g_row, *ws, *ms, *vs)


def _sum_parts(parts, *, name):
    p, r, c = parts.shape

    def body(p_ref, o_ref):
        g = p_ref[0].astype(_F32)
        for i in range(1, p):
            g = g + p_ref[i].astype(_F32)
        o_ref[...] = g

    return pl.pallas_call(body, name=name, out_shape=jax.ShapeDtypeStruct((r, c), _F32))(parts)


def kernel(x, mem, norm_mix_g, w_in, ssd_conv_w, ssd_conv_b, ssd_dt_bias, ssd_A_log, ssd_D, ssd_norm_g, cf_conv_w, cf_conv_b, cf_ln_g, cf_ln_b, w_out, norm_xattn_g, norm_mem_g, w_q, w_kv, w_o, norm_ffn_g, w_gate, w_up, w_down, norm_final_g, loss_target, m_norm_mix_g, m_w_in, m_ssd_conv_w, m_ssd_conv_b, m_ssd_dt_bias, m_ssd_A_log, m_ssd_D, m_ssd_norm_g, m_cf_conv_w, m_cf_conv_b, m_cf_ln_g, m_cf_ln_b, m_w_out, m_norm_xattn_g, m_norm_mem_g, m_w_q, m_w_kv, m_w_o, m_norm_ffn_g, m_w_gate, m_w_up, m_w_down, m_norm_final_g, v_norm_mix_g, v_w_in, v_ssd_conv_w, v_ssd_conv_b, v_ssd_dt_bias, v_ssd_A_log, v_ssd_D, v_ssd_norm_g, v_cf_conv_w, v_cf_conv_b, v_cf_ln_g, v_cf_ln_b, v_w_out, v_norm_xattn_g, v_norm_mem_g, v_w_q, v_w_kv, v_w_o, v_norm_ffn_g, v_w_gate, v_w_up, v_w_down, v_norm_final_g):
    args = dict(locals())
    wts = {n: args[n] for n in WEIGHT_NAMES}
    mom = {n: args["m_" + n] for n in WEIGHT_NAMES}
    var = {n: args["v_" + n] for n in WEIGHT_NAMES}
    me = 4 * lax.axis_index("x") + 2 * lax.axis_index("y") + lax.axis_index("c")

    groups = {'in': ['w_in'], 'conv': ['ssd_conv_w', 'cf_conv_w'], 'mid': ['w_out', 'w_q', 'w_kv', 'w_o'],
              'ffn': ['w_gate', 'w_up', 'w_down']}
    def shard(n, a):
        return jnp.transpose(a[0], (1, 0)) if n in TRANSPOSED else a[0]

    two_level = {'in': _CHIP_FLIPS}
    gathers, started = {}, None
    for grp, names in groups.items():
        shards = [wts[n][0] if grp == 'conv' else shard(n, wts[n]).astype(_MXU) for n in names]
        gathers[grp] = _send_start(shards, False, flips=two_level.get(grp), after=started, name="gather_%s_start" % grp)
        started = gathers[grp][4]

    def fetch(grp, after):
        srcs, lands = _send_wait(gathers[grp], started if after is None else after, False, flips=two_level.get(grp),
                                 name="gather_%s_wait" % grp)
        if grp in two_level:
            lands = _sibling_share(lands, srcs, name="gather_%s_share" % grp)
        out = {}
        for n, own, gth in zip(groups[grp], srcs, lands):
            gth = lax.dynamic_update_slice_in_dim(gth, own[None], me, axis=0)
            if n == 'w_kv' or grp == 'conv':
                out[n] = jnp.transpose(gth, (1, 0, 2)).reshape(gth.shape[1], N_DEV * gth.shape[2])
            else:
                out[n] = gth.reshape(N_DEV * gth.shape[1], gth.shape[2])
        return out

    exchanges = []

    def emit(grads, after=None):
        blocks = []
        for n, gw in grads.items():
            if gw.ndim == 2:
                gw = gw.reshape(N_DEV, gw.shape[0] // N_DEV, gw.shape[1])
            blocks.append(gw.astype(jnp.bfloat16))
        first = next(iter(grads))
        exchanges.append((list(grads), _send_start(blocks, True, after=after, name="exchange_%s_start" % first), first))
        return exchanges[-1][1][4]

    full = {n: wts[n] for n in WEIGHT_NAMES if n not in BIG and n not in groups['conv']}
    full['norm_mix_g'] = _tie(norm_mix_g, started)

    loss_blk, grad_x, g = _local_step(x[0], mem[0], loss_target[0], full, fetch, emit)

    small = [n for n in WEIGHT_NAMES if n not in BIG]
    g['loss'] = loss_blk[0:1, 0:1]
    items = small + ['loss']
    size = {n: math.prod(g[n].shape) for n in items}
    seg = {n: -(-size[n] // LANES) * LANES for n in items}
    off, pos = {}, 0
    for n in items:
        off[n], pos = pos, pos + seg[n]
    rows = -(-pos // (LANES * SUBLANES)) * SUBLANES
    flat = jnp.concatenate([jnp.pad(g[n].reshape(-1), (0, seg[n] - size[n])) for n in items]
                           + [jnp.zeros((rows * LANES - pos,), _F32)])
    small_sent = _send_start([flat.reshape(rows, LANES)], False, name="gather_small_start")

    out_g, out_d, out_m, out_v = {}, {}, {}, {}
    done = small_sent[4]
    for names, handles, first in exchanges:
        srcs, lands = _send_wait(handles, done, True, name="exchange_%s_wait" % first)
        for n, own, parts in zip(names, srcs, lands):
            res = _adamw(parts, shard(n, wts[n]), shard(n, mom[n]), shard(n, var[n]), own=own, me=me, name="adamw_" + n)
            out_g[n], out_d[n], out_m[n], out_v[n] = [(jnp.transpose(r, (1, 0)) if n in TRANSPOSED else r)[None] for r in res]
            done = res[0]

    srcs, lands = _send_wait(small_sent, out_g[exchanges[-1][0][-1]], False, name="gather_small_wait")
    small_parts = lax.dynamic_update_slice_in_dim(lands[0], srcs[0][None], me, axis=0)
    g_row = _sum_parts(small_parts, name="sum_small_grads").reshape(1, rows * LANES)
    loss = g_row[0, off['loss']]
    rep = [n for n in small if n not in groups['conv']]
    as_row = lambda a: a.reshape(1, -1)
    res = _adamw_rows(g_row, [off[n] for n in rep], [as_row(wts[n]) for n in rep], [as_row(mom[n]) for n in rep],
                      [as_row(var[n]) for n in rep], name="adamw_small")
    for i, n in enumerate(rep):
        out_g[n], out_d[n], out_m[n], out_v[n] = [r.reshape(wts[n].shape) for r in res[4 * i:4 * i + 4]]
    for n in groups['conv']:
        k_taps, width = g[n].shape
        g_full = g_row[0, off[n]:off[n] + size[n]].reshape(k_taps, width)
        g_mine = lax.dynamic_slice_in_dim(g_full, me * (width // N_DEV), width // N_DEV, axis=1)
        res = _adamw(g_mine[None], wts[n][0], mom[n][0], var[n][0], name="adamw_" + n)
        out_g[n], out_d[n], out_m[n], out_v[n] = [r[None] for r in res]

    return (loss, grad_x[None], *[out_g[n] for n in WEIGHT_NAMES], *[out_d[n] for n in WEIGHT_NAMES],
            *[out_m[n] for n in WEIGHT_NAMES], *[out_v[n] for n in WEIGHT_NAMES])
```

```python
import functools
import math

import jax
import jax.numpy as jnp
from jax import lax
from jax.experimental import pallas as pl
from jax.experimental.pallas import tpu as pltpu

_F32 = jnp.float32
_MXU = jnp.bfloat16
_PREC = None
_VMEM_LIMIT = 56 * 1024 * 1024

D_MODEL = 1024
HEAD_DIM = 64
SSD_HEADS = 16
SSD_WIDTH = 1024
SSD_STATE = 128
SSD_CONV = 4
CHUNK = 128
XBC_WIDTH = 1536
CF_WIDTH = 1024
CF_CONV = 31
X_HEADS = 4
X_HEAD_DIM = 256
D_FF = 2816
EPS = 1e-6
N_DEV = 8
LANES = 128
SUBLANES = 8

ADAM_LR = 0.001
ADAM_B1 = 0.9
ADAM_B2 = 0.999
ADAM_EPS = 1e-08
ADAM_WD = 0.01
ADAM_STEP = 10

MESH = pl.DeviceIdType.MESH
WEIGHT_NAMES = ['norm_mix_g', 'w_in', 'ssd_conv_w', 'ssd_conv_b', 'ssd_dt_bias', 'ssd_A_log', 'ssd_D', 'ssd_norm_g',
                'cf_conv_w', 'cf_conv_b', 'cf_ln_g', 'cf_ln_b', 'w_out', 'norm_xattn_g', 'norm_mem_g', 'w_q', 'w_kv',
                'w_o', 'norm_ffn_g', 'w_gate', 'w_up', 'w_down', 'norm_final_g']
BIG = ['w_in', 'w_out', 'w_q', 'w_kv', 'w_o', 'w_gate', 'w_up', 'w_down']
TRANSPOSED = ('w_in', 'w_gate', 'w_up')


def _params(sem=None):
    return pltpu.CompilerParams(dimension_semantics=sem, vmem_limit_bytes=_VMEM_LIMIT)


def _pick(n, cands):
    for c in cands:
        if n % c == 0:
            return c
    return n


def _mm(a, b, *, ta=False, tb=False, add=None, out_dtype=_F32, name):
    (kdim, m) = a.shape if ta else a.shape[::-1]
    (n, k2) = b.shape if tb else b.shape[::-1]
    assert kdim == k2, (a.shape, b.shape, ta, tb)
    if ta:
        tm = m if m <= 1024 else _pick(m, (1408, 1024, 512, 256, 128))
        tn = n if n <= 1536 else _pick(n, (1408, 1024, 512, 256, 128))
        size = lambda arr: jnp.dtype(arr.dtype).itemsize
        fits = lambda t: (2 * t * (tm * size(a) + tn * size(b)) + tm * tn * (4 + 2 * jnp.dtype(out_dtype).itemsize)
                          <= _VMEM_LIMIT * 3 // 4)
        tk = next((t for t in (2048, 1024, 512, 256, 128) if kdim % t == 0 and fits(t)), _pick(kdim, (128,)))
    else:
        tm = _pick(m, (512, 256, 128))
        tn = n if n <= 2816 else _pick(n, (1408, 1024, 512, 256, 128))
        tk = kdim if kdim <= 2816 else _pick(kdim, (1408, 1024, 512, 256, 128))
    nk = kdim // tk
    dn = (((0 if ta else 1,), (1 if tb else 0,)), ((), ()))

    def body(*refs):
        a_ref, b_ref = refs[0], refs[1]
        add_ref = refs[2] if add is not None else None
        o_ref = refs[3 if add is not None else 2]
        acc_ref = refs[-1]
        k = pl.program_id(2)
        prod = lax.dot_general(a_ref[...].astype(_MXU), b_ref[...].astype(_MXU), dn,
                               preferred_element_type=_F32, precision=_PREC)

        def finish(r):
            if add_ref is not None:
                r = r + add_ref[...].astype(_F32)
            o_ref[...] = r.astype(o_ref.dtype)

        if nk == 1:
            finish(prod)
            return

        @pl.when(k == 0)
        def _():
            acc_ref[...] = prod

        @pl.when(jnp.logical_and(k > 0, k < nk - 1))
        def _():
            acc_ref[...] += prod

        @pl.when(k == nk - 1)
        def _():
            finish(acc_ref[...] + prod)

    a_spec = pl.BlockSpec((tk, tm), lambda i, j, k: (k, i)) if ta else pl.BlockSpec((tm, tk), lambda i, j, k: (i, k))
    b_spec = pl.BlockSpec((tn, tk), lambda i, j, k: (j, k)) if tb else pl.BlockSpec((tk, tn), lambda i, j, k: (k, j))
    o_spec = pl.BlockSpec((tm, tn), lambda i, j, k: (i, j))
    ins, specs = [a, b], [a_spec, b_spec]
    if add is not None:
        ins.append(add)
        specs.append(o_spec)
    return pl.pallas_call(
        body, name=name, grid=(m // tm, n // tn, nk), in_specs=specs, out_specs=o_spec,
        out_shape=jax.ShapeDtypeStruct((m, n), out_dtype),
        scratch_shapes=[pltpu.VMEM((tm, tn), _F32)] if nk > 1 else [],
        compiler_params=_params(("parallel", "parallel", "arbitrary")),
    )(*ins)


def _mm_tn_fan(a_list, b, *, out_dtype, name, tk=1024):
    kdim, n = b.shape
    tk = min(tk, kdim)
    nk = kdim // tk
    na = len(a_list)

    def body(*refs):
        a_refs, b_ref, o_refs, acc_refs = refs[:na], refs[na], refs[na + 1:2 * na + 1], refs[2 * na + 1:]
        k = pl.program_id(0)
        bv = b_ref[...].astype(_MXU)
        for a_ref, o_ref, acc_ref in zip(a_refs, o_refs, acc_refs):
            prod = lax.dot_general(a_ref[...].astype(_MXU), bv, _DN["tn"], preferred_element_type=_F32, precision=_PREC)

            @pl.when(k == 0)
            def _(acc_ref=acc_ref, prod=prod):
                acc_ref[...] = prod

            @pl.when(k > 0)
            def _(acc_ref=acc_ref, prod=prod):
                acc_ref[...] += prod

            @pl.when(k == nk - 1)
            def _(acc_ref=acc_ref, o_ref=o_ref):
                o_ref[...] = acc_ref[...].astype(o_ref.dtype)

    return pl.pallas_call(
        body, name=name, grid=(nk,),
        in_specs=[pl.BlockSpec((tk, a.shape[1]), lambda k: (k, 0)) for a in a_list] + [pl.BlockSpec((tk, n), lambda k: (k, 0))],
        out_specs=[pl.BlockSpec((a.shape[1], n), lambda k: (0, 0)) for a in a_list],
        out_shape=[jax.ShapeDtypeStruct((a.shape[1], n), out_dtype) for a in a_list],
        scratch_shapes=[pltpu.VMEM((a.shape[1], n), _F32) for a in a_list],
        compiler_params=_params(("arbitrary",)),
    )(*a_list, b)


def _mm_tn_blocked(a, b, nblk, *, out_dtype, name):
    kdim, m = a.shape
    w = b.shape[1] // nblk

    def body(a_ref, b_ref, o_ref):
        o_ref[...] = lax.dot_general(a_ref[...].astype(_MXU), b_ref[...].astype(_MXU), _DN["tn"],
                                     preferred_element_type=_F32, precision=_PREC).astype(o_ref.dtype)

    return pl.pallas_call(
        body, name=name, grid=(nblk,),
        in_specs=[pl.BlockSpec((kdim, m), lambda j: (0, 0)), pl.BlockSpec((kdim, w), lambda j: (0, j))],
        out_specs=pl.BlockSpec((None, m, w), lambda j: (j, 0, 0)),
        out_shape=jax.ShapeDtypeStruct((nblk, m, w), out_dtype),
        compiler_params=_params(("parallel",)),
    )(a, b)


def _resident(shape):
    return pl.BlockSpec(shape, lambda i: (0,) * len(shape), pipeline_mode=pl.Buffered(1))


def _windows(bs):
    arrays, wins = [], []
    for b in bs:
        arr, lo, hi = b if isinstance(b, tuple) else (b, 0, b.shape[0])
        idx = next((i for i, x in enumerate(arrays) if x is arr), None)
        if idx is None:
            arrays.append(arr)
            idx = len(arrays) - 1
        wins.append((idx, lo, hi))
    return arrays, wins


def _mm_fan_out(a, bs, *, tb, out_dtypes, epilogue=None, extra_outs=(), tm=512, name):
    m, kdim = a.shape
    tm = min(tm, m)
    bs, wins = _windows(bs)
    ns = [hi - lo if tb else bs[i].shape[1] for i, lo, hi in wins]
    nb = len(bs)
    kind = "nt" if tb else "nn"

    def body(*refs):
        a_ref, b_refs, o_refs = refs[0], refs[1:1 + nb], refs[1 + nb:]
        av = a_ref[...].astype(_MXU)
        prods = [lax.dot_general(av, b_refs[i][lo:hi, :].astype(_MXU), _DN[kind], preferred_element_type=_F32,
                                 precision=_PREC) for i, lo, hi in wins]
        for o_ref, p in zip(o_refs[:len(wins)], prods):
            o_ref[...] = p.astype(o_ref.dtype)
        if epilogue is not None:
            for o_ref, v in zip(o_refs[len(wins):], _tup(epilogue(*prods))):
                o_ref[...] = v.astype(o_ref.dtype)

    widths = ns + [w for w, _ in extra_outs]
    dtypes = list(out_dtypes) + [dt for _, dt in extra_outs]
    return pl.pallas_call(
        body, name=name, grid=(m // tm,),
        in_specs=[pl.BlockSpec((tm, kdim), lambda i: (i, 0))] + [_resident(b.shape) for b in bs],
        out_specs=[pl.BlockSpec((tm, w), lambda i: (i, 0)) for w in widths],
        out_shape=[jax.ShapeDtypeStruct((m, w), dt) for w, dt in zip(widths, dtypes)],
        compiler_params=_params(("parallel",)),
    )(a, *bs)


def _mm_fan_in(pairs, *, add=None, out_dtype=_F32, prologue=None, pro_ins=(), pro_out_dtypes=(), epilogue=None,
               tm=512, name):
    bs, wins = _windows([b for _, b in pairs])
    nb = len(bs)
    n = bs[0].shape[1]
    rows_in = list(pro_ins) if prologue is not None else [a for a, _ in pairs]
    m = rows_in[0].shape[0]
    tm = min(tm, m)
    n_r = len(rows_in)

    def body(*refs):
        r_refs, b_refs = refs[:n_r], refs[n_r:n_r + nb]
        pos = n_r + nb
        add_ref = refs[pos] if add is not None else None
        pos += add is not None
        epi_ref = refs[pos] if epilogue is not None else None
        pos += epilogue is not None
        o_ref, po_refs = refs[pos], refs[pos + 1:]
        if prologue is not None:
            a_vals = _tup(prologue(*[r[...].astype(_F32) for r in r_refs]))
            for po, v in zip(po_refs, a_vals):
                po[...] = v.astype(po.dtype)
        else:
            a_vals = [r[...] for r in r_refs]
        acc = None
        for av, (i, lo, hi) in zip(a_vals, wins):
            p = lax.dot_general(av.astype(_MXU), b_refs[i][lo:hi, :].astype(_MXU), _DN["nn"], preferred_element_type=_F32,
                                precision=_PREC)
            acc = p if acc is None else acc + p
        if add_ref is not None:
            acc = acc + add_ref[...].astype(_F32)
        o_ref[...] = acc.astype(o_ref.dtype)
        if epilogue is not None:
            po_refs[-1][...] = epilogue[0](acc, epi_ref[...]).astype(po_refs[-1].dtype)

    row = lambda w: pl.BlockSpec((tm, w), lambda i: (i, 0))
    ins = rows_in + bs + ([add] if add is not None else []) + ([epilogue[1]] if epilogue is not None else [])
    in_specs = ([row(r.shape[1]) for r in rows_in] + [_resident(b.shape) for b in bs]
                + ([row(n)] if add is not None else []) + ([_resident(epilogue[1].shape)] if epilogue is not None else []))
    extra = [(hi - lo, dt) for (_, lo, hi), dt in zip(wins, pro_out_dtypes)] if prologue is not None else []
    if epilogue is not None:
        extra.append((n, epilogue[2]))
    res = pl.pallas_call(
        body, name=name, grid=(m // tm,), in_specs=in_specs,
        out_specs=[row(n)] + [row(w) for w, _ in extra],
        out_shape=[jax.ShapeDtypeStruct((m, n), out_dtype)] + [jax.ShapeDtypeStruct((m, w), dt) for w, dt in extra],
        compiler_params=_params(("parallel",)),
    )(*ins)
    return res if extra else res[0]


def _row_spec(r, ts):
    if isinstance(r, tuple):
        arr, width, cblk = r
        return arr, pl.BlockSpec((ts, width), lambda i, cblk=cblk: (i, cblk))
    return r, pl.BlockSpec((ts, r.shape[1]), lambda i: (i, 0))


def _tup(v):
    return tuple(v) if isinstance(v, (tuple, list)) else (v,)


def _row_fwd(f, rows, params, outs, *, name, ts=512):
    s = (rows[0][0] if isinstance(rows[0], tuple) else rows[0]).shape[0]
    ts = min(ts, s)
    arrs, specs = zip(*[_row_spec(r, ts) for r in rows])
    n_r, n_p = len(rows), len(params)

    def body(*refs):
        rv = [r[...].astype(_F32) for r in refs[:n_r]]
        pv = [p[...] for p in refs[n_r:n_r + n_p]]
        res = _tup(f(*rv, *pv))
        for o_ref, v in zip(refs[n_r + n_p:], res):
            o_ref[...] = v.astype(o_ref.dtype)

    res = pl.pallas_call(
        body, name=name, grid=(s // ts,),
        in_specs=list(specs) + [pl.BlockSpec(p.shape, lambda i: (0, 0)) for p in params],
        out_specs=[pl.BlockSpec((ts, w), lambda i: (i, 0)) for w, _ in outs],
        out_shape=[jax.ShapeDtypeStruct((s, w), dt) for w, dt in outs],
        compiler_params=_params(("parallel",)),
    )(*arrs, *params)
    return res[0] if len(outs) == 1 else res


def _row_bwd(f, rows, params, cts, *, need=None, adds=None, row_dtypes=None, name, ts=512):
    s = (rows[0][0] if isinstance(rows[0], tuple) else rows[0]).shape[0]
    ts = min(ts, s)
    arrs, specs = zip(*[_row_spec(r, ts) for r in rows])
    n_r, n_p, n_c = len(rows), len(params), len(cts)
    need = [True] * n_r if need is None else need
    adds = {} if adds is None else adds
    add_keys = sorted(adds)
    row_dtypes = [_F32] * n_r if row_dtypes is None else row_dtypes
    needed = [j for j in range(n_r) if need[j]]
    widths = [specs[j].block_shape[1] for j in range(n_r)]

    def body(*refs):
        pos = 0
        r_refs = refs[pos:pos + n_r]; pos += n_r
        p_refs = refs[pos:pos + n_p]; pos += n_p
        c_refs = refs[pos:pos + n_c]; pos += n_c
        a_refs = refs[pos:pos + len(add_keys)]; pos += len(add_keys)
        dr_refs = refs[pos:pos + len(needed)]; pos += len(needed)
        dp_refs = refs[pos:pos + n_p]
        rv = [r[...].astype(_F32) for r in r_refs]
        pv = [p[...] for p in p_refs]
        _, vjp = jax.vjp(lambda *a: _tup(f(*a)), *rv, *pv)
        g = vjp(tuple(c[...].astype(_F32) for c in c_refs))
        for o_ref, j in zip(dr_refs, needed):
            v = g[j]
            if j in adds:
                v = v + a_refs[add_keys.index(j)][...].astype(_F32)
            o_ref[...] = v.astype(o_ref.dtype)
        if n_p:
            @pl.when(pl.program_id(0) == 0)
            def _():
                for dp in dp_refs:
                    dp[...] = jnp.zeros_like(dp)
            for dp, v in zip(dp_refs, g[n_r:]):
                dp[...] += v

    ct_specs = [pl.BlockSpec((ts, c.shape[1]), lambda i: (i, 0)) for c in cts]
    add_specs = [pl.BlockSpec((ts, adds[j].shape[1]), lambda i: (i, 0)) for j in add_keys]
    res = pl.pallas_call(
        body, name=name, grid=(s // ts,),
        in_specs=list(specs) + [pl.BlockSpec(p.shape, lambda i: (0, 0)) for p in params] + ct_specs + add_specs,
        out_specs=[pl.BlockSpec((ts, widths[j]), lambda i: (i, 0)) for j in needed]
        + [pl.BlockSpec(p.shape, lambda i: (0, 0)) for p in params],
        out_shape=[jax.ShapeDtypeStruct((s, widths[j]), row_dtypes[j]) for j in needed]
        + [jax.ShapeDtypeStruct(p.shape, _F32) for p in params],
        compiler_params=_params(("arbitrary",)),
    )(*arrs, *params, *cts, *[adds[j] for j in add_keys])
    return list(res[:len(needed)]), list(res[len(needed):])


_DN = {"nn": (((1,), (0,)), ((), ())), "nt": (((1,), (1,)), ((), ())), "tn": (((0,), (0,)), ((), ()))}


def _dot_raw(a, b, kind):
    return lax.dot_general(a.astype(_MXU), b.astype(_MXU), _DN[kind], preferred_element_type=_F32, precision=_PREC)


@functools.partial(jax.custom_vjp, nondiff_argnums=(2,))
def _dot1(a, b, kind):
    return _dot_raw(a, b, kind)


def _dot1_bwd(kind, res, ct):
    a, b = res
    if kind == "nn":
        return _dot_raw(ct, b, "nt"), _dot_raw(a, ct, "tn")
    if kind == "nt":
        return _dot_raw(ct, b, "nn"), _dot_raw(ct, a, "tn")
    return _dot_raw(b, ct, "nt"), _dot_raw(a, ct, "nn")


_dot1.defvjp(lambda a, b, kind: (_dot_raw(a, b, kind), (a, b)), _dot1_bwd)


def _sig(v):
    return 1.0 / (1.0 + jnp.exp(-v))


def _silu(v):
    return v * _sig(v)


def _f_rms(x, g):
    return x * lax.rsqrt(jnp.mean(x * x, axis=-1, keepdims=True) + EPS) * g


def _f_gate(y, xs, z, dexp, g):
    v = (y + dexp * xs) * _silu(z)
    half = SSD_WIDTH // 2
    parts = []
    for grp in range(2):
        vg = v[:, grp * half:(grp + 1) * half]
        parts.append(vg * lax.rsqrt(jnp.mean(vg * vg, axis=-1, keepdims=True) + EPS) * g[:, grp * half:(grp + 1) * half])
    return jnp.concatenate(parts, axis=1)


def _f_ln(u, g, b):
    mu = jnp.mean(u, axis=-1, keepdims=True)
    var = jnp.mean(jnp.square(u - mu), axis=-1, keepdims=True)
    return _silu((u - mu) * lax.rsqrt(var + EPS) * g + b)


def _f_swiglu(gate, up):
    return _silu(gate) * up


def _f_att(q, k, v):
    outs = []
    for h in range(X_HEADS):
        sl = slice(h * X_HEAD_DIM, (h + 1) * X_HEAD_DIM)
        s = _dot1(q[:, sl], k[:, sl], "nt") * (X_HEAD_DIM ** -0.5)
        s = s - lax.stop_gradient(jnp.max(s, axis=-1, keepdims=True))
        p = jnp.exp(s)
        p = p / jnp.sum(p, axis=-1, keepdims=True)
        outs.append(_dot1(p, v[:, sl], "nn"))
    return jnp.concatenate(outs, axis=1)


def _loss_bwd(x3, target, g, *, dx_dtype=_F32, name, ts=512):
    s, d = x3.shape

    def f(x, t, gv):
        return 0.5 * jnp.sum(jnp.mean(jnp.square(_f_rms(x, gv) - t), axis=-1))

    def body(x_ref, t_ref, g_ref, dx_ref, dg_ref, l_ref):
        @pl.when(pl.program_id(0) == 0)
        def _():
            dg_ref[...] = jnp.zeros_like(dg_ref)
            l_ref[...] = jnp.zeros_like(l_ref)

        lv, (dx, dg) = jax.value_and_grad(f, argnums=(0, 2))(x_ref[...].astype(_F32), t_ref[...], g_ref[...])
        dx_ref[...] = dx.astype(dx_ref.dtype)
        dg_ref[...] += dg
        l_ref[...] += lv

    row = pl.BlockSpec((ts, d), lambda i: (i, 0))
    return pl.pallas_call(
        body, name=name, grid=(s // ts,),
        in_specs=[row, row, pl.BlockSpec((1, d), lambda i: (0, 0))],
        out_specs=[row, pl.BlockSpec((1, d), lambda i: (0, 0)), pl.BlockSpec((SUBLANES, LANES), lambda i: (0, 0))],
        out_shape=[jax.ShapeDtypeStruct((s, d), dx_dtype), jax.ShapeDtypeStruct((1, d), _F32),
                   jax.ShapeDtypeStruct((SUBLANES, LANES), _F32)],
        compiler_params=_params(("arbitrary",)),
    )(x3, target, g)


_CONV_PAD = 32
_CONV_ROWS = 128
_CONV_CB = 128


def _conv_taps(k_taps):
    groups = {}
    for k in range(k_taps):
        j = k_taps - 1 - k
        groups.setdefault(j % SUBLANES, []).append((k, j))
    return groups


def _conv_window(win, wv, groups, init):
    pad, rows = _CONV_PAD, _CONV_ROWS
    acc = init
    for rot, taps in groups.items():
        rolled = win if rot == 0 else pltpu.roll(win, rot, 0)
        for k, j in taps:
            off = pad - (j - rot)
            acc = acc + rolled[off:off + rows, :] * wv[k:k + 1, :]
    return acc


def _conv_fill(x_refs, xp_ref, s, glu):
    pad, cb = _CONV_PAD, _CONV_CB
    step = _pick(s, (512, 256, _CONV_ROWS))
    xp_ref[0:pad, :] = jnp.zeros((pad, cb), _F32)

    def fill(r, carry):
        base = pl.multiple_of(r * step, step)
        v = x_refs[0][pl.ds(base, step), :].astype(_F32)
        if glu:
            v = v * _sig(x_refs[1][pl.ds(base, step), :].astype(_F32))
        xp_ref[pl.ds(pad + base, step), :] = v
        return carry

    lax.fori_loop(0, s // step, fill, 0)


def _conv_fwd(xs, w, b, k_taps, *, glu=False, act=False, out_dtype=_F32, name):
    s, c = xs[0].shape
    kp = w.shape[0]
    pad, rows, cb = _CONV_PAD, _CONV_ROWS, _CONV_CB
    groups = _conv_taps(k_taps)
    n_in = len(xs)

    def body(*refs):
        x_refs = refs[:n_in]
        w_ref, b_ref, o_ref, xp_ref = refs[n_in:]
        _conv_fill(x_refs, xp_ref, s, glu)
        wv = w_ref[...]
        bias = jnp.broadcast_to(b_ref[...], (rows, cb))

        def chunk(r, carry):
            base = pl.multiple_of(r * rows, rows)
            acc = _conv_window(xp_ref[pl.ds(base, rows + pad), :], wv, groups, bias)
            o_ref[pl.ds(base, rows), :] = (_silu(acc) if act else acc).astype(o_ref.dtype)
            return carry

        lax.fori_loop(0, s // rows, chunk, 0)

    col = pl.BlockSpec((s, cb), lambda i: (0, i))
    return pl.pallas_call(
        body, name=name, grid=(c // cb,),
        in_specs=[col] * n_in + [pl.BlockSpec((kp, cb), lambda i: (0, i)), pl.BlockSpec((1, cb), lambda i: (0, i))],
        out_specs=col, out_shape=jax.ShapeDtypeStruct((s, c), out_dtype),
        scratch_shapes=[pltpu.VMEM((s + pad, cb), _F32)],
        compiler_params=_params(("parallel",)),
    )(*xs, w, b)


def _conv_bwd(xs, w, b, dy, k_taps, *, glu=False, act=False, name):
    s, c = xs[0].shape
    kp = w.shape[0]
    pad, rows, cb = _CONV_PAD, _CONV_ROWS, _CONV_CB
    groups = _conv_taps(k_taps)
    win_rows = rows + pad
    n_in = len(xs)

    def fold(v):
        acc = v[0:SUBLANES, :]
        for i in range(1, rows // SUBLANES):
            acc = acc + v[i * SUBLANES:(i + 1) * SUBLANES, :]
        return acc

    def body(*refs):
        x_refs = refs[:n_in]
        w_ref, b_ref, dy_ref = refs[n_in:n_in + 3]
        dx_refs = refs[n_in + 3:2 * n_in + 3]
        dw_ref, db_ref, xp_ref, dyp_ref, acc_ref, dbacc_ref = refs[2 * n_in + 3:]
        _conv_fill(x_refs, xp_ref, s, glu)
        dyp_ref[s:s + pad, :] = jnp.zeros((pad, cb), _F32)
        acc_ref[...] = jnp.zeros_like(acc_ref)
        dbacc_ref[...] = jnp.zeros_like(dbacc_ref)
        wv = w_ref[...]
        bias = jnp.broadcast_to(b_ref[...], (rows, cb))

        def through_act(r, carry):
            base = pl.multiple_of(r * rows, rows)
            d = dy_ref[pl.ds(base, rows), :].astype(_F32)
            if act:
                pre = _conv_window(xp_ref[pl.ds(base, win_rows), :], wv, groups, bias)
                sg = _sig(pre)
                d = d * (sg * (1.0 + pre * (1.0 - sg)))
            dyp_ref[pl.ds(base, rows), :] = d
            return carry

        lax.fori_loop(0, s // rows, through_act, 0)

        def chunk(r, carry):
            base = pl.multiple_of(r * rows, rows)
            xwin = xp_ref[pl.ds(base, win_rows), :]
            dwin = dyp_ref[pl.ds(base, win_rows), :]
            dyc = dwin[0:rows, :]
            dxacc = jnp.zeros((rows, cb), _F32)
            for rot, taps in groups.items():
                xr = xwin if rot == 0 else pltpu.roll(xwin, rot, 0)
                dr = dwin if rot == 0 else pltpu.roll(dwin, win_rows - rot, 0)
                for k, j in taps:
                    a8 = j - rot
                    dxacc = dxacc + dr[a8:a8 + rows, :] * wv[k:k + 1, :]
                    prod = dyc * xr[pad - a8:pad - a8 + rows, :]
                    acc_ref[k * SUBLANES:(k + 1) * SUBLANES, :] += fold(prod)
            dbacc_ref[...] += fold(dyc)
            if glu:
                av = x_refs[0][pl.ds(base, rows), :].astype(_F32)
                sg = _sig(x_refs[1][pl.ds(base, rows), :].astype(_F32))
                dx_refs[0][pl.ds(base, rows), :] = (dxacc * sg).astype(dx_refs[0].dtype)
                dx_refs[1][pl.ds(base, rows), :] = (dxacc * av * sg * (1.0 - sg)).astype(dx_refs[1].dtype)
            else:
                dx_refs[0][pl.ds(base, rows), :] = dxacc.astype(dx_refs[0].dtype)
            return carry

        lax.fori_loop(0, s // rows, chunk, 0)
        dw_ref[...] = jnp.zeros_like(dw_ref)
        for k in range(k_taps):
            dw_ref[k:k + 1, :] = jnp.sum(acc_ref[k * SUBLANES:(k + 1) * SUBLANES, :], axis=0, keepdims=True)
        db_ref[...] = jnp.sum(dbacc_ref[...], axis=0, keepdims=True)

    col = pl.BlockSpec((s, cb), lambda i: (0, i))
    wspec = pl.BlockSpec((kp, cb), lambda i: (0, i))
    bspec = pl.BlockSpec((1, cb), lambda i: (0, i))
    dx_dtype = xs[0].dtype
    res = pl.pallas_call(
        body, name=name, grid=(c // cb,),
        in_specs=[col] * n_in + [wspec, bspec, col], out_specs=[col] * n_in + [wspec, bspec],
        out_shape=[jax.ShapeDtypeStruct((s, c), dx_dtype)] * n_in
        + [jax.ShapeDtypeStruct((kp, c), _F32), jax.ShapeDtypeStruct((1, c), _F32)],
        scratch_shapes=[pltpu.VMEM((s + pad, cb), _F32), pltpu.VMEM((s + pad, cb), _F32),
                        pltpu.VMEM((kp * SUBLANES, cb), _F32), pltpu.VMEM((SUBLANES, cb), _F32)],
        compiler_params=_params(("parallel",)),
    )(*xs, w, b, dy)
    return list(res[:n_in]), res[n_in], res[n_in + 1]


def _tri_sum(v, lower):
    l = v.shape[0]
    r, c = lax.broadcasted_iota(jnp.int32, (l, l), 0), lax.broadcasted_iota(jnp.int32, (l, l), 1)
    tri = ((r >= c) if lower else (r <= c)).astype(jnp.bfloat16)
    hi = v.astype(jnp.bfloat16)
    r1 = v - hi.astype(_F32)
    mid = r1.astype(jnp.bfloat16)
    lo = (r1 - mid.astype(_F32)).astype(jnp.bfloat16)
    out = jnp.zeros_like(v)
    for part in (hi, mid, lo):
        out = out + lax.dot_general(tri, part, _DN["nn"], preferred_element_type=_F32)
    return out


@jax.custom_vjp
def _cumsum_rows(v):
    return _tri_sum(v, True)


_cumsum_rows.defvjp(lambda v: (_tri_sum(v, True), None), lambda _, ct: (_tri_sum(ct, False),))


def _ssd_chunk(xbc, dtraw, prev, bias, alog):
    l = xbc.shape[0]
    xs = xbc[:, :SSD_WIDTH]
    bm = xbc[:, SSD_WIDTH:SSD_WIDTH + 2 * SSD_STATE]
    cm = xbc[:, SSD_WIDTH + 2 * SSD_STATE:]
    v = dtraw + bias
    dt = jnp.maximum(v, 0.0) + jnp.log1p(jnp.exp(-jnp.abs(v)))
    a_neg = -jnp.exp(alog)
    acs = _cumsum_rows(dt * a_neg)
    acs_t = acs.T
    total = acs[l - 1:l, :]
    row = lax.broadcasted_iota(jnp.int32, (l, l), 0)
    colv = lax.broadcasted_iota(jnp.int32, (l, l), 1)
    causal = row >= colv
    lane_lo = lax.broadcasted_iota(jnp.int32, (l, LANES), 1) < HEAD_DIM
    row_lo = lax.broadcasted_iota(jnp.int32, (LANES, SSD_STATE), 0) < HEAD_DIM

    def pair_lanes(m, h0):
        return jnp.where(lane_lo, m[:, h0:h0 + 1], m[:, h0 + 1:h0 + 2])

    ys, news = [], []
    cb = {}
    for j in range(SSD_HEADS // 2):
        h0 = 2 * j
        grp = h0 // (SSD_HEADS // 2)
        bg = bm[:, grp * SSD_STATE:(grp + 1) * SSD_STATE]
        cg = cm[:, grp * SSD_STATE:(grp + 1) * SSD_STATE]
        if grp not in cb:
            cb[grp] = _dot1(cg, bg, "nt")
        xdt = xs[:, j * LANES:(j + 1) * LANES] * pair_lanes(dt, h0)
        y = jnp.zeros((l, LANES), _F32)
        for hh, mask in ((h0, lane_lo), (h0 + 1, jnp.logical_not(lane_lo))):
            seg = acs[:, hh:hh + 1] - acs_t[hh:hh + 1, :]
            dec = jnp.exp(jnp.where(causal, seg, -jnp.inf))
            y = y + _dot1(cb[grp] * dec, jnp.where(mask, xdt, 0.0), "nn")
        acs_p = pair_lanes(acs, h0)
        prev_p = prev[j * LANES:(j + 1) * LANES, :]
        y = y + _dot1(cg, prev_p, "nt") * jnp.exp(acs_p)
        total_p = jnp.where(lane_lo[0:1, :], total[:, h0:h0 + 1], total[:, h0 + 1:h0 + 2])
        wgt = jnp.exp(total_p - acs_p)
        st = _dot1(xdt * wgt, bg, "tn")
        cdec = jnp.exp(jnp.where(row_lo, total[:, h0:h0 + 1], total[:, h0 + 1:h0 + 2]))
        news.append(prev_p * cdec + st)
        ys.append(y)
    return jnp.concatenate(ys, axis=1), jnp.concatenate(news, axis=0)


def _ssd_gate_chunk(xbc, dtraw, prev, bias, alog, z, dexp, g):
    y, new = _ssd_chunk(xbc, dtraw, prev, bias, alog)
    return _f_gate(y, xbc[:, :SSD_WIDTH], z, dexp, g), new


def _ssd_fwd(xbc, dtraw, bias, alog, z, dexp, g, *, name):
    s = xbc.shape[0]
    nc = s // CHUNK
    nstate = SSD_HEADS * HEAD_DIM

    def body(x_ref, dt_ref, b_ref, a_ref, z_ref, d_ref, g_ref, y_ref, st_ref, state_ref):
        @pl.when(pl.program_id(0) == 0)
        def _():
            state_ref[...] = jnp.zeros_like(state_ref)

        prev = state_ref[...]
        st_ref[...] = prev
        y, new = _ssd_gate_chunk(x_ref[...].astype(_F32), dt_ref[...], prev, b_ref[...], a_ref[...], z_ref[...].astype(_F32),
                                 d_ref[...], g_ref[...])
        y_ref[...] = y.astype(y_ref.dtype)
        state_ref[...] = new

    small = pl.BlockSpec((1, LANES), lambda i: (0, 0))
    wide = pl.BlockSpec((1, SSD_WIDTH), lambda i: (0, 0))
    rows = pl.BlockSpec((CHUNK, SSD_WIDTH), lambda i: (i, 0))
    return pl.pallas_call(
        body, name=name, grid=(nc,),
        in_specs=[pl.BlockSpec((CHUNK, XBC_WIDTH), lambda i: (i, 0)), pl.BlockSpec((CHUNK, LANES), lambda i: (i, 0)),
                  small, small, rows, wide, wide],
        out_specs=[rows, pl.BlockSpec((None, nstate, SSD_STATE), lambda i: (i, 0, 0))],
        out_shape=[jax.ShapeDtypeStruct((s, SSD_WIDTH), _MXU), jax.ShapeDtypeStruct((nc, nstate, SSD_STATE), _F32)],
        scratch_shapes=[pltpu.VMEM((nstate, SSD_STATE), _F32)],
        compiler_params=_params(("arbitrary",)),
    )(xbc, dtraw, bias, alog, z, dexp, g)


def _ssd_bwd(xbc, dtraw, states, bias, alog, z, dexp, g, dy, *, name):
    s = xbc.shape[0]
    nc = s // CHUNK
    nstate = SSD_HEADS * HEAD_DIM

    def body(x_ref, dt_ref, st_ref, b_ref, a_ref, z_ref, d_ref, g_ref, dy_ref,
             dx_ref, ddt_ref, db_ref, da_ref, dz_ref, dd_ref, dg_ref, dstate_ref):
        @pl.when(pl.program_id(0) == 0)
        def _():
            dstate_ref[...] = jnp.zeros_like(dstate_ref)
            for acc in (db_ref, da_ref, dd_ref, dg_ref):
                acc[...] = jnp.zeros_like(acc)

        _, vjp = jax.vjp(_ssd_gate_chunk, x_ref[...].astype(_F32), dt_ref[...], st_ref[...], b_ref[...], a_ref[...],
                         z_ref[...].astype(_F32), d_ref[...], g_ref[...])
        dx, ddt, dprev, db, da, dz, dd, dg = vjp((dy_ref[...].astype(_F32), dstate_ref[...]))
        dx_ref[...] = dx.astype(dx_ref.dtype)
        ddt_ref[...] = ddt
        dz_ref[...] = dz.astype(dz_ref.dtype)
        db_ref[...] += db
        da_ref[...] += da
        dd_ref[...] += dd
        dg_ref[...] += dg
        dstate_ref[...] = dprev

    rev = lambda i: (nc - 1 - i, 0)
    small = pl.BlockSpec((1, LANES), lambda i: (0, 0))
    wide = pl.BlockSpec((1, SSD_WIDTH), lambda i: (0, 0))
    rows = pl.BlockSpec((CHUNK, SSD_WIDTH), rev)
    return pl.pallas_call(
        body, name=name, grid=(nc,),
        in_specs=[pl.BlockSpec((CHUNK, XBC_WIDTH), rev), pl.BlockSpec((CHUNK, LANES), rev),
                  pl.BlockSpec((None, nstate, SSD_STATE), lambda i: (nc - 1 - i, 0, 0)), small, small, rows, wide, wide,
                  rows],
        out_specs=[pl.BlockSpec((CHUNK, XBC_WIDTH), rev), pl.BlockSpec((CHUNK, LANES), rev), small, small, rows, wide, wide],
        out_shape=[jax.ShapeDtypeStruct((s, XBC_WIDTH), xbc.dtype), jax.ShapeDtypeStruct((s, LANES), _F32),
                   jax.ShapeDtypeStruct((1, LANES), _F32), jax.ShapeDtypeStruct((1, LANES), _F32),
                   jax.ShapeDtypeStruct((s, SSD_WIDTH), z.dtype), jax.ShapeDtypeStruct((1, SSD_WIDTH), _F32),
                   jax.ShapeDtypeStruct((1, SSD_WIDTH), _F32)],
        scratch_shapes=[pltpu.VMEM((nstate, SSD_STATE), _F32)],
        compiler_params=_params(("arbitrary",)),
    )(xbc, dtraw, states, bias, alog, z, dexp, g, dy)


def _pad_cols(a, width):
    return jnp.pad(a, ((0, 0), (0, width - a.shape[1])))


def _pad_rows(a, rows):
    return jnp.pad(a, ((0, rows - a.shape[0]), (0, 0)))


def _tie(a, token):
    return a + token[0:1, 0:1].astype(a.dtype)


def _local_step(x, mem, target, w, fetch, emit):
    bf = _MXU
    d = D_MODEL
    h = _row_fwd(_f_rms, [x], [w['norm_mix_g']], [(d, bf)], name="f_norm_mix")
    w_in = fetch('in', h)['w_in']
    z_end, xbc_end, dt_end = SSD_WIDTH, SSD_WIDTH + XBC_WIDTH, SSD_WIDTH + XBC_WIDTH + SSD_HEADS
    w_z, w_xbc = (w_in, 0, z_end), (w_in, z_end, xbc_end)
    w_dt = _pad_rows(w_in[xbc_end:dt_end], LANES)
    w_a, w_g = (w_in, dt_end, dt_end + CF_WIDTH), (w_in, dt_end + CF_WIDTH, w_in.shape[0])
    dt_bias = _pad_cols(w['ssd_dt_bias'], LANES)
    a_log = _pad_cols(w['ssd_A_log'], LANES)
    d_exp = jnp.repeat(w['ssd_D'], HEAD_DIM, axis=1)
    g_final = w['norm_final_g'].reshape(1, D_MODEL)

    z, xbc, dtr, ga, gg = _mm_fan_out(h, [w_z, w_xbc, w_dt, w_a, w_g], tb=True, out_dtypes=[bf, bf, _F32, bf, bf],
                                      name="f_in")
    wc = fetch('conv', xbc)
    ssd_w = _pad_rows(wc['ssd_conv_w'], SUBLANES)
    cf_w = _pad_rows(wc['cf_conv_w'], 32)
    xbc_a = _conv_fwd([xbc], ssd_w, w['ssd_conv_b'], SSD_CONV, act=True, out_dtype=bf, name="f_ssd_conv")
    y_n, states = _ssd_fwd(xbc_a, dtr, dt_bias, a_log, z, d_exp, w['ssd_norm_g'], name="f_ssd")
    u_c = _conv_fwd([ga, gg], cf_w, w['cf_conv_b'], CF_CONV, glu=True, out_dtype=bf, name="f_cf_conv")
    u = _row_fwd(_f_ln, [u_c], [w['cf_ln_g'], w['cf_ln_b']], [(d, bf)], name="f_cf_ln")
    wm = fetch('mid', y_n)
    w_out_y, w_out_u = (wm['w_out'], 0, SSD_WIDTH), (wm['w_out'], SSD_WIDTH, wm['w_out'].shape[0])
    x1, hq = _mm_fan_in([(y_n, w_out_y), (u, w_out_u)], add=x, out_dtype=bf, epilogue=(_f_rms, w['norm_xattn_g'], bf),
                        name="f_out")
    q = _mm(hq, wm['w_q'], out_dtype=bf, name="f_q")
    memn = _row_fwd(_f_rms, [mem], [w['norm_mem_g']], [(d, bf)], name="f_norm_mem")
    kv = _mm(memn, wm['w_kv'], name="f_kv")
    k_mat, v_mat = kv[:, :d], kv[:, d:]
    o = _row_fwd(_f_att, [q], [k_mat, v_mat], [(d, bf)], ts=512, name="f_att")
    x2, hf = _mm_fan_in([(o, wm['w_o'])], add=x1, out_dtype=bf, epilogue=(_f_rms, w['norm_ffn_g'], bf), name="f_o")
    wf = fetch('ffn', hf)
    gate, up, act = _mm_fan_out(hf, [wf['w_gate'], wf['w_up']], tb=True, out_dtypes=[bf, bf], epilogue=_f_swiglu,
                                extra_outs=[(D_FF, bf)], tm=256, name="f_ffn_in")
    x3 = _mm(act, wf['w_down'], add=x2, out_dtype=bf, name="f_down")

    dx3, dg_final, loss = _loss_bwd(x3, target, g_final, dx_dtype=bf, name="b_loss")
    g = {'norm_final_g': dg_final.reshape(d)}

    dact = _mm(dx3, wf['w_down'], tb=True, out_dtype=bf, name="b_down_x")
    dw_down = _mm(act, dx3, ta=True, out_dtype=bf, name="b_down_w")
    def swiglu_bwd(gate_t, up_t, dact_t):
        return jax.vjp(_f_swiglu, gate_t, up_t)[1](dact_t)

    dhf, dgate, dup = _mm_fan_in([(None, wf['w_gate']), (None, wf['w_up'])], prologue=swiglu_bwd, pro_ins=[gate, up, dact],
                                 pro_out_dtypes=[bf, bf], out_dtype=bf, tm=256, name="b_ffn_in_x")
    sent = emit({'w_down': dw_down, 'w_gate': _mm(dgate, hf, ta=True, out_dtype=bf, name="b_gate_w"),
                 'w_up': _mm(dup, hf, ta=True, out_dtype=bf, name="b_up_w")})
    (dx2,), (g['norm_ffn_g'],) = _row_bwd(_f_rms, [x2], [_tie(w['norm_ffn_g'], sent)], [dhf], adds={0: dx3}, row_dtypes=[bf], name="b_norm_ffn")

    do = _mm(dx2, wm['w_o'], tb=True, out_dtype=bf, name="b_o_x")
    dw_o = _mm(o, dx2, ta=True, out_dtype=bf, name="b_o_w")
    (dq,), (dk, dv) = _row_bwd(_f_att, [q], [k_mat, v_mat], [do], row_dtypes=[bf], ts=512, name="b_att")
    dw_q = _mm(hq, dq, ta=True, out_dtype=bf, name="b_q_w")
    dhq = _mm(dq, wm['w_q'], tb=True, out_dtype=bf, name="b_q_x")
    (dx1,), (g['norm_xattn_g'],) = _row_bwd(_f_rms, [x1], [w['norm_xattn_g']], [dhq], adds={0: dx2}, row_dtypes=[bf], name="b_norm_xattn")
    dkv = jnp.concatenate([dk, dv], axis=1)
    dmemn = _mm(dkv, wm['w_kv'], tb=True, name="b_kv_x")
    _, (g['norm_mem_g'],) = _row_bwd(_f_rms, [mem], [w['norm_mem_g']], [dmemn], need=[False], name="b_norm_mem")
    sent = emit({'w_o': dw_o, 'w_q': dw_q, 'w_kv': _mm_tn_blocked(memn, dkv, N_DEV, out_dtype=bf, name="b_kv_w")},
                after=g['norm_mem_g'])

    dyn, du = _mm_fan_out(dx1, [w_out_y, w_out_u], tb=True, out_dtypes=[bf, bf], name="b_out_x")
    (du_c,), (g['cf_ln_g'], g['cf_ln_b']) = _row_bwd(_f_ln, [u_c], [_tie(w['cf_ln_g'], sent), w['cf_ln_b']], [du], row_dtypes=[bf], name="b_cf_ln")
    sent = emit({'w_out': jnp.concatenate(_mm_tn_fan([y_n, u], dx1, out_dtype=bf, name="b_out_w"), axis=0)})
    (dga, dgg), dcf_w, g['cf_conv_b'] = _conv_bwd([ga, gg], cf_w, w['cf_conv_b'], du_c, CF_CONV, glu=True, name="b_cf_conv")
    g['cf_conv_w'] = dcf_w[:CF_CONV]
    dxbc_a, ddtr, ddt_bias, da_log, dz, dd_exp, g['ssd_norm_g'] = _ssd_bwd(
        xbc_a, dtr, states, dt_bias, a_log, z, d_exp, _tie(w['ssd_norm_g'], sent), dyn, name="b_ssd")
    g['ssd_D'] = jnp.sum(dd_exp.reshape(SSD_HEADS, HEAD_DIM), axis=1).reshape(1, SSD_HEADS)
    g['ssd_dt_bias'] = ddt_bias[:, :SSD_HEADS]
    g['ssd_A_log'] = da_log[:, :SSD_HEADS]
    (dxbc,), dssd_w, g['ssd_conv_b'] = _conv_bwd([xbc], ssd_w, w['ssd_conv_b'], dxbc_a, SSD_CONV, act=True, name="b_ssd_conv")
    g['ssd_conv_w'] = dssd_w[:SSD_CONV]

    dw_z, dw_dt, dw_a, dw_g = _mm_tn_fan([dz, ddtr, dga, dgg], h, out_dtype=bf, name="b_in_zdag_w")
    sent = emit({'w_in': jnp.concatenate([
        dw_z, _mm(dxbc, h, ta=True, out_dtype=bf, name="b_in_xbc_w"), dw_dt[:SSD_HEADS], dw_a, dw_g], axis=0)})
    dh = _mm_fan_in([(dz, w_z), (dxbc, w_xbc), (ddtr, _tie(w_dt, sent)), (dga, w_a), (dgg, w_g)], out_dtype=bf,
                    name="b_in_x")
    (dx,), (g['norm_mix_g'],) = _row_bwd(_f_rms, [x], [w['norm_mix_g']], [dh], adds={0: dx1}, name="b_norm_mix")
    return loss, dx, g


_ANY = pl.BlockSpec(memory_space=pl.ANY)


def _place():
    x, y, c = lax.axis_index("x"), lax.axis_index("y"), lax.axis_index("c")
    return x, y, c


_HBM =pl.BlockSpec(memory_space=pltpu.HBM)
_SEM = pl.BlockSpec(memory_space=pltpu.SEMAPHORE)
_EFFECT = pltpu.SideEffectType.DATAFLOW_SIDE_EFFECTING
_FLIPS = [(dx, dy, dc) for dx in (0, 1) for dy in (0, 1) for dc in (0, 1)][1:]


def _peer(flip, x, y, c):
    return (1 - x if flip[0] else x, 1 - y if flip[1] else y, 1 - c if flip[2] else c)


_CHIP_FLIPS = [f for f in _FLIPS if f[2] == 0]


def _send_start(srcs, blocked, *, flips=None, after=None, name):
    n = len(srcs)
    flips = _FLIPS if flips is None else flips
    nf = len(flips)
    lands = [jax.ShapeDtypeStruct(s.shape if blocked else (N_DEV,) + s.shape, s.dtype) for s in srcs]
    n_in = 2 * n + (after is not None)

    def body(*refs):
        src_refs, land_refs = refs[:n], refs[n:2 * n]
        send_sems, recv_sems = refs[n_in], refs[n_in + 1]
        token = refs[-1]
        x, y, c = _place()
        me = 4 * x + 2 * y + c
        for a in range(n):
            for k, flip in enumerate(flips):
                p = _peer(flip, x, y, c)
                src = src_refs[a].at[4 * p[0] + 2 * p[1] + p[2]] if blocked else src_refs[a]
                pltpu.make_async_remote_copy(
                    src_ref=src, dst_ref=land_refs[a].at[me], send_sem=send_sems.at[nf * a + k],
                    recv_sem=recv_sems.at[nf * a + k], device_id=p, device_id_type=MESH).start()
        token[...] = jnp.zeros_like(token)

    res = pl.pallas_call(
        body, name=name,
        out_shape=(pltpu.SemaphoreType.DMA((nf * n,)), pltpu.SemaphoreType.DMA((nf * n,)),
                   *[pltpu.HBM(s.shape, s.dtype) for s in srcs], *[pltpu.HBM(l.shape, l.dtype) for l in lands],
                   jax.ShapeDtypeStruct((SUBLANES, LANES), _F32)),
        in_specs=[_HBM] * (2 * n) + [_ANY] * (after is not None),
        out_specs=(_SEM, _SEM, *[_HBM] * (2 * n), pl.BlockSpec(memory_space=pltpu.VMEM)),
        input_output_aliases={i: 2 + i for i in range(2 * n)},
        compiler_params=pltpu.CompilerParams(has_side_effects=_EFFECT),
    )(*[pltpu.with_memory_space_constraint(s, pltpu.HBM) for s in srcs],
      *[pltpu.with_memory_space_constraint(lax.empty(l.shape, l.dtype), pltpu.HBM) for l in lands],
      *([after] if after is not None else []))
    return res[0], res[1], list(res[2:2 + n]), list(res[2 + n:2 + 2 * n]), res[-1]


def _send_wait(handles, after, blocked, *, flips=None, name):
    send_sems, recv_sems, srcs, lands, _ = handles
    n = len(srcs)
    flips = _FLIPS if flips is None else flips
    nf = len(flips)

    def body(*refs):
        src_refs, land_refs = refs[:n], refs[n:2 * n]
        send_sems, recv_sems = refs[2 * n], refs[2 * n + 1]
        x, y, c = _place()
        for a in range(n):
            for k, flip in enumerate(flips):
                p = _peer(flip, x, y, c)
                pid = 4 * p[0] + 2 * p[1] + p[2]
                cp = pltpu.make_async_remote_copy(
                    src_ref=src_refs[a].at[pid] if blocked else src_refs[a], dst_ref=land_refs[a].at[pid],
                    send_sem=send_sems.at[nf * a + k], recv_sem=recv_sems.at[nf * a + k], device_id=p, device_id_type=MESH)
                cp.wait_send()
                cp.wait_recv()

    res = pl.pallas_call(
        body, name=name,
        out_shape=tuple(pltpu.HBM(s.shape, s.dtype) for s in srcs + lands),
        in_specs=[_HBM] * (2 * n) + [_SEM, _SEM, _ANY], out_specs=tuple([_HBM] * (2 * n)),
        input_output_aliases={i: i for i in range(2 * n)},
        compiler_params=pltpu.CompilerParams(has_side_effects=_EFFECT),
    )(*srcs, *lands, send_sems, recv_sems, after)
    return list(res[:n]), list(res[n:])


def _sibling_share(lands, owns, *, name):
    n = len(lands)

    def body(*refs):
        own_refs, land_refs = refs[n:2 * n], refs[2 * n:3 * n]
        send_sems, recv_sems = refs[3 * n], refs[3 * n + 1]
        x, y, c = _place()
        sibling = (x, y, 1 - c)
        chips = [(1 - x, y), (x, 1 - y), (1 - x, 1 - y)]

        def copy(a, k, block, src=None):
            slot = land_refs[a].at[block]
            return pltpu.make_async_remote_copy(
                src_ref=slot if src is None else src, dst_ref=slot, send_sem=send_sems.at[4 * a + k],
                recv_sem=recv_sems.at[4 * a + k], device_id=sibling, device_id_type=MESH)

        sends = []
        for a in range(n):
            sends.append(copy(a, 0, 4 * x + 2 * y + c, src=own_refs[a]))
            sends += [copy(a, 1 + j, 4 * cx + 2 * cy + c) for j, (cx, cy) in enumerate(chips)]
        for cp in sends:
            cp.start()
        for a in range(n):
            copy(a, 0, 4 * x + 2 * y + (1 - c)).wait_recv()
            for j, (cx, cy) in enumerate(chips):
                copy(a, 1 + j, 4 * cx + 2 * cy + (1 - c)).wait_recv()
        for cp in sends:
            cp.wait_send()

    return pl.pallas_call(
        body, name=name, in_specs=[_ANY] * (2 * n), out_specs=[_ANY] * n,
        out_shape=[jax.ShapeDtypeStruct(l.shape, l.dtype) for l in lands],
        input_output_aliases={i: i for i in range(n)},
        scratch_shapes=[pltpu.SemaphoreType.DMA((4 * n,)), pltpu.SemaphoreType.DMA((4 * n,))],
    )(*lands, *owns)


def _adamw(parts, w, m, v, *, own=None, me=None, name):
    p, r, c = parts.shape
    tr = _pick(r, (256, 176, 128, 64, 32, 16, 8))
    if own is not None and r % 16 == 0:
        tr = next((t for t in (256, 128, 64, 32, 16) if r % t == 0 and r // t >= 4), tr)
    if own is not None:
        tc = c if tr < r else _pick(c, (256, 128))
        return _adamw_own(parts, own, me, w, m, v, tr, tc, name=name)

    def body(p_ref, w_ref, m_ref, v_ref, g_ref, d_ref, nm_ref, nv_ref):
        g = p_ref[0].astype(_F32)
        for i in range(1, p):
            g = g + p_ref[i].astype(_F32)
        _adamw_math(g, w_ref, m_ref, v_ref, g_ref, d_ref, nm_ref, nv_ref)

    blk = pl.BlockSpec((tr, c), lambda i: (i, 0))
    return pl.pallas_call(
        body, name=name, grid=(r // tr,),
        in_specs=[pl.BlockSpec((p, tr, c), lambda i: (0, i, 0)), blk, blk, blk], out_specs=[blk] * 4,
        out_shape=[jax.ShapeDtypeStruct((r, c), _F32)] * 4,
        compiler_params=_params(("parallel",)),
    )(parts, w, m, v)


def _adamw_math(g, w_ref, m_ref, v_ref, g_ref, d_ref, nm_ref, nv_ref):
    wv = w_ref[...]
    mn = ADAM_B1 * m_ref[...] + (1.0 - ADAM_B1) * g
    vn = ADAM_B2 * v_ref[...] + (1.0 - ADAM_B2) * jnp.square(g)
    m_hat = mn / (1.0 - ADAM_B1 ** ADAM_STEP)
    v_hat = vn / (1.0 - ADAM_B2 ** ADAM_STEP)
    g_ref[...] = g
    d_ref[...] = -ADAM_LR * (m_hat / (jnp.sqrt(v_hat) + ADAM_EPS) + ADAM_WD * wv)
    nm_ref[...] = mn
    nv_ref[...] = vn


def _adamw_own(parts, own, me, w, m, v, tr, tc, *, name):
    p, r, c = parts.shape

    def body(me_ref, p_ref, own_ref, w_ref, m_ref, v_ref, g_ref, d_ref, nm_ref, nv_ref):
        mine = own_ref[...].astype(_F32)
        g = jnp.where(me_ref[0] == 0, mine, p_ref[0].astype(_F32))
        for i in range(1, p):
            g = g + jnp.where(me_ref[0] == i, mine, p_ref[i].astype(_F32))
        _adamw_math(g, w_ref, m_ref, v_ref, g_ref, d_ref, nm_ref, nv_ref)

    blk = pl.BlockSpec((tr, tc), lambda i, j, me_ref: (i, j))
    grid_spec = pltpu.PrefetchScalarGridSpec(
        num_scalar_prefetch=1, grid=(r // tr, c // tc),
        in_specs=[pl.BlockSpec((p, tr, tc), lambda i, j, me_ref: (0, i, j)),
                  pl.BlockSpec((None, tr, tc), lambda i, j, me_ref: (me_ref[0], i, j)), blk, blk, blk],
        out_specs=[blk] * 4)
    return pl.pallas_call(
        body, name=name, grid_spec=grid_spec, out_shape=[jax.ShapeDtypeStruct((r, c), _F32)] * 4,
        compiler_params=_params(("parallel", "parallel")),
    )(me.reshape(1).astype(jnp.int32), parts, own, w, m, v)


def _adamw_rows(g_row, offsets, ws, ms, vs, *, name):
    k = len(ws)

    def body(*refs):
        g_ref, w_refs, m_refs, v_refs = refs[0], refs[1:1 + k], refs[1 + k:1 + 2 * k], refs[1 + 2 * k:1 + 3 * k]
        outs = refs[1 + 3 * k:]
        for i in range(k):
            gi = g_ref[:, offsets[i]:offsets[i] + ws[i].shape[1]]
            _adamw_math(gi, w_refs[i], m_refs[i], v_refs[i], *outs[4 * i:4 * i + 4])

    return pl.pallas_call(
        body, name=name, out_shape=[jax.ShapeDtypeStruct(w.shape, _F32) for w in ws for _ in range(4)],
    )(g_row, *ws, *ms, *vs)


def _sum_parts(parts, *, name):
    p, r, c = parts.shape

    def body(p_ref, o_ref):
        g = p_ref[0].astype(_F32)
        for i in range(1, p):
            g = g + p_ref[i].astype(_F32)
        o_ref[...] = g

    return pl.pallas_call(body, name=name, out_shape=jax.ShapeDtypeStruct((r, c), _F32))(parts)


def kernel(x, mem, norm_mix_g, w_in, ssd_conv_w, ssd_conv_b, ssd_dt_bias, ssd_A_log, ssd_D, ssd_norm_g, cf_conv_w, cf_conv_b, cf_ln_g, cf_ln_b, w_out, norm_xattn_g, norm_mem_g, w_q, w_kv, w_o, norm_ffn_g, w_gate, w_up, w_down, norm_final_g, loss_target, m_norm_mix_g, m_w_in, m_ssd_conv_w, m_ssd_conv_b, m_ssd_dt_bias, m_ssd_A_log, m_ssd_D, m_ssd_norm_g, m_cf_conv_w, m_cf_conv_b, m_cf_ln_g, m_cf_ln_b, m_w_out, m_norm_xattn_g, m_norm_mem_g, m_w_q, m_w_kv, m_w_o, m_norm_ffn_g, m_w_gate, m_w_up, m_w_down, m_norm_final_g, v_norm_mix_g, v_w_in, v_ssd_conv_w, v_ssd_conv_b, v_ssd_dt_bias, v_ssd_A_log, v_ssd_D, v_ssd_norm_g, v_cf_conv_w, v_cf_conv_b, v_cf_ln_g, v_cf_ln_b, v_w_out, v_norm_xattn_g, v_norm_mem_g, v_w_q, v_w_kv, v_w_o, v_norm_ffn_g, v_w_gate, v_w_up, v_w_down, v_norm_final_g):
    args = dict(locals())
    wts = {n: args[n] for n in WEIGHT_NAMES}
    mom = {n: args["m_" + n] for n in WEIGHT_NAMES}
    var = {n: args["v_" + n] for n in WEIGHT_NAMES}
    me = 4 * lax.axis_index("x") + 2 * lax.axis_index("y") + lax.axis_index("c")

    groups = {'in': ['w_in'], 'conv': ['ssd_conv_w', 'cf_conv_w'], 'mid': ['w_out', 'w_q', 'w_kv', 'w_o'],
              'ffn': ['w_gate', 'w_up', 'w_down']}
    def shard(n, a):
        return jnp.transpose(a[0], (1, 0)) if n in TRANSPOSED else a[0]

    two_level = {'in': _CHIP_FLIPS}
    gathers, started = {}, None
    for grp, names in groups.items():
        shards = [wts[n][0] if grp == 'conv' else shard(n, wts[n]).astype(_MXU) for n in names]
        gathers[grp] = _send_start(shards, False, flips=two_level.get(grp), after=started, name="gather_%s_start" % grp)
        started = gathers[grp][4]

    def fetch(grp, after):
        srcs, lands = _send_wait(gathers[grp], started if after is None else after, False, flips=two_level.get(grp),
                                 name="gather_%s_wait" % grp)
        if grp in two_level:
            lands = _sibling_share(lands, srcs, name="gather_%s_share" % grp)
        out = {}
        for n, own, gth in zip(groups[grp], srcs, lands):
            gth = lax.dynamic_update_slice_in_dim(gth, own[None], me, axis=0)
            if n == 'w_kv' or grp == 'conv':
                out[n] = jnp.transpose(gth, (1, 0, 2)).reshape(gth.shape[1], N_DEV * gth.shape[2])
            else:
                out[n] = gth.reshape(N_DEV * gth.shape[1], gth.shape[2])
        return out

    exchanges = []

    def emit(grads, after=None):
        blocks = []
        for n, gw in grads.items():
            if gw.ndim == 2:
                gw = gw.reshape(N_DEV, gw.shape[0] // N_DEV, gw.shape[1])
            blocks.append(gw.astype(jnp.bfloat16))
        first = next(iter(grads))
        exchanges.append((list(grads), _send_start(blocks, True, after=after, name="exchange_%s_start" % first), first))
        return exchanges[-1][1][4]

    full = {n: wts[n] for n in WEIGHT_NAMES if n not in BIG and n not in groups['conv']}
    full['norm_mix_g'] = _tie(norm_mix_g, started)

    loss_blk, grad_x, g = _local_step(x[0], mem[0], loss_target[0], full, fetch, emit)

    small = [n for n in WEIGHT_NAMES if n not in BIG]
    g['loss'] = loss_blk[0:1, 0:1]
    items = small + ['loss']
    size = {n: math.prod(g[n].shape) for n in items}
    seg = {n: -(-size[n] // LANES) * LANES for n in items}
    off, pos = {}, 0
    for n in items:
        off[n], pos = pos, pos + seg[n]
    rows = -(-pos // (LANES * SUBLANES)) * SUBLANES
    flat = jnp.concatenate([jnp.pad(g[n].reshape(-1), (0, seg[n] - size[n])) for n in items]
                           + [jnp.zeros((rows * LANES - pos,), _F32)])
    small_sent = _send_start([flat.reshape(rows, LANES)], False, name="gather_small_start")

    out_g, out_d, out_m, out_v = {}, {}, {}, {}
    done = small_sent[4]
    for names, handles, first in exchanges:
        srcs, lands = _send_wait(handles, done, True, name="exchange_%s_wait" % first)
        for n, own, parts in zip(names, srcs, lands):
            res = _adamw(parts, shard(n, wts[n]), shard(n, mom[n]), shard(n, var[n]), own=own, me=me, name="adamw_" + n)
            out_g[n], out_d[n], out_m[n], out_v[n] = [(jnp.transpose(r, (1, 0)) if n in TRANSPOSED else r)[None] for r in res]
            done = res[0]

    srcs, lands = _send_wait(small_sent, out_g[exchanges[-1][0][-1]], False, name="gather_small_wait")
    small_parts = lax.dynamic_update_slice_in_dim(lands[0], srcs[0][None], me, axis=0)
    g_row = _sum_parts(small_parts, name="sum_small_grads").reshape(1, rows * LANES)
    loss = g_row[0, off['loss']]
    rep = [n for n in small if n not in groups['conv']]
    as_row = lambda a: a.reshape(1, -1)
    res = _adamw_rows(g_row, [off[n] for n in rep], [as_row(wts[n]) for n in rep], [as_row(mom[n]) for n in rep],
                      [as_row(var[n]) for n in rep], name="adamw_small")
    for i, n in enumerate(rep):
        out_g[n], out_d[n], out_m[n], out_v[n] = [r.reshape(wts[n].shape) for r in res[4 * i:4 * i + 4]]
    for n in groups['conv']:
        k_taps, width = g[n].shape
        g_full = g_row[0, off[n]:off[n] + size[n]].reshape(k_taps, width)
        g_mine = lax.dynamic_slice_in_dim(g_full, me * (width // N_DEV), width // N_DEV, axis=1)
        res = _adamw(g_mine[None], wts[n][0], mom[n][0], var[n][0], name="adamw_" + n)
        out_g[n], out_d[n], out_m[n], out_v[n] = [r[None] for r in res]

    return (loss, grad_x[None], *[out_g[n] for n in WEIGHT_NAMES], *[out_d[n] for n in WEIGHT_NAMES],
            *[out_m[n] for n in WEIGHT_NAMES], *[out_v[n] for n in WEIGHT_NAMES])
```
